```python
import math
import jax, jax.numpy as jnp
from jax import lax
import numpy as np

D_MODEL = 1024
BATCH = 8
SEQ = 4096
DEPTH = 1

CHUNK = 128
SGU_WIDTH = D_MODEL
SGU_GROUPS = 4
SGU_GROUP_DIM = SGU_WIDTH // SGU_GROUPS
RET_HEADS = 4
RET_QK_DIM = 256
RET_V_DIM = 256
RET_WIDTH = RET_HEADS * RET_V_DIM
D_FF = 2816
ROPE_BASE = 10000.0
NORM_EPS = 1e-6
IN_WIDTHS = (SGU_WIDTH, SGU_WIDTH, RET_HEADS * RET_QK_DIM, RET_HEADS * RET_QK_DIM,
             RET_WIDTH, RET_WIDTH, D_MODEL, D_MODEL)
IN_WIDTH = sum(IN_WIDTHS)

kernel_name = "hybrid_sgu_retention_macaron_block"


def rms_norm(x, g):
    xf = x.astype(jnp.float32)
    y = xf * lax.rsqrt(jnp.mean(xf * xf, axis=-1, keepdims=True) + NORM_EPS)
    return (y * g.astype(jnp.float32)).astype(x.dtype)


def swiglu_ffn(h, w_gate, w_up, w_down):
    return (jax.nn.silu(h @ w_gate) * (h @ w_up)) @ w_down


def rotary(t):
    S, D = t.shape[1], t.shape[3]
    theta = ROPE_BASE ** (-jnp.arange(0, D, 2, dtype=jnp.float32) / D)
    ang = jnp.arange(S, dtype=jnp.float32)[:, None] * theta[None, :]
    cos = jnp.cos(ang)[None, :, None, :]
    sin = jnp.sin(ang)[None, :, None, :]
    t1, t2 = jnp.split(t.astype(jnp.float32), 2, axis=-1)
    return jnp.concatenate([t1 * cos - t2 * sin, t2 * cos + t1 * sin], axis=-1)


def spatial_gating(u, v, norm_g, norm_b, w_s, b_s):
    B, S, _ = v.shape
    vf = v.astype(jnp.float32)
    mu = jnp.mean(vf, axis=-1, keepdims=True)
    var = jnp.mean(jnp.square(vf - mu), axis=-1, keepdims=True)
    vn = (vf - mu) * lax.rsqrt(var + NORM_EPS) * norm_g + norm_b
    vc = vn.reshape(B, S // CHUNK, CHUNK, SGU_GROUPS, SGU_GROUP_DIM)
    s = jnp.einsum('gcm,bnmgd->bncgd', w_s.astype(jnp.float32), vc)
    s = s + b_s.astype(jnp.float32).T[None, None, :, :, None]
    return u * s.reshape(B, S, SGU_WIDTH).astype(u.dtype)


def retention_direction(q, k, v, log_gamma, include_diag):
    C = q.shape[3]
    idx = jnp.arange(C, dtype=jnp.float32)
    diff = idx[:, None] - idx[None, :]
    keep = (diff >= 0) if include_diag else (diff > 0)
    lg = log_gamma[:, None, None]
    decay = jnp.where(keep[None], jnp.exp(jnp.maximum(diff, 0.0)[None] * lg), 0.0)
    scores = jnp.einsum('bhncd,bhnmd->bhncm', q, k) * decay[None, :, None]
    intra = jnp.einsum('bhncm,bhnme->bhnce', scores, v)
    q_dec = q * jnp.exp((idx + 1.0)[None, :] * log_gamma[:, None])[None, :, None, :, None]
    k_dec = k * jnp.exp((C - 1.0 - idx)[None, :] * log_gamma[:, None])[None, :, None, :, None]
    chunk_decay = jnp.exp(C * log_gamma)[None, :, None, None]

    def step(state, xs):
        qn, kn, vn = xs
        out = jnp.einsum('bhcd,bhde->bhce', qn, state)
        state = state * chunk_decay + jnp.einsum('bhcd,bhce->bhde', kn, vn)
        return state, out

    B, H = q.shape[0], q.shape[1]
    init = jnp.zeros((B, H, q.shape[-1], v.shape[-1]), jnp.float32)
    xs = (jnp.moveaxis(q_dec, 2, 0), jnp.moveaxis(k_dec, 2, 0), jnp.moveaxis(v, 2, 0))
    _, cross = lax.scan(step, init, xs)
    return intra + jnp.moveaxis(cross, 0, 2)


def bidirectional_retention(q, k, v, decay_logit):
    B, S, H, dv = v.shape
    N = S // CHUNK
    log_gamma = jax.nn.log_sigmoid(decay_logit.astype(jnp.float32))

    def chunk(t):
        return jnp.transpose(t.astype(jnp.float32), (0, 2, 1, 3)).reshape(B, H, N, CHUNK, t.shape[-1])

    def rev(t):
        return jnp.flip(t, axis=1)

    fwd = retention_direction(chunk(q), chunk(k), chunk(v), log_gamma[0], True)
    bwd = retention_direction(chunk(rev(q)), chunk(rev(k)), chunk(rev(v)), log_gamma[1], False)
    out = fwd.reshape(B, H, S, dv) + jnp.flip(bwd.reshape(B, H, S, dv), axis=2)
    return jnp.transpose(out, (0, 2, 1, 3))


def _fwd_setup_inputs(seed: int = 0) -> dict:
    key = jax.random.key(seed)
    ks = jax.random.split(key, 24)
    L, D = DEPTH, D_MODEL

    def nrm(k, shape, scale):
        return jax.random.normal(k, shape, jnp.float32) * scale

    base_logit = jnp.log(2.0 ** (5.0 + jnp.arange(RET_HEADS, dtype=jnp.float32)) - 1.0)
    return {
        "x": nrm(ks[0], (BATCH, SEQ, D), 1.0),
        "ffn1_norm": 1.0 + nrm(ks[1], (L, D), 0.02),
        "ffn1_w_gate": nrm(ks[2], (L, D, D_FF), D ** -0.5),
        "ffn1_w_up": nrm(ks[3], (L, D, D_FF), D ** -0.5),
        "ffn1_w_down": nrm(ks[4], (L, D_FF, D), D_FF ** -0.5),
        "mix_norm": 1.0 + nrm(ks[5], (L, D), 0.02),
        "w_in": nrm(ks[6], (L, D, IN_WIDTH), D ** -0.5),
        "b_in": nrm(ks[7], (L, IN_WIDTH), 0.02),
        "sgu_norm_g": 1.0 + nrm(ks[8], (L, SGU_WIDTH), 0.02),
        "sgu_norm_b": nrm(ks[9], (L, SGU_WIDTH), 0.02),
        "sgu_w_s": nrm(ks[10], (L, SGU_GROUPS, CHUNK, CHUNK), CHUNK ** -0.5),
        "sgu_b_s": 1.0 + nrm(ks[11], (L, SGU_GROUPS, CHUNK), 0.1),
        "ret_decay_logit": jnp.broadcast_to(base_logit, (L, 2, RET_HEADS)) + nrm(ks[12], (L, 2, RET_HEADS), 0.05),
        "w_branch_a": nrm(ks[13], (L, SGU_WIDTH, D), SGU_WIDTH ** -0.5),
        "w_branch_b": nrm(ks[14], (L, RET_WIDTH, D), RET_WIDTH ** -0.5),
        "w_out": nrm(ks[15], (L, D, D), D ** -0.5),
        "ffn2_norm": 1.0 + nrm(ks[16], (L, D), 0.02),
        "ffn2_w_gate": nrm(ks[17], (L, D, D_FF), D ** -0.5),
        "ffn2_w_up": nrm(ks[18], (L, D, D_FF), D ** -0.5),
        "ffn2_w_down": nrm(ks[19], (L, D_FF, D), D_FF ** -0.5),
        "final_norm": 1.0 + nrm(ks[20], (D,), 0.02),
    }


def _fwd_reference(x, ffn1_norm, ffn1_w_gate, ffn1_w_up, ffn1_w_down, mix_norm, w_in, b_in,
              sgu_norm_g, sgu_norm_b, sgu_w_s, sgu_b_s, ret_decay_logit,
              w_branch_a, w_branch_b, w_out, ffn2_norm, ffn2_w_gate, ffn2_w_up, ffn2_w_down,
              final_norm):
    B, S, _ = x.shape
    split_at = list(np.cumsum(IN_WIDTHS)[:-1])
    for l in range(DEPTH):
        x = x + 0.5 * swiglu_ffn(rms_norm(x, ffn1_norm[l]), ffn1_w_gate[l], ffn1_w_up[l], ffn1_w_down[l])

        h = rms_norm(x, mix_norm[l])
        proj = h @ w_in[l] + b_in[l]
        u_a, v_a, q_r, k_r, v_r, g_r, gate_a, gate_b = jnp.split(proj, split_at, axis=-1)

        a = spatial_gating(jax.nn.gelu(u_a, approximate=False), jax.nn.gelu(v_a, approximate=False),
                           sgu_norm_g[l], sgu_norm_b[l], sgu_w_s[l], sgu_b_s[l])

        q = rotary(q_r.reshape(B, S, RET_HEADS, RET_QK_DIM))
        k = rotary(k_r.reshape(B, S, RET_HEADS, RET_QK_DIM)) * (RET_QK_DIM ** -0.5)
        v = v_r.reshape(B, S, RET_HEADS, RET_V_DIM)
        r = bidirectional_retention(q, k, v, ret_decay_logit[l])
        r = r * lax.rsqrt(jnp.mean(r * r, axis=-1, keepdims=True) + NORM_EPS)
        r = r.reshape(B, S, RET_WIDTH).astype(x.dtype) * jax.nn.silu(g_r)

        mix = jax.nn.sigmoid(gate_a) * (a @ w_branch_a[l]) + jax.nn.sigmoid(gate_b) * (r @ w_branch_b[l])
        x = x + mix @ w_out[l]

        x = x + 0.5 * swiglu_ffn(rms_norm(x, ffn2_norm[l]), ffn2_w_gate[l], ffn2_w_up[l], ffn2_w_down[l])
    return rms_norm(x, final_norm)


import jax as _jax
import jax.numpy as _jnp

TWIN_FORMAT = 'train_step'
FWD_PARAMS = ['x', 'ffn1_norm', 'ffn1_w_gate', 'ffn1_w_up', 'ffn1_w_down', 'mix_norm', 'w_in', 'b_in', 'sgu_norm_g', 'sgu_norm_b', 'sgu_w_s', 'sgu_b_s', 'ret_decay_logit', 'w_branch_a', 'w_branch_b', 'w_out', 'ffn2_norm', 'ffn2_w_gate', 'ffn2_w_up', 'ffn2_w_down', 'final_norm']
TWIN_WEIGHTS = ['ffn1_norm', 'ffn1_w_gate', 'ffn1_w_up', 'ffn1_w_down', 'mix_norm', 'w_in', 'b_in', 'sgu_norm_g', 'sgu_norm_b', 'sgu_w_s', 'sgu_b_s', 'ret_decay_logit', 'w_branch_a', 'w_branch_b', 'w_out', 'ffn2_norm', 'ffn2_w_gate', 'ffn2_w_up', 'ffn2_w_down', 'final_norm']
TWIN_DIFF_INPUT = 'x'
TWIN_INPUTS = ['x', 'ffn1_norm', 'ffn1_w_gate', 'ffn1_w_up', 'ffn1_w_down', 'mix_norm', 'w_in', 'b_in', 'sgu_norm_g', 'sgu_norm_b', 'sgu_w_s', 'sgu_b_s', 'ret_decay_logit', 'w_branch_a', 'w_branch_b', 'w_out', 'ffn2_norm', 'ffn2_w_gate', 'ffn2_w_up', 'ffn2_w_down', 'final_norm', 'loss_target', 'm_ffn1_norm', 'm_ffn1_w_gate', 'm_ffn1_w_up', 'm_ffn1_w_down', 'm_mix_norm', 'm_w_in', 'm_b_in', 'm_sgu_norm_g', 'm_sgu_norm_b', 'm_sgu_w_s', 'm_sgu_b_s', 'm_ret_decay_logit', 'm_w_branch_a', 'm_w_branch_b', 'm_w_out', 'm_ffn2_norm', 'm_ffn2_w_gate', 'm_ffn2_w_up', 'm_ffn2_w_down', 'm_final_norm', 'v_ffn1_norm', 'v_ffn1_w_gate', 'v_ffn1_w_up', 'v_ffn1_w_down', 'v_mix_norm', 'v_w_in', 'v_b_in', 'v_sgu_norm_g', 'v_sgu_norm_b', 'v_sgu_w_s', 'v_sgu_b_s', 'v_ret_decay_logit', 'v_w_branch_a', 'v_w_branch_b', 'v_w_out', 'v_ffn2_norm', 'v_ffn2_w_gate', 'v_ffn2_w_up', 'v_ffn2_w_down', 'v_final_norm']
TWIN_OUTPUTS = ['loss', 'grad_x', 'grad_ffn1_norm', 'grad_ffn1_w_gate', 'grad_ffn1_w_up', 'grad_ffn1_w_down', 'grad_mix_norm', 'grad_w_in', 'grad_b_in', 'grad_sgu_norm_g', 'grad_sgu_norm_b', 'grad_sgu_w_s', 'grad_sgu_b_s', 'grad_ret_decay_logit', 'grad_w_branch_a', 'grad_w_branch_b', 'grad_w_out', 'grad_ffn2_norm', 'grad_ffn2_w_gate', 'grad_ffn2_w_up', 'grad_ffn2_w_down', 'grad_final_norm', 'delta_ffn1_norm', 'delta_ffn1_w_gate', 'delta_ffn1_w_up', 'delta_ffn1_w_down', 'delta_mix_norm', 'delta_w_in', 'delta_b_in', 'delta_sgu_norm_g', 'delta_sgu_norm_b', 'delta_sgu_w_s', 'delta_sgu_b_s', 'delta_ret_decay_logit', 'delta_w_branch_a', 'delta_w_branch_b', 'delta_w_out', 'delta_ffn2_norm', 'delta_ffn2_w_gate', 'delta_ffn2_w_up', 'delta_ffn2_w_down', 'delta_final_norm', 'new_m_ffn1_norm', 'new_m_ffn1_w_gate', 'new_m_ffn1_w_up', 'new_m_ffn1_w_down', 'new_m_mix_norm', 'new_m_w_in', 'new_m_b_in', 'new_m_sgu_norm_g', 'new_m_sgu_norm_b', 'new_m_sgu_w_s', 'new_m_sgu_b_s', 'new_m_ret_decay_logit', 'new_m_w_branch_a', 'new_m_w_branch_b', 'new_m_w_out', 'new_m_ffn2_norm', 'new_m_ffn2_w_gate', 'new_m_ffn2_w_up', 'new_m_ffn2_w_down', 'new_m_final_norm', 'new_v_ffn1_norm', 'new_v_ffn1_w_gate', 'new_v_ffn1_w_up', 'new_v_ffn1_w_down', 'new_v_mix_norm', 'new_v_w_in', 'new_v_b_in', 'new_v_sgu_norm_g', 'new_v_sgu_norm_b', 'new_v_sgu_w_s', 'new_v_sgu_b_s', 'new_v_ret_decay_logit', 'new_v_w_branch_a', 'new_v_w_branch_b', 'new_v_w_out', 'new_v_ffn2_norm', 'new_v_ffn2_w_gate', 'new_v_ffn2_w_up', 'new_v_ffn2_w_down', 'new_v_final_norm']
TWIN_LEAF_KINDS = {'loss': 'loss', 'grad_x': 'grad_x', 'grad_ffn1_norm': 'grad_w', 'grad_ffn1_w_gate': 'grad_w', 'grad_ffn1_w_up': 'grad_w', 'grad_ffn1_w_down': 'grad_w', 'grad_mix_norm': 'grad_w', 'grad_w_in': 'grad_w', 'grad_b_in': 'grad_w', 'grad_sgu_norm_g': 'grad_w', 'grad_sgu_norm_b': 'grad_w', 'grad_sgu_w_s': 'grad_w', 'grad_sgu_b_s': 'grad_w', 'grad_ret_decay_logit': 'grad_w', 'grad_w_branch_a': 'grad_w', 'grad_w_branch_b': 'grad_w', 'grad_w_out': 'grad_w', 'grad_ffn2_norm': 'grad_w', 'grad_ffn2_w_gate': 'grad_w', 'grad_ffn2_w_up': 'grad_w', 'grad_ffn2_w_down': 'grad_w', 'grad_final_norm': 'grad_w', 'delta_ffn1_norm': 'delta_w', 'delta_ffn1_w_gate': 'delta_w', 'delta_ffn1_w_up': 'delta_w', 'delta_ffn1_w_down': 'delta_w', 'delta_mix_norm': 'delta_w', 'delta_w_in': 'delta_w', 'delta_b_in': 'delta_w', 'delta_sgu_norm_g': 'delta_w', 'delta_sgu_norm_b': 'delta_w', 'delta_sgu_w_s': 'delta_w', 'delta_sgu_b_s': 'delta_w', 'delta_ret_decay_logit': 'delta_w', 'delta_w_branch_a': 'delta_w', 'delta_w_branch_b': 'delta_w', 'delta_w_out': 'delta_w', 'delta_ffn2_norm': 'delta_w', 'delta_ffn2_w_gate': 'delta_w', 'delta_ffn2_w_up': 'delta_w', 'delta_ffn2_w_down': 'delta_w', 'delta_final_norm': 'delta_w', 'new_m_ffn1_norm': 'new_m', 'new_m_ffn1_w_gate': 'new_m', 'new_m_ffn1_w_up': 'new_m', 'new_m_ffn1_w_down': 'new_m', 'new_m_mix_norm': 'new_m', 'new_m_w_in': 'new_m', 'new_m_b_in': 'new_m', 'new_m_sgu_norm_g': 'new_m', 'new_m_sgu_norm_b': 'new_m', 'new_m_sgu_w_s': 'new_m', 'new_m_sgu_b_s': 'new_m', 'new_m_ret_decay_logit': 'new_m', 'new_m_w_branch_a': 'new_m', 'new_m_w_branch_b': 'new_m', 'new_m_w_out': 'new_m', 'new_m_ffn2_norm': 'new_m', 'new_m_ffn2_w_gate': 'new_m', 'new_m_ffn2_w_up': 'new_m', 'new_m_ffn2_w_down': 'new_m', 'new_m_final_norm': 'new_m', 'new_v_ffn1_norm': 'new_v', 'new_v_ffn1_w_gate': 'new_v', 'new_v_ffn1_w_up': 'new_v', 'new_v_ffn1_w_down': 'new_v', 'new_v_mix_norm': 'new_v', 'new_v_w_in': 'new_v', 'new_v_b_in': 'new_v', 'new_v_sgu_norm_g': 'new_v', 'new_v_sgu_norm_b': 'new_v', 'new_v_sgu_w_s': 'new_v', 'new_v_sgu_b_s': 'new_v', 'new_v_ret_decay_logit': 'new_v', 'new_v_w_branch_a': 'new_v', 'new_v_w_branch_b': 'new_v', 'new_v_w_out': 'new_v', 'new_v_ffn2_norm': 'new_v', 'new_v_ffn2_w_gate': 'new_v', 'new_v_ffn2_w_up': 'new_v', 'new_v_ffn2_w_down': 'new_v', 'new_v_final_norm': 'new_v'}


def _forward(args):
    return _fwd_reference(*[args[k] for k in FWD_PARAMS])


def _output_shape():
    def fwd():
        inp = _fwd_setup_inputs(0)
        return _fwd_reference(*[inp[k] for k in FWD_PARAMS])
    out = _jax.eval_shape(fwd)
    return out.shape, out.dtype

N_MICROBATCH = 1
ADAM_LR = 0.001
ADAM_B1 = 0.9
ADAM_B2 = 0.999
ADAM_EPS = 1e-08
ADAM_WD = 0.01
ADAM_STEP = 10
PER_EXAMPLE_BATCH_AXIS = {'x': 0, 'loss_target': 0}
SHARED_INPUTS = []
_WEIGHT_DTYPES = {'ffn1_norm': _jnp.float32, 'ffn1_w_gate': _jnp.float32, 'ffn1_w_up': _jnp.float32, 'ffn1_w_down': _jnp.float32, 'mix_norm': _jnp.float32, 'w_in': _jnp.float32, 'b_in': _jnp.float32, 'sgu_norm_g': _jnp.float32, 'sgu_norm_b': _jnp.float32, 'sgu_w_s': _jnp.float32, 'sgu_b_s': _jnp.float32, 'ret_decay_logit': _jnp.float32, 'w_branch_a': _jnp.float32, 'w_branch_b': _jnp.float32, 'w_out': _jnp.float32, 'ffn2_norm': _jnp.float32, 'ffn2_w_gate': _jnp.float32, 'ffn2_w_up': _jnp.float32, 'ffn2_w_down': _jnp.float32, 'final_norm': _jnp.float32}
MOMENT_SCALE = {'ffn1_norm': 8.996756e-02, 'ffn1_w_gate': 3.814726e-02, 'ffn1_w_up': 3.689190e-02, 'ffn1_w_down': 6.119773e-02, 'mix_norm': 1.559799e-01, 'w_in': 5.157344e-02, 'b_in': 5.168211e-02, 'sgu_norm_g': 5.552659e-02, 'sgu_norm_b': 5.506681e-02, 'sgu_w_s': 7.581522e-02, 'sgu_b_s': 8.047773e-02, 'ret_decay_logit': 3.161622e-01, 'w_branch_a': 7.740241e-02, 'w_branch_b': 4.919625e-02, 'w_out': 9.170522e-02, 'ffn2_norm': 6.573935e-02, 'ffn2_w_gate': 2.661599e-02, 'ffn2_w_up': 2.582453e-02, 'ffn2_w_down': 4.269743e-02, 'final_norm': 3.203787e+01}


def _to_microbatches(a, axis):
    t = _jnp.moveaxis(a, axis, 0)
    t = t.reshape((N_MICROBATCH, t.shape[0] // N_MICROBATCH) + t.shape[1:])
    return _jnp.moveaxis(t, 1, axis + 1)


def setup_inputs(seed: int = 0) -> dict:
    inp = _fwd_setup_inputs(seed)
    key = _jax.random.fold_in(_jax.random.key(seed), 7919)
    shape, _ = _output_shape()
    out = dict(inp)
    out["loss_target"] = _jax.random.normal(_jax.random.fold_in(key, 0), shape, _jnp.float32)
    for i, name in enumerate(TWIN_WEIGHTS):
        w = inp[name].astype(_jnp.float32)
        if MOMENT_SCALE is None:
            s = _jnp.sqrt(_jnp.mean(_jnp.square(w)) + 1e-30)
        else:
            s = MOMENT_SCALE[name]
        km, kv = _jax.random.split(_jax.random.fold_in(key, i + 1))
        out[name] = w
        out["m_" + name] = s * _jax.random.normal(km, w.shape, _jnp.float32)
        out["v_" + name] = (s * s) * _jax.random.uniform(kv, w.shape, _jnp.float32, 0.5, 1.5)
    if N_MICROBATCH > 1:
        for name, axis in PER_EXAMPLE_BATCH_AXIS.items():
            out[name] = _to_microbatches(out[name], axis)
    return {'x': out['x'], 'ffn1_norm': out['ffn1_norm'], 'ffn1_w_gate': out['ffn1_w_gate'], 'ffn1_w_up': out['ffn1_w_up'], 'ffn1_w_down': out['ffn1_w_down'], 'mix_norm': out['mix_norm'], 'w_in': out['w_in'], 'b_in': out['b_in'], 'sgu_norm_g': out['sgu_norm_g'], 'sgu_norm_b': out['sgu_norm_b'], 'sgu_w_s': out['sgu_w_s'], 'sgu_b_s': out['sgu_b_s'], 'ret_decay_logit': out['ret_decay_logit'], 'w_branch_a': out['w_branch_a'], 'w_branch_b': out['w_branch_b'], 'w_out': out['w_out'], 'ffn2_norm': out['ffn2_norm'], 'ffn2_w_gate': out['ffn2_w_gate'], 'ffn2_w_up': out['ffn2_w_up'], 'ffn2_w_down': out['ffn2_w_down'], 'final_norm': out['final_norm'], 'loss_target': out['loss_target'], 'm_ffn1_norm': out['m_ffn1_norm'], 'm_ffn1_w_gate': out['m_ffn1_w_gate'], 'm_ffn1_w_up': out['m_ffn1_w_up'], 'm_ffn1_w_down': out['m_ffn1_w_down'], 'm_mix_norm': out['m_mix_norm'], 'm_w_in': out['m_w_in'], 'm_b_in': out['m_b_in'], 'm_sgu_norm_g': out['m_sgu_norm_g'], 'm_sgu_norm_b': out['m_sgu_norm_b'], 'm_sgu_w_s': out['m_sgu_w_s'], 'm_sgu_b_s': out['m_sgu_b_s'], 'm_ret_decay_logit': out['m_ret_decay_logit'], 'm_w_branch_a': out['m_w_branch_a'], 'm_w_branch_b': out['m_w_branch_b'], 'm_w_out': out['m_w_out'], 'm_ffn2_norm': out['m_ffn2_norm'], 'm_ffn2_w_gate': out['m_ffn2_w_gate'], 'm_ffn2_w_up': out['m_ffn2_w_up'], 'm_ffn2_w_down': out['m_ffn2_w_down'], 'm_final_norm': out['m_final_norm'], 'v_ffn1_norm': out['v_ffn1_norm'], 'v_ffn1_w_gate': out['v_ffn1_w_gate'], 'v_ffn1_w_up': out['v_ffn1_w_up'], 'v_ffn1_w_down': out['v_ffn1_w_down'], 'v_mix_norm': out['v_mix_norm'], 'v_w_in': out['v_w_in'], 'v_b_in': out['v_b_in'], 'v_sgu_norm_g': out['v_sgu_norm_g'], 'v_sgu_norm_b': out['v_sgu_norm_b'], 'v_sgu_w_s': out['v_sgu_w_s'], 'v_sgu_b_s': out['v_sgu_b_s'], 'v_ret_decay_logit': out['v_ret_decay_logit'], 'v_w_branch_a': out['v_w_branch_a'], 'v_w_branch_b': out['v_w_branch_b'], 'v_w_out': out['v_w_out'], 'v_ffn2_norm': out['v_ffn2_norm'], 'v_ffn2_w_gate': out['v_ffn2_w_gate'], 'v_ffn2_w_up': out['v_ffn2_w_up'], 'v_ffn2_w_down': out['v_ffn2_w_down'], 'v_final_norm': out['v_final_norm']}


def _loss(weights, diff, rest, loss_target):
    with _jax.named_scope("forward"):
        args = {**rest, TWIN_DIFF_INPUT: diff, **{k: w.astype(_WEIGHT_DTYPES[k]) for k, w in weights.items()}}
        y = _forward(args)
    with _jax.named_scope("loss_head"):
        err = _jnp.square(y.astype(_jnp.float32) - loss_target)
        return 0.5 * _jnp.sum(_jnp.mean(err, axis=-1)) if err.ndim else 0.5 * err


def _adamw(w, g, m, v):
    m = ADAM_B1 * m + (1.0 - ADAM_B1) * g
    v = ADAM_B2 * v + (1.0 - ADAM_B2) * _jnp.square(g)
    m_hat = m / (1.0 - ADAM_B1 ** ADAM_STEP)
    v_hat = v / (1.0 - ADAM_B2 ** ADAM_STEP)
    delta = -ADAM_LR * (m_hat / (_jnp.sqrt(v_hat) + ADAM_EPS) + ADAM_WD * w)
    return delta, m, v


def reference(x, ffn1_norm, ffn1_w_gate, ffn1_w_up, ffn1_w_down, mix_norm, w_in, b_in, sgu_norm_g, sgu_norm_b, sgu_w_s, sgu_b_s, ret_decay_logit, w_branch_a, w_branch_b, w_out, ffn2_norm, ffn2_w_gate, ffn2_w_up, ffn2_w_down, final_norm, loss_target, m_ffn1_norm, m_ffn1_w_gate, m_ffn1_w_up, m_ffn1_w_down, m_mix_norm, m_w_in, m_b_in, m_sgu_norm_g, m_sgu_norm_b, m_sgu_w_s, m_sgu_b_s, m_ret_decay_logit, m_w_branch_a, m_w_branch_b, m_w_out, m_ffn2_norm, m_ffn2_w_gate, m_ffn2_w_up, m_ffn2_w_down, m_final_norm, v_ffn1_norm, v_ffn1_w_gate, v_ffn1_w_up, v_ffn1_w_down, v_mix_norm, v_w_in, v_b_in, v_sgu_norm_g, v_sgu_norm_b, v_sgu_w_s, v_sgu_b_s, v_ret_decay_logit, v_w_branch_a, v_w_branch_b, v_w_out, v_ffn2_norm, v_ffn2_w_gate, v_ffn2_w_up, v_ffn2_w_down, v_final_norm):
    given = dict(x=x, ffn1_norm=ffn1_norm, ffn1_w_gate=ffn1_w_gate, ffn1_w_up=ffn1_w_up, ffn1_w_down=ffn1_w_down, mix_norm=mix_norm, w_in=w_in, b_in=b_in, sgu_norm_g=sgu_norm_g, sgu_norm_b=sgu_norm_b, sgu_w_s=sgu_w_s, sgu_b_s=sgu_b_s, ret_decay_logit=ret_decay_logit, w_branch_a=w_branch_a, w_branch_b=w_branch_b, w_out=w_out, ffn2_norm=ffn2_norm, ffn2_w_gate=ffn2_w_gate, ffn2_w_up=ffn2_w_up, ffn2_w_down=ffn2_w_down, final_norm=final_norm, loss_target=loss_target, m_ffn1_norm=m_ffn1_norm, m_ffn1_w_gate=m_ffn1_w_gate, m_ffn1_w_up=m_ffn1_w_up, m_ffn1_w_down=m_ffn1_w_down, m_mix_norm=m_mix_norm, m_w_in=m_w_in, m_b_in=m_b_in, m_sgu_norm_g=m_sgu_norm_g, m_sgu_norm_b=m_sgu_norm_b, m_sgu_w_s=m_sgu_w_s, m_sgu_b_s=m_sgu_b_s, m_ret_decay_logit=m_ret_decay_logit, m_w_branch_a=m_w_branch_a, m_w_branch_b=m_w_branch_b, m_w_out=m_w_out, m_ffn2_norm=m_ffn2_norm, m_ffn2_w_gate=m_ffn2_w_gate, m_ffn2_w_up=m_ffn2_w_up, m_ffn2_w_down=m_ffn2_w_down, m_final_norm=m_final_norm, v_ffn1_norm=v_ffn1_norm, v_ffn1_w_gate=v_ffn1_w_gate, v_ffn1_w_up=v_ffn1_w_up, v_ffn1_w_down=v_ffn1_w_down, v_mix_norm=v_mix_norm, v_w_in=v_w_in, v_b_in=v_b_in, v_sgu_norm_g=v_sgu_norm_g, v_sgu_norm_b=v_sgu_norm_b, v_sgu_w_s=v_sgu_w_s, v_sgu_b_s=v_sgu_b_s, v_ret_decay_logit=v_ret_decay_logit, v_w_branch_a=v_w_branch_a, v_w_branch_b=v_w_branch_b, v_w_out=v_w_out, v_ffn2_norm=v_ffn2_norm, v_ffn2_w_gate=v_ffn2_w_gate, v_ffn2_w_up=v_ffn2_w_up, v_ffn2_w_down=v_ffn2_w_down, v_final_norm=v_final_norm)
    weights = {n: given[n] for n in TWIN_WEIGHTS}
    shared = {n: given[n] for n in SHARED_INPUTS}
    per_example = {n: given[n] for n in ['x']}
    grad_fn = _jax.value_and_grad(_loss, argnums=(0, 1))

    def one_microbatch(ex, loss_target):
        ex = dict(ex)
        diff = ex.pop(TWIN_DIFF_INPUT)
        return grad_fn(weights, diff, {**shared, **ex}, loss_target)

    if N_MICROBATCH == 1:
        loss, (grad_w, grad_x) = one_microbatch(per_example, given["loss_target"])
    else:
        def body(carry, xs):
            loss_sum, grad_sum = carry
            l_k, (gw_k, gx_k) = one_microbatch(xs[0], xs[1])
            with _jax.named_scope("update"):
                return (loss_sum + l_k, _jax.tree.map(_jnp.add, grad_sum, gw_k)), gx_k

        init = (_jnp.zeros((), _jnp.float32), _jax.tree.map(_jnp.zeros_like, weights))
        (loss, grad_w), grad_x = _jax.lax.scan(body, init, (per_example, given["loss_target"]))
    with _jax.named_scope("update"):
        delta_w, new_m, new_v = {}, {}, {}
        for n in TWIN_WEIGHTS:
            delta_w[n], new_m[n], new_v[n] = _adamw(weights[n], grad_w[n], given["m_" + n], given["v_" + n])
    return (loss, grad_x, *[grad_w[n] for n in TWIN_WEIGHTS], *[delta_w[n] for n in TWIN_WEIGHTS],
            *[new_m[n] for n in TWIN_WEIGHTS], *[new_v[n] for n in TWIN_WEIGHTS])
```

```python
import functools
import math

import jax
import jax.numpy as jnp
from jax import lax
from jax.experimental import pallas as pl
from jax.experimental.pallas import tpu as pltpu

f32 = jnp.float32
CDT = jnp.bfloat16

D = 1024
F = 2816
C = 128
H = 4
DK = 256
G = 4
NDEV = 8
NCHIP = 4
EPS = 1e-6
ROPE_BASE = 10000.0
FT = 256
V7X_VMEM_BYTES = 64 * 1024 * 1024
VMEM_LIMIT = V7X_VMEM_BYTES - 8 * 1024 * 1024

ADAM_LR, ADAM_B1, ADAM_B2, ADAM_EPS, ADAM_WD, ADAM_STEP = 0.001, 0.9, 0.999, 1e-08, 0.01, 10
BC1 = 1.0 - ADAM_B1 ** ADAM_STEP
BC2 = 1.0 - ADAM_B2 ** ADAM_STEP

W_NAMES = ("w_in", "w_branch_a", "w_branch_b", "w_out",
           "ffn1_w_gate", "ffn1_w_up", "ffn1_w_down", "ffn2_w_gate", "ffn2_w_up", "ffn2_w_down")
W_ROWS = dict(w_in=1024, w_branch_a=128, w_branch_b=128, w_out=128,
              ffn1_w_gate=352, ffn1_w_up=352, ffn1_w_down=352, ffn2_w_gate=352, ffn2_w_up=352, ffn2_w_down=352)
W_TRANSPOSED = ("ffn1_w_gate", "ffn1_w_up", "ffn2_w_gate", "ffn2_w_up")
SHARD_ROWS = sum(W_ROWS.values())
W_OFF = {}
W_BASE = {}
_o = 0
for _n in W_NAMES:
    W_OFF[_n] = _o
    W_BASE[_n] = NDEV * _o
    _o += W_ROWS[_n]
FULL_ROWS = NDEV * SHARD_ROWS

SLOT_U, SLOT_V, SLOT_GA, SLOT_GB, SLOT_Q, SLOT_K, SLOT_VR, SLOT_GR = range(8)


def _seg_of_slot(p):
    return jnp.where(p < 2, p, jnp.where(p < 4, p + 4, p - 2))


def _mm(a, b):
    return jnp.dot(a, b, preferred_element_type=f32)


def _mm_nt(a, b):
    return lax.dot_general(a, b, (((1,), (1,)), ((), ())), preferred_element_type=f32)


def _mm_tn(a, b):
    return lax.dot_general(a, b, (((0,), (0,)), ((), ())), preferred_element_type=f32)


def _params(*sem):
    return pltpu.CompilerParams(dimension_semantics=sem, vmem_limit_bytes=VMEM_LIMIT)


def _resident(shape, index_map):
    return pl.BlockSpec(shape, index_map, pipeline_mode=pl.Buffered(1))


def _gelu(x):
    return 0.5 * x * (1.0 + lax.erf(x * (1.0 / math.sqrt(2.0))))


def _gelu_grad(x):
    return 0.5 * (1.0 + lax.erf(x * (1.0 / math.sqrt(2.0)))) + x * jnp.exp(-0.5 * x * x) * (1.0 / math.sqrt(2.0 * math.pi))


def _rms_fwd(x, n):
    r = lax.rsqrt(jnp.mean(x * x, axis=-1, keepdims=True) + EPS)
    xh = x * r
    return r, xh, xh * n


def _rms_bwd(dh, r, xh, n):
    dxh = dh * n
    dx = r * (dxh - xh * jnp.mean(dxh * xh, axis=-1, keepdims=True))
    return dx, jnp.sum(dh * xh, axis=0, keepdims=True)


MESH_ID = pl.DeviceIdType.MESH
_HBM = pl.BlockSpec(memory_space=pltpu.HBM)


def _my_place():
    return lax.axis_index("x"), lax.axis_index("y"), lax.axis_index("c")


def _ici_peers(x, y, c):
    return [((1 - x, y, c), 2 * (1 - x) + y), ((x, 1 - y, c), 2 * x + 1 - y), ((1 - x, 1 - y, c), 2 * (1 - x) + 1 - y)]


def _exchange(name, n_remote, n_local, body_fn, operands, out_shapes, aliases=None):
    def body(*refs):
        refs, (ssem, rsem, lsem) = refs[:-3], refs[-3:]

        def rcopy(i, src, dst, dev):
            return pltpu.make_async_remote_copy(src_ref=src, dst_ref=dst, send_sem=ssem.at[i], recv_sem=rsem.at[i],
                                                device_id=dev, device_id_type=MESH_ID)

        def lcopy(i, src, dst):
            return pltpu.make_async_copy(src, dst, lsem.at[i])

        sends, recvs, locs = body_fn(*refs, rcopy, lcopy)
        for cp in locs:
            cp.start()
        for cp in sends:
            cp.start()
        for cp in recvs:
            cp.wait_recv()
        for cp in sends:
            cp.wait_send()
        for cp in locs:
            cp.wait()

    return pl.pallas_call(
        body, name=name, out_shape=out_shapes,
        in_specs=[_HBM] * len(operands), out_specs=[_HBM] * len(out_shapes),
        scratch_shapes=[pltpu.SemaphoreType.DMA((n_remote,)), pltpu.SemaphoreType.DMA((n_remote,)),
                        pltpu.SemaphoreType.DMA((max(n_local, 1),))],
        input_output_aliases=aliases or {},
    )(*operands)


def _rows(ref, start, n):
    return ref.at[pl.ds(start, n), :]


def _all_gather_weights(shard):
    nw = len(W_NAMES)

    def ici(sh, full, rcopy, lcopy):
        x, y, c = _my_place()
        me = 4 * x + 2 * y + c
        sends, recvs, locs = [], [], []
        for wi, w in enumerate(W_NAMES):
            rows = W_ROWS[w]
            src = _rows(sh, W_OFF[w], rows)
            locs.append(lcopy(wi, src, _rows(full, W_BASE[w] + me * rows, rows)))
            for j, (dev, chip) in enumerate(_ici_peers(x, y, c)):
                sends.append(rcopy(j * nw + wi, src, _rows(full, W_BASE[w] + me * rows, rows), dev))
                theirs = _rows(full, W_BASE[w] + (2 * chip + c) * rows, rows)
                recvs.append(rcopy(j * nw + wi, theirs, theirs, dev))
        return sends, recvs, locs

    full = _exchange("ag_ici", 3 * nw, nw, ici, [shard], [jax.ShapeDtypeStruct((FULL_ROWS, D), shard.dtype)])[0]

    def d2d(full_in, full_out, rcopy, lcopy):
        x, y, c = _my_place()
        sib = (x, y, 1 - c)
        sends, recvs = [], []
        for wi, w in enumerate(W_NAMES):
            rows = W_ROWS[w]
            for k in range(NCHIP):
                mine = _rows(full_out, W_BASE[w] + (2 * k + c) * rows, rows)
                theirs = _rows(full_out, W_BASE[w] + (2 * k + 1 - c) * rows, rows)
                sends.append(rcopy(k * nw + wi, mine, mine, sib))
                recvs.append(rcopy(k * nw + wi, theirs, theirs, sib))
        return sends, recvs, []

    return _exchange("ag_d2d", NCHIP * nw, 0, d2d, [full], [jax.ShapeDtypeStruct(full.shape, full.dtype)], {0: 0})[0]


def _rs_d2d(gfull):
    nw = len(W_NAMES)

    def d2d(g, land, rcopy, lcopy):
        x, y, c = _my_place()
        sib = (x, y, 1 - c)
        sends, recvs = [], []
        for wi, w in enumerate(W_NAMES):
            rows = W_ROWS[w]
            for k in range(NCHIP):
                src = _rows(g, W_BASE[w] + (2 * k + 1 - c) * rows, rows)
                dst = land.at[k, pl.ds(W_OFF[w], rows), :]
                sends.append(rcopy(k * nw + wi, src, dst, sib))
                recvs.append(rcopy(k * nw + wi, dst, dst, sib))
        return sends, recvs, []

    return _exchange("rs_d2d", NCHIP * nw, 0, d2d, [gfull], [jax.ShapeDtypeStruct((NCHIP, SHARD_ROWS, D), gfull.dtype)])[0]


def _rs_ici(part):
    def ici(p, land, rcopy, lcopy):
        x, y, c = _my_place()
        mychip = 2 * x + y
        sends, recvs = [], []
        for j, (dev, chip) in enumerate(_ici_peers(x, y, c)):
            sends.append(rcopy(j, p.at[chip], land.at[mychip], dev))
            recvs.append(rcopy(j, land.at[chip], land.at[chip], dev))
        return sends, recvs, [lcopy(0, p.at[mychip], land.at[mychip])]

    return _exchange("rs_ici", 3, 1, ici, [part], [jax.ShapeDtypeStruct(part.shape, part.dtype)])[0]


def _all_gather_small(a, b):
    def body_fn(a_ref, b_ref, ga, gb, rcopy, lcopy):
        x, y, c = _my_place()
        me = 4 * x + 2 * y + c
        sends, recvs = [], []
        i = 0
        for dx in range(2):
            for dy in range(2):
                for dc in range(2):
                    if dx + dy + dc == 0:
                        continue
                    px, py, pc = (1 - x if dx else x), (1 - y if dy else y), (1 - c if dc else c)
                    peer = 4 * px + 2 * py + pc
                    sends.append(rcopy(2 * i, a_ref, ga.at[me], (px, py, pc)))
                    sends.append(rcopy(2 * i + 1, b_ref, gb.at[me], (px, py, pc)))
                    recvs.append(rcopy(2 * i, ga.at[peer], ga.at[peer], (px, py, pc)))
                    recvs.append(rcopy(2 * i + 1, gb.at[peer], gb.at[peer], (px, py, pc)))
                    i += 1
        return sends, recvs, [lcopy(0, a_ref, ga.at[me]), lcopy(1, b_ref, gb.at[me])]

    return _exchange("ag_small", 14, 2, body_fn, [a, b],
                     [jax.ShapeDtypeStruct((NDEV,) + a.shape, a.dtype), jax.ShapeDtypeStruct((NDEV,) + b.shape, b.dtype)])


def _wblock(w):
    rows = NDEV * W_ROWS[w]
    assert W_BASE[w] % rows == 0
    return _resident((rows, D), lambda *_: (W_BASE[w] // rows, 0))


def _ffn_fwd(name, x, nrm, wfull, wg, wu, wd, tm=512):
    S = x.shape[0]

    def body(x_ref, n_ref, wg_ref, wu_ref, wd_ref, y_ref, g_ref, u_ref, acc_ref):
        xv = x_ref[...]
        _, _, h = _rms_fwd(xv, n_ref[...])
        h = h.astype(CDT)
        for ci in range(F // FT):
            sl = slice(ci * FT, (ci + 1) * FT)
            g = _mm_nt(h, wg_ref[sl, :])
            u = _mm_nt(h, wu_ref[sl, :])
            g_ref[:, sl] = g.astype(CDT)
            u_ref[:, sl] = u.astype(CDT)
            a = (g * jax.nn.sigmoid(g) * u).astype(CDT)
            o = _mm(a, wd_ref[sl, :])
            if ci == 0:
                acc_ref[...] = o
            else:
                acc_ref[...] += o
        y_ref[...] = xv + 0.5 * acc_ref[...]

    return pl.pallas_call(
        body, name=name, grid=(S // tm,),
        out_shape=[jax.ShapeDtypeStruct((S, D), f32), jax.ShapeDtypeStruct((S, F), CDT), jax.ShapeDtypeStruct((S, F), CDT)],
        in_specs=[pl.BlockSpec((tm, D), lambda i: (i, 0)), _resident((1, D), lambda i: (0, 0)),
                  _wblock(wg), _wblock(wu), _wblock(wd)],
        out_specs=[pl.BlockSpec((tm, D), lambda i: (i, 0)), pl.BlockSpec((tm, F), lambda i: (i, 0)),
                   pl.BlockSpec((tm, F), lambda i: (i, 0))],
        scratch_shapes=[pltpu.VMEM((tm, D), f32)],
        compiler_params=_params("arbitrary"),
    )(x, nrm, wfull, wfull, wfull)


def _proj_fwd(x1, nrm, wfull, b3, tm=512):
    S = x1.shape[0]
    wbase = W_BASE["w_in"] // D

    def body(x_ref, n_ref, w_ref, b_ref, p_ref, h_ref):
        @pl.when(pl.program_id(1) == 0)
        def _():
            _, _, h = _rms_fwd(x_ref[...], n_ref[...])
            h_ref[...] = h.astype(CDT)

        p_ref[...] = (_mm(h_ref[...], w_ref[...]) + b_ref[...]).astype(CDT)

    return pl.pallas_call(
        body, name="proj_fwd", grid=(S // tm, 8),
        out_shape=[jax.ShapeDtypeStruct((8, S, D), CDT), jax.ShapeDtypeStruct((S, D), CDT)],
        in_specs=[pl.BlockSpec((tm, D), lambda i, p: (i, 0)), _resident((1, D), lambda i, p: (0, 0)),
                  pl.BlockSpec((D, D), lambda i, p: (wbase + _seg_of_slot(p), 0)),
                  pl.BlockSpec((None, 1, D), lambda i, p: (_seg_of_slot(p), 0, 0))],
        out_specs=[pl.BlockSpec((None, tm, D), lambda i, p: (p, i, 0)), pl.BlockSpec((tm, D), lambda i, p: (i, 0))],
        compiler_params=_params("arbitrary", "arbitrary"),
    )(x1, nrm, wfull, b3)


def _sgu_norm(va, gn, bn):
    mu = jnp.mean(va, axis=-1, keepdims=True)
    xc = va - mu
    rstd = lax.rsqrt(jnp.mean(xc * xc, axis=-1, keepdims=True) + EPS)
    vhat = xc * rstd
    return rstd, vhat, vhat * gn + bn


def _sgu_fwd(proj, gn, bn, ws, bsc, tm=512):
    S = proj.shape[1]
    GW = D // G

    def body(p_ref, gn_ref, bn_ref, ws_ref, bs_ref, a_ref):
        ua = _gelu(p_ref[0].astype(f32))
        va = _gelu(p_ref[1].astype(f32))
        _, _, vn = _sgu_norm(va, gn_ref[...], bn_ref[...])
        vn = vn.astype(CDT)
        for ch in range(tm // C):
            rs = slice(ch * C, (ch + 1) * C)
            for gi in range(G):
                cs = slice(gi * GW, (gi + 1) * GW)
                s = _mm(ws_ref[gi], vn[rs, cs]) + bs_ref[gi]
                a_ref[rs, cs] = (ua[rs, cs] * s).astype(CDT)

    return pl.pallas_call(
        body, name="sgu_fwd", grid=(S // tm,),
        out_shape=jax.ShapeDtypeStruct((S, D), CDT),
        in_specs=[pl.BlockSpec((2, tm, D), lambda i: (0, i, 0)), _resident((1, D), lambda i: (0, 0)),
                  _resident((1, D), lambda i: (0, 0)), _resident((G, C, C), lambda i: (0, 0, 0)),
                  _resident((G, C, 1), lambda i: (0, 0, 0))],
        out_specs=pl.BlockSpec((tm, D), lambda i: (i, 0)),
        compiler_params=_params("arbitrary"),
    )(proj, gn, bn, ws, bsc)


def _decay_tables(dl_ref):
    lg = jax.nn.log_sigmoid(dl_ref[0:2, :])
    lgf, lgb = lg[0:1, :], lg[1:2, :]
    ri = lax.broadcasted_iota(jnp.int32, (C, C), 0)
    ci = lax.broadcasted_iota(jnp.int32, (C, C), 1)
    d = (ri - ci).astype(f32)
    lower = d >= 0
    dmat = jnp.where(lower, jnp.exp(d * lgf[:, :C]), jnp.exp(-d * lgb[:, :C]))
    dmat_t = jnp.where(d <= 0, jnp.exp(-d * lgf[:, :C]), jnp.exp(d * lgb[:, :C]))
    pos = lax.broadcasted_iota(jnp.int32, (C, DK), 0).astype(f32)
    t = dict(
        lgf=lgf, lgb=lgb, d=d, lower=lower, dmat=dmat, dmat_t=dmat_t, pos=pos,
        fq=jnp.exp((pos + 1.0) * lgf), fk=jnp.exp((C - 1.0 - pos) * lgf),
        bq=jnp.exp((C - pos) * lgb), bk=jnp.exp(pos * lgb),
        lamf=jnp.exp(float(C) * lgf), lamb=jnp.exp(float(C) * lgb),
    )
    return t


def _rotate(t, co, si):
    t1, t2 = t[:, :DK // 2], t[:, DK // 2:]
    return jnp.concatenate([t1 * co - t2 * si, t2 * co + t1 * si], axis=-1)


def _unrotate(t, co, si):
    t1, t2 = t[:, :DK // 2], t[:, DK // 2:]
    return jnp.concatenate([t1 * co + t2 * si, t2 * co - t1 * si], axis=-1)


K_SCALE = DK ** -0.5
ROT_ROWS = 512


def _rotate_into(q_ref, k_ref, cos_ref, sin_ref, qs_ref, ks_ref, S):
    for rt in range(S // ROT_ROWS):
        rs = slice(rt * ROT_ROWS, (rt + 1) * ROT_ROWS)
        co, si = cos_ref[rs, :], sin_ref[rs, :]
        qs_ref[rs, :] = _rotate(q_ref[rs, :].astype(f32), co, si).astype(CDT)
        ks_ref[rs, :] = (_rotate(k_ref[rs, :].astype(f32), co, si) * K_SCALE).astype(CDT)


def _ret_fwd(proj, cos, sin, dl):
    S = proj.shape[1]
    NC = S // C

    def body(q_ref, k_ref, v_ref, g_ref, cos_ref, sin_ref, dl_ref, R_ref, r_ref, qs_ref, ks_ref, st_ref):
        t = _decay_tables(dl_ref)
        _rotate_into(q_ref, k_ref, cos_ref, sin_ref, qs_ref, ks_ref, S)

        def chunk(n):
            rows = pl.ds(pl.multiple_of(n * C, C), C)
            return rows, qs_ref[rows, :], ks_ref[rows, :], v_ref[rows, :]

        st_ref[...] = jnp.zeros_like(st_ref)

        def fwd_step(n, carry):
            rows, qn, kn, vn = chunk(n)
            sc = _mm_nt(qn, kn) * t["dmat"]
            out = _mm(sc.astype(CDT), vn)
            out += _mm((qn.astype(f32) * t["fq"]).astype(CDT), st_ref[...].astype(CDT))
            R_ref[rows, :] = out
            st_ref[...] = st_ref[...] * t["lamf"] + _mm_tn((kn.astype(f32) * t["fk"]).astype(CDT), vn)
            return carry

        lax.fori_loop(0, NC, fwd_step, 0)
        st_ref[...] = jnp.zeros_like(st_ref)

        def bwd_step(i, carry):
            rows, qn, kn, vn = chunk(NC - 1 - i)
            R_ref[rows, :] += _mm((qn.astype(f32) * t["bq"]).astype(CDT), st_ref[...].astype(CDT))
            st_ref[...] = st_ref[...] * t["lamb"] + _mm_tn((kn.astype(f32) * t["bk"]).astype(CDT), vn)
            return carry

        lax.fori_loop(0, NC, bwd_step, 0)
        for rt in range(S // ROT_ROWS):
            rs = slice(rt * ROT_ROWS, (rt + 1) * ROT_ROWS)
            R = R_ref[rs, :]
            rn = R * lax.rsqrt(jnp.mean(R * R, axis=-1, keepdims=True) + EPS)
            g = g_ref[rs, :].astype(f32)
            r_ref[rs, :] = (rn * g * jax.nn.sigmoid(g)).astype(CDT)

    def seg(slot):
        return pl.BlockSpec((None, S, DK), lambda h: (slot, 0, h))

    return pl.pallas_call(
        body, name="ret_fwd", grid=(H,),
        out_shape=[jax.ShapeDtypeStruct((S, H * DK), f32), jax.ShapeDtypeStruct((S, H * DK), CDT)],
        in_specs=[seg(SLOT_Q), seg(SLOT_K), seg(SLOT_VR), seg(SLOT_GR),
                  _resident((S, DK // 2), lambda h: (0, 0)), _resident((S, DK // 2), lambda h: (0, 0)),
                  pl.BlockSpec((None, 8, DK), lambda h: (h, 0, 0))],
        out_specs=[pl.BlockSpec((S, DK), lambda h: (0, h)), pl.BlockSpec((S, DK), lambda h: (0, h))],
        scratch_shapes=[pltpu.VMEM((S, DK), CDT), pltpu.VMEM((S, DK), CDT), pltpu.VMEM((DK, DK), f32)],
        compiler_params=_params("arbitrary"),
    )(proj, proj, proj, proj, cos, sin, dl)


def _merge_fwd(a, r, proj, x1, wfull, tm=512):
    S = x1.shape[0]

    def body(a_ref, r_ref, gt_ref, x_ref, wa_ref, wb_ref, wo_ref, x2_ref, ya_ref, yb_ref):
        ya = _mm(a_ref[...], wa_ref[...])
        yb = _mm(r_ref[...], wb_ref[...])
        ya_ref[...] = ya.astype(CDT)
        yb_ref[...] = yb.astype(CDT)
        mix = jax.nn.sigmoid(gt_ref[0].astype(f32)) * ya + jax.nn.sigmoid(gt_ref[1].astype(f32)) * yb
        x2_ref[...] = x_ref[...] + _mm(mix.astype(CDT), wo_ref[...])

    tok = pl.BlockSpec((tm, D), lambda i: (i, 0))
    return pl.pallas_call(
        body, name="merge_fwd", grid=(S // tm,),
        out_shape=[jax.ShapeDtypeStruct((S, D), f32), jax.ShapeDtypeStruct((S, D), CDT), jax.ShapeDtypeStruct((S, D), CDT)],
        in_specs=[tok, tok, pl.BlockSpec((2, tm, D), lambda i: (SLOT_GA // 2, i, 0)), tok,
                  _wblock("w_branch_a"), _wblock("w_branch_b"), _wblock("w_out")],
        out_specs=[tok, tok, tok],
        compiler_params=_params("arbitrary"),
    )(a, r, proj, x1, wfull, wfull, wfull)


def _loss_head(x3, fn, target, tm=512):
    S = x3.shape[0]

    def body(x_ref, n_ref, t_ref, dx_ref, dn_ref, l_ref):
        n = n_ref[...]
        r, xh, y = _rms_fwd(x_ref[...], n)
        e = y - t_ref[...]
        dy = e * (1.0 / D)
        dx, dn = _rms_bwd(dy, r, xh, n)
        dx_ref[...] = dx
        part = 0.5 * jnp.sum(jnp.sum(e * e, axis=-1, keepdims=True), axis=0, keepdims=True) * (1.0 / D)

        @pl.when(pl.program_id(0) == 0)
        def _():
            dn_ref[...] = jnp.zeros_like(dn_ref)
            l_ref[...] = jnp.zeros_like(l_ref)

        dn_ref[...] += dn
        l_ref[...] += jnp.broadcast_to(part, l_ref.shape)

    tok = pl.BlockSpec((tm, D), lambda i: (i, 0))
    return pl.pallas_call(
        body, name="loss_head", grid=(S // tm,),
        out_shape=[jax.ShapeDtypeStruct((S, D), f32), jax.ShapeDtypeStruct((1, D), f32), jax.ShapeDtypeStruct((8, 128), f32)],
        in_specs=[tok, _resident((1, D), lambda i: (0, 0)), tok],
        out_specs=[tok, pl.BlockSpec((1, D), lambda i: (0, 0)), pl.BlockSpec((8, 128), lambda i: (0, 0))],
        compiler_params=_params("arbitrary"),
    )(x3, fn, target)


def _ffn_bwd_act(name, dy, x, g, u, nrm, wfull, wg, wu, wd, tm=256):
    S = x.shape[0]

    def body(dy_ref, x_ref, g_ref, u_ref, n_ref, wg_ref, wu_ref, wd_ref,
             dx_ref, dg_ref, du_ref, a_ref, h_ref, dyh_ref, dn_ref, acc_ref):
        n = n_ref[...]
        dyv = dy_ref[...]
        r, xh, h = _rms_fwd(x_ref[...], n)
        h_ref[...] = h.astype(CDT)
        dyh = (0.5 * dyv).astype(CDT)
        dyh_ref[...] = dyh
        for ci in range(F // FT):
            sl = slice(ci * FT, (ci + 1) * FT)
            da = _mm_nt(dyh, wd_ref[sl, :])
            gv = g_ref[:, sl].astype(f32)
            uv = u_ref[:, sl].astype(f32)
            s = jax.nn.sigmoid(gv)
            silu = gv * s
            a_ref[:, sl] = (silu * uv).astype(CDT)
            du = (da * silu).astype(CDT)
            dg = (da * uv * (s * (1.0 + gv * (1.0 - s)))).astype(CDT)
            du_ref[:, sl] = du
            dg_ref[:, sl] = dg
            dh = _mm(dg, wg_ref[sl, :]) + _mm(du, wu_ref[sl, :])
            if ci == 0:
                acc_ref[...] = dh
            else:
                acc_ref[...] += dh
        dx, dn = _rms_bwd(acc_ref[...], r, xh, n)
        dx_ref[...] = dyv + dx

        @pl.when(pl.program_id(0) == 0)
        def _():
            dn_ref[...] = jnp.zeros_like(dn_ref)

        dn_ref[...] += dn

    tok = pl.BlockSpec((tm, D), lambda i: (i, 0))
    hid = pl.BlockSpec((tm, F), lambda i: (i, 0))
    return pl.pallas_call(
        body, name=name, grid=(S // tm,),
        out_shape=[jax.ShapeDtypeStruct((S, D), f32), jax.ShapeDtypeStruct((S, F), CDT), jax.ShapeDtypeStruct((S, F), CDT),
                   jax.ShapeDtypeStruct((S, F), CDT), jax.ShapeDtypeStruct((S, D), CDT), jax.ShapeDtypeStruct((S, D), CDT),
                   jax.ShapeDtypeStruct((1, D), f32)],
        in_specs=[tok, tok, hid, hid, _resident((1, D), lambda i: (0, 0)), _wblock(wg), _wblock(wu), _wblock(wd)],
        out_specs=[tok, hid, hid, hid, tok, tok, pl.BlockSpec((1, D), lambda i: (0, 0))],
        scratch_shapes=[pltpu.VMEM((tm, D), f32)],
        compiler_params=_params("arbitrary"),
    )(dy, x, g, u, nrm, wfull, wfull, wfull)


TN_ROWS = 256


def _tn_into(name, gfull, xs, ys, row_block_of):
    S, M = xs.shape
    B = ys.shape[0]

    def body(*refs):
        x_ref, y_ref, o_ref = refs[0], refs[1], refs[-1]
        o_ref[...] = _mm_tn(x_ref[...], y_ref[...]).astype(CDT)

    in_specs = [pl.BlockSpec((S, TN_ROWS), lambda b, i: (0, i)), pl.BlockSpec((None, S, D), lambda b, i: (b, 0, 0))]
    operands = [xs, ys]
    aliases = {}
    if gfull is not None:
        in_specs.append(_HBM)
        operands.append(gfull)
        aliases = {2: 0}
    return pl.pallas_call(
        body, name=name, grid=(B, M // TN_ROWS),
        out_shape=jax.ShapeDtypeStruct((FULL_ROWS, D), CDT),
        in_specs=in_specs,
        out_specs=pl.BlockSpec((TN_ROWS, D), lambda b, i: (row_block_of(b, i), 0)),
        input_output_aliases=aliases,
        compiler_params=_params("arbitrary", "arbitrary"),
    )(*operands)


def _wgrad(name, gfull, w, xs, y):
    base = W_BASE[w] // TN_ROWS
    return _tn_into(name, gfull, xs, y[None], lambda b, i: base + i)


def _merge_bwd_act(dx2, ya, yb, proj, wfull, tm=512):
    S = dx2.shape[0]

    def body(dx_ref, ya_ref, yb_ref, gt_ref, wa_ref, wb_ref, wo_ref,
             dp_ref, da_ref, dr_ref, mix_ref, dxb_ref, dya_ref, dyb_ref):
        dxb = dx_ref[...].astype(CDT)
        dxb_ref[...] = dxb
        dmix = _mm_nt(dxb, wo_ref[...])
        ya = ya_ref[...].astype(f32)
        yb = yb_ref[...].astype(f32)
        sa = jax.nn.sigmoid(gt_ref[0].astype(f32))
        sb = jax.nn.sigmoid(gt_ref[1].astype(f32))
        mix_ref[...] = (sa * ya + sb * yb).astype(CDT)
        dya = (dmix * sa).astype(CDT)
        dyb = (dmix * sb).astype(CDT)
        dya_ref[...] = dya
        dyb_ref[...] = dyb
        dp_ref[0] = (dmix * ya * sa * (1.0 - sa)).astype(CDT)
        dp_ref[1] = (dmix * yb * sb * (1.0 - sb)).astype(CDT)
        da_ref[...] = _mm_nt(dya, wa_ref[...]).astype(CDT)
        dr_ref[...] = _mm_nt(dyb, wb_ref[...]).astype(CDT)

    tok = pl.BlockSpec((tm, D), lambda i: (i, 0))
    gates = pl.BlockSpec((2, tm, D), lambda i: (SLOT_GA // 2, i, 0))
    act = jax.ShapeDtypeStruct((S, D), CDT)
    return pl.pallas_call(
        body, name="merge_bwd_act", grid=(S // tm,),
        out_shape=[jax.ShapeDtypeStruct((8, S, D), CDT), act, act, act, act, act, act],
        in_specs=[tok, tok, tok, gates, _wblock("w_branch_a"), _wblock("w_branch_b"), _wblock("w_out")],
        out_specs=[gates, tok, tok, tok, tok, tok, tok],
        compiler_params=_params("arbitrary"),
    )(dx2, ya, yb, proj, wfull, wfull, wfull)


def _sgu_bwd(da, proj, dproj, gn, bn, ws, wst, bsc, tm=512):
    S = proj.shape[1]
    GW = D // G

    def body(da_ref, p_ref, dpin_ref, gn_ref, bn_ref, ws_ref, wst_ref, bs_ref,
             dp_ref, dws_ref, dbs_ref, dgn_ref, dbn_ref, ds_ref, dvn_ref):
        @pl.when(pl.program_id(0) == 0)
        def _():
            dws_ref[...] = jnp.zeros_like(dws_ref)
            dbs_ref[...] = jnp.zeros_like(dbs_ref)
            dgn_ref[...] = jnp.zeros_like(dgn_ref)
            dbn_ref[...] = jnp.zeros_like(dbn_ref)

        pu = p_ref[0].astype(f32)
        pv = p_ref[1].astype(f32)
        ua = _gelu(pu)
        va = _gelu(pv)
        gn = gn_ref[...]
        rstd, vhat, vn = _sgu_norm(va, gn, bn_ref[...])
        vnb = vn.astype(CDT)
        dav = da_ref[...].astype(f32)
        dsb = (dav * ua).astype(CDT)
        ones = jnp.ones((8, GW), CDT)
        for ch in range(tm // C):
            rs = slice(ch * C, (ch + 1) * C)
            for gi in range(G):
                cs = slice(gi * GW, (gi + 1) * GW)
                s = _mm(ws_ref[gi], vnb[rs, cs]) + bs_ref[gi]
                ds_ref[rs, cs] = s
                dsg = dsb[rs, cs]
                dws_ref[gi] += _mm_nt(dsg, vnb[rs, cs])
                dbs_ref[gi] += _mm_nt(ones, dsg)
                dvn_ref[rs, cs] = _mm(wst_ref[gi], dsg)
        dp_ref[0] = (dav * ds_ref[...] * _gelu_grad(pu)).astype(CDT)
        dvn = dvn_ref[...]
        dgn_ref[...] += jnp.sum(dvn * vhat, axis=0, keepdims=True)
        dbn_ref[...] += jnp.sum(dvn, axis=0, keepdims=True)
        dvh = dvn * gn
        dva = rstd * (dvh - jnp.mean(dvh, axis=-1, keepdims=True) - vhat * jnp.mean(dvh * vhat, axis=-1, keepdims=True))
        dp_ref[1] = (dva * _gelu_grad(pv)).astype(CDT)

    uv = pl.BlockSpec((2, tm, D), lambda i: (0, i, 0))
    row = _resident((1, D), lambda i: (0, 0))
    return pl.pallas_call(
        body, name="sgu_bwd", grid=(S // tm,),
        out_shape=[jax.ShapeDtypeStruct(dproj.shape, CDT), jax.ShapeDtypeStruct((G, C, C), f32),
                   jax.ShapeDtypeStruct((G, 8, C), f32), jax.ShapeDtypeStruct((1, D), f32), jax.ShapeDtypeStruct((1, D), f32)],
        in_specs=[pl.BlockSpec((tm, D), lambda i: (i, 0)), uv, _HBM, row, row,
                  _resident((G, C, C), lambda i: (0, 0, 0)), _resident((G, C, C), lambda i: (0, 0, 0)),
                  _resident((G, C, 1), lambda i: (0, 0, 0))],
        out_specs=[uv, pl.BlockSpec((G, C, C), lambda i: (0, 0, 0)), pl.BlockSpec((G, 8, C), lambda i: (0, 0, 0)),
                   pl.BlockSpec((1, D), lambda i: (0, 0)), pl.BlockSpec((1, D), lambda i: (0, 0))],
        scratch_shapes=[pltpu.VMEM((tm, D), f32), pltpu.VMEM((tm, D), f32)],
        input_output_aliases={2: 0},
        compiler_params=_params("arbitrary"),
    )(da, proj, dproj, gn, bn, ws, wst, bsc)


def _ret_bwd(dr, R, proj, dproj, cos, sin, dl):
    S = proj.shape[1]
    NC = S // C

    def body(dr_ref, R_ref, q_ref, k_ref, v_ref, g_ref, dpin_ref, cos_ref, sin_ref, dl_ref,
             dp_ref, dd_ref, qs_ref, ks_ref, dR_ref, sts_ref, dk_ref, dv_ref, st_ref, gst_ref, acc_ref):
        sb_ref = sf_ref = sts_ref
        t = _decay_tables(dl_ref)
        _rotate_into(q_ref, k_ref, cos_ref, sin_ref, qs_ref, ks_ref, S)
        for rt in range(S // ROT_ROWS):
            rs = slice(rt * ROT_ROWS, (rt + 1) * ROT_ROWS)
            Rv = R_ref[rs, :]
            rstd = lax.rsqrt(jnp.mean(Rv * Rv, axis=-1, keepdims=True) + EPS)
            rn = Rv * rstd
            gv = g_ref[rs, :].astype(f32)
            s = jax.nn.sigmoid(gv)
            drv = dr_ref[rs, :].astype(f32)
            dp_ref[3, rs, :] = (drv * rn * (s * (1.0 + gv * (1.0 - s)))).astype(CDT)
            drn = drv * gv * s
            dR_ref[rs, :] = (rstd * (drn - rn * jnp.mean(drn * rn, axis=-1, keepdims=True))).astype(CDT)

        def chunk(n):
            rows = pl.ds(pl.multiple_of(n * C, C), C)
            return rows, qs_ref[rows, :], ks_ref[rows, :], v_ref[rows, :], dR_ref[rows, :]

        st_ref[...] = jnp.zeros_like(st_ref)

        def pass_a(i, carry):
            n = NC - 1 - i
            _, _, kn, vn, _ = chunk(n)
            sb_ref[n] = st_ref[...].astype(CDT)
            st_ref[...] = st_ref[...] * t["lamb"] + _mm_tn((kn.astype(f32) * t["bk"]).astype(CDT), vn)
            return carry

        lax.fori_loop(0, NC, pass_a, 0)

        st_ref[...] = jnp.zeros_like(st_ref)
        gst_ref[...] = jnp.zeros_like(gst_ref)
        acc_ref[...] = jnp.zeros_like(acc_ref)
        dpos = jnp.abs(t["d"])

        def pass_b(n, carry):
            rows, qn, kn, vn, dRn = chunk(n)
            qf, kf = qn.astype(f32), kn.astype(f32)
            sc = _mm_nt(qn, kn)
            dA = _mm_nt(dRn, vn)
            w = sc * dA * t["dmat"] * dpos
            lgf_part = jnp.sum(jnp.where(t["lower"], w, 0.0), axis=0, keepdims=True)
            lgb_part = jnp.sum(jnp.where(t["lower"], 0.0, w), axis=0, keepdims=True)
            dsc = (dA * t["dmat"]).astype(CDT)
            dq = _mm(dsc, kn)
            scT = (_mm_nt(kn, qn) * t["dmat_t"]).astype(CDT)
            dscT = (_mm_nt(vn, dRn) * t["dmat_t"]).astype(CDT)
            dk = _mm(dscT, qn)
            dv = _mm(scT, dRn)
            sf = st_ref[...]
            sfb = sf.astype(CDT)
            sbb = sb_ref[n]
            sf_ref[n] = sfb
            qdf = qf * t["fq"]
            dqdf = _mm_nt(dRn, sfb)
            dq += dqdf * t["fq"]
            lgf_row = jnp.sum(qdf * dqdf * (t["pos"] + 1.0), axis=0, keepdims=True)
            qdb = qf * t["bq"]
            dqdb = _mm_nt(dRn, sbb)
            dq += dqdb * t["bq"]
            lgb_row = jnp.sum(qdb * dqdb * (C - t["pos"]), axis=0, keepdims=True)
            gb = gst_ref[...]
            gbb = gb.astype(CDT)
            kdb = kf * t["bk"]
            dkdb = _mm_nt(vn, gbb)
            dk += dkdb * t["bk"]
            dv += _mm(kdb.astype(CDT), gbb)
            lgb_row += jnp.sum(kdb * dkdb * t["pos"], axis=0, keepdims=True)
            lgb_row += float(C) * t["lamb"] * jnp.sum(gb * sbb.astype(f32), axis=0, keepdims=True)
            co, si = cos_ref[rows, :], sin_ref[rows, :]
            dp_ref[0, rows, :] = _unrotate(dq, co, si).astype(CDT)
            dk_ref[rows, :] = dk
            dv_ref[rows, :] = dv
            acc_ref[0:1, :] += lgf_row + jnp.concatenate([lgf_part, jnp.zeros_like(lgf_part)], axis=1)
            acc_ref[1:2, :] += lgb_row + jnp.concatenate([lgb_part, jnp.zeros_like(lgb_part)], axis=1)
            st_ref[...] = sf * t["lamf"] + _mm_tn((kf * t["fk"]).astype(CDT), vn)
            gst_ref[...] = gb * t["lamb"] + _mm_tn(qdb.astype(CDT), dRn)
            return carry

        lax.fori_loop(0, NC, pass_b, 0)

        gst_ref[...] = jnp.zeros_like(gst_ref)

        def pass_c(i, carry):
            n = NC - 1 - i
            rows, qn, kn, vn, dRn = chunk(n)
            gf = gst_ref[...]
            gfb = gf.astype(CDT)
            kdf = kn.astype(f32) * t["fk"]
            dkdf = _mm_nt(vn, gfb)
            dk = dk_ref[rows, :] + dkdf * t["fk"]
            dv = dv_ref[rows, :] + _mm(kdf.astype(CDT), gfb)
            lgf_row = jnp.sum(kdf * dkdf * (C - 1.0 - t["pos"]), axis=0, keepdims=True)
            lgf_row += float(C) * t["lamf"] * jnp.sum(gf * sf_ref[n].astype(f32), axis=0, keepdims=True)
            acc_ref[0:1, :] += lgf_row
            co, si = cos_ref[rows, :], sin_ref[rows, :]
            dp_ref[1, rows, :] = (_unrotate(dk, co, si) * K_SCALE).astype(CDT)
            dp_ref[2, rows, :] = dv.astype(CDT)
            gst_ref[...] = gf * t["lamf"] + _mm_tn((qn.astype(f32) * t["fq"]).astype(CDT), dRn)
            return carry

        lax.fori_loop(0, NC, pass_c, 0)
        dlg = jnp.sum(acc_ref[...], axis=1, keepdims=True)
        dlogit = dlg * jax.nn.sigmoid(-dl_ref[:, 0:1])
        lane = lax.broadcasted_iota(jnp.int32, (8, 128), 1)
        dd_ref[...] = jnp.where(lane == pl.program_id(0), jnp.broadcast_to(dlogit, (8, 128)), 0.0)

    def seg(slot):
        return pl.BlockSpec((None, S, DK), lambda h: (slot, 0, h), pipeline_mode=pl.Buffered(1))

    head = pl.BlockSpec((S, DK), lambda h: (0, h), pipeline_mode=pl.Buffered(1))
    return pl.pallas_call(
        body, name="ret_bwd", grid=(H,),
        out_shape=[jax.ShapeDtypeStruct(dproj.shape, CDT), jax.ShapeDtypeStruct((H, 8, 128), f32)],
        in_specs=[head, head, seg(SLOT_Q), seg(SLOT_K), seg(SLOT_VR), seg(SLOT_GR), _HBM,
                  _resident((S, DK // 2), lambda h: (0, 0)), _resident((S, DK // 2), lambda h: (0, 0)),
                  pl.BlockSpec((None, 8, DK), lambda h: (h, 0, 0))],
        out_specs=[pl.BlockSpec((4, S, DK), lambda h: (1, 0, h)), pl.BlockSpec((None, 8, 128), lambda h: (h, 0, 0))],
        scratch_shapes=[pltpu.VMEM((S, DK), CDT), pltpu.VMEM((S, DK), CDT), pltpu.VMEM((S, DK), CDT),
                        pltpu.VMEM((NC, DK, DK), CDT),
                        pltpu.VMEM((S, DK), f32), pltpu.VMEM((S, DK), f32),
                        pltpu.VMEM((DK, DK), f32), pltpu.VMEM((DK, DK), f32), pltpu.VMEM((8, DK), f32)],
        input_output_aliases={6: 0},
        compiler_params=_params("arbitrary"),
    )(dr, R, proj, proj, proj, proj, dproj, cos, sin, dl)


def _proj_bwd_act(dproj, dx2, x1, nrm, wfull, tm=512):
    S = x1.shape[0]
    wbase = W_BASE["w_in"] // D

    def body(dp_ref, dx2_ref, x_ref, n_ref, w_ref, dx_ref, dn_ref, db_ref, acc_ref):
        i, p = pl.program_id(0), pl.program_id(1)

        @pl.when((i == 0) & (p == 0))
        def _():
            dn_ref[...] = jnp.zeros_like(dn_ref)
            db_ref[...] = jnp.zeros_like(db_ref)

        dp = dp_ref[...]
        db_ref[_seg_of_slot(p)] += jnp.sum(dp.astype(f32), axis=0, keepdims=True)
        dh = _mm_nt(dp, w_ref[...])

        @pl.when(p == 0)
        def _():
            acc_ref[...] = dh

        @pl.when(p > 0)
        def _():
            acc_ref[...] += dh

        @pl.when(p == 7)
        def _():
            n = n_ref[...]
            r, xh, _ = _rms_fwd(x_ref[...], n)
            dx, dn = _rms_bwd(acc_ref[...], r, xh, n)
            dx_ref[...] = dx2_ref[...] + dx
            dn_ref[...] += dn

    tok = pl.BlockSpec((tm, D), lambda i, p: (i, 0))
    return pl.pallas_call(
        body, name="proj_bwd_act", grid=(S // tm, 8),
        out_shape=[jax.ShapeDtypeStruct((S, D), f32), jax.ShapeDtypeStruct((1, D), f32), jax.ShapeDtypeStruct((8, 1, D), f32)],
        in_specs=[pl.BlockSpec((None, tm, D), lambda i, p: (p, i, 0)), tok, tok, _resident((1, D), lambda i, p: (0, 0)),
                  pl.BlockSpec((D, D), lambda i, p: (wbase + _seg_of_slot(p), 0))],
        out_specs=[tok, pl.BlockSpec((1, D), lambda i, p: (0, 0)), pl.BlockSpec((8, 1, D), lambda i, p: (0, 0, 0))],
        scratch_shapes=[pltpu.VMEM((tm, D), f32)],
        compiler_params=_params("arbitrary", "arbitrary"),
    )(dproj, dx2, x1, nrm, wfull)


def _rs_sum(name, part, gfull, land, my_c, w_first, n_w):
    rows = W_ROWS[w_first]
    gb = W_BASE[w_first] // rows
    lb = W_OFF[w_first] // rows
    assert W_BASE[w_first] % rows == 0 and W_OFF[w_first] % rows == 0

    def body(c_ref, g_ref, l_ref, *rest):
        o_ref = rest[-1]
        o_ref[...] = (g_ref[...].astype(f32) + l_ref[...].astype(f32)).astype(CDT)

    in_specs = [pl.BlockSpec((rows, D), lambda wi, k, c: (gb + NDEV * wi + 2 * k + c[0], 0)),
                pl.BlockSpec((None, rows, D), lambda wi, k, c: (k, lb + wi, 0))]
    operands = [gfull, land]
    aliases = {}
    if part is not None:
        in_specs.append(_HBM)
        operands.append(part)
        aliases = {3: 0}
    return pl.pallas_call(
        body, name=name,
        grid_spec=pltpu.PrefetchScalarGridSpec(
            num_scalar_prefetch=1, grid=(n_w, NCHIP), in_specs=in_specs,
            out_specs=pl.BlockSpec((None, rows, D), lambda wi, k, c: (k, lb + wi, 0))),
        out_shape=jax.ShapeDtypeStruct((NCHIP, SHARD_ROWS, D), CDT),
        input_output_aliases=aliases,
        compiler_params=_params("arbitrary", "arbitrary"),
    )(my_c, *operands)


def _adamw_math(g, w, m, v):
    m2 = ADAM_B1 * m + (1.0 - ADAM_B1) * g
    v2 = ADAM_B2 * v + (1.0 - ADAM_B2) * (g * g)
    delta = -ADAM_LR * ((m2 / BC1) / (jnp.sqrt(v2 / BC2) + ADAM_EPS) + ADAM_WD * w)
    return delta, m2, v2


def _adamw_big(name, landed, wname, w, m, v):
    rows = W_ROWS[wname]
    tr = min(rows, 256) if rows % 256 == 0 else rows
    lb = W_OFF[wname] // tr
    assert W_OFF[wname] % tr == 0

    def body(l_ref, w_ref, m_ref, v_ref, g_ref, d_ref, m2_ref, v2_ref):
        g = l_ref[0].astype(f32)
        for k in range(1, NCHIP):
            g = g + l_ref[k].astype(f32)
        g_ref[...] = g
        d_ref[...], m2_ref[...], v2_ref[...] = _adamw_math(g, w_ref[...], m_ref[...], v_ref[...])

    blk = pl.BlockSpec((tr, D), lambda i: (i, 0))
    o = jax.ShapeDtypeStruct((rows, D), f32)
    return pl.pallas_call(
        body, name=name, grid=(rows // tr,), out_shape=[o, o, o, o],
        in_specs=[pl.BlockSpec((NCHIP, tr, D), lambda i: (0, lb + i, 0)), blk, blk, blk],
        out_specs=[blk, blk, blk, blk],
        compiler_params=_params("arbitrary"),
    )(landed, w, m, v)


ROW_FFN1_NORM, ROW_MIX_NORM, ROW_SGU_G, ROW_SGU_B, ROW_FFN2_NORM, ROW_FINAL_NORM, ROW_B_IN = 0, 1, 2, 3, 4, 5, 8
ROW_WS, ROW_BS, ROW_DECAY = 0, G * C, G * C + G * 8


def _adamw_small(ga, gb, params):
    def body(ga_ref, gb_ref, *refs):
        ins, outs = refs[:30], refs[30:]

        def total(ref, r0, n):
            g = ref[0, r0:r0 + n, :]
            for j in range(1, NDEV):
                g = g + ref[j, r0:r0 + n, :]
            return g

        def apply(i, g, rows=slice(None)):
            w, m, v = ins[3 * i][rows, :], ins[3 * i + 1][rows, :], ins[3 * i + 2][rows, :]
            outs[4 * i][rows, :] = g
            outs[4 * i + 1][rows, :], outs[4 * i + 2][rows, :], outs[4 * i + 3][rows, :] = _adamw_math(g, w, m, v)

        for i, r in enumerate((ROW_FFN1_NORM, ROW_MIX_NORM, ROW_SGU_G, ROW_SGU_B, ROW_FFN2_NORM, ROW_FINAL_NORM)):
            apply(i, total(ga_ref, r, 1))
        apply(6, total(ga_ref, ROW_B_IN, 8))
        apply(7, total(gb_ref, ROW_WS, G * C))
        for gi in range(G):
            apply(8, total(gb_ref, ROW_BS + 8 * gi, 1), slice(gi, gi + 1))
        dec = total(gb_ref, ROW_DECAY, 8)
        for hh in range(1, H):
            dec = dec + total(gb_ref, ROW_DECAY + 8 * hh, 8)
        apply(9, dec)

    flat = [a for p in params for a in p]
    out_shape = [jax.ShapeDtypeStruct(p[0].shape, f32) for p in params for _ in range(4)]
    vm = pl.BlockSpec(memory_space=pltpu.VMEM)
    return pl.pallas_call(
        body, name="adamw_small", out_shape=out_shape,
        in_specs=[vm] * (2 + len(flat)), out_specs=[vm] * len(out_shape),
        compiler_params=pltpu.CompilerParams(vmem_limit_bytes=VMEM_LIMIT),
    )(ga, gb, *flat)


def kernel(x, ffn1_norm, ffn1_w_gate, ffn1_w_up, ffn1_w_down, mix_norm, w_in, b_in, sgu_norm_g, sgu_norm_b, sgu_w_s, sgu_b_s, ret_decay_logit, w_branch_a, w_branch_b, w_out, ffn2_norm, ffn2_w_gate, ffn2_w_up, ffn2_w_down, final_norm, loss_target, m_ffn1_norm, m_ffn1_w_gate, m_ffn1_w_up, m_ffn1_w_down, m_mix_norm, m_w_in, m_b_in, m_sgu_norm_g, m_sgu_norm_b, m_sgu_w_s, m_sgu_b_s, m_ret_decay_logit, m_w_branch_a, m_w_branch_b, m_w_out, m_ffn2_norm, m_ffn2_w_gate, m_ffn2_w_up, m_ffn2_w_down, m_final_norm, v_ffn1_norm, v_ffn1_w_gate, v_ffn1_w_up, v_ffn1_w_down, v_mix_norm, v_w_in, v_b_in, v_sgu_norm_g, v_sgu_norm_b, v_sgu_w_s, v_sgu_b_s, v_ret_decay_logit, v_w_branch_a, v_w_branch_b, v_w_out, v_ffn2_norm, v_ffn2_w_gate, v_ffn2_w_up, v_ffn2_w_down, v_final_norm):
    args = dict(locals())
    S = x.shape[1]
    xs = x[0]
    target = loss_target[0]

    def buf_layout(name, a):
        a = a[0]
        return a.T if name in W_TRANSPOSED else a

    shard = jnp.concatenate([buf_layout(n, args[n]).astype(CDT) for n in W_NAMES], axis=0)
    wfull = _all_gather_weights(shard)

    b3 = b_in.reshape(8, 1, D)
    ws = sgu_w_s[0].astype(CDT)
    wst = jnp.swapaxes(sgu_w_s[0], 1, 2).astype(CDT)
    bsc = sgu_b_s[0].reshape(G, C, 1)
    dl = jnp.zeros((H, 8, DK), f32).at[:, 0:2, :].set(jnp.broadcast_to(ret_decay_logit[0].T[:, :, None], (H, 2, DK)))
    theta = ROPE_BASE ** (-jnp.arange(0, DK, 2, dtype=f32) / DK)
    ang = jnp.arange(S, dtype=f32)[:, None] * theta[None, :]
    cos, sin = jnp.cos(ang), jnp.sin(ang)
    fnorm = final_norm.reshape(1, D)

    x1, g1, u1 = _ffn_fwd("ffn1_fwd", xs, ffn1_norm, wfull, "ffn1_w_gate", "ffn1_w_up", "ffn1_w_down")
    proj, h2 = _proj_fwd(x1, mix_norm, wfull, b3)
    a = _sgu_fwd(proj, sgu_norm_g, sgu_norm_b, ws, bsc)
    R, r = _ret_fwd(proj, cos, sin, dl)
    x2, ya, yb = _merge_fwd(a, r, proj, x1, wfull)
    x3, g2, u2 = _ffn_fwd("ffn2_fwd", x2, ffn2_norm, wfull, "ffn2_w_gate", "ffn2_w_up", "ffn2_w_down")
    dx3, d_final, loss_part = _loss_head(x3, fnorm, target)
    loss = lax.psum(loss_part[0, 0], ("x", "y", "c"))

    dx2, dg2, du2, a2, hf2, dyh2, d_ffn2n = _ffn_bwd_act("ffn2_bwd_act", dx3, x2, g2, u2, ffn2_norm, wfull,
                                                         "ffn2_w_gate", "ffn2_w_up", "ffn2_w_down")
    gfull = _wgrad("ffn2_wd_grad", None, "ffn2_w_down", a2, dyh2)
    gfull = _wgrad("ffn2_wg_grad", gfull, "ffn2_w_gate", dg2, hf2)
    gfull = _wgrad("ffn2_wu_grad", gfull, "ffn2_w_up", du2, hf2)
    dproj, da, dr, mix, dx2b, dya, dyb = _merge_bwd_act(dx2, ya, yb, proj, wfull)
    gfull = _wgrad("wo_grad", gfull, "w_out", mix, dx2b)
    gfull = _wgrad("wa_grad", gfull, "w_branch_a", a, dya)
    gfull = _wgrad("wb_grad", gfull, "w_branch_b", r, dyb)
    dproj, d_ws, d_bs, d_gn, d_bn = _sgu_bwd(da, proj, dproj, sgu_norm_g, sgu_norm_b, ws, wst, bsc)
    dproj, d_dec = _ret_bwd(dr, R, proj, dproj, cos, sin, dl)
    win_base = W_BASE["w_in"] // TN_ROWS
    gfull = _tn_into("win_grad", gfull, h2, dproj, lambda b, i: win_base + _seg_of_slot(b) * (D // TN_ROWS) + i)
    dx1, d_mixn, d_bin = _proj_bwd_act(dproj, dx2, x1, mix_norm, wfull)
    dxs, dg1, du1, a1, hf1, dyh1, d_ffn1n = _ffn_bwd_act("ffn1_bwd_act", dx1, xs, g1, u1, ffn1_norm, wfull,
                                                         "ffn1_w_gate", "ffn1_w_up", "ffn1_w_down")
    gfull = _wgrad("ffn1_wd_grad", gfull, "ffn1_w_down", a1, dyh1)
    gfull = _wgrad("ffn1_wg_grad", gfull, "ffn1_w_gate", dg1, hf1)
    gfull = _wgrad("ffn1_wu_grad", gfull, "ffn1_w_up", du1, hf1)

    my_c = lax.axis_index("c").astype(jnp.int32).reshape(1)
    from_sib = _rs_d2d(gfull)
    part = _rs_sum("rs_sum_in", None, gfull, from_sib, my_c, "w_in", 1)
    part = _rs_sum("rs_sum_br", part, gfull, from_sib, my_c, "w_branch_a", 3)
    part = _rs_sum("rs_sum_ffn", part, gfull, from_sib, my_c, "ffn1_w_gate", 6)
    landed = _rs_ici(part)

    out = {"loss": loss, "grad_x": dxs[None]}

    def native(name, a):
        a = a.T if name in W_TRANSPOSED else a
        return a[None]

    for n in W_NAMES:
        res = _adamw_big("adamw_" + n, landed, n, buf_layout(n, args[n]), buf_layout(n, args["m_" + n]),
                         buf_layout(n, args["v_" + n]))
        for pre, val in zip(("grad_", "delta_", "new_m_", "new_v_"), res):
            out[pre + n] = native(n, val)

    small_a = jnp.concatenate([d_ffn1n, d_mixn, d_gn, d_bn, d_ffn2n, d_final, jnp.zeros((2, D), f32), d_bin.reshape(8, D)], axis=0)
    small_b = jnp.concatenate([d_ws.reshape(G * C, C), d_bs.reshape(G * 8, C), d_dec.reshape(H * 8, 128)], axis=0)
    ga, gb = _all_gather_small(small_a, small_b)

    def pad_decay(a):
        return jnp.zeros((8, 128), f32).at[0:2, 0:H].set(a[0])

    small = [
        ("ffn1_norm", lambda a: a, lambda a: a), ("mix_norm", lambda a: a, lambda a: a),
        ("sgu_norm_g", lambda a: a, lambda a: a), ("sgu_norm_b", lambda a: a, lambda a: a),
        ("ffn2_norm", lambda a: a, lambda a: a),
        ("final_norm", lambda a: a.reshape(1, D), lambda a: a.reshape(D)),
        ("b_in", lambda a: a.reshape(8, D), lambda a: a.reshape(1, 8 * D)),
        ("sgu_w_s", lambda a: a.reshape(G * C, C), lambda a: a.reshape(1, G, C, C)),
        ("sgu_b_s", lambda a: a[0], lambda a: a[None]),
        ("ret_decay_logit", pad_decay, lambda a: a[None, 0:2, 0:H]),
    ]
    res = _adamw_small(ga, gb, [(to(args[n]), to(args["m_" + n]), to(args["v_" + n])) for n, to, _ in small])
    for i, (n, _, back) in enumerate(small):
        for j, pre in enumerate(("grad_", "delta_", "new_m_", "new_v_")):
            out[pre + n] = back(res[4 * i + j])

    weights = ("ffn1_norm", "ffn1_w_gate", "ffn1_w_up", "ffn1_w_down", "mix_norm", "w_in", "b_in", "sgu_norm_g",
               "sgu_norm_b", "sgu_w_s", "sgu_b_s", "ret_decay_logit", "w_branch_a", "w_branch_b", "w_out", "ffn2_norm",
               "ffn2_w_gate", "ffn2_w_up", "ffn2_w_down", "final_norm")
    return (out["loss"], out["grad_x"], *[out["grad_" + n] for n in weights], *[out["delta_" + n] for n in weights],
            *[out["new_m_" + n] for n in weights], *[out["new_v_" + n] for n in weights])
```

```python
import functools
import math

import jax
import jax.numpy as jnp
from jax import lax
from jax.experimental import pallas as pl
from jax.experimental.pallas import tpu as pltpu

f32 = jnp.float32
CDT = jnp.bfloat16

D = 1024
F = 2816
C = 128
H = 4
DK = 256
G = 4
NDEV = 8
NCHIP = 4
EPS = 1e-6
ROPE_BASE = 10000.0
FT = 256
V7X_VMEM_BYTES = 64 * 1024 * 1024
VMEM_LIMIT = V7X_VMEM_BYTES - 8 * 1024 * 1024

ADAM_LR, ADAM_B1, ADAM_B2, ADAM_EPS, ADAM_WD, ADAM_STEP = 0.001, 0.9, 0.999, 1e-08, 0.01, 10
BC1 = 1.0 - ADAM_B1 ** ADAM_STEP
BC2 = 1.0 - ADAM_B2 ** ADAM_STEP

GROUPS = dict(ffn1=("ffn1_w_gate", "ffn1_w_up", "ffn1_w_down"), win=("w_in",),
              br=("w_branch_a", "w_branch_b", "w_out"), ffn2=("ffn2_w_gate", "ffn2_w_up", "ffn2_w_down"))
W_NAMES = tuple(n for g in GROUPS.values() for n in g)
W_ROWS = dict(w_in=1024, w_branch_a=128, w_branch_b=128, w_out=128,
              ffn1_w_gate=352, ffn1_w_up=352, ffn1_w_down=352, ffn2_w_gate=352, ffn2_w_up=352, ffn2_w_down=352)
W_TRANSPOSED = ("ffn1_w_gate", "ffn1_w_up", "ffn2_w_gate", "ffn2_w_up")
W_OFF = {}
W_BASE = {}
G_ROWS = {}
for _g, _names in GROUPS.items():
    _o = 0
    for _n in _names:
        W_OFF[_n] = _o
        W_BASE[_n] = NDEV * _o
        _o += W_ROWS[_n]
    G_ROWS[_g] = _o

SLOT_U, SLOT_V, SLOT_GA, SLOT_GB, SLOT_Q, SLOT_K, SLOT_VR, SLOT_GR = range(8)


def _seg_of_slot(p):
    return jnp.where(p < 2, p, jnp.where(p < 4, p + 4, p - 2))


def _mm(a, b):
    return jnp.dot(a, b, preferred_element_type=f32)


def _mm_nt(a, b):
    return lax.dot_general(a, b, (((1,), (1,)), ((), ())), preferred_element_type=f32)


def _mm_tn(a, b):
    return lax.dot_general(a, b, (((0,), (0,)), ((), ())), preferred_element_type=f32)


def _params(*sem):
    return pltpu.CompilerParams(dimension_semantics=sem, vmem_limit_bytes=VMEM_LIMIT)


def _resident(shape, index_map):
    return pl.BlockSpec(shape, index_map, pipeline_mode=pl.Buffered(1))


def _gelu(x):
    return 0.5 * x * (1.0 + lax.erf(x * (1.0 / math.sqrt(2.0))))


def _gelu_grad(x):
    return 0.5 * (1.0 + lax.erf(x * (1.0 / math.sqrt(2.0)))) + x * jnp.exp(-0.5 * x * x) * (1.0 / math.sqrt(2.0 * math.pi))


def _rms_fwd(x, n):
    r = lax.rsqrt(jnp.mean(x * x, axis=-1, keepdims=True) + EPS)
    xh = x * r
    return r, xh, xh * n


def _rms_bwd(dh, r, xh, n):
    dxh = dh * n
    dx = r * (dxh - xh * jnp.mean(dxh * xh, axis=-1, keepdims=True))
    return dx, jnp.sum(dh * xh, axis=0, keepdims=True)


MESH_ID = pl.DeviceIdType.MESH
_HBM = pl.BlockSpec(memory_space=pltpu.HBM)


def _my_place():
    return lax.axis_index("x"), lax.axis_index("y"), lax.axis_index("c")


def _ici_peers(x, y, c):
    return [((1 - x, y, c), 2 * (1 - x) + y), ((x, 1 - y, c), 2 * x + 1 - y), ((1 - x, 1 - y, c), 2 * (1 - x) + 1 - y)]


class _Hook:
    def __init__(self, operands, out_shapes, n_remote, n_local, start, finish):
        self.operands, self.out_shapes = list(operands), list(out_shapes)
        self.n_remote, self.n_local, self.start, self.finish = n_remote, n_local, start, finish


def _call(body, hooks, operands, *, in_specs, out_specs, out_shape, grid=None, scratch_shapes=(), **kw):
    hooks = tuple(hooks)
    n_in, n_out, n_scr = len(in_specs), len(out_shape), len(scratch_shapes)
    h_ops = [a for h in hooks for a in h.operands]
    h_outs = [s for h in hooks for s in h.out_shapes]
    h_sems = [pltpu.SemaphoreType.DMA((n,)) for h in hooks for n in (h.n_remote, h.n_remote, max(h.n_local, 1))]

    def wrapped(*refs):
        ins, hin = refs[:n_in], refs[n_in:n_in + len(h_ops)]
        o0 = n_in + len(h_ops)
        outs, hout = refs[o0:o0 + n_out], refs[o0 + n_out:o0 + n_out + len(h_outs)]
        s0 = o0 + n_out + len(h_outs)
        scr, hsem = refs[s0:s0 + n_scr], refs[s0 + n_scr:]

        def run(phase):
            ip = op = 0
            for i, h in enumerate(hooks):
                ssem, rsem, lsem = hsem[3 * i:3 * i + 3]

                def rcopy(k, src, dst, dev, ssem=ssem, rsem=rsem):
                    return pltpu.make_async_remote_copy(src_ref=src, dst_ref=dst, send_sem=ssem.at[k], recv_sem=rsem.at[k],
                                                        device_id=dev, device_id_type=MESH_ID)

                def lcopy(k, src, dst, lsem=lsem):
                    return pltpu.make_async_copy(src, dst, lsem.at[k])

                getattr(h, phase)(hin[ip:ip + len(h.operands)], hout[op:op + len(h.out_shapes)], rcopy, lcopy)
                ip += len(h.operands)
                op += len(h.out_shapes)

        def at_edge(phase, last):
            if not hooks:
                return
            if grid is None:
                run(phase)
                return
            cond = None
            for ax, n in enumerate(grid):
                here = pl.program_id(ax) == (n - 1 if last else 0)
                cond = here if cond is None else cond & here
            pl.when(cond)(lambda: run(phase))

        at_edge("start", False)
        body(*ins, *outs, *scr)
        at_edge("finish", True)

    if grid is not None:
        kw["grid"] = grid
    return list(pl.pallas_call(
        wrapped, out_shape=list(out_shape) + h_outs, in_specs=list(in_specs) + [_HBM] * len(h_ops),
        out_specs=list(out_specs) + [_HBM] * len(h_outs), scratch_shapes=list(scratch_shapes) + h_sems, **kw,
    )(*operands, *h_ops))


def _exchange(name, hook):
    return _call(lambda: None, [hook], [], name=name, in_specs=[], out_specs=[], out_shape=[])


def _rows(ref, start, n):
    return ref.at[pl.ds(start, n), :]


def _ag_hook(group, shard):
    names = GROUPS[group]
    nw = len(names)

    def blocks(full, dev_index):
        return [_rows(full, W_BASE[w] + dev_index * W_ROWS[w], W_ROWS[w]) for w in names]

    def mine(sh, full):
        x, y, c = _my_place()
        return [_rows(sh, W_OFF[w], W_ROWS[w]) for w in names], blocks(full, 4 * x + 2 * y + c)

    def start(ins, outs, rcopy, lcopy):
        x, y, c = _my_place()
        srcs, dsts = mine(ins[0], outs[0])
        for wi, (src, dst) in enumerate(zip(srcs, dsts)):
            lcopy(wi, src, dst).start()
            rcopy(wi, src, dst, (x, y, 1 - c)).start()
            for j, (dev, _) in enumerate(_ici_peers(x, y, c)):
                rcopy((1 + j) * nw + wi, src, dst, dev).start()

    def finish(ins, outs, rcopy, lcopy):
        x, y, c = _my_place()
        sib = (x, y, 1 - c)
        full = outs[0]
        peers = _ici_peers(x, y, c)
        for j, (dev, chip) in enumerate(peers):
            for wi, blk in enumerate(blocks(full, 2 * chip + c)):
                rcopy((1 + j) * nw + wi, blk, blk, dev).wait_recv()
                rcopy((4 + j) * nw + wi, blk, blk, sib).start()
        for wi, blk in enumerate(blocks(full, 2 * (2 * x + y) + 1 - c)):
            rcopy(wi, blk, blk, sib).wait_recv()
        for j, (dev, chip) in enumerate(peers):
            for wi, blk in enumerate(blocks(full, 2 * chip + 1 - c)):
                rcopy((4 + j) * nw + wi, blk, blk, sib).wait_recv()
        srcs, dsts = mine(ins[0], full)
        for wi, (src, dst) in enumerate(zip(srcs, dsts)):
            lcopy(wi, src, dst).wait()
            rcopy(wi, src, dst, sib).wait_send()
            for j, (dev, chip) in enumerate(peers):
                rcopy((1 + j) * nw + wi, src, dst, dev).wait_send()
        for j, (dev, chip) in enumerate(peers):
            for wi, blk in enumerate(blocks(full, 2 * chip + c)):
                rcopy((4 + j) * nw + wi, blk, blk, sib).wait_send()

    return _Hook([shard], [jax.ShapeDtypeStruct((NDEV * G_ROWS[group], D), shard.dtype)], 7 * nw, nw, start, finish)


def _rs_d2d_hook(group, gfull):
    names = GROUPS[group]
    nw = len(names)

    def pairs(g, land):
        x, y, c = _my_place()
        out = []
        for wi, w in enumerate(names):
            rows = W_ROWS[w]
            for k in range(NCHIP):
                out.append((k * nw + wi, _rows(g, W_BASE[w] + (2 * k + 1 - c) * rows, rows), land.at[k, pl.ds(W_OFF[w], rows), :]))
        return (x, y, 1 - c), out

    def start(ins, outs, rcopy, lcopy):
        sib, cps = pairs(ins[0], outs[0])
        for i, src, dst in cps:
            rcopy(i, src, dst, sib).start()

    def finish(ins, outs, rcopy, lcopy):
        sib, cps = pairs(ins[0], outs[0])
        for i, src, dst in cps:
            rcopy(i, dst, dst, sib).wait_recv()
        for i, src, dst in cps:
            rcopy(i, src, dst, sib).wait_send()

    return _Hook([gfull], [jax.ShapeDtypeStruct((NCHIP, G_ROWS[group], D), gfull.dtype)], NCHIP * nw, 0, start, finish)


def _rs_ici_hook(part):
    def start(ins, outs, rcopy, lcopy):
        x, y, c = _my_place()
        mychip = 2 * x + y
        lcopy(0, ins[0].at[mychip], outs[0].at[mychip]).start()
        for j, (dev, chip) in enumerate(_ici_peers(x, y, c)):
            rcopy(j, ins[0].at[chip], outs[0].at[mychip], dev).start()

    def finish(ins, outs, rcopy, lcopy):
        x, y, c = _my_place()
        mychip = 2 * x + y
        peers = _ici_peers(x, y, c)
        for j, (dev, chip) in enumerate(peers):
            rcopy(j, outs[0].at[chip], outs[0].at[chip], dev).wait_recv()
        for j, (dev, chip) in enumerate(peers):
            rcopy(j, ins[0].at[chip], outs[0].at[mychip], dev).wait_send()
        lcopy(0, ins[0].at[mychip], outs[0].at[mychip]).wait()

    return _Hook([part], [jax.ShapeDtypeStruct(part.shape, part.dtype)], 3, 1, start, finish)


def _small_hook(a, b):
    def peers():
        x, y, c = _my_place()
        out = []
        for dx in range(2):
            for dy in range(2):
                for dc in range(2):
                    if dx + dy + dc:
                        px, py, pc = (1 - x if dx else x), (1 - y if dy else y), (1 - c if dc else c)
                        out.append(((px, py, pc), 4 * px + 2 * py + pc))
        return 4 * x + 2 * y + c, out

    def start(ins, outs, rcopy, lcopy):
        me, ps = peers()
        for t in range(2):
            lcopy(t, ins[t], outs[t].at[me]).start()
            for i, (dev, _) in enumerate(ps):
                rcopy(2 * i + t, ins[t], outs[t].at[me], dev).start()

    def finish(ins, outs, rcopy, lcopy):
        me, ps = peers()
        for t in range(2):
            for i, (dev, peer) in enumerate(ps):
                rcopy(2 * i + t, outs[t].at[peer], outs[t].at[peer], dev).wait_recv()
            for i, (dev, _) in enumerate(ps):
                rcopy(2 * i + t, ins[t], outs[t].at[me], dev).wait_send()
            lcopy(t, ins[t], outs[t].at[me]).wait()

    return _Hook([a, b], [jax.ShapeDtypeStruct((NDEV,) + a.shape, a.dtype), jax.ShapeDtypeStruct((NDEV,) + b.shape, b.dtype)],
                 14, 2, start, finish)


def _wblock(w):
    rows = NDEV * W_ROWS[w]
    assert W_BASE[w] % rows == 0
    return _resident((rows, D), lambda *_: (W_BASE[w] // rows, 0))


def _ffn_fwd(name, x, nrm, wfull, ffn, hooks=(), tm=512):
    S = x.shape[0]
    wg, wu, wd = ffn + "_w_gate", ffn + "_w_up", ffn + "_w_down"

    def body(x_ref, n_ref, wg_ref, wu_ref, wd_ref, y_ref, g_ref, u_ref, acc_ref):
        xv = x_ref[...]
        _, _, h = _rms_fwd(xv, n_ref[...])
        h = h.astype(CDT)
        for ci in range(F // FT):
            sl = slice(ci * FT, (ci + 1) * FT)
            g = _mm_nt(h, wg_ref[sl, :])
            u = _mm_nt(h, wu_ref[sl, :])
            g_ref[:, sl] = g.astype(CDT)
            u_ref[:, sl] = u.astype(CDT)
            a = (g * jax.nn.sigmoid(g) * u).astype(CDT)
            o = _mm(a, wd_ref[sl, :])
            if ci == 0:
                acc_ref[...] = o
            else:
                acc_ref[...] += o
        y_ref[...] = xv + 0.5 * acc_ref[...]

    return _call(
        body, hooks, [x, nrm, wfull, wfull, wfull], name=name, grid=(S // tm,),
        out_shape=[jax.ShapeDtypeStruct((S, D), f32), jax.ShapeDtypeStruct((S, F), CDT), jax.ShapeDtypeStruct((S, F), CDT)],
        in_specs=[pl.BlockSpec((tm, D), lambda i: (i, 0)), _resident((1, D), lambda i: (0, 0)),
                  _wblock(wg), _wblock(wu), _wblock(wd)],
        out_specs=[pl.BlockSpec((tm, D), lambda i: (i, 0)), pl.BlockSpec((tm, F), lambda i: (i, 0)),
                   pl.BlockSpec((tm, F), lambda i: (i, 0))],
        scratch_shapes=[pltpu.VMEM((tm, D), f32)],
        compiler_params=_params("arbitrary"),
    )


def _proj_fwd(x1, nrm, wfull, b3, hooks=(), tm=512):
    S = x1.shape[0]
    wbase = W_BASE["w_in"] // D

    def body(x_ref, n_ref, w_ref, b_ref, p_ref, h_ref):
        @pl.when(pl.program_id(1) == 0)
        def _():
            _, _, h = _rms_fwd(x_ref[...], n_ref[...])
            h_ref[...] = h.astype(CDT)

        p_ref[...] = (_mm(h_ref[...], w_ref[...]) + b_ref[...]).astype(CDT)

    return _call(
        body, hooks, [x1, nrm, wfull, b3], name="proj_fwd", grid=(S // tm, 8),
        out_shape=[jax.ShapeDtypeStruct((8, S, D), CDT), jax.ShapeDtypeStruct((S, D), CDT)],
        in_specs=[pl.BlockSpec((tm, D), lambda i, p: (i, 0)), _resident((1, D), lambda i, p: (0, 0)),
                  pl.BlockSpec((D, D), lambda i, p: (wbase + _seg_of_slot(p), 0)),
                  pl.BlockSpec((None, 1, D), lambda i, p: (_seg_of_slot(p), 0, 0))],
        out_specs=[pl.BlockSpec((None, tm, D), lambda i, p: (p, i, 0)), pl.BlockSpec((tm, D), lambda i, p: (i, 0))],
        compiler_params=_params("arbitrary", "arbitrary"),
    )


def _sgu_norm(va, gn, bn):
    mu = jnp.mean(va, axis=-1, keepdims=True)
    xc = va - mu
    rstd = lax.rsqrt(jnp.mean(xc * xc, axis=-1, keepdims=True) + EPS)
    vhat = xc * rstd
    return rstd, vhat, vhat * gn + bn


def _sgu_fwd(proj, gn, bn, ws, bsc, tm=512):
    S = proj.shape[1]
    GW = D // G

    def body(p_ref, gn_ref, bn_ref, ws_ref, bs_ref, a_ref):
        ua = _gelu(p_ref[0].astype(f32))
        va = _gelu(p_ref[1].astype(f32))
        _, _, vn = _sgu_norm(va, gn_ref[...], bn_ref[...])
        vn = vn.astype(CDT)
        for ch in range(tm // C):
            rs = slice(ch * C, (ch + 1) * C)
            for gi in range(G):
                cs = slice(gi * GW, (gi + 1) * GW)
                s = _mm(ws_ref[gi], vn[rs, cs]) + bs_ref[gi]
                a_ref[rs, cs] = (ua[rs, cs] * s).astype(CDT)

    return pl.pallas_call(
        body, name="sgu_fwd", grid=(S // tm,),
        out_shape=jax.ShapeDtypeStruct((S, D), CDT),
        in_specs=[pl.BlockSpec((2, tm, D), lambda i: (0, i, 0)), _resident((1, D), lambda i: (0, 0)),
                  _resident((1, D), lambda i: (0, 0)), _resident((G, C, C), lambda i: (0, 0, 0)),
                  _resident((G, C, 1), lambda i: (0, 0, 0))],
        out_specs=pl.BlockSpec((tm, D), lambda i: (i, 0)),
        compiler_params=_params("arbitrary"),
    )(proj, gn, bn, ws, bsc)


def _decay_tables(dl_ref):
    lg = jax.nn.log_sigmoid(dl_ref[0:2, :])
    lgf, lgb = lg[0:1, :], lg[1:2, :]
    ri = lax.broadcasted_iota(jnp.int32, (C, C), 0)
    ci = lax.broadcasted_iota(jnp.int32, (C, C), 1)
    d = (ri - ci).astype(f32)
    lower = d >= 0
    dmat = jnp.where(lower, jnp.exp(d * lgf[:, :C]), jnp.exp(-d * lgb[:, :C]))
    dmat_t = jnp.where(d <= 0, jnp.exp(-d * lgf[:, :C]), jnp.exp(d * lgb[:, :C]))
    pos = lax.broadcasted_iota(jnp.int32, (C, DK), 0).astype(f32)
    t = dict(
        lgf=lgf, lgb=lgb, d=d, lower=lower, dmat=dmat, dmat_t=dmat_t, pos=pos,
        fq=jnp.exp((pos + 1.0) * lgf), fk=jnp.exp((C - 1.0 - pos) * lgf),
        bq=jnp.exp((C - pos) * lgb), bk=jnp.exp(pos * lgb),
        lamf=jnp.exp(float(C) * lgf), lamb=jnp.exp(float(C) * lgb),
    )
    return t


def _rotate(t, co, si):
    t1, t2 = t[:, :DK // 2], t[:, DK // 2:]
    return jnp.concatenate([t1 * co - t2 * si, t2 * co + t1 * si], axis=-1)


def _unrotate(t, co, si):
    t1, t2 = t[:, :DK // 2], t[:, DK // 2:]
    return jnp.concatenate([t1 * co + t2 * si, t2 * co - t1 * si], axis=-1)


K_SCALE = DK ** -0.5
ROT_ROWS = 512


def _rotate_into(q_ref, k_ref, cos_ref, sin_ref, qs_ref, ks_ref, S):
    for rt in range(S // ROT_ROWS):
        rs = slice(rt * ROT_ROWS, (rt + 1) * ROT_ROWS)
        co, si = cos_ref[rs, :], sin_ref[rs, :]
        qs_ref[rs, :] = _rotate(q_ref[rs, :].astype(f32), co, si).astype(CDT)
        ks_ref[rs, :] = (_rotate(k_ref[rs, :].astype(f32), co, si) * K_SCALE).astype(CDT)


def _ret_fwd(proj, cos, sin, dl, hooks=()):
    S = proj.shape[1]
    NC = S // C

    def body(q_ref, k_ref, v_ref, g_ref, cos_ref, sin_ref, dl_ref, R_ref, r_ref, qs_ref, ks_ref, st_ref):
        t = _decay_tables(dl_ref)
        _rotate_into(q_ref, k_ref, cos_ref, sin_ref, qs_ref, ks_ref, S)

        def chunk(n):
            rows = pl.ds(pl.multiple_of(n * C, C), C)
            return rows, qs_ref[rows, :], ks_ref[rows, :], v_ref[rows, :]

        st_ref[...] = jnp.zeros_like(st_ref)

        def fwd_step(n, carry):
            rows, qn, kn, vn = chunk(n)
            sc = _mm_nt(qn, kn) * t["dmat"]
            out = _mm(sc.astype(CDT), vn)
            out += _mm((qn.astype(f32) * t["fq"]).astype(CDT), st_ref[...].astype(CDT))
            R_ref[rows, :] = out
            st_ref[...] = st_ref[...] * t["lamf"] + _mm_tn((kn.astype(f32) * t["fk"]).astype(CDT), vn)
            return carry

        lax.fori_loop(0, NC, fwd_step, 0)
        st_ref[...] = jnp.zeros_like(st_ref)

        def bwd_step(i, carry):
            rows, qn, kn, vn = chunk(NC - 1 - i)
            R_ref[rows, :] += _mm((qn.astype(f32) * t["bq"]).astype(CDT), st_ref[...].astype(CDT))
            st_ref[...] = st_ref[...] * t["lamb"] + _mm_tn((kn.astype(f32) * t["bk"]).astype(CDT), vn)
            return carry

        lax.fori_loop(0, NC, bwd_step, 0)
        for rt in range(S // ROT_ROWS):
            rs = slice(rt * ROT_ROWS, (rt + 1) * ROT_ROWS)
            R = R_ref[rs, :]
            rn = R * lax.rsqrt(jnp.mean(R * R, axis=-1, keepdims=True) + EPS)
            g = g_ref[rs, :].astype(f32)
            r_ref[rs, :] = (rn * g * jax.nn.sigmoid(g)).astype(CDT)

    def seg(slot):
        return pl.BlockSpec((None, S, DK), lambda h: (slot, 0, h))

    return _call(
        body, hooks, [proj, proj, proj, proj, cos, sin, dl], name="ret_fwd", grid=(H,),
        out_shape=[jax.ShapeDtypeStruct((S, H * DK), f32), jax.ShapeDtypeStruct((S, H * DK), CDT)],
        in_specs=[seg(SLOT_Q), seg(SLOT_K), seg(SLOT_VR), seg(SLOT_GR),
                  _resident((S, DK // 2), lambda h: (0, 0)), _resident((S, DK // 2), lambda h: (0, 0)),
                  pl.BlockSpec((None, 8, DK), lambda h: (h, 0, 0))],
        out_specs=[pl.BlockSpec((S, DK), lambda h: (0, h)), pl.BlockSpec((S, DK), lambda h: (0, h))],
        scratch_shapes=[pltpu.VMEM((S, DK), CDT), pltpu.VMEM((S, DK), CDT), pltpu.VMEM((DK, DK), f32)],
        compiler_params=_params("arbitrary"),
    )


def _merge_fwd(a, r, proj, x1, wfull, tm=512):
    S = x1.shape[0]

    def body(a_ref, r_ref, gt_ref, x_ref, wa_ref, wb_ref, wo_ref, x2_ref, ya_ref, yb_ref):
        ya = _mm(a_ref[...], wa_ref[...])
        yb = _mm(r_ref[...], wb_ref[...])
        ya_ref[...] = ya.astype(CDT)
        yb_ref[...] = yb.astype(CDT)
        mix = jax.nn.sigmoid(gt_ref[0].astype(f32)) * ya + jax.nn.sigmoid(gt_ref[1].astype(f32)) * yb
        x2_ref[...] = x_ref[...] + _mm(mix.astype(CDT), wo_ref[...])

    tok = pl.BlockSpec((tm, D), lambda i: (i, 0))
    return pl.pallas_call(
        body, name="merge_fwd", grid=(S // tm,),
        out_shape=[jax.ShapeDtypeStruct((S, D), f32), jax.ShapeDtypeStruct((S, D), CDT), jax.ShapeDtypeStruct((S, D), CDT)],
        in_specs=[tok, tok, pl.BlockSpec((2, tm, D), lambda i: (SLOT_GA // 2, i, 0)), tok,
                  _wblock("w_branch_a"), _wblock("w_branch_b"), _wblock("w_out")],
        out_specs=[tok, tok, tok],
        compiler_params=_params("arbitrary"),
    )(a, r, proj, x1, wfull, wfull, wfull)


def _loss_head(x3, fn, target, tm=512):
    S = x3.shape[0]

    def body(x_ref, n_ref, t_ref, dx_ref, dn_ref, l_ref):
        n = n_ref[...]
        r, xh, y = _rms_fwd(x_ref[...], n)
        e = y - t_ref[...]
        dy = e * (1.0 / D)
        dx, dn = _rms_bwd(dy, r, xh, n)
        dx_ref[...] = dx
        part = 0.5 * jnp.sum(jnp.sum(e * e, axis=-1, keepdims=True), axis=0, keepdims=True) * (1.0 / D)

        @pl.when(pl.program_id(0) == 0)
        def _():
            dn_ref[...] = jnp.zeros_like(dn_ref)
            l_ref[...] = jnp.zeros_like(l_ref)

        dn_ref[...] += dn
        l_ref[...] += jnp.broadcast_to(part, l_ref.shape)

    tok = pl.BlockSpec((tm, D), lambda i: (i, 0))
    return pl.pallas_call(
        body, name="loss_head", grid=(S // tm,),
        out_shape=[jax.ShapeDtypeStruct((S, D), f32), jax.ShapeDtypeStruct((1, D), f32), jax.ShapeDtypeStruct((8, 128), f32)],
        in_specs=[tok, _resident((1, D), lambda i: (0, 0)), tok],
        out_specs=[tok, pl.BlockSpec((1, D), lambda i: (0, 0)), pl.BlockSpec((8, 128), lambda i: (0, 0))],
        compiler_params=_params("arbitrary"),
    )(x3, fn, target)


def _ffn_bwd_act(name, dy, x, g, u, nrm, wfull, ffn, hooks=(), tm=256):
    S = x.shape[0]
    wg, wu, wd = ffn + "_w_gate", ffn + "_w_up", ffn + "_w_down"

    def body(dy_ref, x_ref, g_ref, u_ref, n_ref, wg_ref, wu_ref, wd_ref,
             dx_ref, dg_ref, du_ref, a_ref, h_ref, dyh_ref, dn_ref, acc_ref):
        n = n_ref[...]
        dyv = dy_ref[...]
        r, xh, h = _rms_fwd(x_ref[...], n)
        h_ref[...] = h.astype(CDT)
        dyh = (0.5 * dyv).astype(CDT)
        dyh_ref[...] = dyh
        for ci in range(F // FT):
            sl = slice(ci * FT, (ci + 1) * FT)
            da = _mm_nt(dyh, wd_ref[sl, :])
            gv = g_ref[:, sl].astype(f32)
            uv = u_ref[:, sl].astype(f32)
            s = jax.nn.sigmoid(gv)
            silu = gv * s
            a_ref[:, sl] = (silu * uv).astype(CDT)
            du = (da * silu).astype(CDT)
            dg = (da * uv * (s * (1.0 + gv * (1.0 - s)))).astype(CDT)
            du_ref[:, sl] = du
            dg_ref[:, sl] = dg
            dh = _mm(dg, wg_ref[sl, :]) + _mm(du, wu_ref[sl, :])
            if ci == 0:
                acc_ref[...] = dh
            else:
                acc_ref[...] += dh
        dx, dn = _rms_bwd(acc_ref[...], r, xh, n)
        dx_ref[...] = dyv + dx

        @pl.when(pl.program_id(0) == 0)
        def _():
            dn_ref[...] = jnp.zeros_like(dn_ref)

        dn_ref[...] += dn

    tok = pl.BlockSpec((tm, D), lambda i: (i, 0))
    hid = pl.BlockSpec((tm, F), lambda i: (i, 0))
    return _call(
        body, hooks, [dy, x, g, u, nrm, wfull, wfull, wfull], name=name, grid=(S // tm,),
        out_shape=[jax.ShapeDtypeStruct((S, D), f32), jax.ShapeDtypeStruct((S, F), CDT), jax.ShapeDtypeStruct((S, F), CDT),
                   jax.ShapeDtypeStruct((S, F), CDT), jax.ShapeDtypeStruct((S, D), CDT), jax.ShapeDtypeStruct((S, D), CDT),
                   jax.ShapeDtypeStruct((1, D), f32)],
        in_specs=[tok, tok, hid, hid, _resident((1, D), lambda i: (0, 0)), _wblock(wg), _wblock(wu), _wblock(wd)],
        out_specs=[tok, hid, hid, hid, tok, tok, pl.BlockSpec((1, D), lambda i: (0, 0))],
        scratch_shapes=[pltpu.VMEM((tm, D), f32)],
        compiler_params=_params("arbitrary"),
    )


TN_ROWS = 256


def _tn_into(name, group, gfull, xs, ys, row_block_of, hooks=()):
    S, M = xs.shape
    B = ys.shape[0]

    def body(*refs):
        x_ref, y_ref, o_ref = refs[0], refs[1], refs[-1]
        o_ref[...] = _mm_tn(x_ref[...], y_ref[...]).astype(CDT)

    in_specs = [pl.BlockSpec((S, TN_ROWS), lambda b, i: (0, i)), pl.BlockSpec((None, S, D), lambda b, i: (b, 0, 0))]
    operands = [xs, ys]
    aliases = {}
    if gfull is not None:
        in_specs.append(_HBM)
        operands.append(gfull)
        aliases = {2: 0}
    return _call(
        body, hooks, operands, name=name, grid=(B, M // TN_ROWS),
        out_shape=[jax.ShapeDtypeStruct((NDEV * G_ROWS[group], D), CDT)],
        in_specs=in_specs,
        out_specs=[pl.BlockSpec((TN_ROWS, D), lambda b, i: (row_block_of(b, i), 0))],
        input_output_aliases=aliases,
        compiler_params=_params("arbitrary", "arbitrary"),
    )


def _wgrad(name, group, gfull, w, xs, y, hooks=()):
    base = W_BASE[w] // TN_ROWS
    return _tn_into(name, group, gfull, xs, y[None], lambda b, i: base + i, hooks)


def _merge_bwd_act(dx2, ya, yb, proj, wfull, hooks=(), tm=512):
    S = dx2.shape[0]

    def body(dx_ref, ya_ref, yb_ref, gt_ref, wa_ref, wb_ref, wo_ref,
             dp_ref, da_ref, dr_ref, mix_ref, dxb_ref, dya_ref, dyb_ref):
        dxb = dx_ref[...].astype(CDT)
        dxb_ref[...] = dxb
        dmix = _mm_nt(dxb, wo_ref[...])
        ya = ya_ref[...].astype(f32)
        yb = yb_ref[...].astype(f32)
        sa = jax.nn.sigmoid(gt_ref[0].astype(f32))
        sb = jax.nn.sigmoid(gt_ref[1].astype(f32))
        mix_ref[...] = (sa * ya + sb * yb).astype(CDT)
        dya = (dmix * sa).astype(CDT)
        dyb = (dmix * sb).astype(CDT)
        dya_ref[...] = dya
        dyb_ref[...] = dyb
        dp_ref[0] = (dmix * ya * sa * (1.0 - sa)).astype(CDT)
        dp_ref[1] = (dmix * yb * sb * (1.0 - sb)).astype(CDT)
        da_ref[...] = _mm_nt(dya, wa_ref[...]).astype(CDT)
        dr_ref[...] = _mm_nt(dyb, wb_ref[...]).astype(CDT)

    tok = pl.BlockSpec((tm, D), lambda i: (i, 0))
    gates = pl.BlockSpec((2, tm, D), lambda i: (SLOT_GA // 2, i, 0))
    act = jax.ShapeDtypeStruct((S, D), CDT)
    return _call(
        body, hooks, [dx2, ya, yb, proj, wfull, wfull, wfull], name="merge_bwd_act", grid=(S // tm,),
        out_shape=[jax.ShapeDtypeStruct((8, S, D), CDT), act, act, act, act, act, act],
        in_specs=[tok, tok, tok, gates, _wblock("w_branch_a"), _wblock("w_branch_b"), _wblock("w_out")],
        out_specs=[gates, tok, tok, tok, tok, tok, tok],
        compiler_params=_params("arbitrary"),
    )


def _sgu_bwd(da, proj, dproj, gn, bn, ws, wst, bsc, hooks=(), tm=512):
    S = proj.shape[1]
    GW = D // G

    def body(da_ref, p_ref, dpin_ref, gn_ref, bn_ref, ws_ref, wst_ref, bs_ref,
             dp_ref, dws_ref, dbs_ref, dgn_ref, dbn_ref, ds_ref, dvn_ref):
        @pl.when(pl.program_id(0) == 0)
        def _():
            dws_ref[...] = jnp.zeros_like(dws_ref)
            dbs_ref[...] = jnp.zeros_like(dbs_ref)
            dgn_ref[...] = jnp.zeros_like(dgn_ref)
            dbn_ref[...] = jnp.zeros_like(dbn_ref)

        pu = p_ref[0].astype(f32)
        pv = p_ref[1].astype(f32)
        ua = _gelu(pu)
        va = _gelu(pv)
        gn = gn_ref[...]
        rstd, vhat, vn = _sgu_norm(va, gn, bn_ref[...])
        vnb = vn.astype(CDT)
        dav = da_ref[...].astype(f32)
        dsb = (dav * ua).astype(CDT)
        ones = jnp.ones((8, GW), CDT)
        for ch in range(tm // C):
            rs = slice(ch * C, (ch + 1) * C)
            for gi in range(G):
                cs = slice(gi * GW, (gi + 1) * GW)
                s = _mm(ws_ref[gi], vnb[rs, cs]) + bs_ref[gi]
                ds_ref[rs, cs] = s
                dsg = dsb[rs, cs]
                dws_ref[gi] += _mm_nt(dsg, vnb[rs, cs])
                dbs_ref[gi] += _mm_nt(ones, dsg)
                dvn_ref[rs, cs] = _mm(wst_ref[gi], dsg)
        dp_ref[0] = (dav * ds_ref[...] * _gelu_grad(pu)).astype(CDT)
        dvn = dvn_ref[...]
        dgn_ref[...] += jnp.sum(dvn * vhat, axis=0, keepdims=True)
        dbn_ref[...] += jnp.sum(dvn, axis=0, keepdims=True)
        dvh = dvn * gn
        dva = rstd * (dvh - jnp.mean(dvh, axis=-1, keepdims=True) - vhat * jnp.mean(dvh * vhat, axis=-1, keepdims=True))
        dp_ref[1] = (dva * _gelu_grad(pv)).astype(CDT)

    uv = pl.BlockSpec((2, tm, D), lambda i: (0, i, 0))
    row = _resident((1, D), lambda i: (0, 0))
    return _call(
        body, hooks, [da, proj, dproj, gn, bn, ws, wst, bsc], name="sgu_bwd", grid=(S // tm,),
        out_shape=[jax.ShapeDtypeStruct(dproj.shape, CDT), jax.ShapeDtypeStruct((G, C, C), f32),
                   jax.ShapeDtypeStruct((G, 8, C), f32), jax.ShapeDtypeStruct((1, D), f32), jax.ShapeDtypeStruct((1, D), f32)],
        in_specs=[pl.BlockSpec((tm, D), lambda i: (i, 0)), uv, _HBM, row, row,
                  _resident((G, C, C), lambda i: (0, 0, 0)), _resident((G, C, C), lambda i: (0, 0, 0)),
                  _resident((G, C, 1), lambda i: (0, 0, 0))],
        out_specs=[uv, pl.BlockSpec((G, C, C), lambda i: (0, 0, 0)), pl.BlockSpec((G, 8, C), lambda i: (0, 0, 0)),
                   pl.BlockSpec((1, D), lambda i: (0, 0)), pl.BlockSpec((1, D), lambda i: (0, 0))],
        scratch_shapes=[pltpu.VMEM((tm, D), f32), pltpu.VMEM((tm, D), f32)],
        input_output_aliases={2: 0},
        compiler_params=_params("arbitrary"),
    )


def _ret_bwd(dr, R, proj, dproj, cos, sin, dl, hooks=()):
    S = proj.shape[1]
    NC = S // C

    def body(dr_ref, R_ref, q_ref, k_ref, v_ref, g_ref, dpin_ref, cos_ref, sin_ref, dl_ref,
             dp_ref, dd_ref, qs_ref, ks_ref, dR_ref, sts_ref, dk_ref, dv_ref, st_ref, gst_ref, acc_ref):
        sb_ref = sf_ref = sts_ref
        t = _decay_tables(dl_ref)
        _rotate_into(q_ref, k_ref, cos_ref, sin_ref, qs_ref, ks_ref, S)
        for rt in range(S // ROT_ROWS):
            rs = slice(rt * ROT_ROWS, (rt + 1) * ROT_ROWS)
            Rv = R_ref[rs, :]
            rstd = lax.rsqrt(jnp.mean(Rv * Rv, axis=-1, keepdims=True) + EPS)
            rn = Rv * rstd
            gv = g_ref[rs, :].astype(f32)
            s = jax.nn.sigmoid(gv)
            drv = dr_ref[rs, :].astype(f32)
            dp_ref[3, rs, :] = (drv * rn * (s * (1.0 + gv * (1.0 - s)))).astype(CDT)
            drn = drv * gv * s
            dR_ref[rs, :] = (rstd * (drn - rn * jnp.mean(drn * rn, axis=-1, keepdims=True))).astype(CDT)

        def chunk(n):
            rows = pl.ds(pl.multiple_of(n * C, C), C)
            return rows, qs_ref[rows, :], ks_ref[rows, :], v_ref[rows, :], dR_ref[rows, :]

        st_ref[...] = jnp.zeros_like(st_ref)

        def pass_a(i, carry):
            n = NC - 1 - i
            _, _, kn, vn, _ = chunk(n)
            sb_ref[n] = st_ref[...].astype(CDT)
            st_ref[...] = st_ref[...] * t["lamb"] + _mm_tn((kn.astype(f32) * t["bk"]).astype(CDT), vn)
            return carry

        lax.fori_loop(0, NC, pass_a, 0)

        st_ref[...] = jnp.zeros_like(st_ref)
        gst_ref[...] = jnp.zeros_like(gst_ref)
        acc_ref[...] = jnp.zeros_like(acc_ref)
        dpos = jnp.abs(t["d"])

        def pass_b(n, carry):
            rows, qn, kn, vn, dRn = chunk(n)
            qf, kf = qn.astype(f32), kn.astype(f32)
            sc = _mm_nt(qn, kn)
            dA = _mm_nt(dRn, vn)
            w = sc * dA * t["dmat"] * dpos
            lgf_part = jnp.sum(jnp.where(t["lower"], w, 0.0), axis=0, keepdims=True)
            lgb_part = jnp.sum(jnp.where(t["lower"], 0.0, w), axis=0, keepdims=True)
            dsc = (dA * t["dmat"]).astype(CDT)
            dq = _mm(dsc, kn)
            scT = (_mm_nt(kn, qn) * t["dmat_t"]).astype(CDT)
            dscT = (_mm_nt(vn, dRn) * t["dmat_t"]).astype(CDT)
            dk = _mm(dscT, qn)
            dv = _mm(scT, dRn)
            sf = st_ref[...]
            sfb = sf.astype(CDT)
            sbb = sb_ref[n]
            sf_ref[n] = sfb
            qdf = qf * t["fq"]
            dqdf = _mm_nt(dRn, sfb)
            dq += dqdf * t["fq"]
            lgf_row = jnp.sum(qdf * dqdf * (t["pos"] + 1.0), axis=0, keepdims=True)
            qdb = qf * t["bq"]
            dqdb = _mm_nt(dRn, sbb)
            dq += dqdb * t["bq"]
            lgb_row = jnp.sum(qdb * dqdb * (C - t["pos"]), axis=0, keepdims=True)
            gb = gst_ref[...]
            gbb = gb.astype(CDT)
            kdb = kf * t["bk"]
            dkdb = _mm_nt(vn, gbb)
            dk += dkdb * t["bk"]
            dv += _mm(kdb.astype(CDT), gbb)
            lgb_row += jnp.sum(kdb * dkdb * t["pos"], axis=0, keepdims=True)
            lgb_row += float(C) * t["lamb"] * jnp.sum(gb * sbb.astype(f32), axis=0, keepdims=True)
            co, si = cos_ref[rows, :], sin_ref[rows, :]
            dp_ref[0, rows, :] = _unrotate(dq, co, si).astype(CDT)
            dk_ref[rows, :] = dk
            dv_ref[rows, :] = dv
            acc_ref[0:1, :] += lgf_row + jnp.concatenate([lgf_part, jnp.zeros_like(lgf_part)], axis=1)
            acc_ref[1:2, :] += lgb_row + jnp.concatenate([lgb_part, jnp.zeros_like(lgb_part)], axis=1)
            st_ref[...] = sf * t["lamf"] + _mm_tn((kf * t["fk"]).astype(CDT), vn)
            gst_ref[...] = gb * t["lamb"] + _mm_tn(qdb.astype(CDT), dRn)
            return carry

        lax.fori_loop(0, NC, pass_b, 0)

        gst_ref[...] = jnp.zeros_like(gst_ref)

        def pass_c(i, carry):
            n = NC - 1 - i
            rows, qn, kn, vn, dRn = chunk(n)
            gf = gst_ref[...]
            gfb = gf.astype(CDT)
            kdf = kn.astype(f32) * t["fk"]
            dkdf = _mm_nt(vn, gfb)
            dk = dk_ref[rows, :] + dkdf * t["fk"]
            dv = dv_ref[rows, :] + _mm(kdf.astype(CDT), gfb)
            lgf_row = jnp.sum(kdf * dkdf * (C - 1.0 - t["pos"]), axis=0, keepdims=True)
            lgf_row += float(C) * t["lamf"] * jnp.sum(gf * sf_ref[n].astype(f32), axis=0, keepdims=True)
            acc_ref[0:1, :] += lgf_row
            co, si = cos_ref[rows, :], sin_ref[rows, :]
            dp_ref[1, rows, :] = (_unrotate(dk, co, si) * K_SCALE).astype(CDT)
            dp_ref[2, rows, :] = dv.astype(CDT)
            gst_ref[...] = gf * t["lamf"] + _mm_tn((qn.astype(f32) * t["fq"]).astype(CDT), dRn)
            return carry

        lax.fori_loop(0, NC, pass_c, 0)
        dlg = jnp.sum(acc_ref[...], axis=1, keepdims=True)
        dlogit = dlg * jax.nn.sigmoid(-dl_ref[:, 0:1])
        lane = lax.broadcasted_iota(jnp.int32, (8, 128), 1)
        dd_ref[...] = jnp.where(lane == pl.program_id(0), jnp.broadcast_to(dlogit, (8, 128)), 0.0)

    def seg(slot):
        return pl.BlockSpec((None, S, DK), lambda h: (slot, 0, h), pipeline_mode=pl.Buffered(1))

    head = pl.BlockSpec((S, DK), lambda h: (0, h), pipeline_mode=pl.Buffered(1))
    return _call(
        body, hooks, [dr, R, proj, proj, proj, proj, dproj, cos, sin, dl], name="ret_bwd", grid=(H,),
        out_shape=[jax.ShapeDtypeStruct(dproj.shape, CDT), jax.ShapeDtypeStruct((H, 8, 128), f32)],
        in_specs=[head, head, seg(SLOT_Q), seg(SLOT_K), seg(SLOT_VR), seg(SLOT_GR), _HBM,
                  _resident((S, DK // 2), lambda h: (0, 0)), _resident((S, DK // 2), lambda h: (0, 0)),
                  pl.BlockSpec((None, 8, DK), lambda h: (h, 0, 0))],
        out_specs=[pl.BlockSpec((4, S, DK), lambda h: (1, 0, h)), pl.BlockSpec((None, 8, 128), lambda h: (h, 0, 0))],
        scratch_shapes=[pltpu.VMEM((S, DK), CDT), pltpu.VMEM((S, DK), CDT), pltpu.VMEM((S, DK), CDT),
                        pltpu.VMEM((NC, DK, DK), CDT),
                        pltpu.VMEM((S, DK), f32), pltpu.VMEM((S, DK), f32),
                        pltpu.VMEM((DK, DK), f32), pltpu.VMEM((DK, DK), f32), pltpu.VMEM((8, DK), f32)],
        input_output_aliases={6: 0},
        compiler_params=_params("arbitrary"),
    )


def _proj_bwd_act(dproj, dx2, x1, nrm, wfull, hooks=(), tm=512):
    S = x1.shape[0]
    wbase = W_BASE["w_in"] // D

    def body(dp_ref, dx2_ref, x_ref, n_ref, w_ref, dx_ref, dn_ref, db_ref, acc_ref):
        i, p = pl.program_id(0), pl.program_id(1)

        @pl.when((i == 0) & (p == 0))
        def _():
            dn_ref[...] = jnp.zeros_like(dn_ref)
            db_ref[...] = jnp.zeros_like(db_ref)

        dp = dp_ref[...]
        db_ref[_seg_of_slot(p)] += jnp.sum(dp.astype(f32), axis=0, keepdims=True)
        dh = _mm_nt(dp, w_ref[...])

        @pl.when(p == 0)
        def _():
            acc_ref[...] = dh

        @pl.when(p > 0)
        def _():
            acc_ref[...] += dh

        @pl.when(p == 7)
        def _():
            n = n_ref[...]
            r, xh, _ = _rms_fwd(x_ref[...], n)
            dx, dn = _rms_bwd(acc_ref[...], r, xh, n)
            dx_ref[...] = dx2_ref[...] + dx
            dn_ref[...] += dn

    tok = pl.BlockSpec((tm, D), lambda i, p: (i, 0))
    return _call(
        body, hooks, [dproj, dx2, x1, nrm, wfull], name="proj_bwd_act", grid=(S // tm, 8),
        out_shape=[jax.ShapeDtypeStruct((S, D), f32), jax.ShapeDtypeStruct((1, D), f32), jax.ShapeDtypeStruct((8, 1, D), f32)],
        in_specs=[pl.BlockSpec((None, tm, D), lambda i, p: (p, i, 0)), tok, tok, _resident((1, D), lambda i, p: (0, 0)),
                  pl.BlockSpec((D, D), lambda i, p: (wbase + _seg_of_slot(p), 0))],
        out_specs=[tok, pl.BlockSpec((1, D), lambda i, p: (0, 0)), pl.BlockSpec((8, 1, D), lambda i, p: (0, 0, 0))],
        scratch_shapes=[pltpu.VMEM((tm, D), f32)],
        compiler_params=_params("arbitrary", "arbitrary"),
    )


def _rs_sum(name, group, gfull, land, my_c):
    names = GROUPS[group]
    rows = W_ROWS[names[0]]
    assert all(W_ROWS[w] == rows for w in names)

    def body(c_ref, g_ref, l_ref, o_ref):
        o_ref[...] = (g_ref[...].astype(f32) + l_ref[...].astype(f32)).astype(CDT)

    return pl.pallas_call(
        body, name=name,
        grid_spec=pltpu.PrefetchScalarGridSpec(
            num_scalar_prefetch=1, grid=(len(names), NCHIP),
            in_specs=[pl.BlockSpec((rows, D), lambda wi, k, c: (NDEV * wi + 2 * k + c[0], 0)),
                      pl.BlockSpec((None, rows, D), lambda wi, k, c: (k, wi, 0))],
            out_specs=pl.BlockSpec((None, rows, D), lambda wi, k, c: (k, wi, 0))),
        out_shape=jax.ShapeDtypeStruct((NCHIP, G_ROWS[group], D), CDT),
        compiler_params=_params("arbitrary", "arbitrary"),
    )(my_c, gfull, land)


def _adamw_math(g, w, m, v):
    m2 = ADAM_B1 * m + (1.0 - ADAM_B1) * g
    v2 = ADAM_B2 * v + (1.0 - ADAM_B2) * (g * g)
    delta = -ADAM_LR * ((m2 / BC1) / (jnp.sqrt(v2 / BC2) + ADAM_EPS) + ADAM_WD * w)
    return delta, m2, v2


def _adamw_big(name, landed, wname, w, m, v, hooks=()):
    rows = W_ROWS[wname]
    tr = min(rows, 256) if rows % 256 == 0 else rows
    lb = W_OFF[wname] // tr
    assert W_OFF[wname] % tr == 0

    def body(l_ref, w_ref, m_ref, v_ref, g_ref, d_ref, m2_ref, v2_ref):
        g = l_ref[0].astype(f32)
        for k in range(1, NCHIP):
            g = g + l_ref[k].astype(f32)
        g_ref[...] = g
        d_ref[...], m2_ref[...], v2_ref[...] = _adamw_math(g, w_ref[...], m_ref[...], v_ref[...])

    blk = pl.BlockSpec((tr, D), lambda i: (i, 0))
    o = jax.ShapeDtypeStruct((rows, D), f32)
    return _call(
        body, hooks, [landed, w, m, v], name=name, grid=(rows // tr,), out_shape=[o, o, o, o],
        in_specs=[pl.BlockSpec((NCHIP, tr, D), lambda i: (0, lb + i, 0)), blk, blk, blk],
        out_specs=[blk, blk, blk, blk],
        compiler_params=_params("arbitrary"),
    )


ROW_FFN1_NORM, ROW_MIX_NORM, ROW_SGU_G, ROW_SGU_B, ROW_FFN2_NORM, ROW_FINAL_NORM, ROW_B_IN = 0, 1, 2, 3, 4, 5, 8
ROW_WS, ROW_BS, ROW_DECAY = 0, G * C, G * C + G * 8


def _adamw_small(ga, gb, params):
    def body(ga_ref, gb_ref, *refs):
        ins, outs = refs[:30], refs[30:]

        def total(ref, r0, n):
            g = ref[0, r0:r0 + n, :]
            for j in range(1, NDEV):
                g = g + ref[j, r0:r0 + n, :]
            return g

        def apply(i, g, rows=slice(None)):
            w, m, v = ins[3 * i][rows, :], ins[3 * i + 1][rows, :], ins[3 * i + 2][rows, :]
            outs[4 * i][rows, :] = g
            outs[4 * i + 1][rows, :], outs[4 * i + 2][rows, :], outs[4 * i + 3][rows, :] = _adamw_math(g, w, m, v)

        for i, r in enumerate((ROW_FFN1_NORM, ROW_MIX_NORM, ROW_SGU_G, ROW_SGU_B, ROW_FFN2_NORM, ROW_FINAL_NORM)):
            apply(i, total(ga_ref, r, 1))
        apply(6, total(ga_ref, ROW_B_IN, 8))
        apply(7, total(gb_ref, ROW_WS, G * C))
        for gi in range(G):
            apply(8, total(gb_ref, ROW_BS + 8 * gi, 1), slice(gi, gi + 1))
        dec = total(gb_ref, ROW_DECAY, 8)
        for hh in range(1, H):
            dec = dec + total(gb_ref, ROW_DECAY + 8 * hh, 8)
        apply(9, dec)

    flat = [a for p in params for a in p]
    out_shape = [jax.ShapeDtypeStruct(p[0].shape, f32) for p in params for _ in range(4)]
    vm = pl.BlockSpec(memory_space=pltpu.VMEM)
    return pl.pallas_call(
        body, name="adamw_small", out_shape=out_shape,
        in_specs=[vm] * (2 + len(flat)), out_specs=[vm] * len(out_shape),
        compiler_params=pltpu.CompilerParams(vmem_limit_bytes=VMEM_LIMIT),
    )(ga, gb, *flat)


def kernel(x, ffn1_norm, ffn1_w_gate, ffn1_w_up, ffn1_w_down, mix_norm, w_in, b_in, sgu_norm_g, sgu_norm_b, sgu_w_s, sgu_b_s, ret_decay_logit, w_branch_a, w_branch_b, w_out, ffn2_norm, ffn2_w_gate, ffn2_w_up, ffn2_w_down, final_norm, loss_target, m_ffn1_norm, m_ffn1_w_gate, m_ffn1_w_up, m_ffn1_w_down, m_mix_norm, m_w_in, m_b_in, m_sgu_norm_g, m_sgu_norm_b, m_sgu_w_s, m_sgu_b_s, m_ret_decay_logit, m_w_branch_a, m_w_branch_b, m_w_out, m_ffn2_norm, m_ffn2_w_gate, m_ffn2_w_up, m_ffn2_w_down, m_final_norm, v_ffn1_norm, v_ffn1_w_gate, v_ffn1_w_up, v_ffn1_w_down, v_mix_norm, v_w_in, v_b_in, v_sgu_norm_g, v_sgu_norm_b, v_sgu_w_s, v_sgu_b_s, v_ret_decay_logit, v_w_branch_a, v_w_branch_b, v_w_out, v_ffn2_norm, v_ffn2_w_gate, v_ffn2_w_up, v_ffn2_w_down, v_final_norm):
    args = dict(locals())
    S = x.shape[1]
    xs = x[0]
    target = loss_target[0]

    def buf_layout(name, a):
        a = a[0]
        return a.T if name in W_TRANSPOSED else a

    shard = {g: jnp.concatenate([buf_layout(n, args[n]).astype(CDT) for n in names], axis=0) for g, names in GROUPS.items()}

    b3 = b_in.reshape(8, 1, D)
    ws = sgu_w_s[0].astype(CDT)
    wst = jnp.swapaxes(sgu_w_s[0], 1, 2).astype(CDT)
    bsc = sgu_b_s[0].reshape(G, C, 1)
    dl = jnp.zeros((H, 8, DK), f32).at[:, 0:2, :].set(jnp.broadcast_to(ret_decay_logit[0].T[:, :, None], (H, 2, DK)))
    theta = ROPE_BASE ** (-jnp.arange(0, DK, 2, dtype=f32) / DK)
    ang = jnp.arange(S, dtype=f32)[:, None] * theta[None, :]
    cos, sin = jnp.cos(ang), jnp.sin(ang)
    fnorm = final_norm.reshape(1, D)

    (wf_ffn1,) = _exchange("ag_ffn1", _ag_hook("ffn1", shard["ffn1"]))
    x1, g1, u1, wf_win = _ffn_fwd("ffn1_fwd", xs, ffn1_norm, wf_ffn1, "ffn1", [_ag_hook("win", shard["win"])])
    proj, h2, wf_ffn2 = _proj_fwd(x1, mix_norm, wf_win, b3, [_ag_hook("ffn2", shard["ffn2"])])
    a = _sgu_fwd(proj, sgu_norm_g, sgu_norm_b, ws, bsc)
    R, r, wf_br = _ret_fwd(proj, cos, sin, dl, [_ag_hook("br", shard["br"])])
    x2, ya, yb = _merge_fwd(a, r, proj, x1, wf_br)
    x3, g2, u2 = _ffn_fwd("ffn2_fwd", x2, ffn2_norm, wf_ffn2, "ffn2")
    dx3, d_final, loss_part = _loss_head(x3, fnorm, target)
    loss = lax.psum(loss_part[0, 0], ("x", "y", "c"))

    my_c = lax.axis_index("c").astype(jnp.int32).reshape(1)
    dx2, dg2, du2, a2, hf2, dyh2, d_ffn2n = _ffn_bwd_act("ffn2_bwd_act", dx3, x2, g2, u2, ffn2_norm, wf_ffn2, "ffn2")
    (gf2,) = _wgrad("ffn2_wd_grad", "ffn2", None, "ffn2_w_down", a2, dyh2)
    (gf2,) = _wgrad("ffn2_wg_grad", "ffn2", gf2, "ffn2_w_gate", dg2, hf2)
    (gf2,) = _wgrad("ffn2_wu_grad", "ffn2", gf2, "ffn2_w_up", du2, hf2)
    dproj, da, dr, mix, dx2b, dya, dyb, sib2 = _merge_bwd_act(dx2, ya, yb, proj, wf_br, [_rs_d2d_hook("ffn2", gf2)])
    part2 = _rs_sum("rs_sum_ffn2", "ffn2", gf2, sib2, my_c)
    (gfb,) = _wgrad("wo_grad", "br", None, "w_out", mix, dx2b)
    (gfb,) = _wgrad("wa_grad", "br", gfb, "w_branch_a", a, dya)
    (gfb,) = _wgrad("wb_grad", "br", gfb, "w_branch_b", r, dyb)
    dproj, d_ws, d_bs, d_gn, d_bn, sibb = _sgu_bwd(da, proj, dproj, sgu_norm_g, sgu_norm_b, ws, wst, bsc,
                                                   [_rs_d2d_hook("br", gfb)])
    partb = _rs_sum("rs_sum_br", "br", gfb, sibb, my_c)
    dproj, d_dec, land2, landb = _ret_bwd(dr, R, proj, dproj, cos, sin, dl, [_rs_ici_hook(part2), _rs_ici_hook(partb)])
    (gfw,) = _tn_into("win_grad", "win", None, h2, dproj, lambda b, i: _seg_of_slot(b) * (D // TN_ROWS) + i)
    dx1, d_mixn, d_bin, sibw = _proj_bwd_act(dproj, dx2, x1, mix_norm, wf_win, [_rs_d2d_hook("win", gfw)])
    partw = _rs_sum("rs_sum_win", "win", gfw, sibw, my_c)
    dxs, dg1, du1, a1, hf1, dyh1, d_ffn1n, landw = _ffn_bwd_act("ffn1_bwd_act", dx1, xs, g1, u1, ffn1_norm, wf_ffn1, "ffn1",
                                                                [_rs_ici_hook(partw)])
    small_a = jnp.concatenate([d_ffn1n, d_mixn, d_gn, d_bn, d_ffn2n, d_final, jnp.zeros((2, D), f32), d_bin.reshape(8, D)], axis=0)
    small_b = jnp.concatenate([d_ws.reshape(G * C, C), d_bs.reshape(G * 8, C), d_dec.reshape(H * 8, 128)], axis=0)
    gf1, ga, gb = _wgrad("ffn1_wd_grad", "ffn1", None, "ffn1_w_down", a1, dyh1, [_small_hook(small_a, small_b)])
    (gf1,) = _wgrad("ffn1_wg_grad", "ffn1", gf1, "ffn1_w_gate", dg1, hf1)
    (gf1,) = _wgrad("ffn1_wu_grad", "ffn1", gf1, "ffn1_w_up", du1, hf1)
    (sib1,) = _exchange("rs_d2d_ffn1", _rs_d2d_hook("ffn1", gf1))
    part1 = _rs_sum("rs_sum_ffn1", "ffn1", gf1, sib1, my_c)

    out = {"loss": loss, "grad_x": dxs[None]}

    def native(name, a):
        a = a.T if name in W_TRANSPOSED else a
        return a[None]

    landed = dict(ffn2=land2, br=landb, win=landw)
    for g in ("win", "ffn2", "br", "ffn1"):
        for n in GROUPS[g]:
            hooks = [_rs_ici_hook(part1)] if n == "w_in" else []
            res = _adamw_big("adamw_" + n, landed[g], n, buf_layout(n, args[n]), buf_layout(n, args["m_" + n]),
                             buf_layout(n, args["v_" + n]), hooks)
            if hooks:
                landed["ffn1"] = res.pop()
            for pre, val in zip(("grad_", "delta_", "new_m_", "new_v_"), res):
                out[pre + n] = native(n, val)

    def pad_decay(a):
        return jnp.zeros((8, 128), f32).at[0:2, 0:H].set(a[0])

    small = [
        ("ffn1_norm", lambda a: a, lambda a: a), ("mix_norm", lambda a: a, lambda a: a),
        ("sgu_norm_g", lambda a: a, lambda a: a), ("sgu_norm_b", lambda a: a, lambda a: a),
        ("ffn2_norm", lambda a: a, lambda a: a),
        ("final_norm", lambda a: a.reshape(1, D), lambda a: a.reshape(D)),
        ("b_in", lambda a: a.reshape(8, D), lambda a: a.reshape(1, 8 * D)),
        ("sgu_w_s", lambda a: a.reshape(G * C, C), lambda a: a.reshape(1, G, C, C)),
        ("sgu_b_s", lambda a: a[0], lambda a: a[None]),
        ("ret_decay_logit", pad_decay, lambda a: a[None, 0:2, 0:H]),
    ]
    res = _adamw_small(ga, gb, [(to(args[n]), to(args["m_" + n]), to(args["v_" + n])) for n, to, _ in small])
    for i, (n, _, back) in enumerate(small):
        for j, pre in enumerate(("grad_", "delta_", "new_m_", "new_v_")):
            out[pre + n] = back(res[4 * i + j])

    weights = ("ffn1_norm", "ffn1_w_gate", "ffn1_w_up", "ffn1_w_down", "mix_norm", "w_in", "b_in", "sgu_norm_g",
               "sgu_norm_b", "sgu_w_s", "sgu_b_s", "ret_decay_logit", "w_branch_a", "w_branch_b", "w_out", "ffn2_norm",
               "ffn2_w_gate", "ffn2_w_up", "ffn2_w_down", "final_norm")
    return (out["loss"], out["grad_x"], *[out["grad_" + n] for n in weights], *[out["delta_" + n] for n in weights],
            *[out["new_m_" + n] for n in weights], *[out["new_v_" + n] for n in weights])
```

```python
import functools
import math

import jax
import jax.numpy as jnp
from jax import lax
from jax.experimental import pallas as pl
from jax.experimental.pallas import tpu as pltpu

f32 = jnp.float32
CDT = jnp.bfloat16

D = 1024
F = 2816
C = 128
RC = 256
H = 4
DK = 256
G = 4
NDEV = 8
NCHIP = 4
EPS = 1e-6
ROPE_BASE = 10000.0
FT = 256
V7X_VMEM_BYTES = 64 * 1024 * 1024
VMEM_LIMIT = V7X_VMEM_BYTES - 8 * 1024 * 1024

ADAM_LR, ADAM_B1, ADAM_B2, ADAM_EPS, ADAM_WD, ADAM_STEP = 0.001, 0.9, 0.999, 1e-08, 0.01, 10
BC1 = 1.0 - ADAM_B1 ** ADAM_STEP
BC2 = 1.0 - ADAM_B2 ** ADAM_STEP

GROUPS = dict(ffn1=("ffn1_w_gate", "ffn1_w_up", "ffn1_w_down"), win=("w_in",),
              br=("w_branch_a", "w_branch_b", "w_out"), ffn2=("ffn2_w_gate", "ffn2_w_up", "ffn2_w_down"))
W_NAMES = tuple(n for g in GROUPS.values() for n in g)
W_ROWS = dict(w_in=1024, w_branch_a=128, w_branch_b=128, w_out=128,
              ffn1_w_gate=352, ffn1_w_up=352, ffn1_w_down=352, ffn2_w_gate=352, ffn2_w_up=352, ffn2_w_down=352)
W_TRANSPOSED = ("ffn1_w_gate", "ffn1_w_up", "ffn2_w_gate", "ffn2_w_up")
W_OFF = {}
W_BASE = {}
G_ROWS = {}
for _g, _names in GROUPS.items():
    _o = 0
    for _n in _names:
        W_OFF[_n] = _o
        W_BASE[_n] = NDEV * _o
        _o += W_ROWS[_n]
    G_ROWS[_g] = _o

SLOT_U, SLOT_V, SLOT_GA, SLOT_GB, SLOT_Q, SLOT_K, SLOT_VR, SLOT_GR = range(8)


SEG_OF_SLOT = (0, 1, 6, 7, 2, 3, 4, 5)


def _seg_of_slot(p):
    return jnp.where(p < 2, p, jnp.where(p < 4, p + 4, p - 2))


def _mm(a, b):
    return jnp.dot(a, b, preferred_element_type=f32)


def _mm_nt(a, b):
    return lax.dot_general(a, b, (((1,), (1,)), ((), ())), preferred_element_type=f32)


def _mm_tn(a, b):
    return lax.dot_general(a, b, (((0,), (0,)), ((), ())), preferred_element_type=f32)


def _params(*sem):
    return pltpu.CompilerParams(dimension_semantics=sem, vmem_limit_bytes=VMEM_LIMIT)


def _resident(shape, index_map):
    return pl.BlockSpec(shape, index_map, pipeline_mode=pl.Buffered(1))


def _gelu(x):
    return 0.5 * x * (1.0 + lax.erf(x * (1.0 / math.sqrt(2.0))))


def _gelu_grad(x):
    return 0.5 * (1.0 + lax.erf(x * (1.0 / math.sqrt(2.0)))) + x * jnp.exp(-0.5 * x * x) * (1.0 / math.sqrt(2.0 * math.pi))


def _rms_fwd(x, n):
    r = lax.rsqrt(jnp.mean(x * x, axis=-1, keepdims=True) + EPS)
    xh = x * r
    return r, xh, xh * n


def _rms_bwd(dh, r, xh, n):
    dxh = dh * n
    dx = r * (dxh - xh * jnp.mean(dxh * xh, axis=-1, keepdims=True))
    return dx, jnp.sum(dh * xh, axis=0, keepdims=True)


MESH_ID = pl.DeviceIdType.MESH
_HBM = pl.BlockSpec(memory_space=pltpu.HBM)


def _my_place():
    return lax.axis_index("x"), lax.axis_index("y"), lax.axis_index("c")


def _ici_peers(x, y, c):
    return [((1 - x, y, c), 2 * (1 - x) + y), ((x, 1 - y, c), 2 * x + 1 - y), ((1 - x, 1 - y, c), 2 * (1 - x) + 1 - y)]


class _Hook:
    def __init__(self, operands, out_shapes, n_remote, n_local, start, finish):
        self.operands, self.out_shapes = list(operands), list(out_shapes)
        self.n_remote, self.n_local, self.start, self.finish = n_remote, n_local, start, finish


def _call(body, hooks, operands, *, in_specs, out_specs, out_shape, grid=None, scratch_shapes=(), **kw):
    hooks = tuple(hooks)
    n_in, n_out, n_scr = len(in_specs), len(out_shape), len(scratch_shapes)
    h_ops = [a for h in hooks for a in h.operands]
    h_outs = [s for h in hooks for s in h.out_shapes]
    h_sems = [pltpu.SemaphoreType.DMA((n,)) for h in hooks for n in (h.n_remote, h.n_remote, max(h.n_local, 1))]

    def wrapped(*refs):
        ins, hin = refs[:n_in], refs[n_in:n_in + len(h_ops)]
        o0 = n_in + len(h_ops)
        outs, hout = refs[o0:o0 + n_out], refs[o0 + n_out:o0 + n_out + len(h_outs)]
        s0 = o0 + n_out + len(h_outs)
        scr, hsem = refs[s0:s0 + n_scr], refs[s0 + n_scr:]

        def run(phase):
            ip = op = 0
            for i, h in enumerate(hooks):
                ssem, rsem, lsem = hsem[3 * i:3 * i + 3]

                def rcopy(k, src, dst, dev, ssem=ssem, rsem=rsem):
                    return pltpu.make_async_remote_copy(src_ref=src, dst_ref=dst, send_sem=ssem.at[k], recv_sem=rsem.at[k],
                                                        device_id=dev, device_id_type=MESH_ID)

                def lcopy(k, src, dst, lsem=lsem):
                    return pltpu.make_async_copy(src, dst, lsem.at[k])

                getattr(h, phase)(hin[ip:ip + len(h.operands)], hout[op:op + len(h.out_shapes)], rcopy, lcopy)
                ip += len(h.operands)
                op += len(h.out_shapes)

        def at_edge(phase, last):
            if not hooks:
                return
            if grid is None:
                run(phase)
                return
            cond = None
            for ax, n in enumerate(grid):
                here = pl.program_id(ax) == (n - 1 if last else 0)
                cond = here if cond is None else cond & here
            pl.when(cond)(lambda: run(phase))

        at_edge("start", False)
        body(*ins, *outs, *scr)
        at_edge("finish", True)

    if grid is not None:
        kw["grid"] = grid
    return list(pl.pallas_call(
        wrapped, out_shape=list(out_shape) + h_outs, in_specs=list(in_specs) + [_HBM] * len(h_ops),
        out_specs=list(out_specs) + [_HBM] * len(h_outs), scratch_shapes=list(scratch_shapes) + h_sems, **kw,
    )(*operands, *h_ops))


def _exchange(name, hook):
    return _call(lambda: None, [hook], [], name=name, in_specs=[], out_specs=[], out_shape=[])


def _rows(ref, start, n):
    return ref.at[pl.ds(start, n), :]


def _ag_hook(group, shard):
    names = GROUPS[group]
    nw = len(names)

    def blocks(full, dev_index):
        return [_rows(full, W_BASE[w] + dev_index * W_ROWS[w], W_ROWS[w]) for w in names]

    def mine(sh, full):
        x, y, c = _my_place()
        return [_rows(sh, W_OFF[w], W_ROWS[w]) for w in names], blocks(full, 4 * x + 2 * y + c)

    def start(ins, outs, rcopy, lcopy):
        x, y, c = _my_place()
        srcs, dsts = mine(ins[0], outs[0])
        for wi, (src, dst) in enumerate(zip(srcs, dsts)):
            lcopy(wi, src, dst).start()
            rcopy(wi, src, dst, (x, y, 1 - c)).start()
            for j, (dev, _) in enumerate(_ici_peers(x, y, c)):
                rcopy((1 + j) * nw + wi, src, dst, dev).start()

    def finish(ins, outs, rcopy, lcopy):
        x, y, c = _my_place()
        sib = (x, y, 1 - c)
        full = outs[0]
        peers = _ici_peers(x, y, c)
        for j, (dev, chip) in enumerate(peers):
            for wi, blk in enumerate(blocks(full, 2 * chip + c)):
                rcopy((1 + j) * nw + wi, blk, blk, dev).wait_recv()
                rcopy((4 + j) * nw + wi, blk, blk, sib).start()
        for wi, blk in enumerate(blocks(full, 2 * (2 * x + y) + 1 - c)):
            rcopy(wi, blk, blk, sib).wait_recv()
        for j, (dev, chip) in enumerate(peers):
            for wi, blk in enumerate(blocks(full, 2 * chip + 1 - c)):
                rcopy((4 + j) * nw + wi, blk, blk, sib).wait_recv()
        srcs, dsts = mine(ins[0], full)
        for wi, (src, dst) in enumerate(zip(srcs, dsts)):
            lcopy(wi, src, dst).wait()
            rcopy(wi, src, dst, sib).wait_send()
            for j, (dev, chip) in enumerate(peers):
                rcopy((1 + j) * nw + wi, src, dst, dev).wait_send()
        for j, (dev, chip) in enumerate(peers):
            for wi, blk in enumerate(blocks(full, 2 * chip + c)):
                rcopy((4 + j) * nw + wi, blk, blk, sib).wait_send()

    return _Hook([shard], [jax.ShapeDtypeStruct((NDEV * G_ROWS[group], D), shard.dtype)], 7 * nw, nw, start, finish)


def _rs_d2d_hook(group, gfull):
    names = GROUPS[group]
    nw = len(names)

    def pairs(g, land):
        x, y, c = _my_place()
        out = []
        for wi, w in enumerate(names):
            rows = W_ROWS[w]
            for k in range(NCHIP):
                out.append((k * nw + wi, _rows(g, W_BASE[w] + (2 * k + 1 - c) * rows, rows), land.at[k, pl.ds(W_OFF[w], rows), :]))
        return (x, y, 1 - c), out

    def start(ins, outs, rcopy, lcopy):
        sib, cps = pairs(ins[0], outs[0])
        for i, src, dst in cps:
            rcopy(i, src, dst, sib).start()

    def finish(ins, outs, rcopy, lcopy):
        sib, cps = pairs(ins[0], outs[0])
        for i, src, dst in cps:
            rcopy(i, dst, dst, sib).wait_recv()
        for i, src, dst in cps:
            rcopy(i, src, dst, sib).wait_send()

    return _Hook([gfull], [jax.ShapeDtypeStruct((NCHIP, G_ROWS[group], D), gfull.dtype)], NCHIP * nw, 0, start, finish)


def _rs_ici_hook(part):
    def start(ins, outs, rcopy, lcopy):
        x, y, c = _my_place()
        mychip = 2 * x + y
        lcopy(0, ins[0].at[mychip], outs[0].at[mychip]).start()
        for j, (dev, chip) in enumerate(_ici_peers(x, y, c)):
            rcopy(j, ins[0].at[chip], outs[0].at[mychip], dev).start()

    def finish(ins, outs, rcopy, lcopy):
        x, y, c = _my_place()
        mychip = 2 * x + y
        peers = _ici_peers(x, y, c)
        for j, (dev, chip) in enumerate(peers):
            rcopy(j, outs[0].at[chip], outs[0].at[chip], dev).wait_recv()
        for j, (dev, chip) in enumerate(peers):
            rcopy(j, ins[0].at[chip], outs[0].at[mychip], dev).wait_send()
        lcopy(0, ins[0].at[mychip], outs[0].at[mychip]).wait()

    return _Hook([part], [jax.ShapeDtypeStruct(part.shape, part.dtype)], 3, 1, start, finish)


def _small_hook(a, b):
    def peers():
        x, y, c = _my_place()
        out = []
        for dx in range(2):
            for dy in range(2):
                for dc in range(2):
                    if dx + dy + dc:
                        px, py, pc = (1 - x if dx else x), (1 - y if dy else y), (1 - c if dc else c)
                        out.append(((px, py, pc), 4 * px + 2 * py + pc))
        return 4 * x + 2 * y + c, out

    def start(ins, outs, rcopy, lcopy):
        me, ps = peers()
        for t in range(2):
            lcopy(t, ins[t], outs[t].at[me]).start()
            for i, (dev, _) in enumerate(ps):
                rcopy(2 * i + t, ins[t], outs[t].at[me], dev).start()

    def finish(ins, outs, rcopy, lcopy):
        me, ps = peers()
        for t in range(2):
            for i, (dev, peer) in enumerate(ps):
                rcopy(2 * i + t, outs[t].at[peer], outs[t].at[peer], dev).wait_recv()
            for i, (dev, _) in enumerate(ps):
                rcopy(2 * i + t, ins[t], outs[t].at[me], dev).wait_send()
            lcopy(t, ins[t], outs[t].at[me]).wait()

    return _Hook([a, b], [jax.ShapeDtypeStruct((NDEV,) + a.shape, a.dtype), jax.ShapeDtypeStruct((NDEV,) + b.shape, b.dtype)],
                 14, 2, start, finish)


def _wblock(w):
    rows = NDEV * W_ROWS[w]
    assert W_BASE[w] % rows == 0
    return _resident((rows, D), lambda *_: (W_BASE[w] // rows, 0))


def _ffn_fwd(name, x, nrm, wfull, ffn, hooks=(), tm=512):
    S = x.shape[0]
    wg, wu, wd = ffn + "_w_gate", ffn + "_w_up", ffn + "_w_down"

    def body(x_ref, n_ref, wg_ref, wu_ref, wd_ref, y_ref, g_ref, u_ref, acc_ref):
        xv = x_ref[...]
        _, _, h = _rms_fwd(xv, n_ref[...])
        h = h.astype(CDT)
        for ci in range(F // FT):
            sl = slice(ci * FT, (ci + 1) * FT)
            g = _mm_nt(h, wg_ref[sl, :])
            u = _mm_nt(h, wu_ref[sl, :])
            g_ref[:, sl] = g.astype(CDT)
            u_ref[:, sl] = u.astype(CDT)
            a = (g * jax.nn.sigmoid(g) * u).astype(CDT)
            o = _mm(a, wd_ref[sl, :])
            if ci == 0:
                acc_ref[...] = o
            else:
                acc_ref[...] += o
        y_ref[...] = xv + 0.5 * acc_ref[...]

    return _call(
        body, hooks, [x, nrm, wfull, wfull, wfull], name=name, grid=(S // tm,),
        out_shape=[jax.ShapeDtypeStruct((S, D), f32), jax.ShapeDtypeStruct((S, F), CDT), jax.ShapeDtypeStruct((S, F), CDT)],
        in_specs=[pl.BlockSpec((tm, D), lambda i: (i, 0)), _resident((1, D), lambda i: (0, 0)),
                  _wblock(wg), _wblock(wu), _wblock(wd)],
        out_specs=[pl.BlockSpec((tm, D), lambda i: (i, 0)), pl.BlockSpec((tm, F), lambda i: (i, 0)),
                   pl.BlockSpec((tm, F), lambda i: (i, 0))],
        scratch_shapes=[pltpu.VMEM((tm, D), f32)],
        compiler_params=_params("arbitrary"),
    )


def _proj_fwd(x1, nrm, wfull, b3, hooks=(), tm=512):
    S = x1.shape[0]

    def body(x_ref, n_ref, w_ref, b_ref, p_ref, h_ref):
        _, _, h = _rms_fwd(x_ref[...], n_ref[...])
        h = h.astype(CDT)
        h_ref[...] = h
        for p in range(8):
            seg = SEG_OF_SLOT[p]
            p_ref[p] = (_mm(h, w_ref[seg * D:(seg + 1) * D, :]) + b_ref[seg]).astype(CDT)

    return _call(
        body, hooks, [x1, nrm, wfull, b3], name="proj_fwd", grid=(S // tm,),
        out_shape=[jax.ShapeDtypeStruct((8, S, D), CDT), jax.ShapeDtypeStruct((S, D), CDT)],
        in_specs=[pl.BlockSpec((tm, D), lambda i: (i, 0)), _resident((1, D), lambda i: (0, 0)),
                  _resident((8 * D, D), lambda i: (0, 0)), _resident((8, 1, D), lambda i: (0, 0, 0))],
        out_specs=[pl.BlockSpec((8, tm, D), lambda i: (0, i, 0)), pl.BlockSpec((tm, D), lambda i: (i, 0))],
        compiler_params=_params("arbitrary"),
    )


def _sgu_norm(va, gn, bn):
    mu = jnp.mean(va, axis=-1, keepdims=True)
    xc = va - mu
    rstd = lax.rsqrt(jnp.mean(xc * xc, axis=-1, keepdims=True) + EPS)
    vhat = xc * rstd
    return rstd, vhat, vhat * gn + bn


def _sgu_fwd(proj, gn, bn, ws, bsc, tm=512):
    S = proj.shape[1]
    GW = D // G

    def body(p_ref, gn_ref, bn_ref, ws_ref, bs_ref, a_ref):
        ua = _gelu(p_ref[0].astype(f32))
        va = _gelu(p_ref[1].astype(f32))
        _, _, vn = _sgu_norm(va, gn_ref[...], bn_ref[...])
        vn = vn.astype(CDT)
        for ch in range(tm // C):
            rs = slice(ch * C, (ch + 1) * C)
            for gi in range(G):
                cs = slice(gi * GW, (gi + 1) * GW)
                s = _mm(ws_ref[gi], vn[rs, cs]) + bs_ref[gi]
                a_ref[rs, cs] = (ua[rs, cs] * s).astype(CDT)

    return pl.pallas_call(
        body, name="sgu_fwd", grid=(S // tm,),
        out_shape=jax.ShapeDtypeStruct((S, D), CDT),
        in_specs=[pl.BlockSpec((2, tm, D), lambda i: (0, i, 0)), _resident((1, D), lambda i: (0, 0)),
                  _resident((1, D), lambda i: (0, 0)), _resident((G, C, C), lambda i: (0, 0, 0)),
                  _resident((G, C, 1), lambda i: (0, 0, 0))],
        out_specs=pl.BlockSpec((tm, D), lambda i: (i, 0)),
        compiler_params=_params("arbitrary"),
    )(proj, gn, bn, ws, bsc)


def _decay_tables(dl_ref):
    lg = jax.nn.log_sigmoid(dl_ref[0:2, :])
    lgf, lgb = lg[0:1, :], lg[1:2, :]
    assert RC <= DK
    ri = lax.broadcasted_iota(jnp.int32, (RC, RC), 0)
    ci = lax.broadcasted_iota(jnp.int32, (RC, RC), 1)
    d = (ri - ci).astype(f32)
    lower = d >= 0
    dmat = jnp.where(lower, jnp.exp(d * lgf[:, :RC]), jnp.exp(-d * lgb[:, :RC]))
    dmat_t = jnp.where(d <= 0, jnp.exp(-d * lgf[:, :RC]), jnp.exp(d * lgb[:, :RC]))
    pos = lax.broadcasted_iota(jnp.int32, (RC, DK), 0).astype(f32)
    t = dict(
        lgf=lgf, lgb=lgb, d=d, lower=lower, dmat=dmat, dmat_t=dmat_t, pos=pos,
        fq=jnp.exp((pos + 1.0) * lgf), fk=jnp.exp((RC - 1.0 - pos) * lgf),
        bq=jnp.exp((RC - pos) * lgb), bk=jnp.exp(pos * lgb),
        lamf=jnp.exp(float(RC) * lgf), lamb=jnp.exp(float(RC) * lgb),
    )
    return t


def _rotate(t, co, si):
    t1, t2 = t[:, :DK // 2], t[:, DK // 2:]
    return jnp.concatenate([t1 * co - t2 * si, t2 * co + t1 * si], axis=-1)


def _unrotate(t, co, si):
    t1, t2 = t[:, :DK // 2], t[:, DK // 2:]
    return jnp.concatenate([t1 * co + t2 * si, t2 * co - t1 * si], axis=-1)


K_SCALE = DK ** -0.5
ROT_ROWS = 512


def _rotate_into(q_ref, k_ref, cos_ref, sin_ref, qs_ref, ks_ref, S):
    for rt in range(S // ROT_ROWS):
        rs = slice(rt * ROT_ROWS, (rt + 1) * ROT_ROWS)
        co, si = cos_ref[rs, :], sin_ref[rs, :]
        qs_ref[rs, :] = _rotate(q_ref[rs, :].astype(f32), co, si).astype(CDT)
        ks_ref[rs, :] = (_rotate(k_ref[rs, :].astype(f32), co, si) * K_SCALE).astype(CDT)


def _ret_fwd(proj, cos, sin, dl, hooks=()):
    S = proj.shape[1]
    NC = S // RC

    def body(q_ref, k_ref, v_ref, g_ref, cos_ref, sin_ref, dl_ref, R_ref, r_ref, sfs_ref, sbs_ref,
             qs_ref, ks_ref, rb_ref, sf_ref, sb_ref):
        t = _decay_tables(dl_ref)
        _rotate_into(q_ref, k_ref, cos_ref, sin_ref, qs_ref, ks_ref, S)

        def chunk(n):
            rows = pl.ds(pl.multiple_of(n * RC, RC), RC)
            return rows, qs_ref[rows, :], ks_ref[rows, :], v_ref[rows, :]

        sf_ref[...] = jnp.zeros_like(sf_ref)
        sb_ref[...] = jnp.zeros_like(sb_ref)

        def step(i, carry):
            rows, qn, kn, vn = chunk(i)
            sc = _mm_nt(qn, kn) * t["dmat"]
            out = _mm(sc.astype(CDT), vn)
            sf = sf_ref[...]
            sfb = sf.astype(CDT)
            sfs_ref[i] = sfb
            R_ref[rows, :] = out + _mm((qn.astype(f32) * t["fq"]).astype(CDT), sfb)
            sf_ref[...] = sf * t["lamf"] + _mm_tn((kn.astype(f32) * t["fk"]).astype(CDT), vn)
            m = NC - 1 - i
            rows, qn, kn, vn = chunk(m)
            sb = sb_ref[...]
            sbb = sb.astype(CDT)
            sbs_ref[m] = sbb
            rb_ref[rows, :] = _mm((qn.astype(f32) * t["bq"]).astype(CDT), sbb)
            sb_ref[...] = sb * t["lamb"] + _mm_tn((kn.astype(f32) * t["bk"]).astype(CDT), vn)
            return carry

        lax.fori_loop(0, NC, step, 0)
        for rt in range(S // ROT_ROWS):
            rs = slice(rt * ROT_ROWS, (rt + 1) * ROT_ROWS)
            R = R_ref[rs, :] + rb_ref[rs, :]
            R_ref[rs, :] = R
            rn = R * lax.rsqrt(jnp.mean(R * R, axis=-1, keepdims=True) + EPS)
            g = g_ref[rs, :].astype(f32)
            r_ref[rs, :] = (rn * g * jax.nn.sigmoid(g)).astype(CDT)

    def seg(slot):
        return pl.BlockSpec((None, S, DK), lambda h: (slot, 0, h))

    states = jax.ShapeDtypeStruct((H, NC, DK, DK), CDT)
    state_blk = pl.BlockSpec((None, NC, DK, DK), lambda h: (h, 0, 0, 0))
    return _call(
        body, hooks, [proj, proj, proj, proj, cos, sin, dl], name="ret_fwd", grid=(H,),
        out_shape=[jax.ShapeDtypeStruct((S, H * DK), f32), jax.ShapeDtypeStruct((S, H * DK), CDT), states, states],
        in_specs=[seg(SLOT_Q), seg(SLOT_K), seg(SLOT_VR), seg(SLOT_GR),
                  _resident((S, DK // 2), lambda h: (0, 0)), _resident((S, DK // 2), lambda h: (0, 0)),
                  pl.BlockSpec((None, 8, DK), lambda h: (h, 0, 0))],
        out_specs=[pl.BlockSpec((S, DK), lambda h: (0, h)), pl.BlockSpec((S, DK), lambda h: (0, h)), state_blk, state_blk],
        scratch_shapes=[pltpu.VMEM((S, DK), CDT), pltpu.VMEM((S, DK), CDT), pltpu.VMEM((S, DK), f32),
                        pltpu.VMEM((DK, DK), f32), pltpu.VMEM((DK, DK), f32)],
        compiler_params=_params("arbitrary"),
    )


def _merge_fwd(a, r, proj, x1, wfull, tm=512):
    S = x1.shape[0]

    def body(a_ref, r_ref, gt_ref, x_ref, wa_ref, wb_ref, wo_ref, x2_ref, ya_ref, yb_ref):
        ya = _mm(a_ref[...], wa_ref[...])
        yb = _mm(r_ref[...], wb_ref[...])
        ya_ref[...] = ya.astype(CDT)
        yb_ref[...] = yb.astype(CDT)
        mix = jax.nn.sigmoid(gt_ref[0].astype(f32)) * ya + jax.nn.sigmoid(gt_ref[1].astype(f32)) * yb
        x2_ref[...] = x_ref[...] + _mm(mix.astype(CDT), wo_ref[...])

    tok = pl.BlockSpec((tm, D), lambda i: (i, 0))
    return pl.pallas_call(
        body, name="merge_fwd", grid=(S // tm,),
        out_shape=[jax.ShapeDtypeStruct((S, D), f32), jax.ShapeDtypeStruct((S, D), CDT), jax.ShapeDtypeStruct((S, D), CDT)],
        in_specs=[tok, tok, pl.BlockSpec((2, tm, D), lambda i: (SLOT_GA // 2, i, 0)), tok,
                  _wblock("w_branch_a"), _wblock("w_branch_b"), _wblock("w_out")],
        out_specs=[tok, tok, tok],
        compiler_params=_params("arbitrary"),
    )(a, r, proj, x1, wfull, wfull, wfull)


def _loss_head(x3, fn, target, tm=512):
    S = x3.shape[0]

    def body(x_ref, n_ref, t_ref, dx_ref, dn_ref, l_ref):
        n = n_ref[...]
        r, xh, y = _rms_fwd(x_ref[...], n)
        e = y - t_ref[...]
        dy = e * (1.0 / D)
        dx, dn = _rms_bwd(dy, r, xh, n)
        dx_ref[...] = dx
        part = 0.5 * jnp.sum(jnp.sum(e * e, axis=-1, keepdims=True), axis=0, keepdims=True) * (1.0 / D)

        @pl.when(pl.program_id(0) == 0)
        def _():
            dn_ref[...] = jnp.zeros_like(dn_ref)
            l_ref[...] = jnp.zeros_like(l_ref)

        dn_ref[...] += dn
        l_ref[...] += jnp.broadcast_to(part, l_ref.shape)

    tok = pl.BlockSpec((tm, D), lambda i: (i, 0))
    return pl.pallas_call(
        body, name="loss_head", grid=(S // tm,),
        out_shape=[jax.ShapeDtypeStruct((S, D), f32), jax.ShapeDtypeStruct((1, D), f32), jax.ShapeDtypeStruct((8, 128), f32)],
        in_specs=[tok, _resident((1, D), lambda i: (0, 0)), tok],
        out_specs=[tok, pl.BlockSpec((1, D), lambda i: (0, 0)), pl.BlockSpec((8, 128), lambda i: (0, 0))],
        compiler_params=_params("arbitrary"),
    )(x3, fn, target)


def _ffn_bwd_act(name, dy, x, g, u, nrm, wfull, ffn, hooks=(), tm=256):
    S = x.shape[0]
    wg, wu, wd = ffn + "_w_gate", ffn + "_w_up", ffn + "_w_down"

    def body(dy_ref, x_ref, g_ref, u_ref, n_ref, wg_ref, wu_ref, wd_ref,
             dx_ref, dg_ref, du_ref, a_ref, h_ref, dyh_ref, dn_ref, acc_ref):
        n = n_ref[...]
        dyv = dy_ref[...]
        r, xh, h = _rms_fwd(x_ref[...], n)
        h_ref[...] = h.astype(CDT)
        dyh = (0.5 * dyv).astype(CDT)
        dyh_ref[...] = dyh
        for ci in range(F // FT):
            sl = slice(ci * FT, (ci + 1) * FT)
            da = _mm_nt(dyh, wd_ref[sl, :])
            gv = g_ref[:, sl].astype(f32)
            uv = u_ref[:, sl].astype(f32)
            s = jax.nn.sigmoid(gv)
            silu = gv * s
            a_ref[:, sl] = (silu * uv).astype(CDT)
            du = (da * silu).astype(CDT)
            dg = (da * uv * (s * (1.0 + gv * (1.0 - s)))).astype(CDT)
            du_ref[:, sl] = du
            dg_ref[:, sl] = dg
            dh = _mm(dg, wg_ref[sl, :]) + _mm(du, wu_ref[sl, :])
            if ci == 0:
                acc_ref[...] = dh
            else:
                acc_ref[...] += dh
        dx, dn = _rms_bwd(acc_ref[...], r, xh, n)
        dx_ref[...] = dyv + dx

        @pl.when(pl.program_id(0) == 0)
        def _():
            dn_ref[...] = jnp.zeros_like(dn_ref)

        dn_ref[...] += dn

    tok = pl.BlockSpec((tm, D), lambda i: (i, 0))
    hid = pl.BlockSpec((tm, F), lambda i: (i, 0))
    return _call(
        body, hooks, [dy, x, g, u, nrm, wfull, wfull, wfull], name=name, grid=(S // tm,),
        out_shape=[jax.ShapeDtypeStruct((S, D), f32), jax.ShapeDtypeStruct((S, F), CDT), jax.ShapeDtypeStruct((S, F), CDT),
                   jax.ShapeDtypeStruct((S, F), CDT), jax.ShapeDtypeStruct((S, D), CDT), jax.ShapeDtypeStruct((S, D), CDT),
                   jax.ShapeDtypeStruct((1, D), f32)],
        in_specs=[tok, tok, hid, hid, _resident((1, D), lambda i: (0, 0)), _wblock(wg), _wblock(wu), _wblock(wd)],
        out_specs=[tok, hid, hid, hid, tok, tok, pl.BlockSpec((1, D), lambda i: (0, 0))],
        scratch_shapes=[pltpu.VMEM((tm, D), f32)],
        compiler_params=_params("arbitrary"),
    )


TN_ROWS = 256


def _tn_into(name, group, gfull, xs, ys, row_block_of, hooks=()):
    S, M = xs.shape
    B = ys.shape[0]

    def body(*refs):
        x_ref, y_ref, o_ref = refs[0], refs[1], refs[-1]
        o_ref[...] = _mm_tn(x_ref[...], y_ref[...]).astype(CDT)

    in_specs = [pl.BlockSpec((S, TN_ROWS), lambda b, i: (0, i)), pl.BlockSpec((None, S, D), lambda b, i: (b, 0, 0))]
    operands = [xs, ys]
    aliases = {}
    if gfull is not None:
        in_specs.append(_HBM)
        operands.append(gfull)
        aliases = {2: 0}
    return _call(
        body, hooks, operands, name=name, grid=(B, M // TN_ROWS),
        out_shape=[jax.ShapeDtypeStruct((NDEV * G_ROWS[group], D), CDT)],
        in_specs=in_specs,
        out_specs=[pl.BlockSpec((TN_ROWS, D), lambda b, i: (row_block_of(b, i), 0))],
        input_output_aliases=aliases,
        compiler_params=_params("arbitrary", "arbitrary"),
    )


def _wgrad(name, group, gfull, w, xs, y, hooks=()):
    base = W_BASE[w] // TN_ROWS
    return _tn_into(name, group, gfull, xs, y[None], lambda b, i: base + i, hooks)


def _merge_bwd_act(dx2, ya, yb, proj, wfull, hooks=(), tm=512):
    S = dx2.shape[0]

    def body(dx_ref, ya_ref, yb_ref, gt_ref, wa_ref, wb_ref, wo_ref,
             dp_ref, da_ref, dr_ref, mix_ref, dxb_ref, dya_ref, dyb_ref):
        dxb = dx_ref[...].astype(CDT)
        dxb_ref[...] = dxb
        dmix = _mm_nt(dxb, wo_ref[...])
        ya = ya_ref[...].astype(f32)
        yb = yb_ref[...].astype(f32)
        sa = jax.nn.sigmoid(gt_ref[0].astype(f32))
        sb = jax.nn.sigmoid(gt_ref[1].astype(f32))
        mix_ref[...] = (sa * ya + sb * yb).astype(CDT)
        dya = (dmix * sa).astype(CDT)
        dyb = (dmix * sb).astype(CDT)
        dya_ref[...] = dya
        dyb_ref[...] = dyb
        dp_ref[0] = (dmix * ya * sa * (1.0 - sa)).astype(CDT)
        dp_ref[1] = (dmix * yb * sb * (1.0 - sb)).astype(CDT)
        da_ref[...] = _mm_nt(dya, wa_ref[...]).astype(CDT)
        dr_ref[...] = _mm_nt(dyb, wb_ref[...]).astype(CDT)

    tok = pl.BlockSpec((tm, D), lambda i: (i, 0))
    gates = pl.BlockSpec((2, tm, D), lambda i: (SLOT_GA // 2, i, 0))
    act = jax.ShapeDtypeStruct((S, D), CDT)
    return _call(
        body, hooks, [dx2, ya, yb, proj, wfull, wfull, wfull], name="merge_bwd_act", grid=(S // tm,),
        out_shape=[jax.ShapeDtypeStruct((8, S, D), CDT), act, act, act, act, act, act],
        in_specs=[tok, tok, tok, gates, _wblock("w_branch_a"), _wblock("w_branch_b"), _wblock("w_out")],
        out_specs=[gates, tok, tok, tok, tok, tok, tok],
        compiler_params=_params("arbitrary"),
    )


def _sgu_bwd(da, proj, dproj, gn, bn, ws, wst, bsc, hooks=(), tm=512):
    S = proj.shape[1]
    GW = D // G

    def body(da_ref, p_ref, dpin_ref, gn_ref, bn_ref, ws_ref, wst_ref, bs_ref,
             dp_ref, dws_ref, dbs_ref, dgn_ref, dbn_ref, ds_ref, dvn_ref):
        @pl.when(pl.program_id(0) == 0)
        def _():
            dws_ref[...] = jnp.zeros_like(dws_ref)
            dbs_ref[...] = jnp.zeros_like(dbs_ref)
            dgn_ref[...] = jnp.zeros_like(dgn_ref)
            dbn_ref[...] = jnp.zeros_like(dbn_ref)

        pu = p_ref[0].astype(f32)
        pv = p_ref[1].astype(f32)
        ua = _gelu(pu)
        va = _gelu(pv)
        gn = gn_ref[...]
        rstd, vhat, vn = _sgu_norm(va, gn, bn_ref[...])
        vnb = vn.astype(CDT)
        dav = da_ref[...].astype(f32)
        dsb = (dav * ua).astype(CDT)
        ones = jnp.ones((8, GW), CDT)
        for ch in range(tm // C):
            rs = slice(ch * C, (ch + 1) * C)
            for gi in range(G):
                cs = slice(gi * GW, (gi + 1) * GW)
                s = _mm(ws_ref[gi], vnb[rs, cs]) + bs_ref[gi]
                ds_ref[rs, cs] = s
                dsg = dsb[rs, cs]
                dws_ref[gi] += _mm_nt(dsg, vnb[rs, cs])
                dbs_ref[gi] += _mm_nt(ones, dsg)
                dvn_ref[rs, cs] = _mm(wst_ref[gi], dsg)
        dp_ref[0] = (dav * ds_ref[...] * _gelu_grad(pu)).astype(CDT)
        dvn = dvn_ref[...]
        dgn_ref[...] += jnp.sum(dvn * vhat, axis=0, keepdims=True)
        dbn_ref[...] += jnp.sum(dvn, axis=0, keepdims=True)
        dvh = dvn * gn
        dva = rstd * (dvh - jnp.mean(dvh, axis=-1, keepdims=True) - vhat * jnp.mean(dvh * vhat, axis=-1, keepdims=True))
        dp_ref[1] = (dva * _gelu_grad(pv)).astype(CDT)

    uv = pl.BlockSpec((2, tm, D), lambda i: (0, i, 0))
    row = _resident((1, D), lambda i: (0, 0))
    return _call(
        body, hooks, [da, proj, dproj, gn, bn, ws, wst, bsc], name="sgu_bwd", grid=(S // tm,),
        out_shape=[jax.ShapeDtypeStruct(dproj.shape, CDT), jax.ShapeDtypeStruct((G, C, C), f32),
                   jax.ShapeDtypeStruct((G, 8, C), f32), jax.ShapeDtypeStruct((1, D), f32), jax.ShapeDtypeStruct((1, D), f32)],
        in_specs=[pl.BlockSpec((tm, D), lambda i: (i, 0)), uv, _HBM, row, row,
                  _resident((G, C, C), lambda i: (0, 0, 0)), _resident((G, C, C), lambda i: (0, 0, 0)),
                  _resident((G, C, 1), lambda i: (0, 0, 0))],
        out_specs=[uv, pl.BlockSpec((G, C, C), lambda i: (0, 0, 0)), pl.BlockSpec((G, 8, C), lambda i: (0, 0, 0)),
                   pl.BlockSpec((1, D), lambda i: (0, 0)), pl.BlockSpec((1, D), lambda i: (0, 0))],
        scratch_shapes=[pltpu.VMEM((tm, D), f32), pltpu.VMEM((tm, D), f32)],
        input_output_aliases={2: 0},
        compiler_params=_params("arbitrary"),
    )


def _ret_bwd(dr, R, sfs, sbs, proj, dproj, cos, sin, dl, hooks=()):
    S = proj.shape[1]
    NC = S // RC
    assert NC % 2 == 0

    def body(dr_ref, R_ref, sf_ref, sb_ref, q_ref, k_ref, v_ref, g_ref, dpin_ref, cos_ref, sin_ref, dl_ref,
             dp_ref, dd_ref, qs_ref, ks_ref, dR_ref, dk_ref, dv_ref, gb_ref, gf_ref, acc_ref):
        t = _decay_tables(dl_ref)
        _rotate_into(q_ref, k_ref, cos_ref, sin_ref, qs_ref, ks_ref, S)
        for rt in range(S // ROT_ROWS):
            rs = slice(rt * ROT_ROWS, (rt + 1) * ROT_ROWS)
            Rv = R_ref[rs, :]
            rstd = lax.rsqrt(jnp.mean(Rv * Rv, axis=-1, keepdims=True) + EPS)
            rn = Rv * rstd
            gv = g_ref[rs, :].astype(f32)
            s = jax.nn.sigmoid(gv)
            drv = dr_ref[rs, :].astype(f32)
            dp_ref[3, rs, :] = (drv * rn * (s * (1.0 + gv * (1.0 - s)))).astype(CDT)
            drn = drv * gv * s
            dR_ref[rs, :] = (rstd * (drn - rn * jnp.mean(drn * rn, axis=-1, keepdims=True))).astype(CDT)

        def chunk(n):
            rows = pl.ds(pl.multiple_of(n * RC, RC), RC)
            return rows, qs_ref[rows, :], ks_ref[rows, :], v_ref[rows, :], dR_ref[rows, :]

        def emit_kv(rows, dk, dv, final):
            if not final:
                dk_ref[rows, :] = dk
                dv_ref[rows, :] = dv
            else:
                co, si = cos_ref[rows, :], sin_ref[rows, :]
                dp_ref[1, rows, :] = (_unrotate(dk_ref[rows, :] + dk, co, si) * K_SCALE).astype(CDT)
                dp_ref[2, rows, :] = (dv_ref[rows, :] + dv).astype(CDT)

        gb_ref[...] = jnp.zeros_like(gb_ref)
        gf_ref[...] = jnp.zeros_like(gf_ref)
        acc_ref[...] = jnp.zeros_like(acc_ref)
        dpos = jnp.abs(t["d"])

        def ascend(n, final):
            rows, qn, kn, vn, dRn = chunk(n)
            qf, kf = qn.astype(f32), kn.astype(f32)
            sc = _mm_nt(qn, kn)
            dA = _mm_nt(dRn, vn)
            w = sc * dA * t["dmat"] * dpos
            lgf_part = jnp.sum(jnp.where(t["lower"], w, 0.0), axis=0, keepdims=True)
            lgb_part = jnp.sum(jnp.where(t["lower"], 0.0, w), axis=0, keepdims=True)
            dsc = (dA * t["dmat"]).astype(CDT)
            dq = _mm(dsc, kn)
            scT = (_mm_nt(kn, qn) * t["dmat_t"]).astype(CDT)
            dscT = (_mm_nt(vn, dRn) * t["dmat_t"]).astype(CDT)
            dk = _mm(dscT, qn)
            dv = _mm(scT, dRn)
            sfb = sf_ref[n]
            sbb = sb_ref[n]
            qdf = qf * t["fq"]
            dqdf = _mm_nt(dRn, sfb)
            dq += dqdf * t["fq"]
            lgf_row = jnp.sum(qdf * dqdf * (t["pos"] + 1.0), axis=0, keepdims=True)
            qdb = qf * t["bq"]
            dqdb = _mm_nt(dRn, sbb)
            dq += dqdb * t["bq"]
            lgb_row = jnp.sum(qdb * dqdb * (RC - t["pos"]), axis=0, keepdims=True)
            gb = gb_ref[...]
            gbb = gb.astype(CDT)
            kdb = kf * t["bk"]
            dkdb = _mm_nt(vn, gbb)
            dk += dkdb * t["bk"]
            dv += _mm(kdb.astype(CDT), gbb)
            lgb_row += jnp.sum(kdb * dkdb * t["pos"], axis=0, keepdims=True)
            lgb_row += float(RC) * t["lamb"] * jnp.sum(gb * sbb.astype(f32), axis=0, keepdims=True)
            co, si = cos_ref[rows, :], sin_ref[rows, :]
            dp_ref[0, rows, :] = _unrotate(dq, co, si).astype(CDT)
            emit_kv(rows, dk, dv, final)
            acc_ref[0:1, :] += lgf_row + lgf_part
            acc_ref[1:2, :] += lgb_row + lgb_part
            gb_ref[...] = gb * t["lamb"] + _mm_tn(qdb.astype(CDT), dRn)

        def descend(n, final):
            rows, qn, kn, vn, dRn = chunk(n)
            gf = gf_ref[...]
            gfb = gf.astype(CDT)
            kdf = kn.astype(f32) * t["fk"]
            dkdf = _mm_nt(vn, gfb)
            lgf_row = jnp.sum(kdf * dkdf * (RC - 1.0 - t["pos"]), axis=0, keepdims=True)
            lgf_row += float(RC) * t["lamf"] * jnp.sum(gf * sf_ref[n].astype(f32), axis=0, keepdims=True)
            acc_ref[0:1, :] += lgf_row
            emit_kv(rows, dkdf * t["fk"], _mm(kdf.astype(CDT), gfb), final)
            gf_ref[...] = gf * t["lamf"] + _mm_tn((qn.astype(f32) * t["fq"]).astype(CDT), dRn)

        def sweep(final):
            def step(i, carry):
                ascend(i, final)
                descend(NC - 1 - i, final)
                return carry
            return step

        lax.fori_loop(0, NC // 2, sweep(False), 0)
        lax.fori_loop(NC // 2, NC, sweep(True), 0)
        dlg = jnp.sum(acc_ref[...], axis=1, keepdims=True)
        dlogit = dlg * jax.nn.sigmoid(-dl_ref[:, 0:1])
        lane = lax.broadcasted_iota(jnp.int32, (8, 128), 1)
        dd_ref[...] = jnp.where(lane == pl.program_id(0), jnp.broadcast_to(dlogit, (8, 128)), 0.0)

    def seg(slot):
        return pl.BlockSpec((None, S, DK), lambda h: (slot, 0, h), pipeline_mode=pl.Buffered(1))

    head = pl.BlockSpec((S, DK), lambda h: (0, h), pipeline_mode=pl.Buffered(1))
    states = pl.BlockSpec((None, NC, DK, DK), lambda h: (h, 0, 0, 0), pipeline_mode=pl.Buffered(1))
    return _call(
        body, hooks, [dr, R, sfs, sbs, proj, proj, proj, proj, dproj, cos, sin, dl], name="ret_bwd", grid=(H,),
        out_shape=[jax.ShapeDtypeStruct(dproj.shape, CDT), jax.ShapeDtypeStruct((H, 8, 128), f32)],
        in_specs=[head, head, states, states, seg(SLOT_Q), seg(SLOT_K), seg(SLOT_VR), seg(SLOT_GR), _HBM,
                  _resident((S, DK // 2), lambda h: (0, 0)), _resident((S, DK // 2), lambda h: (0, 0)),
                  pl.BlockSpec((None, 8, DK), lambda h: (h, 0, 0))],
        out_specs=[pl.BlockSpec((4, S, DK), lambda h: (1, 0, h), pipeline_mode=pl.Buffered(1)),
                   pl.BlockSpec((None, 8, 128), lambda h: (h, 0, 0))],
        scratch_shapes=[pltpu.VMEM((S, DK), CDT), pltpu.VMEM((S, DK), CDT), pltpu.VMEM((S, DK), CDT),
                        pltpu.VMEM((S, DK), f32), pltpu.VMEM((S, DK), f32),
                        pltpu.VMEM((DK, DK), f32), pltpu.VMEM((DK, DK), f32), pltpu.VMEM((8, DK), f32)],
        input_output_aliases={8: 0},
        compiler_params=_params("arbitrary"),
    )


def _proj_bwd_act(dproj, dx2, x1, nrm, wfull, hooks=(), tm=512):
    S = x1.shape[0]

    def body(dp_ref, dx2_ref, x_ref, n_ref, w_ref, dx_ref, dn_ref, db_ref, acc_ref):
        @pl.when(pl.program_id(0) == 0)
        def _():
            dn_ref[...] = jnp.zeros_like(dn_ref)
            db_ref[...] = jnp.zeros_like(db_ref)

        for p in range(8):
            seg = SEG_OF_SLOT[p]
            dp = dp_ref[p]
            db_ref[seg] += jnp.sum(dp.astype(f32), axis=0, keepdims=True)
            dh = _mm_nt(dp, w_ref[seg * D:(seg + 1) * D, :])
            if p == 0:
                acc_ref[...] = dh
            else:
                acc_ref[...] += dh
        n = n_ref[...]
        r, xh, _ = _rms_fwd(x_ref[...], n)
        dx, dn = _rms_bwd(acc_ref[...], r, xh, n)
        dx_ref[...] = dx2_ref[...] + dx
        dn_ref[...] += dn

    tok = pl.BlockSpec((tm, D), lambda i: (i, 0))
    return _call(
        body, hooks, [dproj, dx2, x1, nrm, wfull], name="proj_bwd_act", grid=(S // tm,),
        out_shape=[jax.ShapeDtypeStruct((S, D), f32), jax.ShapeDtypeStruct((1, D), f32), jax.ShapeDtypeStruct((8, 1, D), f32)],
        in_specs=[pl.BlockSpec((8, tm, D), lambda i: (0, i, 0)), tok, tok, _resident((1, D), lambda i: (0, 0)),
                  _resident((8 * D, D), lambda i: (0, 0))],
        out_specs=[tok, pl.BlockSpec((1, D), lambda i: (0, 0)), pl.BlockSpec((8, 1, D), lambda i: (0, 0, 0))],
        scratch_shapes=[pltpu.VMEM((tm, D), f32)],
        compiler_params=_params("arbitrary"),
    )


def _rs_sum(name, group, gfull, land, my_c):
    names = GROUPS[group]
    rows = W_ROWS[names[0]]
    assert all(W_ROWS[w] == rows for w in names)

    def body(c_ref, g_ref, l_ref, o_ref):
        o_ref[...] = (g_ref[...].astype(f32) + l_ref[...].astype(f32)).astype(CDT)

    return pl.pallas_call(
        body, name=name,
        grid_spec=pltpu.PrefetchScalarGridSpec(
            num_scalar_prefetch=1, grid=(len(names), NCHIP),
            in_specs=[pl.BlockSpec((rows, D), lambda wi, k, c: (NDEV * wi + 2 * k + c[0], 0)),
                      pl.BlockSpec((None, rows, D), lambda wi, k, c: (k, wi, 0))],
            out_specs=pl.BlockSpec((None, rows, D), lambda wi, k, c: (k, wi, 0))),
        out_shape=jax.ShapeDtypeStruct((NCHIP, G_ROWS[group], D), CDT),
        compiler_params=_params("arbitrary", "arbitrary"),
    )(my_c, gfull, land)


def _adamw_math(g, w, m, v):
    m2 = ADAM_B1 * m + (1.0 - ADAM_B1) * g
    v2 = ADAM_B2 * v + (1.0 - ADAM_B2) * (g * g)
    delta = -ADAM_LR * ((m2 / BC1) / (jnp.sqrt(v2 / BC2) + ADAM_EPS) + ADAM_WD * w)
    return delta, m2, v2


def _adamw_big(name, landed, wname, w, m, v, hooks=()):
    rows = W_ROWS[wname]
    tr = min(rows, 256) if rows % 256 == 0 else rows
    lb = W_OFF[wname] // tr
    assert W_OFF[wname] % tr == 0

    def body(l_ref, w_ref, m_ref, v_ref, g_ref, d_ref, m2_ref, v2_ref):
        g = l_ref[0].astype(f32)
        for k in range(1, NCHIP):
            g = g + l_ref[k].astype(f32)
        g_ref[...] = g
        d_ref[...], m2_ref[...], v2_ref[...] = _adamw_math(g, w_ref[...], m_ref[...], v_ref[...])

    blk = pl.BlockSpec((tr, D), lambda i: (i, 0))
    o = jax.ShapeDtypeStruct((rows, D), f32)
    return _call(
        body, hooks, [landed, w, m, v], name=name, grid=(rows // tr,), out_shape=[o, o, o, o],
        in_specs=[pl.BlockSpec((NCHIP, tr, D), lambda i: (0, lb + i, 0)), blk, blk, blk],
        out_specs=[blk, blk, blk, blk],
        compiler_params=_params("arbitrary"),
    )


ROW_FFN1_NORM, ROW_MIX_NORM, ROW_SGU_G, ROW_SGU_B, ROW_FFN2_NORM, ROW_FINAL_NORM, ROW_B_IN = 0, 1, 2, 3, 4, 5, 8
ROW_WS, ROW_BS, ROW_DECAY = 0, G * C, G * C + G * 8


def _adamw_small(ga, gb, params):
    def body(ga_ref, gb_ref, *refs):
        ins, outs = refs[:30], refs[30:]

        def total(ref, r0, n):
            g = ref[0, r0:r0 + n, :]
            for j in range(1, NDEV):
                g = g + ref[j, r0:r0 + n, :]
            return g

        def apply(i, g, rows=slice(None)):
            w, m, v = ins[3 * i][rows, :], ins[3 * i + 1][rows, :], ins[3 * i + 2][rows, :]
            outs[4 * i][rows, :] = g
            outs[4 * i + 1][rows, :], outs[4 * i + 2][rows, :], outs[4 * i + 3][rows, :] = _adamw_math(g, w, m, v)

        for i, r in enumerate((ROW_FFN1_NORM, ROW_MIX_NORM, ROW_SGU_G, ROW_SGU_B, ROW_FFN2_NORM, ROW_FINAL_NORM)):
            apply(i, total(ga_ref, r, 1))
        apply(6, total(ga_ref, ROW_B_IN, 8))
        apply(7, total(gb_ref, ROW_WS, G * C))
        for gi in range(G):
            apply(8, total(gb_ref, ROW_BS + 8 * gi, 1), slice(gi, gi + 1))
        dec = total(gb_ref, ROW_DECAY, 8)
        for hh in range(1, H):
            dec = dec + total(gb_ref, ROW_DECAY + 8 * hh, 8)
        apply(9, dec)

    flat = [a for p in params for a in p]
    out_shape = [jax.ShapeDtypeStruct(p[0].shape, f32) for p in params for _ in range(4)]
    vm = pl.BlockSpec(memory_space=pltpu.VMEM)
    return pl.pallas_call(
        body, name="adamw_small", out_shape=out_shape,
        in_specs=[vm] * (2 + len(flat)), out_specs=[vm] * len(out_shape),
        compiler_params=pltpu.CompilerParams(vmem_limit_bytes=VMEM_LIMIT),
    )(ga, gb, *flat)


def kernel(x, ffn1_norm, ffn1_w_gate, ffn1_w_up, ffn1_w_down, mix_norm, w_in, b_in, sgu_norm_g, sgu_norm_b, sgu_w_s, sgu_b_s, ret_decay_logit, w_branch_a, w_branch_b, w_out, ffn2_norm, ffn2_w_gate, ffn2_w_up, ffn2_w_down, final_norm, loss_target, m_ffn1_norm, m_ffn1_w_gate, m_ffn1_w_up, m_ffn1_w_down, m_mix_norm, m_w_in, m_b_in, m_sgu_norm_g, m_sgu_norm_b, m_sgu_w_s, m_sgu_b_s, m_ret_decay_logit, m_w_branch_a, m_w_branch_b, m_w_out, m_ffn2_norm, m_ffn2_w_gate, m_ffn2_w_up, m_ffn2_w_down, m_final_norm, v_ffn1_norm, v_ffn1_w_gate, v_ffn1_w_up, v_ffn1_w_down, v_mix_norm, v_w_in, v_b_in, v_sgu_norm_g, v_sgu_norm_b, v_sgu_w_s, v_sgu_b_s, v_ret_decay_logit, v_w_branch_a, v_w_branch_b, v_w_out, v_ffn2_norm, v_ffn2_w_gate, v_ffn2_w_up, v_ffn2_w_down, v_final_norm):
    args = dict(locals())
    S = x.shape[1]
    xs = x[0]
    target = loss_target[0]

    def buf_layout(name, a):
        a = a[0]
        return a.T if name in W_TRANSPOSED else a

    shard = {g: jnp.concatenate([buf_layout(n, args[n]).astype(CDT) for n in names], axis=0) for g, names in GROUPS.items()}

    b3 = b_in.reshape(8, 1, D)
    ws = sgu_w_s[0].astype(CDT)
    wst = jnp.swapaxes(sgu_w_s[0], 1, 2).astype(CDT)
    bsc = sgu_b_s[0].reshape(G, C, 1)
    dl = jnp.zeros((H, 8, DK), f32).at[:, 0:2, :].set(jnp.broadcast_to(ret_decay_logit[0].T[:, :, None], (H, 2, DK)))
    theta = ROPE_BASE ** (-jnp.arange(0, DK, 2, dtype=f32) / DK)
    ang = jnp.arange(S, dtype=f32)[:, None] * theta[None, :]
    cos, sin = jnp.cos(ang), jnp.sin(ang)
    fnorm = final_norm.reshape(1, D)

    (wf_ffn1,) = _exchange("ag_ffn1", _ag_hook("ffn1", shard["ffn1"]))
    x1, g1, u1, wf_win = _ffn_fwd("ffn1_fwd", xs, ffn1_norm, wf_ffn1, "ffn1", [_ag_hook("win", shard["win"])])
    proj, h2, wf_ffn2 = _proj_fwd(x1, mix_norm, wf_win, b3, [_ag_hook("ffn2", shard["ffn2"])])
    a = _sgu_fwd(proj, sgu_norm_g, sgu_norm_b, ws, bsc)
    R, r, sfs, sbs, wf_br = _ret_fwd(proj, cos, sin, dl, [_ag_hook("br", shard["br"])])
    x2, ya, yb = _merge_fwd(a, r, proj, x1, wf_br)
    x3, g2, u2 = _ffn_fwd("ffn2_fwd", x2, ffn2_norm, wf_ffn2, "ffn2")
    dx3, d_final, loss_part = _loss_head(x3, fnorm, target)
    loss = lax.psum(loss_part[0, 0], ("x", "y", "c"))

    my_c = lax.axis_index("c").astype(jnp.int32).reshape(1)
    dx2, dg2, du2, a2, hf2, dyh2, d_ffn2n = _ffn_bwd_act("ffn2_bwd_act", dx3, x2, g2, u2, ffn2_norm, wf_ffn2, "ffn2")
    (gf2,) = _wgrad("ffn2_wd_grad", "ffn2", None, "ffn2_w_down", a2, dyh2)
    (gf2,) = _wgrad("ffn2_wg_grad", "ffn2", gf2, "ffn2_w_gate", dg2, hf2)
    (gf2,) = _wgrad("ffn2_wu_grad", "ffn2", gf2, "ffn2_w_up", du2, hf2)
    dproj, da, dr, mix, dx2b, dya, dyb, sib2 = _merge_bwd_act(dx2, ya, yb, proj, wf_br, [_rs_d2d_hook("ffn2", gf2)])
    part2 = _rs_sum("rs_sum_ffn2", "ffn2", gf2, sib2, my_c)
    (gfb,) = _wgrad("wo_grad", "br", None, "w_out", mix, dx2b)
    (gfb,) = _wgrad("wa_grad", "br", gfb, "w_branch_a", a, dya)
    (gfb,) = _wgrad("wb_grad", "br", gfb, "w_branch_b", r, dyb)
    dproj, d_ws, d_bs, d_gn, d_bn, sibb = _sgu_bwd(da, proj, dproj, sgu_norm_g, sgu_norm_b, ws, wst, bsc,
                                                   [_rs_d2d_hook("br", gfb)])
    partb = _rs_sum("rs_sum_br", "br", gfb, sibb, my_c)
    dproj, d_dec, land2, landb = _ret_bwd(dr, R, sfs, sbs, proj, dproj, cos, sin, dl,
                                          [_rs_ici_hook(part2), _rs_ici_hook(partb)])
    (gfw,) = _tn_into("win_grad", "win", None, h2, dproj, lambda b, i: _seg_of_slot(b) * (D // TN_ROWS) + i)
    dx1, d_mixn, d_bin, sibw = _proj_bwd_act(dproj, dx2, x1, mix_norm, wf_win, [_rs_d2d_hook("win", gfw)])
    partw = _rs_sum("rs_sum_win", "win", gfw, sibw, my_c)
    dxs, dg1, du1, a1, hf1, dyh1, d_ffn1n, landw = _ffn_bwd_act("ffn1_bwd_act", dx1, xs, g1, u1, ffn1_norm, wf_ffn1, "ffn1",
                                                                [_rs_ici_hook(partw)])
    small_a = jnp.concatenate([d_ffn1n, d_mixn, d_gn, d_bn, d_ffn2n, d_final, jnp.zeros((2, D), f32), d_bin.reshape(8, D)], axis=0)
    small_b = jnp.concatenate([d_ws.reshape(G * C, C), d_bs.reshape(G * 8, C), d_dec.reshape(H * 8, 128)], axis=0)
    gf1, ga, gb = _wgrad("ffn1_wd_grad", "ffn1", None, "ffn1_w_down", a1, dyh1, [_small_hook(small_a, small_b)])
    (gf1,) = _wgrad("ffn1_wg_grad", "ffn1", gf1, "ffn1_w_gate", dg1, hf1)
    (gf1,) = _wgrad("ffn1_wu_grad", "ffn1", gf1, "ffn1_w_up", du1, hf1)
    (sib1,) = _exchange("rs_d2d_ffn1", _rs_d2d_hook("ffn1", gf1))
    part1 = _rs_sum("rs_sum_ffn1", "ffn1", gf1, sib1, my_c)

    out = {"loss": loss, "grad_x": dxs[None]}

    def native(name, a):
        a = a.T if name in W_TRANSPOSED else a
        return a[None]

    landed = dict(ffn2=land2, br=landb, win=landw)
    for g in ("win", "ffn2", "br", "ffn1"):
        for n in GROUPS[g]:
            hooks = [_rs_ici_hook(part1)] if n == "w_in" else []
            res = _adamw_big("adamw_" + n, landed[g], n, buf_layout(n, args[n]), buf_layout(n, args["m_" + n]),
                             buf_layout(n, args["v_" + n]), hooks)
            if hooks:
                landed["ffn1"] = res.pop()
            for pre, val in zip(("grad_", "delta_", "new_m_", "new_v_"), res):
                out[pre + n] = native(n, val)

    def pad_decay(a):
        return jnp.zeros((8, 128), f32).at[0:2, 0:H].set(a[0])

    small = [
        ("ffn1_norm", lambda a: a, lambda a: a), ("mix_norm", lambda a: a, lambda a: a),
        ("sgu_norm_g", lambda a: a, lambda a: a), ("sgu_norm_b", lambda a: a, lambda a: a),
        ("ffn2_norm", lambda a: a, lambda a: a),
        ("final_norm", lambda a: a.reshape(1, D), lambda a: a.reshape(D)),
        ("b_in", lambda a: a.reshape(8, D), lambda a: a.reshape(1, 8 * D)),
        ("sgu_w_s", lambda a: a.reshape(G * C, C), lambda a: a.reshape(1, G, C, C)),
        ("sgu_b_s", lambda a: a[0], lambda a: a[None]),
        ("ret_decay_logit", pad_decay, lambda a: a[None, 0:2, 0:H]),
    ]
    res = _adamw_small(ga, gb, [(to(args[n]), to(args["m_" + n]), to(args["v_" + n])) for n, to, _ in small])
    for i, (n, _, back) in enumerate(small):
        for j, pre in enumerate(("grad_", "delta_", "new_m_", "new_v_")):
            out[pre + n] = back(res[4 * i + j])

    weights = ("ffn1_norm", "ffn1_w_gate", "ffn1_w_up", "ffn1_w_down", "mix_norm", "w_in", "b_in", "sgu_norm_g",
               "sgu_norm_b", "sgu_w_s", "sgu_b_s", "ret_decay_logit", "w_branch_a", "w_branch_b", "w_out", "ffn2_norm",
               "ffn2_w_gate", "ffn2_w_up", "ffn2_w_down", "final_norm")
    return (out["loss"], out["grad_x"], *[out["grad_" + n] for n in weights], *[out["delta_" + n] for n in weights],
            *[out["new_m_" + n] for n in weights], *[out["new_v_" + n] for n in weights])
```

```python
import functools
import math

import jax
import jax.numpy as jnp
from jax import lax
from jax.experimental import pallas as pl
from jax.experimental.pallas import tpu as pltpu

f32 = jnp.float32
CDT = jnp.bfloat16

D = 1024
F = 2816
C = 128
RC = 256
H = 4
DK = 256
G = 4
NDEV = 8
NCHIP = 4
EPS = 1e-6
ROPE_BASE = 10000.0
FT = 256
V7X_VMEM_BYTES = 64 * 1024 * 1024
VMEM_LIMIT = V7X_VMEM_BYTES - 8 * 1024 * 1024

ADAM_LR, ADAM_B1, ADAM_B2, ADAM_EPS, ADAM_WD, ADAM_STEP = 0.001, 0.9, 0.999, 1e-08, 0.01, 10
BC1 = 1.0 - ADAM_B1 ** ADAM_STEP
BC2 = 1.0 - ADAM_B2 ** ADAM_STEP

W_ROWS = dict(ffn1_w_gate=352, ffn1_w_up=352, ffn1_w_down=352, w_in=1024, w_branch_a=128, w_branch_b=128, w_out=128,
              ffn2_w_gate=352, ffn2_w_up=352, ffn2_w_down=352)
W_NAMES = tuple(W_ROWS)
W_TRANSPOSED = ("ffn1_w_gate", "ffn1_w_up", "ffn2_w_gate", "ffn2_w_up")

SLOT_U, SLOT_V, SLOT_GA, SLOT_GB, SLOT_Q, SLOT_K, SLOT_VR, SLOT_GR = range(8)


SEG_OF_SLOT = (0, 1, 6, 7, 2, 3, 4, 5)


def _seg_of_slot(p):
    return jnp.where(p < 2, p, jnp.where(p < 4, p + 4, p - 2))


def _mm(a, b):
    return jnp.dot(a, b, preferred_element_type=f32)


def _mm_nt(a, b):
    return lax.dot_general(a, b, (((1,), (1,)), ((), ())), preferred_element_type=f32)


def _mm_tn(a, b):
    return lax.dot_general(a, b, (((0,), (0,)), ((), ())), preferred_element_type=f32)


def _params(*sem):
    return pltpu.CompilerParams(dimension_semantics=sem, vmem_limit_bytes=VMEM_LIMIT)


def _resident(shape, index_map):
    return pl.BlockSpec(shape, index_map, pipeline_mode=pl.Buffered(1))


def _gelu(x):
    return 0.5 * x * (1.0 + lax.erf(x * (1.0 / math.sqrt(2.0))))


def _gelu_grad(x):
    return 0.5 * (1.0 + lax.erf(x * (1.0 / math.sqrt(2.0)))) + x * jnp.exp(-0.5 * x * x) * (1.0 / math.sqrt(2.0 * math.pi))


def _rms_fwd(x, n):
    r = lax.rsqrt(jnp.mean(x * x, axis=-1, keepdims=True) + EPS)
    xh = x * r
    return r, xh, xh * n


def _rms_bwd(dh, r, xh, n):
    dxh = dh * n
    dx = r * (dxh - xh * jnp.mean(dxh * xh, axis=-1, keepdims=True))
    return dx, jnp.sum(dh * xh, axis=0, keepdims=True)


MESH_ID = pl.DeviceIdType.MESH
_HBM = pl.BlockSpec(memory_space=pltpu.HBM)


def _my_place():
    return lax.axis_index("x"), lax.axis_index("y"), lax.axis_index("c")


def _ici_peers(x, y, c):
    return [((1 - x, y, c), 2 * (1 - x) + y), ((x, 1 - y, c), 2 * x + 1 - y), ((1 - x, 1 - y, c), 2 * (1 - x) + 1 - y)]


class _Hook:
    def __init__(self, operands, out_shapes, n_remote, n_local, start, finish):
        self.operands, self.out_shapes = list(operands), list(out_shapes)
        self.n_remote, self.n_local, self.start, self.finish = n_remote, n_local, start, finish


def _call(body, hooks, operands, *, in_specs, out_specs, out_shape, grid=None, scratch_shapes=(), **kw):
    hooks = tuple(hooks)
    n_in, n_out, n_scr = len(in_specs), len(out_shape), len(scratch_shapes)
    h_ops = [a for h in hooks for a in h.operands]
    h_outs = [s for h in hooks for s in h.out_shapes]
    h_sems = [pltpu.SemaphoreType.DMA((n,)) for h in hooks for n in (h.n_remote, h.n_remote, max(h.n_local, 1))]

    def wrapped(*refs):
        ins, hin = refs[:n_in], refs[n_in:n_in + len(h_ops)]
        o0 = n_in + len(h_ops)
        outs, hout = refs[o0:o0 + n_out], refs[o0 + n_out:o0 + n_out + len(h_outs)]
        s0 = o0 + n_out + len(h_outs)
        scr, hsem = refs[s0:s0 + n_scr], refs[s0 + n_scr:]

        def run(phase):
            ip = op = 0
            for i, h in enumerate(hooks):
                ssem, rsem, lsem = hsem[3 * i:3 * i + 3]

                def rcopy(k, src, dst, dev, ssem=ssem, rsem=rsem):
                    return pltpu.make_async_remote_copy(src_ref=src, dst_ref=dst, send_sem=ssem.at[k], recv_sem=rsem.at[k],
                                                        device_id=dev, device_id_type=MESH_ID)

                def lcopy(k, src, dst, lsem=lsem):
                    return pltpu.make_async_copy(src, dst, lsem.at[k])

                getattr(h, phase)(hin[ip:ip + len(h.operands)], hout[op:op + len(h.out_shapes)], rcopy, lcopy)
                ip += len(h.operands)
                op += len(h.out_shapes)

        def at_edge(phase, last):
            if not hooks:
                return
            if grid is None:
                run(phase)
                return
            cond = None
            for ax, n in enumerate(grid):
                here = pl.program_id(ax) == (n - 1 if last else 0)
                cond = here if cond is None else cond & here
            pl.when(cond)(lambda: run(phase))

        at_edge("start", False)
        body(*ins, *outs, *scr)
        at_edge("finish", True)

    if grid is not None:
        kw["grid"] = grid
    return list(pl.pallas_call(
        wrapped, out_shape=list(out_shape) + h_outs, in_specs=list(in_specs) + [_HBM] * len(h_ops),
        out_specs=list(out_specs) + [_HBM] * len(h_outs), scratch_shapes=list(scratch_shapes) + h_sems, **kw,
    )(*operands, *h_ops))


def _exchange(name, hooks):
    return _call(lambda: None, hooks, [], name=name, in_specs=[], out_specs=[], out_shape=[])


def _rows(ref, start, n):
    return ref.at[pl.ds(start, n), :]


def _ag_hook(shard):
    rows = shard.shape[0]

    def block(full, dev_index):
        return _rows(full, dev_index * rows, rows)

    def start(ins, outs, rcopy, lcopy):
        x, y, c = _my_place()
        src, dst = ins[0], block(outs[0], 4 * x + 2 * y + c)
        lcopy(0, src, dst).start()
        rcopy(0, src, dst, (x, y, 1 - c)).start()
        for j, (dev, _) in enumerate(_ici_peers(x, y, c)):
            rcopy(1 + j, src, dst, dev).start()

    def finish(ins, outs, rcopy, lcopy):
        x, y, c = _my_place()
        sib = (x, y, 1 - c)
        full = outs[0]
        peers = _ici_peers(x, y, c)
        for j, (dev, chip) in enumerate(peers):
            blk = block(full, 2 * chip + c)
            rcopy(1 + j, blk, blk, dev).wait_recv()
            rcopy(4 + j, blk, blk, sib).start()
        blk = block(full, 2 * (2 * x + y) + 1 - c)
        rcopy(0, blk, blk, sib).wait_recv()
        for j, (dev, chip) in enumerate(peers):
            blk = block(full, 2 * chip + 1 - c)
            rcopy(4 + j, blk, blk, sib).wait_recv()
        src, dst = ins[0], block(full, 4 * x + 2 * y + c)
        lcopy(0, src, dst).wait()
        rcopy(0, src, dst, sib).wait_send()
        for j, (dev, chip) in enumerate(peers):
            rcopy(1 + j, src, dst, dev).wait_send()
            blk = block(full, 2 * chip + c)
            rcopy(4 + j, blk, blk, sib).wait_send()

    return _Hook([shard], [jax.ShapeDtypeStruct((NDEV * rows, D), shard.dtype)], 7, 1, start, finish)


def _rs_d2d_hook(gfull):
    rows = gfull.shape[0] // NDEV

    def pairs(g, land):
        x, y, c = _my_place()
        return (x, y, 1 - c), [(k, _rows(g, (2 * k + 1 - c) * rows, rows), land.at[k]) for k in range(NCHIP)]

    def start(ins, outs, rcopy, lcopy):
        sib, cps = pairs(ins[0], outs[0])
        for i, src, dst in cps:
            rcopy(i, src, dst, sib).start()

    def finish(ins, outs, rcopy, lcopy):
        sib, cps = pairs(ins[0], outs[0])
        for i, src, dst in cps:
            rcopy(i, dst, dst, sib).wait_recv()
        for i, src, dst in cps:
            rcopy(i, src, dst, sib).wait_send()

    return _Hook([gfull], [jax.ShapeDtypeStruct((NCHIP, rows, D), gfull.dtype)], NCHIP, 0, start, finish)


def _rs_ici_hook(part):
    def start(ins, outs, rcopy, lcopy):
        x, y, c = _my_place()
        mychip = 2 * x + y
        lcopy(0, ins[0].at[mychip], outs[0].at[mychip]).start()
        for j, (dev, chip) in enumerate(_ici_peers(x, y, c)):
            rcopy(j, ins[0].at[chip], outs[0].at[mychip], dev).start()

    def finish(ins, outs, rcopy, lcopy):
        x, y, c = _my_place()
        mychip = 2 * x + y
        peers = _ici_peers(x, y, c)
        for j, (dev, chip) in enumerate(peers):
            rcopy(j, outs[0].at[chip], outs[0].at[chip], dev).wait_recv()
        for j, (dev, chip) in enumerate(peers):
            rcopy(j, ins[0].at[chip], outs[0].at[mychip], dev).wait_send()
        lcopy(0, ins[0].at[mychip], outs[0].at[mychip]).wait()

    return _Hook([part], [jax.ShapeDtypeStruct(part.shape, part.dtype)], 3, 1, start, finish)


def _small_hook(a, b):
    def peers():
        x, y, c = _my_place()
        out = []
        for dx in range(2):
            for dy in range(2):
                for dc in range(2):
                    if dx + dy + dc:
                        px, py, pc = (1 - x if dx else x), (1 - y if dy else y), (1 - c if dc else c)
                        out.append(((px, py, pc), 4 * px + 2 * py + pc))
        return 4 * x + 2 * y + c, out

    def start(ins, outs, rcopy, lcopy):
        me, ps = peers()
        for t in range(2):
            lcopy(t, ins[t], outs[t].at[me]).start()
            for i, (dev, _) in enumerate(ps):
                rcopy(2 * i + t, ins[t], outs[t].at[me], dev).start()

    def finish(ins, outs, rcopy, lcopy):
        me, ps = peers()
        for t in range(2):
            for i, (dev, peer) in enumerate(ps):
                rcopy(2 * i + t, outs[t].at[peer], outs[t].at[peer], dev).wait_recv()
            for i, (dev, _) in enumerate(ps):
                rcopy(2 * i + t, ins[t], outs[t].at[me], dev).wait_send()
            lcopy(t, ins[t], outs[t].at[me]).wait()

    return _Hook([a, b], [jax.ShapeDtypeStruct((NDEV,) + a.shape, a.dtype), jax.ShapeDtypeStruct((NDEV,) + b.shape, b.dtype)],
                 14, 2, start, finish)


def _wblock(w):
    return _resident(w.shape, lambda *_: (0, 0))


def _ffn_fwd(name, x, nrm, wg, wu, wd, hooks=(), tm=512):
    S = x.shape[0]

    def body(x_ref, n_ref, wg_ref, wu_ref, wd_ref, y_ref, g_ref, u_ref, acc_ref):
        xv = x_ref[...]
        _, _, h = _rms_fwd(xv, n_ref[...])
        h = h.astype(CDT)
        for ci in range(F // FT):
            sl = slice(ci * FT, (ci + 1) * FT)
            g = _mm_nt(h, wg_ref[sl, :])
            u = _mm_nt(h, wu_ref[sl, :])
            g_ref[:, sl] = g.astype(CDT)
            u_ref[:, sl] = u.astype(CDT)
            a = (g * jax.nn.sigmoid(g) * u).astype(CDT)
            o = _mm(a, wd_ref[sl, :])
            if ci == 0:
                acc_ref[...] = o
            else:
                acc_ref[...] += o
        y_ref[...] = xv + 0.5 * acc_ref[...]

    return _call(
        body, hooks, [x, nrm, wg, wu, wd], name=name, grid=(S // tm,),
        out_shape=[jax.ShapeDtypeStruct((S, D), f32), jax.ShapeDtypeStruct((S, F), CDT), jax.ShapeDtypeStruct((S, F), CDT)],
        in_specs=[pl.BlockSpec((tm, D), lambda i: (i, 0)), _resident((1, D), lambda i: (0, 0)),
                  _wblock(wg), _wblock(wu), _wblock(wd)],
        out_specs=[pl.BlockSpec((tm, D), lambda i: (i, 0)), pl.BlockSpec((tm, F), lambda i: (i, 0)),
                   pl.BlockSpec((tm, F), lambda i: (i, 0))],
        scratch_shapes=[pltpu.VMEM((tm, D), f32)],
        compiler_params=_params("arbitrary"),
    )


def _proj_fwd(x1, nrm, wfull, b3, hooks=(), tm=512):
    S = x1.shape[0]

    def body(x_ref, n_ref, w_ref, b_ref, p_ref, h_ref):
        _, _, h = _rms_fwd(x_ref[...], n_ref[...])
        h = h.astype(CDT)
        h_ref[...] = h
        for p in range(8):
            seg = SEG_OF_SLOT[p]
            p_ref[p] = (_mm(h, w_ref[seg * D:(seg + 1) * D, :]) + b_ref[seg]).astype(CDT)

    return _call(
        body, hooks, [x1, nrm, wfull, b3], name="proj_fwd", grid=(S // tm,),
        out_shape=[jax.ShapeDtypeStruct((8, S, D), CDT), jax.ShapeDtypeStruct((S, D), CDT)],
        in_specs=[pl.BlockSpec((tm, D), lambda i: (i, 0)), _resident((1, D), lambda i: (0, 0)),
                  _resident((8 * D, D), lambda i: (0, 0)), _resident((8, 1, D), lambda i: (0, 0, 0))],
        out_specs=[pl.BlockSpec((8, tm, D), lambda i: (0, i, 0)), pl.BlockSpec((tm, D), lambda i: (i, 0))],
        compiler_params=_params("arbitrary"),
    )


def _sgu_norm(va, gn, bn):
    mu = jnp.mean(va, axis=-1, keepdims=True)
    xc = va - mu
    rstd = lax.rsqrt(jnp.mean(xc * xc, axis=-1, keepdims=True) + EPS)
    vhat = xc * rstd
    return rstd, vhat, vhat * gn + bn


def _sgu_fwd(proj, gn, bn, ws, bsc, tm=512):
    S = proj.shape[1]
    GW = D // G

    def body(p_ref, gn_ref, bn_ref, ws_ref, bs_ref, a_ref):
        ua = _gelu(p_ref[0].astype(f32))
        va = _gelu(p_ref[1].astype(f32))
        _, _, vn = _sgu_norm(va, gn_ref[...], bn_ref[...])
        vn = vn.astype(CDT)
        for ch in range(tm // C):
            rs = slice(ch * C, (ch + 1) * C)
            for gi in range(G):
                cs = slice(gi * GW, (gi + 1) * GW)
                s = _mm(ws_ref[gi], vn[rs, cs]) + bs_ref[gi]
                a_ref[rs, cs] = (ua[rs, cs] * s).astype(CDT)

    return pl.pallas_call(
        body, name="sgu_fwd", grid=(S // tm,),
        out_shape=jax.ShapeDtypeStruct((S, D), CDT),
        in_specs=[pl.BlockSpec((2, tm, D), lambda i: (0, i, 0)), _resident((1, D), lambda i: (0, 0)),
                  _resident((1, D), lambda i: (0, 0)), _resident((G, C, C), lambda i: (0, 0, 0)),
                  _resident((G, C, 1), lambda i: (0, 0, 0))],
        out_specs=pl.BlockSpec((tm, D), lambda i: (i, 0)),
        compiler_params=_params("arbitrary"),
    )(proj, gn, bn, ws, bsc)


def _decay_tables(dl_ref):
    lg = jax.nn.log_sigmoid(dl_ref[0:2, :])
    lgf, lgb = lg[0:1, :], lg[1:2, :]
    assert RC <= DK
    ri = lax.broadcasted_iota(jnp.int32, (RC, RC), 0)
    ci = lax.broadcasted_iota(jnp.int32, (RC, RC), 1)
    d = (ri - ci).astype(f32)
    lower = d >= 0
    dmat = jnp.where(lower, jnp.exp(d * lgf[:, :RC]), jnp.exp(-d * lgb[:, :RC]))
    dmat_t = jnp.where(d <= 0, jnp.exp(-d * lgf[:, :RC]), jnp.exp(d * lgb[:, :RC]))
    pos = lax.broadcasted_iota(jnp.int32, (RC, DK), 0).astype(f32)
    t = dict(
        lgf=lgf, lgb=lgb, d=d, lower=lower, dmat=dmat, dmat_t=dmat_t, pos=pos,
        fq=jnp.exp((pos + 1.0) * lgf), fk=jnp.exp((RC - 1.0 - pos) * lgf),
        bq=jnp.exp((RC - pos) * lgb), bk=jnp.exp(pos * lgb),
        lamf=jnp.exp(float(RC) * lgf), lamb=jnp.exp(float(RC) * lgb),
    )
    return t


def _rotate(t, co, si):
    t1, t2 = t[:, :DK // 2], t[:, DK // 2:]
    return jnp.concatenate([t1 * co - t2 * si, t2 * co + t1 * si], axis=-1)


def _unrotate(t, co, si):
    t1, t2 = t[:, :DK // 2], t[:, DK // 2:]
    return jnp.concatenate([t1 * co + t2 * si, t2 * co - t1 * si], axis=-1)


K_SCALE = DK ** -0.5
ROT_ROWS = 512


def _rotate_into(q_ref, k_ref, cos_ref, sin_ref, qs_ref, ks_ref, S):
    for rt in range(S // ROT_ROWS):
        rs = slice(rt * ROT_ROWS, (rt + 1) * ROT_ROWS)
        co, si = cos_ref[rs, :], sin_ref[rs, :]
        qs_ref[rs, :] = _rotate(q_ref[rs, :].astype(f32), co, si).astype(CDT)
        ks_ref[rs, :] = (_rotate(k_ref[rs, :].astype(f32), co, si) * K_SCALE).astype(CDT)


def _ret_fwd(proj, cos, sin, dl, hooks=()):
    S = proj.shape[1]
    NC = S // RC

    def body(q_ref, k_ref, v_ref, g_ref, cos_ref, sin_ref, dl_ref, R_ref, r_ref, sfs_ref, sbs_ref,
             qs_ref, ks_ref, rb_ref, sf_ref, sb_ref):
        t = _decay_tables(dl_ref)
        _rotate_into(q_ref, k_ref, cos_ref, sin_ref, qs_ref, ks_ref, S)

        def chunk(n):
            rows = pl.ds(pl.multiple_of(n * RC, RC), RC)
            return rows, qs_ref[rows, :], ks_ref[rows, :], v_ref[rows, :]

        sf_ref[...] = jnp.zeros_like(sf_ref)
        sb_ref[...] = jnp.zeros_like(sb_ref)

        def step(i, carry):
            rows, qn, kn, vn = chunk(i)
            sc = _mm_nt(qn, kn) * t["dmat"]
            out = _mm(sc.astype(CDT), vn)
            sf = sf_ref[...]
            sfb = sf.astype(CDT)
            sfs_ref[i] = sfb
            R_ref[rows, :] = out + _mm((qn.astype(f32) * t["fq"]).astype(CDT), sfb)
            sf_ref[...] = sf * t["lamf"] + _mm_tn((kn.astype(f32) * t["fk"]).astype(CDT), vn)
            m = NC - 1 - i
            rows, qn, kn, vn = chunk(m)
            sb = sb_ref[...]
            sbb = sb.astype(CDT)
            sbs_ref[m] = sbb
            rb_ref[rows, :] = _mm((qn.astype(f32) * t["bq"]).astype(CDT), sbb)
            sb_ref[...] = sb * t["lamb"] + _mm_tn((kn.astype(f32) * t["bk"]).astype(CDT), vn)
            return carry

        lax.fori_loop(0, NC, step, 0)
        for rt in range(S // ROT_ROWS):
            rs = slice(rt * ROT_ROWS, (rt + 1) * ROT_ROWS)
            R = R_ref[rs, :] + rb_ref[rs, :]
            R_ref[rs, :] = R
            rn = R * lax.rsqrt(jnp.mean(R * R, axis=-1, keepdims=True) + EPS)
            g = g_ref[rs, :].astype(f32)
            r_ref[rs, :] = (rn * g * jax.nn.sigmoid(g)).astype(CDT)

    def seg(slot):
        return pl.BlockSpec((None, S, DK), lambda h: (slot, 0, h))

    states = jax.ShapeDtypeStruct((H, NC, DK, DK), CDT)
    state_blk = pl.BlockSpec((None, NC, DK, DK), lambda h: (h, 0, 0, 0))
    return _call(
        body, hooks, [proj, proj, proj, proj, cos, sin, dl], name="ret_fwd", grid=(H,),
        out_shape=[jax.ShapeDtypeStruct((S, H * DK), f32), jax.ShapeDtypeStruct((S, H * DK), CDT), states, states],
        in_specs=[seg(SLOT_Q), seg(SLOT_K), seg(SLOT_VR), seg(SLOT_GR),
                  _resident((S, DK // 2), lambda h: (0, 0)), _resident((S, DK // 2), lambda h: (0, 0)),
                  pl.BlockSpec((None, 8, DK), lambda h: (h, 0, 0))],
        out_specs=[pl.BlockSpec((S, DK), lambda h: (0, h)), pl.BlockSpec((S, DK), lambda h: (0, h)), state_blk, state_blk],
        scratch_shapes=[pltpu.VMEM((S, DK), CDT), pltpu.VMEM((S, DK), CDT), pltpu.VMEM((S, DK), f32),
                        pltpu.VMEM((DK, DK), f32), pltpu.VMEM((DK, DK), f32)],
        compiler_params=_params("arbitrary"),
    )


def _merge_fwd(a, r, proj, x1, wa, wb, wo, tm=512):
    S = x1.shape[0]

    def body(a_ref, r_ref, gt_ref, x_ref, wa_ref, wb_ref, wo_ref, x2_ref, ya_ref, yb_ref):
        ya = _mm(a_ref[...], wa_ref[...])
        yb = _mm(r_ref[...], wb_ref[...])
        ya_ref[...] = ya.astype(CDT)
        yb_ref[...] = yb.astype(CDT)
        mix = jax.nn.sigmoid(gt_ref[0].astype(f32)) * ya + jax.nn.sigmoid(gt_ref[1].astype(f32)) * yb
        x2_ref[...] = x_ref[...] + _mm(mix.astype(CDT), wo_ref[...])

    tok = pl.BlockSpec((tm, D), lambda i: (i, 0))
    return pl.pallas_call(
        body, name="merge_fwd", grid=(S // tm,),
        out_shape=[jax.ShapeDtypeStruct((S, D), f32), jax.ShapeDtypeStruct((S, D), CDT), jax.ShapeDtypeStruct((S, D), CDT)],
        in_specs=[tok, tok, pl.BlockSpec((2, tm, D), lambda i: (SLOT_GA // 2, i, 0)), tok,
                  _wblock(wa), _wblock(wb), _wblock(wo)],
        out_specs=[tok, tok, tok],
        compiler_params=_params("arbitrary"),
    )(a, r, proj, x1, wa, wb, wo)


def _loss_head(x3, fn, target, tm=512):
    S = x3.shape[0]

    def body(x_ref, n_ref, t_ref, dx_ref, dn_ref, l_ref):
        n = n_ref[...]
        r, xh, y = _rms_fwd(x_ref[...], n)
        e = y - t_ref[...]
        dy = e * (1.0 / D)
        dx, dn = _rms_bwd(dy, r, xh, n)
        dx_ref[...] = dx
        part = 0.5 * jnp.sum(jnp.sum(e * e, axis=-1, keepdims=True), axis=0, keepdims=True) * (1.0 / D)

        @pl.when(pl.program_id(0) == 0)
        def _():
            dn_ref[...] = jnp.zeros_like(dn_ref)
            l_ref[...] = jnp.zeros_like(l_ref)

        dn_ref[...] += dn
        l_ref[...] += jnp.broadcast_to(part, l_ref.shape)

    tok = pl.BlockSpec((tm, D), lambda i: (i, 0))
    return pl.pallas_call(
        body, name="loss_head", grid=(S // tm,),
        out_shape=[jax.ShapeDtypeStruct((S, D), f32), jax.ShapeDtypeStruct((1, D), f32), jax.ShapeDtypeStruct((8, 128), f32)],
        in_specs=[tok, _resident((1, D), lambda i: (0, 0)), tok],
        out_specs=[tok, pl.BlockSpec((1, D), lambda i: (0, 0)), pl.BlockSpec((8, 128), lambda i: (0, 0))],
        compiler_params=_params("arbitrary"),
    )(x3, fn, target)


def _ffn_bwd_act(name, dy, x, g, u, nrm, wg, wu, wd, hooks=(), tm=256):
    S = x.shape[0]

    def body(dy_ref, x_ref, g_ref, u_ref, n_ref, wg_ref, wu_ref, wd_ref,
             dx_ref, dg_ref, du_ref, a_ref, h_ref, dyh_ref, dn_ref, acc_ref):
        n = n_ref[...]
        dyv = dy_ref[...]
        r, xh, h = _rms_fwd(x_ref[...], n)
        h_ref[...] = h.astype(CDT)
        dyh = (0.5 * dyv).astype(CDT)
        dyh_ref[...] = dyh
        for ci in range(F // FT):
            sl = slice(ci * FT, (ci + 1) * FT)
            da = _mm_nt(dyh, wd_ref[sl, :])
            gv = g_ref[:, sl].astype(f32)
            uv = u_ref[:, sl].astype(f32)
            s = jax.nn.sigmoid(gv)
            silu = gv * s
            a_ref[:, sl] = (silu * uv).astype(CDT)
            du = (da * silu).astype(CDT)
            dg = (da * uv * (s * (1.0 + gv * (1.0 - s)))).astype(CDT)
            du_ref[:, sl] = du
            dg_ref[:, sl] = dg
            dh = _mm(dg, wg_ref[sl, :]) + _mm(du, wu_ref[sl, :])
            if ci == 0:
                acc_ref[...] = dh
            else:
                acc_ref[...] += dh
        dx, dn = _rms_bwd(acc_ref[...], r, xh, n)
        dx_ref[...] = dyv + dx

        @pl.when(pl.program_id(0) == 0)
        def _():
            dn_ref[...] = jnp.zeros_like(dn_ref)

        dn_ref[...] += dn

    tok = pl.BlockSpec((tm, D), lambda i: (i, 0))
    hid = pl.BlockSpec((tm, F), lambda i: (i, 0))
    return _call(
        body, hooks, [dy, x, g, u, nrm, wg, wu, wd], name=name, grid=(S // tm,),
        out_shape=[jax.ShapeDtypeStruct((S, D), f32), jax.ShapeDtypeStruct((S, F), CDT), jax.ShapeDtypeStruct((S, F), CDT),
                   jax.ShapeDtypeStruct((S, F), CDT), jax.ShapeDtypeStruct((S, D), CDT), jax.ShapeDtypeStruct((S, D), CDT),
                   jax.ShapeDtypeStruct((1, D), f32)],
        in_specs=[tok, tok, hid, hid, _resident((1, D), lambda i: (0, 0)), _wblock(wg), _wblock(wu), _wblock(wd)],
        out_specs=[tok, hid, hid, hid, tok, tok, pl.BlockSpec((1, D), lambda i: (0, 0))],
        scratch_shapes=[pltpu.VMEM((tm, D), f32)],
        compiler_params=_params("arbitrary"),
    )


TN_ROWS = 256


def _tn(name, xs, ys, row_block_of, hooks=()):
    S, M = xs.shape
    B = ys.shape[0]

    def body(x_ref, y_ref, o_ref):
        o_ref[...] = _mm_tn(x_ref[...], y_ref[...]).astype(CDT)

    return _call(
        body, hooks, [xs, ys], name=name, grid=(B, M // TN_ROWS),
        out_shape=[jax.ShapeDtypeStruct((B * M, D), CDT)],
        in_specs=[pl.BlockSpec((S, TN_ROWS), lambda b, i: (0, i)), pl.BlockSpec((None, S, D), lambda b, i: (b, 0, 0))],
        out_specs=[pl.BlockSpec((TN_ROWS, D), lambda b, i: (row_block_of(b, i), 0))],
        compiler_params=_params("arbitrary", "arbitrary"),
    )


def _wgrad(name, xs, y, hooks=()):
    return _tn(name, xs, y[None], lambda b, i: i, hooks)


def _merge_bwd_act(dx2, ya, yb, proj, wa, wb, wo, hooks=(), tm=512):
    S = dx2.shape[0]

    def body(dx_ref, ya_ref, yb_ref, gt_ref, wa_ref, wb_ref, wo_ref,
             dp_ref, da_ref, dr_ref, mix_ref, dxb_ref, dya_ref, dyb_ref):
        dxb = dx_ref[...].astype(CDT)
        dxb_ref[...] = dxb
        dmix = _mm_nt(dxb, wo_ref[...])
        ya = ya_ref[...].astype(f32)
        yb = yb_ref[...].astype(f32)
        sa = jax.nn.sigmoid(gt_ref[0].astype(f32))
        sb = jax.nn.sigmoid(gt_ref[1].astype(f32))
        mix_ref[...] = (sa * ya + sb * yb).astype(CDT)
        dya = (dmix * sa).astype(CDT)
        dyb = (dmix * sb).astype(CDT)
        dya_ref[...] = dya
        dyb_ref[...] = dyb
        dp_ref[0] = (dmix * ya * sa * (1.0 - sa)).astype(CDT)
        dp_ref[1] = (dmix * yb * sb * (1.0 - sb)).astype(CDT)
        da_ref[...] = _mm_nt(dya, wa_ref[...]).astype(CDT)
        dr_ref[...] = _mm_nt(dyb, wb_ref[...]).astype(CDT)

    tok = pl.BlockSpec((tm, D), lambda i: (i, 0))
    gates = pl.BlockSpec((2, tm, D), lambda i: (SLOT_GA // 2, i, 0))
    act = jax.ShapeDtypeStruct((S, D), CDT)
    return _call(
        body, hooks, [dx2, ya, yb, proj, wa, wb, wo], name="merge_bwd_act", grid=(S // tm,),
        out_shape=[jax.ShapeDtypeStruct((8, S, D), CDT), act, act, act, act, act, act],
        in_specs=[tok, tok, tok, gates, _wblock(wa), _wblock(wb), _wblock(wo)],
        out_specs=[gates, tok, tok, tok, tok, tok, tok],
        compiler_params=_params("arbitrary"),
    )


def _sgu_bwd(da, proj, dproj, gn, bn, ws, wst, bsc, hooks=(), tm=512):
    S = proj.shape[1]
    GW = D // G

    def body(da_ref, p_ref, dpin_ref, gn_ref, bn_ref, ws_ref, wst_ref, bs_ref,
             dp_ref, dws_ref, dbs_ref, dgn_ref, dbn_ref, ds_ref, dvn_ref):
        @pl.when(pl.program_id(0) == 0)
        def _():
            dws_ref[...] = jnp.zeros_like(dws_ref)
            dbs_ref[...] = jnp.zeros_like(dbs_ref)
            dgn_ref[...] = jnp.zeros_like(dgn_ref)
            dbn_ref[...] = jnp.zeros_like(dbn_ref)

        pu = p_ref[0].astype(f32)
        pv = p_ref[1].astype(f32)
        ua = _gelu(pu)
        va = _gelu(pv)
        gn = gn_ref[...]
        rstd, vhat, vn = _sgu_norm(va, gn, bn_ref[...])
        vnb = vn.astype(CDT)
        dav = da_ref[...].astype(f32)
        dsb = (dav * ua).astype(CDT)
        ones = jnp.ones((8, GW), CDT)
        for ch in range(tm // C):
            rs = slice(ch * C, (ch + 1) * C)
            for gi in range(G):
                cs = slice(gi * GW, (gi + 1) * GW)
                s = _mm(ws_ref[gi], vnb[rs, cs]) + bs_ref[gi]
                ds_ref[rs, cs] = s
                dsg = dsb[rs, cs]
                dws_ref[gi] += _mm_nt(dsg, vnb[rs, cs])
                dbs_ref[gi] += _mm_nt(ones, dsg)
                dvn_ref[rs, cs] = _mm(wst_ref[gi], dsg)
        dp_ref[0] = (dav * ds_ref[...] * _gelu_grad(pu)).astype(CDT)
        dvn = dvn_ref[...]
        dgn_ref[...] += jnp.sum(dvn * vhat, axis=0, keepdims=True)
        dbn_ref[...] += jnp.sum(dvn, axis=0, keepdims=True)
        dvh = dvn * gn
        dva = rstd * (dvh - jnp.mean(dvh, axis=-1, keepdims=True) - vhat * jnp.mean(dvh * vhat, axis=-1, keepdims=True))
        dp_ref[1] = (dva * _gelu_grad(pv)).astype(CDT)

    uv = pl.BlockSpec((2, tm, D), lambda i: (0, i, 0))
    row = _resident((1, D), lambda i: (0, 0))
    return _call(
        body, hooks, [da, proj, dproj, gn, bn, ws, wst, bsc], name="sgu_bwd", grid=(S // tm,),
        out_shape=[jax.ShapeDtypeStruct(dproj.shape, CDT), jax.ShapeDtypeStruct((G, C, C), f32),
                   jax.ShapeDtypeStruct((G, 8, C), f32), jax.ShapeDtypeStruct((1, D), f32), jax.ShapeDtypeStruct((1, D), f32)],
        in_specs=[pl.BlockSpec((tm, D), lambda i: (i, 0)), uv, _HBM, row, row,
                  _resident((G, C, C), lambda i: (0, 0, 0)), _resident((G, C, C), lambda i: (0, 0, 0)),
                  _resident((G, C, 1), lambda i: (0, 0, 0))],
        out_specs=[uv, pl.BlockSpec((G, C, C), lambda i: (0, 0, 0)), pl.BlockSpec((G, 8, C), lambda i: (0, 0, 0)),
                   pl.BlockSpec((1, D), lambda i: (0, 0)), pl.BlockSpec((1, D), lambda i: (0, 0))],
        scratch_shapes=[pltpu.VMEM((tm, D), f32), pltpu.VMEM((tm, D), f32)],
        input_output_aliases={2: 0},
        compiler_params=_params("arbitrary"),
    )


def _ret_bwd(dr, R, sfs, sbs, proj, dproj, cos, sin, dl, hooks=()):
    S = proj.shape[1]
    NC = S // RC
    assert NC % 2 == 0

    def body(dr_ref, R_ref, sf_ref, sb_ref, q_ref, k_ref, v_ref, g_ref, dpin_ref, cos_ref, sin_ref, dl_ref,
             dp_ref, dd_ref, qs_ref, ks_ref, dR_ref, dk_ref, dv_ref, gb_ref, gf_ref, acc_ref):
        t = _decay_tables(dl_ref)
        _rotate_into(q_ref, k_ref, cos_ref, sin_ref, qs_ref, ks_ref, S)
        for rt in range(S // ROT_ROWS):
            rs = slice(rt * ROT_ROWS, (rt + 1) * ROT_ROWS)
            Rv = R_ref[rs, :]
            rstd = lax.rsqrt(jnp.mean(Rv * Rv, axis=-1, keepdims=True) + EPS)
            rn = Rv * rstd
            gv = g_ref[rs, :].astype(f32)
            s = jax.nn.sigmoid(gv)
            drv = dr_ref[rs, :].astype(f32)
            dp_ref[3, rs, :] = (drv * rn * (s * (1.0 + gv * (1.0 - s)))).astype(CDT)
            drn = drv * gv * s
            dR_ref[rs, :] = (rstd * (drn - rn * jnp.mean(drn * rn, axis=-1, keepdims=True))).astype(CDT)

        def chunk(n):
            rows = pl.ds(pl.multiple_of(n * RC, RC), RC)
            return rows, qs_ref[rows, :], ks_ref[rows, :], v_ref[rows, :], dR_ref[rows, :]

        def emit_kv(rows, dk, dv, final):
            if not final:
                dk_ref[rows, :] = dk
                dv_ref[rows, :] = dv
            else:
                co, si = cos_ref[rows, :], sin_ref[rows, :]
                dp_ref[1, rows, :] = (_unrotate(dk_ref[rows, :] + dk, co, si) * K_SCALE).astype(CDT)
                dp_ref[2, rows, :] = (dv_ref[rows, :] + dv).astype(CDT)

        gb_ref[...] = jnp.zeros_like(gb_ref)
        gf_ref[...] = jnp.zeros_like(gf_ref)
        acc_ref[...] = jnp.zeros_like(acc_ref)
        dpos = jnp.abs(t["d"])

        def ascend(n, final):
            rows, qn, kn, vn, dRn = chunk(n)
            qf, kf = qn.astype(f32), kn.astype(f32)
            sc = _mm_nt(qn, kn)
            dA = _mm_nt(dRn, vn)
            w = sc * dA * t["dmat"] * dpos
            lgf_part = jnp.sum(jnp.where(t["lower"], w, 0.0), axis=0, keepdims=True)
            lgb_part = jnp.sum(jnp.where(t["lower"], 0.0, w), axis=0, keepdims=True)
            dsc = (dA * t["dmat"]).astype(CDT)
            dq = _mm(dsc, kn)
            scT = (_mm_nt(kn, qn) * t["dmat_t"]).astype(CDT)
            dscT = (_mm_nt(vn, dRn) * t["dmat_t"]).astype(CDT)
            dk = _mm(dscT, qn)
            dv = _mm(scT, dRn)
            sfb = sf_ref[n]
            sbb = sb_ref[n]
            qdf = qf * t["fq"]
            dqdf = _mm_nt(dRn, sfb)
            dq += dqdf * t["fq"]
            lgf_row = jnp.sum(qdf * dqdf * (t["pos"] + 1.0), axis=0, keepdims=True)
            qdb = qf * t["bq"]
            dqdb = _mm_nt(dRn, sbb)
            dq += dqdb * t["bq"]
            lgb_row = jnp.sum(qdb * dqdb * (RC - t["pos"]), axis=0, keepdims=True)
            gb = gb_ref[...]
            gbb = gb.astype(CDT)
            kdb = kf * t["bk"]
            dkdb = _mm_nt(vn, gbb)
            dk += dkdb * t["bk"]
            dv += _mm(kdb.astype(CDT), gbb)
            lgb_row += jnp.sum(kdb * dkdb * t["pos"], axis=0, keepdims=True)
            lgb_row += float(RC) * t["lamb"] * jnp.sum(gb * sbb.astype(f32), axis=0, keepdims=True)
            co, si = cos_ref[rows, :], sin_ref[rows, :]
            dp_ref[0, rows, :] = _unrotate(dq, co, si).astype(CDT)
            emit_kv(rows, dk, dv, final)
            acc_ref[0:1, :] += lgf_row + lgf_part
            acc_ref[1:2, :] += lgb_row + lgb_part
            gb_ref[...] = gb * t["lamb"] + _mm_tn(qdb.astype(CDT), dRn)

        def descend(n, final):
            rows, qn, kn, vn, dRn = chunk(n)
            gf = gf_ref[...]
            gfb = gf.astype(CDT)
            kdf = kn.astype(f32) * t["fk"]
            dkdf = _mm_nt(vn, gfb)
            lgf_row = jnp.sum(kdf * dkdf * (RC - 1.0 - t["pos"]), axis=0, keepdims=True)
            lgf_row += float(RC) * t["lamf"] * jnp.sum(gf * sf_ref[n].astype(f32), axis=0, keepdims=True)
            acc_ref[0:1, :] += lgf_row
            emit_kv(rows, dkdf * t["fk"], _mm(kdf.astype(CDT), gfb), final)
            gf_ref[...] = gf * t["lamf"] + _mm_tn((qn.astype(f32) * t["fq"]).astype(CDT), dRn)

        def sweep(final):
            def step(i, carry):
                ascend(i, final)
                descend(NC - 1 - i, final)
                return carry
            return step

        lax.fori_loop(0, NC // 2, sweep(False), 0)
        lax.fori_loop(NC // 2, NC, sweep(True), 0)
        dlg = jnp.sum(acc_ref[...], axis=1, keepdims=True)
        dlogit = dlg * jax.nn.sigmoid(-dl_ref[:, 0:1])
        lane = lax.broadcasted_iota(jnp.int32, (8, 128), 1)
        dd_ref[...] = jnp.where(lane == pl.program_id(0), jnp.broadcast_to(dlogit, (8, 128)), 0.0)

    def seg(slot):
        return pl.BlockSpec((None, S, DK), lambda h: (slot, 0, h), pipeline_mode=pl.Buffered(1))

    head = pl.BlockSpec((S, DK), lambda h: (0, h), pipeline_mode=pl.Buffered(1))
    states = pl.BlockSpec((None, NC, DK, DK), lambda h: (h, 0, 0, 0), pipeline_mode=pl.Buffered(1))
    return _call(
        body, hooks, [dr, R, sfs, sbs, proj, proj, proj, proj, dproj, cos, sin, dl], name="ret_bwd", grid=(H,),
        out_shape=[jax.ShapeDtypeStruct(dproj.shape, CDT), jax.ShapeDtypeStruct((H, 8, 128), f32)],
        in_specs=[head, head, states, states, seg(SLOT_Q), seg(SLOT_K), seg(SLOT_VR), seg(SLOT_GR), _HBM,
                  _resident((S, DK // 2), lambda h: (0, 0)), _resident((S, DK // 2), lambda h: (0, 0)),
                  pl.BlockSpec((None, 8, DK), lambda h: (h, 0, 0))],
        out_specs=[pl.BlockSpec((4, S, DK), lambda h: (1, 0, h), pipeline_mode=pl.Buffered(1)),
                   pl.BlockSpec((None, 8, 128), lambda h: (h, 0, 0))],
        scratch_shapes=[pltpu.VMEM((S, DK), CDT), pltpu.VMEM((S, DK), CDT), pltpu.VMEM((S, DK), CDT),
                        pltpu.VMEM((S, DK), f32), pltpu.VMEM((S, DK), f32),
                        pltpu.VMEM((DK, DK), f32), pltpu.VMEM((DK, DK), f32), pltpu.VMEM((8, DK), f32)],
        input_output_aliases={8: 0},
        compiler_params=_params("arbitrary"),
    )


def _proj_bwd_act(dproj, dx2, x1, nrm, wfull, hooks=(), tm=512):
    S = x1.shape[0]

    def body(dp_ref, dx2_ref, x_ref, n_ref, w_ref, dx_ref, dn_ref, db_ref, acc_ref):
        @pl.when(pl.program_id(0) == 0)
        def _():
            dn_ref[...] = jnp.zeros_like(dn_ref)
            db_ref[...] = jnp.zeros_like(db_ref)

        for p in range(8):
            seg = SEG_OF_SLOT[p]
            dp = dp_ref[p]
            db_ref[seg] += jnp.sum(dp.astype(f32), axis=0, keepdims=True)
            dh = _mm_nt(dp, w_ref[seg * D:(seg + 1) * D, :])
            if p == 0:
                acc_ref[...] = dh
            else:
                acc_ref[...] += dh
        n = n_ref[...]
        r, xh, _ = _rms_fwd(x_ref[...], n)
        dx, dn = _rms_bwd(acc_ref[...], r, xh, n)
        dx_ref[...] = dx2_ref[...] + dx
        dn_ref[...] += dn

    tok = pl.BlockSpec((tm, D), lambda i: (i, 0))
    return _call(
        body, hooks, [dproj, dx2, x1, nrm, wfull], name="proj_bwd_act", grid=(S // tm,),
        out_shape=[jax.ShapeDtypeStruct((S, D), f32), jax.ShapeDtypeStruct((1, D), f32), jax.ShapeDtypeStruct((8, 1, D), f32)],
        in_specs=[pl.BlockSpec((8, tm, D), lambda i: (0, i, 0)), tok, tok, _resident((1, D), lambda i: (0, 0)),
                  _resident((8 * D, D), lambda i: (0, 0))],
        out_specs=[tok, pl.BlockSpec((1, D), lambda i: (0, 0)), pl.BlockSpec((8, 1, D), lambda i: (0, 0, 0))],
        scratch_shapes=[pltpu.VMEM((tm, D), f32)],
        compiler_params=_params("arbitrary"),
    )


def _rs_sum(name, gfulls, lands, my_c):
    n = len(gfulls)
    rows = gfulls[0].shape[0] // NDEV
    assert all(g.shape[0] == NDEV * rows for g in gfulls)

    def body(c_ref, *refs):
        for g_ref, l_ref, o_ref in zip(refs[:n], refs[n:2 * n], refs[2 * n:]):
            o_ref[...] = (g_ref[...].astype(f32) + l_ref[...].astype(f32)).astype(CDT)

    slot = pl.BlockSpec((None, rows, D), lambda k, c: (k, 0, 0))
    return pl.pallas_call(
        body, name=name,
        grid_spec=pltpu.PrefetchScalarGridSpec(
            num_scalar_prefetch=1, grid=(NCHIP,),
            in_specs=[pl.BlockSpec((rows, D), lambda k, c: (2 * k + c[0], 0))] * n + [slot] * n,
            out_specs=[slot] * n),
        out_shape=[jax.ShapeDtypeStruct((NCHIP, rows, D), CDT)] * n,
        compiler_params=_params("arbitrary"),
    )(my_c, *gfulls, *lands)


def _adamw_math(g, w, m, v):
    m2 = ADAM_B1 * m + (1.0 - ADAM_B1) * g
    v2 = ADAM_B2 * v + (1.0 - ADAM_B2) * (g * g)
    delta = -ADAM_LR * ((m2 / BC1) / (jnp.sqrt(v2 / BC2) + ADAM_EPS) + ADAM_WD * w)
    return delta, m2, v2


def _adamw_big(name, landed, w, m, v, hooks=()):
    rows = w.shape[0]
    tr = min(rows, 256) if rows % 256 == 0 else rows

    def body(l_ref, w_ref, m_ref, v_ref, g_ref, d_ref, m2_ref, v2_ref):
        g = l_ref[0].astype(f32)
        for k in range(1, NCHIP):
            g = g + l_ref[k].astype(f32)
        g_ref[...] = g
        d_ref[...], m2_ref[...], v2_ref[...] = _adamw_math(g, w_ref[...], m_ref[...], v_ref[...])

    blk = pl.BlockSpec((tr, D), lambda i: (i, 0))
    o = jax.ShapeDtypeStruct((rows, D), f32)
    return _call(
        body, hooks, [landed, w, m, v], name=name, grid=(rows // tr,), out_shape=[o, o, o, o],
        in_specs=[pl.BlockSpec((NCHIP, tr, D), lambda i: (0, i, 0)), blk, blk, blk],
        out_specs=[blk, blk, blk, blk],
        compiler_params=_params("arbitrary"),
    )


ROW_FFN1_NORM, ROW_MIX_NORM, ROW_SGU_G, ROW_SGU_B, ROW_FFN2_NORM, ROW_FINAL_NORM, ROW_B_IN = 0, 1, 2, 3, 4, 5, 8
ROW_WS, ROW_BS, ROW_DECAY = 0, G * C, G * C + G * 8


def _adamw_small(ga, gb, params):
    def body(ga_ref, gb_ref, *refs):
        ins, outs = refs[:30], refs[30:]

        def total(ref, r0, n):
            g = ref[0, r0:r0 + n, :]
            for j in range(1, NDEV):
                g = g + ref[j, r0:r0 + n, :]
            return g

        def apply(i, g, rows=slice(None)):
            w, m, v = ins[3 * i][rows, :], ins[3 * i + 1][rows, :], ins[3 * i + 2][rows, :]
            outs[4 * i][rows, :] = g
            outs[4 * i + 1][rows, :], outs[4 * i + 2][rows, :], outs[4 * i + 3][rows, :] = _adamw_math(g, w, m, v)

        for i, r in enumerate((ROW_FFN1_NORM, ROW_MIX_NORM, ROW_SGU_G, ROW_SGU_B, ROW_FFN2_NORM, ROW_FINAL_NORM)):
            apply(i, total(ga_ref, r, 1))
        apply(6, total(ga_ref, ROW_B_IN, 8))
        apply(7, total(gb_ref, ROW_WS, G * C))
        for gi in range(G):
            apply(8, total(gb_ref, ROW_BS + 8 * gi, 1), slice(gi, gi + 1))
        dec = total(gb_ref, ROW_DECAY, 8)
        for hh in range(1, H):
            dec = dec + total(gb_ref, ROW_DECAY + 8 * hh, 8)
        apply(9, dec)

    flat = [a for p in params for a in p]
    out_shape = [jax.ShapeDtypeStruct(p[0].shape, f32) for p in params for _ in range(4)]
    vm = pl.BlockSpec(memory_space=pltpu.VMEM)
    return pl.pallas_call(
        body, name="adamw_small", out_shape=out_shape,
        in_specs=[vm] * (2 + len(flat)), out_specs=[vm] * len(out_shape),
        compiler_params=pltpu.CompilerParams(vmem_limit_bytes=VMEM_LIMIT),
    )(ga, gb, *flat)


def kernel(x, ffn1_norm, ffn1_w_gate, ffn1_w_up, ffn1_w_down, mix_norm, w_in, b_in, sgu_norm_g, sgu_norm_b, sgu_w_s, sgu_b_s, ret_decay_logit, w_branch_a, w_branch_b, w_out, ffn2_norm, ffn2_w_gate, ffn2_w_up, ffn2_w_down, final_norm, loss_target, m_ffn1_norm, m_ffn1_w_gate, m_ffn1_w_up, m_ffn1_w_down, m_mix_norm, m_w_in, m_b_in, m_sgu_norm_g, m_sgu_norm_b, m_sgu_w_s, m_sgu_b_s, m_ret_decay_logit, m_w_branch_a, m_w_branch_b, m_w_out, m_ffn2_norm, m_ffn2_w_gate, m_ffn2_w_up, m_ffn2_w_down, m_final_norm, v_ffn1_norm, v_ffn1_w_gate, v_ffn1_w_up, v_ffn1_w_down, v_mix_norm, v_w_in, v_b_in, v_sgu_norm_g, v_sgu_norm_b, v_sgu_w_s, v_sgu_b_s, v_ret_decay_logit, v_w_branch_a, v_w_branch_b, v_w_out, v_ffn2_norm, v_ffn2_w_gate, v_ffn2_w_up, v_ffn2_w_down, v_final_norm):
    args = dict(locals())
    S = x.shape[1]
    xs = x[0]
    target = loss_target[0]

    def buf_layout(name, a):
        a = a[0]
        return a.T if name in W_TRANSPOSED else a

    sh = {n: buf_layout(n, args[n]).astype(CDT) for n in W_NAMES}
    wf = {}

    b3 = b_in.reshape(8, 1, D)
    ws = sgu_w_s[0].astype(CDT)
    wst = jnp.swapaxes(sgu_w_s[0], 1, 2).astype(CDT)
    bsc = sgu_b_s[0].reshape(G, C, 1)
    dl = jnp.zeros((H, 8, DK), f32).at[:, 0:2, :].set(jnp.broadcast_to(ret_decay_logit[0].T[:, :, None], (H, 2, DK)))
    theta = ROPE_BASE ** (-jnp.arange(0, DK, 2, dtype=f32) / DK)
    ang = jnp.arange(S, dtype=f32)[:, None] * theta[None, :]
    cos, sin = jnp.cos(ang), jnp.sin(ang)
    fnorm = final_norm.reshape(1, D)

    f1 = ("ffn1_w_gate", "ffn1_w_up", "ffn1_w_down")
    f2 = ("ffn2_w_gate", "ffn2_w_up", "ffn2_w_down")
    br = ("w_branch_a", "w_branch_b", "w_out")
    wf[f1[0]], wf[f1[1]], wf[f1[2]] = _exchange("ag_ffn1", [_ag_hook(sh[n]) for n in f1])
    x1, g1, u1, wf["w_in"] = _ffn_fwd("ffn1_fwd", xs, ffn1_norm, *[wf[n] for n in f1], [_ag_hook(sh["w_in"])])
    proj, h2, wf[br[0]], wf[br[1]], wf[br[2]], wf[f2[0]] = _proj_fwd(
        x1, mix_norm, wf["w_in"], b3, [_ag_hook(sh[n]) for n in br + f2[:1]])
    a = _sgu_fwd(proj, sgu_norm_g, sgu_norm_b, ws, bsc)
    R, r, sfs, sbs, wf[f2[1]], wf[f2[2]] = _ret_fwd(proj, cos, sin, dl, [_ag_hook(sh[n]) for n in f2[1:]])
    x2, ya, yb = _merge_fwd(a, r, proj, x1, *[wf[n] for n in br])
    x3, g2, u2 = _ffn_fwd("ffn2_fwd", x2, ffn2_norm, *[wf[n] for n in f2])
    dx3, d_final, loss_part = _loss_head(x3, fnorm, target)
    loss = lax.psum(loss_part[0, 0], ("x", "y", "c"))

    my_c = lax.axis_index("c").astype(jnp.int32).reshape(1)
    gw, landed = {}, {}

    def d2d(*names):
        return [_rs_d2d_hook(gw[n]) for n in names]

    def ici(parts):
        return [_rs_ici_hook(p) for p in parts]

    dx2, dg2, du2, a2, hf2, dyh2, d_ffn2n = _ffn_bwd_act("ffn2_bwd_act", dx3, x2, g2, u2, ffn2_norm, *[wf[n] for n in f2])
    (gw[f2[2]],) = _wgrad("ffn2_wd_grad", a2, dyh2)
    (gw[f2[0]],) = _wgrad("ffn2_wg_grad", dg2, hf2)
    (gw[f2[1]],) = _wgrad("ffn2_wu_grad", du2, hf2)
    dproj, da, dr, mix, dx2b, dya, dyb, *sib = _merge_bwd_act(dx2, ya, yb, proj, *[wf[n] for n in br], d2d(*f2))
    part_f2 = _rs_sum("rs_sum_ffn2", [gw[n] for n in f2], sib, my_c)
    (gw["w_out"],) = _wgrad("wo_grad", mix, dx2b)
    (gw["w_branch_a"],) = _wgrad("wa_grad", a, dya)
    (gw["w_branch_b"],) = _wgrad("wb_grad", r, dyb)
    dproj, d_ws, d_bs, d_gn, d_bn, *rest = _sgu_bwd(da, proj, dproj, sgu_norm_g, sgu_norm_b, ws, wst, bsc,
                                                    d2d(*br) + ici(part_f2[:1]))
    landed[f2[0]] = rest.pop()
    part_br = _rs_sum("rs_sum_br", [gw[n] for n in br], rest, my_c)
    dproj, d_dec, *rest = _ret_bwd(dr, R, sfs, sbs, proj, dproj, cos, sin, dl, ici(part_f2[1:] + part_br))
    landed.update(zip(f2[1:] + br, rest))
    (gw["w_in"],) = _tn("win_grad", h2, dproj, lambda b, i: _seg_of_slot(b) * (D // TN_ROWS) + i)
    dx1, d_mixn, d_bin, sib_win = _proj_bwd_act(dproj, dx2, x1, mix_norm, wf["w_in"], d2d("w_in"))
    part_win = _rs_sum("rs_sum_win", [gw["w_in"]], [sib_win], my_c)
    dxs, dg1, du1, a1, hf1, dyh1, d_ffn1n, landed["w_in"] = _ffn_bwd_act(
        "ffn1_bwd_act", dx1, xs, g1, u1, ffn1_norm, *[wf[n] for n in f1], ici(part_win))
    small_a = jnp.concatenate([d_ffn1n, d_mixn, d_gn, d_bn, d_ffn2n, d_final, jnp.zeros((2, D), f32), d_bin.reshape(8, D)], axis=0)
    small_b = jnp.concatenate([d_ws.reshape(G * C, C), d_bs.reshape(G * 8, C), d_dec.reshape(H * 8, 128)], axis=0)
    gw[f1[2]], ga, gb = _wgrad("ffn1_wd_grad", a1, dyh1, [_small_hook(small_a, small_b)])
    gw[f1[0]], sib_d = _wgrad("ffn1_wg_grad", dg1, hf1, d2d(f1[2]))
    part_d = _rs_sum("rs_sum_f1d", [gw[f1[2]]], [sib_d], my_c)
    gw[f1[1]], landed[f1[2]], sib_g = _wgrad("ffn1_wu_grad", du1, hf1, ici(part_d) + d2d(f1[0]))
    part_g = _rs_sum("rs_sum_f1g", [gw[f1[0]]], [sib_g], my_c)
    landed[f1[0]], sib_u = _exchange("rs_tail_a", ici(part_g) + d2d(f1[1]))
    part_u = _rs_sum("rs_sum_f1u", [gw[f1[1]]], [sib_u], my_c)
    (landed[f1[1]],) = _exchange("rs_tail_b", ici(part_u))

    out = {"loss": loss, "grad_x": dxs[None]}

    def native(name, a):
        a = a.T if name in W_TRANSPOSED else a
        return a[None]

    for n in W_NAMES:
        res = _adamw_big("adamw_" + n, landed[n], buf_layout(n, args[n]), buf_layout(n, args["m_" + n]),
                         buf_layout(n, args["v_" + n]))
        for pre, val in zip(("grad_", "delta_", "new_m_", "new_v_"), res):
            out[pre + n] = native(n, val)

    def pad_decay(a):
        return jnp.zeros((8, 128), f32).at[0:2, 0:H].set(a[0])

    small = [
        ("ffn1_norm", lambda a: a, lambda a: a), ("mix_norm", lambda a: a, lambda a: a),
        ("sgu_norm_g", lambda a: a, lambda a: a), ("sgu_norm_b", lambda a: a, lambda a: a),
        ("ffn2_norm", lambda a: a, lambda a: a),
        ("final_norm", lambda a: a.reshape(1, D), lambda a: a.reshape(D)),
        ("b_in", lambda a: a.reshape(8, D), lambda a: a.reshape(1, 8 * D)),
        ("sgu_w_s", lambda a: a.reshape(G * C, C), lambda a: a.reshape(1, G, C, C)),
        ("sgu_b_s", lambda a: a[0], lambda a: a[None]),
        ("ret_decay_logit", pad_decay, lambda a: a[None, 0:2, 0:H]),
    ]
    res = _adamw_small(ga, gb, [(to(args[n]), to(args["m_" + n]), to(args["v_" + n])) for n, to, _ in small])
    for i, (n, _, back) in enumerate(small):
        for j, pre in enumerate(("grad_", "delta_", "new_m_", "new_v_")):
            out[pre + n] = back(res[4 * i + j])

    weights = ("ffn1_norm", "ffn1_w_gate", "ffn1_w_up", "ffn1_w_down", "mix_norm", "w_in", "b_in", "sgu_norm_g",
               "sgu_norm_b", "sgu_w_s", "sgu_b_s", "ret_decay_logit", "w_branch_a", "w_branch_b", "w_out", "ffn2_norm",
               "ffn2_w_gate", "ffn2_w_up", "ffn2_w_down", "final_norm")
    return (out["loss"], out["grad_x"], *[out["grad_" + n] for n in weights], *[out["delta_" + n] for n in weights],
            *[out["new_m_" + n] for n in weights], *[out["new_v_" + n] for n in weights])
```

```python
import functools
import math

import jax
import jax.numpy as jnp
from jax import lax
from jax.experimental import pallas as pl
from jax.experimental.pallas import tpu as pltpu

f32 = jnp.float32
CDT = jnp.bfloat16

D = 1024
F = 2816
C = 128
RC = 256
H = 4
DK = 256
G = 4
NDEV = 8
NCHIP = 4
EPS = 1e-6
ROPE_BASE = 10000.0
FT = 256
V7X_VMEM_BYTES = 64 * 1024 * 1024
VMEM_LIMIT = V7X_VMEM_BYTES - 8 * 1024 * 1024

ADAM_LR, ADAM_B1, ADAM_B2, ADAM_EPS, ADAM_WD, ADAM_STEP = 0.001, 0.9, 0.999, 1e-08, 0.01, 10
BC1 = 1.0 - ADAM_B1 ** ADAM_STEP
BC2 = 1.0 - ADAM_B2 ** ADAM_STEP

W_ROWS = dict(ffn1_w_gate=352, ffn1_w_up=352, ffn1_w_down=352, w_in=1024, w_branch_a=128, w_branch_b=128, w_out=128,
              ffn2_w_gate=352, ffn2_w_up=352, ffn2_w_down=352)
W_NAMES = tuple(W_ROWS)
W_TRANSPOSED = ("ffn1_w_gate", "ffn1_w_up", "ffn2_w_gate", "ffn2_w_up")

SLOT_U, SLOT_V, SLOT_GA, SLOT_GB, SLOT_Q, SLOT_K, SLOT_VR, SLOT_GR = range(8)


SEG_OF_SLOT = (0, 1, 6, 7, 2, 3, 4, 5)


def _seg_of_slot(p):
    return jnp.where(p < 2, p, jnp.where(p < 4, p + 4, p - 2))


def _mm(a, b):
    return jnp.dot(a, b, preferred_element_type=f32)


def _mm_nt(a, b):
    return lax.dot_general(a, b, (((1,), (1,)), ((), ())), preferred_element_type=f32)


def _mm_tn(a, b):
    return lax.dot_general(a, b, (((0,), (0,)), ((), ())), preferred_element_type=f32)


def _params(*sem):
    return pltpu.CompilerParams(dimension_semantics=sem, vmem_limit_bytes=VMEM_LIMIT)


def _resident(shape, index_map):
    return pl.BlockSpec(shape, index_map, pipeline_mode=pl.Buffered(1))


def _gelu(x):
    return 0.5 * x * (1.0 + lax.erf(x * (1.0 / math.sqrt(2.0))))


def _gelu_grad(x):
    return 0.5 * (1.0 + lax.erf(x * (1.0 / math.sqrt(2.0)))) + x * jnp.exp(-0.5 * x * x) * (1.0 / math.sqrt(2.0 * math.pi))


def _rms_fwd(x, n):
    r = lax.rsqrt(jnp.mean(x * x, axis=-1, keepdims=True) + EPS)
    xh = x * r
    return r, xh, xh * n


def _rms_bwd(dh, r, xh, n):
    dxh = dh * n
    dx = r * (dxh - xh * jnp.mean(dxh * xh, axis=-1, keepdims=True))
    return dx, jnp.sum(dh * xh, axis=0, keepdims=True)


MESH_ID = pl.DeviceIdType.MESH
_HBM = pl.BlockSpec(memory_space=pltpu.HBM)


def _my_place():
    return lax.axis_index("x"), lax.axis_index("y"), lax.axis_index("c")


def _ici_peers(x, y, c):
    return [((1 - x, y, c), 2 * (1 - x) + y), ((x, 1 - y, c), 2 * x + 1 - y), ((1 - x, 1 - y, c), 2 * (1 - x) + 1 - y)]


class _Hook:
    def __init__(self, operands, out_shapes, n_remote, n_local, start, finish, relay=None):
        self.operands, self.out_shapes = list(operands), list(out_shapes)
        self.n_remote, self.n_local, self.start, self.finish = n_remote, n_local, start, finish
        self.relay = relay or (lambda *a: None)


def _call(body, hooks, operands, *, in_specs, out_specs, out_shape, grid=None, scratch_shapes=(), **kw):
    hooks = tuple(hooks)
    n_in, n_out, n_scr = len(in_specs), len(out_shape), len(scratch_shapes)
    h_ops = [a for h in hooks for a in h.operands]
    h_outs = [s for h in hooks for s in h.out_shapes]
    h_sems = [pltpu.SemaphoreType.DMA((n,)) for h in hooks for n in (h.n_remote, h.n_remote, max(h.n_local, 1))]

    def wrapped(*refs):
        ins, hin = refs[:n_in], refs[n_in:n_in + len(h_ops)]
        o0 = n_in + len(h_ops)
        outs, hout = refs[o0:o0 + n_out], refs[o0 + n_out:o0 + n_out + len(h_outs)]
        s0 = o0 + n_out + len(h_outs)
        scr, hsem = refs[s0:s0 + n_scr], refs[s0 + n_scr:]

        def run(phase):
            ip = op = 0
            for i, h in enumerate(hooks):
                ssem, rsem, lsem = hsem[3 * i:3 * i + 3]

                def rcopy(k, src, dst, dev, ssem=ssem, rsem=rsem):
                    return pltpu.make_async_remote_copy(src_ref=src, dst_ref=dst, send_sem=ssem.at[k], recv_sem=rsem.at[k],
                                                        device_id=dev, device_id_type=MESH_ID)

                def lcopy(k, src, dst, lsem=lsem):
                    return pltpu.make_async_copy(src, dst, lsem.at[k])

                getattr(h, phase)(hin[ip:ip + len(h.operands)], hout[op:op + len(h.out_shapes)], rcopy, lcopy)
                ip += len(h.operands)
                op += len(h.out_shapes)

        def at_edge(phase, last):
            if not hooks:
                return
            if grid is None:
                run(phase)
                return
            cond = None
            for ax, n in enumerate(grid):
                here = pl.program_id(ax) == (n - 1 if last else 0)
                cond = here if cond is None else cond & here
            pl.when(cond)(lambda: run(phase))

        at_edge("start", False)
        at_edge("relay", True)
        body(*ins, *outs, *scr)
        at_edge("finish", True)

    if grid is not None:
        kw["grid"] = grid
    return list(pl.pallas_call(
        wrapped, out_shape=list(out_shape) + h_outs, in_specs=list(in_specs) + [_HBM] * len(h_ops),
        out_specs=list(out_specs) + [_HBM] * len(h_outs), scratch_shapes=list(scratch_shapes) + h_sems, **kw,
    )(*operands, *h_ops))


def _exchange(name, hooks):
    return _call(lambda: None, hooks, [], name=name, in_specs=[], out_specs=[], out_shape=[])


def _rows(ref, start, n):
    return ref.at[pl.ds(start, n), :]


def _ag_hook(shard):
    rows = shard.shape[0]

    def block(full, dev_index):
        return _rows(full, dev_index * rows, rows)

    def start(ins, outs, rcopy, lcopy):
        x, y, c = _my_place()
        src, dst = ins[0], block(outs[0], 4 * x + 2 * y + c)
        lcopy(0, src, dst).start()
        rcopy(0, src, dst, (x, y, 1 - c)).start()
        for j, (dev, _) in enumerate(_ici_peers(x, y, c)):
            rcopy(1 + j, src, dst, dev).start()

    def relay(ins, outs, rcopy, lcopy):
        x, y, c = _my_place()
        for j, (dev, chip) in enumerate(_ici_peers(x, y, c)):
            blk = block(outs[0], 2 * chip + c)
            rcopy(1 + j, blk, blk, dev).wait_recv()
            rcopy(4 + j, blk, blk, (x, y, 1 - c)).start()

    def finish(ins, outs, rcopy, lcopy):
        x, y, c = _my_place()
        sib = (x, y, 1 - c)
        full = outs[0]
        peers = _ici_peers(x, y, c)
        blk = block(full, 2 * (2 * x + y) + 1 - c)
        rcopy(0, blk, blk, sib).wait_recv()
        for j, (dev, chip) in enumerate(peers):
            blk = block(full, 2 * chip + 1 - c)
            rcopy(4 + j, blk, blk, sib).wait_recv()
        src, dst = ins[0], block(full, 4 * x + 2 * y + c)
        lcopy(0, src, dst).wait()
        rcopy(0, src, dst, sib).wait_send()
        for j, (dev, chip) in enumerate(peers):
            rcopy(1 + j, src, dst, dev).wait_send()
            blk = block(full, 2 * chip + c)
            rcopy(4 + j, blk, blk, sib).wait_send()

    return _Hook([shard], [jax.ShapeDtypeStruct((NDEV * rows, D), shard.dtype)], 7, 1, start, finish, relay)


def _rs_d2d_hook(gfull):
    rows = gfull.shape[0] // NDEV

    def pairs(g, land):
        x, y, c = _my_place()
        return (x, y, 1 - c), [(k, _rows(g, (2 * k + 1 - c) * rows, rows), land.at[k]) for k in range(NCHIP)]

    def start(ins, outs, rcopy, lcopy):
        sib, cps = pairs(ins[0], outs[0])
        for i, src, dst in cps:
            rcopy(i, src, dst, sib).start()

    def finish(ins, outs, rcopy, lcopy):
        sib, cps = pairs(ins[0], outs[0])
        for i, src, dst in cps:
            rcopy(i, dst, dst, sib).wait_recv()
        for i, src, dst in cps:
            rcopy(i, src, dst, sib).wait_send()

    return _Hook([gfull], [jax.ShapeDtypeStruct((NCHIP, rows, D), gfull.dtype)], NCHIP, 0, start, finish)


def _rs_ici_hook(part):
    def start(ins, outs, rcopy, lcopy):
        x, y, c = _my_place()
        mychip = 2 * x + y
        lcopy(0, ins[0].at[mychip], outs[0].at[mychip]).start()
        for j, (dev, chip) in enumerate(_ici_peers(x, y, c)):
            rcopy(j, ins[0].at[chip], outs[0].at[mychip], dev).start()

    def finish(ins, outs, rcopy, lcopy):
        x, y, c = _my_place()
        mychip = 2 * x + y
        peers = _ici_peers(x, y, c)
        for j, (dev, chip) in enumerate(peers):
            rcopy(j, outs[0].at[chip], outs[0].at[chip], dev).wait_recv()
        for j, (dev, chip) in enumerate(peers):
            rcopy(j, ins[0].at[chip], outs[0].at[mychip], dev).wait_send()
        lcopy(0, ins[0].at[mychip], outs[0].at[mychip]).wait()

    return _Hook([part], [jax.ShapeDtypeStruct(part.shape, part.dtype)], 3, 1, start, finish)


def _small_hook(a, b):
    def peers():
        x, y, c = _my_place()
        out = []
        for dx in range(2):
            for dy in range(2):
                for dc in range(2):
                    if dx + dy + dc:
                        px, py, pc = (1 - x if dx else x), (1 - y if dy else y), (1 - c if dc else c)
                        out.append(((px, py, pc), 4 * px + 2 * py + pc))
        return 4 * x + 2 * y + c, out

    def start(ins, outs, rcopy, lcopy):
        me, ps = peers()
        for t in range(2):
            lcopy(t, ins[t], outs[t].at[me]).start()
            for i, (dev, _) in enumerate(ps):
                rcopy(2 * i + t, ins[t], outs[t].at[me], dev).start()

    def finish(ins, outs, rcopy, lcopy):
        me, ps = peers()
        for t in range(2):
            for i, (dev, peer) in enumerate(ps):
                rcopy(2 * i + t, outs[t].at[peer], outs[t].at[peer], dev).wait_recv()
            for i, (dev, _) in enumerate(ps):
                rcopy(2 * i + t, ins[t], outs[t].at[me], dev).wait_send()
            lcopy(t, ins[t], outs[t].at[me]).wait()

    return _Hook([a, b], [jax.ShapeDtypeStruct((NDEV,) + a.shape, a.dtype), jax.ShapeDtypeStruct((NDEV,) + b.shape, b.dtype)],
                 14, 2, start, finish)


def _wblock(w):
    return _resident(w.shape, lambda *_: (0, 0))


def _ffn_fwd(name, x, nrm, wg, wu, wd, hooks=(), tm=512):
    S = x.shape[0]

    def body(x_ref, n_ref, wg_ref, wu_ref, wd_ref, y_ref, g_ref, u_ref, a_ref, h_ref, acc_ref):
        xv = x_ref[...]
        _, _, h = _rms_fwd(xv, n_ref[...])
        h = h.astype(CDT)
        h_ref[...] = h
        for ci in range(F // FT):
            sl = slice(ci * FT, (ci + 1) * FT)
            g = _mm_nt(h, wg_ref[sl, :])
            u = _mm_nt(h, wu_ref[sl, :])
            g_ref[:, sl] = g.astype(CDT)
            u_ref[:, sl] = u.astype(CDT)
            a = (g * jax.nn.sigmoid(g) * u).astype(CDT)
            a_ref[:, sl] = a
            o = _mm(a, wd_ref[sl, :])
            if ci == 0:
                acc_ref[...] = o
            else:
                acc_ref[...] += o
        y_ref[...] = xv + 0.5 * acc_ref[...]

    tok = pl.BlockSpec((tm, D), lambda i: (i, 0))
    hid = pl.BlockSpec((tm, F), lambda i: (i, 0))
    hidden = jax.ShapeDtypeStruct((S, F), CDT)
    return _call(
        body, hooks, [x, nrm, wg, wu, wd], name=name, grid=(S // tm,),
        out_shape=[jax.ShapeDtypeStruct((S, D), f32), hidden, hidden, hidden, jax.ShapeDtypeStruct((S, D), CDT)],
        in_specs=[tok, _resident((1, D), lambda i: (0, 0)), _wblock(wg), _wblock(wu), _wblock(wd)],
        out_specs=[tok, hid, hid, hid, tok],
        scratch_shapes=[pltpu.VMEM((tm, D), f32)],
        compiler_params=_params("arbitrary"),
    )


def _proj_fwd(x1, nrm, wfull, b3, hooks=(), tm=512):
    S = x1.shape[0]

    def body(x_ref, n_ref, w_ref, b_ref, p_ref, h_ref):
        _, _, h = _rms_fwd(x_ref[...], n_ref[...])
        h = h.astype(CDT)
        h_ref[...] = h
        for p in range(8):
            seg = SEG_OF_SLOT[p]
            p_ref[p] = (_mm(h, w_ref[seg * D:(seg + 1) * D, :]) + b_ref[seg]).astype(CDT)

    return _call(
        body, hooks, [x1, nrm, wfull, b3], name="proj_fwd", grid=(S // tm,),
        out_shape=[jax.ShapeDtypeStruct((8, S, D), CDT), jax.ShapeDtypeStruct((S, D), CDT)],
        in_specs=[pl.BlockSpec((tm, D), lambda i: (i, 0)), _resident((1, D), lambda i: (0, 0)),
                  _resident((8 * D, D), lambda i: (0, 0)), _resident((8, 1, D), lambda i: (0, 0, 0))],
        out_specs=[pl.BlockSpec((8, tm, D), lambda i: (0, i, 0)), pl.BlockSpec((tm, D), lambda i: (i, 0))],
        compiler_params=_params("arbitrary"),
    )


def _sgu_norm(va, gn, bn):
    mu = jnp.mean(va, axis=-1, keepdims=True)
    xc = va - mu
    rstd = lax.rsqrt(jnp.mean(xc * xc, axis=-1, keepdims=True) + EPS)
    vhat = xc * rstd
    return rstd, vhat, vhat * gn + bn


def _sgu_fwd(proj, gn, bn, ws, bsc, tm=512):
    S = proj.shape[1]
    GW = D // G

    def body(p_ref, gn_ref, bn_ref, ws_ref, bs_ref, a_ref):
        ua = _gelu(p_ref[0].astype(f32))
        va = _gelu(p_ref[1].astype(f32))
        _, _, vn = _sgu_norm(va, gn_ref[...], bn_ref[...])
        vn = vn.astype(CDT)
        for ch in range(tm // C):
            rs = slice(ch * C, (ch + 1) * C)
            for gi in range(G):
                cs = slice(gi * GW, (gi + 1) * GW)
                s = _mm(ws_ref[gi], vn[rs, cs]) + bs_ref[gi]
                a_ref[rs, cs] = (ua[rs, cs] * s).astype(CDT)

    return pl.pallas_call(
        body, name="sgu_fwd", grid=(S // tm,),
        out_shape=jax.ShapeDtypeStruct((S, D), CDT),
        in_specs=[pl.BlockSpec((2, tm, D), lambda i: (0, i, 0)), _resident((1, D), lambda i: (0, 0)),
                  _resident((1, D), lambda i: (0, 0)), _resident((G, C, C), lambda i: (0, 0, 0)),
                  _resident((G, C, 1), lambda i: (0, 0, 0))],
        out_specs=pl.BlockSpec((tm, D), lambda i: (i, 0)),
        compiler_params=_params("arbitrary"),
    )(proj, gn, bn, ws, bsc)


def _decay_tables(dl_ref):
    lg = jax.nn.log_sigmoid(dl_ref[0:2, :])
    lgf, lgb = lg[0:1, :], lg[1:2, :]
    assert RC <= DK
    ri = lax.broadcasted_iota(jnp.int32, (RC, RC), 0)
    ci = lax.broadcasted_iota(jnp.int32, (RC, RC), 1)
    d = (ri - ci).astype(f32)
    lower = d >= 0
    dmat = jnp.where(lower, jnp.exp(d * lgf[:, :RC]), jnp.exp(-d * lgb[:, :RC]))
    dmat_t = jnp.where(d <= 0, jnp.exp(-d * lgf[:, :RC]), jnp.exp(d * lgb[:, :RC]))
    pos = lax.broadcasted_iota(jnp.int32, (RC, DK), 0).astype(f32)
    t = dict(
        lgf=lgf, lgb=lgb, d=d, lower=lower, dmat=dmat, dmat_t=dmat_t, pos=pos,
        fq=jnp.exp((pos + 1.0) * lgf), fk=jnp.exp((RC - 1.0 - pos) * lgf),
        bq=jnp.exp((RC - pos) * lgb), bk=jnp.exp(pos * lgb),
        lamf=jnp.exp(float(RC) * lgf), lamb=jnp.exp(float(RC) * lgb),
    )
    return t


def _rotate(t, co, si):
    t1, t2 = t[:, :DK // 2], t[:, DK // 2:]
    return jnp.concatenate([t1 * co - t2 * si, t2 * co + t1 * si], axis=-1)


def _unrotate(t, co, si):
    t1, t2 = t[:, :DK // 2], t[:, DK // 2:]
    return jnp.concatenate([t1 * co + t2 * si, t2 * co - t1 * si], axis=-1)


K_SCALE = DK ** -0.5
ROT_ROWS = 512


def _rotate_into(q_ref, k_ref, cos_ref, sin_ref, qs_ref, ks_ref, S):
    for rt in range(S // ROT_ROWS):
        rs = slice(rt * ROT_ROWS, (rt + 1) * ROT_ROWS)
        co, si = cos_ref[rs, :], sin_ref[rs, :]
        qs_ref[rs, :] = _rotate(q_ref[rs, :].astype(f32), co, si).astype(CDT)
        ks_ref[rs, :] = (_rotate(k_ref[rs, :].astype(f32), co, si) * K_SCALE).astype(CDT)


def _ret_fwd(proj, cos, sin, dl, hooks=()):
    S = proj.shape[1]
    NC = S // RC

    def body(q_ref, k_ref, v_ref, g_ref, cos_ref, sin_ref, dl_ref, R_ref, r_ref, sfs_ref, sbs_ref,
             qs_ref, ks_ref, rb_ref, sf_ref, sb_ref):
        t = _decay_tables(dl_ref)
        _rotate_into(q_ref, k_ref, cos_ref, sin_ref, qs_ref, ks_ref, S)

        def chunk(n):
            rows = pl.ds(pl.multiple_of(n * RC, RC), RC)
            return rows, qs_ref[rows, :], ks_ref[rows, :], v_ref[rows, :]

        sf_ref[...] = jnp.zeros_like(sf_ref)
        sb_ref[...] = jnp.zeros_like(sb_ref)

        def step(i, carry):
            rows, qn, kn, vn = chunk(i)
            sc = _mm_nt(qn, kn) * t["dmat"]
            out = _mm(sc.astype(CDT), vn)
            sf = sf_ref[...]
            sfb = sf.astype(CDT)
            sfs_ref[i] = sfb
            R_ref[rows, :] = out + _mm((qn.astype(f32) * t["fq"]).astype(CDT), sfb)
            sf_ref[...] = sf * t["lamf"] + _mm_tn((kn.astype(f32) * t["fk"]).astype(CDT), vn)
            m = NC - 1 - i
            rows, qn, kn, vn = chunk(m)
            sb = sb_ref[...]
            sbb = sb.astype(CDT)
            sbs_ref[m] = sbb
            rb_ref[rows, :] = _mm((qn.astype(f32) * t["bq"]).astype(CDT), sbb)
            sb_ref[...] = sb * t["lamb"] + _mm_tn((kn.astype(f32) * t["bk"]).astype(CDT), vn)
            return carry

        lax.fori_loop(0, NC, step, 0)
        for rt in range(S // ROT_ROWS):
            rs = slice(rt * ROT_ROWS, (rt + 1) * ROT_ROWS)
            R = R_ref[rs, :] + rb_ref[rs, :]
            R_ref[rs, :] = R
            rn = R * lax.rsqrt(jnp.mean(R * R, axis=-1, keepdims=True) + EPS)
            g = g_ref[rs, :].astype(f32)
            r_ref[rs, :] = (rn * g * jax.nn.sigmoid(g)).astype(CDT)

    def seg(slot):
        return pl.BlockSpec((None, S, DK), lambda h: (slot, 0, h))

    states = jax.ShapeDtypeStruct((H, NC, DK, DK), CDT)
    state_blk = pl.BlockSpec((None, NC, DK, DK), lambda h: (h, 0, 0, 0))
    return _call(
        body, hooks, [proj, proj, proj, proj, cos, sin, dl], name="ret_fwd", grid=(H,),
        out_shape=[jax.ShapeDtypeStruct((S, H * DK), f32), jax.ShapeDtypeStruct((S, H * DK), CDT), states, states],
        in_specs=[seg(SLOT_Q), seg(SLOT_K), seg(SLOT_VR), seg(SLOT_GR),
                  _resident((S, DK // 2), lambda h: (0, 0)), _resident((S, DK // 2), lambda h: (0, 0)),
                  pl.BlockSpec((None, 8, DK), lambda h: (h, 0, 0))],
        out_specs=[pl.BlockSpec((S, DK), lambda h: (0, h)), pl.BlockSpec((S, DK), lambda h: (0, h)), state_blk, state_blk],
        scratch_shapes=[pltpu.VMEM((S, DK), CDT), pltpu.VMEM((S, DK), CDT), pltpu.VMEM((S, DK), f32),
                        pltpu.VMEM((DK, DK), f32), pltpu.VMEM((DK, DK), f32)],
        compiler_params=_params("arbitrary"),
    )


def _merge_fwd(a, r, proj, x1, wa, wb, wo, tm=512):
    S = x1.shape[0]

    def body(a_ref, r_ref, gt_ref, x_ref, wa_ref, wb_ref, wo_ref, x2_ref, ya_ref, yb_ref):
        ya = _mm(a_ref[...], wa_ref[...])
        yb = _mm(r_ref[...], wb_ref[...])
        ya_ref[...] = ya.astype(CDT)
        yb_ref[...] = yb.astype(CDT)
        mix = jax.nn.sigmoid(gt_ref[0].astype(f32)) * ya + jax.nn.sigmoid(gt_ref[1].astype(f32)) * yb
        x2_ref[...] = x_ref[...] + _mm(mix.astype(CDT), wo_ref[...])

    tok = pl.BlockSpec((tm, D), lambda i: (i, 0))
    return pl.pallas_call(
        body, name="merge_fwd", grid=(S // tm,),
        out_shape=[jax.ShapeDtypeStruct((S, D), f32), jax.ShapeDtypeStruct((S, D), CDT), jax.ShapeDtypeStruct((S, D), CDT)],
        in_specs=[tok, tok, pl.BlockSpec((2, tm, D), lambda i: (SLOT_GA // 2, i, 0)), tok,
                  _wblock(wa), _wblock(wb), _wblock(wo)],
        out_specs=[tok, tok, tok],
        compiler_params=_params("arbitrary"),
    )(a, r, proj, x1, wa, wb, wo)


def _loss_head(x3, fn, target, tm=512):
    S = x3.shape[0]

    def body(x_ref, n_ref, t_ref, dx_ref, dxh_ref, dn_ref, l_ref):
        n = n_ref[...]
        r, xh, y = _rms_fwd(x_ref[...], n)
        e = y - t_ref[...]
        dy = e * (1.0 / D)
        dx, dn = _rms_bwd(dy, r, xh, n)
        dx_ref[...] = dx
        dxh_ref[...] = (0.5 * dx).astype(CDT)
        part = 0.5 * jnp.sum(jnp.sum(e * e, axis=-1, keepdims=True), axis=0, keepdims=True) * (1.0 / D)

        @pl.when(pl.program_id(0) == 0)
        def _():
            dn_ref[...] = jnp.zeros_like(dn_ref)
            l_ref[...] = jnp.zeros_like(l_ref)

        dn_ref[...] += dn
        l_ref[...] += jnp.broadcast_to(part, l_ref.shape)

    tok = pl.BlockSpec((tm, D), lambda i: (i, 0))
    return pl.pallas_call(
        body, name="loss_head", grid=(S // tm,),
        out_shape=[jax.ShapeDtypeStruct((S, D), f32), jax.ShapeDtypeStruct((S, D), CDT), jax.ShapeDtypeStruct((1, D), f32),
                   jax.ShapeDtypeStruct((8, 128), f32)],
        in_specs=[tok, _resident((1, D), lambda i: (0, 0)), tok],
        out_specs=[tok, tok, pl.BlockSpec((1, D), lambda i: (0, 0)), pl.BlockSpec((8, 128), lambda i: (0, 0))],
        compiler_params=_params("arbitrary"),
    )(x3, fn, target)


def _ffn_bwd_hidden(name, dyh, g, u, wd, hooks=(), tm=512):
    S = dyh.shape[0]

    def body(dyh_ref, g_ref, u_ref, wd_ref, dg_ref, du_ref):
        dyh = dyh_ref[...]
        for ci in range(F // FT):
            sl = slice(ci * FT, (ci + 1) * FT)
            da = _mm_nt(dyh, wd_ref[sl, :])
            gv = g_ref[:, sl].astype(f32)
            uv = u_ref[:, sl].astype(f32)
            s = jax.nn.sigmoid(gv)
            du_ref[:, sl] = (da * (gv * s)).astype(CDT)
            dg_ref[:, sl] = (da * uv * (s * (1.0 + gv * (1.0 - s)))).astype(CDT)

    hid = pl.BlockSpec((tm, F), lambda i: (i, 0))
    hidden = jax.ShapeDtypeStruct((S, F), CDT)
    return _call(
        body, hooks, [dyh, g, u, wd], name=name, grid=(S // tm,), out_shape=[hidden, hidden],
        in_specs=[pl.BlockSpec((tm, D), lambda i: (i, 0)), hid, hid, _wblock(wd)], out_specs=[hid, hid],
        compiler_params=_params("arbitrary"),
    )


def _ffn_bwd_in(name, dy, x, dg, du, nrm, wg, wu, hooks=(), part=None, prev=None, tm=512):
    S = x.shape[0]
    t0, nt = part or (0, S // tm)

    def body(dy_ref, x_ref, dg_ref, du_ref, n_ref, wg_ref, wu_ref, *rest):
        dx_ref, dn_ref, acc_ref = rest[-3:]
        n = n_ref[...]
        r, xh, _ = _rms_fwd(x_ref[...], n)
        for ci in range(F // FT):
            sl = slice(ci * FT, (ci + 1) * FT)
            dh = _mm(dg_ref[:, sl], wg_ref[sl, :]) + _mm(du_ref[:, sl], wu_ref[sl, :])
            if ci == 0:
                acc_ref[...] = dh
            else:
                acc_ref[...] += dh
        dx, dn = _rms_bwd(acc_ref[...], r, xh, n)
        dx_ref[...] = dy_ref[...] + dx

        @pl.when(pl.program_id(0) == 0)
        def _():
            dn_ref[...] = jnp.zeros_like(dn_ref) if prev is None else rest[1][...]

        dn_ref[...] += dn

    tok = pl.BlockSpec((tm, D), lambda i: (t0 + i, 0))
    hid = pl.BlockSpec((tm, F), lambda i: (t0 + i, 0))
    row = pl.BlockSpec((1, D), lambda i: (0, 0))
    in_specs = [tok, tok, hid, hid, _resident((1, D), lambda i: (0, 0)), _wblock(wg), _wblock(wu)]
    operands = [dy, x, dg, du, nrm, wg, wu]
    aliases = {}
    if prev is not None:
        in_specs += [_HBM, row]
        operands += list(prev)
        aliases = {7: 0}
    return _call(
        body, hooks, operands, name=name, grid=(nt,),
        out_shape=[jax.ShapeDtypeStruct((S, D), f32), jax.ShapeDtypeStruct((1, D), f32)],
        in_specs=in_specs, out_specs=[tok, row],
        scratch_shapes=[pltpu.VMEM((tm, D), f32)],
        input_output_aliases=aliases,
        compiler_params=_params("arbitrary"),
    )


TN_ROWS = 512


def _tn(name, xs, ys, block_of, hooks=()):
    S, M = xs.shape
    B = ys.shape[0]
    tr = TN_ROWS if M % TN_ROWS == 0 else M // 2
    assert M % tr == 0 and tr % 128 == 0
    nt = M // tr

    def body(x_ref, y_ref, o_ref):
        o_ref[...] = _mm_tn(x_ref[...], y_ref[...]).astype(CDT)

    return _call(
        body, hooks, [xs, ys], name=name, grid=(B, nt),
        out_shape=[jax.ShapeDtypeStruct((B * M, D), CDT)],
        in_specs=[pl.BlockSpec((S, tr), lambda b, i: (0, i)), pl.BlockSpec((None, S, D), lambda b, i: (b, 0, 0))],
        out_specs=[pl.BlockSpec((tr, D), lambda b, i: (block_of(b) * nt + i, 0))],
        compiler_params=_params("arbitrary", "arbitrary"),
    )


def _wgrad(name, xs, y, hooks=()):
    return _tn(name, xs, y[None], lambda b: 0, hooks)


def _merge_bwd_act(dx2, ya, yb, proj, wa, wb, wo, hooks=(), tm=512):
    S = dx2.shape[0]

    def body(dx_ref, ya_ref, yb_ref, gt_ref, wa_ref, wb_ref, wo_ref,
             dp_ref, da_ref, dr_ref, mix_ref, dxb_ref, dya_ref, dyb_ref):
        dxb = dx_ref[...].astype(CDT)
        dxb_ref[...] = dxb
        dmix = _mm_nt(dxb, wo_ref[...])
        ya = ya_ref[...].astype(f32)
        yb = yb_ref[...].astype(f32)
        sa = jax.nn.sigmoid(gt_ref[0].astype(f32))
        sb = jax.nn.sigmoid(gt_ref[1].astype(f32))
        mix_ref[...] = (sa * ya + sb * yb).astype(CDT)
        dya = (dmix * sa).astype(CDT)
        dyb = (dmix * sb).astype(CDT)
        dya_ref[...] = dya
        dyb_ref[...] = dyb
        dp_ref[0] = (dmix * ya * sa * (1.0 - sa)).astype(CDT)
        dp_ref[1] = (dmix * yb * sb * (1.0 - sb)).astype(CDT)
        da_ref[...] = _mm_nt(dya, wa_ref[...]).astype(CDT)
        dr_ref[...] = _mm_nt(dyb, wb_ref[...]).astype(CDT)

    tok = pl.BlockSpec((tm, D), lambda i: (i, 0))
    gates = pl.BlockSpec((2, tm, D), lambda i: (SLOT_GA // 2, i, 0))
    act = jax.ShapeDtypeStruct((S, D), CDT)
    return _call(
        body, hooks, [dx2, ya, yb, proj, wa, wb, wo], name="merge_bwd_act", grid=(S // tm,),
        out_shape=[jax.ShapeDtypeStruct((8, S, D), CDT), act, act, act, act, act, act],
        in_specs=[tok, tok, tok, gates, _wblock(wa), _wblock(wb), _wblock(wo)],
        out_specs=[gates, tok, tok, tok, tok, tok, tok],
        compiler_params=_params("arbitrary"),
    )


def _sgu_bwd(da, proj, dproj, gn, bn, ws, wst, bsc, hooks=(), tm=512):
    S = proj.shape[1]
    GW = D // G

    def body(da_ref, p_ref, dpin_ref, gn_ref, bn_ref, ws_ref, wst_ref, bs_ref,
             dp_ref, dws_ref, dbs_ref, dgn_ref, dbn_ref, ds_ref, dvn_ref):
        @pl.when(pl.program_id(0) == 0)
        def _():
            dws_ref[...] = jnp.zeros_like(dws_ref)
            dbs_ref[...] = jnp.zeros_like(dbs_ref)
            dgn_ref[...] = jnp.zeros_like(dgn_ref)
            dbn_ref[...] = jnp.zeros_like(dbn_ref)

        pu = p_ref[0].astype(f32)
        pv = p_ref[1].astype(f32)
        ua = _gelu(pu)
        va = _gelu(pv)
        gn = gn_ref[...]
        rstd, vhat, vn = _sgu_norm(va, gn, bn_ref[...])
        vnb = vn.astype(CDT)
        dav = da_ref[...].astype(f32)
        dsb = (dav * ua).astype(CDT)
        ones = jnp.ones((8, GW), CDT)
        for ch in range(tm // C):
            rs = slice(ch * C, (ch + 1) * C)
            for gi in range(G):
                cs = slice(gi * GW, (gi + 1) * GW)
                s = _mm(ws_ref[gi], vnb[rs, cs]) + bs_ref[gi]
                ds_ref[rs, cs] = s
                dsg = dsb[rs, cs]
                dws_ref[gi] += _mm_nt(dsg, vnb[rs, cs])
                dbs_ref[gi] += _mm_nt(ones, dsg)
                dvn_ref[rs, cs] = _mm(wst_ref[gi], dsg)
        dp_ref[0] = (dav * ds_ref[...] * _gelu_grad(pu)).astype(CDT)
        dvn = dvn_ref[...]
        dgn_ref[...] += jnp.sum(dvn * vhat, axis=0, keepdims=True)
        dbn_ref[...] += jnp.sum(dvn, axis=0, keepdims=True)
        dvh = dvn * gn
        dva = rstd * (dvh - jnp.mean(dvh, axis=-1, keepdims=True) - vhat * jnp.mean(dvh * vhat, axis=-1, keepdims=True))
        dp_ref[1] = (dva * _gelu_grad(pv)).astype(CDT)

    uv = pl.BlockSpec((2, tm, D), lambda i: (0, i, 0))
    row = _resident((1, D), lambda i: (0, 0))
    return _call(
        body, hooks, [da, proj, dproj, gn, bn, ws, wst, bsc], name="sgu_bwd", grid=(S // tm,),
        out_shape=[jax.ShapeDtypeStruct(dproj.shape, CDT), jax.ShapeDtypeStruct((G, C, C), f32),
                   jax.ShapeDtypeStruct((G, 8, C), f32), jax.ShapeDtypeStruct((1, D), f32), jax.ShapeDtypeStruct((1, D), f32)],
        in_specs=[pl.BlockSpec((tm, D), lambda i: (i, 0)), uv, _HBM, row, row,
                  _resident((G, C, C), lambda i: (0, 0, 0)), _resident((G, C, C), lambda i: (0, 0, 0)),
                  _resident((G, C, 1), lambda i: (0, 0, 0))],
        out_specs=[uv, pl.BlockSpec((G, C, C), lambda i: (0, 0, 0)), pl.BlockSpec((G, 8, C), lambda i: (0, 0, 0)),
                   pl.BlockSpec((1, D), lambda i: (0, 0)), pl.BlockSpec((1, D), lambda i: (0, 0))],
        scratch_shapes=[pltpu.VMEM((tm, D), f32), pltpu.VMEM((tm, D), f32)],
        input_output_aliases={2: 0},
        compiler_params=_params("arbitrary"),
    )


def _ret_bwd(dr, R, sfs, sbs, proj, dproj, cos, sin, dl, hooks=()):
    S = proj.shape[1]
    NC = S // RC
    assert NC % 2 == 0

    def body(dr_ref, R_ref, sf_ref, sb_ref, q_ref, k_ref, v_ref, g_ref, dpin_ref, cos_ref, sin_ref, dl_ref,
             dp_ref, dd_ref, qs_ref, ks_ref, dR_ref, dk_ref, dv_ref, gb_ref, gf_ref, acc_ref):
        t = _decay_tables(dl_ref)
        _rotate_into(q_ref, k_ref, cos_ref, sin_ref, qs_ref, ks_ref, S)
        for rt in range(S // ROT_ROWS):
            rs = slice(rt * ROT_ROWS, (rt + 1) * ROT_ROWS)
            Rv = R_ref[rs, :]
            rstd = lax.rsqrt(jnp.mean(Rv * Rv, axis=-1, keepdims=True) + EPS)
            rn = Rv * rstd
            gv = g_ref[rs, :].astype(f32)
            s = jax.nn.sigmoid(gv)
            drv = dr_ref[rs, :].astype(f32)
            dp_ref[3, rs, :] = (drv * rn * (s * (1.0 + gv * (1.0 - s)))).astype(CDT)
            drn = drv * gv * s
            dR_ref[rs, :] = (rstd * (drn - rn * jnp.mean(drn * rn, axis=-1, keepdims=True))).astype(CDT)

        def chunk(n):
            rows = pl.ds(pl.multiple_of(n * RC, RC), RC)
            return rows, qs_ref[rows, :], ks_ref[rows, :], v_ref[rows, :], dR_ref[rows, :]

        def emit_kv(rows, dk, dv, final):
            if not final:
                dk_ref[rows, :] = dk
                dv_ref[rows, :] = dv
            else:
                co, si = cos_ref[rows, :], sin_ref[rows, :]
                dp_ref[1, rows, :] = (_unrotate(dk_ref[rows, :] + dk, co, si) * K_SCALE).astype(CDT)
                dp_ref[2, rows, :] = (dv_ref[rows, :] + dv).astype(CDT)

        gb_ref[...] = jnp.zeros_like(gb_ref)
        gf_ref[...] = jnp.zeros_like(gf_ref)
        acc_ref[...] = jnp.zeros_like(acc_ref)
        dpos = jnp.abs(t["d"])

        def ascend(n, final):
            rows, qn, kn, vn, dRn = chunk(n)
            qf, kf = qn.astype(f32), kn.astype(f32)
            sc = _mm_nt(qn, kn)
            dA = _mm_nt(dRn, vn)
            w = sc * dA * t["dmat"] * dpos
            lgf_part = jnp.sum(jnp.where(t["lower"], w, 0.0), axis=0, keepdims=True)
            lgb_part = jnp.sum(jnp.where(t["lower"], 0.0, w), axis=0, keepdims=True)
            dsc = (dA * t["dmat"]).astype(CDT)
            dq = _mm(dsc, kn)
            scT = (_mm_nt(kn, qn) * t["dmat_t"]).astype(CDT)
            dscT = (_mm_nt(vn, dRn) * t["dmat_t"]).astype(CDT)
            dk = _mm(dscT, qn)
            dv = _mm(scT, dRn)
            sfb = sf_ref[n]
            sbb = sb_ref[n]
            qdf = qf * t["fq"]
            dqdf = _mm_nt(dRn, sfb)
            dq += dqdf * t["fq"]
            lgf_row = jnp.sum(qdf * dqdf * (t["pos"] + 1.0), axis=0, keepdims=True)
            qdb = qf * t["bq"]
            dqdb = _mm_nt(dRn, sbb)
            dq += dqdb * t["bq"]
            lgb_row = jnp.sum(qdb * dqdb * (RC - t["pos"]), axis=0, keepdims=True)
            gb = gb_ref[...]
            gbb = gb.astype(CDT)
            kdb = kf * t["bk"]
            dkdb = _mm_nt(vn, gbb)
            dk += dkdb * t["bk"]
            dv += _mm(kdb.astype(CDT), gbb)
            lgb_row += jnp.sum(kdb * dkdb * t["pos"], axis=0, keepdims=True)
            lgb_row += float(RC) * t["lamb"] * jnp.sum(gb * sbb.astype(f32), axis=0, keepdims=True)
            co, si = cos_ref[rows, :], sin_ref[rows, :]
            dp_ref[0, rows, :] = _unrotate(dq, co, si).astype(CDT)
            emit_kv(rows, dk, dv, final)
            acc_ref[0:1, :] += lgf_row + lgf_part
            acc_ref[1:2, :] += lgb_row + lgb_part
            gb_ref[...] = gb * t["lamb"] + _mm_tn(qdb.astype(CDT), dRn)

        def descend(n, final):
            rows, qn, kn, vn, dRn = chunk(n)
            gf = gf_ref[...]
            gfb = gf.astype(CDT)
            kdf = kn.astype(f32) * t["fk"]
            dkdf = _mm_nt(vn, gfb)
            lgf_row = jnp.sum(kdf * dkdf * (RC - 1.0 - t["pos"]), axis=0, keepdims=True)
            lgf_row += float(RC) * t["lamf"] * jnp.sum(gf * sf_ref[n].astype(f32), axis=0, keepdims=True)
            acc_ref[0:1, :] += lgf_row
            emit_kv(rows, dkdf * t["fk"], _mm(kdf.astype(CDT), gfb), final)
            gf_ref[...] = gf * t["lamf"] + _mm_tn((qn.astype(f32) * t["fq"]).astype(CDT), dRn)

        def sweep(final):
            def step(i, carry):
                ascend(i, final)
                descend(NC - 1 - i, final)
                return carry
            return step

        lax.fori_loop(0, NC // 2, sweep(False), 0)
        lax.fori_loop(NC // 2, NC, sweep(True), 0)
        dlg = jnp.sum(acc_ref[...], axis=1, keepdims=True)
        dlogit = dlg * jax.nn.sigmoid(-dl_ref[:, 0:1])
        lane = lax.broadcasted_iota(jnp.int32, (8, 128), 1)
        dd_ref[...] = jnp.where(lane == pl.program_id(0), jnp.broadcast_to(dlogit, (8, 128)), 0.0)

    def seg(slot):
        return pl.BlockSpec((None, S, DK), lambda h: (slot, 0, h), pipeline_mode=pl.Buffered(1))

    head = pl.BlockSpec((S, DK), lambda h: (0, h), pipeline_mode=pl.Buffered(1))
    states = pl.BlockSpec((None, NC, DK, DK), lambda h: (h, 0, 0, 0), pipeline_mode=pl.Buffered(1))
    return _call(
        body, hooks, [dr, R, sfs, sbs, proj, proj, proj, proj, dproj, cos, sin, dl], name="ret_bwd", grid=(H,),
        out_shape=[jax.ShapeDtypeStruct(dproj.shape, CDT), jax.ShapeDtypeStruct((H, 8, 128), f32)],
        in_specs=[head, head, states, states, seg(SLOT_Q), seg(SLOT_K), seg(SLOT_VR), seg(SLOT_GR), _HBM,
                  _resident((S, DK // 2), lambda h: (0, 0)), _resident((S, DK // 2), lambda h: (0, 0)),
                  pl.BlockSpec((None, 8, DK), lambda h: (h, 0, 0))],
        out_specs=[pl.BlockSpec((4, S, DK), lambda h: (1, 0, h), pipeline_mode=pl.Buffered(1)),
                   pl.BlockSpec((None, 8, 128), lambda h: (h, 0, 0))],
        scratch_shapes=[pltpu.VMEM((S, DK), CDT), pltpu.VMEM((S, DK), CDT), pltpu.VMEM((S, DK), CDT),
                        pltpu.VMEM((S, DK), f32), pltpu.VMEM((S, DK), f32),
                        pltpu.VMEM((DK, DK), f32), pltpu.VMEM((DK, DK), f32), pltpu.VMEM((8, DK), f32)],
        input_output_aliases={8: 0},
        compiler_params=_params("arbitrary"),
    )


def _proj_bwd_act(dproj, dx2, x1, nrm, wfull, hooks=(), tm=512):
    S = x1.shape[0]

    def body(dp_ref, dx2_ref, x_ref, n_ref, w_ref, dx_ref, dxh_ref, dn_ref, db_ref, acc_ref):
        @pl.when(pl.program_id(0) == 0)
        def _():
            dn_ref[...] = jnp.zeros_like(dn_ref)
            db_ref[...] = jnp.zeros_like(db_ref)

        for p in range(8):
            seg = SEG_OF_SLOT[p]
            dp = dp_ref[p]
            db_ref[seg] += jnp.sum(dp.astype(f32), axis=0, keepdims=True)
            dh = _mm_nt(dp, w_ref[seg * D:(seg + 1) * D, :])
            if p == 0:
                acc_ref[...] = dh
            else:
                acc_ref[...] += dh
        n = n_ref[...]
        r, xh, _ = _rms_fwd(x_ref[...], n)
        dx, dn = _rms_bwd(acc_ref[...], r, xh, n)
        dx = dx2_ref[...] + dx
        dx_ref[...] = dx
        dxh_ref[...] = (0.5 * dx).astype(CDT)
        dn_ref[...] += dn

    tok = pl.BlockSpec((tm, D), lambda i: (i, 0))
    return _call(
        body, hooks, [dproj, dx2, x1, nrm, wfull], name="proj_bwd_act", grid=(S // tm,),
        out_shape=[jax.ShapeDtypeStruct((S, D), f32), jax.ShapeDtypeStruct((S, D), CDT), jax.ShapeDtypeStruct((1, D), f32),
                   jax.ShapeDtypeStruct((8, 1, D), f32)],
        in_specs=[pl.BlockSpec((8, tm, D), lambda i: (0, i, 0)), tok, tok, _resident((1, D), lambda i: (0, 0)),
                  _resident((8 * D, D), lambda i: (0, 0))],
        out_specs=[tok, tok, pl.BlockSpec((1, D), lambda i: (0, 0)), pl.BlockSpec((8, 1, D), lambda i: (0, 0, 0))],
        scratch_shapes=[pltpu.VMEM((tm, D), f32)],
        compiler_params=_params("arbitrary"),
    )


def _rs_sum(name, gfulls, lands, my_c):
    n = len(gfulls)
    rows = gfulls[0].shape[0] // NDEV
    assert all(g.shape[0] == NDEV * rows for g in gfulls)

    def body(c_ref, *refs):
        for g_ref, l_ref, o_ref in zip(refs[:n], refs[n:2 * n], refs[2 * n:]):
            o_ref[...] = (g_ref[...].astype(f32) + l_ref[...].astype(f32)).astype(CDT)

    slot = pl.BlockSpec((None, rows, D), lambda k, c: (k, 0, 0))
    return pl.pallas_call(
        body, name=name,
        grid_spec=pltpu.PrefetchScalarGridSpec(
            num_scalar_prefetch=1, grid=(NCHIP,),
            in_specs=[pl.BlockSpec((rows, D), lambda k, c: (2 * k + c[0], 0))] * n + [slot] * n,
            out_specs=[slot] * n),
        out_shape=[jax.ShapeDtypeStruct((NCHIP, rows, D), CDT)] * n,
        compiler_params=_params("arbitrary"),
    )(my_c, *gfulls, *lands)


def _adamw_math(g, w, m, v):
    m2 = ADAM_B1 * m + (1.0 - ADAM_B1) * g
    v2 = ADAM_B2 * v + (1.0 - ADAM_B2) * (g * g)
    delta = -ADAM_LR * ((m2 / BC1) / (jnp.sqrt(v2 / BC2) + ADAM_EPS) + ADAM_WD * w)
    return delta, m2, v2


def _adamw_big(name, landed, w, m, v, hooks=()):
    rows = w.shape[0]
    tr = min(rows, 256) if rows % 256 == 0 else rows

    def body(l_ref, w_ref, m_ref, v_ref, g_ref, d_ref, m2_ref, v2_ref):
        g = l_ref[0].astype(f32)
        for k in range(1, NCHIP):
            g = g + l_ref[k].astype(f32)
        g_ref[...] = g
        d_ref[...], m2_ref[...], v2_ref[...] = _adamw_math(g, w_ref[...], m_ref[...], v_ref[...])

    blk = pl.BlockSpec((tr, D), lambda i: (i, 0))
    o = jax.ShapeDtypeStruct((rows, D), f32)
    return _call(
        body, hooks, [landed, w, m, v], name=name, grid=(rows // tr,), out_shape=[o, o, o, o],
        in_specs=[pl.BlockSpec((NCHIP, tr, D), lambda i: (0, i, 0)), blk, blk, blk],
        out_specs=[blk, blk, blk, blk],
        compiler_params=_params("arbitrary"),
    )


ROW_FFN1_NORM, ROW_MIX_NORM, ROW_SGU_G, ROW_SGU_B, ROW_FFN2_NORM, ROW_FINAL_NORM, ROW_B_IN = 0, 1, 2, 3, 4, 5, 8
ROW_WS, ROW_BS, ROW_DECAY = 0, G * C, G * C + G * 8


def _adamw_small(ga, gb, params):
    def body(ga_ref, gb_ref, *refs):
        ins, outs = refs[:30], refs[30:]

        def total(ref, r0, n):
            g = ref[0, r0:r0 + n, :]
            for j in range(1, NDEV):
                g = g + ref[j, r0:r0 + n, :]
            return g

        def apply(i, g, rows=slice(None)):
            w, m, v = ins[3 * i][rows, :], ins[3 * i + 1][rows, :], ins[3 * i + 2][rows, :]
            outs[4 * i][rows, :] = g
            outs[4 * i + 1][rows, :], outs[4 * i + 2][rows, :], outs[4 * i + 3][rows, :] = _adamw_math(g, w, m, v)

        for i, r in enumerate((ROW_FFN1_NORM, ROW_MIX_NORM, ROW_SGU_G, ROW_SGU_B, ROW_FFN2_NORM, ROW_FINAL_NORM)):
            apply(i, total(ga_ref, r, 1))
        apply(6, total(ga_ref, ROW_B_IN, 8))
        apply(7, total(gb_ref, ROW_WS, G * C))
        for gi in range(G):
            apply(8, total(gb_ref, ROW_BS + 8 * gi, 1), slice(gi, gi + 1))
        dec = total(gb_ref, ROW_DECAY, 8)
        for hh in range(1, H):
            dec = dec + total(gb_ref, ROW_DECAY + 8 * hh, 8)
        apply(9, dec)

    flat = [a for p in params for a in p]
    out_shape = [jax.ShapeDtypeStruct(p[0].shape, f32) for p in params for _ in range(4)]
    vm = pl.BlockSpec(memory_space=pltpu.VMEM)
    return pl.pallas_call(
        body, name="adamw_small", out_shape=out_shape,
        in_specs=[vm] * (2 + len(flat)), out_specs=[vm] * len(out_shape),
        compiler_params=pltpu.CompilerParams(vmem_limit_bytes=VMEM_LIMIT),
    )(ga, gb, *flat)


def kernel(x, ffn1_norm, ffn1_w_gate, ffn1_w_up, ffn1_w_down, mix_norm, w_in, b_in, sgu_norm_g, sgu_norm_b, sgu_w_s, sgu_b_s, ret_decay_logit, w_branch_a, w_branch_b, w_out, ffn2_norm, ffn2_w_gate, ffn2_w_up, ffn2_w_down, final_norm, loss_target, m_ffn1_norm, m_ffn1_w_gate, m_ffn1_w_up, m_ffn1_w_down, m_mix_norm, m_w_in, m_b_in, m_sgu_norm_g, m_sgu_norm_b, m_sgu_w_s, m_sgu_b_s, m_ret_decay_logit, m_w_branch_a, m_w_branch_b, m_w_out, m_ffn2_norm, m_ffn2_w_gate, m_ffn2_w_up, m_ffn2_w_down, m_final_norm, v_ffn1_norm, v_ffn1_w_gate, v_ffn1_w_up, v_ffn1_w_down, v_mix_norm, v_w_in, v_b_in, v_sgu_norm_g, v_sgu_norm_b, v_sgu_w_s, v_sgu_b_s, v_ret_decay_logit, v_w_branch_a, v_w_branch_b, v_w_out, v_ffn2_norm, v_ffn2_w_gate, v_ffn2_w_up, v_ffn2_w_down, v_final_norm):
    args = dict(locals())
    S = x.shape[1]
    xs = x[0]
    target = loss_target[0]

    def buf_layout(name, a):
        a = a[0]
        return a.T if name in W_TRANSPOSED else a

    sh = {n: buf_layout(n, args[n]).astype(CDT) for n in W_NAMES}
    wf = {}

    b3 = b_in.reshape(8, 1, D)
    ws = sgu_w_s[0].astype(CDT)
    wst = jnp.swapaxes(sgu_w_s[0], 1, 2).astype(CDT)
    bsc = sgu_b_s[0].reshape(G, C, 1)
    dl = jnp.zeros((H, 8, DK), f32).at[:, 0:2, :].set(jnp.broadcast_to(ret_decay_logit[0].T[:, :, None], (H, 2, DK)))
    theta = ROPE_BASE ** (-jnp.arange(0, DK, 2, dtype=f32) / DK)
    ang = jnp.arange(S, dtype=f32)[:, None] * theta[None, :]
    cos, sin = jnp.cos(ang), jnp.sin(ang)
    fnorm = final_norm.reshape(1, D)

    f1 = ("ffn1_w_gate", "ffn1_w_up", "ffn1_w_down")
    f2 = ("ffn2_w_gate", "ffn2_w_up", "ffn2_w_down")
    br = ("w_branch_a", "w_branch_b", "w_out")
    wf[f1[0]], wf[f1[1]], wf[f1[2]] = _exchange("ag_ffn1", [_ag_hook(sh[n]) for n in f1])
    x1, g1, u1, a1, hf1, wf["w_in"] = _ffn_fwd("ffn1_fwd", xs, ffn1_norm, *[wf[n] for n in f1], [_ag_hook(sh["w_in"])])
    proj, h2, wf[br[0]], wf[br[1]], wf[br[2]], wf[f2[0]] = _proj_fwd(
        x1, mix_norm, wf["w_in"], b3, [_ag_hook(sh[n]) for n in br + f2[:1]])
    a = _sgu_fwd(proj, sgu_norm_g, sgu_norm_b, ws, bsc)
    R, r, sfs, sbs, wf[f2[1]], wf[f2[2]] = _ret_fwd(proj, cos, sin, dl, [_ag_hook(sh[n]) for n in f2[1:]])
    x2, ya, yb = _merge_fwd(a, r, proj, x1, *[wf[n] for n in br])
    x3, g2, u2, a2, hf2 = _ffn_fwd("ffn2_fwd", x2, ffn2_norm, *[wf[n] for n in f2])
    dx3, dyh2, d_final, loss_part = _loss_head(x3, fnorm, target)
    loss = lax.psum(loss_part[0, 0], ("x", "y", "c"))

    my_c = lax.axis_index("c").astype(jnp.int32).reshape(1)
    gw, landed = {}, {}

    def d2d(*names):
        return [_rs_d2d_hook(gw[n]) for n in names]

    def ici(parts):
        return [_rs_ici_hook(p) for p in parts]

    def rs_sum(names, sibs):
        return _rs_sum("rs_sum_" + names[0], [gw[n] for n in names], list(sibs), my_c)

    def ffn_bwd(tag, names, dy, dyh, x, g, u, a, h, nrm, split, before):
        wg, wu, wd = names
        gw[wd], *rest = _wgrad(tag + "_wd_grad", a, dyh, before)
        dg, du, sib = _ffn_bwd_hidden(tag + "_bwd_hidden", dyh, g, u, wf[wd], d2d(wd))
        (part,) = rs_sum([wd], [sib])
        gw[wg], landed[wd] = _wgrad(tag + "_wg_grad", dg, h, ici([part]))
        gw[wu], sib = _wgrad(tag + "_wu_grad", du, h, d2d(wg))
        (part,) = rs_sum([wg], [sib])
        args_in = (dy, x, dg, du, nrm, wf[wg], wf[wu])
        if not split:
            dx, dn, landed[wg], sib = _ffn_bwd_in(tag + "_bwd_in", *args_in, ici([part]) + d2d(wu))
            return dx, dn, rs_sum([wu], [sib]), rest
        half = x.shape[0] // 512 // 2
        dx, dn, landed[wg], sib = _ffn_bwd_in(tag + "_bwd_in_a", *args_in, ici([part]) + d2d(wu), part=(0, half))
        (part,) = rs_sum([wu], [sib])
        dx, dn, landed[wu] = _ffn_bwd_in(tag + "_bwd_in_b", *args_in, ici([part]), part=(half, half), prev=(dx, dn))
        return dx, dn, [], rest

    dx2, d_ffn2n, part_f2u, _ = ffn_bwd("ffn2", f2, dx3, dyh2, x2, g2, u2, a2, hf2, ffn2_norm, False, [])
    dproj, da, dr, mix, dx2b, dya, dyb, landed[f2[1]] = _merge_bwd_act(dx2, ya, yb, proj, *[wf[n] for n in br],
                                                                       ici(part_f2u))
    dproj, d_ws, d_bs, d_gn, d_bn = _sgu_bwd(da, proj, dproj, sgu_norm_g, sgu_norm_b, ws, wst, bsc)
    dproj, d_dec = _ret_bwd(dr, R, sfs, sbs, proj, dproj, cos, sin, dl)
    (gw["w_in"],) = _tn("win_grad", h2, dproj, _seg_of_slot)
    gw["w_out"], sib_win = _wgrad("wo_grad", mix, dx2b, d2d("w_in"))
    (part_win,) = rs_sum(["w_in"], [sib_win])
    (gw["w_branch_a"],) = _wgrad("wa_grad", a, dya)
    (gw["w_branch_b"],) = _wgrad("wb_grad", r, dyb)
    dx1, dyh1, d_mixn, d_bin, landed["w_in"], *sib = _proj_bwd_act(dproj, dx2, x1, mix_norm, wf["w_in"],
                                                                   ici([part_win]) + d2d(*br))
    part_br = rs_sum(br, sib)
    dxs, d_ffn1n, _, rest = ffn_bwd("ffn1", f1, dx1, dyh1, xs, g1, u1, a1, hf1, ffn1_norm, True, ici(part_br))
    landed.update(zip(br, rest))
    small_a = jnp.concatenate([d_ffn1n, d_mixn, d_gn, d_bn, d_ffn2n, d_final, jnp.zeros((2, D), f32), d_bin.reshape(8, D)], axis=0)
    small_b = jnp.concatenate([d_ws.reshape(G * C, C), d_bs.reshape(G * 8, C), d_dec.reshape(H * 8, 128)], axis=0)
    ga, gb = _exchange("ag_small", [_small_hook(small_a, small_b)])

    out = {"loss": loss, "grad_x": dxs[None]}

    def native(name, a):
        a = a.T if name in W_TRANSPOSED else a
        return a[None]

    for n in W_NAMES:
        res = _adamw_big("adamw_" + n, landed[n], buf_layout(n, args[n]), buf_layout(n, args["m_" + n]),
                         buf_layout(n, args["v_" + n]))
        for pre, val in zip(("grad_", "delta_", "new_m_", "new_v_"), res):
            out[pre + n] = native(n, val)

    def pad_decay(a):
        return jnp.zeros((8, 128), f32).at[0:2, 0:H].set(a[0])

    small = [
        ("ffn1_norm", lambda a: a, lambda a: a), ("mix_norm", lambda a: a, lambda a: a),
        ("sgu_norm_g", lambda a: a, lambda a: a), ("sgu_norm_b", lambda a: a, lambda a: a),
        ("ffn2_norm", lambda a: a, lambda a: a),
        ("final_norm", lambda a: a.reshape(1, D), lambda a: a.reshape(D)),
        ("b_in", lambda a: a.reshape(8, D), lambda a: a.reshape(1, 8 * D)),
        ("sgu_w_s", lambda a: a.reshape(G * C, C), lambda a: a.reshape(1, G, C, C)),
        ("sgu_b_s", lambda a: a[0], lambda a: a[None]),
        ("ret_decay_logit", pad_decay, lambda a: a[None, 0:2, 0:H]),
    ]
    res = _adamw_small(ga, gb, [(to(args[n]), to(args["m_" + n]), to(args["v_" + n])) for n, to, _ in small])
    for i, (n, _, back) in enumerate(small):
        for j, pre in enumerate(("grad_", "delta_", "new_m_", "new_v_")):
            out[pre + n] = back(res[4 * i + j])

    weights = ("ffn1_norm", "ffn1_w_gate", "ffn1_w_up", "ffn1_w_down", "mix_norm", "w_in", "b_in", "sgu_norm_g",
               "sgu_norm_b", "sgu_w_s", "sgu_b_s", "ret_decay_logit", "w_branch_a", "w_branch_b", "w_out", "ffn2_norm",
               "ffn2_w_gate", "ffn2_w_up", "ffn2_w_down", "final_norm")
    return (out["loss"], out["grad_x"], *[out["grad_" + n] for n in weights], *[out["delta_" + n] for n in weights],
            *[out["new_m_" + n] for n in weights], *[out["new_v_" + n] for n in weights])
```

```python
import functools
import math

import jax
import jax.numpy as jnp
from jax import lax
from jax.experimental import pallas as pl
from jax.experimental.pallas import tpu as pltpu

f32 = jnp.float32
CDT = jnp.bfloat16

D = 1024
F = 2816
C = 128
RC = 256
H = 4
DK = 256
G = 4
NDEV = 8
NCHIP = 4
EPS = 1e-6
ROPE_BASE = 10000.0
FT = 256
V7X_VMEM_BYTES = 64 * 1024 * 1024
VMEM_LIMIT = V7X_VMEM_BYTES - 8 * 1024 * 1024

ADAM_LR, ADAM_B1, ADAM_B2, ADAM_EPS, ADAM_WD, ADAM_STEP = 0.001, 0.9, 0.999, 1e-08, 0.01, 10
BC1 = 1.0 - ADAM_B1 ** ADAM_STEP
BC2 = 1.0 - ADAM_B2 ** ADAM_STEP

W_ROWS = dict(ffn1_w_gate=352, ffn1_w_up=352, ffn1_w_down=352, w_in=1024, w_branch_a=128, w_branch_b=128, w_out=128,
              ffn2_w_gate=352, ffn2_w_up=352, ffn2_w_down=352)
W_NAMES = tuple(W_ROWS)
W_TRANSPOSED = ("ffn1_w_gate", "ffn1_w_up", "ffn2_w_gate", "ffn2_w_up")

SLOT_U, SLOT_V, SLOT_GA, SLOT_GB, SLOT_Q, SLOT_K, SLOT_VR, SLOT_GR = range(8)


SEG_OF_SLOT = (0, 1, 6, 7, 2, 3, 4, 5)


def _seg_of_slot(p):
    return jnp.where(p < 2, p, jnp.where(p < 4, p + 4, p - 2))


def _mm(a, b):
    return jnp.dot(a, b, preferred_element_type=f32)


def _mm_nt(a, b):
    return lax.dot_general(a, b, (((1,), (1,)), ((), ())), preferred_element_type=f32)


def _mm_tn(a, b):
    return lax.dot_general(a, b, (((0,), (0,)), ((), ())), preferred_element_type=f32)


def _params(*sem):
    return pltpu.CompilerParams(dimension_semantics=sem, vmem_limit_bytes=VMEM_LIMIT)


def _resident(shape, index_map):
    return pl.BlockSpec(shape, index_map, pipeline_mode=pl.Buffered(1))


def _gelu(x):
    return 0.5 * x * (1.0 + lax.erf(x * (1.0 / math.sqrt(2.0))))


def _gelu_grad(x):
    return 0.5 * (1.0 + lax.erf(x * (1.0 / math.sqrt(2.0)))) + x * jnp.exp(-0.5 * x * x) * (1.0 / math.sqrt(2.0 * math.pi))


def _rms_fwd(x, n):
    r = lax.rsqrt(jnp.mean(x * x, axis=-1, keepdims=True) + EPS)
    xh = x * r
    return r, xh, xh * n


def _rms_bwd(dh, r, xh, n):
    dxh = dh * n
    dx = r * (dxh - xh * jnp.mean(dxh * xh, axis=-1, keepdims=True))
    return dx, jnp.sum(dh * xh, axis=0, keepdims=True)


MESH_ID = pl.DeviceIdType.MESH
_HBM = pl.BlockSpec(memory_space=pltpu.HBM)


def _my_place():
    return lax.axis_index("x"), lax.axis_index("y"), lax.axis_index("c")


def _ici_peers(x, y, c):
    return [((1 - x, y, c), 2 * (1 - x) + y), ((x, 1 - y, c), 2 * x + 1 - y), ((1 - x, 1 - y, c), 2 * (1 - x) + 1 - y)]


class _Hook:
    def __init__(self, operands, out_shapes, n_remote, n_local, start, finish, relay=None):
        self.operands, self.out_shapes = list(operands), list(out_shapes)
        self.n_remote, self.n_local, self.start, self.finish = n_remote, n_local, start, finish
        self.relay = relay or (lambda *a: None)


def _call(body, hooks, operands, *, in_specs, out_specs, out_shape, grid=None, scratch_shapes=(), **kw):
    hooks = tuple(hooks)
    n_in, n_out, n_scr = len(in_specs), len(out_shape), len(scratch_shapes)
    h_ops = [a for h in hooks for a in h.operands]
    h_outs = [s for h in hooks for s in h.out_shapes]
    h_sems = [pltpu.SemaphoreType.DMA((n,)) for h in hooks for n in (h.n_remote, h.n_remote, max(h.n_local, 1))]

    def wrapped(*refs):
        ins, hin = refs[:n_in], refs[n_in:n_in + len(h_ops)]
        o0 = n_in + len(h_ops)
        outs, hout = refs[o0:o0 + n_out], refs[o0 + n_out:o0 + n_out + len(h_outs)]
        s0 = o0 + n_out + len(h_outs)
        scr, hsem = refs[s0:s0 + n_scr], refs[s0 + n_scr:]

        def run(phase):
            ip = op = 0
            for i, h in enumerate(hooks):
                ssem, rsem, lsem = hsem[3 * i:3 * i + 3]

                def rcopy(k, src, dst, dev, ssem=ssem, rsem=rsem):
                    return pltpu.make_async_remote_copy(src_ref=src, dst_ref=dst, send_sem=ssem.at[k], recv_sem=rsem.at[k],
                                                        device_id=dev, device_id_type=MESH_ID)

                def lcopy(k, src, dst, lsem=lsem):
                    return pltpu.make_async_copy(src, dst, lsem.at[k])

                getattr(h, phase)(hin[ip:ip + len(h.operands)], hout[op:op + len(h.out_shapes)], rcopy, lcopy)
                ip += len(h.operands)
                op += len(h.out_shapes)

        def at_edge(phase, last):
            if not hooks:
                return
            if grid is None:
                run(phase)
                return
            cond = None
            for ax, n in enumerate(grid):
                here = pl.program_id(ax) == (n - 1 if last else 0)
                cond = here if cond is None else cond & here
            pl.when(cond)(lambda: run(phase))

        at_edge("start", False)
        at_edge("relay", True)
        body(*ins, *outs, *scr)
        at_edge("finish", True)

    if grid is not None:
        kw["grid"] = grid
    return list(pl.pallas_call(
        wrapped, out_shape=list(out_shape) + h_outs, in_specs=list(in_specs) + [_HBM] * len(h_ops),
        out_specs=list(out_specs) + [_HBM] * len(h_outs), scratch_shapes=list(scratch_shapes) + h_sems, **kw,
    )(*operands, *h_ops))


def _exchange(name, hooks):
    return _call(lambda: None, hooks, [], name=name, in_specs=[], out_specs=[], out_shape=[])


def _rows(ref, start, n):
    return ref.at[pl.ds(start, n), :]


def _ag_hook(shard):
    rows = shard.shape[0]

    def block(full, dev_index):
        return _rows(full, dev_index * rows, rows)

    def start(ins, outs, rcopy, lcopy):
        x, y, c = _my_place()
        src, dst = ins[0], block(outs[0], 4 * x + 2 * y + c)
        lcopy(0, src, dst).start()
        rcopy(0, src, dst, (x, y, 1 - c)).start()
        for j, (dev, _) in enumerate(_ici_peers(x, y, c)):
            rcopy(1 + j, src, dst, dev).start()

    def relay(ins, outs, rcopy, lcopy):
        x, y, c = _my_place()
        for j, (dev, chip) in enumerate(_ici_peers(x, y, c)):
            blk = block(outs[0], 2 * chip + c)
            rcopy(1 + j, blk, blk, dev).wait_recv()
            rcopy(4 + j, blk, blk, (x, y, 1 - c)).start()

    def finish(ins, outs, rcopy, lcopy):
        x, y, c = _my_place()
        sib = (x, y, 1 - c)
        full = outs[0]
        peers = _ici_peers(x, y, c)
        blk = block(full, 2 * (2 * x + y) + 1 - c)
        rcopy(0, blk, blk, sib).wait_recv()
        for j, (dev, chip) in enumerate(peers):
            blk = block(full, 2 * chip + 1 - c)
            rcopy(4 + j, blk, blk, sib).wait_recv()
        src, dst = ins[0], block(full, 4 * x + 2 * y + c)
        lcopy(0, src, dst).wait()
        rcopy(0, src, dst, sib).wait_send()
        for j, (dev, chip) in enumerate(peers):
            rcopy(1 + j, src, dst, dev).wait_send()
            blk = block(full, 2 * chip + c)
            rcopy(4 + j, blk, blk, sib).wait_send()

    return _Hook([shard], [jax.ShapeDtypeStruct((NDEV * rows, D), shard.dtype)], 7, 1, start, finish, relay)


def _rs_d2d_hook(gfull):
    rows = gfull.shape[0] // NDEV

    def pairs(g, land):
        x, y, c = _my_place()
        return (x, y, 1 - c), [(k, _rows(g, (2 * k + 1 - c) * rows, rows), land.at[k]) for k in range(NCHIP)]

    def start(ins, outs, rcopy, lcopy):
        sib, cps = pairs(ins[0], outs[0])
        for i, src, dst in cps:
            rcopy(i, src, dst, sib).start()

    def finish(ins, outs, rcopy, lcopy):
        sib, cps = pairs(ins[0], outs[0])
        for i, src, dst in cps:
            rcopy(i, dst, dst, sib).wait_recv()
        for i, src, dst in cps:
            rcopy(i, src, dst, sib).wait_send()

    return _Hook([gfull], [jax.ShapeDtypeStruct((NCHIP, rows, D), gfull.dtype)], NCHIP, 0, start, finish)


def _rs_ici_hook(part):
    def start(ins, outs, rcopy, lcopy):
        x, y, c = _my_place()
        mychip = 2 * x + y
        lcopy(0, ins[0].at[mychip], outs[0].at[mychip]).start()
        for j, (dev, chip) in enumerate(_ici_peers(x, y, c)):
            rcopy(j, ins[0].at[chip], outs[0].at[mychip], dev).start()

    def finish(ins, outs, rcopy, lcopy):
        x, y, c = _my_place()
        mychip = 2 * x + y
        peers = _ici_peers(x, y, c)
        for j, (dev, chip) in enumerate(peers):
            rcopy(j, outs[0].at[chip], outs[0].at[chip], dev).wait_recv()
        for j, (dev, chip) in enumerate(peers):
            rcopy(j, ins[0].at[chip], outs[0].at[mychip], dev).wait_send()
        lcopy(0, ins[0].at[mychip], outs[0].at[mychip]).wait()

    return _Hook([part], [jax.ShapeDtypeStruct(part.shape, part.dtype)], 3, 1, start, finish)


def _small_hook(arrays):
    n = len(arrays)

    def peers():
        x, y, c = _my_place()
        out = []
        for dx in range(2):
            for dy in range(2):
                for dc in range(2):
                    if dx + dy + dc:
                        px, py, pc = (1 - x if dx else x), (1 - y if dy else y), (1 - c if dc else c)
                        out.append(((px, py, pc), 4 * px + 2 * py + pc))
        return 4 * x + 2 * y + c, out

    def start(ins, outs, rcopy, lcopy):
        me, ps = peers()
        for t in range(n):
            lcopy(t, ins[t], outs[t].at[me]).start()
            for i, (dev, _) in enumerate(ps):
                rcopy(n * i + t, ins[t], outs[t].at[me], dev).start()

    def finish(ins, outs, rcopy, lcopy):
        me, ps = peers()
        for t in range(n):
            for i, (dev, peer) in enumerate(ps):
                rcopy(n * i + t, outs[t].at[peer], outs[t].at[peer], dev).wait_recv()
            for i, (dev, _) in enumerate(ps):
                rcopy(n * i + t, ins[t], outs[t].at[me], dev).wait_send()
            lcopy(t, ins[t], outs[t].at[me]).wait()

    return _Hook(arrays, [jax.ShapeDtypeStruct((NDEV,) + a.shape, a.dtype) for a in arrays], 7 * n, n, start, finish)


def _wblock(w):
    return _resident(w.shape, lambda *_: (0, 0))


def _ffn_fwd(name, x, nrm, wg, wu, wd, hooks=(), tm=512):
    S = x.shape[0]

    def body(x_ref, n_ref, wg_ref, wu_ref, wd_ref, y_ref, g_ref, u_ref, a_ref, h_ref, acc_ref):
        xv = x_ref[...]
        _, _, h = _rms_fwd(xv, n_ref[...])
        h = h.astype(CDT)
        h_ref[...] = h
        for ci in range(F // FT):
            sl = slice(ci * FT, (ci + 1) * FT)
            g = _mm_nt(h, wg_ref[sl, :])
            u = _mm_nt(h, wu_ref[sl, :])
            g_ref[:, sl] = g.astype(CDT)
            u_ref[:, sl] = u.astype(CDT)
            a = (g * jax.nn.sigmoid(g) * u).astype(CDT)
            a_ref[:, sl] = a
            o = _mm(a, wd_ref[sl, :])
            if ci == 0:
                acc_ref[...] = o
            else:
                acc_ref[...] += o
        y_ref[...] = xv + 0.5 * acc_ref[...]

    tok = pl.BlockSpec((tm, D), lambda i: (i, 0))
    hid = pl.BlockSpec((tm, F), lambda i: (i, 0))
    hidden = jax.ShapeDtypeStruct((S, F), CDT)
    return _call(
        body, hooks, [x, nrm, wg, wu, wd], name=name, grid=(S // tm,),
        out_shape=[jax.ShapeDtypeStruct((S, D), f32), hidden, hidden, hidden, jax.ShapeDtypeStruct((S, D), CDT)],
        in_specs=[tok, _resident((1, D), lambda i: (0, 0)), _wblock(wg), _wblock(wu), _wblock(wd)],
        out_specs=[tok, hid, hid, hid, tok],
        scratch_shapes=[pltpu.VMEM((tm, D), f32)],
        compiler_params=_params("arbitrary"),
    )


def _proj_fwd(x1, nrm, wfull, b3, hooks=(), tm=512):
    S = x1.shape[0]

    def body(x_ref, n_ref, w_ref, b_ref, p_ref, h_ref):
        _, _, h = _rms_fwd(x_ref[...], n_ref[...])
        h = h.astype(CDT)
        h_ref[...] = h
        for p in range(8):
            seg = SEG_OF_SLOT[p]
            p_ref[p] = (_mm(h, w_ref[seg * D:(seg + 1) * D, :]) + b_ref[seg]).astype(CDT)

    return _call(
        body, hooks, [x1, nrm, wfull, b3], name="proj_fwd", grid=(S // tm,),
        out_shape=[jax.ShapeDtypeStruct((8, S, D), CDT), jax.ShapeDtypeStruct((S, D), CDT)],
        in_specs=[pl.BlockSpec((tm, D), lambda i: (i, 0)), _resident((1, D), lambda i: (0, 0)),
                  _resident((8 * D, D), lambda i: (0, 0)), _resident((8, 1, D), lambda i: (0, 0, 0))],
        out_specs=[pl.BlockSpec((8, tm, D), lambda i: (0, i, 0)), pl.BlockSpec((tm, D), lambda i: (i, 0))],
        compiler_params=_params("arbitrary"),
    )


def _sgu_norm(va, gn, bn):
    mu = jnp.mean(va, axis=-1, keepdims=True)
    xc = va - mu
    rstd = lax.rsqrt(jnp.mean(xc * xc, axis=-1, keepdims=True) + EPS)
    vhat = xc * rstd
    return rstd, vhat, vhat * gn + bn


def _sgu_fwd(proj, gn, bn, ws, bsc, tm=512):
    S = proj.shape[1]
    GW = D // G

    def body(p_ref, gn_ref, bn_ref, ws_ref, bs_ref, a_ref):
        ua = _gelu(p_ref[0].astype(f32))
        va = _gelu(p_ref[1].astype(f32))
        _, _, vn = _sgu_norm(va, gn_ref[...], bn_ref[...])
        vn = vn.astype(CDT)
        for ch in range(tm // C):
            rs = slice(ch * C, (ch + 1) * C)
            for gi in range(G):
                cs = slice(gi * GW, (gi + 1) * GW)
                s = _mm(ws_ref[gi], vn[rs, cs]) + bs_ref[gi]
                a_ref[rs, cs] = (ua[rs, cs] * s).astype(CDT)

    return pl.pallas_call(
        body, name="sgu_fwd", grid=(S // tm,),
        out_shape=jax.ShapeDtypeStruct((S, D), CDT),
        in_specs=[pl.BlockSpec((2, tm, D), lambda i: (0, i, 0)), _resident((1, D), lambda i: (0, 0)),
                  _resident((1, D), lambda i: (0, 0)), _resident((G, C, C), lambda i: (0, 0, 0)),
                  _resident((G, C, 1), lambda i: (0, 0, 0))],
        out_specs=pl.BlockSpec((tm, D), lambda i: (i, 0)),
        compiler_params=_params("arbitrary"),
    )(proj, gn, bn, ws, bsc)


def _decay_tables(dl_ref):
    lg = jax.nn.log_sigmoid(dl_ref[0:2, :])
    lgf, lgb = lg[0:1, :], lg[1:2, :]
    assert RC <= DK
    ri = lax.broadcasted_iota(jnp.int32, (RC, RC), 0)
    ci = lax.broadcasted_iota(jnp.int32, (RC, RC), 1)
    d = (ri - ci).astype(f32)
    lower = d >= 0
    dmat = jnp.where(lower, jnp.exp(d * lgf[:, :RC]), jnp.exp(-d * lgb[:, :RC]))
    dmat_t = jnp.where(d <= 0, jnp.exp(-d * lgf[:, :RC]), jnp.exp(d * lgb[:, :RC]))
    pos = lax.broadcasted_iota(jnp.int32, (RC, DK), 0).astype(f32)
    t = dict(
        lgf=lgf, lgb=lgb, d=d, lower=lower, dmat=dmat, dmat_t=dmat_t, pos=pos,
        fq=jnp.exp((pos + 1.0) * lgf), fk=jnp.exp((RC - 1.0 - pos) * lgf),
        bq=jnp.exp((RC - pos) * lgb), bk=jnp.exp(pos * lgb),
        lamf=jnp.exp(float(RC) * lgf), lamb=jnp.exp(float(RC) * lgb),
    )
    return t


def _rotate(t, co, si):
    t1, t2 = t[:, :DK // 2], t[:, DK // 2:]
    return jnp.concatenate([t1 * co - t2 * si, t2 * co + t1 * si], axis=-1)


def _unrotate(t, co, si):
    t1, t2 = t[:, :DK // 2], t[:, DK // 2:]
    return jnp.concatenate([t1 * co + t2 * si, t2 * co - t1 * si], axis=-1)


K_SCALE = DK ** -0.5
ROT_ROWS = 512


def _rotate_into(q_ref, k_ref, cos_ref, sin_ref, qs_ref, ks_ref, S):
    for rt in range(S // ROT_ROWS):
        rs = slice(rt * ROT_ROWS, (rt + 1) * ROT_ROWS)
        co, si = cos_ref[rs, :], sin_ref[rs, :]
        qs_ref[rs, :] = _rotate(q_ref[rs, :].astype(f32), co, si).astype(CDT)
        ks_ref[rs, :] = (_rotate(k_ref[rs, :].astype(f32), co, si) * K_SCALE).astype(CDT)


def _ret_fwd(proj, cos, sin, dl, hooks=()):
    S = proj.shape[1]
    NC = S // RC

    def body(q_ref, k_ref, v_ref, g_ref, cos_ref, sin_ref, dl_ref, R_ref, r_ref, sfs_ref, sbs_ref,
             qs_ref, ks_ref, rb_ref, sf_ref, sb_ref):
        t = _decay_tables(dl_ref)
        _rotate_into(q_ref, k_ref, cos_ref, sin_ref, qs_ref, ks_ref, S)

        def chunk(n):
            rows = pl.ds(pl.multiple_of(n * RC, RC), RC)
            return rows, qs_ref[rows, :], ks_ref[rows, :], v_ref[rows, :]

        sf_ref[...] = jnp.zeros_like(sf_ref)
        sb_ref[...] = jnp.zeros_like(sb_ref)

        def step(i, carry):
            rows, qn, kn, vn = chunk(i)
            sc = _mm_nt(qn, kn) * t["dmat"]
            out = _mm(sc.astype(CDT), vn)
            sf = sf_ref[...]
            sfb = sf.astype(CDT)
            sfs_ref[i] = sfb
            R_ref[rows, :] = out + _mm((qn.astype(f32) * t["fq"]).astype(CDT), sfb)
            sf_ref[...] = sf * t["lamf"] + _mm_tn((kn.astype(f32) * t["fk"]).astype(CDT), vn)
            m = NC - 1 - i
            rows, qn, kn, vn = chunk(m)
            sb = sb_ref[...]
            sbb = sb.astype(CDT)
            sbs_ref[m] = sbb
            rb_ref[rows, :] = _mm((qn.astype(f32) * t["bq"]).astype(CDT), sbb)
            sb_ref[...] = sb * t["lamb"] + _mm_tn((kn.astype(f32) * t["bk"]).astype(CDT), vn)
            return carry

        lax.fori_loop(0, NC, step, 0)
        for rt in range(S // ROT_ROWS):
            rs = slice(rt * ROT_ROWS, (rt + 1) * ROT_ROWS)
            R = R_ref[rs, :] + rb_ref[rs, :]
            R_ref[rs, :] = R
            rn = R * lax.rsqrt(jnp.mean(R * R, axis=-1, keepdims=True) + EPS)
            g = g_ref[rs, :].astype(f32)
            r_ref[rs, :] = (rn * g * jax.nn.sigmoid(g)).astype(CDT)

    def seg(slot):
        return pl.BlockSpec((None, S, DK), lambda h: (slot, 0, h))

    states = jax.ShapeDtypeStruct((H, NC, DK, DK), CDT)
    state_blk = pl.BlockSpec((None, NC, DK, DK), lambda h: (h, 0, 0, 0))
    return _call(
        body, hooks, [proj, proj, proj, proj, cos, sin, dl], name="ret_fwd", grid=(H,),
        out_shape=[jax.ShapeDtypeStruct((S, H * DK), f32), jax.ShapeDtypeStruct((S, H * DK), CDT), states, states],
        in_specs=[seg(SLOT_Q), seg(SLOT_K), seg(SLOT_VR), seg(SLOT_GR),
                  _resident((S, DK // 2), lambda h: (0, 0)), _resident((S, DK // 2), lambda h: (0, 0)),
                  pl.BlockSpec((None, 8, DK), lambda h: (h, 0, 0))],
        out_specs=[pl.BlockSpec((S, DK), lambda h: (0, h)), pl.BlockSpec((S, DK), lambda h: (0, h)), state_blk, state_blk],
        scratch_shapes=[pltpu.VMEM((S, DK), CDT), pltpu.VMEM((S, DK), CDT), pltpu.VMEM((S, DK), f32),
                        pltpu.VMEM((DK, DK), f32), pltpu.VMEM((DK, DK), f32)],
        compiler_params=_params("arbitrary"),
    )


def _merge_fwd(a, r, proj, x1, wa, wb, wo, tm=512):
    S = x1.shape[0]

    def body(a_ref, r_ref, gt_ref, x_ref, wa_ref, wb_ref, wo_ref, x2_ref, ya_ref, yb_ref):
        ya = _mm(a_ref[...], wa_ref[...])
        yb = _mm(r_ref[...], wb_ref[...])
        ya_ref[...] = ya.astype(CDT)
        yb_ref[...] = yb.astype(CDT)
        mix = jax.nn.sigmoid(gt_ref[0].astype(f32)) * ya + jax.nn.sigmoid(gt_ref[1].astype(f32)) * yb
        x2_ref[...] = x_ref[...] + _mm(mix.astype(CDT), wo_ref[...])

    tok = pl.BlockSpec((tm, D), lambda i: (i, 0))
    return pl.pallas_call(
        body, name="merge_fwd", grid=(S // tm,),
        out_shape=[jax.ShapeDtypeStruct((S, D), f32), jax.ShapeDtypeStruct((S, D), CDT), jax.ShapeDtypeStruct((S, D), CDT)],
        in_specs=[tok, tok, pl.BlockSpec((2, tm, D), lambda i: (SLOT_GA // 2, i, 0)), tok,
                  _wblock(wa), _wblock(wb), _wblock(wo)],
        out_specs=[tok, tok, tok],
        compiler_params=_params("arbitrary"),
    )(a, r, proj, x1, wa, wb, wo)


def _loss_head(x3, fn, target, tm=512):
    S = x3.shape[0]

    def body(x_ref, n_ref, t_ref, dx_ref, dxh_ref, dn_ref, l_ref):
        n = n_ref[...]
        r, xh, y = _rms_fwd(x_ref[...], n)
        e = y - t_ref[...]
        dy = e * (1.0 / D)
        dx, dn = _rms_bwd(dy, r, xh, n)
        dx_ref[...] = dx
        dxh_ref[...] = (0.5 * dx).astype(CDT)
        part = 0.5 * jnp.sum(jnp.sum(e * e, axis=-1, keepdims=True), axis=0, keepdims=True) * (1.0 / D)

        @pl.when(pl.program_id(0) == 0)
        def _():
            dn_ref[...] = jnp.zeros_like(dn_ref)
            l_ref[...] = jnp.zeros_like(l_ref)

        dn_ref[...] += dn
        l_ref[...] += jnp.broadcast_to(part, l_ref.shape)

    tok = pl.BlockSpec((tm, D), lambda i: (i, 0))
    return pl.pallas_call(
        body, name="loss_head", grid=(S // tm,),
        out_shape=[jax.ShapeDtypeStruct((S, D), f32), jax.ShapeDtypeStruct((S, D), CDT), jax.ShapeDtypeStruct((1, D), f32),
                   jax.ShapeDtypeStruct((8, 128), f32)],
        in_specs=[tok, _resident((1, D), lambda i: (0, 0)), tok],
        out_specs=[tok, tok, pl.BlockSpec((1, D), lambda i: (0, 0)), pl.BlockSpec((8, 128), lambda i: (0, 0))],
        compiler_params=_params("arbitrary"),
    )(x3, fn, target)


def _ffn_bwd_hidden(name, dyh, g, u, wd, hooks=(), tm=512):
    S = dyh.shape[0]

    def body(dyh_ref, g_ref, u_ref, wd_ref, dg_ref, du_ref):
        dyh = dyh_ref[...]
        for ci in range(F // FT):
            sl = slice(ci * FT, (ci + 1) * FT)
            da = _mm_nt(dyh, wd_ref[sl, :])
            gv = g_ref[:, sl].astype(f32)
            uv = u_ref[:, sl].astype(f32)
            s = jax.nn.sigmoid(gv)
            du_ref[:, sl] = (da * (gv * s)).astype(CDT)
            dg_ref[:, sl] = (da * uv * (s * (1.0 + gv * (1.0 - s)))).astype(CDT)

    hid = pl.BlockSpec((tm, F), lambda i: (i, 0))
    hidden = jax.ShapeDtypeStruct((S, F), CDT)
    return _call(
        body, hooks, [dyh, g, u, wd], name=name, grid=(S // tm,), out_shape=[hidden, hidden],
        in_specs=[pl.BlockSpec((tm, D), lambda i: (i, 0)), hid, hid, _wblock(wd)], out_specs=[hid, hid],
        compiler_params=_params("arbitrary"),
    )


def _ffn_bwd_in(name, dy, x, dg, du, nrm, wg, wu, hooks=(), part=None, prev=None, tm=512):
    S = x.shape[0]
    t0, nt = part or (0, S // tm)

    def body(dy_ref, x_ref, dg_ref, du_ref, n_ref, wg_ref, wu_ref, *rest):
        dx_ref, dn_ref, acc_ref = rest[-3:]
        n = n_ref[...]
        r, xh, _ = _rms_fwd(x_ref[...], n)
        for ci in range(F // FT):
            sl = slice(ci * FT, (ci + 1) * FT)
            dh = _mm(dg_ref[:, sl], wg_ref[sl, :]) + _mm(du_ref[:, sl], wu_ref[sl, :])
            if ci == 0:
                acc_ref[...] = dh
            else:
                acc_ref[...] += dh
        dx, dn = _rms_bwd(acc_ref[...], r, xh, n)
        dx_ref[...] = dy_ref[...] + dx

        @pl.when(pl.program_id(0) == 0)
        def _():
            dn_ref[...] = jnp.zeros_like(dn_ref) if prev is None else rest[1][...]

        dn_ref[...] += dn

    tok = pl.BlockSpec((tm, D), lambda i: (t0 + i, 0))
    hid = pl.BlockSpec((tm, F), lambda i: (t0 + i, 0))
    row = pl.BlockSpec((1, D), lambda i: (0, 0))
    in_specs = [tok, tok, hid, hid, _resident((1, D), lambda i: (0, 0)), _wblock(wg), _wblock(wu)]
    operands = [dy, x, dg, du, nrm, wg, wu]
    aliases = {}
    if prev is not None:
        in_specs += [_HBM, row]
        operands += list(prev)
        aliases = {7: 0}
    return _call(
        body, hooks, operands, name=name, grid=(nt,),
        out_shape=[jax.ShapeDtypeStruct((S, D), f32), jax.ShapeDtypeStruct((1, D), f32)],
        in_specs=in_specs, out_specs=[tok, row],
        scratch_shapes=[pltpu.VMEM((tm, D), f32)],
        input_output_aliases=aliases,
        compiler_params=_params("arbitrary"),
    )


TN_ROWS = 512


def _tn(name, xs, ys, block_of, hooks=()):
    S, M = xs.shape
    B = ys.shape[0]
    tr = TN_ROWS if M % TN_ROWS == 0 else M // 2
    assert M % tr == 0 and tr % 128 == 0
    nt = M // tr

    def body(x_ref, y_ref, o_ref):
        o_ref[...] = _mm_tn(x_ref[...], y_ref[...]).astype(CDT)

    return _call(
        body, hooks, [xs, ys], name=name, grid=(B, nt),
        out_shape=[jax.ShapeDtypeStruct((B * M, D), CDT)],
        in_specs=[pl.BlockSpec((S, tr), lambda b, i: (0, i)), pl.BlockSpec((None, S, D), lambda b, i: (b, 0, 0))],
        out_specs=[pl.BlockSpec((tr, D), lambda b, i: (block_of(b) * nt + i, 0))],
        compiler_params=_params("arbitrary", "arbitrary"),
    )


def _wgrad(name, xs, y, hooks=()):
    return _tn(name, xs, y[None], lambda b: 0, hooks)


def _merge_bwd_act(dx2, ya, yb, proj, wa, wb, wo, hooks=(), tm=512):
    S = dx2.shape[0]

    def body(dx_ref, ya_ref, yb_ref, gt_ref, wa_ref, wb_ref, wo_ref,
             dp_ref, da_ref, dr_ref, mix_ref, dxb_ref, dya_ref, dyb_ref):
        dxb = dx_ref[...].astype(CDT)
        dxb_ref[...] = dxb
        dmix = _mm_nt(dxb, wo_ref[...])
        ya = ya_ref[...].astype(f32)
        yb = yb_ref[...].astype(f32)
        sa = jax.nn.sigmoid(gt_ref[0].astype(f32))
        sb = jax.nn.sigmoid(gt_ref[1].astype(f32))
        mix_ref[...] = (sa * ya + sb * yb).astype(CDT)
        dya = (dmix * sa).astype(CDT)
        dyb = (dmix * sb).astype(CDT)
        dya_ref[...] = dya
        dyb_ref[...] = dyb
        dp_ref[0] = (dmix * ya * sa * (1.0 - sa)).astype(CDT)
        dp_ref[1] = (dmix * yb * sb * (1.0 - sb)).astype(CDT)
        da_ref[...] = _mm_nt(dya, wa_ref[...]).astype(CDT)
        dr_ref[...] = _mm_nt(dyb, wb_ref[...]).astype(CDT)

    tok = pl.BlockSpec((tm, D), lambda i: (i, 0))
    gates = pl.BlockSpec((2, tm, D), lambda i: (SLOT_GA // 2, i, 0))
    act = jax.ShapeDtypeStruct((S, D), CDT)
    return _call(
        body, hooks, [dx2, ya, yb, proj, wa, wb, wo], name="merge_bwd_act", grid=(S // tm,),
        out_shape=[jax.ShapeDtypeStruct((8, S, D), CDT), act, act, act, act, act, act],
        in_specs=[tok, tok, tok, gates, _wblock(wa), _wblock(wb), _wblock(wo)],
        out_specs=[gates, tok, tok, tok, tok, tok, tok],
        compiler_params=_params("arbitrary"),
    )


def _sgu_bwd(da, proj, dproj, gn, bn, ws, wst, bsc, hooks=(), tm=512):
    S = proj.shape[1]
    GW = D // G

    def body(da_ref, p_ref, dpin_ref, gn_ref, bn_ref, ws_ref, wst_ref, bs_ref,
             dp_ref, dws_ref, dbs_ref, dgn_ref, dbn_ref, ds_ref, dvn_ref):
        @pl.when(pl.program_id(0) == 0)
        def _():
            dws_ref[...] = jnp.zeros_like(dws_ref)
            dbs_ref[...] = jnp.zeros_like(dbs_ref)
            dgn_ref[...] = jnp.zeros_like(dgn_ref)
            dbn_ref[...] = jnp.zeros_like(dbn_ref)

        pu = p_ref[0].astype(f32)
        pv = p_ref[1].astype(f32)
        ua = _gelu(pu)
        va = _gelu(pv)
        gn = gn_ref[...]
        rstd, vhat, vn = _sgu_norm(va, gn, bn_ref[...])
        vnb = vn.astype(CDT)
        dav = da_ref[...].astype(f32)
        dsb = (dav * ua).astype(CDT)
        ones = jnp.ones((8, GW), CDT)
        for ch in range(tm // C):
            rs = slice(ch * C, (ch + 1) * C)
            for gi in range(G):
                cs = slice(gi * GW, (gi + 1) * GW)
                s = _mm(ws_ref[gi], vnb[rs, cs]) + bs_ref[gi]
                ds_ref[rs, cs] = s
                dsg = dsb[rs, cs]
                dws_ref[gi] += _mm_nt(dsg, vnb[rs, cs])
                dbs_ref[gi] += _mm_nt(ones, dsg)
                dvn_ref[rs, cs] = _mm(wst_ref[gi], dsg)
        dp_ref[0] = (dav * ds_ref[...] * _gelu_grad(pu)).astype(CDT)
        dvn = dvn_ref[...]
        dgn_ref[...] += jnp.sum(dvn * vhat, axis=0, keepdims=True)
        dbn_ref[...] += jnp.sum(dvn, axis=0, keepdims=True)
        dvh = dvn * gn
        dva = rstd * (dvh - jnp.mean(dvh, axis=-1, keepdims=True) - vhat * jnp.mean(dvh * vhat, axis=-1, keepdims=True))
        dp_ref[1] = (dva * _gelu_grad(pv)).astype(CDT)

    uv = pl.BlockSpec((2, tm, D), lambda i: (0, i, 0))
    row = _resident((1, D), lambda i: (0, 0))
    return _call(
        body, hooks, [da, proj, dproj, gn, bn, ws, wst, bsc], name="sgu_bwd", grid=(S // tm,),
        out_shape=[jax.ShapeDtypeStruct(dproj.shape, CDT), jax.ShapeDtypeStruct((G, C, C), f32),
                   jax.ShapeDtypeStruct((G, 8, C), f32), jax.ShapeDtypeStruct((1, D), f32), jax.ShapeDtypeStruct((1, D), f32)],
        in_specs=[pl.BlockSpec((tm, D), lambda i: (i, 0)), uv, _HBM, row, row,
                  _resident((G, C, C), lambda i: (0, 0, 0)), _resident((G, C, C), lambda i: (0, 0, 0)),
                  _resident((G, C, 1), lambda i: (0, 0, 0))],
        out_specs=[uv, pl.BlockSpec((G, C, C), lambda i: (0, 0, 0)), pl.BlockSpec((G, 8, C), lambda i: (0, 0, 0)),
                   pl.BlockSpec((1, D), lambda i: (0, 0)), pl.BlockSpec((1, D), lambda i: (0, 0))],
        scratch_shapes=[pltpu.VMEM((tm, D), f32), pltpu.VMEM((tm, D), f32)],
        input_output_aliases={2: 0},
        compiler_params=_params("arbitrary"),
    )


def _ret_bwd(dr, R, sfs, sbs, proj, dproj, cos, sin, dl, hooks=()):
    S = proj.shape[1]
    NC = S // RC
    assert NC % 2 == 0

    def body(dr_ref, R_ref, sf_ref, sb_ref, q_ref, k_ref, v_ref, g_ref, dpin_ref, cos_ref, sin_ref, dl_ref,
             dp_ref, dd_ref, qs_ref, ks_ref, dR_ref, dk_ref, dv_ref, gb_ref, gf_ref, acc_ref):
        t = _decay_tables(dl_ref)
        _rotate_into(q_ref, k_ref, cos_ref, sin_ref, qs_ref, ks_ref, S)
        for rt in range(S // ROT_ROWS):
            rs = slice(rt * ROT_ROWS, (rt + 1) * ROT_ROWS)
            Rv = R_ref[rs, :]
            rstd = lax.rsqrt(jnp.mean(Rv * Rv, axis=-1, keepdims=True) + EPS)
            rn = Rv * rstd
            gv = g_ref[rs, :].astype(f32)
            s = jax.nn.sigmoid(gv)
            drv = dr_ref[rs, :].astype(f32)
            dp_ref[3, rs, :] = (drv * rn * (s * (1.0 + gv * (1.0 - s)))).astype(CDT)
            drn = drv * gv * s
            dR_ref[rs, :] = (rstd * (drn - rn * jnp.mean(drn * rn, axis=-1, keepdims=True))).astype(CDT)

        def chunk(n):
            rows = pl.ds(pl.multiple_of(n * RC, RC), RC)
            return rows, qs_ref[rows, :], ks_ref[rows, :], v_ref[rows, :], dR_ref[rows, :]

        def emit_kv(rows, dk, dv, final):
            if not final:
                dk_ref[rows, :] = dk
                dv_ref[rows, :] = dv
            else:
                co, si = cos_ref[rows, :], sin_ref[rows, :]
                dp_ref[1, rows, :] = (_unrotate(dk_ref[rows, :] + dk, co, si) * K_SCALE).astype(CDT)
                dp_ref[2, rows, :] = (dv_ref[rows, :] + dv).astype(CDT)

        gb_ref[...] = jnp.zeros_like(gb_ref)
        gf_ref[...] = jnp.zeros_like(gf_ref)
        acc_ref[...] = jnp.zeros_like(acc_ref)
        dpos = jnp.abs(t["d"])

        def ascend(n, final):
            rows, qn, kn, vn, dRn = chunk(n)
            qf, kf = qn.astype(f32), kn.astype(f32)
            sc = _mm_nt(qn, kn)
            dA = _mm_nt(dRn, vn)
            w = sc * dA * t["dmat"] * dpos
            lgf_part = jnp.sum(jnp.where(t["lower"], w, 0.0), axis=0, keepdims=True)
            lgb_part = jnp.sum(jnp.where(t["lower"], 0.0, w), axis=0, keepdims=True)
            dsc = (dA * t["dmat"]).astype(CDT)
            dq = _mm(dsc, kn)
            scT = (_mm_nt(kn, qn) * t["dmat_t"]).astype(CDT)
            dscT = (_mm_nt(vn, dRn) * t["dmat_t"]).astype(CDT)
            dk = _mm(dscT, qn)
            dv = _mm(scT, dRn)
            sfb = sf_ref[n]
            sbb = sb_ref[n]
            qdf = qf * t["fq"]
            dqdf = _mm_nt(dRn, sfb)
            dq += dqdf * t["fq"]
            lgf_row = jnp.sum(qdf * dqdf * (t["pos"] + 1.0), axis=0, keepdims=True)
            qdb = qf * t["bq"]
            dqdb = _mm_nt(dRn, sbb)
            dq += dqdb * t["bq"]
            lgb_row = jnp.sum(qdb * dqdb * (RC - t["pos"]), axis=0, keepdims=True)
            gb = gb_ref[...]
            gbb = gb.astype(CDT)
            kdb = kf * t["bk"]
            dkdb = _mm_nt(vn, gbb)
            dk += dkdb * t["bk"]
            dv += _mm(kdb.astype(CDT), gbb)
            lgb_row += jnp.sum(kdb * dkdb * t["pos"], axis=0, keepdims=True)
            lgb_row += float(RC) * t["lamb"] * jnp.sum(gb * sbb.astype(f32), axis=0, keepdims=True)
            co, si = cos_ref[rows, :], sin_ref[rows, :]
            dp_ref[0, rows, :] = _unrotate(dq, co, si).astype(CDT)
            emit_kv(rows, dk, dv, final)
            acc_ref[0:1, :] += lgf_row + lgf_part
            acc_ref[1:2, :] += lgb_row + lgb_part
            gb_ref[...] = gb * t["lamb"] + _mm_tn(qdb.astype(CDT), dRn)

        def descend(n, final):
            rows, qn, kn, vn, dRn = chunk(n)
            gf = gf_ref[...]
            gfb = gf.astype(CDT)
            kdf = kn.astype(f32) * t["fk"]
            dkdf = _mm_nt(vn, gfb)
            lgf_row = jnp.sum(kdf * dkdf * (RC - 1.0 - t["pos"]), axis=0, keepdims=True)
            lgf_row += float(RC) * t["lamf"] * jnp.sum(gf * sf_ref[n].astype(f32), axis=0, keepdims=True)
            acc_ref[0:1, :] += lgf_row
            emit_kv(rows, dkdf * t["fk"], _mm(kdf.astype(CDT), gfb), final)
            gf_ref[...] = gf * t["lamf"] + _mm_tn((qn.astype(f32) * t["fq"]).astype(CDT), dRn)

        def sweep(final):
            def step(i, carry):
                ascend(i, final)
                descend(NC - 1 - i, final)
                return carry
            return step

        lax.fori_loop(0, NC // 2, sweep(False), 0)
        lax.fori_loop(NC // 2, NC, sweep(True), 0)
        dlg = jnp.sum(acc_ref[...], axis=1, keepdims=True)
        dlogit = dlg * jax.nn.sigmoid(-dl_ref[:, 0:1])
        lane = lax.broadcasted_iota(jnp.int32, (8, 128), 1)
        dd_ref[...] = jnp.where(lane == pl.program_id(0), jnp.broadcast_to(dlogit, (8, 128)), 0.0)

    def seg(slot):
        return pl.BlockSpec((None, S, DK), lambda h: (slot, 0, h), pipeline_mode=pl.Buffered(1))

    head = pl.BlockSpec((S, DK), lambda h: (0, h), pipeline_mode=pl.Buffered(1))
    states = pl.BlockSpec((None, NC, DK, DK), lambda h: (h, 0, 0, 0), pipeline_mode=pl.Buffered(1))
    return _call(
        body, hooks, [dr, R, sfs, sbs, proj, proj, proj, proj, dproj, cos, sin, dl], name="ret_bwd", grid=(H,),
        out_shape=[jax.ShapeDtypeStruct(dproj.shape, CDT), jax.ShapeDtypeStruct((H, 8, 128), f32)],
        in_specs=[head, head, states, states, seg(SLOT_Q), seg(SLOT_K), seg(SLOT_VR), seg(SLOT_GR), _HBM,
                  _resident((S, DK // 2), lambda h: (0, 0)), _resident((S, DK // 2), lambda h: (0, 0)),
                  pl.BlockSpec((None, 8, DK), lambda h: (h, 0, 0))],
        out_specs=[pl.BlockSpec((4, S, DK), lambda h: (1, 0, h), pipeline_mode=pl.Buffered(1)),
                   pl.BlockSpec((None, 8, 128), lambda h: (h, 0, 0))],
        scratch_shapes=[pltpu.VMEM((S, DK), CDT), pltpu.VMEM((S, DK), CDT), pltpu.VMEM((S, DK), CDT),
                        pltpu.VMEM((S, DK), f32), pltpu.VMEM((S, DK), f32),
                        pltpu.VMEM((DK, DK), f32), pltpu.VMEM((DK, DK), f32), pltpu.VMEM((8, DK), f32)],
        input_output_aliases={8: 0},
        compiler_params=_params("arbitrary"),
    )


def _proj_bwd_act(dproj, dx2, x1, nrm, wfull, hooks=(), tm=512):
    S = x1.shape[0]

    def body(dp_ref, dx2_ref, x_ref, n_ref, w_ref, dx_ref, dxh_ref, dn_ref, db_ref, acc_ref):
        @pl.when(pl.program_id(0) == 0)
        def _():
            dn_ref[...] = jnp.zeros_like(dn_ref)
            db_ref[...] = jnp.zeros_like(db_ref)

        for p in range(8):
            seg = SEG_OF_SLOT[p]
            dp = dp_ref[p]
            db_ref[seg] += jnp.sum(dp.astype(f32), axis=0, keepdims=True)
            dh = _mm_nt(dp, w_ref[seg * D:(seg + 1) * D, :])
            if p == 0:
                acc_ref[...] = dh
            else:
                acc_ref[...] += dh
        n = n_ref[...]
        r, xh, _ = _rms_fwd(x_ref[...], n)
        dx, dn = _rms_bwd(acc_ref[...], r, xh, n)
        dx = dx2_ref[...] + dx
        dx_ref[...] = dx
        dxh_ref[...] = (0.5 * dx).astype(CDT)
        dn_ref[...] += dn

    tok = pl.BlockSpec((tm, D), lambda i: (i, 0))
    return _call(
        body, hooks, [dproj, dx2, x1, nrm, wfull], name="proj_bwd_act", grid=(S // tm,),
        out_shape=[jax.ShapeDtypeStruct((S, D), f32), jax.ShapeDtypeStruct((S, D), CDT), jax.ShapeDtypeStruct((1, D), f32),
                   jax.ShapeDtypeStruct((8, 1, D), f32)],
        in_specs=[pl.BlockSpec((8, tm, D), lambda i: (0, i, 0)), tok, tok, _resident((1, D), lambda i: (0, 0)),
                  _resident((8 * D, D), lambda i: (0, 0))],
        out_specs=[tok, tok, pl.BlockSpec((1, D), lambda i: (0, 0)), pl.BlockSpec((8, 1, D), lambda i: (0, 0, 0))],
        scratch_shapes=[pltpu.VMEM((tm, D), f32)],
        compiler_params=_params("arbitrary"),
    )


def _rs_sum(name, gfulls, lands, my_c):
    n = len(gfulls)
    rows = gfulls[0].shape[0] // NDEV
    assert all(g.shape[0] == NDEV * rows for g in gfulls)

    def body(c_ref, *refs):
        for g_ref, l_ref, o_ref in zip(refs[:n], refs[n:2 * n], refs[2 * n:]):
            o_ref[...] = (g_ref[...].astype(f32) + l_ref[...].astype(f32)).astype(CDT)

    slot = pl.BlockSpec((None, rows, D), lambda k, c: (k, 0, 0))
    return pl.pallas_call(
        body, name=name,
        grid_spec=pltpu.PrefetchScalarGridSpec(
            num_scalar_prefetch=1, grid=(NCHIP,),
            in_specs=[pl.BlockSpec((rows, D), lambda k, c: (2 * k + c[0], 0))] * n + [slot] * n,
            out_specs=[slot] * n),
        out_shape=[jax.ShapeDtypeStruct((NCHIP, rows, D), CDT)] * n,
        compiler_params=_params("arbitrary"),
    )(my_c, *gfulls, *lands)


def _adamw_math(g, w, m, v):
    m2 = ADAM_B1 * m + (1.0 - ADAM_B1) * g
    v2 = ADAM_B2 * v + (1.0 - ADAM_B2) * (g * g)
    delta = -ADAM_LR * ((m2 / BC1) / (jnp.sqrt(v2 / BC2) + ADAM_EPS) + ADAM_WD * w)
    return delta, m2, v2


def _adamw_big(name, landed, w, m, v, hooks=()):
    rows = w.shape[0]
    tr = min(rows, 256) if rows % 256 == 0 else rows

    def body(l_ref, w_ref, m_ref, v_ref, g_ref, d_ref, m2_ref, v2_ref):
        g = l_ref[0].astype(f32)
        for k in range(1, NCHIP):
            g = g + l_ref[k].astype(f32)
        g_ref[...] = g
        d_ref[...], m2_ref[...], v2_ref[...] = _adamw_math(g, w_ref[...], m_ref[...], v_ref[...])

    blk = pl.BlockSpec((tr, D), lambda i: (i, 0))
    o = jax.ShapeDtypeStruct((rows, D), f32)
    return _call(
        body, hooks, [landed, w, m, v], name=name, grid=(rows // tr,), out_shape=[o, o, o, o],
        in_specs=[pl.BlockSpec((NCHIP, tr, D), lambda i: (0, i, 0)), blk, blk, blk],
        out_specs=[blk, blk, blk, blk],
        compiler_params=_params("arbitrary"),
    )


ROW_FFN1_NORM, ROW_MIX_NORM, ROW_SGU_G, ROW_SGU_B, ROW_FFN2_NORM, ROW_FINAL_NORM, ROW_B_IN = 0, 1, 2, 3, 4, 5, 8
ROW_WS, ROW_BS, ROW_DECAY = 0, G * C, G * C + G * 8


def _adamw_small(ga, gb, gn1, gl, params):
    def body(ga_ref, gb_ref, gn1_ref, gl_ref, *refs):
        ins, outs = refs[:30], refs[30:]

        def total(ref, r0, n):
            g = ref[0, r0:r0 + n, :]
            for j in range(1, NDEV):
                g = g + ref[j, r0:r0 + n, :]
            return g

        def apply(i, g, rows=slice(None)):
            w, m, v = ins[3 * i][rows, :], ins[3 * i + 1][rows, :], ins[3 * i + 2][rows, :]
            outs[4 * i][rows, :] = g
            outs[4 * i + 1][rows, :], outs[4 * i + 2][rows, :], outs[4 * i + 3][rows, :] = _adamw_math(g, w, m, v)

        outs[40][...] = total(gl_ref, 0, 8)
        apply(0, total(gn1_ref, 0, 1))
        for i, r in enumerate((ROW_FFN1_NORM, ROW_MIX_NORM, ROW_SGU_G, ROW_SGU_B, ROW_FFN2_NORM, ROW_FINAL_NORM)):
            if i:
                apply(i, total(ga_ref, r, 1))
        apply(6, total(ga_ref, ROW_B_IN, 8))
        apply(7, total(gb_ref, ROW_WS, G * C))
        for gi in range(G):
            apply(8, total(gb_ref, ROW_BS + 8 * gi, 1), slice(gi, gi + 1))
        dec = total(gb_ref, ROW_DECAY, 8)
        for hh in range(1, H):
            dec = dec + total(gb_ref, ROW_DECAY + 8 * hh, 8)
        apply(9, dec)

    flat = [a for p in params for a in p]
    out_shape = [jax.ShapeDtypeStruct(p[0].shape, f32) for p in params for _ in range(4)]
    out_shape.append(jax.ShapeDtypeStruct((8, 128), f32))
    vm = pl.BlockSpec(memory_space=pltpu.VMEM)
    return pl.pallas_call(
        body, name="adamw_small", out_shape=out_shape,
        in_specs=[vm] * (4 + len(flat)), out_specs=[vm] * len(out_shape),
        compiler_params=pltpu.CompilerParams(vmem_limit_bytes=VMEM_LIMIT),
    )(ga, gb, gn1, gl, *flat)


def kernel(x, ffn1_norm, ffn1_w_gate, ffn1_w_up, ffn1_w_down, mix_norm, w_in, b_in, sgu_norm_g, sgu_norm_b, sgu_w_s, sgu_b_s, ret_decay_logit, w_branch_a, w_branch_b, w_out, ffn2_norm, ffn2_w_gate, ffn2_w_up, ffn2_w_down, final_norm, loss_target, m_ffn1_norm, m_ffn1_w_gate, m_ffn1_w_up, m_ffn1_w_down, m_mix_norm, m_w_in, m_b_in, m_sgu_norm_g, m_sgu_norm_b, m_sgu_w_s, m_sgu_b_s, m_ret_decay_logit, m_w_branch_a, m_w_branch_b, m_w_out, m_ffn2_norm, m_ffn2_w_gate, m_ffn2_w_up, m_ffn2_w_down, m_final_norm, v_ffn1_norm, v_ffn1_w_gate, v_ffn1_w_up, v_ffn1_w_down, v_mix_norm, v_w_in, v_b_in, v_sgu_norm_g, v_sgu_norm_b, v_sgu_w_s, v_sgu_b_s, v_ret_decay_logit, v_w_branch_a, v_w_branch_b, v_w_out, v_ffn2_norm, v_ffn2_w_gate, v_ffn2_w_up, v_ffn2_w_down, v_final_norm):
    args = dict(locals())
    S = x.shape[1]
    xs = x[0]
    target = loss_target[0]

    def buf_layout(name, a):
        a = a[0]
        return a.T if name in W_TRANSPOSED else a

    sh = {n: buf_layout(n, args[n]).astype(CDT) for n in W_NAMES}
    wf = {}

    b3 = b_in.reshape(8, 1, D)
    ws = sgu_w_s[0].astype(CDT)
    wst = jnp.swapaxes(sgu_w_s[0], 1, 2).astype(CDT)
    bsc = sgu_b_s[0].reshape(G, C, 1)
    dl = jnp.zeros((H, 8, DK), f32).at[:, 0:2, :].set(jnp.broadcast_to(ret_decay_logit[0].T[:, :, None], (H, 2, DK)))
    theta = ROPE_BASE ** (-jnp.arange(0, DK, 2, dtype=f32) / DK)
    ang = jnp.arange(S, dtype=f32)[:, None] * theta[None, :]
    cos, sin = jnp.cos(ang), jnp.sin(ang)
    fnorm = final_norm.reshape(1, D)

    f1 = ("ffn1_w_gate", "ffn1_w_up", "ffn1_w_down")
    f2 = ("ffn2_w_gate", "ffn2_w_up", "ffn2_w_down")
    br = ("w_branch_a", "w_branch_b", "w_out")
    wf[f1[0]], wf[f1[1]], wf[f1[2]] = _exchange("ag_ffn1", [_ag_hook(sh[n]) for n in f1])
    x1, g1, u1, a1, hf1, wf["w_in"] = _ffn_fwd("ffn1_fwd", xs, ffn1_norm, *[wf[n] for n in f1], [_ag_hook(sh["w_in"])])
    proj, h2, wf[br[0]], wf[br[1]], wf[br[2]], wf[f2[0]] = _proj_fwd(
        x1, mix_norm, wf["w_in"], b3, [_ag_hook(sh[n]) for n in br + f2[:1]])
    a = _sgu_fwd(proj, sgu_norm_g, sgu_norm_b, ws, bsc)
    R, r, sfs, sbs, wf[f2[1]], wf[f2[2]] = _ret_fwd(proj, cos, sin, dl, [_ag_hook(sh[n]) for n in f2[1:]])
    x2, ya, yb = _merge_fwd(a, r, proj, x1, *[wf[n] for n in br])
    x3, g2, u2, a2, hf2 = _ffn_fwd("ffn2_fwd", x2, ffn2_norm, *[wf[n] for n in f2])
    dx3, dyh2, d_final, loss_part = _loss_head(x3, fnorm, target)

    my_c = lax.axis_index("c").astype(jnp.int32).reshape(1)
    gw, landed = {}, {}

    def d2d(*names):
        return [_rs_d2d_hook(gw[n]) for n in names]

    def ici(parts):
        return [_rs_ici_hook(p) for p in parts]

    def rs_sum(names, sibs):
        return list(_rs_sum("rs_sum_" + names[0], [gw[n] for n in names], list(sibs), my_c))

    def ffn_bwd(tag, names, dy, dyh, x, g, u, a, h, nrm, eager, on_first, on_hidden):
        wg, wu, wd = names
        gw[wd], *first = _wgrad(tag + "_wd_grad", a, dyh, on_first)
        dg, du, sib, *hidden = _ffn_bwd_hidden(tag + "_bwd_hidden", dyh, g, u, wf[wd], d2d(wd) + on_hidden)
        (part_d,) = rs_sum([wd], [sib])
        gw[wg], *got = _wgrad(tag + "_wg_grad", dg, h, ici([part_d]) if eager else [])
        landed.update(zip([wd], got))
        gw[wu], sib = _wgrad(tag + "_wu_grad", du, h, d2d(wg))
        (part_g,) = rs_sum([wg], [sib])
        args_in = (dy, x, dg, du, nrm, wf[wg], wf[wu])
        if not eager:
            dx, dn, sib = _ffn_bwd_in(tag + "_bwd_in", *args_in, d2d(wu))
            return dx, dn, [part_d, part_g] + rs_sum([wu], [sib]), first, hidden
        half = x.shape[0] // 512 // 2
        dx, dn, landed[wg], sib = _ffn_bwd_in(tag + "_bwd_in_a", *args_in, ici([part_g]) + d2d(wu), part=(0, half))
        dx, dn, landed[wu] = _ffn_bwd_in(tag + "_bwd_in_b", *args_in, ici(rs_sum([wu], [sib])), part=(half, half),
                                         prev=(dx, dn))
        return dx, dn, [], first, hidden

    dx2, d_ffn2n, parts_f2, _, _ = ffn_bwd("ffn2", f2, dx3, dyh2, x2, g2, u2, a2, hf2, ffn2_norm, False, [], [])
    dproj, da, dr, mix, dx2b, dya, dyb = _merge_bwd_act(dx2, ya, yb, proj, *[wf[n] for n in br])
    dproj, d_ws, d_bs, d_gn, d_bn = _sgu_bwd(da, proj, dproj, sgu_norm_g, sgu_norm_b, ws, wst, bsc)
    dproj, d_dec, *got = _ret_bwd(dr, R, sfs, sbs, proj, dproj, cos, sin, dl, ici(parts_f2))
    landed.update(zip((f2[2], f2[0], f2[1]), got))
    (gw["w_in"],) = _tn("win_grad", h2, dproj, _seg_of_slot)
    gw["w_out"], sib_win = _wgrad("wo_grad", mix, dx2b, d2d("w_in"))
    (part_win,) = rs_sum(["w_in"], [sib_win])
    (gw["w_branch_a"],) = _wgrad("wa_grad", a, dya)
    (gw["w_branch_b"],) = _wgrad("wb_grad", r, dyb)
    dx1, dyh1, d_mixn, d_bin, landed["w_in"], *sib = _proj_bwd_act(dproj, dx2, x1, mix_norm, wf["w_in"],
                                                                   ici([part_win]) + d2d(*br))
    part_br = rs_sum(br, sib)
    small_a = jnp.concatenate([jnp.zeros((1, D), f32), d_mixn, d_gn, d_bn, d_ffn2n, d_final, jnp.zeros((2, D), f32),
                               d_bin.reshape(8, D)], axis=0)
    small_b = jnp.concatenate([d_ws.reshape(G * C, C), d_bs.reshape(G * 8, C), d_dec.reshape(H * 8, 128)], axis=0)
    dxs, d_ffn1n, _, (ga, gb, gl), got = ffn_bwd("ffn1", f1, dx1, dyh1, xs, g1, u1, a1, hf1, ffn1_norm, True,
                                                [_small_hook([small_a, small_b, loss_part])], ici(part_br))
    landed.update(zip(br, got))
    (gn1,) = _exchange("ag_ffn1_norm", [_small_hook([d_ffn1n])])

    out = {"grad_x": dxs[None]}

    def native(name, a):
        a = a.T if name in W_TRANSPOSED else a
        return a[None]

    for n in W_NAMES:
        res = _adamw_big("adamw_" + n, landed[n], buf_layout(n, args[n]), buf_layout(n, args["m_" + n]),
                         buf_layout(n, args["v_" + n]))
        for pre, val in zip(("grad_", "delta_", "new_m_", "new_v_"), res):
            out[pre + n] = native(n, val)

    def pad_decay(a):
        return jnp.zeros((8, 128), f32).at[0:2, 0:H].set(a[0])

    small = [
        ("ffn1_norm", lambda a: a, lambda a: a), ("mix_norm", lambda a: a, lambda a: a),
        ("sgu_norm_g", lambda a: a, lambda a: a), ("sgu_norm_b", lambda a: a, lambda a: a),
        ("ffn2_norm", lambda a: a, lambda a: a),
        ("final_norm", lambda a: a.reshape(1, D), lambda a: a.reshape(D)),
        ("b_in", lambda a: a.reshape(8, D), lambda a: a.reshape(1, 8 * D)),
        ("sgu_w_s", lambda a: a.reshape(G * C, C), lambda a: a.reshape(1, G, C, C)),
        ("sgu_b_s", lambda a: a[0], lambda a: a[None]),
        ("ret_decay_logit", pad_decay, lambda a: a[None, 0:2, 0:H]),
    ]
    res = _adamw_small(ga, gb, gn1, gl, [(to(args[n]), to(args["m_" + n]), to(args["v_" + n])) for n, to, _ in small])
    out["loss"] = res[40][0, 0]
    for i, (n, _, back) in enumerate(small):
        for j, pre in enumerate(("grad_", "delta_", "new_m_", "new_v_")):
            out[pre + n] = back(res[4 * i + j])

    weights = ("ffn1_norm", "ffn1_w_gate", "ffn1_w_up", "ffn1_w_down", "mix_norm", "w_in", "b_in", "sgu_norm_g",
               "sgu_norm_b", "sgu_w_s", "sgu_b_s", "ret_decay_logit", "w_branch_a", "w_branch_b", "w_out", "ffn2_norm",
               "ffn2_w_gate", "ffn2_w_up", "ffn2_w_down", "final_norm")
    return (out["loss"], out["grad_x"], *[out["grad_" + n] for n in weights], *[out["delta_" + n] for n in weights],
            *[out["new_m_" + n] for n in weights], *[out["new_v_" + n] for n in weights])
```

```python
import functools
import math

import jax
import jax.numpy as jnp
from jax import lax
from jax.experimental import pallas as pl
from jax.experimental.pallas import tpu as pltpu

f32 = jnp.float32
CDT = jnp.bfloat16

D = 1024
F = 2816
C = 128
RC = 256
H = 4
DK = 256
G = 4
NDEV = 8
NCHIP = 4
EPS = 1e-6
ROPE_BASE = 10000.0
FT = 256
V7X_VMEM_BYTES = 64 * 1024 * 1024
VMEM_LIMIT = V7X_VMEM_BYTES - 8 * 1024 * 1024

ADAM_LR, ADAM_B1, ADAM_B2, ADAM_EPS, ADAM_WD, ADAM_STEP = 0.001, 0.9, 0.999, 1e-08, 0.01, 10
BC1 = 1.0 - ADAM_B1 ** ADAM_STEP
BC2 = 1.0 - ADAM_B2 ** ADAM_STEP

W_ROWS = dict(ffn1_w_gate=352, ffn1_w_up=352, ffn1_w_down=352, w_in=1024, w_branch_a=128, w_branch_b=128, w_out=128,
              ffn2_w_gate=352, ffn2_w_up=352, ffn2_w_down=352)
W_NAMES = tuple(W_ROWS)
W_TRANSPOSED = ("ffn1_w_gate", "ffn1_w_up", "ffn2_w_gate", "ffn2_w_up")

SLOT_U, SLOT_V, SLOT_GA, SLOT_GB, SLOT_Q, SLOT_K, SLOT_VR, SLOT_GR = range(8)


SEG_OF_SLOT = (0, 1, 6, 7, 2, 3, 4, 5)


def _seg_of_slot(p):
    return jnp.where(p < 2, p, jnp.where(p < 4, p + 4, p - 2))


def _mm(a, b):
    return jnp.dot(a, b, preferred_element_type=f32)


def _mm_nt(a, b):
    return lax.dot_general(a, b, (((1,), (1,)), ((), ())), preferred_element_type=f32)


def _mm_tn(a, b):
    return lax.dot_general(a, b, (((0,), (0,)), ((), ())), preferred_element_type=f32)


def _params(*sem):
    return pltpu.CompilerParams(dimension_semantics=sem, vmem_limit_bytes=VMEM_LIMIT)


def _resident(shape, index_map):
    return pl.BlockSpec(shape, index_map, pipeline_mode=pl.Buffered(1))


def _gelu(x):
    return 0.5 * x * (1.0 + lax.erf(x * (1.0 / math.sqrt(2.0))))


def _gelu_grad(x):
    return 0.5 * (1.0 + lax.erf(x * (1.0 / math.sqrt(2.0)))) + x * jnp.exp(-0.5 * x * x) * (1.0 / math.sqrt(2.0 * math.pi))


def _rms_fwd(x, n):
    r = lax.rsqrt(jnp.mean(x * x, axis=-1, keepdims=True) + EPS)
    xh = x * r
    return r, xh, xh * n


def _rms_bwd(dh, r, xh, n):
    dxh = dh * n
    dx = r * (dxh - xh * jnp.mean(dxh * xh, axis=-1, keepdims=True))
    return dx, jnp.sum(dh * xh, axis=0, keepdims=True)


MESH_ID = pl.DeviceIdType.MESH
_HBM = pl.BlockSpec(memory_space=pltpu.HBM)


def _my_place():
    return lax.axis_index("x"), lax.axis_index("y"), lax.axis_index("c")


def _ici_peers(x, y, c):
    return [((1 - x, y, c), 2 * (1 - x) + y), ((x, 1 - y, c), 2 * x + 1 - y), ((1 - x, 1 - y, c), 2 * (1 - x) + 1 - y)]


class _Hook:
    def __init__(self, operands, out_shapes, n_remote, n_local, start, finish, relay=None):
        self.operands, self.out_shapes = list(operands), list(out_shapes)
        self.n_remote, self.n_local, self.start, self.finish = n_remote, n_local, start, finish
        self.relay = relay or (lambda *a: None)


def _call(body, hooks, operands, *, in_specs, out_specs, out_shape, grid=None, scratch_shapes=(), **kw):
    hooks = tuple(hooks)
    n_in, n_out, n_scr = len(in_specs), len(out_shape), len(scratch_shapes)
    h_ops = [a for h in hooks for a in h.operands]
    h_outs = [s for h in hooks for s in h.out_shapes]
    h_sems = [pltpu.SemaphoreType.DMA((n,)) for h in hooks for n in (h.n_remote, h.n_remote, max(h.n_local, 1))]

    def wrapped(*refs):
        ins, hin = refs[:n_in], refs[n_in:n_in + len(h_ops)]
        o0 = n_in + len(h_ops)
        outs, hout = refs[o0:o0 + n_out], refs[o0 + n_out:o0 + n_out + len(h_outs)]
        s0 = o0 + n_out + len(h_outs)
        scr, hsem = refs[s0:s0 + n_scr], refs[s0 + n_scr:]

        def run(phase):
            ip = op = 0
            for i, h in enumerate(hooks):
                ssem, rsem, lsem = hsem[3 * i:3 * i + 3]

                def rcopy(k, src, dst, dev, ssem=ssem, rsem=rsem):
                    return pltpu.make_async_remote_copy(src_ref=src, dst_ref=dst, send_sem=ssem.at[k], recv_sem=rsem.at[k],
                                                        device_id=dev, device_id_type=MESH_ID)

                def lcopy(k, src, dst, lsem=lsem):
                    return pltpu.make_async_copy(src, dst, lsem.at[k])

                getattr(h, phase)(hin[ip:ip + len(h.operands)], hout[op:op + len(h.out_shapes)], rcopy, lcopy)
                ip += len(h.operands)
                op += len(h.out_shapes)

        def at_edge(phase, last):
            if not hooks:
                return
            if grid is None:
                run(phase)
                return
            cond = None
            for ax, n in enumerate(grid):
                here = pl.program_id(ax) == (n - 1 if last else 0)
                cond = here if cond is None else cond & here
            pl.when(cond)(lambda: run(phase))

        at_edge("start", False)
        at_edge("relay", True)
        body(*ins, *outs, *scr)
        at_edge("finish", True)

    if grid is not None:
        kw["grid"] = grid
    return list(pl.pallas_call(
        wrapped, out_shape=list(out_shape) + h_outs, in_specs=list(in_specs) + [_HBM] * len(h_ops),
        out_specs=list(out_specs) + [_HBM] * len(h_outs), scratch_shapes=list(scratch_shapes) + h_sems, **kw,
    )(*operands, *h_ops))


def _exchange(name, hooks):
    return _call(lambda: None, hooks, [], name=name, in_specs=[], out_specs=[], out_shape=[])


def _rows(ref, start, n):
    return ref.at[pl.ds(start, n), :]


def _ag_hook(shard):
    rows = shard.shape[0]

    def block(full, dev_index):
        return _rows(full, dev_index * rows, rows)

    def start(ins, outs, rcopy, lcopy):
        x, y, c = _my_place()
        src, dst = ins[0], block(outs[0], 4 * x + 2 * y + c)
        lcopy(0, src, dst).start()
        rcopy(0, src, dst, (x, y, 1 - c)).start()
        for j, (dev, _) in enumerate(_ici_peers(x, y, c)):
            rcopy(1 + j, src, dst, dev).start()

    def relay(ins, outs, rcopy, lcopy):
        x, y, c = _my_place()
        for j, (dev, chip) in enumerate(_ici_peers(x, y, c)):
            blk = block(outs[0], 2 * chip + c)
            rcopy(1 + j, blk, blk, dev).wait_recv()
            rcopy(4 + j, blk, blk, (x, y, 1 - c)).start()

    def finish(ins, outs, rcopy, lcopy):
        x, y, c = _my_place()
        sib = (x, y, 1 - c)
        full = outs[0]
        peers = _ici_peers(x, y, c)
        blk = block(full, 2 * (2 * x + y) + 1 - c)
        rcopy(0, blk, blk, sib).wait_recv()
        for j, (dev, chip) in enumerate(peers):
            blk = block(full, 2 * chip + 1 - c)
            rcopy(4 + j, blk, blk, sib).wait_recv()
        src, dst = ins[0], block(full, 4 * x + 2 * y + c)
        lcopy(0, src, dst).wait()
        rcopy(0, src, dst, sib).wait_send()
        for j, (dev, chip) in enumerate(peers):
            rcopy(1 + j, src, dst, dev).wait_send()
            blk = block(full, 2 * chip + c)
            rcopy(4 + j, blk, blk, sib).wait_send()

    return _Hook([shard], [jax.ShapeDtypeStruct((NDEV * rows, D), shard.dtype)], 7, 1, start, finish, relay)


def _rs_d2d_hook(gfull):
    rows = gfull.shape[0] // NDEV

    def pairs(g, land):
        x, y, c = _my_place()
        return (x, y, 1 - c), [(k, _rows(g, (2 * k + 1 - c) * rows, rows), land.at[k]) for k in range(NCHIP)]

    def start(ins, outs, rcopy, lcopy):
        sib, cps = pairs(ins[0], outs[0])
        for i, src, dst in cps:
            rcopy(i, src, dst, sib).start()

    def finish(ins, outs, rcopy, lcopy):
        sib, cps = pairs(ins[0], outs[0])
        for i, src, dst in cps:
            rcopy(i, dst, dst, sib).wait_recv()
        for i, src, dst in cps:
            rcopy(i, src, dst, sib).wait_send()

    return _Hook([gfull], [jax.ShapeDtypeStruct((NCHIP, rows, D), gfull.dtype)], NCHIP, 0, start, finish)


def _rs_ici_hook(part):
    def start(ins, outs, rcopy, lcopy):
        x, y, c = _my_place()
        mychip = 2 * x + y
        lcopy(0, ins[0].at[mychip], outs[0].at[mychip]).start()
        for j, (dev, chip) in enumerate(_ici_peers(x, y, c)):
            rcopy(j, ins[0].at[chip], outs[0].at[mychip], dev).start()

    def finish(ins, outs, rcopy, lcopy):
        x, y, c = _my_place()
        mychip = 2 * x + y
        peers = _ici_peers(x, y, c)
        for j, (dev, chip) in enumerate(peers):
            rcopy(j, outs[0].at[chip], outs[0].at[chip], dev).wait_recv()
        for j, (dev, chip) in enumerate(peers):
            rcopy(j, ins[0].at[chip], outs[0].at[mychip], dev).wait_send()
        lcopy(0, ins[0].at[mychip], outs[0].at[mychip]).wait()

    return _Hook([part], [jax.ShapeDtypeStruct(part.shape, part.dtype)], 3, 1, start, finish)


def _small_hook(arrays):
    n = len(arrays)

    def peers():
        x, y, c = _my_place()
        out = []
        for dx in range(2):
            for dy in range(2):
                for dc in range(2):
                    if dx + dy + dc:
                        px, py, pc = (1 - x if dx else x), (1 - y if dy else y), (1 - c if dc else c)
                        out.append(((px, py, pc), 4 * px + 2 * py + pc))
        return 4 * x + 2 * y + c, out

    def start(ins, outs, rcopy, lcopy):
        me, ps = peers()
        for t in range(n):
            lcopy(t, ins[t], outs[t].at[me]).start()
            for i, (dev, _) in enumerate(ps):
                rcopy(n * i + t, ins[t], outs[t].at[me], dev).start()

    def finish(ins, outs, rcopy, lcopy):
        me, ps = peers()
        for t in range(n):
            for i, (dev, peer) in enumerate(ps):
                rcopy(n * i + t, outs[t].at[peer], outs[t].at[peer], dev).wait_recv()
            for i, (dev, _) in enumerate(ps):
                rcopy(n * i + t, ins[t], outs[t].at[me], dev).wait_send()
            lcopy(t, ins[t], outs[t].at[me]).wait()

    return _Hook(arrays, [jax.ShapeDtypeStruct((NDEV,) + a.shape, a.dtype) for a in arrays], 7 * n, n, start, finish)


def _wblock(w):
    return _resident(w.shape, lambda *_: (0, 0))


def _ffn_fwd(name, x, nrm, wg, wu, wd, hooks=(), tm=512):
    S = x.shape[0]

    def body(x_ref, n_ref, wg_ref, wu_ref, wd_ref, y_ref, g_ref, u_ref, a_ref, h_ref, acc_ref):
        xv = x_ref[...]
        _, _, h = _rms_fwd(xv, n_ref[...])
        h = h.astype(CDT)
        h_ref[...] = h
        for ci in range(F // FT):
            sl = slice(ci * FT, (ci + 1) * FT)
            g = _mm_nt(h, wg_ref[sl, :])
            u = _mm_nt(h, wu_ref[sl, :])
            g_ref[:, sl] = g.astype(CDT)
            u_ref[:, sl] = u.astype(CDT)
            a = (g * jax.nn.sigmoid(g) * u).astype(CDT)
            a_ref[:, sl] = a
            o = _mm(a, wd_ref[sl, :])
            if ci == 0:
                acc_ref[...] = o
            else:
                acc_ref[...] += o
        y_ref[...] = xv + 0.5 * acc_ref[...]

    tok = pl.BlockSpec((tm, D), lambda i: (i, 0))
    hid = pl.BlockSpec((tm, F), lambda i: (i, 0))
    hidden = jax.ShapeDtypeStruct((S, F), CDT)
    return _call(
        body, hooks, [x, nrm, wg, wu, wd], name=name, grid=(S // tm,),
        out_shape=[jax.ShapeDtypeStruct((S, D), f32), hidden, hidden, hidden, jax.ShapeDtypeStruct((S, D), CDT)],
        in_specs=[tok, _resident((1, D), lambda i: (0, 0)), _wblock(wg), _wblock(wu), _wblock(wd)],
        out_specs=[tok, hid, hid, hid, tok],
        scratch_shapes=[pltpu.VMEM((tm, D), f32)],
        compiler_params=_params("arbitrary"),
    )


def _proj_fwd(x1, nrm, wfull, b3, cos, sin, hooks=(), tm=512):
    S = x1.shape[0]

    def body(x_ref, n_ref, w_ref, b_ref, cos_ref, sin_ref, p_ref, h_ref):
        _, _, h = _rms_fwd(x_ref[...], n_ref[...])
        h = h.astype(CDT)
        h_ref[...] = h
        for p in range(8):
            seg = SEG_OF_SLOT[p]
            z = _mm(h, w_ref[seg * D:(seg + 1) * D, :]) + b_ref[seg]
            if p in (SLOT_Q, SLOT_K):
                co, si = cos_ref[...], sin_ref[...]
                for hh in range(H):
                    cs = slice(hh * DK, (hh + 1) * DK)
                    zr = _rotate(z[:, cs], co, si)
                    p_ref[p, :, cs] = (zr * K_SCALE if p == SLOT_K else zr).astype(CDT)
            else:
                p_ref[p] = z.astype(CDT)

    tab = pl.BlockSpec((tm, DK // 2), lambda i: (i, 0))
    return _call(
        body, hooks, [x1, nrm, wfull, b3, cos, sin], name="proj_fwd", grid=(S // tm,),
        out_shape=[jax.ShapeDtypeStruct((8, S, D), CDT), jax.ShapeDtypeStruct((S, D), CDT)],
        in_specs=[pl.BlockSpec((tm, D), lambda i: (i, 0)), _resident((1, D), lambda i: (0, 0)),
                  _resident((8 * D, D), lambda i: (0, 0)), _resident((8, 1, D), lambda i: (0, 0, 0)), tab, tab],
        out_specs=[pl.BlockSpec((8, tm, D), lambda i: (0, i, 0)), pl.BlockSpec((tm, D), lambda i: (i, 0))],
        compiler_params=_params("arbitrary"),
    )


def _sgu_norm(va, gn, bn):
    mu = jnp.mean(va, axis=-1, keepdims=True)
    xc = va - mu
    rstd = lax.rsqrt(jnp.mean(xc * xc, axis=-1, keepdims=True) + EPS)
    vhat = xc * rstd
    return rstd, vhat, vhat * gn + bn


def _sgu_fwd(proj, gn, bn, ws, bsc, tm=512):
    S = proj.shape[1]
    GW = D // G

    def body(p_ref, gn_ref, bn_ref, ws_ref, bs_ref, a_ref):
        ua = _gelu(p_ref[0].astype(f32))
        va = _gelu(p_ref[1].astype(f32))
        _, _, vn = _sgu_norm(va, gn_ref[...], bn_ref[...])
        vn = vn.astype(CDT)
        for ch in range(tm // C):
            rs = slice(ch * C, (ch + 1) * C)
            for gi in range(G):
                cs = slice(gi * GW, (gi + 1) * GW)
                s = _mm(ws_ref[gi], vn[rs, cs]) + bs_ref[gi]
                a_ref[rs, cs] = (ua[rs, cs] * s).astype(CDT)

    return pl.pallas_call(
        body, name="sgu_fwd", grid=(S // tm,),
        out_shape=jax.ShapeDtypeStruct((S, D), CDT),
        in_specs=[pl.BlockSpec((2, tm, D), lambda i: (0, i, 0)), _resident((1, D), lambda i: (0, 0)),
                  _resident((1, D), lambda i: (0, 0)), _resident((G, C, C), lambda i: (0, 0, 0)),
                  _resident((G, C, 1), lambda i: (0, 0, 0))],
        out_specs=pl.BlockSpec((tm, D), lambda i: (i, 0)),
        compiler_params=_params("arbitrary"),
    )(proj, gn, bn, ws, bsc)


def _decay_tables(dl_ref):
    lg = jax.nn.log_sigmoid(dl_ref[0:2, :])
    lgf, lgb = lg[0:1, :], lg[1:2, :]
    assert RC <= DK
    ri = lax.broadcasted_iota(jnp.int32, (RC, RC), 0)
    ci = lax.broadcasted_iota(jnp.int32, (RC, RC), 1)
    d = (ri - ci).astype(f32)
    lower = d >= 0
    dmat = jnp.where(lower, jnp.exp(d * lgf[:, :RC]), jnp.exp(-d * lgb[:, :RC]))
    dmat_t = jnp.where(d <= 0, jnp.exp(-d * lgf[:, :RC]), jnp.exp(d * lgb[:, :RC]))
    pos = lax.broadcasted_iota(jnp.int32, (RC, DK), 0).astype(f32)
    t = dict(
        lgf=lgf, lgb=lgb, d=d, lower=lower, dmat=dmat, dmat_t=dmat_t, pos=pos,
        fq=jnp.exp((pos + 1.0) * lgf), fk=jnp.exp((RC - 1.0 - pos) * lgf),
        bq=jnp.exp((RC - pos) * lgb), bk=jnp.exp(pos * lgb),
        lamf=jnp.exp(float(RC) * lgf), lamb=jnp.exp(float(RC) * lgb),
    )
    return t


def _rotate(t, co, si):
    t1, t2 = t[:, :DK // 2], t[:, DK // 2:]
    return jnp.concatenate([t1 * co - t2 * si, t2 * co + t1 * si], axis=-1)


def _unrotate(t, co, si):
    t1, t2 = t[:, :DK // 2], t[:, DK // 2:]
    return jnp.concatenate([t1 * co + t2 * si, t2 * co - t1 * si], axis=-1)


K_SCALE = DK ** -0.5
ROW_TILE = 256


def _ret_fwd(proj, dl, hooks=()):
    S = proj.shape[1]
    NC = S // RC

    def body(q_ref, k_ref, v_ref, g_ref, dl_ref, R_ref, r_ref, sfs_ref, sbs_ref, rb_ref, sf_ref, sb_ref):
        t = _decay_tables(dl_ref)

        def chunk(n):
            rows = pl.ds(pl.multiple_of(n * RC, RC), RC)
            return rows, q_ref[rows, :], k_ref[rows, :], v_ref[rows, :]

        sf_ref[...] = jnp.zeros_like(sf_ref)
        sb_ref[...] = jnp.zeros_like(sb_ref)

        def step(i, carry):
            rows, qn, kn, vn = chunk(i)
            sc = _mm_nt(qn, kn) * t["dmat"]
            out = _mm(sc.astype(CDT), vn)
            sf = sf_ref[...]
            sfb = sf.astype(CDT)
            sfs_ref[i] = sfb
            R_ref[rows, :] = out + _mm((qn.astype(f32) * t["fq"]).astype(CDT), sfb)
            sf_ref[...] = sf * t["lamf"] + _mm_tn((kn.astype(f32) * t["fk"]).astype(CDT), vn)
            m = NC - 1 - i
            rows, qn, kn, vn = chunk(m)
            sb = sb_ref[...]
            sbb = sb.astype(CDT)
            sbs_ref[m] = sbb
            rb_ref[rows, :] = _mm((qn.astype(f32) * t["bq"]).astype(CDT), sbb)
            sb_ref[...] = sb * t["lamb"] + _mm_tn((kn.astype(f32) * t["bk"]).astype(CDT), vn)
            return carry

        lax.fori_loop(0, NC, step, 0)

        def finish(i, carry):
            rs = pl.ds(pl.multiple_of(i * ROW_TILE, ROW_TILE), ROW_TILE)
            R = R_ref[rs, :] + rb_ref[rs, :]
            R_ref[rs, :] = R
            rn = R * lax.rsqrt(jnp.mean(R * R, axis=-1, keepdims=True) + EPS)
            g = g_ref[rs, :].astype(f32)
            r_ref[rs, :] = (rn * g * jax.nn.sigmoid(g)).astype(CDT)
            return carry

        lax.fori_loop(0, S // ROW_TILE, finish, 0)

    def seg(slot):
        return pl.BlockSpec((None, S, DK), lambda h: (slot, 0, h))

    states = jax.ShapeDtypeStruct((H, NC, DK, DK), CDT)
    state_blk = pl.BlockSpec((None, NC, DK, DK), lambda h: (h, 0, 0, 0))
    return _call(
        body, hooks, [proj, proj, proj, proj, dl], name="ret_fwd", grid=(H,),
        out_shape=[jax.ShapeDtypeStruct((S, H * DK), f32), jax.ShapeDtypeStruct((S, H * DK), CDT), states, states],
        in_specs=[seg(SLOT_Q), seg(SLOT_K), seg(SLOT_VR), seg(SLOT_GR), pl.BlockSpec((None, 8, DK), lambda h: (h, 0, 0))],
        out_specs=[pl.BlockSpec((S, DK), lambda h: (0, h)), pl.BlockSpec((S, DK), lambda h: (0, h)), state_blk, state_blk],
        scratch_shapes=[pltpu.VMEM((S, DK), f32), pltpu.VMEM((DK, DK), f32), pltpu.VMEM((DK, DK), f32)],
        compiler_params=_params("arbitrary"),
    )


def _merge_fwd(a, r, proj, x1, wa, wb, wo, tm=512):
    S = x1.shape[0]

    def body(a_ref, r_ref, gt_ref, x_ref, wa_ref, wb_ref, wo_ref, x2_ref, ya_ref, yb_ref):
        ya = _mm(a_ref[...], wa_ref[...])
        yb = _mm(r_ref[...], wb_ref[...])
        ya_ref[...] = ya.astype(CDT)
        yb_ref[...] = yb.astype(CDT)
        mix = jax.nn.sigmoid(gt_ref[0].astype(f32)) * ya + jax.nn.sigmoid(gt_ref[1].astype(f32)) * yb
        x2_ref[...] = x_ref[...] + _mm(mix.astype(CDT), wo_ref[...])

    tok = pl.BlockSpec((tm, D), lambda i: (i, 0))
    return pl.pallas_call(
        body, name="merge_fwd", grid=(S // tm,),
        out_shape=[jax.ShapeDtypeStruct((S, D), f32), jax.ShapeDtypeStruct((S, D), CDT), jax.ShapeDtypeStruct((S, D), CDT)],
        in_specs=[tok, tok, pl.BlockSpec((2, tm, D), lambda i: (SLOT_GA // 2, i, 0)), tok,
                  _wblock(wa), _wblock(wb), _wblock(wo)],
        out_specs=[tok, tok, tok],
        compiler_params=_params("arbitrary"),
    )(a, r, proj, x1, wa, wb, wo)


def _loss_head(x3, fn, target, tm=512):
    S = x3.shape[0]

    def body(x_ref, n_ref, t_ref, dx_ref, dxh_ref, dn_ref, l_ref):
        n = n_ref[...]
        r, xh, y = _rms_fwd(x_ref[...], n)
        e = y - t_ref[...]
        dy = e * (1.0 / D)
        dx, dn = _rms_bwd(dy, r, xh, n)
        dx_ref[...] = dx
        dxh_ref[...] = (0.5 * dx).astype(CDT)
        part = 0.5 * jnp.sum(jnp.sum(e * e, axis=-1, keepdims=True), axis=0, keepdims=True) * (1.0 / D)

        @pl.when(pl.program_id(0) == 0)
        def _():
            dn_ref[...] = jnp.zeros_like(dn_ref)
            l_ref[...] = jnp.zeros_like(l_ref)

        dn_ref[...] += dn
        l_ref[...] += jnp.broadcast_to(part, l_ref.shape)

    tok = pl.BlockSpec((tm, D), lambda i: (i, 0))
    return pl.pallas_call(
        body, name="loss_head", grid=(S // tm,),
        out_shape=[jax.ShapeDtypeStruct((S, D), f32), jax.ShapeDtypeStruct((S, D), CDT), jax.ShapeDtypeStruct((1, D), f32),
                   jax.ShapeDtypeStruct((8, 128), f32)],
        in_specs=[tok, _resident((1, D), lambda i: (0, 0)), tok],
        out_specs=[tok, tok, pl.BlockSpec((1, D), lambda i: (0, 0)), pl.BlockSpec((8, 128), lambda i: (0, 0))],
        compiler_params=_params("arbitrary"),
    )(x3, fn, target)


def _ffn_bwd_hidden(name, dyh, g, u, wd, hooks=(), tm=512):
    S = dyh.shape[0]

    def body(dyh_ref, g_ref, u_ref, wd_ref, dg_ref, du_ref):
        dyh = dyh_ref[...]
        for ci in range(F // FT):
            sl = slice(ci * FT, (ci + 1) * FT)
            da = _mm_nt(dyh, wd_ref[sl, :])
            gv = g_ref[:, sl].astype(f32)
            uv = u_ref[:, sl].astype(f32)
            s = jax.nn.sigmoid(gv)
            du_ref[:, sl] = (da * (gv * s)).astype(CDT)
            dg_ref[:, sl] = (da * uv * (s * (1.0 + gv * (1.0 - s)))).astype(CDT)

    hid = pl.BlockSpec((tm, F), lambda i: (i, 0))
    hidden = jax.ShapeDtypeStruct((S, F), CDT)
    return _call(
        body, hooks, [dyh, g, u, wd], name=name, grid=(S // tm,), out_shape=[hidden, hidden],
        in_specs=[pl.BlockSpec((tm, D), lambda i: (i, 0)), hid, hid, _wblock(wd)], out_specs=[hid, hid],
        compiler_params=_params("arbitrary"),
    )


def _ffn_bwd_in(name, dy, x, dg, du, nrm, wg, wu, hooks=(), part=None, prev=None, tm=512):
    S = x.shape[0]
    t0, nt = part or (0, S // tm)

    def body(dy_ref, x_ref, dg_ref, du_ref, n_ref, wg_ref, wu_ref, *rest):
        dx_ref, dn_ref, acc_ref = rest[-3:]
        n = n_ref[...]
        r, xh, _ = _rms_fwd(x_ref[...], n)
        for ci in range(F // FT):
            sl = slice(ci * FT, (ci + 1) * FT)
            dh = _mm(dg_ref[:, sl], wg_ref[sl, :]) + _mm(du_ref[:, sl], wu_ref[sl, :])
            if ci == 0:
                acc_ref[...] = dh
            else:
                acc_ref[...] += dh
        dx, dn = _rms_bwd(acc_ref[...], r, xh, n)
        dx_ref[...] = dy_ref[...] + dx

        @pl.when(pl.program_id(0) == 0)
        def _():
            dn_ref[...] = jnp.zeros_like(dn_ref) if prev is None else rest[1][...]

        dn_ref[...] += dn

    tok = pl.BlockSpec((tm, D), lambda i: (t0 + i, 0))
    hid = pl.BlockSpec((tm, F), lambda i: (t0 + i, 0))
    row = pl.BlockSpec((1, D), lambda i: (0, 0))
    in_specs = [tok, tok, hid, hid, _resident((1, D), lambda i: (0, 0)), _wblock(wg), _wblock(wu)]
    operands = [dy, x, dg, du, nrm, wg, wu]
    aliases = {}
    if prev is not None:
        in_specs += [_HBM, row]
        operands += list(prev)
        aliases = {7: 0}
    return _call(
        body, hooks, operands, name=name, grid=(nt,),
        out_shape=[jax.ShapeDtypeStruct((S, D), f32), jax.ShapeDtypeStruct((1, D), f32)],
        in_specs=in_specs, out_specs=[tok, row],
        scratch_shapes=[pltpu.VMEM((tm, D), f32)],
        input_output_aliases=aliases,
        compiler_params=_params("arbitrary"),
    )


TN_ROWS = 512


def _tn(name, xs, ys, block_of, hooks=()):
    S, M = xs.shape
    B = ys.shape[0]
    tr = TN_ROWS if M % TN_ROWS == 0 else M // 2
    assert M % tr == 0 and tr % 128 == 0
    nt = M // tr

    def body(x_ref, y_ref, o_ref):
        o_ref[...] = _mm_tn(x_ref[...], y_ref[...]).astype(CDT)

    return _call(
        body, hooks, [xs, ys], name=name, grid=(B, nt),
        out_shape=[jax.ShapeDtypeStruct((B * M, D), CDT)],
        in_specs=[pl.BlockSpec((S, tr), lambda b, i: (0, i)), pl.BlockSpec((None, S, D), lambda b, i: (b, 0, 0))],
        out_specs=[pl.BlockSpec((tr, D), lambda b, i: (block_of(b) * nt + i, 0))],
        compiler_params=_params("arbitrary", "arbitrary"),
    )


def _wgrad(name, xs, y, hooks=()):
    return _tn(name, xs, y[None], lambda b: 0, hooks)


def _merge_bwd_act(dx2, ya, yb, proj, wa, wb, wo, hooks=(), tm=512):
    S = dx2.shape[0]

    def body(dx_ref, ya_ref, yb_ref, gt_ref, wa_ref, wb_ref, wo_ref,
             dp_ref, da_ref, dr_ref, mix_ref, dxb_ref, dya_ref, dyb_ref):
        dxb = dx_ref[...].astype(CDT)
        dxb_ref[...] = dxb
        dmix = _mm_nt(dxb, wo_ref[...])
        ya = ya_ref[...].astype(f32)
        yb = yb_ref[...].astype(f32)
        sa = jax.nn.sigmoid(gt_ref[0].astype(f32))
        sb = jax.nn.sigmoid(gt_ref[1].astype(f32))
        mix_ref[...] = (sa * ya + sb * yb).astype(CDT)
        dya = (dmix * sa).astype(CDT)
        dyb = (dmix * sb).astype(CDT)
        dya_ref[...] = dya
        dyb_ref[...] = dyb
        dp_ref[0] = (dmix * ya * sa * (1.0 - sa)).astype(CDT)
        dp_ref[1] = (dmix * yb * sb * (1.0 - sb)).astype(CDT)
        da_ref[...] = _mm_nt(dya, wa_ref[...]).astype(CDT)
        dr_ref[...] = _mm_nt(dyb, wb_ref[...]).astype(CDT)

    tok = pl.BlockSpec((tm, D), lambda i: (i, 0))
    gates = pl.BlockSpec((2, tm, D), lambda i: (SLOT_GA // 2, i, 0))
    act = jax.ShapeDtypeStruct((S, D), CDT)
    return _call(
        body, hooks, [dx2, ya, yb, proj, wa, wb, wo], name="merge_bwd_act", grid=(S // tm,),
        out_shape=[jax.ShapeDtypeStruct((8, S, D), CDT), act, act, act, act, act, act],
        in_specs=[tok, tok, tok, gates, _wblock(wa), _wblock(wb), _wblock(wo)],
        out_specs=[gates, tok, tok, tok, tok, tok, tok],
        compiler_params=_params("arbitrary"),
    )


def _sgu_bwd(da, proj, dproj, gn, bn, ws, wst, bsc, hooks=(), tm=512):
    S = proj.shape[1]
    GW = D // G

    def body(da_ref, p_ref, dpin_ref, gn_ref, bn_ref, ws_ref, wst_ref, bs_ref,
             dp_ref, dws_ref, dbs_ref, dgn_ref, dbn_ref, ds_ref, dvn_ref):
        @pl.when(pl.program_id(0) == 0)
        def _():
            dws_ref[...] = jnp.zeros_like(dws_ref)
            dbs_ref[...] = jnp.zeros_like(dbs_ref)
            dgn_ref[...] = jnp.zeros_like(dgn_ref)
            dbn_ref[...] = jnp.zeros_like(dbn_ref)

        pu = p_ref[0].astype(f32)
        pv = p_ref[1].astype(f32)
        ua = _gelu(pu)
        va = _gelu(pv)
        gn = gn_ref[...]
        rstd, vhat, vn = _sgu_norm(va, gn, bn_ref[...])
        vnb = vn.astype(CDT)
        dav = da_ref[...].astype(f32)
        dsb = (dav * ua).astype(CDT)
        ones = jnp.ones((8, GW), CDT)
        for ch in range(tm // C):
            rs = slice(ch * C, (ch + 1) * C)
            for gi in range(G):
                cs = slice(gi * GW, (gi + 1) * GW)
                s = _mm(ws_ref[gi], vnb[rs, cs]) + bs_ref[gi]
                ds_ref[rs, cs] = s
                dsg = dsb[rs, cs]
                dws_ref[gi] += _mm_nt(dsg, vnb[rs, cs])
                dbs_ref[gi] += _mm_nt(ones, dsg)
                dvn_ref[rs, cs] = _mm(wst_ref[gi], dsg)
        dp_ref[0] = (dav * ds_ref[...] * _gelu_grad(pu)).astype(CDT)
        dvn = dvn_ref[...]
        dgn_ref[...] += jnp.sum(dvn * vhat, axis=0, keepdims=True)
        dbn_ref[...] += jnp.sum(dvn, axis=0, keepdims=True)
        dvh = dvn * gn
        dva = rstd * (dvh - jnp.mean(dvh, axis=-1, keepdims=True) - vhat * jnp.mean(dvh * vhat, axis=-1, keepdims=True))
        dp_ref[1] = (dva * _gelu_grad(pv)).astype(CDT)

    uv = pl.BlockSpec((2, tm, D), lambda i: (0, i, 0))
    row = _resident((1, D), lambda i: (0, 0))
    return _call(
        body, hooks, [da, proj, dproj, gn, bn, ws, wst, bsc], name="sgu_bwd", grid=(S // tm,),
        out_shape=[jax.ShapeDtypeStruct(dproj.shape, CDT), jax.ShapeDtypeStruct((G, C, C), f32),
                   jax.ShapeDtypeStruct((G, 8, C), f32), jax.ShapeDtypeStruct((1, D), f32), jax.ShapeDtypeStruct((1, D), f32)],
        in_specs=[pl.BlockSpec((tm, D), lambda i: (i, 0)), uv, _HBM, row, row,
                  _resident((G, C, C), lambda i: (0, 0, 0)), _resident((G, C, C), lambda i: (0, 0, 0)),
                  _resident((G, C, 1), lambda i: (0, 0, 0))],
        out_specs=[uv, pl.BlockSpec((G, C, C), lambda i: (0, 0, 0)), pl.BlockSpec((G, 8, C), lambda i: (0, 0, 0)),
                   pl.BlockSpec((1, D), lambda i: (0, 0)), pl.BlockSpec((1, D), lambda i: (0, 0))],
        scratch_shapes=[pltpu.VMEM((tm, D), f32), pltpu.VMEM((tm, D), f32)],
        input_output_aliases={2: 0},
        compiler_params=_params("arbitrary"),
    )


def _ret_bwd(dr, R, sfs, sbs, proj, dproj, cos, sin, dl, hooks=()):
    S = proj.shape[1]
    NC = S // RC
    assert NC % 2 == 0

    def body(dr_ref, R_ref, sf_ref, sb_ref, q_ref, k_ref, v_ref, g_ref, dpin_ref, cos_ref, sin_ref, dl_ref,
             dp_ref, dd_ref, dR_ref, gb_ref, gf_ref, acc_ref):
        t = _decay_tables(dl_ref)

        def gate_norm_bwd(i, carry):
            rs = pl.ds(pl.multiple_of(i * ROW_TILE, ROW_TILE), ROW_TILE)
            Rv = R_ref[rs, :]
            rstd = lax.rsqrt(jnp.mean(Rv * Rv, axis=-1, keepdims=True) + EPS)
            rn = Rv * rstd
            gv = g_ref[rs, :].astype(f32)
            s = jax.nn.sigmoid(gv)
            drv = dr_ref[rs, :].astype(f32)
            dp_ref[3, rs, :] = (drv * rn * (s * (1.0 + gv * (1.0 - s)))).astype(CDT)
            drn = drv * gv * s
            dR_ref[rs, :] = (rstd * (drn - rn * jnp.mean(drn * rn, axis=-1, keepdims=True))).astype(CDT)
            return carry

        lax.fori_loop(0, S // ROW_TILE, gate_norm_bwd, 0)

        def chunk(n):
            rows = pl.ds(pl.multiple_of(n * RC, RC), RC)
            return rows, q_ref[rows, :], k_ref[rows, :], v_ref[rows, :], dR_ref[rows, :]

        def emit_kv(rows, dk, dv, final):
            if not final:
                dp_ref[1, rows, :] = dk.astype(CDT)
                dp_ref[2, rows, :] = dv.astype(CDT)
            else:
                co, si = cos_ref[rows, :], sin_ref[rows, :]
                dk = dp_ref[1, rows, :].astype(f32) + dk
                dp_ref[1, rows, :] = (_unrotate(dk, co, si) * K_SCALE).astype(CDT)
                dp_ref[2, rows, :] = (dp_ref[2, rows, :].astype(f32) + dv).astype(CDT)

        gb_ref[...] = jnp.zeros_like(gb_ref)
        gf_ref[...] = jnp.zeros_like(gf_ref)
        acc_ref[...] = jnp.zeros_like(acc_ref)
        dpos = jnp.abs(t["d"])

        def ascend(n, final):
            rows, qn, kn, vn, dRn = chunk(n)
            qf, kf = qn.astype(f32), kn.astype(f32)
            sc = _mm_nt(qn, kn)
            dA = _mm_nt(dRn, vn)
            w = sc * dA * t["dmat"] * dpos
            lgf_part = jnp.sum(jnp.where(t["lower"], w, 0.0), axis=0, keepdims=True)
            lgb_part = jnp.sum(jnp.where(t["lower"], 0.0, w), axis=0, keepdims=True)
            dsc = (dA * t["dmat"]).astype(CDT)
            dq = _mm(dsc, kn)
            scT = (_mm_nt(kn, qn) * t["dmat_t"]).astype(CDT)
            dscT = (_mm_nt(vn, dRn) * t["dmat_t"]).astype(CDT)
            dk = _mm(dscT, qn)
            dv = _mm(scT, dRn)
            sfb = sf_ref[n]
            sbb = sb_ref[n]
            qdf = qf * t["fq"]
            dqdf = _mm_nt(dRn, sfb)
            dq += dqdf * t["fq"]
            lgf_row = jnp.sum(qdf * dqdf * (t["pos"] + 1.0), axis=0, keepdims=True)
            qdb = qf * t["bq"]
            dqdb = _mm_nt(dRn, sbb)
            dq += dqdb * t["bq"]
            lgb_row = jnp.sum(qdb * dqdb * (RC - t["pos"]), axis=0, keepdims=True)
            gb = gb_ref[...]
            gbb = gb.astype(CDT)
            kdb = kf * t["bk"]
            dkdb = _mm_nt(vn, gbb)
            dk += dkdb * t["bk"]
            dv += _mm(kdb.astype(CDT), gbb)
            lgb_row += jnp.sum(kdb * dkdb * t["pos"], axis=0, keepdims=True)
            lgb_row += float(RC) * t["lamb"] * jnp.sum(gb * sbb.astype(f32), axis=0, keepdims=True)
            co, si = cos_ref[rows, :], sin_ref[rows, :]
            dp_ref[0, rows, :] = _unrotate(dq, co, si).astype(CDT)
            emit_kv(rows, dk, dv, final)
            acc_ref[0:1, :] += lgf_row + lgf_part
            acc_ref[1:2, :] += lgb_row + lgb_part
            gb_ref[...] = gb * t["lamb"] + _mm_tn(qdb.astype(CDT), dRn)

        def descend(n, final):
            rows, qn, kn, vn, dRn = chunk(n)
            gf = gf_ref[...]
            gfb = gf.astype(CDT)
            kdf = kn.astype(f32) * t["fk"]
            dkdf = _mm_nt(vn, gfb)
            lgf_row = jnp.sum(kdf * dkdf * (RC - 1.0 - t["pos"]), axis=0, keepdims=True)
            lgf_row += float(RC) * t["lamf"] * jnp.sum(gf * sf_ref[n].astype(f32), axis=0, keepdims=True)
            acc_ref[0:1, :] += lgf_row
            emit_kv(rows, dkdf * t["fk"], _mm(kdf.astype(CDT), gfb), final)
            gf_ref[...] = gf * t["lamf"] + _mm_tn((qn.astype(f32) * t["fq"]).astype(CDT), dRn)

        def sweep(final):
            def step(i, carry):
                ascend(i, final)
                descend(NC - 1 - i, final)
                return carry
            return step

        lax.fori_loop(0, NC // 2, sweep(False), 0)
        lax.fori_loop(NC // 2, NC, sweep(True), 0)
        dlg = jnp.sum(acc_ref[...], axis=1, keepdims=True)
        dlogit = dlg * jax.nn.sigmoid(-dl_ref[:, 0:1])
        lane = lax.broadcasted_iota(jnp.int32, (8, 128), 1)
        dd_ref[...] = jnp.where(lane == pl.program_id(0), jnp.broadcast_to(dlogit, (8, 128)), 0.0)

    def seg(slot):
        return pl.BlockSpec((None, S, DK), lambda h: (slot, 0, h))

    head = pl.BlockSpec((S, DK), lambda h: (0, h))
    states = pl.BlockSpec((None, NC, DK, DK), lambda h: (h, 0, 0, 0))
    return _call(
        body, hooks, [dr, R, sfs, sbs, proj, proj, proj, proj, dproj, cos, sin, dl], name="ret_bwd", grid=(H,),
        out_shape=[jax.ShapeDtypeStruct(dproj.shape, CDT), jax.ShapeDtypeStruct((H, 8, 128), f32)],
        in_specs=[head, head, states, states, seg(SLOT_Q), seg(SLOT_K), seg(SLOT_VR), seg(SLOT_GR), _HBM,
                  _resident((S, DK // 2), lambda h: (0, 0)), _resident((S, DK // 2), lambda h: (0, 0)),
                  pl.BlockSpec((None, 8, DK), lambda h: (h, 0, 0))],
        out_specs=[pl.BlockSpec((4, S, DK), lambda h: (1, 0, h), pipeline_mode=pl.Buffered(1)),
                   pl.BlockSpec((None, 8, 128), lambda h: (h, 0, 0))],
        scratch_shapes=[pltpu.VMEM((S, DK), CDT),
                        pltpu.VMEM((DK, DK), f32), pltpu.VMEM((DK, DK), f32), pltpu.VMEM((8, DK), f32)],
        input_output_aliases={8: 0},
        compiler_params=_params("arbitrary"),
    )


def _proj_bwd_act(dproj, dx2, x1, nrm, wfull, hooks=(), tm=512):
    S = x1.shape[0]

    def body(dp_ref, dx2_ref, x_ref, n_ref, w_ref, dx_ref, dxh_ref, dn_ref, db_ref, acc_ref):
        @pl.when(pl.program_id(0) == 0)
        def _():
            dn_ref[...] = jnp.zeros_like(dn_ref)
            db_ref[...] = jnp.zeros_like(db_ref)

        for p in range(8):
            seg = SEG_OF_SLOT[p]
            dp = dp_ref[p]
            db_ref[seg] += jnp.sum(dp.astype(f32), axis=0, keepdims=True)
            dh = _mm_nt(dp, w_ref[seg * D:(seg + 1) * D, :])
            if p == 0:
                acc_ref[...] = dh
            else:
                acc_ref[...] += dh
        n = n_ref[...]
        r, xh, _ = _rms_fwd(x_ref[...], n)
        dx, dn = _rms_bwd(acc_ref[...], r, xh, n)
        dx = dx2_ref[...] + dx
        dx_ref[...] = dx
        dxh_ref[...] = (0.5 * dx).astype(CDT)
        dn_ref[...] += dn

    tok = pl.BlockSpec((tm, D), lambda i: (i, 0))
    return _call(
        body, hooks, [dproj, dx2, x1, nrm, wfull], name="proj_bwd_act", grid=(S // tm,),
        out_shape=[jax.ShapeDtypeStruct((S, D), f32), jax.ShapeDtypeStruct((S, D), CDT), jax.ShapeDtypeStruct((1, D), f32),
                   jax.ShapeDtypeStruct((8, 1, D), f32)],
        in_specs=[pl.BlockSpec((8, tm, D), lambda i: (0, i, 0)), tok, tok, _resident((1, D), lambda i: (0, 0)),
                  _resident((8 * D, D), lambda i: (0, 0))],
        out_specs=[tok, tok, pl.BlockSpec((1, D), lambda i: (0, 0)), pl.BlockSpec((8, 1, D), lambda i: (0, 0, 0))],
        scratch_shapes=[pltpu.VMEM((tm, D), f32)],
        compiler_params=_params("arbitrary"),
    )


def _rs_sum(name, gfulls, lands, my_c):
    n = len(gfulls)
    rows = gfulls[0].shape[0] // NDEV
    assert all(g.shape[0] == NDEV * rows for g in gfulls)

    def body(c_ref, *refs):
        for g_ref, l_ref, o_ref in zip(refs[:n], refs[n:2 * n], refs[2 * n:]):
            o_ref[...] = (g_ref[...].astype(f32) + l_ref[...].astype(f32)).astype(CDT)

    slot = pl.BlockSpec((None, rows, D), lambda k, c: (k, 0, 0))
    return pl.pallas_call(
        body, name=name,
        grid_spec=pltpu.PrefetchScalarGridSpec(
            num_scalar_prefetch=1, grid=(NCHIP,),
            in_specs=[pl.BlockSpec((rows, D), lambda k, c: (2 * k + c[0], 0))] * n + [slot] * n,
            out_specs=[slot] * n),
        out_shape=[jax.ShapeDtypeStruct((NCHIP, rows, D), CDT)] * n,
        compiler_params=_params("arbitrary"),
    )(my_c, *gfulls, *lands)


def _adamw_math(g, w, m, v):
    m2 = ADAM_B1 * m + (1.0 - ADAM_B1) * g
    v2 = ADAM_B2 * v + (1.0 - ADAM_B2) * (g * g)
    delta = -ADAM_LR * ((m2 / BC1) / (jnp.sqrt(v2 / BC2) + ADAM_EPS) + ADAM_WD * w)
    return delta, m2, v2


def _adamw_big(name, landed, w, m, v, hooks=()):
    rows = w.shape[0]
    tr = min(rows, 256) if rows % 256 == 0 else rows

    def body(l_ref, w_ref, m_ref, v_ref, g_ref, d_ref, m2_ref, v2_ref):
        g = l_ref[0].astype(f32)
        for k in range(1, NCHIP):
            g = g + l_ref[k].astype(f32)
        g_ref[...] = g
        d_ref[...], m2_ref[...], v2_ref[...] = _adamw_math(g, w_ref[...], m_ref[...], v_ref[...])

    blk = pl.BlockSpec((tr, D), lambda i: (i, 0))
    o = jax.ShapeDtypeStruct((rows, D), f32)
    return _call(
        body, hooks, [landed, w, m, v], name=name, grid=(rows // tr,), out_shape=[o, o, o, o],
        in_specs=[pl.BlockSpec((NCHIP, tr, D), lambda i: (0, i, 0)), blk, blk, blk],
        out_specs=[blk, blk, blk, blk],
        compiler_params=_params("arbitrary"),
    )


ROW_FFN1_NORM, ROW_MIX_NORM, ROW_SGU_G, ROW_SGU_B, ROW_FFN2_NORM, ROW_FINAL_NORM, ROW_B_IN = 0, 1, 2, 3, 4, 5, 8
ROW_WS, ROW_BS, ROW_DECAY = 0, G * C, G * C + G * 8


def _adamw_small(ga, gb, gn1, gl, params):
    def body(ga_ref, gb_ref, gn1_ref, gl_ref, *refs):
        ins, outs = refs[:30], refs[30:]

        def total(ref, r0, n):
            g = ref[0, r0:r0 + n, :]
            for j in range(1, NDEV):
                g = g + ref[j, r0:r0 + n, :]
            return g

        def apply(i, g, rows=slice(None)):
            w, m, v = ins[3 * i][rows, :], ins[3 * i + 1][rows, :], ins[3 * i + 2][rows, :]
            outs[4 * i][rows, :] = g
            outs[4 * i + 1][rows, :], outs[4 * i + 2][rows, :], outs[4 * i + 3][rows, :] = _adamw_math(g, w, m, v)

        outs[40][...] = total(gl_ref, 0, 8)
        apply(0, total(gn1_ref, 0, 1))
        for i, r in enumerate((ROW_FFN1_NORM, ROW_MIX_NORM, ROW_SGU_G, ROW_SGU_B, ROW_FFN2_NORM, ROW_FINAL_NORM)):
            if i:
                apply(i, total(ga_ref, r, 1))
        apply(6, total(ga_ref, ROW_B_IN, 8))
        apply(7, total(gb_ref, ROW_WS, G * C))
        for gi in range(G):
            apply(8, total(gb_ref, ROW_BS + 8 * gi, 1), slice(gi, gi + 1))
        dec = total(gb_ref, ROW_DECAY, 8)
        for hh in range(1, H):
            dec = dec + total(gb_ref, ROW_DECAY + 8 * hh, 8)
        apply(9, dec)

    flat = [a for p in params for a in p]
    out_shape = [jax.ShapeDtypeStruct(p[0].shape, f32) for p in params for _ in range(4)]
    out_shape.append(jax.ShapeDtypeStruct((8, 128), f32))
    vm = pl.BlockSpec(memory_space=pltpu.VMEM)
    return pl.pallas_call(
        body, name="adamw_small", out_shape=out_shape,
        in_specs=[vm] * (4 + len(flat)), out_specs=[vm] * len(out_shape),
        compiler_params=pltpu.CompilerParams(vmem_limit_bytes=VMEM_LIMIT),
    )(ga, gb, gn1, gl, *flat)


def kernel(x, ffn1_norm, ffn1_w_gate, ffn1_w_up, ffn1_w_down, mix_norm, w_in, b_in, sgu_norm_g, sgu_norm_b, sgu_w_s, sgu_b_s, ret_decay_logit, w_branch_a, w_branch_b, w_out, ffn2_norm, ffn2_w_gate, ffn2_w_up, ffn2_w_down, final_norm, loss_target, m_ffn1_norm, m_ffn1_w_gate, m_ffn1_w_up, m_ffn1_w_down, m_mix_norm, m_w_in, m_b_in, m_sgu_norm_g, m_sgu_norm_b, m_sgu_w_s, m_sgu_b_s, m_ret_decay_logit, m_w_branch_a, m_w_branch_b, m_w_out, m_ffn2_norm, m_ffn2_w_gate, m_ffn2_w_up, m_ffn2_w_down, m_final_norm, v_ffn1_norm, v_ffn1_w_gate, v_ffn1_w_up, v_ffn1_w_down, v_mix_norm, v_w_in, v_b_in, v_sgu_norm_g, v_sgu_norm_b, v_sgu_w_s, v_sgu_b_s, v_ret_decay_logit, v_w_branch_a, v_w_branch_b, v_w_out, v_ffn2_norm, v_ffn2_w_gate, v_ffn2_w_up, v_ffn2_w_down, v_final_norm):
    args = dict(locals())
    S = x.shape[1]
    xs = x[0]
    target = loss_target[0]

    def buf_layout(name, a):
        a = a[0]
        return a.T if name in W_TRANSPOSED else a

    sh = {n: buf_layout(n, args[n]).astype(CDT) for n in W_NAMES}
    wf = {}

    b3 = b_in.reshape(8, 1, D)
    ws = sgu_w_s[0].astype(CDT)
    wst = jnp.swapaxes(sgu_w_s[0], 1, 2).astype(CDT)
    bsc = sgu_b_s[0].reshape(G, C, 1)
    dl = jnp.zeros((H, 8, DK), f32).at[:, 0:2, :].set(jnp.broadcast_to(ret_decay_logit[0].T[:, :, None], (H, 2, DK)))
    theta = ROPE_BASE ** (-jnp.arange(0, DK, 2, dtype=f32) / DK)
    ang = jnp.arange(S, dtype=f32)[:, None] * theta[None, :]
    cos, sin = jnp.cos(ang), jnp.sin(ang)
    fnorm = final_norm.reshape(1, D)

    f1 = ("ffn1_w_gate", "ffn1_w_up", "ffn1_w_down")
    f2 = ("ffn2_w_gate", "ffn2_w_up", "ffn2_w_down")
    br = ("w_branch_a", "w_branch_b", "w_out")
    wf[f1[0]], wf[f1[1]], wf[f1[2]] = _exchange("ag_ffn1", [_ag_hook(sh[n]) for n in f1])
    x1, g1, u1, a1, hf1, wf["w_in"] = _ffn_fwd("ffn1_fwd", xs, ffn1_norm, *[wf[n] for n in f1], [_ag_hook(sh["w_in"])])
    proj, h2, wf[br[0]], wf[br[1]], wf[br[2]], wf[f2[0]] = _proj_fwd(
        x1, mix_norm, wf["w_in"], b3, cos, sin, [_ag_hook(sh[n]) for n in br + f2[:1]])
    a = _sgu_fwd(proj, sgu_norm_g, sgu_norm_b, ws, bsc)
    R, r, sfs, sbs, wf[f2[1]], wf[f2[2]] = _ret_fwd(proj, dl, [_ag_hook(sh[n]) for n in f2[1:]])
    x2, ya, yb = _merge_fwd(a, r, proj, x1, *[wf[n] for n in br])
    x3, g2, u2, a2, hf2 = _ffn_fwd("ffn2_fwd", x2, ffn2_norm, *[wf[n] for n in f2])
    dx3, dyh2, d_final, loss_part = _loss_head(x3, fnorm, target)

    my_c = lax.axis_index("c").astype(jnp.int32).reshape(1)
    gw, landed = {}, {}

    def d2d(*names):
        return [_rs_d2d_hook(gw[n]) for n in names]

    def ici(parts):
        return [_rs_ici_hook(p) for p in parts]

    def rs_sum(names, sibs):
        return list(_rs_sum("rs_sum_" + names[0], [gw[n] for n in names], list(sibs), my_c))

    def ffn_bwd(tag, names, dy, dyh, x, g, u, a, h, nrm, eager, on_first, on_hidden):
        wg, wu, wd = names
        gw[wd], *first = _wgrad(tag + "_wd_grad", a, dyh, on_first)
        dg, du, sib, *hidden = _ffn_bwd_hidden(tag + "_bwd_hidden", dyh, g, u, wf[wd], d2d(wd) + on_hidden)
        (part_d,) = rs_sum([wd], [sib])
        gw[wg], *got = _wgrad(tag + "_wg_grad", dg, h, ici([part_d]) if eager else [])
        landed.update(zip([wd], got))
        gw[wu], sib = _wgrad(tag + "_wu_grad", du, h, d2d(wg))
        (part_g,) = rs_sum([wg], [sib])
        args_in = (dy, x, dg, du, nrm, wf[wg], wf[wu])
        if not eager:
            dx, dn, sib = _ffn_bwd_in(tag + "_bwd_in", *args_in, d2d(wu))
            return dx, dn, [part_d, part_g] + rs_sum([wu], [sib]), first, hidden
        half = x.shape[0] // 512 // 2
        dx, dn, landed[wg], sib = _ffn_bwd_in(tag + "_bwd_in_a", *args_in, ici([part_g]) + d2d(wu), part=(0, half))
        dx, dn, landed[wu] = _ffn_bwd_in(tag + "_bwd_in_b", *args_in, ici(rs_sum([wu], [sib])), part=(half, half),
                                         prev=(dx, dn))
        return dx, dn, [], first, hidden

    dx2, d_ffn2n, parts_f2, _, _ = ffn_bwd("ffn2", f2, dx3, dyh2, x2, g2, u2, a2, hf2, ffn2_norm, False, [], [])
    dproj, da, dr, mix, dx2b, dya, dyb = _merge_bwd_act(dx2, ya, yb, proj, *[wf[n] for n in br])
    dproj, d_ws, d_bs, d_gn, d_bn = _sgu_bwd(da, proj, dproj, sgu_norm_g, sgu_norm_b, ws, wst, bsc)
    dproj, d_dec, *got = _ret_bwd(dr, R, sfs, sbs, proj, dproj, cos, sin, dl, ici(parts_f2))
    landed.update(zip((f2[2], f2[0], f2[1]), got))
    (gw["w_in"],) = _tn("win_grad", h2, dproj, _seg_of_slot)
    gw["w_out"], sib_win = _wgrad("wo_grad", mix, dx2b, d2d("w_in"))
    (part_win,) = rs_sum(["w_in"], [sib_win])
    (gw["w_branch_a"],) = _wgrad("wa_grad", a, dya)
    (gw["w_branch_b"],) = _wgrad("wb_grad", r, dyb)
    dx1, dyh1, d_mixn, d_bin, landed["w_in"], *sib = _proj_bwd_act(dproj, dx2, x1, mix_norm, wf["w_in"],
                                                                   ici([part_win]) + d2d(*br))
    part_br = rs_sum(br, sib)
    small_a = jnp.concatenate([jnp.zeros((1, D), f32), d_mixn, d_gn, d_bn, d_ffn2n, d_final, jnp.zeros((2, D), f32),
                               d_bin.reshape(8, D)], axis=0)
    small_b = jnp.concatenate([d_ws.reshape(G * C, C), d_bs.reshape(G * 8, C), d_dec.reshape(H * 8, 128)], axis=0)
    dxs, d_ffn1n, _, (ga, gb, gl), got = ffn_bwd("ffn1", f1, dx1, dyh1, xs, g1, u1, a1, hf1, ffn1_norm, True,
                                                [_small_hook([small_a, small_b, loss_part])], ici(part_br))
    landed.update(zip(br, got))
    (gn1,) = _exchange("ag_ffn1_norm", [_small_hook([d_ffn1n])])

    out = {"grad_x": dxs[None]}

    def native(name, a):
        a = a.T if name in W_TRANSPOSED else a
        return a[None]

    for n in W_NAMES:
        res = _adamw_big("adamw_" + n, landed[n], buf_layout(n, args[n]), buf_layout(n, args["m_" + n]),
                         buf_layout(n, args["v_" + n]))
        for pre, val in zip(("grad_", "delta_", "new_m_", "new_v_"), res):
            out[pre + n] = native(n, val)

    def pad_decay(a):
        return jnp.zeros((8, 128), f32).at[0:2, 0:H].set(a[0])

    small = [
        ("ffn1_norm", lambda a: a, lambda a: a), ("mix_norm", lambda a: a, lambda a: a),
        ("sgu_norm_g", lambda a: a, lambda a: a), ("sgu_norm_b", lambda a: a, lambda a: a),
        ("ffn2_norm", lambda a: a, lambda a: a),
        ("final_norm", lambda a: a.reshape(1, D), lambda a: a.reshape(D)),
        ("b_in", lambda a: a.reshape(8, D), lambda a: a.reshape(1, 8 * D)),
        ("sgu_w_s", lambda a: a.reshape(G * C, C), lambda a: a.reshape(1, G, C, C)),
        ("sgu_b_s", lambda a: a[0], lambda a: a[None]),
        ("ret_decay_logit", pad_decay, lambda a: a[None, 0:2, 0:H]),
    ]
    res = _adamw_small(ga, gb, gn1, gl, [(to(args[n]), to(args["m_" + n]), to(args["v_" + n])) for n, to, _ in small])
    out["loss"] = res[40][0, 0]
    for i, (n, _, back) in enumerate(small):
        for j, pre in enumerate(("grad_", "delta_", "new_m_", "new_v_")):
            out[pre + n] = back(res[4 * i + j])

    weights = ("ffn1_norm", "ffn1_w_gate", "ffn1_w_up", "ffn1_w_down", "mix_norm", "w_in", "b_in", "sgu_norm_g",
               "sgu_norm_b", "sgu_w_s", "sgu_b_s", "ret_decay_logit", "w_branch_a", "w_branch_b", "w_out", "ffn2_norm",
               "ffn2_w_gate", "ffn2_w_up", "ffn2_w_down", "final_norm")
    return (out["loss"], out["grad_x"], *[out["grad_" + n] for n in weights], *[out["delta_" + n] for n in weights],
            *[out["new_m_" + n] for n in weights], *[out["new_v_" + n] for n in weights])
```

```python
import functools
import math

import jax
import jax.numpy as jnp
from jax import lax
from jax.experimental import pallas as pl
from jax.experimental.pallas import tpu as pltpu

f32 = jnp.float32
CDT = jnp.bfloat16

D = 1024
F = 2816
C = 128
RC = 256
H = 4
DK = 256
G = 4
NDEV = 8
NCHIP = 4
EPS = 1e-6
ROPE_BASE = 10000.0
FT = 256
V7X_VMEM_BYTES = 64 * 1024 * 1024
VMEM_LIMIT = V7X_VMEM_BYTES - 8 * 1024 * 1024

ADAM_LR, ADAM_B1, ADAM_B2, ADAM_EPS, ADAM_WD, ADAM_STEP = 0.001, 0.9, 0.999, 1e-08, 0.01, 10
BC1 = 1.0 - ADAM_B1 ** ADAM_STEP
BC2 = 1.0 - ADAM_B2 ** ADAM_STEP

W_ROWS = dict(ffn1_w_gate=352, ffn1_w_up=352, ffn1_w_down=352, w_in=1024, w_branch_a=128, w_branch_b=128, w_out=128,
              ffn2_w_gate=352, ffn2_w_up=352, ffn2_w_down=352)
W_NAMES = tuple(W_ROWS)
W_TRANSPOSED = ("ffn1_w_gate", "ffn1_w_up", "ffn2_w_gate", "ffn2_w_up")

SLOT_U, SLOT_V, SLOT_GA, SLOT_GB, SLOT_Q, SLOT_K, SLOT_VR, SLOT_GR = range(8)


SEG_OF_SLOT = (0, 1, 6, 7, 2, 3, 4, 5)


def _seg_of_slot(p):
    return jnp.where(p < 2, p, jnp.where(p < 4, p + 4, p - 2))


def _mm(a, b):
    return jnp.dot(a, b, preferred_element_type=f32)


def _mm_nt(a, b):
    return lax.dot_general(a, b, (((1,), (1,)), ((), ())), preferred_element_type=f32)


def _mm_tn(a, b):
    return lax.dot_general(a, b, (((0,), (0,)), ((), ())), preferred_element_type=f32)


def _params(*sem):
    return pltpu.CompilerParams(dimension_semantics=sem, vmem_limit_bytes=VMEM_LIMIT)


def _resident(shape, index_map):
    return pl.BlockSpec(shape, index_map, pipeline_mode=pl.Buffered(1))


def _gelu(x):
    return 0.5 * x * (1.0 + lax.erf(x * (1.0 / math.sqrt(2.0))))


def _gelu_grad(x):
    return 0.5 * (1.0 + lax.erf(x * (1.0 / math.sqrt(2.0)))) + x * jnp.exp(-0.5 * x * x) * (1.0 / math.sqrt(2.0 * math.pi))


def _rms_fwd(x, n):
    r = lax.rsqrt(jnp.mean(x * x, axis=-1, keepdims=True) + EPS)
    xh = x * r
    return r, xh, xh * n


def _rms_bwd(dh, r, xh, n):
    dxh = dh * n
    dx = r * (dxh - xh * jnp.mean(dxh * xh, axis=-1, keepdims=True))
    return dx, jnp.sum(dh * xh, axis=0, keepdims=True)


MESH_ID = pl.DeviceIdType.MESH
_HBM = pl.BlockSpec(memory_space=pltpu.HBM)


def _my_place():
    return lax.axis_index("x"), lax.axis_index("y"), lax.axis_index("c")


def _ici_peers(x, y, c):
    return [((1 - x, y, c), 2 * (1 - x) + y), ((x, 1 - y, c), 2 * x + 1 - y), ((1 - x, 1 - y, c), 2 * (1 - x) + 1 - y)]


class _Hook:
    def __init__(self, operands, out_shapes, n_remote, n_local, start, finish, relay=None):
        self.operands, self.out_shapes = list(operands), list(out_shapes)
        self.n_remote, self.n_local, self.start, self.finish = n_remote, n_local, start, finish
        self.relay = relay or (lambda *a: None)


def _call(body, hooks, operands, *, in_specs, out_specs, out_shape, grid=None, scratch_shapes=(), **kw):
    hooks = tuple(hooks)
    n_in, n_out, n_scr = len(in_specs), len(out_shape), len(scratch_shapes)
    h_ops = [a for h in hooks for a in h.operands]
    h_outs = [s for h in hooks for s in h.out_shapes]
    h_sems = [pltpu.SemaphoreType.DMA((n,)) for h in hooks for n in (h.n_remote, h.n_remote, max(h.n_local, 1))]

    def wrapped(*refs):
        ins, hin = refs[:n_in], refs[n_in:n_in + len(h_ops)]
        o0 = n_in + len(h_ops)
        outs, hout = refs[o0:o0 + n_out], refs[o0 + n_out:o0 + n_out + len(h_outs)]
        s0 = o0 + n_out + len(h_outs)
        scr, hsem = refs[s0:s0 + n_scr], refs[s0 + n_scr:]

        def run(phase):
            ip = op = 0
            for i, h in enumerate(hooks):
                ssem, rsem, lsem = hsem[3 * i:3 * i + 3]

                def rcopy(k, src, dst, dev, ssem=ssem, rsem=rsem):
                    return pltpu.make_async_remote_copy(src_ref=src, dst_ref=dst, send_sem=ssem.at[k], recv_sem=rsem.at[k],
                                                        device_id=dev, device_id_type=MESH_ID)

                def lcopy(k, src, dst, lsem=lsem):
                    return pltpu.make_async_copy(src, dst, lsem.at[k])

                getattr(h, phase)(hin[ip:ip + len(h.operands)], hout[op:op + len(h.out_shapes)], rcopy, lcopy)
                ip += len(h.operands)
                op += len(h.out_shapes)

        def at_edge(phase, last):
            if not hooks:
                return
            if grid is None:
                run(phase)
                return
            cond = None
            for ax, n in enumerate(grid):
                here = pl.program_id(ax) == (n - 1 if last else 0)
                cond = here if cond is None else cond & here
            pl.when(cond)(lambda: run(phase))

        at_edge("start", False)
        at_edge("relay", True)
        body(*ins, *outs, *scr)
        at_edge("finish", True)

    if grid is not None:
        kw["grid"] = grid
    return list(pl.pallas_call(
        wrapped, out_shape=list(out_shape) + h_outs, in_specs=list(in_specs) + [_HBM] * len(h_ops),
        out_specs=list(out_specs) + [_HBM] * len(h_outs), scratch_shapes=list(scratch_shapes) + h_sems, **kw,
    )(*operands, *h_ops))


def _exchange(name, hooks):
    return _call(lambda: None, hooks, [], name=name, in_specs=[], out_specs=[], out_shape=[])


def _rows(ref, start, n):
    return ref.at[pl.ds(start, n), :]


def _ag_hook(shard, early=True):
    rows = shard.shape[0]

    def block(full, dev_index):
        return _rows(full, dev_index * rows, rows)

    def start(ins, outs, rcopy, lcopy):
        x, y, c = _my_place()
        src, dst = ins[0], block(outs[0], 4 * x + 2 * y + c)
        lcopy(0, src, dst).start()
        rcopy(0, src, dst, (x, y, 1 - c)).start()
        for j, (dev, _) in enumerate(_ici_peers(x, y, c)):
            rcopy(1 + j, src, dst, dev).start()

    def relay(ins, outs, rcopy, lcopy):
        x, y, c = _my_place()
        for j, (dev, chip) in enumerate(_ici_peers(x, y, c)):
            blk = block(outs[0], 2 * chip + c)
            rcopy(1 + j, blk, blk, dev).wait_recv()
            rcopy(4 + j, blk, blk, (x, y, 1 - c)).start()

    def finish(ins, outs, rcopy, lcopy):
        if not early:
            relay(ins, outs, rcopy, lcopy)
        x, y, c = _my_place()
        sib = (x, y, 1 - c)
        full = outs[0]
        peers = _ici_peers(x, y, c)
        blk = block(full, 2 * (2 * x + y) + 1 - c)
        rcopy(0, blk, blk, sib).wait_recv()
        for j, (dev, chip) in enumerate(peers):
            blk = block(full, 2 * chip + 1 - c)
            rcopy(4 + j, blk, blk, sib).wait_recv()
        src, dst = ins[0], block(full, 4 * x + 2 * y + c)
        lcopy(0, src, dst).wait()
        rcopy(0, src, dst, sib).wait_send()
        for j, (dev, chip) in enumerate(peers):
            rcopy(1 + j, src, dst, dev).wait_send()
            blk = block(full, 2 * chip + c)
            rcopy(4 + j, blk, blk, sib).wait_send()

    return _Hook([shard], [jax.ShapeDtypeStruct((NDEV * rows, D), shard.dtype)], 7, 1, start, finish, relay if early else None)


def _rs_d2d_hook(gfull):
    rows = gfull.shape[0] // NDEV

    def pairs(g, land):
        x, y, c = _my_place()
        return (x, y, 1 - c), [(k, _rows(g, (2 * k + 1 - c) * rows, rows), land.at[k]) for k in range(NCHIP)]

    def start(ins, outs, rcopy, lcopy):
        sib, cps = pairs(ins[0], outs[0])
        for i, src, dst in cps:
            rcopy(i, src, dst, sib).start()

    def finish(ins, outs, rcopy, lcopy):
        sib, cps = pairs(ins[0], outs[0])
        for i, src, dst in cps:
            rcopy(i, dst, dst, sib).wait_recv()
        for i, src, dst in cps:
            rcopy(i, src, dst, sib).wait_send()

    return _Hook([gfull], [jax.ShapeDtypeStruct((NCHIP, rows, D), gfull.dtype)], NCHIP, 0, start, finish)


def _rs_ici_hook(part):
    def start(ins, outs, rcopy, lcopy):
        x, y, c = _my_place()
        mychip = 2 * x + y
        lcopy(0, ins[0].at[mychip], outs[0].at[mychip]).start()
        for j, (dev, chip) in enumerate(_ici_peers(x, y, c)):
            rcopy(j, ins[0].at[chip], outs[0].at[mychip], dev).start()

    def finish(ins, outs, rcopy, lcopy):
        x, y, c = _my_place()
        mychip = 2 * x + y
        peers = _ici_peers(x, y, c)
        for j, (dev, chip) in enumerate(peers):
            rcopy(j, outs[0].at[chip], outs[0].at[chip], dev).wait_recv()
        for j, (dev, chip) in enumerate(peers):
            rcopy(j, ins[0].at[chip], outs[0].at[mychip], dev).wait_send()
        lcopy(0, ins[0].at[mychip], outs[0].at[mychip]).wait()

    return _Hook([part], [jax.ShapeDtypeStruct(part.shape, part.dtype)], 3, 1, start, finish)


def _small_hook(arrays):
    n = len(arrays)

    def peers():
        x, y, c = _my_place()
        out = []
        for dx in range(2):
            for dy in range(2):
                for dc in range(2):
                    if dx + dy + dc:
                        px, py, pc = (1 - x if dx else x), (1 - y if dy else y), (1 - c if dc else c)
                        out.append(((px, py, pc), 4 * px + 2 * py + pc))
        return 4 * x + 2 * y + c, out

    def start(ins, outs, rcopy, lcopy):
        me, ps = peers()
        for t in range(n):
            lcopy(t, ins[t], outs[t].at[me]).start()
            for i, (dev, _) in enumerate(ps):
                rcopy(n * i + t, ins[t], outs[t].at[me], dev).start()

    def finish(ins, outs, rcopy, lcopy):
        me, ps = peers()
        for t in range(n):
            for i, (dev, peer) in enumerate(ps):
                rcopy(n * i + t, outs[t].at[peer], outs[t].at[peer], dev).wait_recv()
            for i, (dev, _) in enumerate(ps):
                rcopy(n * i + t, ins[t], outs[t].at[me], dev).wait_send()
            lcopy(t, ins[t], outs[t].at[me]).wait()

    return _Hook(arrays, [jax.ShapeDtypeStruct((NDEV,) + a.shape, a.dtype) for a in arrays], 7 * n, n, start, finish)


def _wblock(w):
    return _resident(w.shape, lambda *_: (0, 0))


def _ffn_fwd(name, x, nrm, wg, wu, wd, hooks=(), tm=512):
    S = x.shape[0]

    def body(x_ref, n_ref, wg_ref, wu_ref, wd_ref, y_ref, g_ref, u_ref, a_ref, h_ref, acc_ref):
        xv = x_ref[...]
        _, _, h = _rms_fwd(xv, n_ref[...])
        h = h.astype(CDT)
        h_ref[...] = h
        for ci in range(F // FT):
            sl = slice(ci * FT, (ci + 1) * FT)
            g = _mm_nt(h, wg_ref[sl, :])
            u = _mm_nt(h, wu_ref[sl, :])
            g_ref[:, sl] = g.astype(CDT)
            u_ref[:, sl] = u.astype(CDT)
            a = (g * jax.nn.sigmoid(g) * u).astype(CDT)
            a_ref[:, sl] = a
            o = _mm(a, wd_ref[sl, :])
            if ci == 0:
                acc_ref[...] = o
            else:
                acc_ref[...] += o
        y_ref[...] = xv + 0.5 * acc_ref[...]

    tok = pl.BlockSpec((tm, D), lambda i: (i, 0))
    hid = pl.BlockSpec((tm, F), lambda i: (i, 0))
    hidden = jax.ShapeDtypeStruct((S, F), CDT)
    return _call(
        body, hooks, [x, nrm, wg, wu, wd], name=name, grid=(S // tm,),
        out_shape=[jax.ShapeDtypeStruct((S, D), f32), hidden, hidden, hidden, jax.ShapeDtypeStruct((S, D), CDT)],
        in_specs=[tok, _resident((1, D), lambda i: (0, 0)), _wblock(wg), _wblock(wu), _wblock(wd)],
        out_specs=[tok, hid, hid, hid, tok],
        scratch_shapes=[pltpu.VMEM((tm, D), f32)],
        compiler_params=_params("arbitrary"),
    )


def _proj_fwd(x1, nrm, wfull, b3, cos, sin, hooks=(), tm=512):
    S = x1.shape[0]

    def body(x_ref, n_ref, w_ref, b_ref, cos_ref, sin_ref, p_ref, h_ref):
        _, _, h = _rms_fwd(x_ref[...], n_ref[...])
        h = h.astype(CDT)
        h_ref[...] = h
        for p in range(8):
            seg = SEG_OF_SLOT[p]
            z = _mm(h, w_ref[seg * D:(seg + 1) * D, :]) + b_ref[seg]
            if p in (SLOT_Q, SLOT_K):
                co, si = cos_ref[...], sin_ref[...]
                for hh in range(H):
                    cs = slice(hh * DK, (hh + 1) * DK)
                    zr = _rotate(z[:, cs], co, si)
                    p_ref[p, :, cs] = (zr * K_SCALE if p == SLOT_K else zr).astype(CDT)
            else:
                p_ref[p] = z.astype(CDT)

    tab = pl.BlockSpec((tm, DK // 2), lambda i: (i, 0))
    return _call(
        body, hooks, [x1, nrm, wfull, b3, cos, sin], name="proj_fwd", grid=(S // tm,),
        out_shape=[jax.ShapeDtypeStruct((8, S, D), CDT), jax.ShapeDtypeStruct((S, D), CDT)],
        in_specs=[pl.BlockSpec((tm, D), lambda i: (i, 0)), _resident((1, D), lambda i: (0, 0)),
                  _resident((8 * D, D), lambda i: (0, 0)), _resident((8, 1, D), lambda i: (0, 0, 0)), tab, tab],
        out_specs=[pl.BlockSpec((8, tm, D), lambda i: (0, i, 0)), pl.BlockSpec((tm, D), lambda i: (i, 0))],
        compiler_params=_params("arbitrary"),
    )


def _sgu_norm(va, gn, bn):
    mu = jnp.mean(va, axis=-1, keepdims=True)
    xc = va - mu
    rstd = lax.rsqrt(jnp.mean(xc * xc, axis=-1, keepdims=True) + EPS)
    vhat = xc * rstd
    return rstd, vhat, vhat * gn + bn


def _sgu_fwd(proj, gn, bn, ws, bsc, tm=512):
    S = proj.shape[1]
    GW = D // G

    def body(p_ref, gn_ref, bn_ref, ws_ref, bs_ref, a_ref):
        ua = _gelu(p_ref[0].astype(f32))
        va = _gelu(p_ref[1].astype(f32))
        _, _, vn = _sgu_norm(va, gn_ref[...], bn_ref[...])
        vn = vn.astype(CDT)
        for ch in range(tm // C):
            rs = slice(ch * C, (ch + 1) * C)
            for gi in range(G):
                cs = slice(gi * GW, (gi + 1) * GW)
                s = _mm(ws_ref[gi], vn[rs, cs]) + bs_ref[gi]
                a_ref[rs, cs] = (ua[rs, cs] * s).astype(CDT)

    return pl.pallas_call(
        body, name="sgu_fwd", grid=(S // tm,),
        out_shape=jax.ShapeDtypeStruct((S, D), CDT),
        in_specs=[pl.BlockSpec((2, tm, D), lambda i: (0, i, 0)), _resident((1, D), lambda i: (0, 0)),
                  _resident((1, D), lambda i: (0, 0)), _resident((G, C, C), lambda i: (0, 0, 0)),
                  _resident((G, C, 1), lambda i: (0, 0, 0))],
        out_specs=pl.BlockSpec((tm, D), lambda i: (i, 0)),
        compiler_params=_params("arbitrary"),
    )(proj, gn, bn, ws, bsc)


def _decay_tables(dl_ref):
    lg = jax.nn.log_sigmoid(dl_ref[0:2, :])
    lgf, lgb = lg[0:1, :], lg[1:2, :]
    assert RC <= DK
    ri = lax.broadcasted_iota(jnp.int32, (RC, RC), 0)
    ci = lax.broadcasted_iota(jnp.int32, (RC, RC), 1)
    d = (ri - ci).astype(f32)
    lower = d >= 0
    dmat = jnp.where(lower, jnp.exp(d * lgf[:, :RC]), jnp.exp(-d * lgb[:, :RC]))
    dmat_t = jnp.where(d <= 0, jnp.exp(-d * lgf[:, :RC]), jnp.exp(d * lgb[:, :RC]))
    pos = lax.broadcasted_iota(jnp.int32, (RC, DK), 0).astype(f32)
    t = dict(
        lgf=lgf, lgb=lgb, d=d, lower=lower, dmat=dmat, dmat_t=dmat_t, pos=pos,
        fq=jnp.exp((pos + 1.0) * lgf), fk=jnp.exp((RC - 1.0 - pos) * lgf),
        bq=jnp.exp((RC - pos) * lgb), bk=jnp.exp(pos * lgb),
        lamf=jnp.exp(float(RC) * lgf), lamb=jnp.exp(float(RC) * lgb),
    )
    return t


def _rotate(t, co, si):
    t1, t2 = t[:, :DK // 2], t[:, DK // 2:]
    return jnp.concatenate([t1 * co - t2 * si, t2 * co + t1 * si], axis=-1)


def _unrotate(t, co, si):
    t1, t2 = t[:, :DK // 2], t[:, DK // 2:]
    return jnp.concatenate([t1 * co + t2 * si, t2 * co - t1 * si], axis=-1)


K_SCALE = DK ** -0.5
ROW_TILE = 256


def _ret_fwd(proj, dl, hooks=()):
    S = proj.shape[1]
    NC = S // RC

    def body(q_ref, k_ref, v_ref, g_ref, dl_ref, R_ref, r_ref, sfs_ref, sbs_ref, rb_ref, sf_ref, sb_ref):
        t = _decay_tables(dl_ref)

        def chunk(n):
            rows = pl.ds(pl.multiple_of(n * RC, RC), RC)
            return rows, q_ref[rows, :], k_ref[rows, :], v_ref[rows, :]

        sf_ref[...] = jnp.zeros_like(sf_ref)
        sb_ref[...] = jnp.zeros_like(sb_ref)

        def step(i, carry):
            rows, qn, kn, vn = chunk(i)
            sc = _mm_nt(qn, kn) * t["dmat"]
            out = _mm(sc.astype(CDT), vn)
            sf = sf_ref[...]
            sfb = sf.astype(CDT)
            sfs_ref[i] = sfb
            R_ref[rows, :] = out + _mm((qn.astype(f32) * t["fq"]).astype(CDT), sfb)
            sf_ref[...] = sf * t["lamf"] + _mm_tn((kn.astype(f32) * t["fk"]).astype(CDT), vn)
            m = NC - 1 - i
            rows, qn, kn, vn = chunk(m)
            sb = sb_ref[...]
            sbb = sb.astype(CDT)
            sbs_ref[m] = sbb
            rb_ref[rows, :] = _mm((qn.astype(f32) * t["bq"]).astype(CDT), sbb)
            sb_ref[...] = sb * t["lamb"] + _mm_tn((kn.astype(f32) * t["bk"]).astype(CDT), vn)
            return carry

        lax.fori_loop(0, NC, step, 0)

        def finish(i, carry):
            rs = pl.ds(pl.multiple_of(i * ROW_TILE, ROW_TILE), ROW_TILE)
            R = R_ref[rs, :] + rb_ref[rs, :]
            R_ref[rs, :] = R
            rn = R * lax.rsqrt(jnp.mean(R * R, axis=-1, keepdims=True) + EPS)
            g = g_ref[rs, :].astype(f32)
            r_ref[rs, :] = (rn * g * jax.nn.sigmoid(g)).astype(CDT)
            return carry

        lax.fori_loop(0, S // ROW_TILE, finish, 0)

    def seg(slot):
        return pl.BlockSpec((None, S, DK), lambda h: (slot, 0, h))

    states = jax.ShapeDtypeStruct((H, NC, DK, DK), CDT)
    state_blk = pl.BlockSpec((None, NC, DK, DK), lambda h: (h, 0, 0, 0))
    return _call(
        body, hooks, [proj, proj, proj, proj, dl], name="ret_fwd", grid=(H,),
        out_shape=[jax.ShapeDtypeStruct((S, H * DK), f32), jax.ShapeDtypeStruct((S, H * DK), CDT), states, states],
        in_specs=[seg(SLOT_Q), seg(SLOT_K), seg(SLOT_VR), seg(SLOT_GR), pl.BlockSpec((None, 8, DK), lambda h: (h, 0, 0))],
        out_specs=[pl.BlockSpec((S, DK), lambda h: (0, h)), pl.BlockSpec((S, DK), lambda h: (0, h)), state_blk, state_blk],
        scratch_shapes=[pltpu.VMEM((S, DK), f32), pltpu.VMEM((DK, DK), f32), pltpu.VMEM((DK, DK), f32)],
        compiler_params=_params("arbitrary"),
    )


def _merge_fwd(a, r, proj, x1, wa, wb, wo, hooks=(), tm=512):
    S = x1.shape[0]

    def body(a_ref, r_ref, gt_ref, x_ref, wa_ref, wb_ref, wo_ref, x2_ref, ya_ref, yb_ref):
        ya = _mm(a_ref[...], wa_ref[...])
        yb = _mm(r_ref[...], wb_ref[...])
        ya_ref[...] = ya.astype(CDT)
        yb_ref[...] = yb.astype(CDT)
        mix = jax.nn.sigmoid(gt_ref[0].astype(f32)) * ya + jax.nn.sigmoid(gt_ref[1].astype(f32)) * yb
        x2_ref[...] = x_ref[...] + _mm(mix.astype(CDT), wo_ref[...])

    tok = pl.BlockSpec((tm, D), lambda i: (i, 0))
    return _call(
        body, hooks, [a, r, proj, x1, wa, wb, wo], name="merge_fwd", grid=(S // tm,),
        out_shape=[jax.ShapeDtypeStruct((S, D), f32), jax.ShapeDtypeStruct((S, D), CDT), jax.ShapeDtypeStruct((S, D), CDT)],
        in_specs=[tok, tok, pl.BlockSpec((2, tm, D), lambda i: (SLOT_GA // 2, i, 0)), tok,
                  _wblock(wa), _wblock(wb), _wblock(wo)],
        out_specs=[tok, tok, tok],
        compiler_params=_params("arbitrary"),
    )


def _loss_head(x3, fn, target, tm=512):
    S = x3.shape[0]

    def body(x_ref, n_ref, t_ref, dx_ref, dxh_ref, dn_ref, l_ref):
        n = n_ref[...]
        r, xh, y = _rms_fwd(x_ref[...], n)
        e = y - t_ref[...]
        dy = e * (1.0 / D)
        dx, dn = _rms_bwd(dy, r, xh, n)
        dx_ref[...] = dx
        dxh_ref[...] = (0.5 * dx).astype(CDT)
        part = 0.5 * jnp.sum(jnp.sum(e * e, axis=-1, keepdims=True), axis=0, keepdims=True) * (1.0 / D)

        @pl.when(pl.program_id(0) == 0)
        def _():
            dn_ref[...] = jnp.zeros_like(dn_ref)
            l_ref[...] = jnp.zeros_like(l_ref)

        dn_ref[...] += dn
        l_ref[...] += jnp.broadcast_to(part, l_ref.shape)

    tok = pl.BlockSpec((tm, D), lambda i: (i, 0))
    return pl.pallas_call(
        body, name="loss_head", grid=(S // tm,),
        out_shape=[jax.ShapeDtypeStruct((S, D), f32), jax.ShapeDtypeStruct((S, D), CDT), jax.ShapeDtypeStruct((1, D), f32),
                   jax.ShapeDtypeStruct((8, 128), f32)],
        in_specs=[tok, _resident((1, D), lambda i: (0, 0)), tok],
        out_specs=[tok, tok, pl.BlockSpec((1, D), lambda i: (0, 0)), pl.BlockSpec((8, 128), lambda i: (0, 0))],
        compiler_params=_params("arbitrary"),
    )(x3, fn, target)


def _ffn_bwd_hidden(name, dyh, g, u, wd, hooks=(), tm=512):
    S = dyh.shape[0]

    def body(dyh_ref, g_ref, u_ref, wd_ref, dg_ref, du_ref):
        dyh = dyh_ref[...]
        for ci in range(F // FT):
            sl = slice(ci * FT, (ci + 1) * FT)
            da = _mm_nt(dyh, wd_ref[sl, :])
            gv = g_ref[:, sl].astype(f32)
            uv = u_ref[:, sl].astype(f32)
            s = jax.nn.sigmoid(gv)
            du_ref[:, sl] = (da * (gv * s)).astype(CDT)
            dg_ref[:, sl] = (da * uv * (s * (1.0 + gv * (1.0 - s)))).astype(CDT)

    hid = pl.BlockSpec((tm, F), lambda i: (i, 0))
    hidden = jax.ShapeDtypeStruct((S, F), CDT)
    return _call(
        body, hooks, [dyh, g, u, wd], name=name, grid=(S // tm,), out_shape=[hidden, hidden],
        in_specs=[pl.BlockSpec((tm, D), lambda i: (i, 0)), hid, hid, _wblock(wd)], out_specs=[hid, hid],
        compiler_params=_params("arbitrary"),
    )


def _ffn_bwd_in(name, dy, x, dg, du, nrm, wg, wu, hooks=(), part=None, prev=None, tm=512):
    S = x.shape[0]
    t0, nt = part or (0, S // tm)

    def body(dy_ref, x_ref, dg_ref, du_ref, n_ref, wg_ref, wu_ref, *rest):
        dx_ref, dn_ref, acc_ref = rest[-3:]
        n = n_ref[...]
        r, xh, _ = _rms_fwd(x_ref[...], n)
        for ci in range(F // FT):
            sl = slice(ci * FT, (ci + 1) * FT)
            dh = _mm(dg_ref[:, sl], wg_ref[sl, :]) + _mm(du_ref[:, sl], wu_ref[sl, :])
            if ci == 0:
                acc_ref[...] = dh
            else:
                acc_ref[...] += dh
        dx, dn = _rms_bwd(acc_ref[...], r, xh, n)
        dx_ref[...] = dy_ref[...] + dx

        @pl.when(pl.program_id(0) == 0)
        def _():
            dn_ref[...] = jnp.zeros_like(dn_ref) if prev is None else rest[1][...]

        dn_ref[...] += dn

    tok = pl.BlockSpec((tm, D), lambda i: (t0 + i, 0))
    hid = pl.BlockSpec((tm, F), lambda i: (t0 + i, 0))
    row = pl.BlockSpec((1, D), lambda i: (0, 0))
    in_specs = [tok, tok, hid, hid, _resident((1, D), lambda i: (0, 0)), _wblock(wg), _wblock(wu)]
    operands = [dy, x, dg, du, nrm, wg, wu]
    aliases = {}
    if prev is not None:
        in_specs += [_HBM, row]
        operands += list(prev)
        aliases = {7: 0}
    return _call(
        body, hooks, operands, name=name, grid=(nt,),
        out_shape=[jax.ShapeDtypeStruct((S, D), f32), jax.ShapeDtypeStruct((1, D), f32)],
        in_specs=in_specs, out_specs=[tok, row],
        scratch_shapes=[pltpu.VMEM((tm, D), f32)],
        input_output_aliases=aliases,
        compiler_params=_params("arbitrary"),
    )


TN_ROWS = 512


def _tn(name, xs, ys, block_of, hooks=()):
    S, M = xs.shape
    B = ys.shape[0]
    tr = TN_ROWS if M % TN_ROWS == 0 else M // 2
    assert M % tr == 0 and tr % 128 == 0
    nt = M // tr

    def body(x_ref, y_ref, o_ref):
        o_ref[...] = _mm_tn(x_ref[...], y_ref[...]).astype(CDT)

    return _call(
        body, hooks, [xs, ys], name=name, grid=(B, nt),
        out_shape=[jax.ShapeDtypeStruct((B * M, D), CDT)],
        in_specs=[pl.BlockSpec((S, tr), lambda b, i: (0, i)), pl.BlockSpec((None, S, D), lambda b, i: (b, 0, 0))],
        out_specs=[pl.BlockSpec((tr, D), lambda b, i: (block_of(b) * nt + i, 0))],
        compiler_params=_params("arbitrary", "arbitrary"),
    )


def _wgrad(name, xs, y, hooks=()):
    return _tn(name, xs, y[None], lambda b: 0, hooks)


def _merge_bwd_act(dx2, ya, yb, proj, wa, wb, wo, hooks=(), tm=512):
    S = dx2.shape[0]

    def body(dx_ref, ya_ref, yb_ref, gt_ref, wa_ref, wb_ref, wo_ref,
             dp_ref, da_ref, dr_ref, mix_ref, dxb_ref, dya_ref, dyb_ref):
        dxb = dx_ref[...].astype(CDT)
        dxb_ref[...] = dxb
        dmix = _mm_nt(dxb, wo_ref[...])
        ya = ya_ref[...].astype(f32)
        yb = yb_ref[...].astype(f32)
        sa = jax.nn.sigmoid(gt_ref[0].astype(f32))
        sb = jax.nn.sigmoid(gt_ref[1].astype(f32))
        mix_ref[...] = (sa * ya + sb * yb).astype(CDT)
        dya = (dmix * sa).astype(CDT)
        dyb = (dmix * sb).astype(CDT)
        dya_ref[...] = dya
        dyb_ref[...] = dyb
        dp_ref[0] = (dmix * ya * sa * (1.0 - sa)).astype(CDT)
        dp_ref[1] = (dmix * yb * sb * (1.0 - sb)).astype(CDT)
        da_ref[...] = _mm_nt(dya, wa_ref[...]).astype(CDT)
        dr_ref[...] = _mm_nt(dyb, wb_ref[...]).astype(CDT)

    tok = pl.BlockSpec((tm, D), lambda i: (i, 0))
    gates = pl.BlockSpec((2, tm, D), lambda i: (SLOT_GA // 2, i, 0))
    act = jax.ShapeDtypeStruct((S, D), CDT)
    return _call(
        body, hooks, [dx2, ya, yb, proj, wa, wb, wo], name="merge_bwd_act", grid=(S // tm,),
        out_shape=[jax.ShapeDtypeStruct((8, S, D), CDT), act, act, act, act, act, act],
        in_specs=[tok, tok, tok, gates, _wblock(wa), _wblock(wb), _wblock(wo)],
        out_specs=[gates, tok, tok, tok, tok, tok, tok],
        compiler_params=_params("arbitrary"),
    )


def _sgu_bwd(da, proj, dproj, gn, bn, ws, wst, bsc, hooks=(), tm=512):
    S = proj.shape[1]
    GW = D // G

    def body(da_ref, p_ref, dpin_ref, gn_ref, bn_ref, ws_ref, wst_ref, bs_ref,
             dp_ref, dws_ref, dbs_ref, dgn_ref, dbn_ref, ds_ref, dvn_ref):
        @pl.when(pl.program_id(0) == 0)
        def _():
            dws_ref[...] = jnp.zeros_like(dws_ref)
            dbs_ref[...] = jnp.zeros_like(dbs_ref)
            dgn_ref[...] = jnp.zeros_like(dgn_ref)
            dbn_ref[...] = jnp.zeros_like(dbn_ref)

        pu = p_ref[0].astype(f32)
        pv = p_ref[1].astype(f32)
        ua = _gelu(pu)
        va = _gelu(pv)
        gn = gn_ref[...]
        rstd, vhat, vn = _sgu_norm(va, gn, bn_ref[...])
        vnb = vn.astype(CDT)
        dav = da_ref[...].astype(f32)
        dsb = (dav * ua).astype(CDT)
        ones = jnp.ones((8, GW), CDT)
        for ch in range(tm // C):
            rs = slice(ch * C, (ch + 1) * C)
            for gi in range(G):
                cs = slice(gi * GW, (gi + 1) * GW)
                s = _mm(ws_ref[gi], vnb[rs, cs]) + bs_ref[gi]
                ds_ref[rs, cs] = s
                dsg = dsb[rs, cs]
                dws_ref[gi] += _mm_nt(dsg, vnb[rs, cs])
                dbs_ref[gi] += _mm_nt(ones, dsg)
                dvn_ref[rs, cs] = _mm(wst_ref[gi], dsg)
        dp_ref[0] = (dav * ds_ref[...] * _gelu_grad(pu)).astype(CDT)
        dvn = dvn_ref[...]
        dgn_ref[...] += jnp.sum(dvn * vhat, axis=0, keepdims=True)
        dbn_ref[...] += jnp.sum(dvn, axis=0, keepdims=True)
        dvh = dvn * gn
        dva = rstd * (dvh - jnp.mean(dvh, axis=-1, keepdims=True) - vhat * jnp.mean(dvh * vhat, axis=-1, keepdims=True))
        dp_ref[1] = (dva * _gelu_grad(pv)).astype(CDT)

    uv = pl.BlockSpec((2, tm, D), lambda i: (0, i, 0))
    row = _resident((1, D), lambda i: (0, 0))
    return _call(
        body, hooks, [da, proj, dproj, gn, bn, ws, wst, bsc], name="sgu_bwd", grid=(S // tm,),
        out_shape=[jax.ShapeDtypeStruct(dproj.shape, CDT), jax.ShapeDtypeStruct((G, C, C), f32),
                   jax.ShapeDtypeStruct((G, 8, C), f32), jax.ShapeDtypeStruct((1, D), f32), jax.ShapeDtypeStruct((1, D), f32)],
        in_specs=[pl.BlockSpec((tm, D), lambda i: (i, 0)), uv, _HBM, row, row,
                  _resident((G, C, C), lambda i: (0, 0, 0)), _resident((G, C, C), lambda i: (0, 0, 0)),
                  _resident((G, C, 1), lambda i: (0, 0, 0))],
        out_specs=[uv, pl.BlockSpec((G, C, C), lambda i: (0, 0, 0)), pl.BlockSpec((G, 8, C), lambda i: (0, 0, 0)),
                   pl.BlockSpec((1, D), lambda i: (0, 0)), pl.BlockSpec((1, D), lambda i: (0, 0))],
        scratch_shapes=[pltpu.VMEM((tm, D), f32), pltpu.VMEM((tm, D), f32)],
        input_output_aliases={2: 0},
        compiler_params=_params("arbitrary"),
    )


def _ret_bwd(dr, R, sfs, sbs, proj, dproj, cos, sin, dl, hooks=()):
    S = proj.shape[1]
    NC = S // RC
    assert NC % 2 == 0

    def body(dr_ref, R_ref, sf_ref, sb_ref, q_ref, k_ref, v_ref, g_ref, dpin_ref, cos_ref, sin_ref, dl_ref,
             dp_ref, dd_ref, dR_ref, gb_ref, gf_ref, acc_ref):
        t = _decay_tables(dl_ref)

        def gate_norm_bwd(i, carry):
            rs = pl.ds(pl.multiple_of(i * ROW_TILE, ROW_TILE), ROW_TILE)
            Rv = R_ref[rs, :]
            rstd = lax.rsqrt(jnp.mean(Rv * Rv, axis=-1, keepdims=True) + EPS)
            rn = Rv * rstd
            gv = g_ref[rs, :].astype(f32)
            s = jax.nn.sigmoid(gv)
            drv = dr_ref[rs, :].astype(f32)
            dp_ref[3, rs, :] = (drv * rn * (s * (1.0 + gv * (1.0 - s)))).astype(CDT)
            drn = drv * gv * s
            dR_ref[rs, :] = (rstd * (drn - rn * jnp.mean(drn * rn, axis=-1, keepdims=True))).astype(CDT)
            return carry

        lax.fori_loop(0, S // ROW_TILE, gate_norm_bwd, 0)

        def chunk(n):
            rows = pl.ds(pl.multiple_of(n * RC, RC), RC)
            return rows, q_ref[rows, :], k_ref[rows, :], v_ref[rows, :], dR_ref[rows, :]

        def emit_kv(rows, dk, dv, final):
            if not final:
                dp_ref[1, rows, :] = dk.astype(CDT)
                dp_ref[2, rows, :] = dv.astype(CDT)
            else:
                co, si = cos_ref[rows, :], sin_ref[rows, :]
                dk = dp_ref[1, rows, :].astype(f32) + dk
                dp_ref[1, rows, :] = (_unrotate(dk, co, si) * K_SCALE).astype(CDT)
                dp_ref[2, rows, :] = (dp_ref[2, rows, :].astype(f32) + dv).astype(CDT)

        gb_ref[...] = jnp.zeros_like(gb_ref)
        gf_ref[...] = jnp.zeros_like(gf_ref)
        acc_ref[...] = jnp.zeros_like(acc_ref)
        dpos = jnp.abs(t["d"])

        def ascend(n, final):
            rows, qn, kn, vn, dRn = chunk(n)
            qf, kf = qn.astype(f32), kn.astype(f32)
            sc = _mm_nt(qn, kn)
            dA = _mm_nt(dRn, vn)
            w = sc * dA * t["dmat"] * dpos
            lgf_part = jnp.sum(jnp.where(t["lower"], w, 0.0), axis=0, keepdims=True)
            lgb_part = jnp.sum(jnp.where(t["lower"], 0.0, w), axis=0, keepdims=True)
            dsc = (dA * t["dmat"]).astype(CDT)
            dq = _mm(dsc, kn)
            scT = (_mm_nt(kn, qn) * t["dmat_t"]).astype(CDT)
            dscT = (_mm_nt(vn, dRn) * t["dmat_t"]).astype(CDT)
            dk = _mm(dscT, qn)
            dv = _mm(scT, dRn)
            sfb = sf_ref[n]
            sbb = sb_ref[n]
            qdf = qf * t["fq"]
            dqdf = _mm_nt(dRn, sfb)
            dq += dqdf * t["fq"]
            lgf_row = jnp.sum(qdf * dqdf * (t["pos"] + 1.0), axis=0, keepdims=True)
            qdb = qf * t["bq"]
            dqdb = _mm_nt(dRn, sbb)
            dq += dqdb * t["bq"]
            lgb_row = jnp.sum(qdb * dqdb * (RC - t["pos"]), axis=0, keepdims=True)
            gb = gb_ref[...]
            gbb = gb.astype(CDT)
            kdb = kf * t["bk"]
            dkdb = _mm_nt(vn, gbb)
            dk += dkdb * t["bk"]
            dv += _mm(kdb.astype(CDT), gbb)
            lgb_row += jnp.sum(kdb * dkdb * t["pos"], axis=0, keepdims=True)
            lgb_row += float(RC) * t["lamb"] * jnp.sum(gb * sbb.astype(f32), axis=0, keepdims=True)
            co, si = cos_ref[rows, :], sin_ref[rows, :]
            dp_ref[0, rows, :] = _unrotate(dq, co, si).astype(CDT)
            emit_kv(rows, dk, dv, final)
            acc_ref[0:1, :] += lgf_row + lgf_part
            acc_ref[1:2, :] += lgb_row + lgb_part
            gb_ref[...] = gb * t["lamb"] + _mm_tn(qdb.astype(CDT), dRn)

        def descend(n, final):
            rows, qn, kn, vn, dRn = chunk(n)
            gf = gf_ref[...]
            gfb = gf.astype(CDT)
            kdf = kn.astype(f32) * t["fk"]
            dkdf = _mm_nt(vn, gfb)
            lgf_row = jnp.sum(kdf * dkdf * (RC - 1.0 - t["pos"]), axis=0, keepdims=True)
            lgf_row += float(RC) * t["lamf"] * jnp.sum(gf * sf_ref[n].astype(f32), axis=0, keepdims=True)
            acc_ref[0:1, :] += lgf_row
            emit_kv(rows, dkdf * t["fk"], _mm(kdf.astype(CDT), gfb), final)
            gf_ref[...] = gf * t["lamf"] + _mm_tn((qn.astype(f32) * t["fq"]).astype(CDT), dRn)

        def sweep(final):
            def step(i, carry):
                ascend(i, final)
                descend(NC - 1 - i, final)
                return carry
            return step

        lax.fori_loop(0, NC // 2, sweep(False), 0)
        lax.fori_loop(NC // 2, NC, sweep(True), 0)
        dlg = jnp.sum(acc_ref[...], axis=1, keepdims=True)
        dlogit = dlg * jax.nn.sigmoid(-dl_ref[:, 0:1])
        lane = lax.broadcasted_iota(jnp.int32, (8, 128), 1)
        dd_ref[...] = jnp.where(lane == pl.program_id(0), jnp.broadcast_to(dlogit, (8, 128)), 0.0)

    def seg(slot):
        return pl.BlockSpec((None, S, DK), lambda h: (slot, 0, h))

    head = pl.BlockSpec((S, DK), lambda h: (0, h))
    states = pl.BlockSpec((None, NC, DK, DK), lambda h: (h, 0, 0, 0))
    return _call(
        body, hooks, [dr, R, sfs, sbs, proj, proj, proj, proj, dproj, cos, sin, dl], name="ret_bwd", grid=(H,),
        out_shape=[jax.ShapeDtypeStruct(dproj.shape, CDT), jax.ShapeDtypeStruct((H, 8, 128), f32)],
        in_specs=[head, head, states, states, seg(SLOT_Q), seg(SLOT_K), seg(SLOT_VR), seg(SLOT_GR), _HBM,
                  _resident((S, DK // 2), lambda h: (0, 0)), _resident((S, DK // 2), lambda h: (0, 0)),
                  pl.BlockSpec((None, 8, DK), lambda h: (h, 0, 0))],
        out_specs=[pl.BlockSpec((4, S, DK), lambda h: (1, 0, h), pipeline_mode=pl.Buffered(1)),
                   pl.BlockSpec((None, 8, 128), lambda h: (h, 0, 0))],
        scratch_shapes=[pltpu.VMEM((S, DK), CDT),
                        pltpu.VMEM((DK, DK), f32), pltpu.VMEM((DK, DK), f32), pltpu.VMEM((8, DK), f32)],
        input_output_aliases={8: 0},
        compiler_params=_params("arbitrary"),
    )


def _proj_bwd_act(dproj, dx2, x1, nrm, wfull, hooks=(), tm=512):
    S = x1.shape[0]

    def body(dp_ref, dx2_ref, x_ref, n_ref, w_ref, dx_ref, dxh_ref, dn_ref, db_ref, acc_ref):
        @pl.when(pl.program_id(0) == 0)
        def _():
            dn_ref[...] = jnp.zeros_like(dn_ref)
            db_ref[...] = jnp.zeros_like(db_ref)

        for p in range(8):
            seg = SEG_OF_SLOT[p]
            dp = dp_ref[p]
            db_ref[seg] += jnp.sum(dp.astype(f32), axis=0, keepdims=True)
            dh = _mm_nt(dp, w_ref[seg * D:(seg + 1) * D, :])
            if p == 0:
                acc_ref[...] = dh
            else:
                acc_ref[...] += dh
        n = n_ref[...]
        r, xh, _ = _rms_fwd(x_ref[...], n)
        dx, dn = _rms_bwd(acc_ref[...], r, xh, n)
        dx = dx2_ref[...] + dx
        dx_ref[...] = dx
        dxh_ref[...] = (0.5 * dx).astype(CDT)
        dn_ref[...] += dn

    tok = pl.BlockSpec((tm, D), lambda i: (i, 0))
    return _call(
        body, hooks, [dproj, dx2, x1, nrm, wfull], name="proj_bwd_act", grid=(S // tm,),
        out_shape=[jax.ShapeDtypeStruct((S, D), f32), jax.ShapeDtypeStruct((S, D), CDT), jax.ShapeDtypeStruct((1, D), f32),
                   jax.ShapeDtypeStruct((8, 1, D), f32)],
        in_specs=[pl.BlockSpec((8, tm, D), lambda i: (0, i, 0)), tok, tok, _resident((1, D), lambda i: (0, 0)),
                  _resident((8 * D, D), lambda i: (0, 0))],
        out_specs=[tok, tok, pl.BlockSpec((1, D), lambda i: (0, 0)), pl.BlockSpec((8, 1, D), lambda i: (0, 0, 0))],
        scratch_shapes=[pltpu.VMEM((tm, D), f32)],
        compiler_params=_params("arbitrary"),
    )


def _rs_sum(name, gfulls, lands, my_c):
    n = len(gfulls)
    rows = gfulls[0].shape[0] // NDEV
    assert all(g.shape[0] == NDEV * rows for g in gfulls)

    def body(c_ref, *refs):
        for g_ref, l_ref, o_ref in zip(refs[:n], refs[n:2 * n], refs[2 * n:]):
            o_ref[...] = (g_ref[...].astype(f32) + l_ref[...].astype(f32)).astype(CDT)

    slot = pl.BlockSpec((None, rows, D), lambda k, c: (k, 0, 0))
    return pl.pallas_call(
        body, name=name,
        grid_spec=pltpu.PrefetchScalarGridSpec(
            num_scalar_prefetch=1, grid=(NCHIP,),
            in_specs=[pl.BlockSpec((rows, D), lambda k, c: (2 * k + c[0], 0))] * n + [slot] * n,
            out_specs=[slot] * n),
        out_shape=[jax.ShapeDtypeStruct((NCHIP, rows, D), CDT)] * n,
        compiler_params=_params("arbitrary"),
    )(my_c, *gfulls, *lands)


def _adamw_math(g, w, m, v):
    m2 = ADAM_B1 * m + (1.0 - ADAM_B1) * g
    v2 = ADAM_B2 * v + (1.0 - ADAM_B2) * (g * g)
    delta = -ADAM_LR * ((m2 / BC1) / (jnp.sqrt(v2 / BC2) + ADAM_EPS) + ADAM_WD * w)
    return delta, m2, v2


def _adamw_big(name, landed, w, m, v, hooks=()):
    rows = w.shape[0]
    tr = min(rows, 256) if rows % 256 == 0 else rows

    def body(l_ref, w_ref, m_ref, v_ref, g_ref, d_ref, m2_ref, v2_ref):
        g = l_ref[0].astype(f32)
        for k in range(1, NCHIP):
            g = g + l_ref[k].astype(f32)
        g_ref[...] = g
        d_ref[...], m2_ref[...], v2_ref[...] = _adamw_math(g, w_ref[...], m_ref[...], v_ref[...])

    blk = pl.BlockSpec((tr, D), lambda i: (i, 0))
    o = jax.ShapeDtypeStruct((rows, D), f32)
    return _call(
        body, hooks, [landed, w, m, v], name=name, grid=(rows // tr,), out_shape=[o, o, o, o],
        in_specs=[pl.BlockSpec((NCHIP, tr, D), lambda i: (0, i, 0)), blk, blk, blk],
        out_specs=[blk, blk, blk, blk],
        compiler_params=_params("arbitrary"),
    )


ROW_FFN1_NORM, ROW_MIX_NORM, ROW_SGU_G, ROW_SGU_B, ROW_FFN2_NORM, ROW_FINAL_NORM, ROW_B_IN = 0, 1, 2, 3, 4, 5, 8
ROW_WS, ROW_BS, ROW_DECAY = 0, G * C, G * C + G * 8


def _adamw_small(ga, gb, gn1, gl, params):
    def body(ga_ref, gb_ref, gn1_ref, gl_ref, *refs):
        ins, outs = refs[:30], refs[30:]

        def total(ref, r0, n):
            g = ref[0, r0:r0 + n, :]
            for j in range(1, NDEV):
                g = g + ref[j, r0:r0 + n, :]
            return g

        def apply(i, g, rows=slice(None)):
            w, m, v = ins[3 * i][rows, :], ins[3 * i + 1][rows, :], ins[3 * i + 2][rows, :]
            outs[4 * i][rows, :] = g
            outs[4 * i + 1][rows, :], outs[4 * i + 2][rows, :], outs[4 * i + 3][rows, :] = _adamw_math(g, w, m, v)

        outs[40][...] = total(gl_ref, 0, 8)
        apply(0, total(gn1_ref, 0, 1))
        for i, r in enumerate((ROW_FFN1_NORM, ROW_MIX_NORM, ROW_SGU_G, ROW_SGU_B, ROW_FFN2_NORM, ROW_FINAL_NORM)):
            if i:
                apply(i, total(ga_ref, r, 1))
        apply(6, total(ga_ref, ROW_B_IN, 8))
        apply(7, total(gb_ref, ROW_WS, G * C))
        for gi in range(G):
            apply(8, total(gb_ref, ROW_BS + 8 * gi, 1), slice(gi, gi + 1))
        dec = total(gb_ref, ROW_DECAY, 8)
        for hh in range(1, H):
            dec = dec + total(gb_ref, ROW_DECAY + 8 * hh, 8)
        apply(9, dec)

    flat = [a for p in params for a in p]
    out_shape = [jax.ShapeDtypeStruct(p[0].shape, f32) for p in params for _ in range(4)]
    out_shape.append(jax.ShapeDtypeStruct((8, 128), f32))
    vm = pl.BlockSpec(memory_space=pltpu.VMEM)
    return pl.pallas_call(
        body, name="adamw_small", out_shape=out_shape,
        in_specs=[vm] * (4 + len(flat)), out_specs=[vm] * len(out_shape),
        compiler_params=pltpu.CompilerParams(vmem_limit_bytes=VMEM_LIMIT),
    )(ga, gb, gn1, gl, *flat)


def kernel(x, ffn1_norm, ffn1_w_gate, ffn1_w_up, ffn1_w_down, mix_norm, w_in, b_in, sgu_norm_g, sgu_norm_b, sgu_w_s, sgu_b_s, ret_decay_logit, w_branch_a, w_branch_b, w_out, ffn2_norm, ffn2_w_gate, ffn2_w_up, ffn2_w_down, final_norm, loss_target, m_ffn1_norm, m_ffn1_w_gate, m_ffn1_w_up, m_ffn1_w_down, m_mix_norm, m_w_in, m_b_in, m_sgu_norm_g, m_sgu_norm_b, m_sgu_w_s, m_sgu_b_s, m_ret_decay_logit, m_w_branch_a, m_w_branch_b, m_w_out, m_ffn2_norm, m_ffn2_w_gate, m_ffn2_w_up, m_ffn2_w_down, m_final_norm, v_ffn1_norm, v_ffn1_w_gate, v_ffn1_w_up, v_ffn1_w_down, v_mix_norm, v_w_in, v_b_in, v_sgu_norm_g, v_sgu_norm_b, v_sgu_w_s, v_sgu_b_s, v_ret_decay_logit, v_w_branch_a, v_w_branch_b, v_w_out, v_ffn2_norm, v_ffn2_w_gate, v_ffn2_w_up, v_ffn2_w_down, v_final_norm):
    args = dict(locals())
    S = x.shape[1]
    xs = x[0]
    target = loss_target[0]

    def buf_layout(name, a):
        a = a[0]
        return a.T if name in W_TRANSPOSED else a

    sh = {n: buf_layout(n, args[n]).astype(CDT) for n in W_NAMES}
    wf = {}

    b3 = b_in.reshape(8, 1, D)
    ws = sgu_w_s[0].astype(CDT)
    wst = jnp.swapaxes(sgu_w_s[0], 1, 2).astype(CDT)
    bsc = sgu_b_s[0].reshape(G, C, 1)
    dl = jnp.zeros((H, 8, DK), f32).at[:, 0:2, :].set(jnp.broadcast_to(ret_decay_logit[0].T[:, :, None], (H, 2, DK)))
    theta = ROPE_BASE ** (-jnp.arange(0, DK, 2, dtype=f32) / DK)
    ang = jnp.arange(S, dtype=f32)[:, None] * theta[None, :]
    cos, sin = jnp.cos(ang), jnp.sin(ang)
    fnorm = final_norm.reshape(1, D)

    f1 = ("ffn1_w_gate", "ffn1_w_up", "ffn1_w_down")
    f2 = ("ffn2_w_gate", "ffn2_w_up", "ffn2_w_down")
    br = ("w_branch_a", "w_branch_b", "w_out")
    wf[f1[0]], wf[f1[1]], wf[f1[2]] = _exchange("ag_ffn1", [_ag_hook(sh[n]) for n in f1])
    x1, g1, u1, a1, hf1, wf["w_in"] = _ffn_fwd("ffn1_fwd", xs, ffn1_norm, *[wf[n] for n in f1],
                                               [_ag_hook(sh["w_in"], early=False)])
    proj, h2, wf[br[0]], wf[br[1]], wf[br[2]], wf[f2[0]] = _proj_fwd(
        x1, mix_norm, wf["w_in"], b3, cos, sin, [_ag_hook(sh[n]) for n in br + f2[:1]])
    a = _sgu_fwd(proj, sgu_norm_g, sgu_norm_b, ws, bsc)
    R, r, sfs, sbs, wf[f2[1]] = _ret_fwd(proj, dl, [_ag_hook(sh[f2[1]])])
    x2, ya, yb, wf[f2[2]] = _merge_fwd(a, r, proj, x1, *[wf[n] for n in br], [_ag_hook(sh[f2[2]])])
    x3, g2, u2, a2, hf2 = _ffn_fwd("ffn2_fwd", x2, ffn2_norm, *[wf[n] for n in f2])
    dx3, dyh2, d_final, loss_part = _loss_head(x3, fnorm, target)

    my_c = lax.axis_index("c").astype(jnp.int32).reshape(1)
    gw, landed = {}, {}

    def d2d(*names):
        return [_rs_d2d_hook(gw[n]) for n in names]

    def ici(parts):
        return [_rs_ici_hook(p) for p in parts]

    def rs_sum(names, sibs):
        return list(_rs_sum("rs_sum_" + names[0], [gw[n] for n in names], list(sibs), my_c))

    def ffn_bwd(tag, names, dy, dyh, x, g, u, a, h, nrm, eager, on_first, on_hidden):
        wg, wu, wd = names
        gw[wd], *first = _wgrad(tag + "_wd_grad", a, dyh, on_first)
        dg, du, sib, *hidden = _ffn_bwd_hidden(tag + "_bwd_hidden", dyh, g, u, wf[wd], d2d(wd) + on_hidden)
        (part_d,) = rs_sum([wd], [sib])
        gw[wg], *got = _wgrad(tag + "_wg_grad", dg, h, ici([part_d]) if eager else [])
        landed.update(zip([wd], got))
        gw[wu], sib = _wgrad(tag + "_wu_grad", du, h, d2d(wg))
        (part_g,) = rs_sum([wg], [sib])
        args_in = (dy, x, dg, du, nrm, wf[wg], wf[wu])
        if not eager:
            dx, dn, sib = _ffn_bwd_in(tag + "_bwd_in", *args_in, d2d(wu))
            return dx, dn, [part_d, part_g] + rs_sum([wu], [sib]), first, hidden
        half = x.shape[0] // 512 // 2
        dx, dn, landed[wg], sib = _ffn_bwd_in(tag + "_bwd_in_a", *args_in, ici([part_g]) + d2d(wu), part=(0, half))
        dx, dn, landed[wu] = _ffn_bwd_in(tag + "_bwd_in_b", *args_in, ici(rs_sum([wu], [sib])), part=(half, half),
                                         prev=(dx, dn))
        return dx, dn, [], first, hidden

    dx2, d_ffn2n, parts_f2, _, _ = ffn_bwd("ffn2", f2, dx3, dyh2, x2, g2, u2, a2, hf2, ffn2_norm, False, [], [])
    dproj, da, dr, mix, dx2b, dya, dyb = _merge_bwd_act(dx2, ya, yb, proj, *[wf[n] for n in br])
    dproj, d_ws, d_bs, d_gn, d_bn = _sgu_bwd(da, proj, dproj, sgu_norm_g, sgu_norm_b, ws, wst, bsc)
    small_sgu = jnp.concatenate([d_ws.reshape(G * C, C), d_bs.reshape(G * 8, C)], axis=0)
    dproj, d_dec, *got, g_sgu, gl = _ret_bwd(dr, R, sfs, sbs, proj, dproj, cos, sin, dl,
                                             ici(parts_f2) + [_small_hook([small_sgu, loss_part])])
    landed.update(zip((f2[2], f2[0], f2[1]), got))
    (gw["w_in"],) = _tn("win_grad", h2, dproj, _seg_of_slot)
    gw["w_out"], sib_win = _wgrad("wo_grad", mix, dx2b, d2d("w_in"))
    (part_win,) = rs_sum(["w_in"], [sib_win])
    (gw["w_branch_a"],) = _wgrad("wa_grad", a, dya)
    (gw["w_branch_b"],) = _wgrad("wb_grad", r, dyb)
    dx1, dyh1, d_mixn, d_bin, landed["w_in"], *sib = _proj_bwd_act(dproj, dx2, x1, mix_norm, wf["w_in"],
                                                                   ici([part_win]) + d2d(*br))
    part_br = rs_sum(br, sib)
    small_a = jnp.concatenate([jnp.zeros((1, D), f32), d_mixn, d_gn, d_bn, d_ffn2n, d_final, jnp.zeros((2, D), f32),
                               d_bin.reshape(8, D)], axis=0)
    dxs, d_ffn1n, _, (ga, g_dec), got = ffn_bwd("ffn1", f1, dx1, dyh1, xs, g1, u1, a1, hf1, ffn1_norm, True,
                                               [_small_hook([small_a, d_dec.reshape(H * 8, 128)])], ici(part_br))
    landed.update(zip(br, got))
    gb = jnp.concatenate([g_sgu, g_dec], axis=1)
    (gn1,) = _exchange("ag_ffn1_norm", [_small_hook([d_ffn1n])])

    out = {"grad_x": dxs[None]}

    def native(name, a):
        a = a.T if name in W_TRANSPOSED else a
        return a[None]

    for n in W_NAMES:
        res = _adamw_big("adamw_" + n, landed[n], buf_layout(n, args[n]), buf_layout(n, args["m_" + n]),
                         buf_layout(n, args["v_" + n]))
        for pre, val in zip(("grad_", "delta_", "new_m_", "new_v_"), res):
            out[pre + n] = native(n, val)

    def pad_decay(a):
        return jnp.zeros((8, 128), f32).at[0:2, 0:H].set(a[0])

    small = [
        ("ffn1_norm", lambda a: a, lambda a: a), ("mix_norm", lambda a: a, lambda a: a),
        ("sgu_norm_g", lambda a: a, lambda a: a), ("sgu_norm_b", lambda a: a, lambda a: a),
        ("ffn2_norm", lambda a: a, lambda a: a),
        ("final_norm", lambda a: a.reshape(1, D), lambda a: a.reshape(D)),
        ("b_in", lambda a: a.reshape(8, D), lambda a: a.reshape(1, 8 * D)),
        ("sgu_w_s", lambda a: a.reshape(G * C, C), lambda a: a.reshape(1, G, C, C)),
        ("sgu_b_s", lambda a: a[0], lambda a: a[None]),
        ("ret_decay_logit", pad_decay, lambda a: a[None, 0:2, 0:H]),
    ]
    res = _adamw_small(ga, gb, gn1, gl, [(to(args[n]), to(args["m_" + n]), to(args["v_" + n])) for n, to, _ in small])
    out["loss"] = res[40][0, 0]
    for i, (n, _, back) in enumerate(small):
        for j, pre in enumerate(("grad_", "delta_", "new_m_", "new_v_")):
            out[pre + n] = back(res[4 * i + j])

    weights = ("ffn1_norm", "ffn1_w_gate", "ffn1_w_up", "ffn1_w_down", "mix_norm", "w_in", "b_in", "sgu_norm_g",
               "sgu_norm_b", "sgu_w_s", "sgu_b_s", "ret_decay_logit", "w_branch_a", "w_branch_b", "w_out", "ffn2_norm",
               "ffn2_w_gate", "ffn2_w_up", "ffn2_w_down", "final_norm")
    return (out["loss"], out["grad_x"], *[out["grad_" + n] for n in weights], *[out["delta_" + n] for n in weights],
            *[out["new_m_" + n] for n in weights], *[out["new_v_" + n] for n in weights])
```

```python
import functools
import math

import jax
import jax.numpy as jnp
from jax import lax
from jax.experimental import pallas as pl
from jax.experimental.pallas import tpu as pltpu
from jax.experimental.pallas import tpu_sc as plsc

f32 = jnp.float32
CDT = jnp.bfloat16

D = 1024
F = 2816
C = 128
RC = 256
H = 4
DK = 256
G = 4
NDEV = 8
NCHIP = 4
EPS = 1e-6
ROPE_BASE = 10000.0
FT = 256
V7X_VMEM_BYTES = 64 * 1024 * 1024
VMEM_LIMIT = V7X_VMEM_BYTES - 8 * 1024 * 1024

ADAM_LR, ADAM_B1, ADAM_B2, ADAM_EPS, ADAM_WD, ADAM_STEP = 0.001, 0.9, 0.999, 1e-08, 0.01, 10
BC1 = 1.0 - ADAM_B1 ** ADAM_STEP
BC2 = 1.0 - ADAM_B2 ** ADAM_STEP

W_ROWS = dict(ffn1_w_gate=352, ffn1_w_up=352, ffn1_w_down=352, w_in=1024, w_branch_a=128, w_branch_b=128, w_out=128,
              ffn2_w_gate=352, ffn2_w_up=352, ffn2_w_down=352)
W_NAMES = tuple(W_ROWS)
W_TRANSPOSED = ("ffn1_w_gate", "ffn1_w_up", "ffn2_w_gate", "ffn2_w_up")

SLOT_U, SLOT_V, SLOT_GA, SLOT_GB, SLOT_Q, SLOT_K, SLOT_VR, SLOT_GR = range(8)


SEG_OF_SLOT = (0, 1, 6, 7, 2, 3, 4, 5)


def _seg_of_slot(p):
    return jnp.where(p < 2, p, jnp.where(p < 4, p + 4, p - 2))


def _mm(a, b):
    return jnp.dot(a, b, preferred_element_type=f32)


def _mm_nt(a, b):
    return lax.dot_general(a, b, (((1,), (1,)), ((), ())), preferred_element_type=f32)


def _mm_tn(a, b):
    return lax.dot_general(a, b, (((0,), (0,)), ((), ())), preferred_element_type=f32)


def _params(*sem):
    return pltpu.CompilerParams(dimension_semantics=sem, vmem_limit_bytes=VMEM_LIMIT)


def _resident(shape, index_map):
    return pl.BlockSpec(shape, index_map, pipeline_mode=pl.Buffered(1))


def _gelu(x):
    return 0.5 * x * (1.0 + lax.erf(x * (1.0 / math.sqrt(2.0))))


def _gelu_grad(x):
    return 0.5 * (1.0 + lax.erf(x * (1.0 / math.sqrt(2.0)))) + x * jnp.exp(-0.5 * x * x) * (1.0 / math.sqrt(2.0 * math.pi))


def _rms_fwd(x, n):
    r = lax.rsqrt(jnp.mean(x * x, axis=-1, keepdims=True) + EPS)
    xh = x * r
    return r, xh, xh * n


def _rms_bwd(dh, r, xh, n):
    dxh = dh * n
    dx = r * (dxh - xh * jnp.mean(dxh * xh, axis=-1, keepdims=True))
    return dx, jnp.sum(dh * xh, axis=0, keepdims=True)


MESH_ID = pl.DeviceIdType.MESH
_HBM = pl.BlockSpec(memory_space=pltpu.HBM)


def _my_place():
    return lax.axis_index("x"), lax.axis_index("y"), lax.axis_index("c")


def _ici_peers(x, y, c):
    return [((1 - x, y, c), 2 * (1 - x) + y), ((x, 1 - y, c), 2 * x + 1 - y), ((1 - x, 1 - y, c), 2 * (1 - x) + 1 - y)]


class _Hook:
    def __init__(self, operands, out_shapes, n_remote, n_local, start, finish, relay=None):
        self.operands, self.out_shapes = list(operands), list(out_shapes)
        self.n_remote, self.n_local, self.start, self.finish = n_remote, n_local, start, finish
        self.relay = relay or (lambda *a: None)


def _call(body, hooks, operands, *, in_specs, out_specs, out_shape, grid=None, scratch_shapes=(), **kw):
    hooks = tuple(hooks)
    n_in, n_out, n_scr = len(in_specs), len(out_shape), len(scratch_shapes)
    h_ops = [a for h in hooks for a in h.operands]
    h_outs = [s for h in hooks for s in h.out_shapes]
    h_sems = [pltpu.SemaphoreType.DMA((n,)) for h in hooks for n in (h.n_remote, h.n_remote, max(h.n_local, 1))]

    def wrapped(*refs):
        ins, hin = refs[:n_in], refs[n_in:n_in + len(h_ops)]
        o0 = n_in + len(h_ops)
        outs, hout = refs[o0:o0 + n_out], refs[o0 + n_out:o0 + n_out + len(h_outs)]
        s0 = o0 + n_out + len(h_outs)
        scr, hsem = refs[s0:s0 + n_scr], refs[s0 + n_scr:]

        def run(phase):
            ip = op = 0
            for i, h in enumerate(hooks):
                ssem, rsem, lsem = hsem[3 * i:3 * i + 3]

                def rcopy(k, src, dst, dev, ssem=ssem, rsem=rsem):
                    return pltpu.make_async_remote_copy(src_ref=src, dst_ref=dst, send_sem=ssem.at[k], recv_sem=rsem.at[k],
                                                        device_id=dev, device_id_type=MESH_ID)

                def lcopy(k, src, dst, lsem=lsem):
                    return pltpu.make_async_copy(src, dst, lsem.at[k])

                getattr(h, phase)(hin[ip:ip + len(h.operands)], hout[op:op + len(h.out_shapes)], rcopy, lcopy)
                ip += len(h.operands)
                op += len(h.out_shapes)

        def at_edge(phase, last):
            if not hooks:
                return
            if grid is None:
                run(phase)
                return
            cond = None
            for ax, n in enumerate(grid):
                here = pl.program_id(ax) == (n - 1 if last else 0)
                cond = here if cond is None else cond & here
            pl.when(cond)(lambda: run(phase))

        at_edge("start", False)
        at_edge("relay", True)
        body(*ins, *outs, *scr)
        at_edge("finish", True)

    if grid is not None:
        kw["grid"] = grid
    return list(pl.pallas_call(
        wrapped, out_shape=list(out_shape) + h_outs, in_specs=list(in_specs) + [_HBM] * len(h_ops),
        out_specs=list(out_specs) + [_HBM] * len(h_outs), scratch_shapes=list(scratch_shapes) + h_sems, **kw,
    )(*operands, *h_ops))


def _exchange(name, hooks):
    return _call(lambda: None, hooks, [], name=name, in_specs=[], out_specs=[], out_shape=[])


def _rows(ref, start, n):
    return ref.at[pl.ds(start, n), :]


def _ag_hook(shard, early=True):
    rows = shard.shape[0]

    def block(full, dev_index):
        return _rows(full, dev_index * rows, rows)

    def start(ins, outs, rcopy, lcopy):
        x, y, c = _my_place()
        src, dst = ins[0], block(outs[0], 4 * x + 2 * y + c)
        lcopy(0, src, dst).start()
        rcopy(0, src, dst, (x, y, 1 - c)).start()
        for j, (dev, _) in enumerate(_ici_peers(x, y, c)):
            rcopy(1 + j, src, dst, dev).start()

    def relay(ins, outs, rcopy, lcopy):
        x, y, c = _my_place()
        for j, (dev, chip) in enumerate(_ici_peers(x, y, c)):
            blk = block(outs[0], 2 * chip + c)
            rcopy(1 + j, blk, blk, dev).wait_recv()
            rcopy(4 + j, blk, blk, (x, y, 1 - c)).start()

    def finish(ins, outs, rcopy, lcopy):
        if not early:
            relay(ins, outs, rcopy, lcopy)
        x, y, c = _my_place()
        sib = (x, y, 1 - c)
        full = outs[0]
        peers = _ici_peers(x, y, c)
        blk = block(full, 2 * (2 * x + y) + 1 - c)
        rcopy(0, blk, blk, sib).wait_recv()
        for j, (dev, chip) in enumerate(peers):
            blk = block(full, 2 * chip + 1 - c)
            rcopy(4 + j, blk, blk, sib).wait_recv()
        src, dst = ins[0], block(full, 4 * x + 2 * y + c)
        lcopy(0, src, dst).wait()
        rcopy(0, src, dst, sib).wait_send()
        for j, (dev, chip) in enumerate(peers):
            rcopy(1 + j, src, dst, dev).wait_send()
            blk = block(full, 2 * chip + c)
            rcopy(4 + j, blk, blk, sib).wait_send()

    return _Hook([shard], [jax.ShapeDtypeStruct((NDEV * rows, D), shard.dtype)], 7, 1, start, finish, relay if early else None)


def _ag_sequencer(name, collective_id, shards):
    n = len(shards)
    rows = [s.shape[0] for s in shards]
    src = [jax.new_ref(s, memory_space=pltpu.MemorySpace.HBM) for s in shards]
    dst = [jax.empty_ref(jax.ShapeDtypeStruct((NDEV * s.shape[0], D), s.dtype), memory_space=pltpu.MemorySpace.HBM)
           for s in shards]

    @pl.kernel(mesh=plsc.ScalarSubcoreMesh(axis_name="sequencer", num_cores=1), name=name,
               scratch_types=(pltpu.SemaphoreType.DMA((7 * n,)), pltpu.SemaphoreType.DMA((7 * n,)),
                              pltpu.SemaphoreType.DMA((n,))),
               compiler_params=pltpu.CompilerParams(collective_id=collective_id))
    def launch(ssem, rsem, lsem):
        x, y, c = _my_place()
        sib = (x, y, 1 - c)
        peers = _ici_peers(x, y, c)
        barrier = pltpu.get_barrier_semaphore()
        for dev in [sib] + [p for p, _ in peers]:
            pl.semaphore_signal(barrier, inc=1, device_id=dev, device_id_type=MESH_ID)
        pl.semaphore_wait(barrier, 1 + len(peers))

        def rcopy(i, k, s, d, dev):
            return pltpu.make_async_remote_copy(src_ref=s, dst_ref=d, send_sem=ssem.at[7 * i + k],
                                                recv_sem=rsem.at[7 * i + k], device_id=dev, device_id_type=MESH_ID)

        def block(i, dev_index):
            return dst[i].at[pl.ds(dev_index * rows[i], rows[i]), :]

        me = 4 * x + 2 * y + c
        for i in range(n):
            pltpu.make_async_copy(src[i], block(i, me), lsem.at[i]).start()
            rcopy(i, 0, src[i], block(i, me), sib).start()
            for j, (dev, _) in enumerate(peers):
                rcopy(i, 1 + j, src[i], block(i, me), dev).start()
        for i in range(n):
            for j, (dev, chip) in enumerate(peers):
                blk = block(i, 2 * chip + c)
                rcopy(i, 1 + j, blk, blk, dev).wait_recv()
                rcopy(i, 4 + j, blk, blk, sib).start()
        for i in range(n):
            blk = block(i, 2 * (2 * x + y) + 1 - c)
            rcopy(i, 0, blk, blk, sib).wait_recv()
            for j, (dev, chip) in enumerate(peers):
                blk = block(i, 2 * chip + 1 - c)
                rcopy(i, 4 + j, blk, blk, sib).wait_recv()
            pltpu.make_async_copy(src[i], block(i, me), lsem.at[i]).wait()
            rcopy(i, 0, src[i], block(i, me), sib).wait_send()
            for j, (dev, chip) in enumerate(peers):
                rcopy(i, 1 + j, src[i], block(i, me), dev).wait_send()
                blk = block(i, 2 * chip + c)
                rcopy(i, 4 + j, blk, blk, sib).wait_send()

    launch()
    return [d[...] for d in dst]


def _rs_d2d_hook(gfull):
    rows = gfull.shape[0] // NDEV

    def pairs(g, land):
        x, y, c = _my_place()
        return (x, y, 1 - c), [(k, _rows(g, (2 * k + 1 - c) * rows, rows), land.at[k]) for k in range(NCHIP)]

    def start(ins, outs, rcopy, lcopy):
        sib, cps = pairs(ins[0], outs[0])
        for i, src, dst in cps:
            rcopy(i, src, dst, sib).start()

    def finish(ins, outs, rcopy, lcopy):
        sib, cps = pairs(ins[0], outs[0])
        for i, src, dst in cps:
            rcopy(i, dst, dst, sib).wait_recv()
        for i, src, dst in cps:
            rcopy(i, src, dst, sib).wait_send()

    return _Hook([gfull], [jax.ShapeDtypeStruct((NCHIP, rows, D), gfull.dtype)], NCHIP, 0, start, finish)


def _rs_ici_hook(part):
    def start(ins, outs, rcopy, lcopy):
        x, y, c = _my_place()
        mychip = 2 * x + y
        lcopy(0, ins[0].at[mychip], outs[0].at[mychip]).start()
        for j, (dev, chip) in enumerate(_ici_peers(x, y, c)):
            rcopy(j, ins[0].at[chip], outs[0].at[mychip], dev).start()

    def finish(ins, outs, rcopy, lcopy):
        x, y, c = _my_place()
        mychip = 2 * x + y
        peers = _ici_peers(x, y, c)
        for j, (dev, chip) in enumerate(peers):
            rcopy(j, outs[0].at[chip], outs[0].at[chip], dev).wait_recv()
        for j, (dev, chip) in enumerate(peers):
            rcopy(j, ins[0].at[chip], outs[0].at[mychip], dev).wait_send()
        lcopy(0, ins[0].at[mychip], outs[0].at[mychip]).wait()

    return _Hook([part], [jax.ShapeDtypeStruct(part.shape, part.dtype)], 3, 1, start, finish)


def _small_hook(arrays):
    n = len(arrays)

    def peers():
        x, y, c = _my_place()
        out = []
        for dx in range(2):
            for dy in range(2):
                for dc in range(2):
                    if dx + dy + dc:
                        px, py, pc = (1 - x if dx else x), (1 - y if dy else y), (1 - c if dc else c)
                        out.append(((px, py, pc), 4 * px + 2 * py + pc))
        return 4 * x + 2 * y + c, out

    def start(ins, outs, rcopy, lcopy):
        me, ps = peers()
        for t in range(n):
            lcopy(t, ins[t], outs[t].at[me]).start()
            for i, (dev, _) in enumerate(ps):
                rcopy(n * i + t, ins[t], outs[t].at[me], dev).start()

    def finish(ins, outs, rcopy, lcopy):
        me, ps = peers()
        for t in range(n):
            for i, (dev, peer) in enumerate(ps):
                rcopy(n * i + t, outs[t].at[peer], outs[t].at[peer], dev).wait_recv()
            for i, (dev, _) in enumerate(ps):
                rcopy(n * i + t, ins[t], outs[t].at[me], dev).wait_send()
            lcopy(t, ins[t], outs[t].at[me]).wait()

    return _Hook(arrays, [jax.ShapeDtypeStruct((NDEV,) + a.shape, a.dtype) for a in arrays], 7 * n, n, start, finish)


def _wblock(w):
    return _resident(w.shape, lambda *_: (0, 0))


def _ffn_fwd(name, x, nrm, wg, wu, wd, hooks=(), tm=512):
    S = x.shape[0]

    def body(x_ref, n_ref, wg_ref, wu_ref, wd_ref, y_ref, g_ref, u_ref, a_ref, h_ref, acc_ref):
        xv = x_ref[...]
        _, _, h = _rms_fwd(xv, n_ref[...])
        h = h.astype(CDT)
        h_ref[...] = h
        for ci in range(F // FT):
            sl = slice(ci * FT, (ci + 1) * FT)
            g = _mm_nt(h, wg_ref[sl, :])
            u = _mm_nt(h, wu_ref[sl, :])
            g_ref[:, sl] = g.astype(CDT)
            u_ref[:, sl] = u.astype(CDT)
            a = (g * jax.nn.sigmoid(g) * u).astype(CDT)
            a_ref[:, sl] = a
            o = _mm(a, wd_ref[sl, :])
            if ci == 0:
                acc_ref[...] = o
            else:
                acc_ref[...] += o
        y_ref[...] = xv + 0.5 * acc_ref[...]

    tok = pl.BlockSpec((tm, D), lambda i: (i, 0))
    hid = pl.BlockSpec((tm, F), lambda i: (i, 0))
    hidden = jax.ShapeDtypeStruct((S, F), CDT)
    return _call(
        body, hooks, [x, nrm, wg, wu, wd], name=name, grid=(S // tm,),
        out_shape=[jax.ShapeDtypeStruct((S, D), f32), hidden, hidden, hidden, jax.ShapeDtypeStruct((S, D), CDT)],
        in_specs=[tok, _resident((1, D), lambda i: (0, 0)), _wblock(wg), _wblock(wu), _wblock(wd)],
        out_specs=[tok, hid, hid, hid, tok],
        scratch_shapes=[pltpu.VMEM((tm, D), f32)],
        compiler_params=_params("arbitrary"),
    )


def _proj_fwd(x1, nrm, wfull, b3, cos, sin, hooks=(), tm=512):
    S = x1.shape[0]

    def body(x_ref, n_ref, w_ref, b_ref, cos_ref, sin_ref, p_ref, h_ref):
        _, _, h = _rms_fwd(x_ref[...], n_ref[...])
        h = h.astype(CDT)
        h_ref[...] = h
        for p in range(8):
            seg = SEG_OF_SLOT[p]
            z = _mm(h, w_ref[seg * D:(seg + 1) * D, :]) + b_ref[seg]
            if p in (SLOT_Q, SLOT_K):
                co, si = cos_ref[...], sin_ref[...]
                for hh in range(H):
                    cs = slice(hh * DK, (hh + 1) * DK)
                    zr = _rotate(z[:, cs], co, si)
                    p_ref[p, :, cs] = (zr * K_SCALE if p == SLOT_K else zr).astype(CDT)
            else:
                p_ref[p] = z.astype(CDT)

    tab = pl.BlockSpec((tm, DK // 2), lambda i: (i, 0))
    return _call(
        body, hooks, [x1, nrm, wfull, b3, cos, sin], name="proj_fwd", grid=(S // tm,),
        out_shape=[jax.ShapeDtypeStruct((8, S, D), CDT), jax.ShapeDtypeStruct((S, D), CDT)],
        in_specs=[pl.BlockSpec((tm, D), lambda i: (i, 0)), _resident((1, D), lambda i: (0, 0)),
                  _resident((8 * D, D), lambda i: (0, 0)), _resident((8, 1, D), lambda i: (0, 0, 0)), tab, tab],
        out_specs=[pl.BlockSpec((8, tm, D), lambda i: (0, i, 0)), pl.BlockSpec((tm, D), lambda i: (i, 0))],
        compiler_params=_params("arbitrary"),
    )


def _sgu_norm(va, gn, bn):
    mu = jnp.mean(va, axis=-1, keepdims=True)
    xc = va - mu
    rstd = lax.rsqrt(jnp.mean(xc * xc, axis=-1, keepdims=True) + EPS)
    vhat = xc * rstd
    return rstd, vhat, vhat * gn + bn


def _sgu_fwd(proj, gn, bn, ws, bsc, tm=512):
    S = proj.shape[1]
    GW = D // G

    def body(p_ref, gn_ref, bn_ref, ws_ref, bs_ref, a_ref):
        ua = _gelu(p_ref[0].astype(f32))
        va = _gelu(p_ref[1].astype(f32))
        _, _, vn = _sgu_norm(va, gn_ref[...], bn_ref[...])
        vn = vn.astype(CDT)
        for ch in range(tm // C):
            rs = slice(ch * C, (ch + 1) * C)
            for gi in range(G):
                cs = slice(gi * GW, (gi + 1) * GW)
                s = _mm(ws_ref[gi], vn[rs, cs]) + bs_ref[gi]
                a_ref[rs, cs] = (ua[rs, cs] * s).astype(CDT)

    return pl.pallas_call(
        body, name="sgu_fwd", grid=(S // tm,),
        out_shape=jax.ShapeDtypeStruct((S, D), CDT),
        in_specs=[pl.BlockSpec((2, tm, D), lambda i: (0, i, 0)), _resident((1, D), lambda i: (0, 0)),
                  _resident((1, D), lambda i: (0, 0)), _resident((G, C, C), lambda i: (0, 0, 0)),
                  _resident((G, C, 1), lambda i: (0, 0, 0))],
        out_specs=pl.BlockSpec((tm, D), lambda i: (i, 0)),
        compiler_params=_params("arbitrary"),
    )(proj, gn, bn, ws, bsc)


def _decay_tables(dl_ref):
    lg = jax.nn.log_sigmoid(dl_ref[0:2, :])
    lgf, lgb = lg[0:1, :], lg[1:2, :]
    assert RC <= DK
    ri = lax.broadcasted_iota(jnp.int32, (RC, RC), 0)
    ci = lax.broadcasted_iota(jnp.int32, (RC, RC), 1)
    d = (ri - ci).astype(f32)
    lower = d >= 0
    dmat = jnp.where(lower, jnp.exp(d * lgf[:, :RC]), jnp.exp(-d * lgb[:, :RC]))
    dmat_t = jnp.where(d <= 0, jnp.exp(-d * lgf[:, :RC]), jnp.exp(d * lgb[:, :RC]))
    pos = lax.broadcasted_iota(jnp.int32, (RC, DK), 0).astype(f32)
    t = dict(
        lgf=lgf, lgb=lgb, d=d, lower=lower, dmat=dmat, dmat_t=dmat_t, pos=pos,
        fq=jnp.exp((pos + 1.0) * lgf), fk=jnp.exp((RC - 1.0 - pos) * lgf),
        bq=jnp.exp((RC - pos) * lgb), bk=jnp.exp(pos * lgb),
        lamf=jnp.exp(float(RC) * lgf), lamb=jnp.exp(float(RC) * lgb),
    )
    return t


def _rotate(t, co, si):
    t1, t2 = t[:, :DK // 2], t[:, DK // 2:]
    return jnp.concatenate([t1 * co - t2 * si, t2 * co + t1 * si], axis=-1)


def _unrotate(t, co, si):
    t1, t2 = t[:, :DK // 2], t[:, DK // 2:]
    return jnp.concatenate([t1 * co + t2 * si, t2 * co - t1 * si], axis=-1)


K_SCALE = DK ** -0.5
ROW_TILE = 256


def _ret_fwd(proj, dl, hooks=()):
    S = proj.shape[1]
    NC = S // RC

    def body(q_ref, k_ref, v_ref, g_ref, dl_ref, R_ref, r_ref, sfs_ref, sbs_ref, rb_ref, sf_ref, sb_ref):
        t = _decay_tables(dl_ref)

        def chunk(n):
            rows = pl.ds(pl.multiple_of(n * RC, RC), RC)
            return rows, q_ref[rows, :], k_ref[rows, :], v_ref[rows, :]

        sf_ref[...] = jnp.zeros_like(sf_ref)
        sb_ref[...] = jnp.zeros_like(sb_ref)

        def step(i, carry):
            rows, qn, kn, vn = chunk(i)
            sc = _mm_nt(qn, kn) * t["dmat"]
            out = _mm(sc.astype(CDT), vn)
            sf = sf_ref[...]
            sfb = sf.astype(CDT)
            sfs_ref[i] = sfb
            R_ref[rows, :] = out + _mm((qn.astype(f32) * t["fq"]).astype(CDT), sfb)
            sf_ref[...] = sf * t["lamf"] + _mm_tn((kn.astype(f32) * t["fk"]).astype(CDT), vn)
            m = NC - 1 - i
            rows, qn, kn, vn = chunk(m)
            sb = sb_ref[...]
            sbb = sb.astype(CDT)
            sbs_ref[m] = sbb
            rb_ref[rows, :] = _mm((qn.astype(f32) * t["bq"]).astype(CDT), sbb)
            sb_ref[...] = sb * t["lamb"] + _mm_tn((kn.astype(f32) * t["bk"]).astype(CDT), vn)
            return carry

        lax.fori_loop(0, NC, step, 0)

        def finish(i, carry):
            rs = pl.ds(pl.multiple_of(i * ROW_TILE, ROW_TILE), ROW_TILE)
            R = R_ref[rs, :] + rb_ref[rs, :]
            R_ref[rs, :] = R
            rn = R * lax.rsqrt(jnp.mean(R * R, axis=-1, keepdims=True) + EPS)
            g = g_ref[rs, :].astype(f32)
            r_ref[rs, :] = (rn * g * jax.nn.sigmoid(g)).astype(CDT)
            return carry

        lax.fori_loop(0, S // ROW_TILE, finish, 0)

    def seg(slot):
        return pl.BlockSpec((None, S, DK), lambda h: (slot, 0, h))

    states = jax.ShapeDtypeStruct((H, NC, DK, DK), CDT)
    state_blk = pl.BlockSpec((None, NC, DK, DK), lambda h: (h, 0, 0, 0))
    return _call(
        body, hooks, [proj, proj, proj, proj, dl], name="ret_fwd", grid=(H,),
        out_shape=[jax.ShapeDtypeStruct((S, H * DK), f32), jax.ShapeDtypeStruct((S, H * DK), CDT), states, states],
        in_specs=[seg(SLOT_Q), seg(SLOT_K), seg(SLOT_VR), seg(SLOT_GR), pl.BlockSpec((None, 8, DK), lambda h: (h, 0, 0))],
        out_specs=[pl.BlockSpec((S, DK), lambda h: (0, h)), pl.BlockSpec((S, DK), lambda h: (0, h)), state_blk, state_blk],
        scratch_shapes=[pltpu.VMEM((S, DK), f32), pltpu.VMEM((DK, DK), f32), pltpu.VMEM((DK, DK), f32)],
        compiler_params=_params("arbitrary"),
    )


def _merge_fwd(a, r, proj, x1, wa, wb, wo, hooks=(), tm=512):
    S = x1.shape[0]

    def body(a_ref, r_ref, gt_ref, x_ref, wa_ref, wb_ref, wo_ref, x2_ref, ya_ref, yb_ref):
        ya = _mm(a_ref[...], wa_ref[...])
        yb = _mm(r_ref[...], wb_ref[...])
        ya_ref[...] = ya.astype(CDT)
        yb_ref[...] = yb.astype(CDT)
        mix = jax.nn.sigmoid(gt_ref[0].astype(f32)) * ya + jax.nn.sigmoid(gt_ref[1].astype(f32)) * yb
        x2_ref[...] = x_ref[...] + _mm(mix.astype(CDT), wo_ref[...])

    tok = pl.BlockSpec((tm, D), lambda i: (i, 0))
    return _call(
        body, hooks, [a, r, proj, x1, wa, wb, wo], name="merge_fwd", grid=(S // tm,),
        out_shape=[jax.ShapeDtypeStruct((S, D), f32), jax.ShapeDtypeStruct((S, D), CDT), jax.ShapeDtypeStruct((S, D), CDT)],
        in_specs=[tok, tok, pl.BlockSpec((2, tm, D), lambda i: (SLOT_GA // 2, i, 0)), tok,
                  _wblock(wa), _wblock(wb), _wblock(wo)],
        out_specs=[tok, tok, tok],
        compiler_params=_params("arbitrary"),
    )


def _loss_head(x3, fn, target, tm=512):
    S = x3.shape[0]

    def body(x_ref, n_ref, t_ref, dx_ref, dxh_ref, dn_ref, l_ref):
        n = n_ref[...]
        r, xh, y = _rms_fwd(x_ref[...], n)
        e = y - t_ref[...]
        dy = e * (1.0 / D)
        dx, dn = _rms_bwd(dy, r, xh, n)
        dx_ref[...] = dx
        dxh_ref[...] = (0.5 * dx).astype(CDT)
        part = 0.5 * jnp.sum(jnp.sum(e * e, axis=-1, keepdims=True), axis=0, keepdims=True) * (1.0 / D)

        @pl.when(pl.program_id(0) == 0)
        def _():
            dn_ref[...] = jnp.zeros_like(dn_ref)
            l_ref[...] = jnp.zeros_like(l_ref)

        dn_ref[...] += dn
        l_ref[...] += jnp.broadcast_to(part, l_ref.shape)

    tok = pl.BlockSpec((tm, D), lambda i: (i, 0))
    return pl.pallas_call(
        body, name="loss_head", grid=(S // tm,),
        out_shape=[jax.ShapeDtypeStruct((S, D), f32), jax.ShapeDtypeStruct((S, D), CDT), jax.ShapeDtypeStruct((1, D), f32),
                   jax.ShapeDtypeStruct((8, 128), f32)],
        in_specs=[tok, _resident((1, D), lambda i: (0, 0)), tok],
        out_specs=[tok, tok, pl.BlockSpec((1, D), lambda i: (0, 0)), pl.BlockSpec((8, 128), lambda i: (0, 0))],
        compiler_params=_params("arbitrary"),
    )(x3, fn, target)


def _ffn_bwd_hidden(name, dyh, g, u, wd, hooks=(), tm=512):
    S = dyh.shape[0]

    def body(dyh_ref, g_ref, u_ref, wd_ref, dg_ref, du_ref):
        dyh = dyh_ref[...]
        for ci in range(F // FT):
            sl = slice(ci * FT, (ci + 1) * FT)
            da = _mm_nt(dyh, wd_ref[sl, :])
            gv = g_ref[:, sl].astype(f32)
            uv = u_ref[:, sl].astype(f32)
            s = jax.nn.sigmoid(gv)
            du_ref[:, sl] = (da * (gv * s)).astype(CDT)
            dg_ref[:, sl] = (da * uv * (s * (1.0 + gv * (1.0 - s)))).astype(CDT)

    hid = pl.BlockSpec((tm, F), lambda i: (i, 0))
    hidden = jax.ShapeDtypeStruct((S, F), CDT)
    return _call(
        body, hooks, [dyh, g, u, wd], name=name, grid=(S // tm,), out_shape=[hidden, hidden],
        in_specs=[pl.BlockSpec((tm, D), lambda i: (i, 0)), hid, hid, _wblock(wd)], out_specs=[hid, hid],
        compiler_params=_params("arbitrary"),
    )


def _ffn_bwd_in(name, dy, x, dg, du, nrm, wg, wu, hooks=(), part=None, prev=None, tm=512):
    S = x.shape[0]
    t0, nt = part or (0, S // tm)

    def body(dy_ref, x_ref, dg_ref, du_ref, n_ref, wg_ref, wu_ref, *rest):
        dx_ref, dn_ref, acc_ref = rest[-3:]
        n = n_ref[...]
        r, xh, _ = _rms_fwd(x_ref[...], n)
        for ci in range(F // FT):
            sl = slice(ci * FT, (ci + 1) * FT)
            dh = _mm(dg_ref[:, sl], wg_ref[sl, :]) + _mm(du_ref[:, sl], wu_ref[sl, :])
            if ci == 0:
                acc_ref[...] = dh
            else:
                acc_ref[...] += dh
        dx, dn = _rms_bwd(acc_ref[...], r, xh, n)
        dx_ref[...] = dy_ref[...] + dx

        @pl.when(pl.program_id(0) == 0)
        def _():
            dn_ref[...] = jnp.zeros_like(dn_ref) if prev is None else rest[1][...]

        dn_ref[...] += dn

    tok = pl.BlockSpec((tm, D), lambda i: (t0 + i, 0))
    hid = pl.BlockSpec((tm, F), lambda i: (t0 + i, 0))
    row = pl.BlockSpec((1, D), lambda i: (0, 0))
    in_specs = [tok, tok, hid, hid, _resident((1, D), lambda i: (0, 0)), _wblock(wg), _wblock(wu)]
    operands = [dy, x, dg, du, nrm, wg, wu]
    aliases = {}
    if prev is not None:
        in_specs += [_HBM, row]
        operands += list(prev)
        aliases = {7: 0}
    return _call(
        body, hooks, operands, name=name, grid=(nt,),
        out_shape=[jax.ShapeDtypeStruct((S, D), f32), jax.ShapeDtypeStruct((1, D), f32)],
        in_specs=in_specs, out_specs=[tok, row],
        scratch_shapes=[pltpu.VMEM((tm, D), f32)],
        input_output_aliases=aliases,
        compiler_params=_params("arbitrary"),
    )


TN_ROWS = 512


def _tn(name, xs, ys, block_of, hooks=()):
    S, M = xs.shape
    B = ys.shape[0]
    tr = TN_ROWS if M % TN_ROWS == 0 else M // 2
    assert M % tr == 0 and tr % 128 == 0
    nt = M // tr

    def body(x_ref, y_ref, o_ref):
        o_ref[...] = _mm_tn(x_ref[...], y_ref[...]).astype(CDT)

    return _call(
        body, hooks, [xs, ys], name=name, grid=(B, nt),
        out_shape=[jax.ShapeDtypeStruct((B * M, D), CDT)],
        in_specs=[pl.BlockSpec((S, tr), lambda b, i: (0, i)), pl.BlockSpec((None, S, D), lambda b, i: (b, 0, 0))],
        out_specs=[pl.BlockSpec((tr, D), lambda b, i: (block_of(b) * nt + i, 0))],
        compiler_params=_params("arbitrary", "arbitrary"),
    )


def _wgrad(name, xs, y, hooks=()):
    return _tn(name, xs, y[None], lambda b: 0, hooks)


def _merge_bwd_act(dx2, ya, yb, proj, wa, wb, wo, hooks=(), tm=512):
    S = dx2.shape[0]

    def body(dx_ref, ya_ref, yb_ref, gt_ref, wa_ref, wb_ref, wo_ref,
             dp_ref, da_ref, dr_ref, mix_ref, dxb_ref, dya_ref, dyb_ref):
        dxb = dx_ref[...].astype(CDT)
        dxb_ref[...] = dxb
        dmix = _mm_nt(dxb, wo_ref[...])
        ya = ya_ref[...].astype(f32)
        yb = yb_ref[...].astype(f32)
        sa = jax.nn.sigmoid(gt_ref[0].astype(f32))
        sb = jax.nn.sigmoid(gt_ref[1].astype(f32))
        mix_ref[...] = (sa * ya + sb * yb).astype(CDT)
        dya = (dmix * sa).astype(CDT)
        dyb = (dmix * sb).astype(CDT)
        dya_ref[...] = dya
        dyb_ref[...] = dyb
        dp_ref[0] = (dmix * ya * sa * (1.0 - sa)).astype(CDT)
        dp_ref[1] = (dmix * yb * sb * (1.0 - sb)).astype(CDT)
        da_ref[...] = _mm_nt(dya, wa_ref[...]).astype(CDT)
        dr_ref[...] = _mm_nt(dyb, wb_ref[...]).astype(CDT)

    tok = pl.BlockSpec((tm, D), lambda i: (i, 0))
    gates = pl.BlockSpec((2, tm, D), lambda i: (SLOT_GA // 2, i, 0))
    act = jax.ShapeDtypeStruct((S, D), CDT)
    return _call(
        body, hooks, [dx2, ya, yb, proj, wa, wb, wo], name="merge_bwd_act", grid=(S // tm,),
        out_shape=[jax.ShapeDtypeStruct((8, S, D), CDT), act, act, act, act, act, act],
        in_specs=[tok, tok, tok, gates, _wblock(wa), _wblock(wb), _wblock(wo)],
        out_specs=[gates, tok, tok, tok, tok, tok, tok],
        compiler_params=_params("arbitrary"),
    )


def _sgu_bwd(da, proj, dproj, gn, bn, ws, wst, bsc, hooks=(), tm=512):
    S = proj.shape[1]
    GW = D // G

    def body(da_ref, p_ref, dpin_ref, gn_ref, bn_ref, ws_ref, wst_ref, bs_ref,
             dp_ref, dws_ref, dbs_ref, dgn_ref, dbn_ref, ds_ref, dvn_ref):
        @pl.when(pl.program_id(0) == 0)
        def _():
            dws_ref[...] = jnp.zeros_like(dws_ref)
            dbs_ref[...] = jnp.zeros_like(dbs_ref)
            dgn_ref[...] = jnp.zeros_like(dgn_ref)
            dbn_ref[...] = jnp.zeros_like(dbn_ref)

        pu = p_ref[0].astype(f32)
        pv = p_ref[1].astype(f32)
        ua = _gelu(pu)
        va = _gelu(pv)
        gn = gn_ref[...]
        rstd, vhat, vn = _sgu_norm(va, gn, bn_ref[...])
        vnb = vn.astype(CDT)
        dav = da_ref[...].astype(f32)
        dsb = (dav * ua).astype(CDT)
        ones = jnp.ones((8, GW), CDT)
        for ch in range(tm // C):
            rs = slice(ch * C, (ch + 1) * C)
            for gi in range(G):
                cs = slice(gi * GW, (gi + 1) * GW)
                s = _mm(ws_ref[gi], vnb[rs, cs]) + bs_ref[gi]
                ds_ref[rs, cs] = s
                dsg = dsb[rs, cs]
                dws_ref[gi] += _mm_nt(dsg, vnb[rs, cs])
                dbs_ref[gi] += _mm_nt(ones, dsg)
                dvn_ref[rs, cs] = _mm(wst_ref[gi], dsg)
        dp_ref[0] = (dav * ds_ref[...] * _gelu_grad(pu)).astype(CDT)
        dvn = dvn_ref[...]
        dgn_ref[...] += jnp.sum(dvn * vhat, axis=0, keepdims=True)
        dbn_ref[...] += jnp.sum(dvn, axis=0, keepdims=True)
        dvh = dvn * gn
        dva = rstd * (dvh - jnp.mean(dvh, axis=-1, keepdims=True) - vhat * jnp.mean(dvh * vhat, axis=-1, keepdims=True))
        dp_ref[1] = (dva * _gelu_grad(pv)).astype(CDT)

    uv = pl.BlockSpec((2, tm, D), lambda i: (0, i, 0))
    row = _resident((1, D), lambda i: (0, 0))
    return _call(
        body, hooks, [da, proj, dproj, gn, bn, ws, wst, bsc], name="sgu_bwd", grid=(S // tm,),
        out_shape=[jax.ShapeDtypeStruct(dproj.shape, CDT), jax.ShapeDtypeStruct((G, C, C), f32),
                   jax.ShapeDtypeStruct((G, 8, C), f32), jax.ShapeDtypeStruct((1, D), f32), jax.ShapeDtypeStruct((1, D), f32)],
        in_specs=[pl.BlockSpec((tm, D), lambda i: (i, 0)), uv, _HBM, row, row,
                  _resident((G, C, C), lambda i: (0, 0, 0)), _resident((G, C, C), lambda i: (0, 0, 0)),
                  _resident((G, C, 1), lambda i: (0, 0, 0))],
        out_specs=[uv, pl.BlockSpec((G, C, C), lambda i: (0, 0, 0)), pl.BlockSpec((G, 8, C), lambda i: (0, 0, 0)),
                   pl.BlockSpec((1, D), lambda i: (0, 0)), pl.BlockSpec((1, D), lambda i: (0, 0))],
        scratch_shapes=[pltpu.VMEM((tm, D), f32), pltpu.VMEM((tm, D), f32)],
        input_output_aliases={2: 0},
        compiler_params=_params("arbitrary"),
    )


def _ret_bwd(dr, R, sfs, sbs, proj, dproj, cos, sin, dl, hooks=()):
    S = proj.shape[1]
    NC = S // RC
    assert NC % 2 == 0

    def body(dr_ref, R_ref, sf_ref, sb_ref, q_ref, k_ref, v_ref, g_ref, dpin_ref, cos_ref, sin_ref, dl_ref,
             dp_ref, dd_ref, dR_ref, gb_ref, gf_ref, acc_ref):
        t = _decay_tables(dl_ref)

        def gate_norm_bwd(i, carry):
            rs = pl.ds(pl.multiple_of(i * ROW_TILE, ROW_TILE), ROW_TILE)
            Rv = R_ref[rs, :]
            rstd = lax.rsqrt(jnp.mean(Rv * Rv, axis=-1, keepdims=True) + EPS)
            rn = Rv * rstd
            gv = g_ref[rs, :].astype(f32)
            s = jax.nn.sigmoid(gv)
            drv = dr_ref[rs, :].astype(f32)
            dp_ref[3, rs, :] = (drv * rn * (s * (1.0 + gv * (1.0 - s)))).astype(CDT)
            drn = drv * gv * s
            dR_ref[rs, :] = (rstd * (drn - rn * jnp.mean(drn * rn, axis=-1, keepdims=True))).astype(CDT)
            return carry

        lax.fori_loop(0, S // ROW_TILE, gate_norm_bwd, 0)

        def chunk(n):
            rows = pl.ds(pl.multiple_of(n * RC, RC), RC)
            return rows, q_ref[rows, :], k_ref[rows, :], v_ref[rows, :], dR_ref[rows, :]

        def emit_kv(rows, dk, dv, final):
            if not final:
                dp_ref[1, rows, :] = dk.astype(CDT)
                dp_ref[2, rows, :] = dv.astype(CDT)
            else:
                co, si = cos_ref[rows, :], sin_ref[rows, :]
                dk = dp_ref[1, rows, :].astype(f32) + dk
                dp_ref[1, rows, :] = (_unrotate(dk, co, si) * K_SCALE).astype(CDT)
                dp_ref[2, rows, :] = (dp_ref[2, rows, :].astype(f32) + dv).astype(CDT)

        gb_ref[...] = jnp.zeros_like(gb_ref)
        gf_ref[...] = jnp.zeros_like(gf_ref)
        acc_ref[...] = jnp.zeros_like(acc_ref)
        dpos = jnp.abs(t["d"])

        def ascend(n, final):
            rows, qn, kn, vn, dRn = chunk(n)
            qf, kf = qn.astype(f32), kn.astype(f32)
            sc = _mm_nt(qn, kn)
            dA = _mm_nt(dRn, vn)
            w = sc * dA * t["dmat"] * dpos
            lgf_part = jnp.sum(jnp.where(t["lower"], w, 0.0), axis=0, keepdims=True)
            lgb_part = jnp.sum(jnp.where(t["lower"], 0.0, w), axis=0, keepdims=True)
            dsc = (dA * t["dmat"]).astype(CDT)
            dq = _mm(dsc, kn)
            scT = (_mm_nt(kn, qn) * t["dmat_t"]).astype(CDT)
            dscT = (_mm_nt(vn, dRn) * t["dmat_t"]).astype(CDT)
            dk = _mm(dscT, qn)
            dv = _mm(scT, dRn)
            sfb = sf_ref[n]
            sbb = sb_ref[n]
            qdf = qf * t["fq"]
            dqdf = _mm_nt(dRn, sfb)
            dq += dqdf * t["fq"]
            lgf_row = jnp.sum(qdf * dqdf * (t["pos"] + 1.0), axis=0, keepdims=True)
            qdb = qf * t["bq"]
            dqdb = _mm_nt(dRn, sbb)
            dq += dqdb * t["bq"]
            lgb_row = jnp.sum(qdb * dqdb * (RC - t["pos"]), axis=0, keepdims=True)
            gb = gb_ref[...]
            gbb = gb.astype(CDT)
            kdb = kf * t["bk"]
            dkdb = _mm_nt(vn, gbb)
            dk += dkdb * t["bk"]
            dv += _mm(kdb.astype(CDT), gbb)
            lgb_row += jnp.sum(kdb * dkdb * t["pos"], axis=0, keepdims=True)
            lgb_row += float(RC) * t["lamb"] * jnp.sum(gb * sbb.astype(f32), axis=0, keepdims=True)
            co, si = cos_ref[rows, :], sin_ref[rows, :]
            dp_ref[0, rows, :] = _unrotate(dq, co, si).astype(CDT)
            emit_kv(rows, dk, dv, final)
            acc_ref[0:1, :] += lgf_row + lgf_part
            acc_ref[1:2, :] += lgb_row + lgb_part
            gb_ref[...] = gb * t["lamb"] + _mm_tn(qdb.astype(CDT), dRn)

        def descend(n, final):
            rows, qn, kn, vn, dRn = chunk(n)
            gf = gf_ref[...]
            gfb = gf.astype(CDT)
            kdf = kn.astype(f32) * t["fk"]
            dkdf = _mm_nt(vn, gfb)
            lgf_row = jnp.sum(kdf * dkdf * (RC - 1.0 - t["pos"]), axis=0, keepdims=True)
            lgf_row += float(RC) * t["lamf"] * jnp.sum(gf * sf_ref[n].astype(f32), axis=0, keepdims=True)
            acc_ref[0:1, :] += lgf_row
            emit_kv(rows, dkdf * t["fk"], _mm(kdf.astype(CDT), gfb), final)
            gf_ref[...] = gf * t["lamf"] + _mm_tn((qn.astype(f32) * t["fq"]).astype(CDT), dRn)

        def sweep(final):
            def step(i, carry):
                ascend(i, final)
                descend(NC - 1 - i, final)
                return carry
            return step

        lax.fori_loop(0, NC // 2, sweep(False), 0)
        lax.fori_loop(NC // 2, NC, sweep(True), 0)
        dlg = jnp.sum(acc_ref[...], axis=1, keepdims=True)
        dlogit = dlg * jax.nn.sigmoid(-dl_ref[:, 0:1])
        lane = lax.broadcasted_iota(jnp.int32, (8, 128), 1)
        dd_ref[...] = jnp.where(lane == pl.program_id(0), jnp.broadcast_to(dlogit, (8, 128)), 0.0)

    def seg(slot):
        return pl.BlockSpec((None, S, DK), lambda h: (slot, 0, h))

    head = pl.BlockSpec((S, DK), lambda h: (0, h))
    states = pl.BlockSpec((None, NC, DK, DK), lambda h: (h, 0, 0, 0))
    return _call(
        body, hooks, [dr, R, sfs, sbs, proj, proj, proj, proj, dproj, cos, sin, dl], name="ret_bwd", grid=(H,),
        out_shape=[jax.ShapeDtypeStruct(dproj.shape, CDT), jax.ShapeDtypeStruct((H, 8, 128), f32)],
        in_specs=[head, head, states, states, seg(SLOT_Q), seg(SLOT_K), seg(SLOT_VR), seg(SLOT_GR), _HBM,
                  _resident((S, DK // 2), lambda h: (0, 0)), _resident((S, DK // 2), lambda h: (0, 0)),
                  pl.BlockSpec((None, 8, DK), lambda h: (h, 0, 0))],
        out_specs=[pl.BlockSpec((4, S, DK), lambda h: (1, 0, h), pipeline_mode=pl.Buffered(1)),
                   pl.BlockSpec((None, 8, 128), lambda h: (h, 0, 0))],
        scratch_shapes=[pltpu.VMEM((S, DK), CDT),
                        pltpu.VMEM((DK, DK), f32), pltpu.VMEM((DK, DK), f32), pltpu.VMEM((8, DK), f32)],
        input_output_aliases={8: 0},
        compiler_params=_params("arbitrary"),
    )


def _proj_bwd_act(dproj, dx2, x1, nrm, wfull, hooks=(), tm=512):
    S = x1.shape[0]

    def body(dp_ref, dx2_ref, x_ref, n_ref, w_ref, dx_ref, dxh_ref, dn_ref, db_ref, acc_ref):
        @pl.when(pl.program_id(0) == 0)
        def _():
            dn_ref[...] = jnp.zeros_like(dn_ref)
            db_ref[...] = jnp.zeros_like(db_ref)

        for p in range(8):
            seg = SEG_OF_SLOT[p]
            dp = dp_ref[p]
            db_ref[seg] += jnp.sum(dp.astype(f32), axis=0, keepdims=True)
            dh = _mm_nt(dp, w_ref[seg * D:(seg + 1) * D, :])
            if p == 0:
                acc_ref[...] = dh
            else:
                acc_ref[...] += dh
        n = n_ref[...]
        r, xh, _ = _rms_fwd(x_ref[...], n)
        dx, dn = _rms_bwd(acc_ref[...], r, xh, n)
        dx = dx2_ref[...] + dx
        dx_ref[...] = dx
        dxh_ref[...] = (0.5 * dx).astype(CDT)
        dn_ref[...] += dn

    tok = pl.BlockSpec((tm, D), lambda i: (i, 0))
    return _call(
        body, hooks, [dproj, dx2, x1, nrm, wfull], name="proj_bwd_act", grid=(S // tm,),
        out_shape=[jax.ShapeDtypeStruct((S, D), f32), jax.ShapeDtypeStruct((S, D), CDT), jax.ShapeDtypeStruct((1, D), f32),
                   jax.ShapeDtypeStruct((8, 1, D), f32)],
        in_specs=[pl.BlockSpec((8, tm, D), lambda i: (0, i, 0)), tok, tok, _resident((1, D), lambda i: (0, 0)),
                  _resident((8 * D, D), lambda i: (0, 0))],
        out_specs=[tok, tok, pl.BlockSpec((1, D), lambda i: (0, 0)), pl.BlockSpec((8, 1, D), lambda i: (0, 0, 0))],
        scratch_shapes=[pltpu.VMEM((tm, D), f32)],
        compiler_params=_params("arbitrary"),
    )


def _rs_sum(name, gfulls, lands, my_c):
    n = len(gfulls)
    rows = gfulls[0].shape[0] // NDEV
    assert all(g.shape[0] == NDEV * rows for g in gfulls)

    def body(c_ref, *refs):
        for g_ref, l_ref, o_ref in zip(refs[:n], refs[n:2 * n], refs[2 * n:]):
            o_ref[...] = (g_ref[...].astype(f32) + l_ref[...].astype(f32)).astype(CDT)

    slot = pl.BlockSpec((None, rows, D), lambda k, c: (k, 0, 0))
    return pl.pallas_call(
        body, name=name,
        grid_spec=pltpu.PrefetchScalarGridSpec(
            num_scalar_prefetch=1, grid=(NCHIP,),
            in_specs=[pl.BlockSpec((rows, D), lambda k, c: (2 * k + c[0], 0))] * n + [slot] * n,
            out_specs=[slot] * n),
        out_shape=[jax.ShapeDtypeStruct((NCHIP, rows, D), CDT)] * n,
        compiler_params=_params("arbitrary"),
    )(my_c, *gfulls, *lands)


def _adamw_math(g, w, m, v):
    m2 = ADAM_B1 * m + (1.0 - ADAM_B1) * g
    v2 = ADAM_B2 * v + (1.0 - ADAM_B2) * (g * g)
    delta = -ADAM_LR * ((m2 / BC1) / (jnp.sqrt(v2 / BC2) + ADAM_EPS) + ADAM_WD * w)
    return delta, m2, v2


def _adamw_big(name, landed, w, m, v, hooks=()):
    rows = w.shape[0]
    tr = min(rows, 256) if rows % 256 == 0 else rows

    def body(l_ref, w_ref, m_ref, v_ref, g_ref, d_ref, m2_ref, v2_ref):
        g = l_ref[0].astype(f32)
        for k in range(1, NCHIP):
            g = g + l_ref[k].astype(f32)
        g_ref[...] = g
        d_ref[...], m2_ref[...], v2_ref[...] = _adamw_math(g, w_ref[...], m_ref[...], v_ref[...])

    blk = pl.BlockSpec((tr, D), lambda i: (i, 0))
    o = jax.ShapeDtypeStruct((rows, D), f32)
    return _call(
        body, hooks, [landed, w, m, v], name=name, grid=(rows // tr,), out_shape=[o, o, o, o],
        in_specs=[pl.BlockSpec((NCHIP, tr, D), lambda i: (0, i, 0)), blk, blk, blk],
        out_specs=[blk, blk, blk, blk],
        compiler_params=_params("arbitrary"),
    )


ROW_FFN1_NORM, ROW_MIX_NORM, ROW_SGU_G, ROW_SGU_B, ROW_FFN2_NORM, ROW_FINAL_NORM, ROW_B_IN = 0, 1, 2, 3, 4, 5, 8
ROW_WS, ROW_BS, ROW_DECAY = 0, G * C, G * C + G * 8


def _adamw_small(ga, gb, gn1, gl, params):
    def body(ga_ref, gb_ref, gn1_ref, gl_ref, *refs):
        ins, outs = refs[:30], refs[30:]

        def total(ref, r0, n):
            g = ref[0, r0:r0 + n, :]
            for j in range(1, NDEV):
                g = g + ref[j, r0:r0 + n, :]
            return g

        def apply(i, g, rows=slice(None)):
            w, m, v = ins[3 * i][rows, :], ins[3 * i + 1][rows, :], ins[3 * i + 2][rows, :]
            outs[4 * i][rows, :] = g
            outs[4 * i + 1][rows, :], outs[4 * i + 2][rows, :], outs[4 * i + 3][rows, :] = _adamw_math(g, w, m, v)

        outs[40][...] = total(gl_ref, 0, 8)
        apply(0, total(gn1_ref, 0, 1))
        for i, r in enumerate((ROW_FFN1_NORM, ROW_MIX_NORM, ROW_SGU_G, ROW_SGU_B, ROW_FFN2_NORM, ROW_FINAL_NORM)):
            if i:
                apply(i, total(ga_ref, r, 1))
        apply(6, total(ga_ref, ROW_B_IN, 8))
        apply(7, total(gb_ref, ROW_WS, G * C))
        for gi in range(G):
            apply(8, total(gb_ref, ROW_BS + 8 * gi, 1), slice(gi, gi + 1))
        dec = total(gb_ref, ROW_DECAY, 8)
        for hh in range(1, H):
            dec = dec + total(gb_ref, ROW_DECAY + 8 * hh, 8)
        apply(9, dec)

    flat = [a for p in params for a in p]
    out_shape = [jax.ShapeDtypeStruct(p[0].shape, f32) for p in params for _ in range(4)]
    out_shape.append(jax.ShapeDtypeStruct((8, 128), f32))
    vm = pl.BlockSpec(memory_space=pltpu.VMEM)
    return pl.pallas_call(
        body, name="adamw_small", out_shape=out_shape,
        in_specs=[vm] * (4 + len(flat)), out_specs=[vm] * len(out_shape),
        compiler_params=pltpu.CompilerParams(vmem_limit_bytes=VMEM_LIMIT),
    )(ga, gb, gn1, gl, *flat)


def kernel(x, ffn1_norm, ffn1_w_gate, ffn1_w_up, ffn1_w_down, mix_norm, w_in, b_in, sgu_norm_g, sgu_norm_b, sgu_w_s, sgu_b_s, ret_decay_logit, w_branch_a, w_branch_b, w_out, ffn2_norm, ffn2_w_gate, ffn2_w_up, ffn2_w_down, final_norm, loss_target, m_ffn1_norm, m_ffn1_w_gate, m_ffn1_w_up, m_ffn1_w_down, m_mix_norm, m_w_in, m_b_in, m_sgu_norm_g, m_sgu_norm_b, m_sgu_w_s, m_sgu_b_s, m_ret_decay_logit, m_w_branch_a, m_w_branch_b, m_w_out, m_ffn2_norm, m_ffn2_w_gate, m_ffn2_w_up, m_ffn2_w_down, m_final_norm, v_ffn1_norm, v_ffn1_w_gate, v_ffn1_w_up, v_ffn1_w_down, v_mix_norm, v_w_in, v_b_in, v_sgu_norm_g, v_sgu_norm_b, v_sgu_w_s, v_sgu_b_s, v_ret_decay_logit, v_w_branch_a, v_w_branch_b, v_w_out, v_ffn2_norm, v_ffn2_w_gate, v_ffn2_w_up, v_ffn2_w_down, v_final_norm):
    args = dict(locals())
    S = x.shape[1]
    xs = x[0]
    target = loss_target[0]

    def buf_layout(name, a):
        a = a[0]
        return a.T if name in W_TRANSPOSED else a

    sh = {n: buf_layout(n, args[n]).astype(CDT) for n in W_NAMES}
    wf = {}

    b3 = b_in.reshape(8, 1, D)
    ws = sgu_w_s[0].astype(CDT)
    wst = jnp.swapaxes(sgu_w_s[0], 1, 2).astype(CDT)
    bsc = sgu_b_s[0].reshape(G, C, 1)
    dl = jnp.zeros((H, 8, DK), f32).at[:, 0:2, :].set(jnp.broadcast_to(ret_decay_logit[0].T[:, :, None], (H, 2, DK)))
    theta = ROPE_BASE ** (-jnp.arange(0, DK, 2, dtype=f32) / DK)
    ang = jnp.arange(S, dtype=f32)[:, None] * theta[None, :]
    cos, sin = jnp.cos(ang), jnp.sin(ang)
    fnorm = final_norm.reshape(1, D)

    f1 = ("ffn1_w_gate", "ffn1_w_up", "ffn1_w_down")
    f2 = ("ffn2_w_gate", "ffn2_w_up", "ffn2_w_down")
    br = ("w_branch_a", "w_branch_b", "w_out")
    for cid, names in enumerate((f1, ("w_in",), br, f2)):
        wf.update(zip(names, _ag_sequencer("ag_" + names[0], 1 + cid, [sh[n] for n in names])))
    x1, g1, u1, a1, hf1 = _ffn_fwd("ffn1_fwd", xs, ffn1_norm, *[wf[n] for n in f1])
    proj, h2 = _proj_fwd(x1, mix_norm, wf["w_in"], b3, cos, sin)
    a = _sgu_fwd(proj, sgu_norm_g, sgu_norm_b, ws, bsc)
    R, r, sfs, sbs = _ret_fwd(proj, dl)
    x2, ya, yb = _merge_fwd(a, r, proj, x1, *[wf[n] for n in br])
    x3, g2, u2, a2, hf2 = _ffn_fwd("ffn2_fwd", x2, ffn2_norm, *[wf[n] for n in f2])
    dx3, dyh2, d_final, loss_part = _loss_head(x3, fnorm, target)

    my_c = lax.axis_index("c").astype(jnp.int32).reshape(1)
    gw, landed = {}, {}

    def d2d(*names):
        return [_rs_d2d_hook(gw[n]) for n in names]

    def ici(parts):
        return [_rs_ici_hook(p) for p in parts]

    def rs_sum(names, sibs):
        return list(_rs_sum("rs_sum_" + names[0], [gw[n] for n in names], list(sibs), my_c))

    def ffn_bwd(tag, names, dy, dyh, x, g, u, a, h, nrm, eager, on_first, on_hidden):
        wg, wu, wd = names
        gw[wd], *first = _wgrad(tag + "_wd_grad", a, dyh, on_first)
        dg, du, sib, *hidden = _ffn_bwd_hidden(tag + "_bwd_hidden", dyh, g, u, wf[wd], d2d(wd) + on_hidden)
        (part_d,) = rs_sum([wd], [sib])
        gw[wg], *got = _wgrad(tag + "_wg_grad", dg, h, ici([part_d]) if eager else [])
        landed.update(zip([wd], got))
        gw[wu], sib = _wgrad(tag + "_wu_grad", du, h, d2d(wg))
        (part_g,) = rs_sum([wg], [sib])
        args_in = (dy, x, dg, du, nrm, wf[wg], wf[wu])
        if not eager:
            dx, dn, sib = _ffn_bwd_in(tag + "_bwd_in", *args_in, d2d(wu))
            return dx, dn, [part_d, part_g] + rs_sum([wu], [sib]), first, hidden
        half = x.shape[0] // 512 // 2
        dx, dn, landed[wg], sib = _ffn_bwd_in(tag + "_bwd_in_a", *args_in, ici([part_g]) + d2d(wu), part=(0, half))
        dx, dn, landed[wu] = _ffn_bwd_in(tag + "_bwd_in_b", *args_in, ici(rs_sum([wu], [sib])), part=(half, half),
                                         prev=(dx, dn))
        return dx, dn, [], first, hidden

    dx2, d_ffn2n, parts_f2, _, _ = ffn_bwd("ffn2", f2, dx3, dyh2, x2, g2, u2, a2, hf2, ffn2_norm, False, [], [])
    dproj, da, dr, mix, dx2b, dya, dyb = _merge_bwd_act(dx2, ya, yb, proj, *[wf[n] for n in br])
    dproj, d_ws, d_bs, d_gn, d_bn = _sgu_bwd(da, proj, dproj, sgu_norm_g, sgu_norm_b, ws, wst, bsc)
    small_sgu = jnp.concatenate([d_ws.reshape(G * C, C), d_bs.reshape(G * 8, C)], axis=0)
    dproj, d_dec, *got, g_sgu, gl = _ret_bwd(dr, R, sfs, sbs, proj, dproj, cos, sin, dl,
                                             ici(parts_f2) + [_small_hook([small_sgu, loss_part])])
    landed.update(zip((f2[2], f2[0], f2[1]), got))
    (gw["w_in"],) = _tn("win_grad", h2, dproj, _seg_of_slot)
    gw["w_out"], sib_win = _wgrad("wo_grad", mix, dx2b, d2d("w_in"))
    (part_win,) = rs_sum(["w_in"], [sib_win])
    (gw["w_branch_a"],) = _wgrad("wa_grad", a, dya)
    (gw["w_branch_b"],) = _wgrad("wb_grad", r, dyb)
    dx1, dyh1, d_mixn, d_bin, landed["w_in"], *sib = _proj_bwd_act(dproj, dx2, x1, mix_norm, wf["w_in"],
                                                                   ici([part_win]) + d2d(*br))
    part_br = rs_sum(br, sib)
    small_a = jnp.concatenate([jnp.zeros((1, D), f32), d_mixn, d_gn, d_bn, d_ffn2n, d_final, jnp.zeros((2, D), f32),
                               d_bin.reshape(8, D)], axis=0)
    dxs, d_ffn1n, _, (ga, g_dec), got = ffn_bwd("ffn1", f1, dx1, dyh1, xs, g1, u1, a1, hf1, ffn1_norm, True,
                                               [_small_hook([small_a, d_dec.reshape(H * 8, 128)])], ici(part_br))
    landed.update(zip(br, got))
    gb = jnp.concatenate([g_sgu, g_dec], axis=1)
    (gn1,) = _exchange("ag_ffn1_norm", [_small_hook([d_ffn1n])])

    out = {"grad_x": dxs[None]}

    def native(name, a):
        a = a.T if name in W_TRANSPOSED else a
        return a[None]

    for n in W_NAMES:
        res = _adamw_big("adamw_" + n, landed[n], buf_layout(n, args[n]), buf_layout(n, args["m_" + n]),
                         buf_layout(n, args["v_" + n]))
        for pre, val in zip(("grad_", "delta_", "new_m_", "new_v_"), res):
            out[pre + n] = native(n, val)

    def pad_decay(a):
        return jnp.zeros((8, 128), f32).at[0:2, 0:H].set(a[0])

    small = [
        ("ffn1_norm", lambda a: a, lambda a: a), ("mix_norm", lambda a: a, lambda a: a),
        ("sgu_norm_g", lambda a: a, lambda a: a), ("sgu_norm_b", lambda a: a, lambda a: a),
        ("ffn2_norm", lambda a: a, lambda a: a),
        ("final_norm", lambda a: a.reshape(1, D), lambda a: a.reshape(D)),
        ("b_in", lambda a: a.reshape(8, D), lambda a: a.reshape(1, 8 * D)),
        ("sgu_w_s", lambda a: a.reshape(G * C, C), lambda a: a.reshape(1, G, C, C)),
        ("sgu_b_s", lambda a: a[0], lambda a: a[None]),
        ("ret_decay_logit", pad_decay, lambda a: a[None, 0:2, 0:H]),
    ]
    res = _adamw_small(ga, gb, gn1, gl, [(to(args[n]), to(args["m_" + n]), to(args["v_" + n])) for n, to, _ in small])
    out["loss"] = res[40][0, 0]
    for i, (n, _, back) in enumerate(small):
        for j, pre in enumerate(("grad_", "delta_", "new_m_", "new_v_")):
            out[pre + n] = back(res[4 * i + j])

    weights = ("ffn1_norm", "ffn1_w_gate", "ffn1_w_up", "ffn1_w_down", "mix_norm", "w_in", "b_in", "sgu_norm_g",
               "sgu_norm_b", "sgu_w_s", "sgu_b_s", "ret_decay_logit", "w_branch_a", "w_branch_b", "w_out", "ffn2_norm",
               "ffn2_w_gate", "ffn2_w_up", "ffn2_w_down", "final_norm")
    return (out["loss"], out["grad_x"], *[out["grad_" + n] for n in weights], *[out["delta_" + n] for n in weights],
            *[out["new_m_" + n] for n in weights], *[out["new_v_" + n] for n in weights])
```

```python
import functools
import math

import jax
import jax.numpy as jnp
from jax import lax
from jax.experimental import pallas as pl
from jax.experimental.pallas import tpu as pltpu
from jax.experimental.pallas import tpu_sc as plsc

f32 = jnp.float32
CDT = jnp.bfloat16

D = 1024
F = 2816
C = 128
RC = 256
H = 4
DK = 256
G = 4
NDEV = 8
NCHIP = 4
EPS = 1e-6
ROPE_BASE = 10000.0
FT = 256
V7X_VMEM_BYTES = 64 * 1024 * 1024
VMEM_LIMIT = V7X_VMEM_BYTES - 8 * 1024 * 1024

ADAM_LR, ADAM_B1, ADAM_B2, ADAM_EPS, ADAM_WD, ADAM_STEP = 0.001, 0.9, 0.999, 1e-08, 0.01, 10
BC1 = 1.0 - ADAM_B1 ** ADAM_STEP
BC2 = 1.0 - ADAM_B2 ** ADAM_STEP

W_ROWS = dict(ffn1_w_gate=352, ffn1_w_up=352, ffn1_w_down=352, w_in=1024, w_branch_a=128, w_branch_b=128, w_out=128,
              ffn2_w_gate=352, ffn2_w_up=352, ffn2_w_down=352)
W_NAMES = tuple(W_ROWS)
W_TRANSPOSED = ("ffn1_w_gate", "ffn1_w_up", "ffn2_w_gate", "ffn2_w_up")

SLOT_U, SLOT_V, SLOT_GA, SLOT_GB, SLOT_Q, SLOT_K, SLOT_VR, SLOT_GR = range(8)


SEG_OF_SLOT = (0, 1, 6, 7, 2, 3, 4, 5)


def _seg_of_slot(p):
    return jnp.where(p < 2, p, jnp.where(p < 4, p + 4, p - 2))


def _mm(a, b):
    return jnp.dot(a, b, preferred_element_type=f32)


def _mm_nt(a, b):
    return lax.dot_general(a, b, (((1,), (1,)), ((), ())), preferred_element_type=f32)


def _mm_tn(a, b):
    return lax.dot_general(a, b, (((0,), (0,)), ((), ())), preferred_element_type=f32)


def _params(*sem):
    return pltpu.CompilerParams(dimension_semantics=sem, vmem_limit_bytes=VMEM_LIMIT)


def _resident(shape, index_map):
    return pl.BlockSpec(shape, index_map, pipeline_mode=pl.Buffered(1))


def _gelu(x):
    return 0.5 * x * (1.0 + lax.erf(x * (1.0 / math.sqrt(2.0))))


def _gelu_grad(x):
    return 0.5 * (1.0 + lax.erf(x * (1.0 / math.sqrt(2.0)))) + x * jnp.exp(-0.5 * x * x) * (1.0 / math.sqrt(2.0 * math.pi))


def _rms_fwd(x, n):
    r = lax.rsqrt(jnp.mean(x * x, axis=-1, keepdims=True) + EPS)
    xh = x * r
    return r, xh, xh * n


def _rms_bwd(dh, r, xh, n):
    dxh = dh * n
    dx = r * (dxh - xh * jnp.mean(dxh * xh, axis=-1, keepdims=True))
    return dx, jnp.sum(dh * xh, axis=0, keepdims=True)


MESH_ID = pl.DeviceIdType.MESH
_HBM = pl.BlockSpec(memory_space=pltpu.HBM)


def _my_place():
    return lax.axis_index("x"), lax.axis_index("y"), lax.axis_index("c")


def _ici_peers(x, y, c):
    return [((1 - x, y, c), 2 * (1 - x) + y), ((x, 1 - y, c), 2 * x + 1 - y), ((1 - x, 1 - y, c), 2 * (1 - x) + 1 - y)]


class _Hook:
    def __init__(self, operands, out_shapes, n_remote, n_local, start, finish, relay=None):
        self.operands, self.out_shapes = list(operands), list(out_shapes)
        self.n_remote, self.n_local, self.start, self.finish = n_remote, n_local, start, finish
        self.relay = relay or (lambda *a: None)


def _call(body, hooks, operands, *, in_specs, out_specs, out_shape, grid=None, scratch_shapes=(), **kw):
    hooks = tuple(hooks)
    n_in, n_out, n_scr = len(in_specs), len(out_shape), len(scratch_shapes)
    h_ops = [a for h in hooks for a in h.operands]
    h_outs = [s for h in hooks for s in h.out_shapes]
    h_sems = [pltpu.SemaphoreType.DMA((n,)) for h in hooks for n in (h.n_remote, h.n_remote, max(h.n_local, 1))]

    def wrapped(*refs):
        ins, hin = refs[:n_in], refs[n_in:n_in + len(h_ops)]
        o0 = n_in + len(h_ops)
        outs, hout = refs[o0:o0 + n_out], refs[o0 + n_out:o0 + n_out + len(h_outs)]
        s0 = o0 + n_out + len(h_outs)
        scr, hsem = refs[s0:s0 + n_scr], refs[s0 + n_scr:]

        def run(phase):
            ip = op = 0
            for i, h in enumerate(hooks):
                ssem, rsem, lsem = hsem[3 * i:3 * i + 3]

                def rcopy(k, src, dst, dev, ssem=ssem, rsem=rsem):
                    return pltpu.make_async_remote_copy(src_ref=src, dst_ref=dst, send_sem=ssem.at[k], recv_sem=rsem.at[k],
                                                        device_id=dev, device_id_type=MESH_ID)

                def lcopy(k, src, dst, lsem=lsem):
                    return pltpu.make_async_copy(src, dst, lsem.at[k])

                getattr(h, phase)(hin[ip:ip + len(h.operands)], hout[op:op + len(h.out_shapes)], rcopy, lcopy)
                ip += len(h.operands)
                op += len(h.out_shapes)

        def at_edge(phase, last):
            if not hooks:
                return
            if grid is None:
                run(phase)
                return
            cond = None
            for ax, n in enumerate(grid):
                here = pl.program_id(ax) == (n - 1 if last else 0)
                cond = here if cond is None else cond & here
            pl.when(cond)(lambda: run(phase))

        at_edge("start", False)
        at_edge("relay", True)
        body(*ins, *outs, *scr)
        at_edge("finish", True)

    if grid is not None:
        kw["grid"] = grid
    return list(pl.pallas_call(
        wrapped, out_shape=list(out_shape) + h_outs, in_specs=list(in_specs) + [_HBM] * len(h_ops),
        out_specs=list(out_specs) + [_HBM] * len(h_outs), scratch_shapes=list(scratch_shapes) + h_sems, **kw,
    )(*operands, *h_ops))


def _exchange(name, hooks):
    return _call(lambda: None, hooks, [], name=name, in_specs=[], out_specs=[], out_shape=[])


def _rows(ref, start, n):
    return ref.at[pl.ds(start, n), :]


def _ag_hook(shard, early=True):
    rows = shard.shape[0]

    def block(full, dev_index):
        return _rows(full, dev_index * rows, rows)

    def start(ins, outs, rcopy, lcopy):
        x, y, c = _my_place()
        src, dst = ins[0], block(outs[0], 4 * x + 2 * y + c)
        lcopy(0, src, dst).start()
        rcopy(0, src, dst, (x, y, 1 - c)).start()
        for j, (dev, _) in enumerate(_ici_peers(x, y, c)):
            rcopy(1 + j, src, dst, dev).start()

    def relay(ins, outs, rcopy, lcopy):
        x, y, c = _my_place()
        for j, (dev, chip) in enumerate(_ici_peers(x, y, c)):
            blk = block(outs[0], 2 * chip + c)
            rcopy(1 + j, blk, blk, dev).wait_recv()
            rcopy(4 + j, blk, blk, (x, y, 1 - c)).start()

    def finish(ins, outs, rcopy, lcopy):
        if not early:
            relay(ins, outs, rcopy, lcopy)
        x, y, c = _my_place()
        sib = (x, y, 1 - c)
        full = outs[0]
        peers = _ici_peers(x, y, c)
        blk = block(full, 2 * (2 * x + y) + 1 - c)
        rcopy(0, blk, blk, sib).wait_recv()
        for j, (dev, chip) in enumerate(peers):
            blk = block(full, 2 * chip + 1 - c)
            rcopy(4 + j, blk, blk, sib).wait_recv()
        src, dst = ins[0], block(full, 4 * x + 2 * y + c)
        lcopy(0, src, dst).wait()
        rcopy(0, src, dst, sib).wait_send()
        for j, (dev, chip) in enumerate(peers):
            rcopy(1 + j, src, dst, dev).wait_send()
            blk = block(full, 2 * chip + c)
            rcopy(4 + j, blk, blk, sib).wait_send()

    return _Hook([shard], [jax.ShapeDtypeStruct((NDEV * rows, D), shard.dtype)], 7, 1, start, finish, relay if early else None)


SIBLING, CHIPS, SIBLING_AND_CHIPS, EVERYONE = "sibling", "chips", "sibling and chips", "everyone"
REACH_ID = {SIBLING: 5, CHIPS: 6, EVERYONE: 7}


def _sequence(name, collective_id, reach, hooks):
    ins = [[jax.new_ref(a, memory_space=pltpu.MemorySpace.HBM) for a in h.operands] for h in hooks]
    outs = [[jax.empty_ref(s, memory_space=pltpu.MemorySpace.HBM) for s in h.out_shapes] for h in hooks]
    sems = tuple(pltpu.SemaphoreType.DMA((n,)) for h in hooks for n in (h.n_remote, h.n_remote, max(h.n_local, 1)))

    @pl.kernel(mesh=plsc.ScalarSubcoreMesh(axis_name="sequencer", num_cores=1), name=name, scratch_types=sems,
               compiler_params=pltpu.CompilerParams(collective_id=collective_id))
    def launch(*sem_refs):
        x, y, c = _my_place()
        chips = [dev for dev, _ in _ici_peers(x, y, c)]
        others = [(1 - x if dx else x, 1 - y if dy else y, 1 - c if dc else c)
                  for dx in range(2) for dy in range(2) for dc in range(2) if dx + dy + dc]
        devs = {SIBLING: [(x, y, 1 - c)], CHIPS: chips, SIBLING_AND_CHIPS: [(x, y, 1 - c)] + chips, EVERYONE: others}[reach]
        barrier = pltpu.get_barrier_semaphore()
        for dev in devs:
            pl.semaphore_signal(barrier, inc=1, device_id=dev, device_id_type=MESH_ID)
        pl.semaphore_wait(barrier, len(devs))
        for phase in ("start", "relay", "finish"):
            for i, h in enumerate(hooks):
                ssem, rsem, lsem = sem_refs[3 * i:3 * i + 3]

                def rcopy(k, src, dst, dev, ssem=ssem, rsem=rsem):
                    return pltpu.make_async_remote_copy(src_ref=src, dst_ref=dst, send_sem=ssem.at[k], recv_sem=rsem.at[k],
                                                        device_id=dev, device_id_type=MESH_ID)

                def lcopy(k, src, dst, lsem=lsem):
                    return pltpu.make_async_copy(src, dst, lsem.at[k])

                getattr(h, phase)(ins[i], outs[i], rcopy, lcopy)

    launch()
    return [o[...] for os in outs for o in os]


def _rs_d2d_hook(gfull):
    rows = gfull.shape[0] // NDEV

    def pairs(g, land):
        x, y, c = _my_place()
        return (x, y, 1 - c), [(k, _rows(g, (2 * k + 1 - c) * rows, rows), land.at[k]) for k in range(NCHIP)]

    def start(ins, outs, rcopy, lcopy):
        sib, cps = pairs(ins[0], outs[0])
        for i, src, dst in cps:
            rcopy(i, src, dst, sib).start()

    def finish(ins, outs, rcopy, lcopy):
        sib, cps = pairs(ins[0], outs[0])
        for i, src, dst in cps:
            rcopy(i, dst, dst, sib).wait_recv()
        for i, src, dst in cps:
            rcopy(i, src, dst, sib).wait_send()

    return _Hook([gfull], [jax.ShapeDtypeStruct((NCHIP, rows, D), gfull.dtype)], NCHIP, 0, start, finish)


def _rs_ici_hook(part):
    def start(ins, outs, rcopy, lcopy):
        x, y, c = _my_place()
        mychip = 2 * x + y
        lcopy(0, ins[0].at[mychip], outs[0].at[mychip]).start()
        for j, (dev, chip) in enumerate(_ici_peers(x, y, c)):
            rcopy(j, ins[0].at[chip], outs[0].at[mychip], dev).start()

    def finish(ins, outs, rcopy, lcopy):
        x, y, c = _my_place()
        mychip = 2 * x + y
        peers = _ici_peers(x, y, c)
        for j, (dev, chip) in enumerate(peers):
            rcopy(j, outs[0].at[chip], outs[0].at[chip], dev).wait_recv()
        for j, (dev, chip) in enumerate(peers):
            rcopy(j, ins[0].at[chip], outs[0].at[mychip], dev).wait_send()
        lcopy(0, ins[0].at[mychip], outs[0].at[mychip]).wait()

    return _Hook([part], [jax.ShapeDtypeStruct(part.shape, part.dtype)], 3, 1, start, finish)


def _small_hook(arrays):
    n = len(arrays)

    def peers():
        x, y, c = _my_place()
        out = []
        for dx in range(2):
            for dy in range(2):
                for dc in range(2):
                    if dx + dy + dc:
                        px, py, pc = (1 - x if dx else x), (1 - y if dy else y), (1 - c if dc else c)
                        out.append(((px, py, pc), 4 * px + 2 * py + pc))
        return 4 * x + 2 * y + c, out

    def start(ins, outs, rcopy, lcopy):
        me, ps = peers()
        for t in range(n):
            lcopy(t, ins[t], outs[t].at[me]).start()
            for i, (dev, _) in enumerate(ps):
                rcopy(n * i + t, ins[t], outs[t].at[me], dev).start()

    def finish(ins, outs, rcopy, lcopy):
        me, ps = peers()
        for t in range(n):
            for i, (dev, peer) in enumerate(ps):
                rcopy(n * i + t, outs[t].at[peer], outs[t].at[peer], dev).wait_recv()
            for i, (dev, _) in enumerate(ps):
                rcopy(n * i + t, ins[t], outs[t].at[me], dev).wait_send()
            lcopy(t, ins[t], outs[t].at[me]).wait()

    return _Hook(arrays, [jax.ShapeDtypeStruct((NDEV,) + a.shape, a.dtype) for a in arrays], 7 * n, n, start, finish)


def _wblock(w):
    return _resident(w.shape, lambda *_: (0, 0))


def _ffn_fwd(name, x, nrm, wg, wu, wd, hooks=(), tm=512):
    S = x.shape[0]

    def body(x_ref, n_ref, wg_ref, wu_ref, wd_ref, y_ref, g_ref, u_ref, a_ref, h_ref, acc_ref):
        xv = x_ref[...]
        _, _, h = _rms_fwd(xv, n_ref[...])
        h = h.astype(CDT)
        h_ref[...] = h
        for ci in range(F // FT):
            sl = slice(ci * FT, (ci + 1) * FT)
            g = _mm_nt(h, wg_ref[sl, :])
            u = _mm_nt(h, wu_ref[sl, :])
            g_ref[:, sl] = g.astype(CDT)
            u_ref[:, sl] = u.astype(CDT)
            a = (g * jax.nn.sigmoid(g) * u).astype(CDT)
            a_ref[:, sl] = a
            o = _mm(a, wd_ref[sl, :])
            if ci == 0:
                acc_ref[...] = o
            else:
                acc_ref[...] += o
        y_ref[...] = xv + 0.5 * acc_ref[...]

    tok = pl.BlockSpec((tm, D), lambda i: (i, 0))
    hid = pl.BlockSpec((tm, F), lambda i: (i, 0))
    hidden = jax.ShapeDtypeStruct((S, F), CDT)
    return _call(
        body, hooks, [x, nrm, wg, wu, wd], name=name, grid=(S // tm,),
        out_shape=[jax.ShapeDtypeStruct((S, D), f32), hidden, hidden, hidden, jax.ShapeDtypeStruct((S, D), CDT)],
        in_specs=[tok, _resident((1, D), lambda i: (0, 0)), _wblock(wg), _wblock(wu), _wblock(wd)],
        out_specs=[tok, hid, hid, hid, tok],
        scratch_shapes=[pltpu.VMEM((tm, D), f32)],
        compiler_params=_params("arbitrary"),
    )


def _proj_fwd(x1, nrm, wfull, b3, cos, sin, hooks=(), tm=512):
    S = x1.shape[0]

    def body(x_ref, n_ref, w_ref, b_ref, cos_ref, sin_ref, p_ref, h_ref):
        _, _, h = _rms_fwd(x_ref[...], n_ref[...])
        h = h.astype(CDT)
        h_ref[...] = h
        for p in range(8):
            seg = SEG_OF_SLOT[p]
            z = _mm(h, w_ref[seg * D:(seg + 1) * D, :]) + b_ref[seg]
            if p in (SLOT_Q, SLOT_K):
                co, si = cos_ref[...], sin_ref[...]
                for hh in range(H):
                    cs = slice(hh * DK, (hh + 1) * DK)
                    zr = _rotate(z[:, cs], co, si)
                    p_ref[p, :, cs] = (zr * K_SCALE if p == SLOT_K else zr).astype(CDT)
            else:
                p_ref[p] = z.astype(CDT)

    tab = pl.BlockSpec((tm, DK // 2), lambda i: (i, 0))
    return _call(
        body, hooks, [x1, nrm, wfull, b3, cos, sin], name="proj_fwd", grid=(S // tm,),
        out_shape=[jax.ShapeDtypeStruct((8, S, D), CDT), jax.ShapeDtypeStruct((S, D), CDT)],
        in_specs=[pl.BlockSpec((tm, D), lambda i: (i, 0)), _resident((1, D), lambda i: (0, 0)),
                  _resident((8 * D, D), lambda i: (0, 0)), _resident((8, 1, D), lambda i: (0, 0, 0)), tab, tab],
        out_specs=[pl.BlockSpec((8, tm, D), lambda i: (0, i, 0)), pl.BlockSpec((tm, D), lambda i: (i, 0))],
        compiler_params=_params("arbitrary"),
    )


def _sgu_norm(va, gn, bn):
    mu = jnp.mean(va, axis=-1, keepdims=True)
    xc = va - mu
    rstd = lax.rsqrt(jnp.mean(xc * xc, axis=-1, keepdims=True) + EPS)
    vhat = xc * rstd
    return rstd, vhat, vhat * gn + bn


def _sgu_fwd(proj, gn, bn, ws, bsc, tm=512):
    S = proj.shape[1]
    GW = D // G

    def body(p_ref, gn_ref, bn_ref, ws_ref, bs_ref, a_ref):
        ua = _gelu(p_ref[0].astype(f32))
        va = _gelu(p_ref[1].astype(f32))
        _, _, vn = _sgu_norm(va, gn_ref[...], bn_ref[...])
        vn = vn.astype(CDT)
        for ch in range(tm // C):
            rs = slice(ch * C, (ch + 1) * C)
            for gi in range(G):
                cs = slice(gi * GW, (gi + 1) * GW)
                s = _mm(ws_ref[gi], vn[rs, cs]) + bs_ref[gi]
                a_ref[rs, cs] = (ua[rs, cs] * s).astype(CDT)

    return pl.pallas_call(
        body, name="sgu_fwd", grid=(S // tm,),
        out_shape=jax.ShapeDtypeStruct((S, D), CDT),
        in_specs=[pl.BlockSpec((2, tm, D), lambda i: (0, i, 0)), _resident((1, D), lambda i: (0, 0)),
                  _resident((1, D), lambda i: (0, 0)), _resident((G, C, C), lambda i: (0, 0, 0)),
                  _resident((G, C, 1), lambda i: (0, 0, 0))],
        out_specs=pl.BlockSpec((tm, D), lambda i: (i, 0)),
        compiler_params=_params("arbitrary"),
    )(proj, gn, bn, ws, bsc)


def _decay_tables(dl_ref):
    lg = jax.nn.log_sigmoid(dl_ref[0:2, :])
    lgf, lgb = lg[0:1, :], lg[1:2, :]
    assert RC <= DK
    ri = lax.broadcasted_iota(jnp.int32, (RC, RC), 0)
    ci = lax.broadcasted_iota(jnp.int32, (RC, RC), 1)
    d = (ri - ci).astype(f32)
    lower = d >= 0
    dmat = jnp.where(lower, jnp.exp(d * lgf[:, :RC]), jnp.exp(-d * lgb[:, :RC]))
    dmat_t = jnp.where(d <= 0, jnp.exp(-d * lgf[:, :RC]), jnp.exp(d * lgb[:, :RC]))
    pos = lax.broadcasted_iota(jnp.int32, (RC, DK), 0).astype(f32)
    t = dict(
        lgf=lgf, lgb=lgb, d=d, lower=lower, dmat=dmat, dmat_t=dmat_t, pos=pos,
        fq=jnp.exp((pos + 1.0) * lgf), fk=jnp.exp((RC - 1.0 - pos) * lgf),
        bq=jnp.exp((RC - pos) * lgb), bk=jnp.exp(pos * lgb),
        lamf=jnp.exp(float(RC) * lgf), lamb=jnp.exp(float(RC) * lgb),
    )
    return t


def _rotate(t, co, si):
    t1, t2 = t[:, :DK // 2], t[:, DK // 2:]
    return jnp.concatenate([t1 * co - t2 * si, t2 * co + t1 * si], axis=-1)


def _unrotate(t, co, si):
    t1, t2 = t[:, :DK // 2], t[:, DK // 2:]
    return jnp.concatenate([t1 * co + t2 * si, t2 * co - t1 * si], axis=-1)


K_SCALE = DK ** -0.5
ROW_TILE = 256


def _ret_fwd(proj, dl, hooks=()):
    S = proj.shape[1]
    NC = S // RC

    def body(q_ref, k_ref, v_ref, g_ref, dl_ref, R_ref, r_ref, sfs_ref, sbs_ref, rb_ref, sf_ref, sb_ref):
        t = _decay_tables(dl_ref)

        def chunk(n):
            rows = pl.ds(pl.multiple_of(n * RC, RC), RC)
            return rows, q_ref[rows, :], k_ref[rows, :], v_ref[rows, :]

        sf_ref[...] = jnp.zeros_like(sf_ref)
        sb_ref[...] = jnp.zeros_like(sb_ref)

        def step(i, carry):
            rows, qn, kn, vn = chunk(i)
            sc = _mm_nt(qn, kn) * t["dmat"]
            out = _mm(sc.astype(CDT), vn)
            sf = sf_ref[...]
            sfb = sf.astype(CDT)
            sfs_ref[i] = sfb
            R_ref[rows, :] = out + _mm((qn.astype(f32) * t["fq"]).astype(CDT), sfb)
            sf_ref[...] = sf * t["lamf"] + _mm_tn((kn.astype(f32) * t["fk"]).astype(CDT), vn)
            m = NC - 1 - i
            rows, qn, kn, vn = chunk(m)
            sb = sb_ref[...]
            sbb = sb.astype(CDT)
            sbs_ref[m] = sbb
            rb_ref[rows, :] = _mm((qn.astype(f32) * t["bq"]).astype(CDT), sbb)
            sb_ref[...] = sb * t["lamb"] + _mm_tn((kn.astype(f32) * t["bk"]).astype(CDT), vn)
            return carry

        lax.fori_loop(0, NC, step, 0)

        def finish(i, carry):
            rs = pl.ds(pl.multiple_of(i * ROW_TILE, ROW_TILE), ROW_TILE)
            R = R_ref[rs, :] + rb_ref[rs, :]
            R_ref[rs, :] = R
            rn = R * lax.rsqrt(jnp.mean(R * R, axis=-1, keepdims=True) + EPS)
            g = g_ref[rs, :].astype(f32)
            r_ref[rs, :] = (rn * g * jax.nn.sigmoid(g)).astype(CDT)
            return carry

        lax.fori_loop(0, S // ROW_TILE, finish, 0)

    def seg(slot):
        return pl.BlockSpec((None, S, DK), lambda h: (slot, 0, h))

    states = jax.ShapeDtypeStruct((H, NC, DK, DK), CDT)
    state_blk = pl.BlockSpec((None, NC, DK, DK), lambda h: (h, 0, 0, 0))
    return _call(
        body, hooks, [proj, proj, proj, proj, dl], name="ret_fwd", grid=(H,),
        out_shape=[jax.ShapeDtypeStruct((S, H * DK), f32), jax.ShapeDtypeStruct((S, H * DK), CDT), states, states],
        in_specs=[seg(SLOT_Q), seg(SLOT_K), seg(SLOT_VR), seg(SLOT_GR), pl.BlockSpec((None, 8, DK), lambda h: (h, 0, 0))],
        out_specs=[pl.BlockSpec((S, DK), lambda h: (0, h)), pl.BlockSpec((S, DK), lambda h: (0, h)), state_blk, state_blk],
        scratch_shapes=[pltpu.VMEM((S, DK), f32), pltpu.VMEM((DK, DK), f32), pltpu.VMEM((DK, DK), f32)],
        compiler_params=_params("arbitrary"),
    )


def _merge_fwd(a, r, proj, x1, wa, wb, wo, hooks=(), tm=512):
    S = x1.shape[0]

    def body(a_ref, r_ref, gt_ref, x_ref, wa_ref, wb_ref, wo_ref, x2_ref, ya_ref, yb_ref):
        ya = _mm(a_ref[...], wa_ref[...])
        yb = _mm(r_ref[...], wb_ref[...])
        ya_ref[...] = ya.astype(CDT)
        yb_ref[...] = yb.astype(CDT)
        mix = jax.nn.sigmoid(gt_ref[0].astype(f32)) * ya + jax.nn.sigmoid(gt_ref[1].astype(f32)) * yb
        x2_ref[...] = x_ref[...] + _mm(mix.astype(CDT), wo_ref[...])

    tok = pl.BlockSpec((tm, D), lambda i: (i, 0))
    return _call(
        body, hooks, [a, r, proj, x1, wa, wb, wo], name="merge_fwd", grid=(S // tm,),
        out_shape=[jax.ShapeDtypeStruct((S, D), f32), jax.ShapeDtypeStruct((S, D), CDT), jax.ShapeDtypeStruct((S, D), CDT)],
        in_specs=[tok, tok, pl.BlockSpec((2, tm, D), lambda i: (SLOT_GA // 2, i, 0)), tok,
                  _wblock(wa), _wblock(wb), _wblock(wo)],
        out_specs=[tok, tok, tok],
        compiler_params=_params("arbitrary"),
    )


def _loss_head(x3, fn, target, tm=512):
    S = x3.shape[0]

    def body(x_ref, n_ref, t_ref, dx_ref, dxh_ref, dn_ref, l_ref):
        n = n_ref[...]
        r, xh, y = _rms_fwd(x_ref[...], n)
        e = y - t_ref[...]
        dy = e * (1.0 / D)
        dx, dn = _rms_bwd(dy, r, xh, n)
        dx_ref[...] = dx
        dxh_ref[...] = (0.5 * dx).astype(CDT)
        part = 0.5 * jnp.sum(jnp.sum(e * e, axis=-1, keepdims=True), axis=0, keepdims=True) * (1.0 / D)

        @pl.when(pl.program_id(0) == 0)
        def _():
            dn_ref[...] = jnp.zeros_like(dn_ref)
            l_ref[...] = jnp.zeros_like(l_ref)

        dn_ref[...] += dn
        l_ref[...] += jnp.broadcast_to(part, l_ref.shape)

    tok = pl.BlockSpec((tm, D), lambda i: (i, 0))
    return pl.pallas_call(
        body, name="loss_head", grid=(S // tm,),
        out_shape=[jax.ShapeDtypeStruct((S, D), f32), jax.ShapeDtypeStruct((S, D), CDT), jax.ShapeDtypeStruct((1, D), f32),
                   jax.ShapeDtypeStruct((8, 128), f32)],
        in_specs=[tok, _resident((1, D), lambda i: (0, 0)), tok],
        out_specs=[tok, tok, pl.BlockSpec((1, D), lambda i: (0, 0)), pl.BlockSpec((8, 128), lambda i: (0, 0))],
        compiler_params=_params("arbitrary"),
    )(x3, fn, target)


def _ffn_bwd_hidden(name, dyh, g, u, wd, hooks=(), tm=512):
    S = dyh.shape[0]

    def body(dyh_ref, g_ref, u_ref, wd_ref, dg_ref, du_ref):
        dyh = dyh_ref[...]
        for ci in range(F // FT):
            sl = slice(ci * FT, (ci + 1) * FT)
            da = _mm_nt(dyh, wd_ref[sl, :])
            gv = g_ref[:, sl].astype(f32)
            uv = u_ref[:, sl].astype(f32)
            s = jax.nn.sigmoid(gv)
            du_ref[:, sl] = (da * (gv * s)).astype(CDT)
            dg_ref[:, sl] = (da * uv * (s * (1.0 + gv * (1.0 - s)))).astype(CDT)

    hid = pl.BlockSpec((tm, F), lambda i: (i, 0))
    hidden = jax.ShapeDtypeStruct((S, F), CDT)
    return _call(
        body, hooks, [dyh, g, u, wd], name=name, grid=(S // tm,), out_shape=[hidden, hidden],
        in_specs=[pl.BlockSpec((tm, D), lambda i: (i, 0)), hid, hid, _wblock(wd)], out_specs=[hid, hid],
        compiler_params=_params("arbitrary"),
    )


def _ffn_bwd_in(name, dy, x, dg, du, nrm, wg, wu, hooks=(), part=None, prev=None, tm=512):
    S = x.shape[0]
    t0, nt = part or (0, S // tm)

    def body(dy_ref, x_ref, dg_ref, du_ref, n_ref, wg_ref, wu_ref, *rest):
        dx_ref, dn_ref, acc_ref = rest[-3:]
        n = n_ref[...]
        r, xh, _ = _rms_fwd(x_ref[...], n)
        for ci in range(F // FT):
            sl = slice(ci * FT, (ci + 1) * FT)
            dh = _mm(dg_ref[:, sl], wg_ref[sl, :]) + _mm(du_ref[:, sl], wu_ref[sl, :])
            if ci == 0:
                acc_ref[...] = dh
            else:
                acc_ref[...] += dh
        dx, dn = _rms_bwd(acc_ref[...], r, xh, n)
        dx_ref[...] = dy_ref[...] + dx

        @pl.when(pl.program_id(0) == 0)
        def _():
            dn_ref[...] = jnp.zeros_like(dn_ref) if prev is None else rest[1][...]

        dn_ref[...] += dn

    tok = pl.BlockSpec((tm, D), lambda i: (t0 + i, 0))
    hid = pl.BlockSpec((tm, F), lambda i: (t0 + i, 0))
    row = pl.BlockSpec((1, D), lambda i: (0, 0))
    in_specs = [tok, tok, hid, hid, _resident((1, D), lambda i: (0, 0)), _wblock(wg), _wblock(wu)]
    operands = [dy, x, dg, du, nrm, wg, wu]
    aliases = {}
    if prev is not None:
        in_specs += [_HBM, row]
        operands += list(prev)
        aliases = {7: 0}
    return _call(
        body, hooks, operands, name=name, grid=(nt,),
        out_shape=[jax.ShapeDtypeStruct((S, D), f32), jax.ShapeDtypeStruct((1, D), f32)],
        in_specs=in_specs, out_specs=[tok, row],
        scratch_shapes=[pltpu.VMEM((tm, D), f32)],
        input_output_aliases=aliases,
        compiler_params=_params("arbitrary"),
    )


TN_ROWS = 512


def _tn(name, xs, ys, block_of, hooks=()):
    S, M = xs.shape
    B = ys.shape[0]
    tr = TN_ROWS if M % TN_ROWS == 0 else M // 2
    assert M % tr == 0 and tr % 128 == 0
    nt = M // tr

    def body(x_ref, y_ref, o_ref):
        o_ref[...] = _mm_tn(x_ref[...], y_ref[...]).astype(CDT)

    return _call(
        body, hooks, [xs, ys], name=name, grid=(B, nt),
        out_shape=[jax.ShapeDtypeStruct((B * M, D), CDT)],
        in_specs=[pl.BlockSpec((S, tr), lambda b, i: (0, i)), pl.BlockSpec((None, S, D), lambda b, i: (b, 0, 0))],
        out_specs=[pl.BlockSpec((tr, D), lambda b, i: (block_of(b) * nt + i, 0))],
        compiler_params=_params("arbitrary", "arbitrary"),
    )


def _wgrad(name, xs, y, hooks=()):
    return _tn(name, xs, y[None], lambda b: 0, hooks)


def _merge_bwd_act(dx2, ya, yb, proj, wa, wb, wo, hooks=(), tm=512):
    S = dx2.shape[0]

    def body(dx_ref, ya_ref, yb_ref, gt_ref, wa_ref, wb_ref, wo_ref,
             dp_ref, da_ref, dr_ref, mix_ref, dxb_ref, dya_ref, dyb_ref):
        dxb = dx_ref[...].astype(CDT)
        dxb_ref[...] = dxb
        dmix = _mm_nt(dxb, wo_ref[...])
        ya = ya_ref[...].astype(f32)
        yb = yb_ref[...].astype(f32)
        sa = jax.nn.sigmoid(gt_ref[0].astype(f32))
        sb = jax.nn.sigmoid(gt_ref[1].astype(f32))
        mix_ref[...] = (sa * ya + sb * yb).astype(CDT)
        dya = (dmix * sa).astype(CDT)
        dyb = (dmix * sb).astype(CDT)
        dya_ref[...] = dya
        dyb_ref[...] = dyb
        dp_ref[0] = (dmix * ya * sa * (1.0 - sa)).astype(CDT)
        dp_ref[1] = (dmix * yb * sb * (1.0 - sb)).astype(CDT)
        da_ref[...] = _mm_nt(dya, wa_ref[...]).astype(CDT)
        dr_ref[...] = _mm_nt(dyb, wb_ref[...]).astype(CDT)

    tok = pl.BlockSpec((tm, D), lambda i: (i, 0))
    gates = pl.BlockSpec((2, tm, D), lambda i: (SLOT_GA // 2, i, 0))
    act = jax.ShapeDtypeStruct((S, D), CDT)
    return _call(
        body, hooks, [dx2, ya, yb, proj, wa, wb, wo], name="merge_bwd_act", grid=(S // tm,),
        out_shape=[jax.ShapeDtypeStruct((8, S, D), CDT), act, act, act, act, act, act],
        in_specs=[tok, tok, tok, gates, _wblock(wa), _wblock(wb), _wblock(wo)],
        out_specs=[gates, tok, tok, tok, tok, tok, tok],
        compiler_params=_params("arbitrary"),
    )


def _sgu_bwd(da, proj, dproj, gn, bn, ws, wst, bsc, hooks=(), tm=512):
    S = proj.shape[1]
    GW = D // G

    def body(da_ref, p_ref, dpin_ref, gn_ref, bn_ref, ws_ref, wst_ref, bs_ref,
             dp_ref, dws_ref, dbs_ref, dgn_ref, dbn_ref, ds_ref, dvn_ref):
        @pl.when(pl.program_id(0) == 0)
        def _():
            dws_ref[...] = jnp.zeros_like(dws_ref)
            dbs_ref[...] = jnp.zeros_like(dbs_ref)
            dgn_ref[...] = jnp.zeros_like(dgn_ref)
            dbn_ref[...] = jnp.zeros_like(dbn_ref)

        pu = p_ref[0].astype(f32)
        pv = p_ref[1].astype(f32)
        ua = _gelu(pu)
        va = _gelu(pv)
        gn = gn_ref[...]
        rstd, vhat, vn = _sgu_norm(va, gn, bn_ref[...])
        vnb = vn.astype(CDT)
        dav = da_ref[...].astype(f32)
        dsb = (dav * ua).astype(CDT)
        ones = jnp.ones((8, GW), CDT)
        for ch in range(tm // C):
            rs = slice(ch * C, (ch + 1) * C)
            for gi in range(G):
                cs = slice(gi * GW, (gi + 1) * GW)
                s = _mm(ws_ref[gi], vnb[rs, cs]) + bs_ref[gi]
                ds_ref[rs, cs] = s
                dsg = dsb[rs, cs]
                dws_ref[gi] += _mm_nt(dsg, vnb[rs, cs])
                dbs_ref[gi] += _mm_nt(ones, dsg)
                dvn_ref[rs, cs] = _mm(wst_ref[gi], dsg)
        dp_ref[0] = (dav * ds_ref[...] * _gelu_grad(pu)).astype(CDT)
        dvn = dvn_ref[...]
        dgn_ref[...] += jnp.sum(dvn * vhat, axis=0, keepdims=True)
        dbn_ref[...] += jnp.sum(dvn, axis=0, keepdims=True)
        dvh = dvn * gn
        dva = rstd * (dvh - jnp.mean(dvh, axis=-1, keepdims=True) - vhat * jnp.mean(dvh * vhat, axis=-1, keepdims=True))
        dp_ref[1] = (dva * _gelu_grad(pv)).astype(CDT)

    uv = pl.BlockSpec((2, tm, D), lambda i: (0, i, 0))
    row = _resident((1, D), lambda i: (0, 0))
    return _call(
        body, hooks, [da, proj, dproj, gn, bn, ws, wst, bsc], name="sgu_bwd", grid=(S // tm,),
        out_shape=[jax.ShapeDtypeStruct(dproj.shape, CDT), jax.ShapeDtypeStruct((G, C, C), f32),
                   jax.ShapeDtypeStruct((G, 8, C), f32), jax.ShapeDtypeStruct((1, D), f32), jax.ShapeDtypeStruct((1, D), f32)],
        in_specs=[pl.BlockSpec((tm, D), lambda i: (i, 0)), uv, _HBM, row, row,
                  _resident((G, C, C), lambda i: (0, 0, 0)), _resident((G, C, C), lambda i: (0, 0, 0)),
                  _resident((G, C, 1), lambda i: (0, 0, 0))],
        out_specs=[uv, pl.BlockSpec((G, C, C), lambda i: (0, 0, 0)), pl.BlockSpec((G, 8, C), lambda i: (0, 0, 0)),
                   pl.BlockSpec((1, D), lambda i: (0, 0)), pl.BlockSpec((1, D), lambda i: (0, 0))],
        scratch_shapes=[pltpu.VMEM((tm, D), f32), pltpu.VMEM((tm, D), f32)],
        input_output_aliases={2: 0},
        compiler_params=_params("arbitrary"),
    )


def _ret_bwd(dr, R, sfs, sbs, proj, dproj, cos, sin, dl, hooks=()):
    S = proj.shape[1]
    NC = S // RC
    assert NC % 2 == 0

    def body(dr_ref, R_ref, sf_ref, sb_ref, q_ref, k_ref, v_ref, g_ref, dpin_ref, cos_ref, sin_ref, dl_ref,
             dp_ref, dd_ref, dR_ref, gb_ref, gf_ref, acc_ref):
        t = _decay_tables(dl_ref)

        def gate_norm_bwd(i, carry):
            rs = pl.ds(pl.multiple_of(i * ROW_TILE, ROW_TILE), ROW_TILE)
            Rv = R_ref[rs, :]
            rstd = lax.rsqrt(jnp.mean(Rv * Rv, axis=-1, keepdims=True) + EPS)
            rn = Rv * rstd
            gv = g_ref[rs, :].astype(f32)
            s = jax.nn.sigmoid(gv)
            drv = dr_ref[rs, :].astype(f32)
            dp_ref[3, rs, :] = (drv * rn * (s * (1.0 + gv * (1.0 - s)))).astype(CDT)
            drn = drv * gv * s
            dR_ref[rs, :] = (rstd * (drn - rn * jnp.mean(drn * rn, axis=-1, keepdims=True))).astype(CDT)
            return carry

        lax.fori_loop(0, S // ROW_TILE, gate_norm_bwd, 0)

        def chunk(n):
            rows = pl.ds(pl.multiple_of(n * RC, RC), RC)
            return rows, q_ref[rows, :], k_ref[rows, :], v_ref[rows, :], dR_ref[rows, :]

        def emit_kv(rows, dk, dv, final):
            if not final:
                dp_ref[1, rows, :] = dk.astype(CDT)
                dp_ref[2, rows, :] = dv.astype(CDT)
            else:
                co, si = cos_ref[rows, :], sin_ref[rows, :]
                dk = dp_ref[1, rows, :].astype(f32) + dk
                dp_ref[1, rows, :] = (_unrotate(dk, co, si) * K_SCALE).astype(CDT)
                dp_ref[2, rows, :] = (dp_ref[2, rows, :].astype(f32) + dv).astype(CDT)

        gb_ref[...] = jnp.zeros_like(gb_ref)
        gf_ref[...] = jnp.zeros_like(gf_ref)
        acc_ref[...] = jnp.zeros_like(acc_ref)
        dpos = jnp.abs(t["d"])

        def ascend(n, final):
            rows, qn, kn, vn, dRn = chunk(n)
            qf, kf = qn.astype(f32), kn.astype(f32)
            sc = _mm_nt(qn, kn)
            dA = _mm_nt(dRn, vn)
            w = sc * dA * t["dmat"] * dpos
            lgf_part = jnp.sum(jnp.where(t["lower"], w, 0.0), axis=0, keepdims=True)
            lgb_part = jnp.sum(jnp.where(t["lower"], 0.0, w), axis=0, keepdims=True)
            dsc = (dA * t["dmat"]).astype(CDT)
            dq = _mm(dsc, kn)
            scT = (_mm_nt(kn, qn) * t["dmat_t"]).astype(CDT)
            dscT = (_mm_nt(vn, dRn) * t["dmat_t"]).astype(CDT)
            dk = _mm(dscT, qn)
            dv = _mm(scT, dRn)
            sfb = sf_ref[n]
            sbb = sb_ref[n]
            qdf = qf * t["fq"]
            dqdf = _mm_nt(dRn, sfb)
            dq += dqdf * t["fq"]
            lgf_row = jnp.sum(qdf * dqdf * (t["pos"] + 1.0), axis=0, keepdims=True)
            qdb = qf * t["bq"]
            dqdb = _mm_nt(dRn, sbb)
            dq += dqdb * t["bq"]
            lgb_row = jnp.sum(qdb * dqdb * (RC - t["pos"]), axis=0, keepdims=True)
            gb = gb_ref[...]
            gbb = gb.astype(CDT)
            kdb = kf * t["bk"]
            dkdb = _mm_nt(vn, gbb)
            dk += dkdb * t["bk"]
            dv += _mm(kdb.astype(CDT), gbb)
            lgb_row += jnp.sum(kdb * dkdb * t["pos"], axis=0, keepdims=True)
            lgb_row += float(RC) * t["lamb"] * jnp.sum(gb * sbb.astype(f32), axis=0, keepdims=True)
            co, si = cos_ref[rows, :], sin_ref[rows, :]
            dp_ref[0, rows, :] = _unrotate(dq, co, si).astype(CDT)
            emit_kv(rows, dk, dv, final)
            acc_ref[0:1, :] += lgf_row + lgf_part
            acc_ref[1:2, :] += lgb_row + lgb_part
            gb_ref[...] = gb * t["lamb"] + _mm_tn(qdb.astype(CDT), dRn)

        def descend(n, final):
            rows, qn, kn, vn, dRn = chunk(n)
            gf = gf_ref[...]
            gfb = gf.astype(CDT)
            kdf = kn.astype(f32) * t["fk"]
            dkdf = _mm_nt(vn, gfb)
            lgf_row = jnp.sum(kdf * dkdf * (RC - 1.0 - t["pos"]), axis=0, keepdims=True)
            lgf_row += float(RC) * t["lamf"] * jnp.sum(gf * sf_ref[n].astype(f32), axis=0, keepdims=True)
            acc_ref[0:1, :] += lgf_row
            emit_kv(rows, dkdf * t["fk"], _mm(kdf.astype(CDT), gfb), final)
            gf_ref[...] = gf * t["lamf"] + _mm_tn((qn.astype(f32) * t["fq"]).astype(CDT), dRn)

        def sweep(final):
            def step(i, carry):
                ascend(i, final)
                descend(NC - 1 - i, final)
                return carry
            return step

        lax.fori_loop(0, NC // 2, sweep(False), 0)
        lax.fori_loop(NC // 2, NC, sweep(True), 0)
        dlg = jnp.sum(acc_ref[...], axis=1, keepdims=True)
        dlogit = dlg * jax.nn.sigmoid(-dl_ref[:, 0:1])
        lane = lax.broadcasted_iota(jnp.int32, (8, 128), 1)
        dd_ref[...] = jnp.where(lane == pl.program_id(0), jnp.broadcast_to(dlogit, (8, 128)), 0.0)

    def seg(slot):
        return pl.BlockSpec((None, S, DK), lambda h: (slot, 0, h))

    head = pl.BlockSpec((S, DK), lambda h: (0, h))
    states = pl.BlockSpec((None, NC, DK, DK), lambda h: (h, 0, 0, 0))
    return _call(
        body, hooks, [dr, R, sfs, sbs, proj, proj, proj, proj, dproj, cos, sin, dl], name="ret_bwd", grid=(H,),
        out_shape=[jax.ShapeDtypeStruct(dproj.shape, CDT), jax.ShapeDtypeStruct((H, 8, 128), f32)],
        in_specs=[head, head, states, states, seg(SLOT_Q), seg(SLOT_K), seg(SLOT_VR), seg(SLOT_GR), _HBM,
                  _resident((S, DK // 2), lambda h: (0, 0)), _resident((S, DK // 2), lambda h: (0, 0)),
                  pl.BlockSpec((None, 8, DK), lambda h: (h, 0, 0))],
        out_specs=[pl.BlockSpec((4, S, DK), lambda h: (1, 0, h), pipeline_mode=pl.Buffered(1)),
                   pl.BlockSpec((None, 8, 128), lambda h: (h, 0, 0))],
        scratch_shapes=[pltpu.VMEM((S, DK), CDT),
                        pltpu.VMEM((DK, DK), f32), pltpu.VMEM((DK, DK), f32), pltpu.VMEM((8, DK), f32)],
        input_output_aliases={8: 0},
        compiler_params=_params("arbitrary"),
    )


def _proj_bwd_act(dproj, dx2, x1, nrm, wfull, hooks=(), tm=512):
    S = x1.shape[0]

    def body(dp_ref, dx2_ref, x_ref, n_ref, w_ref, dx_ref, dxh_ref, dn_ref, db_ref, acc_ref):
        @pl.when(pl.program_id(0) == 0)
        def _():
            dn_ref[...] = jnp.zeros_like(dn_ref)
            db_ref[...] = jnp.zeros_like(db_ref)

        for p in range(8):
            seg = SEG_OF_SLOT[p]
            dp = dp_ref[p]
            db_ref[seg] += jnp.sum(dp.astype(f32), axis=0, keepdims=True)
            dh = _mm_nt(dp, w_ref[seg * D:(seg + 1) * D, :])
            if p == 0:
                acc_ref[...] = dh
            else:
                acc_ref[...] += dh
        n = n_ref[...]
        r, xh, _ = _rms_fwd(x_ref[...], n)
        dx, dn = _rms_bwd(acc_ref[...], r, xh, n)
        dx = dx2_ref[...] + dx
        dx_ref[...] = dx
        dxh_ref[...] = (0.5 * dx).astype(CDT)
        dn_ref[...] += dn

    tok = pl.BlockSpec((tm, D), lambda i: (i, 0))
    return _call(
        body, hooks, [dproj, dx2, x1, nrm, wfull], name="proj_bwd_act", grid=(S // tm,),
        out_shape=[jax.ShapeDtypeStruct((S, D), f32), jax.ShapeDtypeStruct((S, D), CDT), jax.ShapeDtypeStruct((1, D), f32),
                   jax.ShapeDtypeStruct((8, 1, D), f32)],
        in_specs=[pl.BlockSpec((8, tm, D), lambda i: (0, i, 0)), tok, tok, _resident((1, D), lambda i: (0, 0)),
                  _resident((8 * D, D), lambda i: (0, 0))],
        out_specs=[tok, tok, pl.BlockSpec((1, D), lambda i: (0, 0)), pl.BlockSpec((8, 1, D), lambda i: (0, 0, 0))],
        scratch_shapes=[pltpu.VMEM((tm, D), f32)],
        compiler_params=_params("arbitrary"),
    )


def _rs_sum(name, gfulls, lands, my_c):
    n = len(gfulls)
    rows = gfulls[0].shape[0] // NDEV
    assert all(g.shape[0] == NDEV * rows for g in gfulls)

    def body(c_ref, *refs):
        for g_ref, l_ref, o_ref in zip(refs[:n], refs[n:2 * n], refs[2 * n:]):
            o_ref[...] = (g_ref[...].astype(f32) + l_ref[...].astype(f32)).astype(CDT)

    slot = pl.BlockSpec((None, rows, D), lambda k, c: (k, 0, 0))
    return pl.pallas_call(
        body, name=name,
        grid_spec=pltpu.PrefetchScalarGridSpec(
            num_scalar_prefetch=1, grid=(NCHIP,),
            in_specs=[pl.BlockSpec((rows, D), lambda k, c: (2 * k + c[0], 0))] * n + [slot] * n,
            out_specs=[slot] * n),
        out_shape=[jax.ShapeDtypeStruct((NCHIP, rows, D), CDT)] * n,
        compiler_params=_params("arbitrary"),
    )(my_c, *gfulls, *lands)


def _adamw_math(g, w, m, v):
    m2 = ADAM_B1 * m + (1.0 - ADAM_B1) * g
    v2 = ADAM_B2 * v + (1.0 - ADAM_B2) * (g * g)
    delta = -ADAM_LR * ((m2 / BC1) / (jnp.sqrt(v2 / BC2) + ADAM_EPS) + ADAM_WD * w)
    return delta, m2, v2


def _adamw_big(name, landed, w, m, v, hooks=()):
    rows = w.shape[0]
    tr = min(rows, 256) if rows % 256 == 0 else rows

    def body(l_ref, w_ref, m_ref, v_ref, g_ref, d_ref, m2_ref, v2_ref):
        g = l_ref[0].astype(f32)
        for k in range(1, NCHIP):
            g = g + l_ref[k].astype(f32)
        g_ref[...] = g
        d_ref[...], m2_ref[...], v2_ref[...] = _adamw_math(g, w_ref[...], m_ref[...], v_ref[...])

    blk = pl.BlockSpec((tr, D), lambda i: (i, 0))
    o = jax.ShapeDtypeStruct((rows, D), f32)
    return _call(
        body, hooks, [landed, w, m, v], name=name, grid=(rows // tr,), out_shape=[o, o, o, o],
        in_specs=[pl.BlockSpec((NCHIP, tr, D), lambda i: (0, i, 0)), blk, blk, blk],
        out_specs=[blk, blk, blk, blk],
        compiler_params=_params("arbitrary"),
    )


ROW_FFN1_NORM, ROW_MIX_NORM, ROW_SGU_G, ROW_SGU_B, ROW_FFN2_NORM, ROW_FINAL_NORM, ROW_B_IN = 0, 1, 2, 3, 4, 5, 8
ROW_WS, ROW_BS, ROW_DECAY = 0, G * C, G * C + G * 8


def _adamw_small(ga, gb, gn1, gl, params):
    def body(ga_ref, gb_ref, gn1_ref, gl_ref, *refs):
        ins, outs = refs[:30], refs[30:]

        def total(ref, r0, n):
            g = ref[0, r0:r0 + n, :]
            for j in range(1, NDEV):
                g = g + ref[j, r0:r0 + n, :]
            return g

        def apply(i, g, rows=slice(None)):
            w, m, v = ins[3 * i][rows, :], ins[3 * i + 1][rows, :], ins[3 * i + 2][rows, :]
            outs[4 * i][rows, :] = g
            outs[4 * i + 1][rows, :], outs[4 * i + 2][rows, :], outs[4 * i + 3][rows, :] = _adamw_math(g, w, m, v)

        outs[40][...] = total(gl_ref, 0, 8)
        apply(0, total(gn1_ref, 0, 1))
        for i, r in enumerate((ROW_FFN1_NORM, ROW_MIX_NORM, ROW_SGU_G, ROW_SGU_B, ROW_FFN2_NORM, ROW_FINAL_NORM)):
            if i:
                apply(i, total(ga_ref, r, 1))
        apply(6, total(ga_ref, ROW_B_IN, 8))
        apply(7, total(gb_ref, ROW_WS, G * C))
        for gi in range(G):
            apply(8, total(gb_ref, ROW_BS + 8 * gi, 1), slice(gi, gi + 1))
        dec = total(gb_ref, ROW_DECAY, 8)
        for hh in range(1, H):
            dec = dec + total(gb_ref, ROW_DECAY + 8 * hh, 8)
        apply(9, dec)

    flat = [a for p in params for a in p]
    out_shape = [jax.ShapeDtypeStruct(p[0].shape, f32) for p in params for _ in range(4)]
    out_shape.append(jax.ShapeDtypeStruct((8, 128), f32))
    vm = pl.BlockSpec(memory_space=pltpu.VMEM)
    return pl.pallas_call(
        body, name="adamw_small", out_shape=out_shape,
        in_specs=[vm] * (4 + len(flat)), out_specs=[vm] * len(out_shape),
        compiler_params=pltpu.CompilerParams(vmem_limit_bytes=VMEM_LIMIT),
    )(ga, gb, gn1, gl, *flat)


def kernel(x, ffn1_norm, ffn1_w_gate, ffn1_w_up, ffn1_w_down, mix_norm, w_in, b_in, sgu_norm_g, sgu_norm_b, sgu_w_s, sgu_b_s, ret_decay_logit, w_branch_a, w_branch_b, w_out, ffn2_norm, ffn2_w_gate, ffn2_w_up, ffn2_w_down, final_norm, loss_target, m_ffn1_norm, m_ffn1_w_gate, m_ffn1_w_up, m_ffn1_w_down, m_mix_norm, m_w_in, m_b_in, m_sgu_norm_g, m_sgu_norm_b, m_sgu_w_s, m_sgu_b_s, m_ret_decay_logit, m_w_branch_a, m_w_branch_b, m_w_out, m_ffn2_norm, m_ffn2_w_gate, m_ffn2_w_up, m_ffn2_w_down, m_final_norm, v_ffn1_norm, v_ffn1_w_gate, v_ffn1_w_up, v_ffn1_w_down, v_mix_norm, v_w_in, v_b_in, v_sgu_norm_g, v_sgu_norm_b, v_sgu_w_s, v_sgu_b_s, v_ret_decay_logit, v_w_branch_a, v_w_branch_b, v_w_out, v_ffn2_norm, v_ffn2_w_gate, v_ffn2_w_up, v_ffn2_w_down, v_final_norm):
    args = dict(locals())
    S = x.shape[1]
    xs = x[0]
    target = loss_target[0]

    def buf_layout(name, a):
        a = a[0]
        return a.T if name in W_TRANSPOSED else a

    sh = {n: buf_layout(n, args[n]).astype(CDT) for n in W_NAMES}
    wf = {}

    b3 = b_in.reshape(8, 1, D)
    ws = sgu_w_s[0].astype(CDT)
    wst = jnp.swapaxes(sgu_w_s[0], 1, 2).astype(CDT)
    bsc = sgu_b_s[0].reshape(G, C, 1)
    dl = jnp.zeros((H, 8, DK), f32).at[:, 0:2, :].set(jnp.broadcast_to(ret_decay_logit[0].T[:, :, None], (H, 2, DK)))
    theta = ROPE_BASE ** (-jnp.arange(0, DK, 2, dtype=f32) / DK)
    ang = jnp.arange(S, dtype=f32)[:, None] * theta[None, :]
    cos, sin = jnp.cos(ang), jnp.sin(ang)
    fnorm = final_norm.reshape(1, D)

    f1 = ("ffn1_w_gate", "ffn1_w_up", "ffn1_w_down")
    f2 = ("ffn2_w_gate", "ffn2_w_up", "ffn2_w_down")
    br = ("w_branch_a", "w_branch_b", "w_out")
    for cid, names in enumerate((f1, ("w_in",), br, f2)):
        wf.update(zip(names, _sequence("ag_" + names[0], 1 + cid, SIBLING_AND_CHIPS, [_ag_hook(sh[n]) for n in names])))
    x1, g1, u1, a1, hf1 = _ffn_fwd("ffn1_fwd", xs, ffn1_norm, *[wf[n] for n in f1])
    proj, h2 = _proj_fwd(x1, mix_norm, wf["w_in"], b3, cos, sin)
    a = _sgu_fwd(proj, sgu_norm_g, sgu_norm_b, ws, bsc)
    R, r, sfs, sbs = _ret_fwd(proj, dl)
    x2, ya, yb = _merge_fwd(a, r, proj, x1, *[wf[n] for n in br])
    x3, g2, u2, a2, hf2 = _ffn_fwd("ffn2_fwd", x2, ffn2_norm, *[wf[n] for n in f2])
    dx3, dyh2, d_final, loss_part = _loss_head(x3, fnorm, target)

    my_c = lax.axis_index("c").astype(jnp.int32).reshape(1)
    gw, landed = {}, {}

    def to_sibling(names):
        return _sequence("rs_sib_" + names[0], REACH_ID[SIBLING], SIBLING, [_rs_d2d_hook(gw[n]) for n in names])

    def to_chips(names, sibs):
        parts = _rs_sum("rs_sum_" + names[0], [gw[n] for n in names], list(sibs), my_c)
        landed.update(zip(names, _sequence("rs_chips_" + names[0], REACH_ID[CHIPS], CHIPS, [_rs_ici_hook(p) for p in parts])))

    def gather_small(tag, arrays):
        return _sequence("ag_small_" + tag, REACH_ID[EVERYONE], EVERYONE, [_small_hook(arrays)])

    def ffn_bwd(tag, names, dy, dyh, x, g, u, a, h, nrm, each_alone):
        wg, wu, wd = names
        (gw[wd],) = _wgrad(tag + "_wd_grad", a, dyh)
        if each_alone:
            sib_d = to_sibling([wd])
        dg, du = _ffn_bwd_hidden(tag + "_bwd_hidden", dyh, g, u, wf[wd])
        if each_alone:
            to_chips([wd], sib_d)
        (gw[wg],) = _wgrad(tag + "_wg_grad", dg, h)
        if each_alone:
            sib_g = to_sibling([wg])
        (gw[wu],) = _wgrad(tag + "_wu_grad", du, h)
        if each_alone:
            to_chips([wg], sib_g)
            to_chips([wu], to_sibling([wu]))
        else:
            sibs = to_sibling([wd, wg, wu])
        dx, dn = _ffn_bwd_in(tag + "_bwd_in", dy, x, dg, du, nrm, wf[wg], wf[wu])
        if not each_alone:
            to_chips([wd, wg, wu], sibs)
        return dx, dn

    dx2, d_ffn2n = ffn_bwd("ffn2", f2, dx3, dyh2, x2, g2, u2, a2, hf2, ffn2_norm, False)
    dproj, da, dr, mix, dx2b, dya, dyb = _merge_bwd_act(dx2, ya, yb, proj, *[wf[n] for n in br])
    (gw["w_out"],) = _wgrad("wo_grad", mix, dx2b)
    (gw["w_branch_a"],) = _wgrad("wa_grad", a, dya)
    (gw["w_branch_b"],) = _wgrad("wb_grad", r, dyb)
    sib_br = to_sibling(list(br))
    dproj, d_ws, d_bs, d_gn, d_bn = _sgu_bwd(da, proj, dproj, sgu_norm_g, sgu_norm_b, ws, wst, bsc)
    to_chips(list(br), sib_br)
    g_sgu, gl = gather_small("sgu", [jnp.concatenate([d_ws.reshape(G * C, C), d_bs.reshape(G * 8, C)], axis=0), loss_part])
    dproj, d_dec = _ret_bwd(dr, R, sfs, sbs, proj, dproj, cos, sin, dl)
    (gw["w_in"],) = _tn("win_grad", h2, dproj, _seg_of_slot)
    sib_win = to_sibling(["w_in"])
    dx1, dyh1, d_mixn, d_bin = _proj_bwd_act(dproj, dx2, x1, mix_norm, wf["w_in"])
    to_chips(["w_in"], sib_win)
    small_a = jnp.concatenate([jnp.zeros((1, D), f32), d_mixn, d_gn, d_bn, d_ffn2n, d_final, jnp.zeros((2, D), f32),
                               d_bin.reshape(8, D)], axis=0)
    ga, g_dec = gather_small("rest", [small_a, d_dec.reshape(H * 8, 128)])
    gb = jnp.concatenate([g_sgu, g_dec], axis=1)
    dxs, d_ffn1n = ffn_bwd("ffn1", f1, dx1, dyh1, xs, g1, u1, a1, hf1, ffn1_norm, True)
    (gn1,) = _exchange("ag_ffn1_norm", [_small_hook([d_ffn1n])])

    out = {"grad_x": dxs[None]}

    def native(name, a):
        a = a.T if name in W_TRANSPOSED else a
        return a[None]

    for n in W_NAMES:
        res = _adamw_big("adamw_" + n, landed[n], buf_layout(n, args[n]), buf_layout(n, args["m_" + n]),
                         buf_layout(n, args["v_" + n]))
        for pre, val in zip(("grad_", "delta_", "new_m_", "new_v_"), res):
            out[pre + n] = native(n, val)

    def pad_decay(a):
        return jnp.zeros((8, 128), f32).at[0:2, 0:H].set(a[0])

    small = [
        ("ffn1_norm", lambda a: a, lambda a: a), ("mix_norm", lambda a: a, lambda a: a),
        ("sgu_norm_g", lambda a: a, lambda a: a), ("sgu_norm_b", lambda a: a, lambda a: a),
        ("ffn2_norm", lambda a: a, lambda a: a),
        ("final_norm", lambda a: a.reshape(1, D), lambda a: a.reshape(D)),
        ("b_in", lambda a: a.reshape(8, D), lambda a: a.reshape(1, 8 * D)),
        ("sgu_w_s", lambda a: a.reshape(G * C, C), lambda a: a.reshape(1, G, C, C)),
        ("sgu_b_s", lambda a: a[0], lambda a: a[None]),
        ("ret_decay_logit", pad_decay, lambda a: a[None, 0:2, 0:H]),
    ]
    res = _adamw_small(ga, gb, gn1, gl, [(to(args[n]), to(args["m_" + n]), to(args["v_" + n])) for n, to, _ in small])
    out["loss"] = res[40][0, 0]
    for i, (n, _, back) in enumerate(small):
        for j, pre in enumerate(("grad_", "delta_", "new_m_", "new_v_")):
            out[pre + n] = back(res[4 * i + j])

    weights = ("ffn1_norm", "ffn1_w_gate", "ffn1_w_up", "ffn1_w_down", "mix_norm", "w_in", "b_in", "sgu_norm_g",
               "sgu_norm_b", "sgu_w_s", "sgu_b_s", "ret_decay_logit", "w_branch_a", "w_branch_b", "w_out", "ffn2_norm",
               "ffn2_w_gate", "ffn2_w_up", "ffn2_w_down", "final_norm")
    return (out["loss"], out["grad_x"], *[out["grad_" + n] for n in weights], *[out["delta_" + n] for n in weights],
            *[out["new_m_" + n] for n in weights], *[out["new_v_" + n] for n in weights])
```

```python
import functools
import math

import jax
import jax.numpy as jnp
from jax import lax
from jax.experimental import pallas as pl
from jax.experimental.pallas import tpu as pltpu
from jax.experimental.pallas import tpu_sc as plsc

f32 = jnp.float32
CDT = jnp.bfloat16

D = 1024
F = 2816
C = 128
RC = 256
H = 4
DK = 256
G = 4
NDEV = 8
NCHIP = 4
EPS = 1e-6
ROPE_BASE = 10000.0
FT = 256
V7X_VMEM_BYTES = 64 * 1024 * 1024
VMEM_LIMIT = V7X_VMEM_BYTES - 8 * 1024 * 1024

ADAM_LR, ADAM_B1, ADAM_B2, ADAM_EPS, ADAM_WD, ADAM_STEP = 0.001, 0.9, 0.999, 1e-08, 0.01, 10
BC1 = 1.0 - ADAM_B1 ** ADAM_STEP
BC2 = 1.0 - ADAM_B2 ** ADAM_STEP

W_ROWS = dict(ffn1_w_gate=352, ffn1_w_up=352, ffn1_w_down=352, w_in=1024, w_branch_a=128, w_branch_b=128, w_out=128,
              ffn2_w_gate=352, ffn2_w_up=352, ffn2_w_down=352)
W_NAMES = tuple(W_ROWS)
W_TRANSPOSED = ("ffn1_w_gate", "ffn1_w_up", "ffn2_w_gate", "ffn2_w_up")

SLOT_U, SLOT_V, SLOT_GA, SLOT_GB, SLOT_Q, SLOT_K, SLOT_VR, SLOT_GR = range(8)


SEG_OF_SLOT = (0, 1, 6, 7, 2, 3, 4, 5)


def _seg_of_slot(p):
    return jnp.where(p < 2, p, jnp.where(p < 4, p + 4, p - 2))


def _mm(a, b):
    return jnp.dot(a, b, preferred_element_type=f32)


def _mm_nt(a, b):
    return lax.dot_general(a, b, (((1,), (1,)), ((), ())), preferred_element_type=f32)


def _mm_tn(a, b):
    return lax.dot_general(a, b, (((0,), (0,)), ((), ())), preferred_element_type=f32)


def _params(*sem):
    return pltpu.CompilerParams(dimension_semantics=sem, vmem_limit_bytes=VMEM_LIMIT)


def _resident(shape, index_map):
    return pl.BlockSpec(shape, index_map, pipeline_mode=pl.Buffered(1))


def _gelu(x):
    return 0.5 * x * (1.0 + lax.erf(x * (1.0 / math.sqrt(2.0))))


def _gelu_grad(x):
    return 0.5 * (1.0 + lax.erf(x * (1.0 / math.sqrt(2.0)))) + x * jnp.exp(-0.5 * x * x) * (1.0 / math.sqrt(2.0 * math.pi))


def _rms_fwd(x, n):
    r = lax.rsqrt(jnp.mean(x * x, axis=-1, keepdims=True) + EPS)
    xh = x * r
    return r, xh, xh * n


def _rms_bwd(dh, r, xh, n):
    dxh = dh * n
    dx = r * (dxh - xh * jnp.mean(dxh * xh, axis=-1, keepdims=True))
    return dx, jnp.sum(dh * xh, axis=0, keepdims=True)


MESH_ID = pl.DeviceIdType.MESH
_HBM = pl.BlockSpec(memory_space=pltpu.HBM)


def _my_place():
    return lax.axis_index("x"), lax.axis_index("y"), lax.axis_index("c")


def _ici_peers(x, y, c):
    return [((1 - x, y, c), 2 * (1 - x) + y), ((x, 1 - y, c), 2 * x + 1 - y), ((1 - x, 1 - y, c), 2 * (1 - x) + 1 - y)]


class _Hook:
    def __init__(self, operands, out_shapes, n_remote, n_local, start, finish, relay=None):
        self.operands, self.out_shapes = list(operands), list(out_shapes)
        self.n_remote, self.n_local, self.start, self.finish = n_remote, n_local, start, finish
        self.relay = relay or (lambda *a: None)


def _call(body, hooks, operands, *, in_specs, out_specs, out_shape, grid=None, scratch_shapes=(), **kw):
    hooks = tuple(hooks)
    n_in, n_out, n_scr = len(in_specs), len(out_shape), len(scratch_shapes)
    h_ops = [a for h in hooks for a in h.operands]
    h_outs = [s for h in hooks for s in h.out_shapes]
    h_sems = [pltpu.SemaphoreType.DMA((n,)) for h in hooks for n in (h.n_remote, h.n_remote, max(h.n_local, 1))]

    def wrapped(*refs):
        ins, hin = refs[:n_in], refs[n_in:n_in + len(h_ops)]
        o0 = n_in + len(h_ops)
        outs, hout = refs[o0:o0 + n_out], refs[o0 + n_out:o0 + n_out + len(h_outs)]
        s0 = o0 + n_out + len(h_outs)
        scr, hsem = refs[s0:s0 + n_scr], refs[s0 + n_scr:]

        def run(phase):
            ip = op = 0
            for i, h in enumerate(hooks):
                ssem, rsem, lsem = hsem[3 * i:3 * i + 3]

                def rcopy(k, src, dst, dev, ssem=ssem, rsem=rsem):
                    return pltpu.make_async_remote_copy(src_ref=src, dst_ref=dst, send_sem=ssem.at[k], recv_sem=rsem.at[k],
                                                        device_id=dev, device_id_type=MESH_ID)

                def lcopy(k, src, dst, lsem=lsem):
                    return pltpu.make_async_copy(src, dst, lsem.at[k])

                getattr(h, phase)(hin[ip:ip + len(h.operands)], hout[op:op + len(h.out_shapes)], rcopy, lcopy)
                ip += len(h.operands)
                op += len(h.out_shapes)

        def at_edge(phase, last):
            if not hooks:
                return
            if grid is None:
                run(phase)
                return
            cond = None
            for ax, n in enumerate(grid):
                here = pl.program_id(ax) == (n - 1 if last else 0)
                cond = here if cond is None else cond & here
            pl.when(cond)(lambda: run(phase))

        at_edge("start", False)
        at_edge("relay", True)
        body(*ins, *outs, *scr)
        at_edge("finish", True)

    if grid is not None:
        kw["grid"] = grid
    return list(pl.pallas_call(
        wrapped, out_shape=list(out_shape) + h_outs, in_specs=list(in_specs) + [_HBM] * len(h_ops),
        out_specs=list(out_specs) + [_HBM] * len(h_outs), scratch_shapes=list(scratch_shapes) + h_sems, **kw,
    )(*operands, *h_ops))


def _exchange(name, hooks):
    return _call(lambda: None, hooks, [], name=name, in_specs=[], out_specs=[], out_shape=[])


def _rows(ref, start, n):
    return ref.at[pl.ds(start, n), :]


def _ag_hook(shard, early=True):
    rows = shard.shape[0]

    def block(full, dev_index):
        return _rows(full, dev_index * rows, rows)

    def start(ins, outs, rcopy, lcopy):
        x, y, c = _my_place()
        src, dst = ins[0], block(outs[0], 4 * x + 2 * y + c)
        lcopy(0, src, dst).start()
        rcopy(0, src, dst, (x, y, 1 - c)).start()
        for j, (dev, _) in enumerate(_ici_peers(x, y, c)):
            rcopy(1 + j, src, dst, dev).start()

    def relay(ins, outs, rcopy, lcopy):
        x, y, c = _my_place()
        for j, (dev, chip) in enumerate(_ici_peers(x, y, c)):
            blk = block(outs[0], 2 * chip + c)
            rcopy(1 + j, blk, blk, dev).wait_recv()
            rcopy(4 + j, blk, blk, (x, y, 1 - c)).start()

    def finish(ins, outs, rcopy, lcopy):
        if not early:
            relay(ins, outs, rcopy, lcopy)
        x, y, c = _my_place()
        sib = (x, y, 1 - c)
        full = outs[0]
        peers = _ici_peers(x, y, c)
        blk = block(full, 2 * (2 * x + y) + 1 - c)
        rcopy(0, blk, blk, sib).wait_recv()
        for j, (dev, chip) in enumerate(peers):
            blk = block(full, 2 * chip + 1 - c)
            rcopy(4 + j, blk, blk, sib).wait_recv()
        src, dst = ins[0], block(full, 4 * x + 2 * y + c)
        lcopy(0, src, dst).wait()
        rcopy(0, src, dst, sib).wait_send()
        for j, (dev, chip) in enumerate(peers):
            rcopy(1 + j, src, dst, dev).wait_send()
            blk = block(full, 2 * chip + c)
            rcopy(4 + j, blk, blk, sib).wait_send()

    return _Hook([shard], [jax.ShapeDtypeStruct((NDEV * rows, D), shard.dtype)], 7, 1, start, finish, relay if early else None)


SIBLING, CHIPS, SIBLING_AND_CHIPS, EVERYONE = "sibling", "chips", "sibling and chips", "everyone"
REACH_ID = {SIBLING: 5, CHIPS: 6, EVERYONE: 7}


def _sequence(name, collective_id, reach, hooks):
    ins = [[jax.new_ref(a, memory_space=pltpu.MemorySpace.HBM) for a in h.operands] for h in hooks]
    outs = [[jax.empty_ref(s, memory_space=pltpu.MemorySpace.HBM) for s in h.out_shapes] for h in hooks]
    sems = tuple(pltpu.SemaphoreType.DMA((n,)) for h in hooks for n in (h.n_remote, h.n_remote, max(h.n_local, 1)))

    @pl.kernel(mesh=plsc.ScalarSubcoreMesh(axis_name="sequencer", num_cores=1), name=name, scratch_types=sems,
               compiler_params=pltpu.CompilerParams(collective_id=collective_id))
    def launch(*sem_refs):
        x, y, c = _my_place()
        chips = [dev for dev, _ in _ici_peers(x, y, c)]
        others = [(1 - x if dx else x, 1 - y if dy else y, 1 - c if dc else c)
                  for dx in range(2) for dy in range(2) for dc in range(2) if dx + dy + dc]
        devs = {SIBLING: [(x, y, 1 - c)], CHIPS: chips, SIBLING_AND_CHIPS: [(x, y, 1 - c)] + chips, EVERYONE: others}[reach]
        barrier = pltpu.get_barrier_semaphore()
        for dev in devs:
            pl.semaphore_signal(barrier, inc=1, device_id=dev, device_id_type=MESH_ID)
        pl.semaphore_wait(barrier, len(devs))
        for phase in ("start", "relay", "finish"):
            for i, h in enumerate(hooks):
                ssem, rsem, lsem = sem_refs[3 * i:3 * i + 3]

                def rcopy(k, src, dst, dev, ssem=ssem, rsem=rsem):
                    return pltpu.make_async_remote_copy(src_ref=src, dst_ref=dst, send_sem=ssem.at[k], recv_sem=rsem.at[k],
                                                        device_id=dev, device_id_type=MESH_ID)

                def lcopy(k, src, dst, lsem=lsem):
                    return pltpu.make_async_copy(src, dst, lsem.at[k])

                getattr(h, phase)(ins[i], outs[i], rcopy, lcopy)

    launch()
    return [o[...] for os in outs for o in os]


def _rs_d2d_hook(gfull):
    rows = gfull.shape[0] // NDEV

    def pairs(g, land):
        x, y, c = _my_place()
        return (x, y, 1 - c), [(k, _rows(g, (2 * k + 1 - c) * rows, rows), land.at[k]) for k in range(NCHIP)]

    def start(ins, outs, rcopy, lcopy):
        sib, cps = pairs(ins[0], outs[0])
        for i, src, dst in cps:
            rcopy(i, src, dst, sib).start()

    def finish(ins, outs, rcopy, lcopy):
        sib, cps = pairs(ins[0], outs[0])
        for i, src, dst in cps:
            rcopy(i, dst, dst, sib).wait_recv()
        for i, src, dst in cps:
            rcopy(i, src, dst, sib).wait_send()

    return _Hook([gfull], [jax.ShapeDtypeStruct((NCHIP, rows, D), gfull.dtype)], NCHIP, 0, start, finish)


def _rs_ici_hook(part):
    def start(ins, outs, rcopy, lcopy):
        x, y, c = _my_place()
        mychip = 2 * x + y
        lcopy(0, ins[0].at[mychip], outs[0].at[mychip]).start()
        for j, (dev, chip) in enumerate(_ici_peers(x, y, c)):
            rcopy(j, ins[0].at[chip], outs[0].at[mychip], dev).start()

    def finish(ins, outs, rcopy, lcopy):
        x, y, c = _my_place()
        mychip = 2 * x + y
        peers = _ici_peers(x, y, c)
        for j, (dev, chip) in enumerate(peers):
            rcopy(j, outs[0].at[chip], outs[0].at[chip], dev).wait_recv()
        for j, (dev, chip) in enumerate(peers):
            rcopy(j, ins[0].at[chip], outs[0].at[mychip], dev).wait_send()
        lcopy(0, ins[0].at[mychip], outs[0].at[mychip]).wait()

    return _Hook([part], [jax.ShapeDtypeStruct(part.shape, part.dtype)], 3, 1, start, finish)


def _small_hook(arrays):
    n = len(arrays)

    def peers():
        x, y, c = _my_place()
        out = []
        for dx in range(2):
            for dy in range(2):
                for dc in range(2):
                    if dx + dy + dc:
                        px, py, pc = (1 - x if dx else x), (1 - y if dy else y), (1 - c if dc else c)
                        out.append(((px, py, pc), 4 * px + 2 * py + pc))
        return 4 * x + 2 * y + c, out

    def start(ins, outs, rcopy, lcopy):
        me, ps = peers()
        for t in range(n):
            lcopy(t, ins[t], outs[t].at[me]).start()
            for i, (dev, _) in enumerate(ps):
                rcopy(n * i + t, ins[t], outs[t].at[me], dev).start()

    def finish(ins, outs, rcopy, lcopy):
        me, ps = peers()
        for t in range(n):
            for i, (dev, peer) in enumerate(ps):
                rcopy(n * i + t, outs[t].at[peer], outs[t].at[peer], dev).wait_recv()
            for i, (dev, _) in enumerate(ps):
                rcopy(n * i + t, ins[t], outs[t].at[me], dev).wait_send()
            lcopy(t, ins[t], outs[t].at[me]).wait()

    return _Hook(arrays, [jax.ShapeDtypeStruct((NDEV,) + a.shape, a.dtype) for a in arrays], 7 * n, n, start, finish)


def _wblock(w):
    return _resident(w.shape, lambda *_: (0, 0))


def _ffn_fwd(name, x, nrm, wg, wu, wd, hooks=(), tm=512):
    S = x.shape[0]

    def body(x_ref, n_ref, wg_ref, wu_ref, wd_ref, y_ref, g_ref, u_ref, a_ref, h_ref, acc_ref):
        xv = x_ref[...]
        _, _, h = _rms_fwd(xv, n_ref[...])
        h = h.astype(CDT)
        h_ref[...] = h
        for ci in range(F // FT):
            sl = slice(ci * FT, (ci + 1) * FT)
            g = _mm_nt(h, wg_ref[sl, :])
            u = _mm_nt(h, wu_ref[sl, :])
            g_ref[:, sl] = g.astype(CDT)
            u_ref[:, sl] = u.astype(CDT)
            a = (g * jax.nn.sigmoid(g) * u).astype(CDT)
            a_ref[:, sl] = a
            o = _mm(a, wd_ref[sl, :])
            if ci == 0:
                acc_ref[...] = o
            else:
                acc_ref[...] += o
        y_ref[...] = xv + 0.5 * acc_ref[...]

    tok = pl.BlockSpec((tm, D), lambda i: (i, 0))
    hid = pl.BlockSpec((tm, F), lambda i: (i, 0))
    hidden = jax.ShapeDtypeStruct((S, F), CDT)
    return _call(
        body, hooks, [x, nrm, wg, wu, wd], name=name, grid=(S // tm,),
        out_shape=[jax.ShapeDtypeStruct((S, D), f32), hidden, hidden, hidden, jax.ShapeDtypeStruct((S, D), CDT)],
        in_specs=[tok, _resident((1, D), lambda i: (0, 0)), _wblock(wg), _wblock(wu), _wblock(wd)],
        out_specs=[tok, hid, hid, hid, tok],
        scratch_shapes=[pltpu.VMEM((tm, D), f32)],
        compiler_params=_params("arbitrary"),
    )


def _proj_fwd(x1, nrm, wfull, b3, cos, sin, hooks=(), tm=512):
    S = x1.shape[0]

    def body(x_ref, n_ref, w_ref, b_ref, cos_ref, sin_ref, p_ref, h_ref):
        _, _, h = _rms_fwd(x_ref[...], n_ref[...])
        h = h.astype(CDT)
        h_ref[...] = h
        for p in range(8):
            seg = SEG_OF_SLOT[p]
            z = _mm(h, w_ref[seg * D:(seg + 1) * D, :]) + b_ref[seg]
            if p in (SLOT_Q, SLOT_K):
                co, si = cos_ref[...], sin_ref[...]
                for hh in range(H):
                    cs = slice(hh * DK, (hh + 1) * DK)
                    zr = _rotate(z[:, cs], co, si)
                    p_ref[p, :, cs] = (zr * K_SCALE if p == SLOT_K else zr).astype(CDT)
            else:
                p_ref[p] = z.astype(CDT)

    tab = pl.BlockSpec((tm, DK // 2), lambda i: (i, 0))
    return _call(
        body, hooks, [x1, nrm, wfull, b3, cos, sin], name="proj_fwd", grid=(S // tm,),
        out_shape=[jax.ShapeDtypeStruct((8, S, D), CDT), jax.ShapeDtypeStruct((S, D), CDT)],
        in_specs=[pl.BlockSpec((tm, D), lambda i: (i, 0)), _resident((1, D), lambda i: (0, 0)),
                  _resident((8 * D, D), lambda i: (0, 0)), _resident((8, 1, D), lambda i: (0, 0, 0)), tab, tab],
        out_specs=[pl.BlockSpec((8, tm, D), lambda i: (0, i, 0)), pl.BlockSpec((tm, D), lambda i: (i, 0))],
        compiler_params=_params("arbitrary"),
    )


def _sgu_norm(va, gn, bn):
    mu = jnp.mean(va, axis=-1, keepdims=True)
    xc = va - mu
    rstd = lax.rsqrt(jnp.mean(xc * xc, axis=-1, keepdims=True) + EPS)
    vhat = xc * rstd
    return rstd, vhat, vhat * gn + bn


def _sgu_fwd(proj, gn, bn, ws, bsc, tm=512):
    S = proj.shape[1]
    GW = D // G

    def body(p_ref, gn_ref, bn_ref, ws_ref, bs_ref, a_ref):
        ua = _gelu(p_ref[0].astype(f32))
        va = _gelu(p_ref[1].astype(f32))
        _, _, vn = _sgu_norm(va, gn_ref[...], bn_ref[...])
        vn = vn.astype(CDT)
        for ch in range(tm // C):
            rs = slice(ch * C, (ch + 1) * C)
            for gi in range(G):
                cs = slice(gi * GW, (gi + 1) * GW)
                s = _mm(ws_ref[gi], vn[rs, cs]) + bs_ref[gi]
                a_ref[rs, cs] = (ua[rs, cs] * s).astype(CDT)

    return pl.pallas_call(
        body, name="sgu_fwd", grid=(S // tm,),
        out_shape=jax.ShapeDtypeStruct((S, D), CDT),
        in_specs=[pl.BlockSpec((2, tm, D), lambda i: (0, i, 0)), _resident((1, D), lambda i: (0, 0)),
                  _resident((1, D), lambda i: (0, 0)), _resident((G, C, C), lambda i: (0, 0, 0)),
                  _resident((G, C, 1), lambda i: (0, 0, 0))],
        out_specs=pl.BlockSpec((tm, D), lambda i: (i, 0)),
        compiler_params=_params("arbitrary"),
    )(proj, gn, bn, ws, bsc)


def _decay_tables(dl_ref):
    lg = jax.nn.log_sigmoid(dl_ref[0:2, :])
    lgf, lgb = lg[0:1, :], lg[1:2, :]
    assert RC <= DK
    ri = lax.broadcasted_iota(jnp.int32, (RC, RC), 0)
    ci = lax.broadcasted_iota(jnp.int32, (RC, RC), 1)
    d = (ri - ci).astype(f32)
    lower = d >= 0
    dmat = jnp.where(lower, jnp.exp(d * lgf[:, :RC]), jnp.exp(-d * lgb[:, :RC]))
    dmat_t = jnp.where(d <= 0, jnp.exp(-d * lgf[:, :RC]), jnp.exp(d * lgb[:, :RC]))
    pos = lax.broadcasted_iota(jnp.int32, (RC, DK), 0).astype(f32)
    t = dict(
        lgf=lgf, lgb=lgb, d=d, lower=lower, dmat=dmat, dmat_t=dmat_t, pos=pos,
        fq=jnp.exp((pos + 1.0) * lgf), fk=jnp.exp((RC - 1.0 - pos) * lgf),
        bq=jnp.exp((RC - pos) * lgb), bk=jnp.exp(pos * lgb),
        lamf=jnp.exp(float(RC) * lgf), lamb=jnp.exp(float(RC) * lgb),
    )
    return t


def _rotate(t, co, si):
    t1, t2 = t[:, :DK // 2], t[:, DK // 2:]
    return jnp.concatenate([t1 * co - t2 * si, t2 * co + t1 * si], axis=-1)


def _unrotate(t, co, si):
    t1, t2 = t[:, :DK // 2], t[:, DK // 2:]
    return jnp.concatenate([t1 * co + t2 * si, t2 * co - t1 * si], axis=-1)


K_SCALE = DK ** -0.5
ROW_TILE = 256


def _ret_fwd(proj, dl, hooks=()):
    S = proj.shape[1]
    NC = S // RC

    def body(q_ref, k_ref, v_ref, g_ref, dl_ref, R_ref, r_ref, sfs_ref, sbs_ref, rb_ref, sf_ref, sb_ref):
        t = _decay_tables(dl_ref)

        def chunk(n):
            rows = pl.ds(pl.multiple_of(n * RC, RC), RC)
            return rows, q_ref[rows, :], k_ref[rows, :], v_ref[rows, :]

        sf_ref[...] = jnp.zeros_like(sf_ref)
        sb_ref[...] = jnp.zeros_like(sb_ref)

        def step(i, carry):
            rows, qn, kn, vn = chunk(i)
            sc = _mm_nt(qn, kn) * t["dmat"]
            out = _mm(sc.astype(CDT), vn)
            sf = sf_ref[...]
            sfb = sf.astype(CDT)
            sfs_ref[i] = sfb
            R_ref[rows, :] = out + _mm((qn.astype(f32) * t["fq"]).astype(CDT), sfb)
            sf_ref[...] = sf * t["lamf"] + _mm_tn((kn.astype(f32) * t["fk"]).astype(CDT), vn)
            m = NC - 1 - i
            rows, qn, kn, vn = chunk(m)
            sb = sb_ref[...]
            sbb = sb.astype(CDT)
            sbs_ref[m] = sbb
            rb_ref[rows, :] = _mm((qn.astype(f32) * t["bq"]).astype(CDT), sbb)
            sb_ref[...] = sb * t["lamb"] + _mm_tn((kn.astype(f32) * t["bk"]).astype(CDT), vn)
            return carry

        lax.fori_loop(0, NC, step, 0)

        def finish(i, carry):
            rs = pl.ds(pl.multiple_of(i * ROW_TILE, ROW_TILE), ROW_TILE)
            R = R_ref[rs, :] + rb_ref[rs, :]
            R_ref[rs, :] = R
            rn = R * lax.rsqrt(jnp.mean(R * R, axis=-1, keepdims=True) + EPS)
            g = g_ref[rs, :].astype(f32)
            r_ref[rs, :] = (rn * g * jax.nn.sigmoid(g)).astype(CDT)
            return carry

        lax.fori_loop(0, S // ROW_TILE, finish, 0)

    def seg(slot):
        return pl.BlockSpec((None, S, DK), lambda h: (slot, 0, h))

    states = jax.ShapeDtypeStruct((H, NC, DK, DK), CDT)
    state_blk = pl.BlockSpec((None, NC, DK, DK), lambda h: (h, 0, 0, 0))
    return _call(
        body, hooks, [proj, proj, proj, proj, dl], name="ret_fwd", grid=(H,),
        out_shape=[jax.ShapeDtypeStruct((S, H * DK), f32), jax.ShapeDtypeStruct((S, H * DK), CDT), states, states],
        in_specs=[seg(SLOT_Q), seg(SLOT_K), seg(SLOT_VR), seg(SLOT_GR), pl.BlockSpec((None, 8, DK), lambda h: (h, 0, 0))],
        out_specs=[pl.BlockSpec((S, DK), lambda h: (0, h)), pl.BlockSpec((S, DK), lambda h: (0, h)), state_blk, state_blk],
        scratch_shapes=[pltpu.VMEM((S, DK), f32), pltpu.VMEM((DK, DK), f32), pltpu.VMEM((DK, DK), f32)],
        compiler_params=_params("arbitrary"),
    )


def _merge_fwd(a, r, proj, x1, wa, wb, wo, hooks=(), tm=512):
    S = x1.shape[0]

    def body(a_ref, r_ref, gt_ref, x_ref, wa_ref, wb_ref, wo_ref, x2_ref, ya_ref, yb_ref):
        ya = _mm(a_ref[...], wa_ref[...])
        yb = _mm(r_ref[...], wb_ref[...])
        ya_ref[...] = ya.astype(CDT)
        yb_ref[...] = yb.astype(CDT)
        mix = jax.nn.sigmoid(gt_ref[0].astype(f32)) * ya + jax.nn.sigmoid(gt_ref[1].astype(f32)) * yb
        x2_ref[...] = x_ref[...] + _mm(mix.astype(CDT), wo_ref[...])

    tok = pl.BlockSpec((tm, D), lambda i: (i, 0))
    return _call(
        body, hooks, [a, r, proj, x1, wa, wb, wo], name="merge_fwd", grid=(S // tm,),
        out_shape=[jax.ShapeDtypeStruct((S, D), f32), jax.ShapeDtypeStruct((S, D), CDT), jax.ShapeDtypeStruct((S, D), CDT)],
        in_specs=[tok, tok, pl.BlockSpec((2, tm, D), lambda i: (SLOT_GA // 2, i, 0)), tok,
                  _wblock(wa), _wblock(wb), _wblock(wo)],
        out_specs=[tok, tok, tok],
        compiler_params=_params("arbitrary"),
    )


def _loss_head(x3, fn, target, tm=512):
    S = x3.shape[0]

    def body(x_ref, n_ref, t_ref, dx_ref, dxh_ref, dn_ref, l_ref):
        n = n_ref[...]
        r, xh, y = _rms_fwd(x_ref[...], n)
        e = y - t_ref[...]
        dy = e * (1.0 / D)
        dx, dn = _rms_bwd(dy, r, xh, n)
        dx_ref[...] = dx
        dxh_ref[...] = (0.5 * dx).astype(CDT)
        part = 0.5 * jnp.sum(jnp.sum(e * e, axis=-1, keepdims=True), axis=0, keepdims=True) * (1.0 / D)

        @pl.when(pl.program_id(0) == 0)
        def _():
            dn_ref[...] = jnp.zeros_like(dn_ref)
            l_ref[...] = jnp.zeros_like(l_ref)

        dn_ref[...] += dn
        l_ref[...] += jnp.broadcast_to(part, l_ref.shape)

    tok = pl.BlockSpec((tm, D), lambda i: (i, 0))
    return pl.pallas_call(
        body, name="loss_head", grid=(S // tm,),
        out_shape=[jax.ShapeDtypeStruct((S, D), f32), jax.ShapeDtypeStruct((S, D), CDT), jax.ShapeDtypeStruct((1, D), f32),
                   jax.ShapeDtypeStruct((8, 128), f32)],
        in_specs=[tok, _resident((1, D), lambda i: (0, 0)), tok],
        out_specs=[tok, tok, pl.BlockSpec((1, D), lambda i: (0, 0)), pl.BlockSpec((8, 128), lambda i: (0, 0))],
        compiler_params=_params("arbitrary"),
    )(x3, fn, target)


def _ffn_bwd_hidden(name, dyh, g, u, wd, hooks=(), tm=512):
    S = dyh.shape[0]

    def body(dyh_ref, g_ref, u_ref, wd_ref, dg_ref, du_ref):
        dyh = dyh_ref[...]
        for ci in range(F // FT):
            sl = slice(ci * FT, (ci + 1) * FT)
            da = _mm_nt(dyh, wd_ref[sl, :])
            gv = g_ref[:, sl].astype(f32)
            uv = u_ref[:, sl].astype(f32)
            s = jax.nn.sigmoid(gv)
            du_ref[:, sl] = (da * (gv * s)).astype(CDT)
            dg_ref[:, sl] = (da * uv * (s * (1.0 + gv * (1.0 - s)))).astype(CDT)

    hid = pl.BlockSpec((tm, F), lambda i: (i, 0))
    hidden = jax.ShapeDtypeStruct((S, F), CDT)
    return _call(
        body, hooks, [dyh, g, u, wd], name=name, grid=(S // tm,), out_shape=[hidden, hidden],
        in_specs=[pl.BlockSpec((tm, D), lambda i: (i, 0)), hid, hid, _wblock(wd)], out_specs=[hid, hid],
        compiler_params=_params("arbitrary"),
    )


def _ffn_bwd_in(name, dy, x, dg, du, nrm, wg, wu, hooks=(), part=None, prev=None, tm=512):
    S = x.shape[0]
    t0, nt = part or (0, S // tm)

    def body(dy_ref, x_ref, dg_ref, du_ref, n_ref, wg_ref, wu_ref, *rest):
        dx_ref, dn_ref, acc_ref = rest[-3:]
        n = n_ref[...]
        r, xh, _ = _rms_fwd(x_ref[...], n)
        for ci in range(F // FT):
            sl = slice(ci * FT, (ci + 1) * FT)
            dh = _mm(dg_ref[:, sl], wg_ref[sl, :]) + _mm(du_ref[:, sl], wu_ref[sl, :])
            if ci == 0:
                acc_ref[...] = dh
            else:
                acc_ref[...] += dh
        dx, dn = _rms_bwd(acc_ref[...], r, xh, n)
        dx_ref[...] = dy_ref[...] + dx

        @pl.when(pl.program_id(0) == 0)
        def _():
            dn_ref[...] = jnp.zeros_like(dn_ref) if prev is None else rest[1][...]

        dn_ref[...] += dn

    tok = pl.BlockSpec((tm, D), lambda i: (t0 + i, 0))
    hid = pl.BlockSpec((tm, F), lambda i: (t0 + i, 0))
    row = pl.BlockSpec((1, D), lambda i: (0, 0))
    in_specs = [tok, tok, hid, hid, _resident((1, D), lambda i: (0, 0)), _wblock(wg), _wblock(wu)]
    operands = [dy, x, dg, du, nrm, wg, wu]
    aliases = {}
    if prev is not None:
        in_specs += [_HBM, row]
        operands += list(prev)
        aliases = {7: 0}
    return _call(
        body, hooks, operands, name=name, grid=(nt,),
        out_shape=[jax.ShapeDtypeStruct((S, D), f32), jax.ShapeDtypeStruct((1, D), f32)],
        in_specs=in_specs, out_specs=[tok, row],
        scratch_shapes=[pltpu.VMEM((tm, D), f32)],
        input_output_aliases=aliases,
        compiler_params=_params("arbitrary"),
    )


TN_ROWS = 512


def _tn(name, xs, ys, block_of, hooks=()):
    S, M = xs.shape
    B = ys.shape[0]
    tr = TN_ROWS if M % TN_ROWS == 0 else M // 2
    assert M % tr == 0 and tr % 128 == 0
    nt = M // tr

    def body(x_ref, y_ref, o_ref):
        o_ref[...] = _mm_tn(x_ref[...], y_ref[...]).astype(CDT)

    return _call(
        body, hooks, [xs, ys], name=name, grid=(B, nt),
        out_shape=[jax.ShapeDtypeStruct((B * M, D), CDT)],
        in_specs=[pl.BlockSpec((S, tr), lambda b, i: (0, i)), pl.BlockSpec((None, S, D), lambda b, i: (b, 0, 0))],
        out_specs=[pl.BlockSpec((tr, D), lambda b, i: (block_of(b) * nt + i, 0))],
        compiler_params=_params("arbitrary", "arbitrary"),
    )


def _wgrad(name, xs, y, hooks=()):
    return _tn(name, xs, y[None], lambda b: 0, hooks)


def _merge_bwd_act(dx2, ya, yb, proj, wa, wb, wo, hooks=(), tm=512):
    S = dx2.shape[0]

    def body(dx_ref, ya_ref, yb_ref, gt_ref, wa_ref, wb_ref, wo_ref,
             dp_ref, da_ref, dr_ref, mix_ref, dxb_ref, dya_ref, dyb_ref):
        dxb = dx_ref[...].astype(CDT)
        dxb_ref[...] = dxb
        dmix = _mm_nt(dxb, wo_ref[...])
        ya = ya_ref[...].astype(f32)
        yb = yb_ref[...].astype(f32)
        sa = jax.nn.sigmoid(gt_ref[0].astype(f32))
        sb = jax.nn.sigmoid(gt_ref[1].astype(f32))
        mix_ref[...] = (sa * ya + sb * yb).astype(CDT)
        dya = (dmix * sa).astype(CDT)
        dyb = (dmix * sb).astype(CDT)
        dya_ref[...] = dya
        dyb_ref[...] = dyb
        dp_ref[0] = (dmix * ya * sa * (1.0 - sa)).astype(CDT)
        dp_ref[1] = (dmix * yb * sb * (1.0 - sb)).astype(CDT)
        da_ref[...] = _mm_nt(dya, wa_ref[...]).astype(CDT)
        dr_ref[...] = _mm_nt(dyb, wb_ref[...]).astype(CDT)

    tok = pl.BlockSpec((tm, D), lambda i: (i, 0))
    gates = pl.BlockSpec((2, tm, D), lambda i: (SLOT_GA // 2, i, 0))
    act = jax.ShapeDtypeStruct((S, D), CDT)
    return _call(
        body, hooks, [dx2, ya, yb, proj, wa, wb, wo], name="merge_bwd_act", grid=(S // tm,),
        out_shape=[jax.ShapeDtypeStruct((8, S, D), CDT), act, act, act, act, act, act],
        in_specs=[tok, tok, tok, gates, _wblock(wa), _wblock(wb), _wblock(wo)],
        out_specs=[gates, tok, tok, tok, tok, tok, tok],
        compiler_params=_params("arbitrary"),
    )


def _sgu_bwd(da, proj, dproj, gn, bn, ws, wst, bsc, hooks=(), tm=512):
    S = proj.shape[1]
    GW = D // G

    def body(da_ref, p_ref, dpin_ref, gn_ref, bn_ref, ws_ref, wst_ref, bs_ref,
             dp_ref, dws_ref, dbs_ref, dgn_ref, dbn_ref, ds_ref, dvn_ref):
        @pl.when(pl.program_id(0) == 0)
        def _():
            dws_ref[...] = jnp.zeros_like(dws_ref)
            dbs_ref[...] = jnp.zeros_like(dbs_ref)
            dgn_ref[...] = jnp.zeros_like(dgn_ref)
            dbn_ref[...] = jnp.zeros_like(dbn_ref)

        pu = p_ref[0].astype(f32)
        pv = p_ref[1].astype(f32)
        ua = _gelu(pu)
        va = _gelu(pv)
        gn = gn_ref[...]
        rstd, vhat, vn = _sgu_norm(va, gn, bn_ref[...])
        vnb = vn.astype(CDT)
        dav = da_ref[...].astype(f32)
        dsb = (dav * ua).astype(CDT)
        ones = jnp.ones((8, GW), CDT)
        for ch in range(tm // C):
            rs = slice(ch * C, (ch + 1) * C)
            for gi in range(G):
                cs = slice(gi * GW, (gi + 1) * GW)
                s = _mm(ws_ref[gi], vnb[rs, cs]) + bs_ref[gi]
                ds_ref[rs, cs] = s
                dsg = dsb[rs, cs]
                dws_ref[gi] += _mm_nt(dsg, vnb[rs, cs])
                dbs_ref[gi] += _mm_nt(ones, dsg)
                dvn_ref[rs, cs] = _mm(wst_ref[gi], dsg)
        dp_ref[0] = (dav * ds_ref[...] * _gelu_grad(pu)).astype(CDT)
        dvn = dvn_ref[...]
        dgn_ref[...] += jnp.sum(dvn * vhat, axis=0, keepdims=True)
        dbn_ref[...] += jnp.sum(dvn, axis=0, keepdims=True)
        dvh = dvn * gn
        dva = rstd * (dvh - jnp.mean(dvh, axis=-1, keepdims=True) - vhat * jnp.mean(dvh * vhat, axis=-1, keepdims=True))
        dp_ref[1] = (dva * _gelu_grad(pv)).astype(CDT)

    uv = pl.BlockSpec((2, tm, D), lambda i: (0, i, 0))
    row = _resident((1, D), lambda i: (0, 0))
    return _call(
        body, hooks, [da, proj, dproj, gn, bn, ws, wst, bsc], name="sgu_bwd", grid=(S // tm,),
        out_shape=[jax.ShapeDtypeStruct(dproj.shape, CDT), jax.ShapeDtypeStruct((G, C, C), f32),
                   jax.ShapeDtypeStruct((G, 8, C), f32), jax.ShapeDtypeStruct((1, D), f32), jax.ShapeDtypeStruct((1, D), f32)],
        in_specs=[pl.BlockSpec((tm, D), lambda i: (i, 0)), uv, _HBM, row, row,
                  _resident((G, C, C), lambda i: (0, 0, 0)), _resident((G, C, C), lambda i: (0, 0, 0)),
                  _resident((G, C, 1), lambda i: (0, 0, 0))],
        out_specs=[uv, pl.BlockSpec((G, C, C), lambda i: (0, 0, 0)), pl.BlockSpec((G, 8, C), lambda i: (0, 0, 0)),
                   pl.BlockSpec((1, D), lambda i: (0, 0)), pl.BlockSpec((1, D), lambda i: (0, 0))],
        scratch_shapes=[pltpu.VMEM((tm, D), f32), pltpu.VMEM((tm, D), f32)],
        input_output_aliases={2: 0},
        compiler_params=_params("arbitrary"),
    )


def _ret_bwd(dr, R, sfs, sbs, proj, dproj, cos, sin, dl, hooks=()):
    S = proj.shape[1]
    NC = S // RC
    assert NC % 2 == 0

    def body(dr_ref, R_ref, sf_ref, sb_ref, q_ref, k_ref, v_ref, g_ref, dpin_ref, cos_ref, sin_ref, dl_ref,
             dp_ref, dd_ref, dR_ref, gb_ref, gf_ref, acc_ref):
        t = _decay_tables(dl_ref)

        def gate_norm_bwd(i, carry):
            rs = pl.ds(pl.multiple_of(i * ROW_TILE, ROW_TILE), ROW_TILE)
            Rv = R_ref[rs, :]
            rstd = lax.rsqrt(jnp.mean(Rv * Rv, axis=-1, keepdims=True) + EPS)
            rn = Rv * rstd
            gv = g_ref[rs, :].astype(f32)
            s = jax.nn.sigmoid(gv)
            drv = dr_ref[rs, :].astype(f32)
            dp_ref[3, rs, :] = (drv * rn * (s * (1.0 + gv * (1.0 - s)))).astype(CDT)
            drn = drv * gv * s
            dR_ref[rs, :] = (rstd * (drn - rn * jnp.mean(drn * rn, axis=-1, keepdims=True))).astype(CDT)
            return carry

        lax.fori_loop(0, S // ROW_TILE, gate_norm_bwd, 0)

        def chunk(n):
            rows = pl.ds(pl.multiple_of(n * RC, RC), RC)
            return rows, q_ref[rows, :], k_ref[rows, :], v_ref[rows, :], dR_ref[rows, :]

        def emit_kv(rows, dk, dv, final):
            if not final:
                dp_ref[1, rows, :] = dk.astype(CDT)
                dp_ref[2, rows, :] = dv.astype(CDT)
            else:
                co, si = cos_ref[rows, :], sin_ref[rows, :]
                dk = dp_ref[1, rows, :].astype(f32) + dk
                dp_ref[1, rows, :] = (_unrotate(dk, co, si) * K_SCALE).astype(CDT)
                dp_ref[2, rows, :] = (dp_ref[2, rows, :].astype(f32) + dv).astype(CDT)

        gb_ref[...] = jnp.zeros_like(gb_ref)
        gf_ref[...] = jnp.zeros_like(gf_ref)
        acc_ref[...] = jnp.zeros_like(acc_ref)
        dpos = jnp.abs(t["d"])

        def ascend(n, final):
            rows, qn, kn, vn, dRn = chunk(n)
            qf, kf = qn.astype(f32), kn.astype(f32)
            sc = _mm_nt(qn, kn)
            dA = _mm_nt(dRn, vn)
            w = sc * dA * t["dmat"] * dpos
            lgf_part = jnp.sum(jnp.where(t["lower"], w, 0.0), axis=0, keepdims=True)
            lgb_part = jnp.sum(jnp.where(t["lower"], 0.0, w), axis=0, keepdims=True)
            dsc = (dA * t["dmat"]).astype(CDT)
            dq = _mm(dsc, kn)
            scT = (_mm_nt(kn, qn) * t["dmat_t"]).astype(CDT)
            dscT = (_mm_nt(vn, dRn) * t["dmat_t"]).astype(CDT)
            dk = _mm(dscT, qn)
            dv = _mm(scT, dRn)
            sfb = sf_ref[n]
            sbb = sb_ref[n]
            qdf = qf * t["fq"]
            dqdf = _mm_nt(dRn, sfb)
            dq += dqdf * t["fq"]
            lgf_row = jnp.sum(qdf * dqdf * (t["pos"] + 1.0), axis=0, keepdims=True)
            qdb = qf * t["bq"]
            dqdb = _mm_nt(dRn, sbb)
            dq += dqdb * t["bq"]
            lgb_row = jnp.sum(qdb * dqdb * (RC - t["pos"]), axis=0, keepdims=True)
            gb = gb_ref[...]
            gbb = gb.astype(CDT)
            kdb = kf * t["bk"]
            dkdb = _mm_nt(vn, gbb)
            dk += dkdb * t["bk"]
            dv += _mm(kdb.astype(CDT), gbb)
            lgb_row += jnp.sum(kdb * dkdb * t["pos"], axis=0, keepdims=True)
            lgb_row += float(RC) * t["lamb"] * jnp.sum(gb * sbb.astype(f32), axis=0, keepdims=True)
            co, si = cos_ref[rows, :], sin_ref[rows, :]
            dp_ref[0, rows, :] = _unrotate(dq, co, si).astype(CDT)
            emit_kv(rows, dk, dv, final)
            acc_ref[0:1, :] += lgf_row + lgf_part
            acc_ref[1:2, :] += lgb_row + lgb_part
            gb_ref[...] = gb * t["lamb"] + _mm_tn(qdb.astype(CDT), dRn)

        def descend(n, final):
            rows, qn, kn, vn, dRn = chunk(n)
            gf = gf_ref[...]
            gfb = gf.astype(CDT)
            kdf = kn.astype(f32) * t["fk"]
            dkdf = _mm_nt(vn, gfb)
            lgf_row = jnp.sum(kdf * dkdf * (RC - 1.0 - t["pos"]), axis=0, keepdims=True)
            lgf_row += float(RC) * t["lamf"] * jnp.sum(gf * sf_ref[n].astype(f32), axis=0, keepdims=True)
            acc_ref[0:1, :] += lgf_row
            emit_kv(rows, dkdf * t["fk"], _mm(kdf.astype(CDT), gfb), final)
            gf_ref[...] = gf * t["lamf"] + _mm_tn((qn.astype(f32) * t["fq"]).astype(CDT), dRn)

        def sweep(final):
            def step(i, carry):
                ascend(i, final)
                descend(NC - 1 - i, final)
                return carry
            return step

        lax.fori_loop(0, NC // 2, sweep(False), 0)
        lax.fori_loop(NC // 2, NC, sweep(True), 0)
        dlg = jnp.sum(acc_ref[...], axis=1, keepdims=True)
        dlogit = dlg * jax.nn.sigmoid(-dl_ref[:, 0:1])
        lane = lax.broadcasted_iota(jnp.int32, (8, 128), 1)
        dd_ref[...] = jnp.where(lane == pl.program_id(0), jnp.broadcast_to(dlogit, (8, 128)), 0.0)

    def seg(slot):
        return pl.BlockSpec((None, S, DK), lambda h: (slot, 0, h))

    head = pl.BlockSpec((S, DK), lambda h: (0, h))
    states = pl.BlockSpec((None, NC, DK, DK), lambda h: (h, 0, 0, 0))
    return _call(
        body, hooks, [dr, R, sfs, sbs, proj, proj, proj, proj, dproj, cos, sin, dl], name="ret_bwd", grid=(H,),
        out_shape=[jax.ShapeDtypeStruct(dproj.shape, CDT), jax.ShapeDtypeStruct((H, 8, 128), f32)],
        in_specs=[head, head, states, states, seg(SLOT_Q), seg(SLOT_K), seg(SLOT_VR), seg(SLOT_GR), _HBM,
                  _resident((S, DK // 2), lambda h: (0, 0)), _resident((S, DK // 2), lambda h: (0, 0)),
                  pl.BlockSpec((None, 8, DK), lambda h: (h, 0, 0))],
        out_specs=[pl.BlockSpec((4, S, DK), lambda h: (1, 0, h), pipeline_mode=pl.Buffered(1)),
                   pl.BlockSpec((None, 8, 128), lambda h: (h, 0, 0))],
        scratch_shapes=[pltpu.VMEM((S, DK), CDT),
                        pltpu.VMEM((DK, DK), f32), pltpu.VMEM((DK, DK), f32), pltpu.VMEM((8, DK), f32)],
        input_output_aliases={8: 0},
        compiler_params=_params("arbitrary"),
    )


def _proj_bwd_act(dproj, dx2, x1, nrm, wfull, hooks=(), tm=512):
    S = x1.shape[0]

    def body(dp_ref, dx2_ref, x_ref, n_ref, w_ref, dx_ref, dxh_ref, dn_ref, db_ref, acc_ref):
        @pl.when(pl.program_id(0) == 0)
        def _():
            dn_ref[...] = jnp.zeros_like(dn_ref)
            db_ref[...] = jnp.zeros_like(db_ref)

        for p in range(8):
            seg = SEG_OF_SLOT[p]
            dp = dp_ref[p]
            db_ref[seg] += jnp.sum(dp.astype(f32), axis=0, keepdims=True)
            dh = _mm_nt(dp, w_ref[seg * D:(seg + 1) * D, :])
            if p == 0:
                acc_ref[...] = dh
            else:
                acc_ref[...] += dh
        n = n_ref[...]
        r, xh, _ = _rms_fwd(x_ref[...], n)
        dx, dn = _rms_bwd(acc_ref[...], r, xh, n)
        dx = dx2_ref[...] + dx
        dx_ref[...] = dx
        dxh_ref[...] = (0.5 * dx).astype(CDT)
        dn_ref[...] += dn

    tok = pl.BlockSpec((tm, D), lambda i: (i, 0))
    return _call(
        body, hooks, [dproj, dx2, x1, nrm, wfull], name="proj_bwd_act", grid=(S // tm,),
        out_shape=[jax.ShapeDtypeStruct((S, D), f32), jax.ShapeDtypeStruct((S, D), CDT), jax.ShapeDtypeStruct((1, D), f32),
                   jax.ShapeDtypeStruct((8, 1, D), f32)],
        in_specs=[pl.BlockSpec((8, tm, D), lambda i: (0, i, 0)), tok, tok, _resident((1, D), lambda i: (0, 0)),
                  _resident((8 * D, D), lambda i: (0, 0))],
        out_specs=[tok, tok, pl.BlockSpec((1, D), lambda i: (0, 0)), pl.BlockSpec((8, 1, D), lambda i: (0, 0, 0))],
        scratch_shapes=[pltpu.VMEM((tm, D), f32)],
        compiler_params=_params("arbitrary"),
    )


def _rs_sum(name, gfulls, lands, my_c):
    n = len(gfulls)
    rows = gfulls[0].shape[0] // NDEV
    assert all(g.shape[0] == NDEV * rows for g in gfulls)

    def body(c_ref, *refs):
        for g_ref, l_ref, o_ref in zip(refs[:n], refs[n:2 * n], refs[2 * n:]):
            o_ref[...] = (g_ref[...].astype(f32) + l_ref[...].astype(f32)).astype(CDT)

    slot = pl.BlockSpec((None, rows, D), lambda k, c: (k, 0, 0))
    return pl.pallas_call(
        body, name=name,
        grid_spec=pltpu.PrefetchScalarGridSpec(
            num_scalar_prefetch=1, grid=(NCHIP,),
            in_specs=[pl.BlockSpec((rows, D), lambda k, c: (2 * k + c[0], 0))] * n + [slot] * n,
            out_specs=[slot] * n),
        out_shape=[jax.ShapeDtypeStruct((NCHIP, rows, D), CDT)] * n,
        compiler_params=_params("arbitrary"),
    )(my_c, *gfulls, *lands)


def _adamw_math(g, w, m, v):
    m2 = ADAM_B1 * m + (1.0 - ADAM_B1) * g
    v2 = ADAM_B2 * v + (1.0 - ADAM_B2) * (g * g)
    delta = -ADAM_LR * ((m2 / BC1) / (jnp.sqrt(v2 / BC2) + ADAM_EPS) + ADAM_WD * w)
    return delta, m2, v2


def _adamw_big(name, landed, w, m, v, after):
    rows = w.shape[0]
    tr = min(rows, 256) if rows % 256 == 0 else rows

    def body(l_ref, w_ref, m_ref, v_ref, after_ref, g_ref, d_ref, m2_ref, v2_ref):
        g = l_ref[0].astype(f32)
        for k in range(1, NCHIP):
            g = g + l_ref[k].astype(f32)
        g_ref[...] = g
        d_ref[...], m2_ref[...], v2_ref[...] = _adamw_math(g, w_ref[...], m_ref[...], v_ref[...])

    blk = pl.BlockSpec((tr, D), lambda i: (i, 0))
    o = jax.ShapeDtypeStruct((rows, D), f32)
    return pl.pallas_call(
        body, name=name, grid=(rows // tr,), out_shape=[o, o, o, o],
        in_specs=[pl.BlockSpec((NCHIP, tr, D), lambda i: (0, i, 0)), blk, blk, blk, _HBM],
        out_specs=[blk, blk, blk, blk],
        compiler_params=_params("arbitrary"),
    )(landed, w, m, v, after)


ROW_FFN1_NORM, ROW_MIX_NORM, ROW_SGU_G, ROW_SGU_B, ROW_FFN2_NORM, ROW_FINAL_NORM, ROW_B_IN = 0, 1, 2, 3, 4, 5, 8
ROW_WS, ROW_BS, ROW_DECAY = 0, G * C, G * C + G * 8


def _adamw_small(ga, gb, gn1, gl, params):
    def body(ga_ref, gb_ref, gn1_ref, gl_ref, *refs):
        ins, outs = refs[:30], refs[30:]

        def total(ref, r0, n):
            g = ref[0, r0:r0 + n, :]
            for j in range(1, NDEV):
                g = g + ref[j, r0:r0 + n, :]
            return g

        def apply(i, g, rows=slice(None)):
            w, m, v = ins[3 * i][rows, :], ins[3 * i + 1][rows, :], ins[3 * i + 2][rows, :]
            outs[4 * i][rows, :] = g
            outs[4 * i + 1][rows, :], outs[4 * i + 2][rows, :], outs[4 * i + 3][rows, :] = _adamw_math(g, w, m, v)

        outs[40][...] = total(gl_ref, 0, 8)
        apply(0, total(gn1_ref, 0, 1))
        for i, r in enumerate((ROW_FFN1_NORM, ROW_MIX_NORM, ROW_SGU_G, ROW_SGU_B, ROW_FFN2_NORM, ROW_FINAL_NORM)):
            if i:
                apply(i, total(ga_ref, r, 1))
        apply(6, total(ga_ref, ROW_B_IN, 8))
        apply(7, total(gb_ref, ROW_WS, G * C))
        for gi in range(G):
            apply(8, total(gb_ref, ROW_BS + 8 * gi, 1), slice(gi, gi + 1))
        dec = total(gb_ref, ROW_DECAY, 8)
        for hh in range(1, H):
            dec = dec + total(gb_ref, ROW_DECAY + 8 * hh, 8)
        apply(9, dec)

    flat = [a for p in params for a in p]
    out_shape = [jax.ShapeDtypeStruct(p[0].shape, f32) for p in params for _ in range(4)]
    out_shape.append(jax.ShapeDtypeStruct((8, 128), f32))
    vm = pl.BlockSpec(memory_space=pltpu.VMEM)
    return pl.pallas_call(
        body, name="adamw_small", out_shape=out_shape,
        in_specs=[vm] * (4 + len(flat)), out_specs=[vm] * len(out_shape),
        compiler_params=pltpu.CompilerParams(vmem_limit_bytes=VMEM_LIMIT),
    )(ga, gb, gn1, gl, *flat)


def kernel(x, ffn1_norm, ffn1_w_gate, ffn1_w_up, ffn1_w_down, mix_norm, w_in, b_in, sgu_norm_g, sgu_norm_b, sgu_w_s, sgu_b_s, ret_decay_logit, w_branch_a, w_branch_b, w_out, ffn2_norm, ffn2_w_gate, ffn2_w_up, ffn2_w_down, final_norm, loss_target, m_ffn1_norm, m_ffn1_w_gate, m_ffn1_w_up, m_ffn1_w_down, m_mix_norm, m_w_in, m_b_in, m_sgu_norm_g, m_sgu_norm_b, m_sgu_w_s, m_sgu_b_s, m_ret_decay_logit, m_w_branch_a, m_w_branch_b, m_w_out, m_ffn2_norm, m_ffn2_w_gate, m_ffn2_w_up, m_ffn2_w_down, m_final_norm, v_ffn1_norm, v_ffn1_w_gate, v_ffn1_w_up, v_ffn1_w_down, v_mix_norm, v_w_in, v_b_in, v_sgu_norm_g, v_sgu_norm_b, v_sgu_w_s, v_sgu_b_s, v_ret_decay_logit, v_w_branch_a, v_w_branch_b, v_w_out, v_ffn2_norm, v_ffn2_w_gate, v_ffn2_w_up, v_ffn2_w_down, v_final_norm):
    args = dict(locals())
    S = x.shape[1]
    xs = x[0]
    target = loss_target[0]

    def buf_layout(name, a):
        a = a[0]
        return a.T if name in W_TRANSPOSED else a

    sh = {n: buf_layout(n, args[n]).astype(CDT) for n in W_NAMES}
    wf = {}

    b3 = b_in.reshape(8, 1, D)
    ws = sgu_w_s[0].astype(CDT)
    wst = jnp.swapaxes(sgu_w_s[0], 1, 2).astype(CDT)
    bsc = sgu_b_s[0].reshape(G, C, 1)
    dl = jnp.zeros((H, 8, DK), f32).at[:, 0:2, :].set(jnp.broadcast_to(ret_decay_logit[0].T[:, :, None], (H, 2, DK)))
    theta = ROPE_BASE ** (-jnp.arange(0, DK, 2, dtype=f32) / DK)
    ang = jnp.arange(S, dtype=f32)[:, None] * theta[None, :]
    cos, sin = jnp.cos(ang), jnp.sin(ang)
    fnorm = final_norm.reshape(1, D)

    f1 = ("ffn1_w_gate", "ffn1_w_up", "ffn1_w_down")
    f2 = ("ffn2_w_gate", "ffn2_w_up", "ffn2_w_down")
    br = ("w_branch_a", "w_branch_b", "w_out")
    for cid, names in enumerate((f1, ("w_in",), br, f2)):
        wf.update(zip(names, _sequence("ag_" + names[0], 1 + cid, SIBLING_AND_CHIPS, [_ag_hook(sh[n]) for n in names])))
    x1, g1, u1, a1, hf1 = _ffn_fwd("ffn1_fwd", xs, ffn1_norm, *[wf[n] for n in f1])
    proj, h2 = _proj_fwd(x1, mix_norm, wf["w_in"], b3, cos, sin)
    a = _sgu_fwd(proj, sgu_norm_g, sgu_norm_b, ws, bsc)
    R, r, sfs, sbs = _ret_fwd(proj, dl)
    x2, ya, yb = _merge_fwd(a, r, proj, x1, *[wf[n] for n in br])
    x3, g2, u2, a2, hf2 = _ffn_fwd("ffn2_fwd", x2, ffn2_norm, *[wf[n] for n in f2])
    dx3, dyh2, d_final, loss_part = _loss_head(x3, fnorm, target)

    my_c = lax.axis_index("c").astype(jnp.int32).reshape(1)
    gw, landed = {}, {}

    def to_sibling(names):
        return _sequence("rs_sib_" + names[0], REACH_ID[SIBLING], SIBLING, [_rs_d2d_hook(gw[n]) for n in names])

    def to_chips(names, sibs):
        parts = _rs_sum("rs_sum_" + names[0], [gw[n] for n in names], list(sibs), my_c)
        landed.update(zip(names, _sequence("rs_chips_" + names[0], REACH_ID[CHIPS], CHIPS, [_rs_ici_hook(p) for p in parts])))

    def gather_small(tag, arrays):
        return _sequence("ag_small_" + tag, REACH_ID[EVERYONE], EVERYONE, [_small_hook(arrays)])

    def ffn_bwd(tag, names, dy, dyh, x, g, u, a, h, nrm, each_alone):
        wg, wu, wd = names
        (gw[wd],) = _wgrad(tag + "_wd_grad", a, dyh)
        if each_alone:
            sib_d = to_sibling([wd])
        dg, du = _ffn_bwd_hidden(tag + "_bwd_hidden", dyh, g, u, wf[wd])
        if each_alone:
            to_chips([wd], sib_d)
        (gw[wg],) = _wgrad(tag + "_wg_grad", dg, h)
        if each_alone:
            sib_g = to_sibling([wg])
        (gw[wu],) = _wgrad(tag + "_wu_grad", du, h)
        if each_alone:
            to_chips([wg], sib_g)
            to_chips([wu], to_sibling([wu]))
        else:
            sibs = to_sibling([wd, wg, wu])
        dx, dn = _ffn_bwd_in(tag + "_bwd_in", dy, x, dg, du, nrm, wf[wg], wf[wu])
        if not each_alone:
            to_chips([wd, wg, wu], sibs)
        return dx, dn

    dx2, d_ffn2n = ffn_bwd("ffn2", f2, dx3, dyh2, x2, g2, u2, a2, hf2, ffn2_norm, False)
    dproj, da, dr, mix, dx2b, dya, dyb = _merge_bwd_act(dx2, ya, yb, proj, *[wf[n] for n in br])
    (gw["w_out"],) = _wgrad("wo_grad", mix, dx2b)
    (gw["w_branch_a"],) = _wgrad("wa_grad", a, dya)
    (gw["w_branch_b"],) = _wgrad("wb_grad", r, dyb)
    sib_br = to_sibling(list(br))
    dproj, d_ws, d_bs, d_gn, d_bn = _sgu_bwd(da, proj, dproj, sgu_norm_g, sgu_norm_b, ws, wst, bsc)
    to_chips(list(br), sib_br)
    g_sgu, gl = gather_small("sgu", [jnp.concatenate([d_ws.reshape(G * C, C), d_bs.reshape(G * 8, C)], axis=0), loss_part])
    dproj, d_dec = _ret_bwd(dr, R, sfs, sbs, proj, dproj, cos, sin, dl)
    (gw["w_in"],) = _tn("win_grad", h2, dproj, _seg_of_slot)
    sib_win = to_sibling(["w_in"])
    dx1, dyh1, d_mixn, d_bin = _proj_bwd_act(dproj, dx2, x1, mix_norm, wf["w_in"])
    to_chips(["w_in"], sib_win)
    small_a = jnp.concatenate([jnp.zeros((1, D), f32), d_mixn, d_gn, d_bn, d_ffn2n, d_final, jnp.zeros((2, D), f32),
                               d_bin.reshape(8, D)], axis=0)
    ga, g_dec = gather_small("rest", [small_a, d_dec.reshape(H * 8, 128)])
    gb = jnp.concatenate([g_sgu, g_dec], axis=1)
    dxs, d_ffn1n = ffn_bwd("ffn1", f1, dx1, dyh1, xs, g1, u1, a1, hf1, ffn1_norm, True)
    (gn1,) = _exchange("ag_ffn1_norm", [_small_hook([d_ffn1n])])

    out = {"grad_x": dxs[None]}

    def native(name, a):
        a = a.T if name in W_TRANSPOSED else a
        return a[None]

    after = gn1
    for n in ("w_in",) + f2 + br + (f1[2], f1[0], f1[1]):
        res = _adamw_big("adamw_" + n, landed[n], buf_layout(n, args[n]), buf_layout(n, args["m_" + n]),
                         buf_layout(n, args["v_" + n]), after)
        after = res[0]
        for pre, val in zip(("grad_", "delta_", "new_m_", "new_v_"), res):
            out[pre + n] = native(n, val)

    def pad_decay(a):
        return jnp.zeros((8, 128), f32).at[0:2, 0:H].set(a[0])

    small = [
        ("ffn1_norm", lambda a: a, lambda a: a), ("mix_norm", lambda a: a, lambda a: a),
        ("sgu_norm_g", lambda a: a, lambda a: a), ("sgu_norm_b", lambda a: a, lambda a: a),
        ("ffn2_norm", lambda a: a, lambda a: a),
        ("final_norm", lambda a: a.reshape(1, D), lambda a: a.reshape(D)),
        ("b_in", lambda a: a.reshape(8, D), lambda a: a.reshape(1, 8 * D)),
        ("sgu_w_s", lambda a: a.reshape(G * C, C), lambda a: a.reshape(1, G, C, C)),
        ("sgu_b_s", lambda a: a[0], lambda a: a[None]),
        ("ret_decay_logit", pad_decay, lambda a: a[None, 0:2, 0:H]),
    ]
    res = _adamw_small(ga, gb, gn1, gl, [(to(args[n]), to(args["m_" + n]), to(args["v_" + n])) for n, to, _ in small])
    out["loss"] = res[40][0, 0]
    for i, (n, _, back) in enumerate(small):
        for j, pre in enumerate(("grad_", "delta_", "new_m_", "new_v_")):
            out[pre + n] = back(res[4 * i + j])

    weights = ("ffn1_norm", "ffn1_w_gate", "ffn1_w_up", "ffn1_w_down", "mix_norm", "w_in", "b_in", "sgu_norm_g",
               "sgu_norm_b", "sgu_w_s", "sgu_b_s", "ret_decay_logit", "w_branch_a", "w_branch_b", "w_out", "ffn2_norm",
               "ffn2_w_gate", "ffn2_w_up", "ffn2_w_down", "final_norm")
    return (out["loss"], out["grad_x"], *[out["grad_" + n] for n in weights], *[out["delta_" + n] for n in weights],
            *[out["new_m_" + n] for n in weights], *[out["new_v_" + n] for n in weights])
```

```python
import functools
import math

import jax
import jax.numpy as jnp
from jax import lax
from jax.experimental import pallas as pl
from jax.experimental.pallas import tpu as pltpu
from jax.experimental.pallas import tpu_sc as plsc

f32 = jnp.float32
CDT = jnp.bfloat16

D = 1024
F = 2816
C = 128
RC = 256
H = 4
DK = 256
G = 4
NDEV = 8
NCHIP = 4
EPS = 1e-6
ROPE_BASE = 10000.0
FT = 256
V7X_VMEM_BYTES = 64 * 1024 * 1024
VMEM_LIMIT = V7X_VMEM_BYTES - 8 * 1024 * 1024

ADAM_LR, ADAM_B1, ADAM_B2, ADAM_EPS, ADAM_WD, ADAM_STEP = 0.001, 0.9, 0.999, 1e-08, 0.01, 10
BC1 = 1.0 - ADAM_B1 ** ADAM_STEP
BC2 = 1.0 - ADAM_B2 ** ADAM_STEP

W_ROWS = dict(ffn1_w_gate=352, ffn1_w_up=352, ffn1_w_down=352, w_in=1024, w_branch_a=128, w_branch_b=128, w_out=128,
              ffn2_w_gate=352, ffn2_w_up=352, ffn2_w_down=352)
W_NAMES = tuple(W_ROWS)
W_TRANSPOSED = ("ffn1_w_gate", "ffn1_w_up", "ffn2_w_gate", "ffn2_w_up")

SLOT_U, SLOT_V, SLOT_GA, SLOT_GB, SLOT_Q, SLOT_K, SLOT_VR, SLOT_GR = range(8)


SEG_OF_SLOT = (0, 1, 6, 7, 2, 3, 4, 5)


def _seg_of_slot(p):
    return jnp.where(p < 2, p, jnp.where(p < 4, p + 4, p - 2))


def _mm(a, b):
    return jnp.dot(a, b, preferred_element_type=f32)


def _mm_nt(a, b):
    return lax.dot_general(a, b, (((1,), (1,)), ((), ())), preferred_element_type=f32)


def _mm_tn(a, b):
    return lax.dot_general(a, b, (((0,), (0,)), ((), ())), preferred_element_type=f32)


def _params(*sem):
    return pltpu.CompilerParams(dimension_semantics=sem, vmem_limit_bytes=VMEM_LIMIT)


def _resident(shape, index_map):
    return pl.BlockSpec(shape, index_map, pipeline_mode=pl.Buffered(1))


def _gelu(x):
    return 0.5 * x * (1.0 + lax.erf(x * (1.0 / math.sqrt(2.0))))


def _gelu_grad(x):
    return 0.5 * (1.0 + lax.erf(x * (1.0 / math.sqrt(2.0)))) + x * jnp.exp(-0.5 * x * x) * (1.0 / math.sqrt(2.0 * math.pi))


def _rms_fwd(x, n):
    r = lax.rsqrt(jnp.mean(x * x, axis=-1, keepdims=True) + EPS)
    xh = x * r
    return r, xh, xh * n


def _rms_bwd(dh, r, xh, n):
    dxh = dh * n
    dx = r * (dxh - xh * jnp.mean(dxh * xh, axis=-1, keepdims=True))
    return dx, jnp.sum(dh * xh, axis=0, keepdims=True)


MESH_ID = pl.DeviceIdType.MESH
_HBM = pl.BlockSpec(memory_space=pltpu.HBM)


def _my_place():
    return lax.axis_index("x"), lax.axis_index("y"), lax.axis_index("c")


def _ici_peers(x, y, c):
    return [((1 - x, y, c), 2 * (1 - x) + y), ((x, 1 - y, c), 2 * x + 1 - y), ((1 - x, 1 - y, c), 2 * (1 - x) + 1 - y)]


class _Hook:
    def __init__(self, operands, out_shapes, n_remote, n_local, start, finish, relay=None):
        self.operands, self.out_shapes = list(operands), list(out_shapes)
        self.n_remote, self.n_local, self.start, self.finish = n_remote, n_local, start, finish
        self.relay = relay or (lambda *a: None)


def _call(body, hooks, operands, *, in_specs, out_specs, out_shape, grid=None, scratch_shapes=(), **kw):
    hooks = tuple(hooks)
    n_in, n_out, n_scr = len(in_specs), len(out_shape), len(scratch_shapes)
    h_ops = [a for h in hooks for a in h.operands]
    h_outs = [s for h in hooks for s in h.out_shapes]
    h_sems = [pltpu.SemaphoreType.DMA((n,)) for h in hooks for n in (h.n_remote, h.n_remote, max(h.n_local, 1))]

    def wrapped(*refs):
        ins, hin = refs[:n_in], refs[n_in:n_in + len(h_ops)]
        o0 = n_in + len(h_ops)
        outs, hout = refs[o0:o0 + n_out], refs[o0 + n_out:o0 + n_out + len(h_outs)]
        s0 = o0 + n_out + len(h_outs)
        scr, hsem = refs[s0:s0 + n_scr], refs[s0 + n_scr:]

        def run(phase):
            ip = op = 0
            for i, h in enumerate(hooks):
                ssem, rsem, lsem = hsem[3 * i:3 * i + 3]

                def rcopy(k, src, dst, dev, ssem=ssem, rsem=rsem):
                    return pltpu.make_async_remote_copy(src_ref=src, dst_ref=dst, send_sem=ssem.at[k], recv_sem=rsem.at[k],
                                                        device_id=dev, device_id_type=MESH_ID)

                def lcopy(k, src, dst, lsem=lsem):
                    return pltpu.make_async_copy(src, dst, lsem.at[k])

                getattr(h, phase)(hin[ip:ip + len(h.operands)], hout[op:op + len(h.out_shapes)], rcopy, lcopy)
                ip += len(h.operands)
                op += len(h.out_shapes)

        def at_edge(phase, last):
            if not hooks:
                return
            if grid is None:
                run(phase)
                return
            cond = None
            for ax, n in enumerate(grid):
                here = pl.program_id(ax) == (n - 1 if last else 0)
                cond = here if cond is None else cond & here
            pl.when(cond)(lambda: run(phase))

        at_edge("start", False)
        at_edge("relay", True)
        body(*ins, *outs, *scr)
        at_edge("finish", True)

    if grid is not None:
        kw["grid"] = grid
    return list(pl.pallas_call(
        wrapped, out_shape=list(out_shape) + h_outs, in_specs=list(in_specs) + [_HBM] * len(h_ops),
        out_specs=list(out_specs) + [_HBM] * len(h_outs), scratch_shapes=list(scratch_shapes) + h_sems, **kw,
    )(*operands, *h_ops))


def _exchange(name, hooks):
    return _call(lambda: None, hooks, [], name=name, in_specs=[], out_specs=[], out_shape=[])


def _rows(ref, start, n):
    return ref.at[pl.ds(start, n), :]


def _ag_hook(shard, early=True):
    rows = shard.shape[0]

    def block(full, dev_index):
        return _rows(full, dev_index * rows, rows)

    def start(ins, outs, rcopy, lcopy):
        x, y, c = _my_place()
        src, dst = ins[0], block(outs[0], 4 * x + 2 * y + c)
        lcopy(0, src, dst).start()
        rcopy(0, src, dst, (x, y, 1 - c)).start()
        for j, (dev, _) in enumerate(_ici_peers(x, y, c)):
            rcopy(1 + j, src, dst, dev).start()

    def relay(ins, outs, rcopy, lcopy):
        x, y, c = _my_place()
        for j, (dev, chip) in enumerate(_ici_peers(x, y, c)):
            blk = block(outs[0], 2 * chip + c)
            rcopy(1 + j, blk, blk, dev).wait_recv()
            rcopy(4 + j, blk, blk, (x, y, 1 - c)).start()

    def finish(ins, outs, rcopy, lcopy):
        if not early:
            relay(ins, outs, rcopy, lcopy)
        x, y, c = _my_place()
        sib = (x, y, 1 - c)
        full = outs[0]
        peers = _ici_peers(x, y, c)
        blk = block(full, 2 * (2 * x + y) + 1 - c)
        rcopy(0, blk, blk, sib).wait_recv()
        for j, (dev, chip) in enumerate(peers):
            blk = block(full, 2 * chip + 1 - c)
            rcopy(4 + j, blk, blk, sib).wait_recv()
        src, dst = ins[0], block(full, 4 * x + 2 * y + c)
        lcopy(0, src, dst).wait()
        rcopy(0, src, dst, sib).wait_send()
        for j, (dev, chip) in enumerate(peers):
            rcopy(1 + j, src, dst, dev).wait_send()
            blk = block(full, 2 * chip + c)
            rcopy(4 + j, blk, blk, sib).wait_send()

    return _Hook([shard], [jax.ShapeDtypeStruct((NDEV * rows, D), shard.dtype)], 7, 1, start, finish, relay if early else None)


SIBLING, CHIPS, SIBLING_AND_CHIPS, EVERYONE = "sibling", "chips", "sibling and chips", "everyone"
REACH_ID = {SIBLING: 5, CHIPS: 6, EVERYONE: 7}


def _sequence(name, collective_id, reach, hooks):
    ins = [[jax.new_ref(a, memory_space=pltpu.MemorySpace.HBM) for a in h.operands] for h in hooks]
    outs = [[jax.empty_ref(s, memory_space=pltpu.MemorySpace.HBM) for s in h.out_shapes] for h in hooks]
    sems = tuple(pltpu.SemaphoreType.DMA((n,)) for h in hooks for n in (h.n_remote, h.n_remote, max(h.n_local, 1)))

    @pl.kernel(mesh=plsc.ScalarSubcoreMesh(axis_name="sequencer", num_cores=1), name=name, scratch_types=sems,
               compiler_params=pltpu.CompilerParams(collective_id=collective_id))
    def launch(*sem_refs):
        x, y, c = _my_place()
        chips = [dev for dev, _ in _ici_peers(x, y, c)]
        others = [(1 - x if dx else x, 1 - y if dy else y, 1 - c if dc else c)
                  for dx in range(2) for dy in range(2) for dc in range(2) if dx + dy + dc]
        devs = {SIBLING: [(x, y, 1 - c)], CHIPS: chips, SIBLING_AND_CHIPS: [(x, y, 1 - c)] + chips, EVERYONE: others}[reach]
        barrier = pltpu.get_barrier_semaphore()
        for dev in devs:
            pl.semaphore_signal(barrier, inc=1, device_id=dev, device_id_type=MESH_ID)
        pl.semaphore_wait(barrier, len(devs))
        for phase in ("start", "relay", "finish"):
            for i, h in enumerate(hooks):
                ssem, rsem, lsem = sem_refs[3 * i:3 * i + 3]

                def rcopy(k, src, dst, dev, ssem=ssem, rsem=rsem):
                    return pltpu.make_async_remote_copy(src_ref=src, dst_ref=dst, send_sem=ssem.at[k], recv_sem=rsem.at[k],
                                                        device_id=dev, device_id_type=MESH_ID)

                def lcopy(k, src, dst, lsem=lsem):
                    return pltpu.make_async_copy(src, dst, lsem.at[k])

                getattr(h, phase)(ins[i], outs[i], rcopy, lcopy)

    launch()
    return [o[...] for os in outs for o in os]


def _rs_d2d_hook(gfull):
    rows = gfull.shape[0] // NDEV

    def pairs(g, land):
        x, y, c = _my_place()
        return (x, y, 1 - c), [(k, _rows(g, (2 * k + 1 - c) * rows, rows), land.at[k]) for k in range(NCHIP)]

    def start(ins, outs, rcopy, lcopy):
        sib, cps = pairs(ins[0], outs[0])
        for i, src, dst in cps:
            rcopy(i, src, dst, sib).start()

    def finish(ins, outs, rcopy, lcopy):
        sib, cps = pairs(ins[0], outs[0])
        for i, src, dst in cps:
            rcopy(i, dst, dst, sib).wait_recv()
        for i, src, dst in cps:
            rcopy(i, src, dst, sib).wait_send()

    return _Hook([gfull], [jax.ShapeDtypeStruct((NCHIP, rows, D), gfull.dtype)], NCHIP, 0, start, finish)


def _rs_ici_hook(part):
    def start(ins, outs, rcopy, lcopy):
        x, y, c = _my_place()
        mychip = 2 * x + y
        lcopy(0, ins[0].at[mychip], outs[0].at[mychip]).start()
        for j, (dev, chip) in enumerate(_ici_peers(x, y, c)):
            rcopy(j, ins[0].at[chip], outs[0].at[mychip], dev).start()

    def finish(ins, outs, rcopy, lcopy):
        x, y, c = _my_place()
        mychip = 2 * x + y
        peers = _ici_peers(x, y, c)
        for j, (dev, chip) in enumerate(peers):
            rcopy(j, outs[0].at[chip], outs[0].at[chip], dev).wait_recv()
        for j, (dev, chip) in enumerate(peers):
            rcopy(j, ins[0].at[chip], outs[0].at[mychip], dev).wait_send()
        lcopy(0, ins[0].at[mychip], outs[0].at[mychip]).wait()

    return _Hook([part], [jax.ShapeDtypeStruct(part.shape, part.dtype)], 3, 1, start, finish)


def _small_hook(arrays):
    n = len(arrays)

    def peers():
        x, y, c = _my_place()
        out = []
        for dx in range(2):
            for dy in range(2):
                for dc in range(2):
                    if dx + dy + dc:
                        px, py, pc = (1 - x if dx else x), (1 - y if dy else y), (1 - c if dc else c)
                        out.append(((px, py, pc), 4 * px + 2 * py + pc))
        return 4 * x + 2 * y + c, out

    def start(ins, outs, rcopy, lcopy):
        me, ps = peers()
        for t in range(n):
            lcopy(t, ins[t], outs[t].at[me]).start()
            for i, (dev, _) in enumerate(ps):
                rcopy(n * i + t, ins[t], outs[t].at[me], dev).start()

    def finish(ins, outs, rcopy, lcopy):
        me, ps = peers()
        for t in range(n):
            for i, (dev, peer) in enumerate(ps):
                rcopy(n * i + t, outs[t].at[peer], outs[t].at[peer], dev).wait_recv()
            for i, (dev, _) in enumerate(ps):
                rcopy(n * i + t, ins[t], outs[t].at[me], dev).wait_send()
            lcopy(t, ins[t], outs[t].at[me]).wait()

    return _Hook(arrays, [jax.ShapeDtypeStruct((NDEV,) + a.shape, a.dtype) for a in arrays], 7 * n, n, start, finish)


def _wblock(w):
    return _resident(w.shape, lambda *_: (0, 0))


def _ffn_fwd(name, x, nrm, wg, wu, wd, hooks=(), tm=512):
    S = x.shape[0]

    def body(x_ref, n_ref, wg_ref, wu_ref, wd_ref, y_ref, g_ref, u_ref, a_ref, h_ref, acc_ref):
        xv = x_ref[...]
        _, _, h = _rms_fwd(xv, n_ref[...])
        h = h.astype(CDT)
        h_ref[...] = h
        for ci in range(F // FT):
            sl = slice(ci * FT, (ci + 1) * FT)
            g = _mm_nt(h, wg_ref[sl, :])
            u = _mm_nt(h, wu_ref[sl, :])
            g_ref[:, sl] = g.astype(CDT)
            u_ref[:, sl] = u.astype(CDT)
            a = (g * jax.nn.sigmoid(g) * u).astype(CDT)
            a_ref[:, sl] = a
            o = _mm(a, wd_ref[sl, :])
            if ci == 0:
                acc_ref[...] = o
            else:
                acc_ref[...] += o
        y_ref[...] = xv + 0.5 * acc_ref[...]

    tok = pl.BlockSpec((tm, D), lambda i: (i, 0))
    hid = pl.BlockSpec((tm, F), lambda i: (i, 0))
    hidden = jax.ShapeDtypeStruct((S, F), CDT)
    return _call(
        body, hooks, [x, nrm, wg, wu, wd], name=name, grid=(S // tm,),
        out_shape=[jax.ShapeDtypeStruct((S, D), f32), hidden, hidden, hidden, jax.ShapeDtypeStruct((S, D), CDT)],
        in_specs=[tok, _resident((1, D), lambda i: (0, 0)), _wblock(wg), _wblock(wu), _wblock(wd)],
        out_specs=[tok, hid, hid, hid, tok],
        scratch_shapes=[pltpu.VMEM((tm, D), f32)],
        compiler_params=_params("arbitrary"),
    )


def _proj_fwd(x1, nrm, wfull, b3, cos, sin, hooks=(), tm=512):
    S = x1.shape[0]

    def body(x_ref, n_ref, w_ref, b_ref, cos_ref, sin_ref, p_ref, h_ref):
        _, _, h = _rms_fwd(x_ref[...], n_ref[...])
        h = h.astype(CDT)
        h_ref[...] = h
        for p in range(8):
            seg = SEG_OF_SLOT[p]
            z = _mm(h, w_ref[seg * D:(seg + 1) * D, :]) + b_ref[seg]
            if p in (SLOT_Q, SLOT_K):
                co, si = cos_ref[...], sin_ref[...]
                for hh in range(H):
                    cs = slice(hh * DK, (hh + 1) * DK)
                    zr = _rotate(z[:, cs], co, si)
                    p_ref[p, :, cs] = (zr * K_SCALE if p == SLOT_K else zr).astype(CDT)
            else:
                p_ref[p] = z.astype(CDT)

    tab = pl.BlockSpec((tm, DK // 2), lambda i: (i, 0))
    return _call(
        body, hooks, [x1, nrm, wfull, b3, cos, sin], name="proj_fwd", grid=(S // tm,),
        out_shape=[jax.ShapeDtypeStruct((8, S, D), CDT), jax.ShapeDtypeStruct((S, D), CDT)],
        in_specs=[pl.BlockSpec((tm, D), lambda i: (i, 0)), _resident((1, D), lambda i: (0, 0)),
                  _resident((8 * D, D), lambda i: (0, 0)), _resident((8, 1, D), lambda i: (0, 0, 0)), tab, tab],
        out_specs=[pl.BlockSpec((8, tm, D), lambda i: (0, i, 0)), pl.BlockSpec((tm, D), lambda i: (i, 0))],
        compiler_params=_params("arbitrary"),
    )


def _sgu_norm(va, gn, bn):
    mu = jnp.mean(va, axis=-1, keepdims=True)
    xc = va - mu
    rstd = lax.rsqrt(jnp.mean(xc * xc, axis=-1, keepdims=True) + EPS)
    vhat = xc * rstd
    return rstd, vhat, vhat * gn + bn


def _sgu_fwd(proj, gn, bn, ws, bsc, tm=512):
    S = proj.shape[1]
    GW = D // G

    def body(p_ref, gn_ref, bn_ref, ws_ref, bs_ref, a_ref):
        ua = _gelu(p_ref[0].astype(f32))
        va = _gelu(p_ref[1].astype(f32))
        _, _, vn = _sgu_norm(va, gn_ref[...], bn_ref[...])
        vn = vn.astype(CDT)
        for ch in range(tm // C):
            rs = slice(ch * C, (ch + 1) * C)
            for gi in range(G):
                cs = slice(gi * GW, (gi + 1) * GW)
                s = _mm(ws_ref[gi], vn[rs, cs]) + bs_ref[gi]
                a_ref[rs, cs] = (ua[rs, cs] * s).astype(CDT)

    return pl.pallas_call(
        body, name="sgu_fwd", grid=(S // tm,),
        out_shape=jax.ShapeDtypeStruct((S, D), CDT),
        in_specs=[pl.BlockSpec((2, tm, D), lambda i: (0, i, 0)), _resident((1, D), lambda i: (0, 0)),
                  _resident((1, D), lambda i: (0, 0)), _resident((G, C, C), lambda i: (0, 0, 0)),
                  _resident((G, C, 1), lambda i: (0, 0, 0))],
        out_specs=pl.BlockSpec((tm, D), lambda i: (i, 0)),
        compiler_params=_params("arbitrary"),
    )(proj, gn, bn, ws, bsc)


def _decay_tables(dl_ref):
    lg = jax.nn.log_sigmoid(dl_ref[0:2, :])
    lgf, lgb = lg[0:1, :], lg[1:2, :]
    assert RC <= DK
    ri = lax.broadcasted_iota(jnp.int32, (RC, RC), 0)
    ci = lax.broadcasted_iota(jnp.int32, (RC, RC), 1)
    d = (ri - ci).astype(f32)
    lower = d >= 0
    dmat = jnp.where(lower, jnp.exp(d * lgf[:, :RC]), jnp.exp(-d * lgb[:, :RC]))
    dmat_t = jnp.where(d <= 0, jnp.exp(-d * lgf[:, :RC]), jnp.exp(d * lgb[:, :RC]))
    pos = lax.broadcasted_iota(jnp.int32, (RC, DK), 0).astype(f32)
    t = dict(
        lgf=lgf, lgb=lgb, d=d, lower=lower, dmat=dmat, dmat_t=dmat_t, pos=pos,
        fq=jnp.exp((pos + 1.0) * lgf), fk=jnp.exp((RC - 1.0 - pos) * lgf),
        bq=jnp.exp((RC - pos) * lgb), bk=jnp.exp(pos * lgb),
        lamf=jnp.exp(float(RC) * lgf), lamb=jnp.exp(float(RC) * lgb),
    )
    return t


def _rotate(t, co, si):
    t1, t2 = t[:, :DK // 2], t[:, DK // 2:]
    return jnp.concatenate([t1 * co - t2 * si, t2 * co + t1 * si], axis=-1)


def _unrotate(t, co, si):
    t1, t2 = t[:, :DK // 2], t[:, DK // 2:]
    return jnp.concatenate([t1 * co + t2 * si, t2 * co - t1 * si], axis=-1)


K_SCALE = DK ** -0.5
ROW_TILE = 256


def _ret_fwd(proj, dl, hooks=()):
    S = proj.shape[1]
    NC = S // RC

    def body(q_ref, k_ref, v_ref, g_ref, dl_ref, R_ref, r_ref, sfs_ref, sbs_ref, rb_ref, sf_ref, sb_ref):
        t = _decay_tables(dl_ref)

        def chunk(n):
            rows = pl.ds(pl.multiple_of(n * RC, RC), RC)
            return rows, q_ref[rows, :], k_ref[rows, :], v_ref[rows, :]

        sf_ref[...] = jnp.zeros_like(sf_ref)
        sb_ref[...] = jnp.zeros_like(sb_ref)

        def step(i, carry):
            rows, qn, kn, vn = chunk(i)
            sc = _mm_nt(qn, kn) * t["dmat"]
            out = _mm(sc.astype(CDT), vn)
            sf = sf_ref[...]
            sfb = sf.astype(CDT)
            sfs_ref[i] = sfb
            R_ref[rows, :] = out + _mm((qn.astype(f32) * t["fq"]).astype(CDT), sfb)
            sf_ref[...] = sf * t["lamf"] + _mm_tn((kn.astype(f32) * t["fk"]).astype(CDT), vn)
            m = NC - 1 - i
            rows, qn, kn, vn = chunk(m)
            sb = sb_ref[...]
            sbb = sb.astype(CDT)
            sbs_ref[m] = sbb
            rb_ref[rows, :] = _mm((qn.astype(f32) * t["bq"]).astype(CDT), sbb)
            sb_ref[...] = sb * t["lamb"] + _mm_tn((kn.astype(f32) * t["bk"]).astype(CDT), vn)
            return carry

        lax.fori_loop(0, NC, step, 0)

        def finish(i, carry):
            rs = pl.ds(pl.multiple_of(i * ROW_TILE, ROW_TILE), ROW_TILE)
            R = R_ref[rs, :] + rb_ref[rs, :]
            R_ref[rs, :] = R
            rn = R * lax.rsqrt(jnp.mean(R * R, axis=-1, keepdims=True) + EPS)
            g = g_ref[rs, :].astype(f32)
            r_ref[rs, :] = (rn * g * jax.nn.sigmoid(g)).astype(CDT)
            return carry

        lax.fori_loop(0, S // ROW_TILE, finish, 0)

    def seg(slot):
        return pl.BlockSpec((None, S, DK), lambda h: (slot, 0, h))

    states = jax.ShapeDtypeStruct((H, NC, DK, DK), CDT)
    state_blk = pl.BlockSpec((None, NC, DK, DK), lambda h: (h, 0, 0, 0))
    return _call(
        body, hooks, [proj, proj, proj, proj, dl], name="ret_fwd", grid=(H,),
        out_shape=[jax.ShapeDtypeStruct((S, H * DK), f32), jax.ShapeDtypeStruct((S, H * DK), CDT), states, states],
        in_specs=[seg(SLOT_Q), seg(SLOT_K), seg(SLOT_VR), seg(SLOT_GR), pl.BlockSpec((None, 8, DK), lambda h: (h, 0, 0))],
        out_specs=[pl.BlockSpec((S, DK), lambda h: (0, h)), pl.BlockSpec((S, DK), lambda h: (0, h)), state_blk, state_blk],
        scratch_shapes=[pltpu.VMEM((S, DK), f32), pltpu.VMEM((DK, DK), f32), pltpu.VMEM((DK, DK), f32)],
        compiler_params=_params("arbitrary"),
    )


def _merge_fwd(a, r, proj, x1, wa, wb, wo, hooks=(), tm=512):
    S = x1.shape[0]

    def body(a_ref, r_ref, gt_ref, x_ref, wa_ref, wb_ref, wo_ref, x2_ref, ya_ref, yb_ref):
        ya = _mm(a_ref[...], wa_ref[...])
        yb = _mm(r_ref[...], wb_ref[...])
        ya_ref[...] = ya.astype(CDT)
        yb_ref[...] = yb.astype(CDT)
        mix = jax.nn.sigmoid(gt_ref[0].astype(f32)) * ya + jax.nn.sigmoid(gt_ref[1].astype(f32)) * yb
        x2_ref[...] = x_ref[...] + _mm(mix.astype(CDT), wo_ref[...])

    tok = pl.BlockSpec((tm, D), lambda i: (i, 0))
    return _call(
        body, hooks, [a, r, proj, x1, wa, wb, wo], name="merge_fwd", grid=(S // tm,),
        out_shape=[jax.ShapeDtypeStruct((S, D), f32), jax.ShapeDtypeStruct((S, D), CDT), jax.ShapeDtypeStruct((S, D), CDT)],
        in_specs=[tok, tok, pl.BlockSpec((2, tm, D), lambda i: (SLOT_GA // 2, i, 0)), tok,
                  _wblock(wa), _wblock(wb), _wblock(wo)],
        out_specs=[tok, tok, tok],
        compiler_params=_params("arbitrary"),
    )


def _loss_head(x3, fn, target, tm=512):
    S = x3.shape[0]

    def body(x_ref, n_ref, t_ref, dx_ref, dxh_ref, dn_ref, l_ref):
        n = n_ref[...]
        r, xh, y = _rms_fwd(x_ref[...], n)
        e = y - t_ref[...]
        dy = e * (1.0 / D)
        dx, dn = _rms_bwd(dy, r, xh, n)
        dx_ref[...] = dx
        dxh_ref[...] = (0.5 * dx).astype(CDT)
        part = 0.5 * jnp.sum(jnp.sum(e * e, axis=-1, keepdims=True), axis=0, keepdims=True) * (1.0 / D)

        @pl.when(pl.program_id(0) == 0)
        def _():
            dn_ref[...] = jnp.zeros_like(dn_ref)
            l_ref[...] = jnp.zeros_like(l_ref)

        dn_ref[...] += dn
        l_ref[...] += jnp.broadcast_to(part, l_ref.shape)

    tok = pl.BlockSpec((tm, D), lambda i: (i, 0))
    return pl.pallas_call(
        body, name="loss_head", grid=(S // tm,),
        out_shape=[jax.ShapeDtypeStruct((S, D), f32), jax.ShapeDtypeStruct((S, D), CDT), jax.ShapeDtypeStruct((1, D), f32),
                   jax.ShapeDtypeStruct((8, 128), f32)],
        in_specs=[tok, _resident((1, D), lambda i: (0, 0)), tok],
        out_specs=[tok, tok, pl.BlockSpec((1, D), lambda i: (0, 0)), pl.BlockSpec((8, 128), lambda i: (0, 0))],
        compiler_params=_params("arbitrary"),
    )(x3, fn, target)


def _ffn_bwd_hidden(name, dyh, g, u, wd, hooks=(), tm=512):
    S = dyh.shape[0]

    def body(dyh_ref, g_ref, u_ref, wd_ref, dg_ref, du_ref):
        dyh = dyh_ref[...]
        for ci in range(F // FT):
            sl = slice(ci * FT, (ci + 1) * FT)
            da = _mm_nt(dyh, wd_ref[sl, :])
            gv = g_ref[:, sl].astype(f32)
            uv = u_ref[:, sl].astype(f32)
            s = jax.nn.sigmoid(gv)
            du_ref[:, sl] = (da * (gv * s)).astype(CDT)
            dg_ref[:, sl] = (da * uv * (s * (1.0 + gv * (1.0 - s)))).astype(CDT)

    hid = pl.BlockSpec((tm, F), lambda i: (i, 0))
    hidden = jax.ShapeDtypeStruct((S, F), CDT)
    return _call(
        body, hooks, [dyh, g, u, wd], name=name, grid=(S // tm,), out_shape=[hidden, hidden],
        in_specs=[pl.BlockSpec((tm, D), lambda i: (i, 0)), hid, hid, _wblock(wd)], out_specs=[hid, hid],
        compiler_params=_params("arbitrary"),
    )


def _ffn_bwd_in(name, dy, x, dg, du, nrm, wg, wu, hooks=(), part=None, prev=None, tm=512):
    S = x.shape[0]
    t0, nt = part or (0, S // tm)

    def body(dy_ref, x_ref, dg_ref, du_ref, n_ref, wg_ref, wu_ref, *rest):
        dx_ref, dn_ref, acc_ref = rest[-3:]
        n = n_ref[...]
        r, xh, _ = _rms_fwd(x_ref[...], n)
        for ci in range(F // FT):
            sl = slice(ci * FT, (ci + 1) * FT)
            dh = _mm(dg_ref[:, sl], wg_ref[sl, :]) + _mm(du_ref[:, sl], wu_ref[sl, :])
            if ci == 0:
                acc_ref[...] = dh
            else:
                acc_ref[...] += dh
        dx, dn = _rms_bwd(acc_ref[...], r, xh, n)
        dx_ref[...] = dy_ref[...] + dx

        @pl.when(pl.program_id(0) == 0)
        def _():
            dn_ref[...] = jnp.zeros_like(dn_ref) if prev is None else rest[1][...]

        dn_ref[...] += dn

    tok = pl.BlockSpec((tm, D), lambda i: (t0 + i, 0))
    hid = pl.BlockSpec((tm, F), lambda i: (t0 + i, 0))
    row = pl.BlockSpec((1, D), lambda i: (0, 0))
    in_specs = [tok, tok, hid, hid, _resident((1, D), lambda i: (0, 0)), _wblock(wg), _wblock(wu)]
    operands = [dy, x, dg, du, nrm, wg, wu]
    aliases = {}
    if prev is not None:
        in_specs += [_HBM, row]
        operands += list(prev)
        aliases = {7: 0}
    return _call(
        body, hooks, operands, name=name, grid=(nt,),
        out_shape=[jax.ShapeDtypeStruct((S, D), f32), jax.ShapeDtypeStruct((1, D), f32)],
        in_specs=in_specs, out_specs=[tok, row],
        scratch_shapes=[pltpu.VMEM((tm, D), f32)],
        input_output_aliases=aliases,
        compiler_params=_params("arbitrary"),
    )


TN_ROWS = 512


def _tn(name, xs, ys, block_of, hooks=()):
    S, M = xs.shape
    B = ys.shape[0]
    tr = TN_ROWS if M % TN_ROWS == 0 else M // 2
    assert M % tr == 0 and tr % 128 == 0
    nt = M // tr

    def body(x_ref, y_ref, o_ref):
        o_ref[...] = _mm_tn(x_ref[...], y_ref[...]).astype(CDT)

    return _call(
        body, hooks, [xs, ys], name=name, grid=(B, nt),
        out_shape=[jax.ShapeDtypeStruct((B * M, D), CDT)],
        in_specs=[pl.BlockSpec((S, tr), lambda b, i: (0, i)), pl.BlockSpec((None, S, D), lambda b, i: (b, 0, 0))],
        out_specs=[pl.BlockSpec((tr, D), lambda b, i: (block_of(b) * nt + i, 0))],
        compiler_params=_params("arbitrary", "arbitrary"),
    )


def _wgrad(name, xs, y, hooks=()):
    return _tn(name, xs, y[None], lambda b: 0, hooks)


def _merge_bwd_act(dx2, ya, yb, proj, wa, wb, wo, hooks=(), tm=512):
    S = dx2.shape[0]

    def body(dx_ref, ya_ref, yb_ref, gt_ref, wa_ref, wb_ref, wo_ref,
             dp_ref, da_ref, dr_ref, mix_ref, dxb_ref, dya_ref, dyb_ref):
        dxb = dx_ref[...].astype(CDT)
        dxb_ref[...] = dxb
        dmix = _mm_nt(dxb, wo_ref[...])
        ya = ya_ref[...].astype(f32)
        yb = yb_ref[...].astype(f32)
        sa = jax.nn.sigmoid(gt_ref[0].astype(f32))
        sb = jax.nn.sigmoid(gt_ref[1].astype(f32))
        mix_ref[...] = (sa * ya + sb * yb).astype(CDT)
        dya = (dmix * sa).astype(CDT)
        dyb = (dmix * sb).astype(CDT)
        dya_ref[...] = dya
        dyb_ref[...] = dyb
        dp_ref[0] = (dmix * ya * sa * (1.0 - sa)).astype(CDT)
        dp_ref[1] = (dmix * yb * sb * (1.0 - sb)).astype(CDT)
        da_ref[...] = _mm_nt(dya, wa_ref[...]).astype(CDT)
        dr_ref[...] = _mm_nt(dyb, wb_ref[...]).astype(CDT)

    tok = pl.BlockSpec((tm, D), lambda i: (i, 0))
    gates = pl.BlockSpec((2, tm, D), lambda i: (SLOT_GA // 2, i, 0))
    act = jax.ShapeDtypeStruct((S, D), CDT)
    return _call(
        body, hooks, [dx2, ya, yb, proj, wa, wb, wo], name="merge_bwd_act", grid=(S // tm,),
        out_shape=[jax.ShapeDtypeStruct((8, S, D), CDT), act, act, act, act, act, act],
        in_specs=[tok, tok, tok, gates, _wblock(wa), _wblock(wb), _wblock(wo)],
        out_specs=[gates, tok, tok, tok, tok, tok, tok],
        compiler_params=_params("arbitrary"),
    )


def _sgu_bwd(da, proj, dproj, gn, bn, ws, wst, bsc, hooks=(), tm=512):
    S = proj.shape[1]
    GW = D // G

    def body(da_ref, p_ref, dpin_ref, gn_ref, bn_ref, ws_ref, wst_ref, bs_ref,
             dp_ref, dws_ref, dbs_ref, dgn_ref, dbn_ref, ds_ref, dvn_ref):
        @pl.when(pl.program_id(0) == 0)
        def _():
            dws_ref[...] = jnp.zeros_like(dws_ref)
            dbs_ref[...] = jnp.zeros_like(dbs_ref)
            dgn_ref[...] = jnp.zeros_like(dgn_ref)
            dbn_ref[...] = jnp.zeros_like(dbn_ref)

        pu = p_ref[0].astype(f32)
        pv = p_ref[1].astype(f32)
        ua = _gelu(pu)
        va = _gelu(pv)
        gn = gn_ref[...]
        rstd, vhat, vn = _sgu_norm(va, gn, bn_ref[...])
        vnb = vn.astype(CDT)
        dav = da_ref[...].astype(f32)
        dsb = (dav * ua).astype(CDT)
        ones = jnp.ones((8, GW), CDT)
        for ch in range(tm // C):
            rs = slice(ch * C, (ch + 1) * C)
            for gi in range(G):
                cs = slice(gi * GW, (gi + 1) * GW)
                s = _mm(ws_ref[gi], vnb[rs, cs]) + bs_ref[gi]
                ds_ref[rs, cs] = s
                dsg = dsb[rs, cs]
                dws_ref[gi] += _mm_nt(dsg, vnb[rs, cs])
                dbs_ref[gi] += _mm_nt(ones, dsg)
                dvn_ref[rs, cs] = _mm(wst_ref[gi], dsg)
        dp_ref[0] = (dav * ds_ref[...] * _gelu_grad(pu)).astype(CDT)
        dvn = dvn_ref[...]
        dgn_ref[...] += jnp.sum(dvn * vhat, axis=0, keepdims=True)
        dbn_ref[...] += jnp.sum(dvn, axis=0, keepdims=True)
        dvh = dvn * gn
        dva = rstd * (dvh - jnp.mean(dvh, axis=-1, keepdims=True) - vhat * jnp.mean(dvh * vhat, axis=-1, keepdims=True))
        dp_ref[1] = (dva * _gelu_grad(pv)).astype(CDT)

    uv = pl.BlockSpec((2, tm, D), lambda i: (0, i, 0))
    row = _resident((1, D), lambda i: (0, 0))
    return _call(
        body, hooks, [da, proj, dproj, gn, bn, ws, wst, bsc], name="sgu_bwd", grid=(S // tm,),
        out_shape=[jax.ShapeDtypeStruct(dproj.shape, CDT), jax.ShapeDtypeStruct((G, C, C), f32),
                   jax.ShapeDtypeStruct((G, 8, C), f32), jax.ShapeDtypeStruct((1, D), f32), jax.ShapeDtypeStruct((1, D), f32)],
        in_specs=[pl.BlockSpec((tm, D), lambda i: (i, 0)), uv, _HBM, row, row,
                  _resident((G, C, C), lambda i: (0, 0, 0)), _resident((G, C, C), lambda i: (0, 0, 0)),
                  _resident((G, C, 1), lambda i: (0, 0, 0))],
        out_specs=[uv, pl.BlockSpec((G, C, C), lambda i: (0, 0, 0)), pl.BlockSpec((G, 8, C), lambda i: (0, 0, 0)),
                   pl.BlockSpec((1, D), lambda i: (0, 0)), pl.BlockSpec((1, D), lambda i: (0, 0))],
        scratch_shapes=[pltpu.VMEM((tm, D), f32), pltpu.VMEM((tm, D), f32)],
        input_output_aliases={2: 0},
        compiler_params=_params("arbitrary"),
    )


def _ret_bwd(dr, R, sfs, sbs, proj, dproj, cos, sin, dl, hooks=()):
    S = proj.shape[1]
    NC = S // RC
    assert NC % 2 == 0

    def body(dr_ref, R_ref, sf_ref, sb_ref, q_ref, k_ref, v_ref, g_ref, dpin_ref, cos_ref, sin_ref, dl_ref,
             dp_ref, dd_ref, dR_ref, gb_ref, gf_ref, acc_ref):
        t = _decay_tables(dl_ref)

        def gate_norm_bwd(i, carry):
            rs = pl.ds(pl.multiple_of(i * ROW_TILE, ROW_TILE), ROW_TILE)
            Rv = R_ref[rs, :]
            rstd = lax.rsqrt(jnp.mean(Rv * Rv, axis=-1, keepdims=True) + EPS)
            rn = Rv * rstd
            gv = g_ref[rs, :].astype(f32)
            s = jax.nn.sigmoid(gv)
            drv = dr_ref[rs, :].astype(f32)
            dp_ref[3, rs, :] = (drv * rn * (s * (1.0 + gv * (1.0 - s)))).astype(CDT)
            drn = drv * gv * s
            dR_ref[rs, :] = (rstd * (drn - rn * jnp.mean(drn * rn, axis=-1, keepdims=True))).astype(CDT)
            return carry

        lax.fori_loop(0, S // ROW_TILE, gate_norm_bwd, 0)

        def chunk(n):
            rows = pl.ds(pl.multiple_of(n * RC, RC), RC)
            return rows, q_ref[rows, :], k_ref[rows, :], v_ref[rows, :], dR_ref[rows, :]

        def emit_kv(rows, dk, dv, final):
            if not final:
                dp_ref[1, rows, :] = dk.astype(CDT)
                dp_ref[2, rows, :] = dv.astype(CDT)
            else:
                co, si = cos_ref[rows, :], sin_ref[rows, :]
                dk = dp_ref[1, rows, :].astype(f32) + dk
                dp_ref[1, rows, :] = (_unrotate(dk, co, si) * K_SCALE).astype(CDT)
                dp_ref[2, rows, :] = (dp_ref[2, rows, :].astype(f32) + dv).astype(CDT)

        gb_ref[...] = jnp.zeros_like(gb_ref)
        gf_ref[...] = jnp.zeros_like(gf_ref)
        acc_ref[...] = jnp.zeros_like(acc_ref)
        dpos = jnp.abs(t["d"])

        def ascend(n, final):
            rows, qn, kn, vn, dRn = chunk(n)
            qf, kf = qn.astype(f32), kn.astype(f32)
            sc = _mm_nt(qn, kn)
            dA = _mm_nt(dRn, vn)
            w = sc * dA * t["dmat"] * dpos
            lgf_part = jnp.sum(jnp.where(t["lower"], w, 0.0), axis=0, keepdims=True)
            lgb_part = jnp.sum(jnp.where(t["lower"], 0.0, w), axis=0, keepdims=True)
            dsc = (dA * t["dmat"]).astype(CDT)
            dq = _mm(dsc, kn)
            scT = (_mm_nt(kn, qn) * t["dmat_t"]).astype(CDT)
            dscT = (_mm_nt(vn, dRn) * t["dmat_t"]).astype(CDT)
            dk = _mm(dscT, qn)
            dv = _mm(scT, dRn)
            sfb = sf_ref[n]
            sbb = sb_ref[n]
            qdf = qf * t["fq"]
            dqdf = _mm_nt(dRn, sfb)
            dq += dqdf * t["fq"]
            lgf_row = jnp.sum(qdf * dqdf * (t["pos"] + 1.0), axis=0, keepdims=True)
            qdb = qf * t["bq"]
            dqdb = _mm_nt(dRn, sbb)
            dq += dqdb * t["bq"]
            lgb_row = jnp.sum(qdb * dqdb * (RC - t["pos"]), axis=0, keepdims=True)
            gb = gb_ref[...]
            gbb = gb.astype(CDT)
            kdb = kf * t["bk"]
            dkdb = _mm_nt(vn, gbb)
            dk += dkdb * t["bk"]
            dv += _mm(kdb.astype(CDT), gbb)
            lgb_row += jnp.sum(kdb * dkdb * t["pos"], axis=0, keepdims=True)
            lgb_row += float(RC) * t["lamb"] * jnp.sum(gb * sbb.astype(f32), axis=0, keepdims=True)
            co, si = cos_ref[rows, :], sin_ref[rows, :]
            dp_ref[0, rows, :] = _unrotate(dq, co, si).astype(CDT)
            emit_kv(rows, dk, dv, final)
            acc_ref[0:1, :] += lgf_row + lgf_part
            acc_ref[1:2, :] += lgb_row + lgb_part
            gb_ref[...] = gb * t["lamb"] + _mm_tn(qdb.astype(CDT), dRn)

        def descend(n, final):
            rows, qn, kn, vn, dRn = chunk(n)
            gf = gf_ref[...]
            gfb = gf.astype(CDT)
            kdf = kn.astype(f32) * t["fk"]
            dkdf = _mm_nt(vn, gfb)
            lgf_row = jnp.sum(kdf * dkdf * (RC - 1.0 - t["pos"]), axis=0, keepdims=True)
            lgf_row += float(RC) * t["lamf"] * jnp.sum(gf * sf_ref[n].astype(f32), axis=0, keepdims=True)
            acc_ref[0:1, :] += lgf_row
            emit_kv(rows, dkdf * t["fk"], _mm(kdf.astype(CDT), gfb), final)
            gf_ref[...] = gf * t["lamf"] + _mm_tn((qn.astype(f32) * t["fq"]).astype(CDT), dRn)

        def sweep(final):
            def step(i, carry):
                ascend(i, final)
                descend(NC - 1 - i, final)
                return carry
            return step

        lax.fori_loop(0, NC // 2, sweep(False), 0)
        lax.fori_loop(NC // 2, NC, sweep(True), 0)
        dlg = jnp.sum(acc_ref[...], axis=1, keepdims=True)
        dlogit = dlg * jax.nn.sigmoid(-dl_ref[:, 0:1])
        lane = lax.broadcasted_iota(jnp.int32, (8, 128), 1)
        dd_ref[...] = jnp.where(lane == pl.program_id(0), jnp.broadcast_to(dlogit, (8, 128)), 0.0)

    def seg(slot):
        return pl.BlockSpec((None, S, DK), lambda h: (slot, 0, h))

    head = pl.BlockSpec((S, DK), lambda h: (0, h))
    states = pl.BlockSpec((None, NC, DK, DK), lambda h: (h, 0, 0, 0))
    return _call(
        body, hooks, [dr, R, sfs, sbs, proj, proj, proj, proj, dproj, cos, sin, dl], name="ret_bwd", grid=(H,),
        out_shape=[jax.ShapeDtypeStruct(dproj.shape, CDT), jax.ShapeDtypeStruct((H, 8, 128), f32)],
        in_specs=[head, head, states, states, seg(SLOT_Q), seg(SLOT_K), seg(SLOT_VR), seg(SLOT_GR), _HBM,
                  _resident((S, DK // 2), lambda h: (0, 0)), _resident((S, DK // 2), lambda h: (0, 0)),
                  pl.BlockSpec((None, 8, DK), lambda h: (h, 0, 0))],
        out_specs=[pl.BlockSpec((4, S, DK), lambda h: (1, 0, h), pipeline_mode=pl.Buffered(1)),
                   pl.BlockSpec((None, 8, 128), lambda h: (h, 0, 0))],
        scratch_shapes=[pltpu.VMEM((S, DK), CDT),
                        pltpu.VMEM((DK, DK), f32), pltpu.VMEM((DK, DK), f32), pltpu.VMEM((8, DK), f32)],
        input_output_aliases={8: 0},
        compiler_params=_params("arbitrary"),
    )


def _proj_bwd_act(dproj, dx2, x1, nrm, wfull, hooks=(), tm=512):
    S = x1.shape[0]

    def body(dp_ref, dx2_ref, x_ref, n_ref, w_ref, dx_ref, dxh_ref, dn_ref, db_ref, acc_ref):
        @pl.when(pl.program_id(0) == 0)
        def _():
            dn_ref[...] = jnp.zeros_like(dn_ref)
            db_ref[...] = jnp.zeros_like(db_ref)

        for p in range(8):
            seg = SEG_OF_SLOT[p]
            dp = dp_ref[p]
            db_ref[seg] += jnp.sum(dp.astype(f32), axis=0, keepdims=True)
            dh = _mm_nt(dp, w_ref[seg * D:(seg + 1) * D, :])
            if p == 0:
                acc_ref[...] = dh
            else:
                acc_ref[...] += dh
        n = n_ref[...]
        r, xh, _ = _rms_fwd(x_ref[...], n)
        dx, dn = _rms_bwd(acc_ref[...], r, xh, n)
        dx = dx2_ref[...] + dx
        dx_ref[...] = dx
        dxh_ref[...] = (0.5 * dx).astype(CDT)
        dn_ref[...] += dn

    tok = pl.BlockSpec((tm, D), lambda i: (i, 0))
    return _call(
        body, hooks, [dproj, dx2, x1, nrm, wfull], name="proj_bwd_act", grid=(S // tm,),
        out_shape=[jax.ShapeDtypeStruct((S, D), f32), jax.ShapeDtypeStruct((S, D), CDT), jax.ShapeDtypeStruct((1, D), f32),
                   jax.ShapeDtypeStruct((8, 1, D), f32)],
        in_specs=[pl.BlockSpec((8, tm, D), lambda i: (0, i, 0)), tok, tok, _resident((1, D), lambda i: (0, 0)),
                  _resident((8 * D, D), lambda i: (0, 0))],
        out_specs=[tok, tok, pl.BlockSpec((1, D), lambda i: (0, 0)), pl.BlockSpec((8, 1, D), lambda i: (0, 0, 0))],
        scratch_shapes=[pltpu.VMEM((tm, D), f32)],
        compiler_params=_params("arbitrary"),
    )


def _rs_sum(name, gfulls, lands, my_c):
    n = len(gfulls)
    rows = gfulls[0].shape[0] // NDEV
    assert all(g.shape[0] == NDEV * rows for g in gfulls)

    def body(c_ref, *refs):
        for g_ref, l_ref, o_ref in zip(refs[:n], refs[n:2 * n], refs[2 * n:]):
            o_ref[...] = (g_ref[...].astype(f32) + l_ref[...].astype(f32)).astype(CDT)

    slot = pl.BlockSpec((None, rows, D), lambda k, c: (k, 0, 0))
    return pl.pallas_call(
        body, name=name,
        grid_spec=pltpu.PrefetchScalarGridSpec(
            num_scalar_prefetch=1, grid=(NCHIP,),
            in_specs=[pl.BlockSpec((rows, D), lambda k, c: (2 * k + c[0], 0))] * n + [slot] * n,
            out_specs=[slot] * n),
        out_shape=[jax.ShapeDtypeStruct((NCHIP, rows, D), CDT)] * n,
        compiler_params=_params("arbitrary"),
    )(my_c, *gfulls, *lands)


def _adamw_math(g, w, m, v):
    m2 = ADAM_B1 * m + (1.0 - ADAM_B1) * g
    v2 = ADAM_B2 * v + (1.0 - ADAM_B2) * (g * g)
    delta = -ADAM_LR * ((m2 / BC1) / (jnp.sqrt(v2 / BC2) + ADAM_EPS) + ADAM_WD * w)
    return delta, m2, v2


def _adamw_big(name, landed, w, m, v, after):
    rows = w.shape[0]
    tr = min(rows, 256) if rows % 256 == 0 else rows

    def body(l_ref, w_ref, m_ref, v_ref, after_ref, g_ref, d_ref, m2_ref, v2_ref):
        g = l_ref[0].astype(f32)
        for k in range(1, NCHIP):
            g = g + l_ref[k].astype(f32)
        g_ref[...] = g
        d_ref[...], m2_ref[...], v2_ref[...] = _adamw_math(g, w_ref[...], m_ref[...], v_ref[...])

    blk = pl.BlockSpec((tr, D), lambda i: (i, 0))
    o = jax.ShapeDtypeStruct((rows, D), f32)
    return pl.pallas_call(
        body, name=name, grid=(rows // tr,), out_shape=[o, o, o, o],
        in_specs=[pl.BlockSpec((NCHIP, tr, D), lambda i: (0, i, 0)), blk, blk, blk, _HBM],
        out_specs=[blk, blk, blk, blk],
        compiler_params=_params("arbitrary"),
    )(landed, w, m, v, after)


ROW_FFN1_NORM, ROW_MIX_NORM, ROW_SGU_G, ROW_SGU_B, ROW_FFN2_NORM, ROW_FINAL_NORM, ROW_B_IN = 0, 1, 2, 3, 4, 5, 8
ROW_WS, ROW_BS, ROW_DECAY = 0, G * C, G * C + G * 8


def _adamw_small(ga, gb, gn1, gl, params):
    def body(ga_ref, gb_ref, gn1_ref, gl_ref, *refs):
        ins, outs = refs[:30], refs[30:]

        def total(ref, r0, n):
            g = ref[0, r0:r0 + n, :]
            for j in range(1, NDEV):
                g = g + ref[j, r0:r0 + n, :]
            return g

        def apply(i, g, rows=slice(None)):
            w, m, v = ins[3 * i][rows, :], ins[3 * i + 1][rows, :], ins[3 * i + 2][rows, :]
            outs[4 * i][rows, :] = g
            outs[4 * i + 1][rows, :], outs[4 * i + 2][rows, :], outs[4 * i + 3][rows, :] = _adamw_math(g, w, m, v)

        outs[40][...] = total(gl_ref, 0, 8)
        apply(0, total(gn1_ref, 0, 1))
        for i, r in enumerate((ROW_FFN1_NORM, ROW_MIX_NORM, ROW_SGU_G, ROW_SGU_B, ROW_FFN2_NORM, ROW_FINAL_NORM)):
            if i:
                apply(i, total(ga_ref, r, 1))
        apply(6, total(ga_ref, ROW_B_IN, 8))
        apply(7, total(gb_ref, ROW_WS, G * C))
        for gi in range(G):
            apply(8, total(gb_ref, ROW_BS + 8 * gi, 1), slice(gi, gi + 1))
        dec = total(gb_ref, ROW_DECAY, 8)
        for hh in range(1, H):
            dec = dec + total(gb_ref, ROW_DECAY + 8 * hh, 8)
        apply(9, dec)

    flat = [a for p in params for a in p]
    out_shape = [jax.ShapeDtypeStruct(p[0].shape, f32) for p in params for _ in range(4)]
    out_shape.append(jax.ShapeDtypeStruct((8, 128), f32))
    vm = pl.BlockSpec(memory_space=pltpu.VMEM)
    return pl.pallas_call(
        body, name="adamw_small", out_shape=out_shape,
        in_specs=[vm] * (4 + len(flat)), out_specs=[vm] * len(out_shape),
        compiler_params=pltpu.CompilerParams(vmem_limit_bytes=VMEM_LIMIT),
    )(ga, gb, gn1, gl, *flat)


def kernel(x, ffn1_norm, ffn1_w_gate, ffn1_w_up, ffn1_w_down, mix_norm, w_in, b_in, sgu_norm_g, sgu_norm_b, sgu_w_s, sgu_b_s, ret_decay_logit, w_branch_a, w_branch_b, w_out, ffn2_norm, ffn2_w_gate, ffn2_w_up, ffn2_w_down, final_norm, loss_target, m_ffn1_norm, m_ffn1_w_gate, m_ffn1_w_up, m_ffn1_w_down, m_mix_norm, m_w_in, m_b_in, m_sgu_norm_g, m_sgu_norm_b, m_sgu_w_s, m_sgu_b_s, m_ret_decay_logit, m_w_branch_a, m_w_branch_b, m_w_out, m_ffn2_norm, m_ffn2_w_gate, m_ffn2_w_up, m_ffn2_w_down, m_final_norm, v_ffn1_norm, v_ffn1_w_gate, v_ffn1_w_up, v_ffn1_w_down, v_mix_norm, v_w_in, v_b_in, v_sgu_norm_g, v_sgu_norm_b, v_sgu_w_s, v_sgu_b_s, v_ret_decay_logit, v_w_branch_a, v_w_branch_b, v_w_out, v_ffn2_norm, v_ffn2_w_gate, v_ffn2_w_up, v_ffn2_w_down, v_final_norm):
    args = dict(locals())
    S = x.shape[1]
    xs = x[0]
    target = loss_target[0]

    def buf_layout(name, a):
        a = a[0]
        return a.T if name in W_TRANSPOSED else a

    sh = {n: buf_layout(n, args[n]).astype(CDT) for n in W_NAMES}
    wf = {}

    b3 = b_in.reshape(8, 1, D)
    ws = sgu_w_s[0].astype(CDT)
    wst = jnp.swapaxes(sgu_w_s[0], 1, 2).astype(CDT)
    bsc = sgu_b_s[0].reshape(G, C, 1)
    dl = jnp.zeros((H, 8, DK), f32).at[:, 0:2, :].set(jnp.broadcast_to(ret_decay_logit[0].T[:, :, None], (H, 2, DK)))
    theta = ROPE_BASE ** (-jnp.arange(0, DK, 2, dtype=f32) / DK)
    ang = jnp.arange(S, dtype=f32)[:, None] * theta[None, :]
    cos, sin = jnp.cos(ang), jnp.sin(ang)
    fnorm = final_norm.reshape(1, D)

    f1 = ("ffn1_w_gate", "ffn1_w_up", "ffn1_w_down")
    f2 = ("ffn2_w_gate", "ffn2_w_up", "ffn2_w_down")
    br = ("w_branch_a", "w_branch_b", "w_out")
    for cid, names in enumerate((f1, ("w_in",), br, f2)):
        wf.update(zip(names, _sequence("ag_" + names[0], 1 + cid, SIBLING_AND_CHIPS, [_ag_hook(sh[n]) for n in names])))
    x1, g1, u1, a1, hf1 = _ffn_fwd("ffn1_fwd", xs, ffn1_norm, *[wf[n] for n in f1])
    proj, h2 = _proj_fwd(x1, mix_norm, wf["w_in"], b3, cos, sin)
    a = _sgu_fwd(proj, sgu_norm_g, sgu_norm_b, ws, bsc)
    R, r, sfs, sbs = _ret_fwd(proj, dl)
    x2, ya, yb = _merge_fwd(a, r, proj, x1, *[wf[n] for n in br])
    x3, g2, u2, a2, hf2 = _ffn_fwd("ffn2_fwd", x2, ffn2_norm, *[wf[n] for n in f2])
    dx3, dyh2, d_final, loss_part = _loss_head(x3, fnorm, target)

    my_c = lax.axis_index("c").astype(jnp.int32).reshape(1)
    gw, landed = {}, {}

    def d2d(*names):
        return [_rs_d2d_hook(gw[n]) for n in names]

    def to_chips(names, sibs, more=()):
        parts = _rs_sum("rs_sum_" + names[0], [gw[n] for n in names], list(sibs), my_c)
        hooks = [_rs_ici_hook(p) for p in parts] + ([_small_hook(list(more))] if more else [])
        reach = EVERYONE if more else CHIPS
        got = _sequence("rs_chips_" + names[0], REACH_ID[reach], reach, hooks)
        landed.update(zip(names, got))
        return got[len(names):]

    def ffn_bwd(tag, names, dy, dyh, x, g, u, a, h, nrm, each_alone, more=()):
        wg, wu, wd = names
        (gw[wd],) = _wgrad(tag + "_wd_grad", a, dyh)
        dg, du, sib_d = _ffn_bwd_hidden(tag + "_bwd_hidden", dyh, g, u, wf[wd], d2d(wd))
        if each_alone:
            to_chips([wd], [sib_d])
        (gw[wg],) = _wgrad(tag + "_wg_grad", dg, h)
        gw[wu], sib_g = _wgrad(tag + "_wu_grad", du, h, d2d(wg))
        if each_alone:
            to_chips([wg], [sib_g])
        dx, dn, sib_u = _ffn_bwd_in(tag + "_bwd_in", dy, x, dg, du, nrm, wf[wg], wf[wu], d2d(wu))
        if each_alone:
            return dx, dn, to_chips([wu], [sib_u], more(dn))
        return dx, dn, to_chips([wd, wg, wu], [sib_d, sib_g, sib_u])

    dx2, d_ffn2n, _ = ffn_bwd("ffn2", f2, dx3, dyh2, x2, g2, u2, a2, hf2, ffn2_norm, False)
    dproj, da, dr, mix, dx2b, dya, dyb = _merge_bwd_act(dx2, ya, yb, proj, *[wf[n] for n in br])
    dproj, d_ws, d_bs, d_gn, d_bn = _sgu_bwd(da, proj, dproj, sgu_norm_g, sgu_norm_b, ws, wst, bsc)
    dproj, d_dec = _ret_bwd(dr, R, sfs, sbs, proj, dproj, cos, sin, dl)
    (gw["w_in"],) = _tn("win_grad", h2, dproj, _seg_of_slot)
    gw["w_out"], sib_win = _wgrad("wo_grad", mix, dx2b, d2d("w_in"))
    small_sgu = jnp.concatenate([d_ws.reshape(G * C, C), d_bs.reshape(G * 8, C)], axis=0)
    g_sgu, gl = to_chips(["w_in"], [sib_win], [small_sgu, loss_part])
    (gw["w_branch_a"],) = _wgrad("wa_grad", a, dya)
    (gw["w_branch_b"],) = _wgrad("wb_grad", r, dyb)
    dx1, dyh1, d_mixn, d_bin, *sib_br = _proj_bwd_act(dproj, dx2, x1, mix_norm, wf["w_in"], d2d(*br))
    small_a = jnp.concatenate([jnp.zeros((1, D), f32), d_mixn, d_gn, d_bn, d_ffn2n, d_final, jnp.zeros((2, D), f32),
                               d_bin.reshape(8, D)], axis=0)
    ga, g_dec = to_chips(list(br), sib_br, [small_a, d_dec.reshape(H * 8, 128)])
    gb = jnp.concatenate([g_sgu, g_dec], axis=1)
    dxs, d_ffn1n, (gn1,) = ffn_bwd("ffn1", f1, dx1, dyh1, xs, g1, u1, a1, hf1, ffn1_norm, True, lambda dn: [dn])

    out = {"grad_x": dxs[None]}

    def native(name, a):
        a = a.T if name in W_TRANSPOSED else a
        return a[None]

    after = dxs
    for n in f2 + ("w_in",) + br + (f1[2], f1[0], f1[1]):
        res = _adamw_big("adamw_" + n, landed[n], buf_layout(n, args[n]), buf_layout(n, args["m_" + n]),
                         buf_layout(n, args["v_" + n]), after)
        after = res[0]
        for pre, val in zip(("grad_", "delta_", "new_m_", "new_v_"), res):
            out[pre + n] = native(n, val)

    def pad_decay(a):
        return jnp.zeros((8, 128), f32).at[0:2, 0:H].set(a[0])

    small = [
        ("ffn1_norm", lambda a: a, lambda a: a), ("mix_norm", lambda a: a, lambda a: a),
        ("sgu_norm_g", lambda a: a, lambda a: a), ("sgu_norm_b", lambda a: a, lambda a: a),
        ("ffn2_norm", lambda a: a, lambda a: a),
        ("final_norm", lambda a: a.reshape(1, D), lambda a: a.reshape(D)),
        ("b_in", lambda a: a.reshape(8, D), lambda a: a.reshape(1, 8 * D)),
        ("sgu_w_s", lambda a: a.reshape(G * C, C), lambda a: a.reshape(1, G, C, C)),
        ("sgu_b_s", lambda a: a[0], lambda a: a[None]),
        ("ret_decay_logit", pad_decay, lambda a: a[None, 0:2, 0:H]),
    ]
    res = _adamw_small(ga, gb, gn1, gl, [(to(args[n]), to(args["m_" + n]), to(args["v_" + n])) for n, to, _ in small])
    out["loss"] = res[40][0, 0]
    for i, (n, _, back) in enumerate(small):
        for j, pre in enumerate(("grad_", "delta_", "new_m_", "new_v_")):
            out[pre + n] = back(res[4 * i + j])

    weights = ("ffn1_norm", "ffn1_w_gate", "ffn1_w_up", "ffn1_w_down", "mix_norm", "w_in", "b_in", "sgu_norm_g",
               "sgu_norm_b", "sgu_w_s", "sgu_b_s", "ret_decay_logit", "w_branch_a", "w_branch_b", "w_out", "ffn2_norm",
               "ffn2_w_gate", "ffn2_w_up", "ffn2_w_down", "final_norm")
    return (out["loss"], out["grad_x"], *[out["grad_" + n] for n in weights], *[out["delta_" + n] for n in weights],
            *[out["new_m_" + n] for n in weights], *[out["new_v_" + n] for n in weights])
```

```python
import functools
import math

import jax
import jax.numpy as jnp
from jax import lax
from jax.experimental import pallas as pl
from jax.experimental.pallas import tpu as pltpu
from jax.experimental.pallas import tpu_sc as plsc

f32 = jnp.float32
CDT = jnp.bfloat16

D = 1024
F = 2816
C = 128
RC = 256
H = 4
DK = 256
G = 4
NDEV = 8
NCHIP = 4
EPS = 1e-6
ROPE_BASE = 10000.0
FT = 256
V7X_VMEM_BYTES = 64 * 1024 * 1024
VMEM_LIMIT = V7X_VMEM_BYTES - 8 * 1024 * 1024

ADAM_LR, ADAM_B1, ADAM_B2, ADAM_EPS, ADAM_WD, ADAM_STEP = 0.001, 0.9, 0.999, 1e-08, 0.01, 10
BC1 = 1.0 - ADAM_B1 ** ADAM_STEP
BC2 = 1.0 - ADAM_B2 ** ADAM_STEP

W_ROWS = dict(ffn1_w_gate=352, ffn1_w_up=352, ffn1_w_down=352, w_in=1024, w_branch_a=128, w_branch_b=128, w_out=128,
              ffn2_w_gate=352, ffn2_w_up=352, ffn2_w_down=352)
W_NAMES = tuple(W_ROWS)
W_TRANSPOSED = ("ffn1_w_gate", "ffn1_w_up", "ffn2_w_gate", "ffn2_w_up")

SLOT_U, SLOT_V, SLOT_GA, SLOT_GB, SLOT_Q, SLOT_K, SLOT_VR, SLOT_GR = range(8)


SEG_OF_SLOT = (0, 1, 6, 7, 2, 3, 4, 5)


def _seg_of_slot(p):
    return jnp.where(p < 2, p, jnp.where(p < 4, p + 4, p - 2))


def _mm(a, b):
    return jnp.dot(a, b, preferred_element_type=f32)


def _mm_nt(a, b):
    return lax.dot_general(a, b, (((1,), (1,)), ((), ())), preferred_element_type=f32)


def _mm_tn(a, b):
    return lax.dot_general(a, b, (((0,), (0,)), ((), ())), preferred_element_type=f32)


def _params(*sem):
    return pltpu.CompilerParams(dimension_semantics=sem, vmem_limit_bytes=VMEM_LIMIT)


def _resident(shape, index_map):
    return pl.BlockSpec(shape, index_map, pipeline_mode=pl.Buffered(1))


def _gelu(x):
    return 0.5 * x * (1.0 + lax.erf(x * (1.0 / math.sqrt(2.0))))


def _gelu_grad(x):
    return 0.5 * (1.0 + lax.erf(x * (1.0 / math.sqrt(2.0)))) + x * jnp.exp(-0.5 * x * x) * (1.0 / math.sqrt(2.0 * math.pi))


def _rms_fwd(x, n):
    r = lax.rsqrt(jnp.mean(x * x, axis=-1, keepdims=True) + EPS)
    xh = x * r
    return r, xh, xh * n


def _rms_bwd(dh, r, xh, n):
    dxh = dh * n
    dx = r * (dxh - xh * jnp.mean(dxh * xh, axis=-1, keepdims=True))
    return dx, jnp.sum(dh * xh, axis=0, keepdims=True)


MESH_ID = pl.DeviceIdType.MESH
_HBM = pl.BlockSpec(memory_space=pltpu.HBM)


def _my_place():
    return lax.axis_index("x"), lax.axis_index("y"), lax.axis_index("c")


def _ici_peers(x, y, c):
    return [((1 - x, y, c), 2 * (1 - x) + y), ((x, 1 - y, c), 2 * x + 1 - y), ((1 - x, 1 - y, c), 2 * (1 - x) + 1 - y)]


class _Hook:
    def __init__(self, operands, out_shapes, n_remote, n_local, start, finish, relay=None):
        self.operands, self.out_shapes = list(operands), list(out_shapes)
        self.n_remote, self.n_local, self.start, self.finish = n_remote, n_local, start, finish
        self.relay = relay or (lambda *a: None)


def _call(body, hooks, operands, *, in_specs, out_specs, out_shape, grid=None, scratch_shapes=(), **kw):
    hooks = tuple(hooks)
    n_in, n_out, n_scr = len(in_specs), len(out_shape), len(scratch_shapes)
    h_ops = [a for h in hooks for a in h.operands]
    h_outs = [s for h in hooks for s in h.out_shapes]
    h_sems = [pltpu.SemaphoreType.DMA((n,)) for h in hooks for n in (h.n_remote, h.n_remote, max(h.n_local, 1))]

    def wrapped(*refs):
        ins, hin = refs[:n_in], refs[n_in:n_in + len(h_ops)]
        o0 = n_in + len(h_ops)
        outs, hout = refs[o0:o0 + n_out], refs[o0 + n_out:o0 + n_out + len(h_outs)]
        s0 = o0 + n_out + len(h_outs)
        scr, hsem = refs[s0:s0 + n_scr], refs[s0 + n_scr:]

        def run(phase):
            ip = op = 0
            for i, h in enumerate(hooks):
                ssem, rsem, lsem = hsem[3 * i:3 * i + 3]

                def rcopy(k, src, dst, dev, ssem=ssem, rsem=rsem):
                    return pltpu.make_async_remote_copy(src_ref=src, dst_ref=dst, send_sem=ssem.at[k], recv_sem=rsem.at[k],
                                                        device_id=dev, device_id_type=MESH_ID)

                def lcopy(k, src, dst, lsem=lsem):
                    return pltpu.make_async_copy(src, dst, lsem.at[k])

                getattr(h, phase)(hin[ip:ip + len(h.operands)], hout[op:op + len(h.out_shapes)], rcopy, lcopy)
                ip += len(h.operands)
                op += len(h.out_shapes)

        def at_edge(phase, last):
            if not hooks:
                return
            if grid is None:
                run(phase)
                return
            cond = None
            for ax, n in enumerate(grid):
                here = pl.program_id(ax) == (n - 1 if last else 0)
                cond = here if cond is None else cond & here
            pl.when(cond)(lambda: run(phase))

        at_edge("start", False)
        at_edge("relay", True)
        body(*ins, *outs, *scr)
        at_edge("finish", True)

    if grid is not None:
        kw["grid"] = grid
    return list(pl.pallas_call(
        wrapped, out_shape=list(out_shape) + h_outs, in_specs=list(in_specs) + [_HBM] * len(h_ops),
        out_specs=list(out_specs) + [_HBM] * len(h_outs), scratch_shapes=list(scratch_shapes) + h_sems, **kw,
    )(*operands, *h_ops))


def _exchange(name, hooks):
    return _call(lambda: None, hooks, [], name=name, in_specs=[], out_specs=[], out_shape=[])


def _rows(ref, start, n):
    return ref.at[pl.ds(start, n), :]


def _ag_hook(shard, early=True):
    rows = shard.shape[0]

    def block(full, dev_index):
        return _rows(full, dev_index * rows, rows)

    def start(ins, outs, rcopy, lcopy):
        x, y, c = _my_place()
        src, dst = ins[0], block(outs[0], 4 * x + 2 * y + c)
        lcopy(0, src, dst).start()
        rcopy(0, src, dst, (x, y, 1 - c)).start()
        for j, (dev, _) in enumerate(_ici_peers(x, y, c)):
            rcopy(1 + j, src, dst, dev).start()

    def relay(ins, outs, rcopy, lcopy):
        x, y, c = _my_place()
        for j, (dev, chip) in enumerate(_ici_peers(x, y, c)):
            blk = block(outs[0], 2 * chip + c)
            rcopy(1 + j, blk, blk, dev).wait_recv()
            rcopy(4 + j, blk, blk, (x, y, 1 - c)).start()

    def finish(ins, outs, rcopy, lcopy):
        if not early:
            relay(ins, outs, rcopy, lcopy)
        x, y, c = _my_place()
        sib = (x, y, 1 - c)
        full = outs[0]
        peers = _ici_peers(x, y, c)
        blk = block(full, 2 * (2 * x + y) + 1 - c)
        rcopy(0, blk, blk, sib).wait_recv()
        for j, (dev, chip) in enumerate(peers):
            blk = block(full, 2 * chip + 1 - c)
            rcopy(4 + j, blk, blk, sib).wait_recv()
        src, dst = ins[0], block(full, 4 * x + 2 * y + c)
        lcopy(0, src, dst).wait()
        rcopy(0, src, dst, sib).wait_send()
        for j, (dev, chip) in enumerate(peers):
            rcopy(1 + j, src, dst, dev).wait_send()
            blk = block(full, 2 * chip + c)
            rcopy(4 + j, blk, blk, sib).wait_send()

    return _Hook([shard], [jax.ShapeDtypeStruct((NDEV * rows, D), shard.dtype)], 7, 1, start, finish, relay if early else None)


SIBLING, CHIPS, SIBLING_AND_CHIPS, EVERYONE = "sibling", "chips", "sibling and chips", "everyone"
REACH_ID = {SIBLING: 5, CHIPS: 6, EVERYONE: 7}


def _sequence(name, collective_id, reach, hooks):
    ins = [[jax.new_ref(a, memory_space=pltpu.MemorySpace.HBM) for a in h.operands] for h in hooks]
    outs = [[jax.empty_ref(s, memory_space=pltpu.MemorySpace.HBM) for s in h.out_shapes] for h in hooks]
    sems = tuple(pltpu.SemaphoreType.DMA((n,)) for h in hooks for n in (h.n_remote, h.n_remote, max(h.n_local, 1)))

    @pl.kernel(mesh=plsc.ScalarSubcoreMesh(axis_name="sequencer", num_cores=1), name=name, scratch_types=sems,
               compiler_params=pltpu.CompilerParams(collective_id=collective_id))
    def launch(*sem_refs):
        x, y, c = _my_place()
        chips = [dev for dev, _ in _ici_peers(x, y, c)]
        others = [(1 - x if dx else x, 1 - y if dy else y, 1 - c if dc else c)
                  for dx in range(2) for dy in range(2) for dc in range(2) if dx + dy + dc]
        devs = {SIBLING: [(x, y, 1 - c)], CHIPS: chips, SIBLING_AND_CHIPS: [(x, y, 1 - c)] + chips, EVERYONE: others}[reach]
        barrier = pltpu.get_barrier_semaphore()
        for dev in devs:
            pl.semaphore_signal(barrier, inc=1, device_id=dev, device_id_type=MESH_ID)
        pl.semaphore_wait(barrier, len(devs))
        for phase in ("start", "relay", "finish"):
            for i, h in enumerate(hooks):
                ssem, rsem, lsem = sem_refs[3 * i:3 * i + 3]

                def rcopy(k, src, dst, dev, ssem=ssem, rsem=rsem):
                    return pltpu.make_async_remote_copy(src_ref=src, dst_ref=dst, send_sem=ssem.at[k], recv_sem=rsem.at[k],
                                                        device_id=dev, device_id_type=MESH_ID)

                def lcopy(k, src, dst, lsem=lsem):
                    return pltpu.make_async_copy(src, dst, lsem.at[k])

                getattr(h, phase)(ins[i], outs[i], rcopy, lcopy)

    launch()
    return [o[...] for os in outs for o in os]


def _rs_d2d_hook(gfull):
    rows = gfull.shape[0] // NDEV

    def pairs(g, land):
        x, y, c = _my_place()
        return (x, y, 1 - c), [(k, _rows(g, (2 * k + 1 - c) * rows, rows), land.at[k]) for k in range(NCHIP)]

    def start(ins, outs, rcopy, lcopy):
        sib, cps = pairs(ins[0], outs[0])
        for i, src, dst in cps:
            rcopy(i, src, dst, sib).start()

    def finish(ins, outs, rcopy, lcopy):
        sib, cps = pairs(ins[0], outs[0])
        for i, src, dst in cps:
            rcopy(i, dst, dst, sib).wait_recv()
        for i, src, dst in cps:
            rcopy(i, src, dst, sib).wait_send()

    return _Hook([gfull], [jax.ShapeDtypeStruct((NCHIP, rows, D), gfull.dtype)], NCHIP, 0, start, finish)


def _rs_ici_hook(part):
    def start(ins, outs, rcopy, lcopy):
        x, y, c = _my_place()
        mychip = 2 * x + y
        lcopy(0, ins[0].at[mychip], outs[0].at[mychip]).start()
        for j, (dev, chip) in enumerate(_ici_peers(x, y, c)):
            rcopy(j, ins[0].at[chip], outs[0].at[mychip], dev).start()

    def finish(ins, outs, rcopy, lcopy):
        x, y, c = _my_place()
        mychip = 2 * x + y
        peers = _ici_peers(x, y, c)
        for j, (dev, chip) in enumerate(peers):
            rcopy(j, outs[0].at[chip], outs[0].at[chip], dev).wait_recv()
        for j, (dev, chip) in enumerate(peers):
            rcopy(j, ins[0].at[chip], outs[0].at[mychip], dev).wait_send()
        lcopy(0, ins[0].at[mychip], outs[0].at[mychip]).wait()

    return _Hook([part], [jax.ShapeDtypeStruct(part.shape, part.dtype)], 3, 1, start, finish)


def _small_hook(arrays):
    n = len(arrays)

    def peers():
        x, y, c = _my_place()
        out = []
        for dx in range(2):
            for dy in range(2):
                for dc in range(2):
                    if dx + dy + dc:
                        px, py, pc = (1 - x if dx else x), (1 - y if dy else y), (1 - c if dc else c)
                        out.append(((px, py, pc), 4 * px + 2 * py + pc))
        return 4 * x + 2 * y + c, out

    def start(ins, outs, rcopy, lcopy):
        me, ps = peers()
        for t in range(n):
            lcopy(t, ins[t], outs[t].at[me]).start()
            for i, (dev, _) in enumerate(ps):
                rcopy(n * i + t, ins[t], outs[t].at[me], dev).start()

    def finish(ins, outs, rcopy, lcopy):
        me, ps = peers()
        for t in range(n):
            for i, (dev, peer) in enumerate(ps):
                rcopy(n * i + t, outs[t].at[peer], outs[t].at[peer], dev).wait_recv()
            for i, (dev, _) in enumerate(ps):
                rcopy(n * i + t, ins[t], outs[t].at[me], dev).wait_send()
            lcopy(t, ins[t], outs[t].at[me]).wait()

    return _Hook(arrays, [jax.ShapeDtypeStruct((NDEV,) + a.shape, a.dtype) for a in arrays], 7 * n, n, start, finish)


def _wblock(w):
    return _resident(w.shape, lambda *_: (0, 0))


def _ffn_fwd(name, x, nrm, wg, wu, wd, hooks=(), tm=512):
    S = x.shape[0]

    def body(x_ref, n_ref, wg_ref, wu_ref, wd_ref, y_ref, g_ref, u_ref, a_ref, h_ref, acc_ref):
        xv = x_ref[...]
        _, _, h = _rms_fwd(xv, n_ref[...])
        h = h.astype(CDT)
        h_ref[...] = h
        for ci in range(F // FT):
            sl = slice(ci * FT, (ci + 1) * FT)
            g = _mm_nt(h, wg_ref[sl, :])
            u = _mm_nt(h, wu_ref[sl, :])
            g_ref[:, sl] = g.astype(CDT)
            u_ref[:, sl] = u.astype(CDT)
            a = (g * jax.nn.sigmoid(g) * u).astype(CDT)
            a_ref[:, sl] = a
            o = _mm(a, wd_ref[sl, :])
            if ci == 0:
                acc_ref[...] = o
            else:
                acc_ref[...] += o
        y_ref[...] = xv + 0.5 * acc_ref[...]

    tok = pl.BlockSpec((tm, D), lambda i: (i, 0))
    hid = pl.BlockSpec((tm, F), lambda i: (i, 0))
    hidden = jax.ShapeDtypeStruct((S, F), CDT)
    return _call(
        body, hooks, [x, nrm, wg, wu, wd], name=name, grid=(S // tm,),
        out_shape=[jax.ShapeDtypeStruct((S, D), f32), hidden, hidden, hidden, jax.ShapeDtypeStruct((S, D), CDT)],
        in_specs=[tok, _resident((1, D), lambda i: (0, 0)), _wblock(wg), _wblock(wu), _wblock(wd)],
        out_specs=[tok, hid, hid, hid, tok],
        scratch_shapes=[pltpu.VMEM((tm, D), f32)],
        compiler_params=_params("arbitrary"),
    )


def _proj_fwd(x1, nrm, wfull, b3, cos, sin, hooks=(), tm=512):
    S = x1.shape[0]

    def body(x_ref, n_ref, w_ref, b_ref, cos_ref, sin_ref, p_ref, h_ref):
        _, _, h = _rms_fwd(x_ref[...], n_ref[...])
        h = h.astype(CDT)
        h_ref[...] = h
        for p in range(8):
            seg = SEG_OF_SLOT[p]
            z = _mm(h, w_ref[seg * D:(seg + 1) * D, :]) + b_ref[seg]
            if p in (SLOT_Q, SLOT_K):
                co, si = cos_ref[...], sin_ref[...]
                for hh in range(H):
                    cs = slice(hh * DK, (hh + 1) * DK)
                    zr = _rotate(z[:, cs], co, si)
                    p_ref[p, :, cs] = (zr * K_SCALE if p == SLOT_K else zr).astype(CDT)
            else:
                p_ref[p] = z.astype(CDT)

    tab = pl.BlockSpec((tm, DK // 2), lambda i: (i, 0))
    return _call(
        body, hooks, [x1, nrm, wfull, b3, cos, sin], name="proj_fwd", grid=(S // tm,),
        out_shape=[jax.ShapeDtypeStruct((8, S, D), CDT), jax.ShapeDtypeStruct((S, D), CDT)],
        in_specs=[pl.BlockSpec((tm, D), lambda i: (i, 0)), _resident((1, D), lambda i: (0, 0)),
                  _resident((8 * D, D), lambda i: (0, 0)), _resident((8, 1, D), lambda i: (0, 0, 0)), tab, tab],
        out_specs=[pl.BlockSpec((8, tm, D), lambda i: (0, i, 0)), pl.BlockSpec((tm, D), lambda i: (i, 0))],
        compiler_params=_params("arbitrary"),
    )


def _sgu_norm(va, gn, bn):
    mu = jnp.mean(va, axis=-1, keepdims=True)
    xc = va - mu
    rstd = lax.rsqrt(jnp.mean(xc * xc, axis=-1, keepdims=True) + EPS)
    vhat = xc * rstd
    return rstd, vhat, vhat * gn + bn


def _sgu_fwd(proj, gn, bn, ws, bsc, tm=512):
    S = proj.shape[1]
    GW = D // G

    def body(p_ref, gn_ref, bn_ref, ws_ref, bs_ref, a_ref):
        ua = _gelu(p_ref[0].astype(f32))
        va = _gelu(p_ref[1].astype(f32))
        _, _, vn = _sgu_norm(va, gn_ref[...], bn_ref[...])
        vn = vn.astype(CDT)
        for ch in range(tm // C):
            rs = slice(ch * C, (ch + 1) * C)
            for gi in range(G):
                cs = slice(gi * GW, (gi + 1) * GW)
                s = _mm(ws_ref[gi], vn[rs, cs]) + bs_ref[gi]
                a_ref[rs, cs] = (ua[rs, cs] * s).astype(CDT)

    return pl.pallas_call(
        body, name="sgu_fwd", grid=(S // tm,),
        out_shape=jax.ShapeDtypeStruct((S, D), CDT),
        in_specs=[pl.BlockSpec((2, tm, D), lambda i: (0, i, 0)), _resident((1, D), lambda i: (0, 0)),
                  _resident((1, D), lambda i: (0, 0)), _resident((G, C, C), lambda i: (0, 0, 0)),
                  _resident((G, C, 1), lambda i: (0, 0, 0))],
        out_specs=pl.BlockSpec((tm, D), lambda i: (i, 0)),
        compiler_params=_params("arbitrary"),
    )(proj, gn, bn, ws, bsc)


def _decay_tables(dl_ref):
    lg = jax.nn.log_sigmoid(dl_ref[0:2, :])
    lgf, lgb = lg[0:1, :], lg[1:2, :]
    assert RC <= DK
    ri = lax.broadcasted_iota(jnp.int32, (RC, RC), 0)
    ci = lax.broadcasted_iota(jnp.int32, (RC, RC), 1)
    d = (ri - ci).astype(f32)
    lower = d >= 0
    dmat = jnp.where(lower, jnp.exp(d * lgf[:, :RC]), jnp.exp(-d * lgb[:, :RC]))
    dmat_t = jnp.where(d <= 0, jnp.exp(-d * lgf[:, :RC]), jnp.exp(d * lgb[:, :RC]))
    pos = lax.broadcasted_iota(jnp.int32, (RC, DK), 0).astype(f32)
    t = dict(
        lgf=lgf, lgb=lgb, d=d, lower=lower, dmat=dmat, dmat_t=dmat_t, pos=pos,
        fq=jnp.exp((pos + 1.0) * lgf), fk=jnp.exp((RC - 1.0 - pos) * lgf),
        bq=jnp.exp((RC - pos) * lgb), bk=jnp.exp(pos * lgb),
        lamf=jnp.exp(float(RC) * lgf), lamb=jnp.exp(float(RC) * lgb),
    )
    return t


def _rotate(t, co, si):
    t1, t2 = t[:, :DK // 2], t[:, DK // 2:]
    return jnp.concatenate([t1 * co - t2 * si, t2 * co + t1 * si], axis=-1)


def _unrotate(t, co, si):
    t1, t2 = t[:, :DK // 2], t[:, DK // 2:]
    return jnp.concatenate([t1 * co + t2 * si, t2 * co - t1 * si], axis=-1)


K_SCALE = DK ** -0.5
ROW_TILE = 256


def _ret_fwd(proj, dl, hooks=()):
    S = proj.shape[1]
    NC = S // RC

    def body(q_ref, k_ref, v_ref, g_ref, dl_ref, R_ref, r_ref, sfs_ref, sbs_ref, rb_ref, sf_ref, sb_ref):
        t = _decay_tables(dl_ref)

        def chunk(n):
            rows = pl.ds(pl.multiple_of(n * RC, RC), RC)
            return rows, q_ref[rows, :], k_ref[rows, :], v_ref[rows, :]

        sf_ref[...] = jnp.zeros_like(sf_ref)
        sb_ref[...] = jnp.zeros_like(sb_ref)

        def step(i, carry):
            rows, qn, kn, vn = chunk(i)
            sc = _mm_nt(qn, kn) * t["dmat"]
            out = _mm(sc.astype(CDT), vn)
            sf = sf_ref[...]
            sfb = sf.astype(CDT)
            sfs_ref[i] = sfb
            R_ref[rows, :] = out + _mm((qn.astype(f32) * t["fq"]).astype(CDT), sfb)
            sf_ref[...] = sf * t["lamf"] + _mm_tn((kn.astype(f32) * t["fk"]).astype(CDT), vn)
            m = NC - 1 - i
            rows, qn, kn, vn = chunk(m)
            sb = sb_ref[...]
            sbb = sb.astype(CDT)
            sbs_ref[m] = sbb
            rb_ref[rows, :] = _mm((qn.astype(f32) * t["bq"]).astype(CDT), sbb)
            sb_ref[...] = sb * t["lamb"] + _mm_tn((kn.astype(f32) * t["bk"]).astype(CDT), vn)
            return carry

        lax.fori_loop(0, NC, step, 0)

        def finish(i, carry):
            rs = pl.ds(pl.multiple_of(i * ROW_TILE, ROW_TILE), ROW_TILE)
            R = R_ref[rs, :] + rb_ref[rs, :]
            R_ref[rs, :] = R
            rn = R * lax.rsqrt(jnp.mean(R * R, axis=-1, keepdims=True) + EPS)
            g = g_ref[rs, :].astype(f32)
            r_ref[rs, :] = (rn * g * jax.nn.sigmoid(g)).astype(CDT)
            return carry

        lax.fori_loop(0, S // ROW_TILE, finish, 0)

    def seg(slot):
        return pl.BlockSpec((None, S, DK), lambda h: (slot, 0, h))

    states = jax.ShapeDtypeStruct((H, NC, DK, DK), CDT)
    state_blk = pl.BlockSpec((None, NC, DK, DK), lambda h: (h, 0, 0, 0))
    return _call(
        body, hooks, [proj, proj, proj, proj, dl], name="ret_fwd", grid=(H,),
        out_shape=[jax.ShapeDtypeStruct((S, H * DK), f32), jax.ShapeDtypeStruct((S, H * DK), CDT), states, states],
        in_specs=[seg(SLOT_Q), seg(SLOT_K), seg(SLOT_VR), seg(SLOT_GR), pl.BlockSpec((None, 8, DK), lambda h: (h, 0, 0))],
        out_specs=[pl.BlockSpec((S, DK), lambda h: (0, h)), pl.BlockSpec((S, DK), lambda h: (0, h)), state_blk, state_blk],
        scratch_shapes=[pltpu.VMEM((S, DK), f32), pltpu.VMEM((DK, DK), f32), pltpu.VMEM((DK, DK), f32)],
        compiler_params=_params("arbitrary"),
    )


def _merge_fwd(a, r, proj, x1, wa, wb, wo, hooks=(), tm=512):
    S = x1.shape[0]

    def body(a_ref, r_ref, gt_ref, x_ref, wa_ref, wb_ref, wo_ref, x2_ref, ya_ref, yb_ref):
        ya = _mm(a_ref[...], wa_ref[...])
        yb = _mm(r_ref[...], wb_ref[...])
        ya_ref[...] = ya.astype(CDT)
        yb_ref[...] = yb.astype(CDT)
        mix = jax.nn.sigmoid(gt_ref[0].astype(f32)) * ya + jax.nn.sigmoid(gt_ref[1].astype(f32)) * yb
        x2_ref[...] = x_ref[...] + _mm(mix.astype(CDT), wo_ref[...])

    tok = pl.BlockSpec((tm, D), lambda i: (i, 0))
    return _call(
        body, hooks, [a, r, proj, x1, wa, wb, wo], name="merge_fwd", grid=(S // tm,),
        out_shape=[jax.ShapeDtypeStruct((S, D), f32), jax.ShapeDtypeStruct((S, D), CDT), jax.ShapeDtypeStruct((S, D), CDT)],
        in_specs=[tok, tok, pl.BlockSpec((2, tm, D), lambda i: (SLOT_GA // 2, i, 0)), tok,
                  _wblock(wa), _wblock(wb), _wblock(wo)],
        out_specs=[tok, tok, tok],
        compiler_params=_params("arbitrary"),
    )


def _loss_head(x3, fn, target, tm=512):
    S = x3.shape[0]

    def body(x_ref, n_ref, t_ref, dx_ref, dxh_ref, dn_ref, l_ref):
        n = n_ref[...]
        r, xh, y = _rms_fwd(x_ref[...], n)
        e = y - t_ref[...]
        dy = e * (1.0 / D)
        dx, dn = _rms_bwd(dy, r, xh, n)
        dx_ref[...] = dx
        dxh_ref[...] = (0.5 * dx).astype(CDT)
        part = 0.5 * jnp.sum(jnp.sum(e * e, axis=-1, keepdims=True), axis=0, keepdims=True) * (1.0 / D)

        @pl.when(pl.program_id(0) == 0)
        def _():
            dn_ref[...] = jnp.zeros_like(dn_ref)
            l_ref[...] = jnp.zeros_like(l_ref)

        dn_ref[...] += dn
        l_ref[...] += jnp.broadcast_to(part, l_ref.shape)

    tok = pl.BlockSpec((tm, D), lambda i: (i, 0))
    return pl.pallas_call(
        body, name="loss_head", grid=(S // tm,),
        out_shape=[jax.ShapeDtypeStruct((S, D), f32), jax.ShapeDtypeStruct((S, D), CDT), jax.ShapeDtypeStruct((1, D), f32),
                   jax.ShapeDtypeStruct((8, 128), f32)],
        in_specs=[tok, _resident((1, D), lambda i: (0, 0)), tok],
        out_specs=[tok, tok, pl.BlockSpec((1, D), lambda i: (0, 0)), pl.BlockSpec((8, 128), lambda i: (0, 0))],
        compiler_params=_params("arbitrary"),
    )(x3, fn, target)


def _ffn_bwd_hidden(name, dyh, g, u, wd, hooks=(), tm=512):
    S = dyh.shape[0]

    def body(dyh_ref, g_ref, u_ref, wd_ref, dg_ref, du_ref):
        dyh = dyh_ref[...]
        for ci in range(F // FT):
            sl = slice(ci * FT, (ci + 1) * FT)
            da = _mm_nt(dyh, wd_ref[sl, :])
            gv = g_ref[:, sl].astype(f32)
            uv = u_ref[:, sl].astype(f32)
            s = jax.nn.sigmoid(gv)
            du_ref[:, sl] = (da * (gv * s)).astype(CDT)
            dg_ref[:, sl] = (da * uv * (s * (1.0 + gv * (1.0 - s)))).astype(CDT)

    hid = pl.BlockSpec((tm, F), lambda i: (i, 0))
    hidden = jax.ShapeDtypeStruct((S, F), CDT)
    return _call(
        body, hooks, [dyh, g, u, wd], name=name, grid=(S // tm,), out_shape=[hidden, hidden],
        in_specs=[pl.BlockSpec((tm, D), lambda i: (i, 0)), hid, hid, _wblock(wd)], out_specs=[hid, hid],
        compiler_params=_params("arbitrary"),
    )


def _ffn_bwd_in(name, dy, x, dg, du, nrm, wg, wu, hooks=(), part=None, prev=None, tm=512):
    S = x.shape[0]
    t0, nt = part or (0, S // tm)

    def body(dy_ref, x_ref, dg_ref, du_ref, n_ref, wg_ref, wu_ref, *rest):
        dx_ref, dn_ref, acc_ref = rest[-3:]
        n = n_ref[...]
        r, xh, _ = _rms_fwd(x_ref[...], n)
        for ci in range(F // FT):
            sl = slice(ci * FT, (ci + 1) * FT)
            dh = _mm(dg_ref[:, sl], wg_ref[sl, :]) + _mm(du_ref[:, sl], wu_ref[sl, :])
            if ci == 0:
                acc_ref[...] = dh
            else:
                acc_ref[...] += dh
        dx, dn = _rms_bwd(acc_ref[...], r, xh, n)
        dx_ref[...] = dy_ref[...] + dx

        @pl.when(pl.program_id(0) == 0)
        def _():
            dn_ref[...] = jnp.zeros_like(dn_ref) if prev is None else rest[1][...]

        dn_ref[...] += dn

    tok = pl.BlockSpec((tm, D), lambda i: (t0 + i, 0))
    hid = pl.BlockSpec((tm, F), lambda i: (t0 + i, 0))
    row = pl.BlockSpec((1, D), lambda i: (0, 0))
    in_specs = [tok, tok, hid, hid, _resident((1, D), lambda i: (0, 0)), _wblock(wg), _wblock(wu)]
    operands = [dy, x, dg, du, nrm, wg, wu]
    aliases = {}
    if prev is not None:
        in_specs += [_HBM, row]
        operands += list(prev)
        aliases = {7: 0}
    return _call(
        body, hooks, operands, name=name, grid=(nt,),
        out_shape=[jax.ShapeDtypeStruct((S, D), f32), jax.ShapeDtypeStruct((1, D), f32)],
        in_specs=in_specs, out_specs=[tok, row],
        scratch_shapes=[pltpu.VMEM((tm, D), f32)],
        input_output_aliases=aliases,
        compiler_params=_params("arbitrary"),
    )


TN_ROWS = 512


def _tn(name, xs, ys, block_of, hooks=()):
    S, M = xs.shape
    B = ys.shape[0]
    tr = TN_ROWS if M % TN_ROWS == 0 else M // 2
    assert M % tr == 0 and tr % 128 == 0
    nt = M // tr

    def body(x_ref, y_ref, o_ref):
        o_ref[...] = _mm_tn(x_ref[...], y_ref[...]).astype(CDT)

    return _call(
        body, hooks, [xs, ys], name=name, grid=(B, nt),
        out_shape=[jax.ShapeDtypeStruct((B * M, D), CDT)],
        in_specs=[pl.BlockSpec((S, tr), lambda b, i: (0, i)), pl.BlockSpec((None, S, D), lambda b, i: (b, 0, 0))],
        out_specs=[pl.BlockSpec((tr, D), lambda b, i: (block_of(b) * nt + i, 0))],
        compiler_params=_params("arbitrary", "arbitrary"),
    )


def _wgrad(name, xs, y, hooks=()):
    return _tn(name, xs, y[None], lambda b: 0, hooks)


def _merge_bwd_act(dx2, ya, yb, proj, wa, wb, wo, hooks=(), tm=512):
    S = dx2.shape[0]

    def body(dx_ref, ya_ref, yb_ref, gt_ref, wa_ref, wb_ref, wo_ref,
             dp_ref, da_ref, dr_ref, mix_ref, dxb_ref, dya_ref, dyb_ref):
        dxb = dx_ref[...].astype(CDT)
        dxb_ref[...] = dxb
        dmix = _mm_nt(dxb, wo_ref[...])
        ya = ya_ref[...].astype(f32)
        yb = yb_ref[...].astype(f32)
        sa = jax.nn.sigmoid(gt_ref[0].astype(f32))
        sb = jax.nn.sigmoid(gt_ref[1].astype(f32))
        mix_ref[...] = (sa * ya + sb * yb).astype(CDT)
        dya = (dmix * sa).astype(CDT)
        dyb = (dmix * sb).astype(CDT)
        dya_ref[...] = dya
        dyb_ref[...] = dyb
        dp_ref[0] = (dmix * ya * sa * (1.0 - sa)).astype(CDT)
        dp_ref[1] = (dmix * yb * sb * (1.0 - sb)).astype(CDT)
        da_ref[...] = _mm_nt(dya, wa_ref[...]).astype(CDT)
        dr_ref[...] = _mm_nt(dyb, wb_ref[...]).astype(CDT)

    tok = pl.BlockSpec((tm, D), lambda i: (i, 0))
    gates = pl.BlockSpec((2, tm, D), lambda i: (SLOT_GA // 2, i, 0))
    act = jax.ShapeDtypeStruct((S, D), CDT)
    return _call(
        body, hooks, [dx2, ya, yb, proj, wa, wb, wo], name="merge_bwd_act", grid=(S // tm,),
        out_shape=[jax.ShapeDtypeStruct((8, S, D), CDT), act, act, act, act, act, act],
        in_specs=[tok, tok, tok, gates, _wblock(wa), _wblock(wb), _wblock(wo)],
        out_specs=[gates, tok, tok, tok, tok, tok, tok],
        compiler_params=_params("arbitrary"),
    )


def _sgu_bwd(da, proj, dproj, gn, bn, ws, wst, bsc, hooks=(), tm=512):
    S = proj.shape[1]
    GW = D // G

    def body(da_ref, p_ref, dpin_ref, gn_ref, bn_ref, ws_ref, wst_ref, bs_ref,
             dp_ref, dws_ref, dbs_ref, dgn_ref, dbn_ref, ds_ref, dvn_ref):
        @pl.when(pl.program_id(0) == 0)
        def _():
            dws_ref[...] = jnp.zeros_like(dws_ref)
            dbs_ref[...] = jnp.zeros_like(dbs_ref)
            dgn_ref[...] = jnp.zeros_like(dgn_ref)
            dbn_ref[...] = jnp.zeros_like(dbn_ref)

        pu = p_ref[0].astype(f32)
        pv = p_ref[1].astype(f32)
        ua = _gelu(pu)
        va = _gelu(pv)
        gn = gn_ref[...]
        rstd, vhat, vn = _sgu_norm(va, gn, bn_ref[...])
        vnb = vn.astype(CDT)
        dav = da_ref[...].astype(f32)
        dsb = (dav * ua).astype(CDT)
        ones = jnp.ones((8, GW), CDT)
        for ch in range(tm // C):
            rs = slice(ch * C, (ch + 1) * C)
            for gi in range(G):
                cs = slice(gi * GW, (gi + 1) * GW)
                s = _mm(ws_ref[gi], vnb[rs, cs]) + bs_ref[gi]
                ds_ref[rs, cs] = s
                dsg = dsb[rs, cs]
                dws_ref[gi] += _mm_nt(dsg, vnb[rs, cs])
                dbs_ref[gi] += _mm_nt(ones, dsg)
                dvn_ref[rs, cs] = _mm(wst_ref[gi], dsg)
        dp_ref[0] = (dav * ds_ref[...] * _gelu_grad(pu)).astype(CDT)
        dvn = dvn_ref[...]
        dgn_ref[...] += jnp.sum(dvn * vhat, axis=0, keepdims=True)
        dbn_ref[...] += jnp.sum(dvn, axis=0, keepdims=True)
        dvh = dvn * gn
        dva = rstd * (dvh - jnp.mean(dvh, axis=-1, keepdims=True) - vhat * jnp.mean(dvh * vhat, axis=-1, keepdims=True))
        dp_ref[1] = (dva * _gelu_grad(pv)).astype(CDT)

    uv = pl.BlockSpec((2, tm, D), lambda i: (0, i, 0))
    row = _resident((1, D), lambda i: (0, 0))
    return _call(
        body, hooks, [da, proj, dproj, gn, bn, ws, wst, bsc], name="sgu_bwd", grid=(S // tm,),
        out_shape=[jax.ShapeDtypeStruct(dproj.shape, CDT), jax.ShapeDtypeStruct((G, C, C), f32),
                   jax.ShapeDtypeStruct((G, 8, C), f32), jax.ShapeDtypeStruct((1, D), f32), jax.ShapeDtypeStruct((1, D), f32)],
        in_specs=[pl.BlockSpec((tm, D), lambda i: (i, 0)), uv, _HBM, row, row,
                  _resident((G, C, C), lambda i: (0, 0, 0)), _resident((G, C, C), lambda i: (0, 0, 0)),
                  _resident((G, C, 1), lambda i: (0, 0, 0))],
        out_specs=[uv, pl.BlockSpec((G, C, C), lambda i: (0, 0, 0)), pl.BlockSpec((G, 8, C), lambda i: (0, 0, 0)),
                   pl.BlockSpec((1, D), lambda i: (0, 0)), pl.BlockSpec((1, D), lambda i: (0, 0))],
        scratch_shapes=[pltpu.VMEM((tm, D), f32), pltpu.VMEM((tm, D), f32)],
        input_output_aliases={2: 0},
        compiler_params=_params("arbitrary"),
    )


def _ret_bwd(dr, R, sfs, sbs, proj, dproj, cos, sin, dl, hooks=()):
    S = proj.shape[1]
    NC = S // RC
    assert NC % 2 == 0

    def body(dr_ref, R_ref, sf_ref, sb_ref, q_ref, k_ref, v_ref, g_ref, dpin_ref, cos_ref, sin_ref, dl_ref,
             dp_ref, dd_ref, dR_ref, gb_ref, gf_ref, acc_ref):
        t = _decay_tables(dl_ref)

        def gate_norm_bwd(i, carry):
            rs = pl.ds(pl.multiple_of(i * ROW_TILE, ROW_TILE), ROW_TILE)
            Rv = R_ref[rs, :]
            rstd = lax.rsqrt(jnp.mean(Rv * Rv, axis=-1, keepdims=True) + EPS)
            rn = Rv * rstd
            gv = g_ref[rs, :].astype(f32)
            s = jax.nn.sigmoid(gv)
            drv = dr_ref[rs, :].astype(f32)
            dp_ref[3, rs, :] = (drv * rn * (s * (1.0 + gv * (1.0 - s)))).astype(CDT)
            drn = drv * gv * s
            dR_ref[rs, :] = (rstd * (drn - rn * jnp.mean(drn * rn, axis=-1, keepdims=True))).astype(CDT)
            return carry

        lax.fori_loop(0, S // ROW_TILE, gate_norm_bwd, 0)

        def chunk(n):
            rows = pl.ds(pl.multiple_of(n * RC, RC), RC)
            return rows, q_ref[rows, :], k_ref[rows, :], v_ref[rows, :], dR_ref[rows, :]

        def emit_kv(rows, dk, dv, final):
            if not final:
                dp_ref[1, rows, :] = dk.astype(CDT)
                dp_ref[2, rows, :] = dv.astype(CDT)
            else:
                co, si = cos_ref[rows, :], sin_ref[rows, :]
                dk = dp_ref[1, rows, :].astype(f32) + dk
                dp_ref[1, rows, :] = (_unrotate(dk, co, si) * K_SCALE).astype(CDT)
                dp_ref[2, rows, :] = (dp_ref[2, rows, :].astype(f32) + dv).astype(CDT)

        gb_ref[...] = jnp.zeros_like(gb_ref)
        gf_ref[...] = jnp.zeros_like(gf_ref)
        acc_ref[...] = jnp.zeros_like(acc_ref)
        dpos = jnp.abs(t["d"])

        def ascend(n, final):
            rows, qn, kn, vn, dRn = chunk(n)
            qf, kf = qn.astype(f32), kn.astype(f32)
            sc = _mm_nt(qn, kn)
            dA = _mm_nt(dRn, vn)
            w = sc * dA * t["dmat"] * dpos
            lgf_part = jnp.sum(jnp.where(t["lower"], w, 0.0), axis=0, keepdims=True)
            lgb_part = jnp.sum(jnp.where(t["lower"], 0.0, w), axis=0, keepdims=True)
            dsc = (dA * t["dmat"]).astype(CDT)
            dq = _mm(dsc, kn)
            scT = (_mm_nt(kn, qn) * t["dmat_t"]).astype(CDT)
            dscT = (_mm_nt(vn, dRn) * t["dmat_t"]).astype(CDT)
            dk = _mm(dscT, qn)
            dv = _mm(scT, dRn)
            sfb = sf_ref[n]
            sbb = sb_ref[n]
            qdf = qf * t["fq"]
            dqdf = _mm_nt(dRn, sfb)
            dq += dqdf * t["fq"]
            lgf_row = jnp.sum(qdf * dqdf * (t["pos"] + 1.0), axis=0, keepdims=True)
            qdb = qf * t["bq"]
            dqdb = _mm_nt(dRn, sbb)
            dq += dqdb * t["bq"]
            lgb_row = jnp.sum(qdb * dqdb * (RC - t["pos"]), axis=0, keepdims=True)
            gb = gb_ref[...]
            gbb = gb.astype(CDT)
            kdb = kf * t["bk"]
            dkdb = _mm_nt(vn, gbb)
            dk += dkdb * t["bk"]
            dv += _mm(kdb.astype(CDT), gbb)
            lgb_row += jnp.sum(kdb * dkdb * t["pos"], axis=0, keepdims=True)
            lgb_row += float(RC) * t["lamb"] * jnp.sum(gb * sbb.astype(f32), axis=0, keepdims=True)
            co, si = cos_ref[rows, :], sin_ref[rows, :]
            dp_ref[0, rows, :] = _unrotate(dq, co, si).astype(CDT)
            emit_kv(rows, dk, dv, final)
            acc_ref[0:1, :] += lgf_row + lgf_part
            acc_ref[1:2, :] += lgb_row + lgb_part
            gb_ref[...] = gb * t["lamb"] + _mm_tn(qdb.astype(CDT), dRn)

        def descend(n, final):
            rows, qn, kn, vn, dRn = chunk(n)
            gf = gf_ref[...]
            gfb = gf.astype(CDT)
            kdf = kn.astype(f32) * t["fk"]
            dkdf = _mm_nt(vn, gfb)
            lgf_row = jnp.sum(kdf * dkdf * (RC - 1.0 - t["pos"]), axis=0, keepdims=True)
            lgf_row += float(RC) * t["lamf"] * jnp.sum(gf * sf_ref[n].astype(f32), axis=0, keepdims=True)
            acc_ref[0:1, :] += lgf_row
            emit_kv(rows, dkdf * t["fk"], _mm(kdf.astype(CDT), gfb), final)
            gf_ref[...] = gf * t["lamf"] + _mm_tn((qn.astype(f32) * t["fq"]).astype(CDT), dRn)

        def sweep(final):
            def step(i, carry):
                ascend(i, final)
                descend(NC - 1 - i, final)
                return carry
            return step

        lax.fori_loop(0, NC // 2, sweep(False), 0)
        lax.fori_loop(NC // 2, NC, sweep(True), 0)
        dlg = jnp.sum(acc_ref[...], axis=1, keepdims=True)
        dlogit = dlg * jax.nn.sigmoid(-dl_ref[:, 0:1])
        lane = lax.broadcasted_iota(jnp.int32, (8, 128), 1)
        dd_ref[...] = jnp.where(lane == pl.program_id(0), jnp.broadcast_to(dlogit, (8, 128)), 0.0)

    def seg(slot):
        return pl.BlockSpec((None, S, DK), lambda h: (slot, 0, h))

    head = pl.BlockSpec((S, DK), lambda h: (0, h))
    states = pl.BlockSpec((None, NC, DK, DK), lambda h: (h, 0, 0, 0))
    return _call(
        body, hooks, [dr, R, sfs, sbs, proj, proj, proj, proj, dproj, cos, sin, dl], name="ret_bwd", grid=(H,),
        out_shape=[jax.ShapeDtypeStruct(dproj.shape, CDT), jax.ShapeDtypeStruct((H, 8, 128), f32)],
        in_specs=[head, head, states, states, seg(SLOT_Q), seg(SLOT_K), seg(SLOT_VR), seg(SLOT_GR), _HBM,
                  _resident((S, DK // 2), lambda h: (0, 0)), _resident((S, DK // 2), lambda h: (0, 0)),
                  pl.BlockSpec((None, 8, DK), lambda h: (h, 0, 0))],
        out_specs=[pl.BlockSpec((4, S, DK), lambda h: (1, 0, h), pipeline_mode=pl.Buffered(1)),
                   pl.BlockSpec((None, 8, 128), lambda h: (h, 0, 0))],
        scratch_shapes=[pltpu.VMEM((S, DK), CDT),
                        pltpu.VMEM((DK, DK), f32), pltpu.VMEM((DK, DK), f32), pltpu.VMEM((8, DK), f32)],
        input_output_aliases={8: 0},
        compiler_params=_params("arbitrary"),
    )


def _proj_bwd_act(dproj, dx2, x1, nrm, wfull, hooks=(), tm=512):
    S = x1.shape[0]

    def body(dp_ref, dx2_ref, x_ref, n_ref, w_ref, dx_ref, dxh_ref, dn_ref, db_ref, acc_ref):
        @pl.when(pl.program_id(0) == 0)
        def _():
            dn_ref[...] = jnp.zeros_like(dn_ref)
            db_ref[...] = jnp.zeros_like(db_ref)

        for p in range(8):
            seg = SEG_OF_SLOT[p]
            dp = dp_ref[p]
            db_ref[seg] += jnp.sum(dp.astype(f32), axis=0, keepdims=True)
            dh = _mm_nt(dp, w_ref[seg * D:(seg + 1) * D, :])
            if p == 0:
                acc_ref[...] = dh
            else:
                acc_ref[...] += dh
        n = n_ref[...]
        r, xh, _ = _rms_fwd(x_ref[...], n)
        dx, dn = _rms_bwd(acc_ref[...], r, xh, n)
        dx = dx2_ref[...] + dx
        dx_ref[...] = dx
        dxh_ref[...] = (0.5 * dx).astype(CDT)
        dn_ref[...] += dn

    tok = pl.BlockSpec((tm, D), lambda i: (i, 0))
    return _call(
        body, hooks, [dproj, dx2, x1, nrm, wfull], name="proj_bwd_act", grid=(S // tm,),
        out_shape=[jax.ShapeDtypeStruct((S, D), f32), jax.ShapeDtypeStruct((S, D), CDT), jax.ShapeDtypeStruct((1, D), f32),
                   jax.ShapeDtypeStruct((8, 1, D), f32)],
        in_specs=[pl.BlockSpec((8, tm, D), lambda i: (0, i, 0)), tok, tok, _resident((1, D), lambda i: (0, 0)),
                  _resident((8 * D, D), lambda i: (0, 0))],
        out_specs=[tok, tok, pl.BlockSpec((1, D), lambda i: (0, 0)), pl.BlockSpec((8, 1, D), lambda i: (0, 0, 0))],
        scratch_shapes=[pltpu.VMEM((tm, D), f32)],
        compiler_params=_params("arbitrary"),
    )


def _rs_sum(name, gfulls, lands, my_c):
    n = len(gfulls)
    rows = gfulls[0].shape[0] // NDEV
    assert all(g.shape[0] == NDEV * rows for g in gfulls)

    def body(c_ref, *refs):
        for g_ref, l_ref, o_ref in zip(refs[:n], refs[n:2 * n], refs[2 * n:]):
            o_ref[...] = (g_ref[...].astype(f32) + l_ref[...].astype(f32)).astype(CDT)

    slot = pl.BlockSpec((None, rows, D), lambda k, c: (k, 0, 0))
    return pl.pallas_call(
        body, name=name,
        grid_spec=pltpu.PrefetchScalarGridSpec(
            num_scalar_prefetch=1, grid=(NCHIP,),
            in_specs=[pl.BlockSpec((rows, D), lambda k, c: (2 * k + c[0], 0))] * n + [slot] * n,
            out_specs=[slot] * n),
        out_shape=[jax.ShapeDtypeStruct((NCHIP, rows, D), CDT)] * n,
        compiler_params=_params("arbitrary"),
    )(my_c, *gfulls, *lands)


def _adamw_math(g, w, m, v):
    m2 = ADAM_B1 * m + (1.0 - ADAM_B1) * g
    v2 = ADAM_B2 * v + (1.0 - ADAM_B2) * (g * g)
    delta = -ADAM_LR * ((m2 / BC1) / (jnp.sqrt(v2 / BC2) + ADAM_EPS) + ADAM_WD * w)
    return delta, m2, v2


def _adamw_big(name, landed, w, m, v, after):
    rows = w.shape[0]
    tr = min(rows, 256) if rows % 256 == 0 else rows

    def body(l_ref, w_ref, m_ref, v_ref, after_ref, g_ref, d_ref, m2_ref, v2_ref):
        g = l_ref[0].astype(f32)
        for k in range(1, NCHIP):
            g = g + l_ref[k].astype(f32)
        g_ref[...] = g
        d_ref[...], m2_ref[...], v2_ref[...] = _adamw_math(g, w_ref[...], m_ref[...], v_ref[...])

    blk = pl.BlockSpec((tr, D), lambda i: (i, 0))
    o = jax.ShapeDtypeStruct((rows, D), f32)
    return pl.pallas_call(
        body, name=name, grid=(rows // tr,), out_shape=[o, o, o, o],
        in_specs=[pl.BlockSpec((NCHIP, tr, D), lambda i: (0, i, 0)), blk, blk, blk, _HBM],
        out_specs=[blk, blk, blk, blk],
        compiler_params=_params("arbitrary"),
    )(landed, w, m, v, after)


ROW_FFN1_NORM, ROW_MIX_NORM, ROW_SGU_G, ROW_SGU_B, ROW_FFN2_NORM, ROW_FINAL_NORM, ROW_B_IN = 0, 1, 2, 3, 4, 5, 8
ROW_WS, ROW_BS, ROW_DECAY = 0, G * C, G * C + G * 8


def _adamw_small(ga, gb, gn1, gl, params):
    def body(ga_ref, gb_ref, gn1_ref, gl_ref, *refs):
        ins, outs = refs[:30], refs[30:]

        def total(ref, r0, n):
            g = ref[0, r0:r0 + n, :]
            for j in range(1, NDEV):
                g = g + ref[j, r0:r0 + n, :]
            return g

        def apply(i, g, rows=slice(None)):
            w, m, v = ins[3 * i][rows, :], ins[3 * i + 1][rows, :], ins[3 * i + 2][rows, :]
            outs[4 * i][rows, :] = g
            outs[4 * i + 1][rows, :], outs[4 * i + 2][rows, :], outs[4 * i + 3][rows, :] = _adamw_math(g, w, m, v)

        outs[40][...] = total(gl_ref, 0, 8)
        apply(0, total(gn1_ref, 0, 1))
        for i, r in enumerate((ROW_FFN1_NORM, ROW_MIX_NORM, ROW_SGU_G, ROW_SGU_B, ROW_FFN2_NORM, ROW_FINAL_NORM)):
            if i:
                apply(i, total(ga_ref, r, 1))
        apply(6, total(ga_ref, ROW_B_IN, 8))
        apply(7, total(gb_ref, ROW_WS, G * C))
        for gi in range(G):
            apply(8, total(gb_ref, ROW_BS + 8 * gi, 1), slice(gi, gi + 1))
        dec = total(gb_ref, ROW_DECAY, 8)
        for hh in range(1, H):
            dec = dec + total(gb_ref, ROW_DECAY + 8 * hh, 8)
        apply(9, dec)

    flat = [a for p in params for a in p]
    out_shape = [jax.ShapeDtypeStruct(p[0].shape, f32) for p in params for _ in range(4)]
    out_shape.append(jax.ShapeDtypeStruct((8, 128), f32))
    vm = pl.BlockSpec(memory_space=pltpu.VMEM)
    return pl.pallas_call(
        body, name="adamw_small", out_shape=out_shape,
        in_specs=[vm] * (4 + len(flat)), out_specs=[vm] * len(out_shape),
        compiler_params=pltpu.CompilerParams(vmem_limit_bytes=VMEM_LIMIT),
    )(ga, gb, gn1, gl, *flat)


def kernel(x, ffn1_norm, ffn1_w_gate, ffn1_w_up, ffn1_w_down, mix_norm, w_in, b_in, sgu_norm_g, sgu_norm_b, sgu_w_s, sgu_b_s, ret_decay_logit, w_branch_a, w_branch_b, w_out, ffn2_norm, ffn2_w_gate, ffn2_w_up, ffn2_w_down, final_norm, loss_target, m_ffn1_norm, m_ffn1_w_gate, m_ffn1_w_up, m_ffn1_w_down, m_mix_norm, m_w_in, m_b_in, m_sgu_norm_g, m_sgu_norm_b, m_sgu_w_s, m_sgu_b_s, m_ret_decay_logit, m_w_branch_a, m_w_branch_b, m_w_out, m_ffn2_norm, m_ffn2_w_gate, m_ffn2_w_up, m_ffn2_w_down, m_final_norm, v_ffn1_norm, v_ffn1_w_gate, v_ffn1_w_up, v_ffn1_w_down, v_mix_norm, v_w_in, v_b_in, v_sgu_norm_g, v_sgu_norm_b, v_sgu_w_s, v_sgu_b_s, v_ret_decay_logit, v_w_branch_a, v_w_branch_b, v_w_out, v_ffn2_norm, v_ffn2_w_gate, v_ffn2_w_up, v_ffn2_w_down, v_final_norm):
    args = dict(locals())
    S = x.shape[1]
    xs = x[0]
    target = loss_target[0]

    def buf_layout(name, a):
        a = a[0]
        return a.T if name in W_TRANSPOSED else a

    sh = {n: buf_layout(n, args[n]).astype(CDT) for n in W_NAMES}
    wf = {}

    b3 = b_in.reshape(8, 1, D)
    ws = sgu_w_s[0].astype(CDT)
    wst = jnp.swapaxes(sgu_w_s[0], 1, 2).astype(CDT)
    bsc = sgu_b_s[0].reshape(G, C, 1)
    dl = jnp.zeros((H, 8, DK), f32).at[:, 0:2, :].set(jnp.broadcast_to(ret_decay_logit[0].T[:, :, None], (H, 2, DK)))
    theta = ROPE_BASE ** (-jnp.arange(0, DK, 2, dtype=f32) / DK)
    ang = jnp.arange(S, dtype=f32)[:, None] * theta[None, :]
    cos, sin = jnp.cos(ang), jnp.sin(ang)
    fnorm = final_norm.reshape(1, D)

    f1 = ("ffn1_w_gate", "ffn1_w_up", "ffn1_w_down")
    f2 = ("ffn2_w_gate", "ffn2_w_up", "ffn2_w_down")
    br = ("w_branch_a", "w_branch_b", "w_out")
    for cid, names in enumerate((f1, ("w_in",), br, f2)):
        wf.update(zip(names, _sequence("ag_" + names[0], 1 + cid, SIBLING_AND_CHIPS, [_ag_hook(sh[n]) for n in names])))
    x1, g1, u1, a1, hf1 = _ffn_fwd("ffn1_fwd", xs, ffn1_norm, *[wf[n] for n in f1])
    proj, h2 = _proj_fwd(x1, mix_norm, wf["w_in"], b3, cos, sin)
    a = _sgu_fwd(proj, sgu_norm_g, sgu_norm_b, ws, bsc)
    R, r, sfs, sbs = _ret_fwd(proj, dl)
    x2, ya, yb = _merge_fwd(a, r, proj, x1, *[wf[n] for n in br])
    x3, g2, u2, a2, hf2 = _ffn_fwd("ffn2_fwd", x2, ffn2_norm, *[wf[n] for n in f2])
    dx3, dyh2, d_final, loss_part = _loss_head(x3, fnorm, target)

    my_c = lax.axis_index("c").astype(jnp.int32).reshape(1)
    gw, landed = {}, {}

    def d2d(*names):
        return [_rs_d2d_hook(gw[n]) for n in names]

    def behind(x, token):
        return lax.optimization_barrier((x, token))[0]

    def to_chips(names, sibs, more=()):
        parts = _rs_sum("rs_sum_" + names[0], [gw[n] for n in names], list(sibs), my_c)
        hooks = [_rs_ici_hook(p) for p in parts] + ([_small_hook(list(more))] if more else [])
        reach = EVERYONE if more else CHIPS
        got = _sequence("rs_chips_" + names[0], REACH_ID[reach], reach, hooks)
        landed.update(zip(names, got))
        return got[len(names):], parts[0]

    def ffn_bwd(tag, names, dy, dyh, x, g, u, a, h, nrm, each_alone, more=()):
        wg, wu, wd = names
        (gw[wd],) = _wgrad(tag + "_wd_grad", a, dyh)
        dg, du, sib_d = _ffn_bwd_hidden(tag + "_bwd_hidden", dyh, g, u, wf[wd], d2d(wd))
        if each_alone:
            dg = behind(dg, to_chips([wd], [sib_d])[1])
        (gw[wg],) = _wgrad(tag + "_wg_grad", dg, h)
        gw[wu], sib_g = _wgrad(tag + "_wu_grad", du, h, d2d(wg))
        if each_alone:
            dy = behind(dy, to_chips([wg], [sib_g])[1])
        dx, dn, sib_u = _ffn_bwd_in(tag + "_bwd_in", dy, x, dg, du, nrm, wf[wg], wf[wu], d2d(wu))
        if each_alone:
            return (dx, dn) + to_chips([wu], [sib_u], more(dn))
        return (dx, dn) + to_chips([wd, wg, wu], [sib_d, sib_g, sib_u])

    dx2, d_ffn2n, _, token = ffn_bwd("ffn2", f2, dx3, dyh2, x2, g2, u2, a2, hf2, ffn2_norm, False)
    dproj, da, dr, mix, dx2b, dya, dyb = _merge_bwd_act(behind(dx2, token), ya, yb, proj, *[wf[n] for n in br])
    dproj, d_ws, d_bs, d_gn, d_bn = _sgu_bwd(da, proj, dproj, sgu_norm_g, sgu_norm_b, ws, wst, bsc)
    dproj, d_dec = _ret_bwd(dr, R, sfs, sbs, proj, dproj, cos, sin, dl)
    (gw["w_in"],) = _tn("win_grad", h2, dproj, _seg_of_slot)
    gw["w_out"], sib_win = _wgrad("wo_grad", mix, dx2b, d2d("w_in"))
    small_sgu = jnp.concatenate([d_ws.reshape(G * C, C), d_bs.reshape(G * 8, C)], axis=0)
    (g_sgu, gl), token = to_chips(["w_in"], [sib_win], [small_sgu, loss_part])
    (gw["w_branch_a"],) = _wgrad("wa_grad", a, behind(dya, token))
    (gw["w_branch_b"],) = _wgrad("wb_grad", r, dyb)
    dx1, dyh1, d_mixn, d_bin, *sib_br = _proj_bwd_act(behind(dproj, token), dx2, x1, mix_norm, wf["w_in"], d2d(*br))
    small_a = jnp.concatenate([jnp.zeros((1, D), f32), d_mixn, d_gn, d_bn, d_ffn2n, d_final, jnp.zeros((2, D), f32),
                               d_bin.reshape(8, D)], axis=0)
    (ga, g_dec), token = to_chips(list(br), sib_br, [small_a, d_dec.reshape(H * 8, 128)])
    gb = jnp.concatenate([g_sgu, g_dec], axis=1)
    dxs, d_ffn1n, (gn1,), token = ffn_bwd("ffn1", f1, dx1, behind(dyh1, token), xs, g1, u1, a1, hf1, ffn1_norm, True,
                                          lambda dn: [dn])

    out = {"grad_x": dxs[None]}

    def native(name, a):
        a = a.T if name in W_TRANSPOSED else a
        return a[None]

    after = token
    for n in f2 + ("w_in",) + br + (f1[2], f1[0], f1[1]):
        res = _adamw_big("adamw_" + n, landed[n], buf_layout(n, args[n]), buf_layout(n, args["m_" + n]),
                         buf_layout(n, args["v_" + n]), after)
        after = res[0]
        for pre, val in zip(("grad_", "delta_", "new_m_", "new_v_"), res):
            out[pre + n] = native(n, val)

    def pad_decay(a):
        return jnp.zeros((8, 128), f32).at[0:2, 0:H].set(a[0])

    small = [
        ("ffn1_norm", lambda a: a, lambda a: a), ("mix_norm", lambda a: a, lambda a: a),
        ("sgu_norm_g", lambda a: a, lambda a: a), ("sgu_norm_b", lambda a: a, lambda a: a),
        ("ffn2_norm", lambda a: a, lambda a: a),
        ("final_norm", lambda a: a.reshape(1, D), lambda a: a.reshape(D)),
        ("b_in", lambda a: a.reshape(8, D), lambda a: a.reshape(1, 8 * D)),
        ("sgu_w_s", lambda a: a.reshape(G * C, C), lambda a: a.reshape(1, G, C, C)),
        ("sgu_b_s", lambda a: a[0], lambda a: a[None]),
        ("ret_decay_logit", pad_decay, lambda a: a[None, 0:2, 0:H]),
    ]
    res = _adamw_small(ga, gb, gn1, gl, [(to(args[n]), to(args["m_" + n]), to(args["v_" + n])) for n, to, _ in small])
    out["loss"] = res[40][0, 0]
    for i, (n, _, back) in enumerate(small):
        for j, pre in enumerate(("grad_", "delta_", "new_m_", "new_v_")):
            out[pre + n] = back(res[4 * i + j])

    weights = ("ffn1_norm", "ffn1_w_gate", "ffn1_w_up", "ffn1_w_down", "mix_norm", "w_in", "b_in", "sgu_norm_g",
               "sgu_norm_b", "sgu_w_s", "sgu_b_s", "ret_decay_logit", "w_branch_a", "w_branch_b", "w_out", "ffn2_norm",
               "ffn2_w_gate", "ffn2_w_up", "ffn2_w_down", "final_norm")
    return (out["loss"], out["grad_x"], *[out["grad_" + n] for n in weights], *[out["delta_" + n] for n in weights],
            *[out["new_m_" + n] for n in weights], *[out["new_v_" + n] for n in weights])
```

```python
import functools
import math

import jax
import jax.numpy as jnp
from jax import lax
from jax.experimental import pallas as pl
from jax.experimental.pallas import tpu as pltpu
from jax.experimental.pallas import tpu_sc as plsc

f32 = jnp.float32
CDT = jnp.bfloat16

D = 1024
F = 2816
C = 128
RC = 256
H = 4
DK = 256
G = 4
NDEV = 8
NCHIP = 4
EPS = 1e-6
ROPE_BASE = 10000.0
FT = 256
V7X_VMEM_BYTES = 64 * 1024 * 1024
VMEM_LIMIT = V7X_VMEM_BYTES - 8 * 1024 * 1024

ADAM_LR, ADAM_B1, ADAM_B2, ADAM_EPS, ADAM_WD, ADAM_STEP = 0.001, 0.9, 0.999, 1e-08, 0.01, 10
BC1 = 1.0 - ADAM_B1 ** ADAM_STEP
BC2 = 1.0 - ADAM_B2 ** ADAM_STEP

W_ROWS = dict(ffn1_w_gate=352, ffn1_w_up=352, ffn1_w_down=352, w_in=1024, w_branch_a=128, w_branch_b=128, w_out=128,
              ffn2_w_gate=352, ffn2_w_up=352, ffn2_w_down=352)
W_NAMES = tuple(W_ROWS)
W_TRANSPOSED = ("ffn1_w_gate", "ffn1_w_up", "ffn2_w_gate", "ffn2_w_up")

SLOT_U, SLOT_V, SLOT_GA, SLOT_GB, SLOT_Q, SLOT_K, SLOT_VR, SLOT_GR = range(8)


SEG_OF_SLOT = (0, 1, 6, 7, 2, 3, 4, 5)


def _seg_of_slot(p):
    return jnp.where(p < 2, p, jnp.where(p < 4, p + 4, p - 2))


def _mm(a, b):
    return jnp.dot(a, b, preferred_element_type=f32)


def _mm_nt(a, b):
    return lax.dot_general(a, b, (((1,), (1,)), ((), ())), preferred_element_type=f32)


def _mm_tn(a, b):
    return lax.dot_general(a, b, (((0,), (0,)), ((), ())), preferred_element_type=f32)


def _params(*sem):
    return pltpu.CompilerParams(dimension_semantics=sem, vmem_limit_bytes=VMEM_LIMIT)


def _resident(shape, index_map):
    return pl.BlockSpec(shape, index_map, pipeline_mode=pl.Buffered(1))


def _gelu(x):
    return 0.5 * x * (1.0 + lax.erf(x * (1.0 / math.sqrt(2.0))))


def _gelu_grad(x):
    return 0.5 * (1.0 + lax.erf(x * (1.0 / math.sqrt(2.0)))) + x * jnp.exp(-0.5 * x * x) * (1.0 / math.sqrt(2.0 * math.pi))


def _rms_fwd(x, n):
    r = lax.rsqrt(jnp.mean(x * x, axis=-1, keepdims=True) + EPS)
    xh = x * r
    return r, xh, xh * n


def _rms_bwd(dh, r, xh, n):
    dxh = dh * n
    dx = r * (dxh - xh * jnp.mean(dxh * xh, axis=-1, keepdims=True))
    return dx, jnp.sum(dh * xh, axis=0, keepdims=True)


MESH_ID = pl.DeviceIdType.MESH
_HBM = pl.BlockSpec(memory_space=pltpu.HBM)


def _my_place():
    return lax.axis_index("x"), lax.axis_index("y"), lax.axis_index("c")


def _ici_peers(x, y, c):
    return [((1 - x, y, c), 2 * (1 - x) + y), ((x, 1 - y, c), 2 * x + 1 - y), ((1 - x, 1 - y, c), 2 * (1 - x) + 1 - y)]


class _Hook:
    def __init__(self, operands, out_shapes, n_remote, n_local, start, finish, relay=None):
        self.operands, self.out_shapes = list(operands), list(out_shapes)
        self.n_remote, self.n_local, self.start, self.finish = n_remote, n_local, start, finish
        self.relay = relay or (lambda *a: None)


def _call(body, hooks, operands, *, in_specs, out_specs, out_shape, grid=None, scratch_shapes=(), **kw):
    hooks = tuple(hooks)
    n_in, n_out, n_scr = len(in_specs), len(out_shape), len(scratch_shapes)
    h_ops = [a for h in hooks for a in h.operands]
    h_outs = [s for h in hooks for s in h.out_shapes]
    h_sems = [pltpu.SemaphoreType.DMA((n,)) for h in hooks for n in (h.n_remote, h.n_remote, max(h.n_local, 1))]

    def wrapped(*refs):
        ins, hin = refs[:n_in], refs[n_in:n_in + len(h_ops)]
        o0 = n_in + len(h_ops)
        outs, hout = refs[o0:o0 + n_out], refs[o0 + n_out:o0 + n_out + len(h_outs)]
        s0 = o0 + n_out + len(h_outs)
        scr, hsem = refs[s0:s0 + n_scr], refs[s0 + n_scr:]

        def run(phase):
            ip = op = 0
            for i, h in enumerate(hooks):
                ssem, rsem, lsem = hsem[3 * i:3 * i + 3]

                def rcopy(k, src, dst, dev, ssem=ssem, rsem=rsem):
                    return pltpu.make_async_remote_copy(src_ref=src, dst_ref=dst, send_sem=ssem.at[k], recv_sem=rsem.at[k],
                                                        device_id=dev, device_id_type=MESH_ID)

                def lcopy(k, src, dst, lsem=lsem):
                    return pltpu.make_async_copy(src, dst, lsem.at[k])

                getattr(h, phase)(hin[ip:ip + len(h.operands)], hout[op:op + len(h.out_shapes)], rcopy, lcopy)
                ip += len(h.operands)
                op += len(h.out_shapes)

        def at_edge(phase, last):
            if not hooks:
                return
            if grid is None:
                run(phase)
                return
            cond = None
            for ax, n in enumerate(grid):
                here = pl.program_id(ax) == (n - 1 if last else 0)
                cond = here if cond is None else cond & here
            pl.when(cond)(lambda: run(phase))

        at_edge("start", False)
        at_edge("relay", True)
        body(*ins, *outs, *scr)
        at_edge("finish", True)

    if grid is not None:
        kw["grid"] = grid
    return list(pl.pallas_call(
        wrapped, out_shape=list(out_shape) + h_outs, in_specs=list(in_specs) + [_HBM] * len(h_ops),
        out_specs=list(out_specs) + [_HBM] * len(h_outs), scratch_shapes=list(scratch_shapes) + h_sems, **kw,
    )(*operands, *h_ops))


def _exchange(name, hooks):
    return _call(lambda: None, hooks, [], name=name, in_specs=[], out_specs=[], out_shape=[])


def _rows(ref, start, n):
    return ref.at[pl.ds(start, n), :]


def _ag_hook(shard, early=True):
    rows = shard.shape[0]

    def block(full, dev_index):
        return _rows(full, dev_index * rows, rows)

    def start(ins, outs, rcopy, lcopy):
        x, y, c = _my_place()
        src, dst = ins[0], block(outs[0], 4 * x + 2 * y + c)
        lcopy(0, src, dst).start()
        rcopy(0, src, dst, (x, y, 1 - c)).start()
        for j, (dev, _) in enumerate(_ici_peers(x, y, c)):
            rcopy(1 + j, src, dst, dev).start()

    def relay(ins, outs, rcopy, lcopy):
        x, y, c = _my_place()
        for j, (dev, chip) in enumerate(_ici_peers(x, y, c)):
            blk = block(outs[0], 2 * chip + c)
            rcopy(1 + j, blk, blk, dev).wait_recv()
            rcopy(4 + j, blk, blk, (x, y, 1 - c)).start()

    def finish(ins, outs, rcopy, lcopy):
        if not early:
            relay(ins, outs, rcopy, lcopy)
        x, y, c = _my_place()
        sib = (x, y, 1 - c)
        full = outs[0]
        peers = _ici_peers(x, y, c)
        blk = block(full, 2 * (2 * x + y) + 1 - c)
        rcopy(0, blk, blk, sib).wait_recv()
        for j, (dev, chip) in enumerate(peers):
            blk = block(full, 2 * chip + 1 - c)
            rcopy(4 + j, blk, blk, sib).wait_recv()
        src, dst = ins[0], block(full, 4 * x + 2 * y + c)
        lcopy(0, src, dst).wait()
        rcopy(0, src, dst, sib).wait_send()
        for j, (dev, chip) in enumerate(peers):
            rcopy(1 + j, src, dst, dev).wait_send()
            blk = block(full, 2 * chip + c)
            rcopy(4 + j, blk, blk, sib).wait_send()

    return _Hook([shard], [jax.ShapeDtypeStruct((NDEV * rows, D), shard.dtype)], 7, 1, start, finish, relay if early else None)


SIBLING, CHIPS, SIBLING_AND_CHIPS, EVERYONE = "sibling", "chips", "sibling and chips", "everyone"
REACH_ID = {SIBLING: 5, CHIPS: 6, EVERYONE: 7}


def _sequence(name, collective_id, reach, hooks):
    ins = [[jax.new_ref(a, memory_space=pltpu.MemorySpace.HBM) for a in h.operands] for h in hooks]
    outs = [[jax.empty_ref(s, memory_space=pltpu.MemorySpace.HBM) for s in h.out_shapes] for h in hooks]
    sems = tuple(pltpu.SemaphoreType.DMA((n,)) for h in hooks for n in (h.n_remote, h.n_remote, max(h.n_local, 1)))

    @pl.kernel(mesh=plsc.ScalarSubcoreMesh(axis_name="sequencer", num_cores=1), name=name, scratch_types=sems,
               compiler_params=pltpu.CompilerParams(collective_id=collective_id))
    def launch(*sem_refs):
        x, y, c = _my_place()
        chips = [dev for dev, _ in _ici_peers(x, y, c)]
        others = [(1 - x if dx else x, 1 - y if dy else y, 1 - c if dc else c)
                  for dx in range(2) for dy in range(2) for dc in range(2) if dx + dy + dc]
        devs = {SIBLING: [(x, y, 1 - c)], CHIPS: chips, SIBLING_AND_CHIPS: [(x, y, 1 - c)] + chips, EVERYONE: others}[reach]
        barrier = pltpu.get_barrier_semaphore()
        for dev in devs:
            pl.semaphore_signal(barrier, inc=1, device_id=dev, device_id_type=MESH_ID)
        pl.semaphore_wait(barrier, len(devs))
        for phase in ("start", "relay", "finish"):
            for i, h in enumerate(hooks):
                ssem, rsem, lsem = sem_refs[3 * i:3 * i + 3]

                def rcopy(k, src, dst, dev, ssem=ssem, rsem=rsem):
                    return pltpu.make_async_remote_copy(src_ref=src, dst_ref=dst, send_sem=ssem.at[k], recv_sem=rsem.at[k],
                                                        device_id=dev, device_id_type=MESH_ID)

                def lcopy(k, src, dst, lsem=lsem):
                    return pltpu.make_async_copy(src, dst, lsem.at[k])

                getattr(h, phase)(ins[i], outs[i], rcopy, lcopy)

    launch()
    return [o[...] for os in outs for o in os]


def _rs_d2d_hook(gfull):
    rows = gfull.shape[0] // NDEV

    def pairs(g, land):
        x, y, c = _my_place()
        return (x, y, 1 - c), [(k, _rows(g, (2 * k + 1 - c) * rows, rows), land.at[k]) for k in range(NCHIP)]

    def start(ins, outs, rcopy, lcopy):
        sib, cps = pairs(ins[0], outs[0])
        for i, src, dst in cps:
            rcopy(i, src, dst, sib).start()

    def finish(ins, outs, rcopy, lcopy):
        sib, cps = pairs(ins[0], outs[0])
        for i, src, dst in cps:
            rcopy(i, dst, dst, sib).wait_recv()
        for i, src, dst in cps:
            rcopy(i, src, dst, sib).wait_send()

    return _Hook([gfull], [jax.ShapeDtypeStruct((NCHIP, rows, D), gfull.dtype)], NCHIP, 0, start, finish)


def _rs_ici_hook(part):
    def start(ins, outs, rcopy, lcopy):
        x, y, c = _my_place()
        mychip = 2 * x + y
        lcopy(0, ins[0].at[mychip], outs[0].at[mychip]).start()
        for j, (dev, chip) in enumerate(_ici_peers(x, y, c)):
            rcopy(j, ins[0].at[chip], outs[0].at[mychip], dev).start()

    def finish(ins, outs, rcopy, lcopy):
        x, y, c = _my_place()
        mychip = 2 * x + y
        peers = _ici_peers(x, y, c)
        for j, (dev, chip) in enumerate(peers):
            rcopy(j, outs[0].at[chip], outs[0].at[chip], dev).wait_recv()
        for j, (dev, chip) in enumerate(peers):
            rcopy(j, ins[0].at[chip], outs[0].at[mychip], dev).wait_send()
        lcopy(0, ins[0].at[mychip], outs[0].at[mychip]).wait()

    return _Hook([part], [jax.ShapeDtypeStruct(part.shape, part.dtype)], 3, 1, start, finish)


def _small_hook(arrays):
    n = len(arrays)

    def peers():
        x, y, c = _my_place()
        out = []
        for dx in range(2):
            for dy in range(2):
                for dc in range(2):
                    if dx + dy + dc:
                        px, py, pc = (1 - x if dx else x), (1 - y if dy else y), (1 - c if dc else c)
                        out.append(((px, py, pc), 4 * px + 2 * py + pc))
        return 4 * x + 2 * y + c, out

    def start(ins, outs, rcopy, lcopy):
        me, ps = peers()
        for t in range(n):
            lcopy(t, ins[t], outs[t].at[me]).start()
            for i, (dev, _) in enumerate(ps):
                rcopy(n * i + t, ins[t], outs[t].at[me], dev).start()

    def finish(ins, outs, rcopy, lcopy):
        me, ps = peers()
        for t in range(n):
            for i, (dev, peer) in enumerate(ps):
                rcopy(n * i + t, outs[t].at[peer], outs[t].at[peer], dev).wait_recv()
            for i, (dev, _) in enumerate(ps):
                rcopy(n * i + t, ins[t], outs[t].at[me], dev).wait_send()
            lcopy(t, ins[t], outs[t].at[me]).wait()

    return _Hook(arrays, [jax.ShapeDtypeStruct((NDEV,) + a.shape, a.dtype) for a in arrays], 7 * n, n, start, finish)


def _wblock(w):
    return _resident(w.shape, lambda *_: (0, 0))


def _ffn_fwd(name, x, nrm, wg, wu, wd, hooks=(), tm=512):
    S = x.shape[0]

    def body(x_ref, n_ref, wg_ref, wu_ref, wd_ref, y_ref, g_ref, u_ref, a_ref, h_ref, acc_ref):
        xv = x_ref[...]
        _, _, h = _rms_fwd(xv, n_ref[...])
        h = h.astype(CDT)
        h_ref[...] = h
        for ci in range(F // FT):
            sl = slice(ci * FT, (ci + 1) * FT)
            g = _mm_nt(h, wg_ref[sl, :])
            u = _mm_nt(h, wu_ref[sl, :])
            g_ref[:, sl] = g.astype(CDT)
            u_ref[:, sl] = u.astype(CDT)
            a = (g * jax.nn.sigmoid(g) * u).astype(CDT)
            a_ref[:, sl] = a
            o = _mm(a, wd_ref[sl, :])
            if ci == 0:
                acc_ref[...] = o
            else:
                acc_ref[...] += o
        y_ref[...] = xv + 0.5 * acc_ref[...]

    tok = pl.BlockSpec((tm, D), lambda i: (i, 0))
    hid = pl.BlockSpec((tm, F), lambda i: (i, 0))
    hidden = jax.ShapeDtypeStruct((S, F), CDT)
    return _call(
        body, hooks, [x, nrm, wg, wu, wd], name=name, grid=(S // tm,),
        out_shape=[jax.ShapeDtypeStruct((S, D), f32), hidden, hidden, hidden, jax.ShapeDtypeStruct((S, D), CDT)],
        in_specs=[tok, _resident((1, D), lambda i: (0, 0)), _wblock(wg), _wblock(wu), _wblock(wd)],
        out_specs=[tok, hid, hid, hid, tok],
        scratch_shapes=[pltpu.VMEM((tm, D), f32)],
        compiler_params=_params("arbitrary"),
    )


def _proj_fwd(x1, nrm, wfull, b3, cos, sin, hooks=(), tm=512):
    S = x1.shape[0]

    def body(x_ref, n_ref, w_ref, b_ref, cos_ref, sin_ref, p_ref, h_ref):
        _, _, h = _rms_fwd(x_ref[...], n_ref[...])
        h = h.astype(CDT)
        h_ref[...] = h
        for p in range(8):
            seg = SEG_OF_SLOT[p]
            z = _mm(h, w_ref[seg * D:(seg + 1) * D, :]) + b_ref[seg]
            if p in (SLOT_Q, SLOT_K):
                co, si = cos_ref[...], sin_ref[...]
                for hh in range(H):
                    cs = slice(hh * DK, (hh + 1) * DK)
                    zr = _rotate(z[:, cs], co, si)
                    p_ref[p, :, cs] = (zr * K_SCALE if p == SLOT_K else zr).astype(CDT)
            else:
                p_ref[p] = z.astype(CDT)

    tab = pl.BlockSpec((tm, DK // 2), lambda i: (i, 0))
    return _call(
        body, hooks, [x1, nrm, wfull, b3, cos, sin], name="proj_fwd", grid=(S // tm,),
        out_shape=[jax.ShapeDtypeStruct((8, S, D), CDT), jax.ShapeDtypeStruct((S, D), CDT)],
        in_specs=[pl.BlockSpec((tm, D), lambda i: (i, 0)), _resident((1, D), lambda i: (0, 0)),
                  _resident((8 * D, D), lambda i: (0, 0)), _resident((8, 1, D), lambda i: (0, 0, 0)), tab, tab],
        out_specs=[pl.BlockSpec((8, tm, D), lambda i: (0, i, 0)), pl.BlockSpec((tm, D), lambda i: (i, 0))],
        compiler_params=_params("arbitrary"),
    )


def _sgu_norm(va, gn, bn):
    mu = jnp.mean(va, axis=-1, keepdims=True)
    xc = va - mu
    rstd = lax.rsqrt(jnp.mean(xc * xc, axis=-1, keepdims=True) + EPS)
    vhat = xc * rstd
    return rstd, vhat, vhat * gn + bn


def _sgu_fwd(proj, gn, bn, ws, bsc, tm=512):
    S = proj.shape[1]
    GW = D // G

    def body(p_ref, gn_ref, bn_ref, ws_ref, bs_ref, a_ref):
        ua = _gelu(p_ref[0].astype(f32))
        va = _gelu(p_ref[1].astype(f32))
        _, _, vn = _sgu_norm(va, gn_ref[...], bn_ref[...])
        vn = vn.astype(CDT)
        for ch in range(tm // C):
            rs = slice(ch * C, (ch + 1) * C)
            for gi in range(G):
                cs = slice(gi * GW, (gi + 1) * GW)
                s = _mm(ws_ref[gi], vn[rs, cs]) + bs_ref[gi]
                a_ref[rs, cs] = (ua[rs, cs] * s).astype(CDT)

    return pl.pallas_call(
        body, name="sgu_fwd", grid=(S // tm,),
        out_shape=jax.ShapeDtypeStruct((S, D), CDT),
        in_specs=[pl.BlockSpec((2, tm, D), lambda i: (0, i, 0)), _resident((1, D), lambda i: (0, 0)),
                  _resident((1, D), lambda i: (0, 0)), _resident((G, C, C), lambda i: (0, 0, 0)),
                  _resident((G, C, 1), lambda i: (0, 0, 0))],
        out_specs=pl.BlockSpec((tm, D), lambda i: (i, 0)),
        compiler_params=_params("arbitrary"),
    )(proj, gn, bn, ws, bsc)


def _decay_tables(dl_ref):
    lg = jax.nn.log_sigmoid(dl_ref[0:2, :])
    lgf, lgb = lg[0:1, :], lg[1:2, :]
    assert RC <= DK
    ri = lax.broadcasted_iota(jnp.int32, (RC, RC), 0)
    ci = lax.broadcasted_iota(jnp.int32, (RC, RC), 1)
    d = (ri - ci).astype(f32)
    lower = d >= 0
    dmat = jnp.where(lower, jnp.exp(d * lgf[:, :RC]), jnp.exp(-d * lgb[:, :RC]))
    dmat_t = jnp.where(d <= 0, jnp.exp(-d * lgf[:, :RC]), jnp.exp(d * lgb[:, :RC]))
    pos = lax.broadcasted_iota(jnp.int32, (RC, DK), 0).astype(f32)
    t = dict(
        lgf=lgf, lgb=lgb, d=d, lower=lower, dmat=dmat, dmat_t=dmat_t, pos=pos,
        fq=jnp.exp((pos + 1.0) * lgf), fk=jnp.exp((RC - 1.0 - pos) * lgf),
        bq=jnp.exp((RC - pos) * lgb), bk=jnp.exp(pos * lgb),
        lamf=jnp.exp(float(RC) * lgf), lamb=jnp.exp(float(RC) * lgb),
    )
    return t


def _rotate(t, co, si):
    t1, t2 = t[:, :DK // 2], t[:, DK // 2:]
    return jnp.concatenate([t1 * co - t2 * si, t2 * co + t1 * si], axis=-1)


def _unrotate(t, co, si):
    t1, t2 = t[:, :DK // 2], t[:, DK // 2:]
    return jnp.concatenate([t1 * co + t2 * si, t2 * co - t1 * si], axis=-1)


K_SCALE = DK ** -0.5
ROW_TILE = 256


def _ret_fwd(proj, dl, hooks=()):
    S = proj.shape[1]
    NC = S // RC

    def body(q_ref, k_ref, v_ref, g_ref, dl_ref, R_ref, r_ref, sfs_ref, sbs_ref, rb_ref, sf_ref, sb_ref):
        t = _decay_tables(dl_ref)

        def chunk(n):
            rows = pl.ds(pl.multiple_of(n * RC, RC), RC)
            return rows, q_ref[rows, :], k_ref[rows, :], v_ref[rows, :]

        sf_ref[...] = jnp.zeros_like(sf_ref)
        sb_ref[...] = jnp.zeros_like(sb_ref)

        def step(i, carry):
            rows, qn, kn, vn = chunk(i)
            sc = _mm_nt(qn, kn) * t["dmat"]
            out = _mm(sc.astype(CDT), vn)
            sf = sf_ref[...]
            sfb = sf.astype(CDT)
            sfs_ref[i] = sfb
            R_ref[rows, :] = out + _mm((qn.astype(f32) * t["fq"]).astype(CDT), sfb)
            sf_ref[...] = sf * t["lamf"] + _mm_tn((kn.astype(f32) * t["fk"]).astype(CDT), vn)
            m = NC - 1 - i
            rows, qn, kn, vn = chunk(m)
            sb = sb_ref[...]
            sbb = sb.astype(CDT)
            sbs_ref[m] = sbb
            rb_ref[rows, :] = _mm((qn.astype(f32) * t["bq"]).astype(CDT), sbb)
            sb_ref[...] = sb * t["lamb"] + _mm_tn((kn.astype(f32) * t["bk"]).astype(CDT), vn)
            return carry

        lax.fori_loop(0, NC, step, 0)

        def finish(i, carry):
            rs = pl.ds(pl.multiple_of(i * ROW_TILE, ROW_TILE), ROW_TILE)
            R = R_ref[rs, :] + rb_ref[rs, :]
            R_ref[rs, :] = R
            rn = R * lax.rsqrt(jnp.mean(R * R, axis=-1, keepdims=True) + EPS)
            g = g_ref[rs, :].astype(f32)
            r_ref[rs, :] = (rn * g * jax.nn.sigmoid(g)).astype(CDT)
            return carry

        lax.fori_loop(0, S // ROW_TILE, finish, 0)

    def seg(slot):
        return pl.BlockSpec((None, S, DK), lambda h: (slot, 0, h))

    states = jax.ShapeDtypeStruct((H, NC, DK, DK), CDT)
    state_blk = pl.BlockSpec((None, NC, DK, DK), lambda h: (h, 0, 0, 0))
    return _call(
        body, hooks, [proj, proj, proj, proj, dl], name="ret_fwd", grid=(H,),
        out_shape=[jax.ShapeDtypeStruct((S, H * DK), f32), jax.ShapeDtypeStruct((S, H * DK), CDT), states, states],
        in_specs=[seg(SLOT_Q), seg(SLOT_K), seg(SLOT_VR), seg(SLOT_GR), pl.BlockSpec((None, 8, DK), lambda h: (h, 0, 0))],
        out_specs=[pl.BlockSpec((S, DK), lambda h: (0, h)), pl.BlockSpec((S, DK), lambda h: (0, h)), state_blk, state_blk],
        scratch_shapes=[pltpu.VMEM((S, DK), f32), pltpu.VMEM((DK, DK), f32), pltpu.VMEM((DK, DK), f32)],
        compiler_params=_params("arbitrary"),
    )


def _merge_fwd(a, r, proj, x1, wa, wb, wo, hooks=(), tm=512):
    S = x1.shape[0]

    def body(a_ref, r_ref, gt_ref, x_ref, wa_ref, wb_ref, wo_ref, x2_ref, ya_ref, yb_ref):
        ya = _mm(a_ref[...], wa_ref[...])
        yb = _mm(r_ref[...], wb_ref[...])
        ya_ref[...] = ya.astype(CDT)
        yb_ref[...] = yb.astype(CDT)
        mix = jax.nn.sigmoid(gt_ref[0].astype(f32)) * ya + jax.nn.sigmoid(gt_ref[1].astype(f32)) * yb
        x2_ref[...] = x_ref[...] + _mm(mix.astype(CDT), wo_ref[...])

    tok = pl.BlockSpec((tm, D), lambda i: (i, 0))
    return _call(
        body, hooks, [a, r, proj, x1, wa, wb, wo], name="merge_fwd", grid=(S // tm,),
        out_shape=[jax.ShapeDtypeStruct((S, D), f32), jax.ShapeDtypeStruct((S, D), CDT), jax.ShapeDtypeStruct((S, D), CDT)],
        in_specs=[tok, tok, pl.BlockSpec((2, tm, D), lambda i: (SLOT_GA // 2, i, 0)), tok,
                  _wblock(wa), _wblock(wb), _wblock(wo)],
        out_specs=[tok, tok, tok],
        compiler_params=_params("arbitrary"),
    )


def _loss_head(x3, fn, target, tm=512):
    S = x3.shape[0]

    def body(x_ref, n_ref, t_ref, dx_ref, dxh_ref, dn_ref, l_ref):
        n = n_ref[...]
        r, xh, y = _rms_fwd(x_ref[...], n)
        e = y - t_ref[...]
        dy = e * (1.0 / D)
        dx, dn = _rms_bwd(dy, r, xh, n)
        dx_ref[...] = dx
        dxh_ref[...] = (0.5 * dx).astype(CDT)
        part = 0.5 * jnp.sum(jnp.sum(e * e, axis=-1, keepdims=True), axis=0, keepdims=True) * (1.0 / D)

        @pl.when(pl.program_id(0) == 0)
        def _():
            dn_ref[...] = jnp.zeros_like(dn_ref)
            l_ref[...] = jnp.zeros_like(l_ref)

        dn_ref[...] += dn
        l_ref[...] += jnp.broadcast_to(part, l_ref.shape)

    tok = pl.BlockSpec((tm, D), lambda i: (i, 0))
    return pl.pallas_call(
        body, name="loss_head", grid=(S // tm,),
        out_shape=[jax.ShapeDtypeStruct((S, D), f32), jax.ShapeDtypeStruct((S, D), CDT), jax.ShapeDtypeStruct((1, D), f32),
                   jax.ShapeDtypeStruct((8, 128), f32)],
        in_specs=[tok, _resident((1, D), lambda i: (0, 0)), tok],
        out_specs=[tok, tok, pl.BlockSpec((1, D), lambda i: (0, 0)), pl.BlockSpec((8, 128), lambda i: (0, 0))],
        compiler_params=_params("arbitrary"),
    )(x3, fn, target)


def _ffn_bwd_hidden(name, dyh, g, u, wd, hooks=(), tm=512):
    S = dyh.shape[0]

    def body(dyh_ref, g_ref, u_ref, wd_ref, dg_ref, du_ref):
        dyh = dyh_ref[...]
        for ci in range(F // FT):
            sl = slice(ci * FT, (ci + 1) * FT)
            da = _mm_nt(dyh, wd_ref[sl, :])
            gv = g_ref[:, sl].astype(f32)
            uv = u_ref[:, sl].astype(f32)
            s = jax.nn.sigmoid(gv)
            du_ref[:, sl] = (da * (gv * s)).astype(CDT)
            dg_ref[:, sl] = (da * uv * (s * (1.0 + gv * (1.0 - s)))).astype(CDT)

    hid = pl.BlockSpec((tm, F), lambda i: (i, 0))
    hidden = jax.ShapeDtypeStruct((S, F), CDT)
    return _call(
        body, hooks, [dyh, g, u, wd], name=name, grid=(S // tm,), out_shape=[hidden, hidden],
        in_specs=[pl.BlockSpec((tm, D), lambda i: (i, 0)), hid, hid, _wblock(wd)], out_specs=[hid, hid],
        compiler_params=_params("arbitrary"),
    )


def _ffn_bwd_in(name, dy, x, dg, du, nrm, wg, wu, hooks=(), part=None, prev=None, tm=512):
    S = x.shape[0]
    t0, nt = part or (0, S // tm)

    def body(dy_ref, x_ref, dg_ref, du_ref, n_ref, wg_ref, wu_ref, *rest):
        dx_ref, dn_ref, acc_ref = rest[-3:]
        n = n_ref[...]
        r, xh, _ = _rms_fwd(x_ref[...], n)
        for ci in range(F // FT):
            sl = slice(ci * FT, (ci + 1) * FT)
            dh = _mm(dg_ref[:, sl], wg_ref[sl, :]) + _mm(du_ref[:, sl], wu_ref[sl, :])
            if ci == 0:
                acc_ref[...] = dh
            else:
                acc_ref[...] += dh
        dx, dn = _rms_bwd(acc_ref[...], r, xh, n)
        dx_ref[...] = dy_ref[...] + dx

        @pl.when(pl.program_id(0) == 0)
        def _():
            dn_ref[...] = jnp.zeros_like(dn_ref) if prev is None else rest[1][...]

        dn_ref[...] += dn

    tok = pl.BlockSpec((tm, D), lambda i: (t0 + i, 0))
    hid = pl.BlockSpec((tm, F), lambda i: (t0 + i, 0))
    row = pl.BlockSpec((1, D), lambda i: (0, 0))
    in_specs = [tok, tok, hid, hid, _resident((1, D), lambda i: (0, 0)), _wblock(wg), _wblock(wu)]
    operands = [dy, x, dg, du, nrm, wg, wu]
    aliases = {}
    if prev is not None:
        in_specs += [_HBM, row]
        operands += list(prev)
        aliases = {7: 0}
    return _call(
        body, hooks, operands, name=name, grid=(nt,),
        out_shape=[jax.ShapeDtypeStruct((S, D), f32), jax.ShapeDtypeStruct((1, D), f32)],
        in_specs=in_specs, out_specs=[tok, row],
        scratch_shapes=[pltpu.VMEM((tm, D), f32)],
        input_output_aliases=aliases,
        compiler_params=_params("arbitrary"),
    )


TN_ROWS = 512


def _tn(name, xs, ys, block_of, hooks=()):
    S, M = xs.shape
    B = ys.shape[0]
    tr = TN_ROWS if M % TN_ROWS == 0 else M // 2
    assert M % tr == 0 and tr % 128 == 0
    nt = M // tr

    def body(x_ref, y_ref, o_ref):
        o_ref[...] = _mm_tn(x_ref[...], y_ref[...]).astype(CDT)

    return _call(
        body, hooks, [xs, ys], name=name, grid=(B, nt),
        out_shape=[jax.ShapeDtypeStruct((B * M, D), CDT)],
        in_specs=[pl.BlockSpec((S, tr), lambda b, i: (0, i)), pl.BlockSpec((None, S, D), lambda b, i: (b, 0, 0))],
        out_specs=[pl.BlockSpec((tr, D), lambda b, i: (block_of(b) * nt + i, 0))],
        compiler_params=_params("arbitrary", "arbitrary"),
    )


def _wgrad(name, xs, y, hooks=()):
    return _tn(name, xs, y[None], lambda b: 0, hooks)


def _merge_bwd_act(dx2, ya, yb, proj, wa, wb, wo, hooks=(), tm=512):
    S = dx2.shape[0]

    def body(dx_ref, ya_ref, yb_ref, gt_ref, wa_ref, wb_ref, wo_ref,
             dp_ref, da_ref, dr_ref, mix_ref, dxb_ref, dya_ref, dyb_ref):
        dxb = dx_ref[...].astype(CDT)
        dxb_ref[...] = dxb
        dmix = _mm_nt(dxb, wo_ref[...])
        ya = ya_ref[...].astype(f32)
        yb = yb_ref[...].astype(f32)
        sa = jax.nn.sigmoid(gt_ref[0].astype(f32))
        sb = jax.nn.sigmoid(gt_ref[1].astype(f32))
        mix_ref[...] = (sa * ya + sb * yb).astype(CDT)
        dya = (dmix * sa).astype(CDT)
        dyb = (dmix * sb).astype(CDT)
        dya_ref[...] = dya
        dyb_ref[...] = dyb
        dp_ref[0] = (dmix * ya * sa * (1.0 - sa)).astype(CDT)
        dp_ref[1] = (dmix * yb * sb * (1.0 - sb)).astype(CDT)
        da_ref[...] = _mm_nt(dya, wa_ref[...]).astype(CDT)
        dr_ref[...] = _mm_nt(dyb, wb_ref[...]).astype(CDT)

    tok = pl.BlockSpec((tm, D), lambda i: (i, 0))
    gates = pl.BlockSpec((2, tm, D), lambda i: (SLOT_GA // 2, i, 0))
    act = jax.ShapeDtypeStruct((S, D), CDT)
    return _call(
        body, hooks, [dx2, ya, yb, proj, wa, wb, wo], name="merge_bwd_act", grid=(S // tm,),
        out_shape=[jax.ShapeDtypeStruct((8, S, D), CDT), act, act, act, act, act, act],
        in_specs=[tok, tok, tok, gates, _wblock(wa), _wblock(wb), _wblock(wo)],
        out_specs=[gates, tok, tok, tok, tok, tok, tok],
        compiler_params=_params("arbitrary"),
    )


def _sgu_bwd(da, proj, dproj, gn, bn, ws, wst, bsc, hooks=(), tm=512):
    S = proj.shape[1]
    GW = D // G

    def body(da_ref, p_ref, dpin_ref, gn_ref, bn_ref, ws_ref, wst_ref, bs_ref,
             dp_ref, dws_ref, dbs_ref, dgn_ref, dbn_ref, ds_ref, dvn_ref):
        @pl.when(pl.program_id(0) == 0)
        def _():
            dws_ref[...] = jnp.zeros_like(dws_ref)
            dbs_ref[...] = jnp.zeros_like(dbs_ref)
            dgn_ref[...] = jnp.zeros_like(dgn_ref)
            dbn_ref[...] = jnp.zeros_like(dbn_ref)

        pu = p_ref[0].astype(f32)
        pv = p_ref[1].astype(f32)
        ua = _gelu(pu)
        va = _gelu(pv)
        gn = gn_ref[...]
        rstd, vhat, vn = _sgu_norm(va, gn, bn_ref[...])
        vnb = vn.astype(CDT)
        dav = da_ref[...].astype(f32)
        dsb = (dav * ua).astype(CDT)
        ones = jnp.ones((8, GW), CDT)
        for ch in range(tm // C):
            rs = slice(ch * C, (ch + 1) * C)
            for gi in range(G):
                cs = slice(gi * GW, (gi + 1) * GW)
                s = _mm(ws_ref[gi], vnb[rs, cs]) + bs_ref[gi]
                ds_ref[rs, cs] = s
                dsg = dsb[rs, cs]
                dws_ref[gi] += _mm_nt(dsg, vnb[rs, cs])
                dbs_ref[gi] += _mm_nt(ones, dsg)
                dvn_ref[rs, cs] = _mm(wst_ref[gi], dsg)
        dp_ref[0] = (dav * ds_ref[...] * _gelu_grad(pu)).astype(CDT)
        dvn = dvn_ref[...]
        dgn_ref[...] += jnp.sum(dvn * vhat, axis=0, keepdims=True)
        dbn_ref[...] += jnp.sum(dvn, axis=0, keepdims=True)
        dvh = dvn * gn
        dva = rstd * (dvh - jnp.mean(dvh, axis=-1, keepdims=True) - vhat * jnp.mean(dvh * vhat, axis=-1, keepdims=True))
        dp_ref[1] = (dva * _gelu_grad(pv)).astype(CDT)

    uv = pl.BlockSpec((2, tm, D), lambda i: (0, i, 0))
    row = _resident((1, D), lambda i: (0, 0))
    return _call(
        body, hooks, [da, proj, dproj, gn, bn, ws, wst, bsc], name="sgu_bwd", grid=(S // tm,),
        out_shape=[jax.ShapeDtypeStruct(dproj.shape, CDT), jax.ShapeDtypeStruct((G, C, C), f32),
                   jax.ShapeDtypeStruct((G, 8, C), f32), jax.ShapeDtypeStruct((1, D), f32), jax.ShapeDtypeStruct((1, D), f32)],
        in_specs=[pl.BlockSpec((tm, D), lambda i: (i, 0)), uv, _HBM, row, row,
                  _resident((G, C, C), lambda i: (0, 0, 0)), _resident((G, C, C), lambda i: (0, 0, 0)),
                  _resident((G, C, 1), lambda i: (0, 0, 0))],
        out_specs=[uv, pl.BlockSpec((G, C, C), lambda i: (0, 0, 0)), pl.BlockSpec((G, 8, C), lambda i: (0, 0, 0)),
                   pl.BlockSpec((1, D), lambda i: (0, 0)), pl.BlockSpec((1, D), lambda i: (0, 0))],
        scratch_shapes=[pltpu.VMEM((tm, D), f32), pltpu.VMEM((tm, D), f32)],
        input_output_aliases={2: 0},
        compiler_params=_params("arbitrary"),
    )


def _ret_bwd(dr, R, sfs, sbs, proj, dproj, cos, sin, dl, hooks=()):
    S = proj.shape[1]
    NC = S // RC
    assert NC % 2 == 0

    def body(dr_ref, R_ref, sf_ref, sb_ref, q_ref, k_ref, v_ref, g_ref, dpin_ref, cos_ref, sin_ref, dl_ref,
             dp_ref, dd_ref, dR_ref, gb_ref, gf_ref, acc_ref):
        t = _decay_tables(dl_ref)

        def gate_norm_bwd(i, carry):
            rs = pl.ds(pl.multiple_of(i * ROW_TILE, ROW_TILE), ROW_TILE)
            Rv = R_ref[rs, :]
            rstd = lax.rsqrt(jnp.mean(Rv * Rv, axis=-1, keepdims=True) + EPS)
            rn = Rv * rstd
            gv = g_ref[rs, :].astype(f32)
            s = jax.nn.sigmoid(gv)
            drv = dr_ref[rs, :].astype(f32)
            dp_ref[3, rs, :] = (drv * rn * (s * (1.0 + gv * (1.0 - s)))).astype(CDT)
            drn = drv * gv * s
            dR_ref[rs, :] = (rstd * (drn - rn * jnp.mean(drn * rn, axis=-1, keepdims=True))).astype(CDT)
            return carry

        lax.fori_loop(0, S // ROW_TILE, gate_norm_bwd, 0)

        def chunk(n):
            rows = pl.ds(pl.multiple_of(n * RC, RC), RC)
            return rows, q_ref[rows, :], k_ref[rows, :], v_ref[rows, :], dR_ref[rows, :]

        def emit_kv(rows, dk, dv, final):
            if not final:
                dp_ref[1, rows, :] = dk.astype(CDT)
                dp_ref[2, rows, :] = dv.astype(CDT)
            else:
                co, si = cos_ref[rows, :], sin_ref[rows, :]
                dk = dp_ref[1, rows, :].astype(f32) + dk
                dp_ref[1, rows, :] = (_unrotate(dk, co, si) * K_SCALE).astype(CDT)
                dp_ref[2, rows, :] = (dp_ref[2, rows, :].astype(f32) + dv).astype(CDT)

        gb_ref[...] = jnp.zeros_like(gb_ref)
        gf_ref[...] = jnp.zeros_like(gf_ref)
        acc_ref[...] = jnp.zeros_like(acc_ref)
        dpos = jnp.abs(t["d"])

        def ascend(n, final):
            rows, qn, kn, vn, dRn = chunk(n)
            qf, kf = qn.astype(f32), kn.astype(f32)
            sc = _mm_nt(qn, kn)
            dA = _mm_nt(dRn, vn)
            w = sc * dA * t["dmat"] * dpos
            lgf_part = jnp.sum(jnp.where(t["lower"], w, 0.0), axis=0, keepdims=True)
            lgb_part = jnp.sum(jnp.where(t["lower"], 0.0, w), axis=0, keepdims=True)
            dsc = (dA * t["dmat"]).astype(CDT)
            dq = _mm(dsc, kn)
            scT = (_mm_nt(kn, qn) * t["dmat_t"]).astype(CDT)
            dscT = (_mm_nt(vn, dRn) * t["dmat_t"]).astype(CDT)
            dk = _mm(dscT, qn)
            dv = _mm(scT, dRn)
            sfb = sf_ref[n]
            sbb = sb_ref[n]
            qdf = qf * t["fq"]
            dqdf = _mm_nt(dRn, sfb)
            dq += dqdf * t["fq"]
            lgf_row = jnp.sum(qdf * dqdf * (t["pos"] + 1.0), axis=0, keepdims=True)
            qdb = qf * t["bq"]
            dqdb = _mm_nt(dRn, sbb)
            dq += dqdb * t["bq"]
            lgb_row = jnp.sum(qdb * dqdb * (RC - t["pos"]), axis=0, keepdims=True)
            gb = gb_ref[...]
            gbb = gb.astype(CDT)
            kdb = kf * t["bk"]
            dkdb = _mm_nt(vn, gbb)
            dk += dkdb * t["bk"]
            dv += _mm(kdb.astype(CDT), gbb)
            lgb_row += jnp.sum(kdb * dkdb * t["pos"], axis=0, keepdims=True)
            lgb_row += float(RC) * t["lamb"] * jnp.sum(gb * sbb.astype(f32), axis=0, keepdims=True)
            co, si = cos_ref[rows, :], sin_ref[rows, :]
            dp_ref[0, rows, :] = _unrotate(dq, co, si).astype(CDT)
            emit_kv(rows, dk, dv, final)
            acc_ref[0:1, :] += lgf_row + lgf_part
            acc_ref[1:2, :] += lgb_row + lgb_part
            gb_ref[...] = gb * t["lamb"] + _mm_tn(qdb.astype(CDT), dRn)

        def descend(n, final):
            rows, qn, kn, vn, dRn = chunk(n)
            gf = gf_ref[...]
            gfb = gf.astype(CDT)
            kdf = kn.astype(f32) * t["fk"]
            dkdf = _mm_nt(vn, gfb)
            lgf_row = jnp.sum(kdf * dkdf * (RC - 1.0 - t["pos"]), axis=0, keepdims=True)
            lgf_row += float(RC) * t["lamf"] * jnp.sum(gf * sf_ref[n].astype(f32), axis=0, keepdims=True)
            acc_ref[0:1, :] += lgf_row
            emit_kv(rows, dkdf * t["fk"], _mm(kdf.astype(CDT), gfb), final)
            gf_ref[...] = gf * t["lamf"] + _mm_tn((qn.astype(f32) * t["fq"]).astype(CDT), dRn)

        def sweep(final):
            def step(i, carry):
                ascend(i, final)
                descend(NC - 1 - i, final)
                return carry
            return step

        lax.fori_loop(0, NC // 2, sweep(False), 0)
        lax.fori_loop(NC // 2, NC, sweep(True), 0)
        dlg = jnp.sum(acc_ref[...], axis=1, keepdims=True)
        dlogit = dlg * jax.nn.sigmoid(-dl_ref[:, 0:1])
        lane = lax.broadcasted_iota(jnp.int32, (8, 128), 1)
        dd_ref[...] = jnp.where(lane == pl.program_id(0), jnp.broadcast_to(dlogit, (8, 128)), 0.0)

    def seg(slot):
        return pl.BlockSpec((None, S, DK), lambda h: (slot, 0, h))

    head = pl.BlockSpec((S, DK), lambda h: (0, h))
    states = pl.BlockSpec((None, NC, DK, DK), lambda h: (h, 0, 0, 0))
    return _call(
        body, hooks, [dr, R, sfs, sbs, proj, proj, proj, proj, dproj, cos, sin, dl], name="ret_bwd", grid=(H,),
        out_shape=[jax.ShapeDtypeStruct(dproj.shape, CDT), jax.ShapeDtypeStruct((H, 8, 128), f32)],
        in_specs=[head, head, states, states, seg(SLOT_Q), seg(SLOT_K), seg(SLOT_VR), seg(SLOT_GR), _HBM,
                  _resident((S, DK // 2), lambda h: (0, 0)), _resident((S, DK // 2), lambda h: (0, 0)),
                  pl.BlockSpec((None, 8, DK), lambda h: (h, 0, 0))],
        out_specs=[pl.BlockSpec((4, S, DK), lambda h: (1, 0, h), pipeline_mode=pl.Buffered(1)),
                   pl.BlockSpec((None, 8, 128), lambda h: (h, 0, 0))],
        scratch_shapes=[pltpu.VMEM((S, DK), CDT),
                        pltpu.VMEM((DK, DK), f32), pltpu.VMEM((DK, DK), f32), pltpu.VMEM((8, DK), f32)],
        input_output_aliases={8: 0},
        compiler_params=_params("arbitrary"),
    )


def _proj_bwd_act(dproj, dx2, x1, nrm, wfull, hooks=(), tm=512):
    S = x1.shape[0]

    def body(dp_ref, dx2_ref, x_ref, n_ref, w_ref, dx_ref, dxh_ref, dn_ref, db_ref, acc_ref):
        @pl.when(pl.program_id(0) == 0)
        def _():
            dn_ref[...] = jnp.zeros_like(dn_ref)
            db_ref[...] = jnp.zeros_like(db_ref)

        for p in range(8):
            seg = SEG_OF_SLOT[p]
            dp = dp_ref[p]
            db_ref[seg] += jnp.sum(dp.astype(f32), axis=0, keepdims=True)
            dh = _mm_nt(dp, w_ref[seg * D:(seg + 1) * D, :])
            if p == 0:
                acc_ref[...] = dh
            else:
                acc_ref[...] += dh
        n = n_ref[...]
        r, xh, _ = _rms_fwd(x_ref[...], n)
        dx, dn = _rms_bwd(acc_ref[...], r, xh, n)
        dx = dx2_ref[...] + dx
        dx_ref[...] = dx
        dxh_ref[...] = (0.5 * dx).astype(CDT)
        dn_ref[...] += dn

    tok = pl.BlockSpec((tm, D), lambda i: (i, 0))
    return _call(
        body, hooks, [dproj, dx2, x1, nrm, wfull], name="proj_bwd_act", grid=(S // tm,),
        out_shape=[jax.ShapeDtypeStruct((S, D), f32), jax.ShapeDtypeStruct((S, D), CDT), jax.ShapeDtypeStruct((1, D), f32),
                   jax.ShapeDtypeStruct((8, 1, D), f32)],
        in_specs=[pl.BlockSpec((8, tm, D), lambda i: (0, i, 0)), tok, tok, _resident((1, D), lambda i: (0, 0)),
                  _resident((8 * D, D), lambda i: (0, 0))],
        out_specs=[tok, tok, pl.BlockSpec((1, D), lambda i: (0, 0)), pl.BlockSpec((8, 1, D), lambda i: (0, 0, 0))],
        scratch_shapes=[pltpu.VMEM((tm, D), f32)],
        compiler_params=_params("arbitrary"),
    )


def _rs_sum(name, gfulls, lands, my_c):
    n = len(gfulls)
    rows = gfulls[0].shape[0] // NDEV
    assert all(g.shape[0] == NDEV * rows for g in gfulls)

    def body(c_ref, *refs):
        for g_ref, l_ref, o_ref in zip(refs[:n], refs[n:2 * n], refs[2 * n:]):
            o_ref[...] = (g_ref[...].astype(f32) + l_ref[...].astype(f32)).astype(CDT)

    slot = pl.BlockSpec((None, rows, D), lambda k, c: (k, 0, 0))
    return pl.pallas_call(
        body, name=name,
        grid_spec=pltpu.PrefetchScalarGridSpec(
            num_scalar_prefetch=1, grid=(NCHIP,),
            in_specs=[pl.BlockSpec((rows, D), lambda k, c: (2 * k + c[0], 0))] * n + [slot] * n,
            out_specs=[slot] * n),
        out_shape=[jax.ShapeDtypeStruct((NCHIP, rows, D), CDT)] * n,
        compiler_params=_params("arbitrary"),
    )(my_c, *gfulls, *lands)


def _adamw_math(g, w, m, v):
    m2 = ADAM_B1 * m + (1.0 - ADAM_B1) * g
    v2 = ADAM_B2 * v + (1.0 - ADAM_B2) * (g * g)
    delta = -ADAM_LR * ((m2 / BC1) / (jnp.sqrt(v2 / BC2) + ADAM_EPS) + ADAM_WD * w)
    return delta, m2, v2


def _adamw_big(name, landed, w, m, v, after):
    rows = w.shape[0]
    tr = min(rows, 256) if rows % 256 == 0 else rows

    def body(l_ref, w_ref, m_ref, v_ref, after_ref, g_ref, d_ref, m2_ref, v2_ref):
        g = l_ref[0].astype(f32)
        for k in range(1, NCHIP):
            g = g + l_ref[k].astype(f32)
        g_ref[...] = g
        d_ref[...], m2_ref[...], v2_ref[...] = _adamw_math(g, w_ref[...], m_ref[...], v_ref[...])

    blk = pl.BlockSpec((tr, D), lambda i: (i, 0))
    o = jax.ShapeDtypeStruct((rows, D), f32)
    return pl.pallas_call(
        body, name=name, grid=(rows // tr,), out_shape=[o, o, o, o],
        in_specs=[pl.BlockSpec((NCHIP, tr, D), lambda i: (0, i, 0)), blk, blk, blk, _HBM],
        out_specs=[blk, blk, blk, blk],
        compiler_params=_params("arbitrary"),
    )(landed, w, m, v, after)


ROW_FFN1_NORM, ROW_MIX_NORM, ROW_SGU_G, ROW_SGU_B, ROW_FFN2_NORM, ROW_FINAL_NORM, ROW_B_IN = 0, 1, 2, 3, 4, 5, 8
ROW_WS, ROW_BS, ROW_DECAY = 0, G * C, G * C + G * 8


def _adamw_small(ga, gb, gn1, gl, params):
    def body(ga_ref, gb_ref, gn1_ref, gl_ref, *refs):
        ins, outs = refs[:30], refs[30:]

        def total(ref, r0, n):
            g = ref[0, r0:r0 + n, :]
            for j in range(1, NDEV):
                g = g + ref[j, r0:r0 + n, :]
            return g

        def apply(i, g, rows=slice(None)):
            w, m, v = ins[3 * i][rows, :], ins[3 * i + 1][rows, :], ins[3 * i + 2][rows, :]
            outs[4 * i][rows, :] = g
            outs[4 * i + 1][rows, :], outs[4 * i + 2][rows, :], outs[4 * i + 3][rows, :] = _adamw_math(g, w, m, v)

        outs[40][...] = total(gl_ref, 0, 8)
        apply(0, total(gn1_ref, 0, 1))
        for i, r in enumerate((ROW_FFN1_NORM, ROW_MIX_NORM, ROW_SGU_G, ROW_SGU_B, ROW_FFN2_NORM, ROW_FINAL_NORM)):
            if i:
                apply(i, total(ga_ref, r, 1))
        apply(6, total(ga_ref, ROW_B_IN, 8))
        apply(7, total(gb_ref, ROW_WS, G * C))
        for gi in range(G):
            apply(8, total(gb_ref, ROW_BS + 8 * gi, 1), slice(gi, gi + 1))
        dec = total(gb_ref, ROW_DECAY, 8)
        for hh in range(1, H):
            dec = dec + total(gb_ref, ROW_DECAY + 8 * hh, 8)
        apply(9, dec)

    flat = [a for p in params for a in p]
    out_shape = [jax.ShapeDtypeStruct(p[0].shape, f32) for p in params for _ in range(4)]
    out_shape.append(jax.ShapeDtypeStruct((8, 128), f32))
    vm = pl.BlockSpec(memory_space=pltpu.VMEM)
    return pl.pallas_call(
        body, name="adamw_small", out_shape=out_shape,
        in_specs=[vm] * (4 + len(flat)), out_specs=[vm] * len(out_shape),
        compiler_params=pltpu.CompilerParams(vmem_limit_bytes=VMEM_LIMIT),
    )(ga, gb, gn1, gl, *flat)


def kernel(x, ffn1_norm, ffn1_w_gate, ffn1_w_up, ffn1_w_down, mix_norm, w_in, b_in, sgu_norm_g, sgu_norm_b, sgu_w_s, sgu_b_s, ret_decay_logit, w_branch_a, w_branch_b, w_out, ffn2_norm, ffn2_w_gate, ffn2_w_up, ffn2_w_down, final_norm, loss_target, m_ffn1_norm, m_ffn1_w_gate, m_ffn1_w_up, m_ffn1_w_down, m_mix_norm, m_w_in, m_b_in, m_sgu_norm_g, m_sgu_norm_b, m_sgu_w_s, m_sgu_b_s, m_ret_decay_logit, m_w_branch_a, m_w_branch_b, m_w_out, m_ffn2_norm, m_ffn2_w_gate, m_ffn2_w_up, m_ffn2_w_down, m_final_norm, v_ffn1_norm, v_ffn1_w_gate, v_ffn1_w_up, v_ffn1_w_down, v_mix_norm, v_w_in, v_b_in, v_sgu_norm_g, v_sgu_norm_b, v_sgu_w_s, v_sgu_b_s, v_ret_decay_logit, v_w_branch_a, v_w_branch_b, v_w_out, v_ffn2_norm, v_ffn2_w_gate, v_ffn2_w_up, v_ffn2_w_down, v_final_norm):
    args = dict(locals())
    S = x.shape[1]
    xs = x[0]
    target = loss_target[0]

    def buf_layout(name, a):
        a = a[0]
        return a.T if name in W_TRANSPOSED else a

    sh = {n: buf_layout(n, args[n]).astype(CDT) for n in W_NAMES}
    wf = {}

    b3 = b_in.reshape(8, 1, D)
    ws = sgu_w_s[0].astype(CDT)
    wst = jnp.swapaxes(sgu_w_s[0], 1, 2).astype(CDT)
    bsc = sgu_b_s[0].reshape(G, C, 1)
    dl = jnp.zeros((H, 8, DK), f32).at[:, 0:2, :].set(jnp.broadcast_to(ret_decay_logit[0].T[:, :, None], (H, 2, DK)))
    theta = ROPE_BASE ** (-jnp.arange(0, DK, 2, dtype=f32) / DK)
    ang = jnp.arange(S, dtype=f32)[:, None] * theta[None, :]
    cos, sin = jnp.cos(ang), jnp.sin(ang)
    fnorm = final_norm.reshape(1, D)

    f1 = ("ffn1_w_gate", "ffn1_w_up", "ffn1_w_down")
    f2 = ("ffn2_w_gate", "ffn2_w_up", "ffn2_w_down")
    br = ("w_branch_a", "w_branch_b", "w_out")
    for cid, names in enumerate((f1, ("w_in",), br, f2)):
        wf.update(zip(names, _sequence("ag_" + names[0], 1 + cid, SIBLING_AND_CHIPS, [_ag_hook(sh[n]) for n in names])))
    x1, g1, u1, a1, hf1 = _ffn_fwd("ffn1_fwd", xs, ffn1_norm, *[wf[n] for n in f1])
    proj, h2 = _proj_fwd(x1, mix_norm, wf["w_in"], b3, cos, sin)
    a = _sgu_fwd(proj, sgu_norm_g, sgu_norm_b, ws, bsc)
    R, r, sfs, sbs = _ret_fwd(proj, dl)
    x2, ya, yb = _merge_fwd(a, r, proj, x1, *[wf[n] for n in br])
    x3, g2, u2, a2, hf2 = _ffn_fwd("ffn2_fwd", x2, ffn2_norm, *[wf[n] for n in f2])
    dx3, dyh2, d_final, loss_part = _loss_head(x3, fnorm, target)

    my_c = lax.axis_index("c").astype(jnp.int32).reshape(1)
    gw, landed, sequenced = {}, {}, []

    def d2d(*names):
        return [_rs_d2d_hook(gw[n]) for n in names]

    def behind(x, token):
        return lax.optimization_barrier((x, token))[0]

    def to_chips(names, sibs, more=()):
        parts = list(_rs_sum("rs_sum_" + names[0], [gw[n] for n in names], list(sibs), my_c))
        token = parts[0]
        if sequenced:
            parts[0] = behind(parts[0], sequenced[-1])
        hooks = [_rs_ici_hook(p) for p in parts] + ([_small_hook(list(more))] if more else [])
        reach = EVERYONE if more else CHIPS
        got = _sequence("rs_chips_" + names[0], REACH_ID[reach], reach, hooks)
        sequenced.append(got[0])
        landed.update(zip(names, got))
        return got[len(names):], token

    def ffn_bwd(tag, names, dy, dyh, x, g, u, a, h, nrm, each_alone, more=()):
        wg, wu, wd = names
        (gw[wd],) = _wgrad(tag + "_wd_grad", a, dyh)
        dg, du, sib_d = _ffn_bwd_hidden(tag + "_bwd_hidden", dyh, g, u, wf[wd], d2d(wd))
        if each_alone:
            dg = behind(dg, to_chips([wd], [sib_d])[1])
        (gw[wg],) = _wgrad(tag + "_wg_grad", dg, h)
        gw[wu], sib_g = _wgrad(tag + "_wu_grad", du, h, d2d(wg))
        if each_alone:
            dy = behind(dy, to_chips([wg], [sib_g])[1])
        dx, dn, sib_u = _ffn_bwd_in(tag + "_bwd_in", dy, x, dg, du, nrm, wf[wg], wf[wu], d2d(wu))
        if each_alone:
            return (dx, dn) + to_chips([wu], [sib_u], more(dn))
        return (dx, dn) + to_chips([wd, wg, wu], [sib_d, sib_g, sib_u])

    dx2, d_ffn2n, _, token = ffn_bwd("ffn2", f2, dx3, dyh2, x2, g2, u2, a2, hf2, ffn2_norm, False)
    dproj, da, dr, mix, dx2b, dya, dyb = _merge_bwd_act(behind(dx2, token), ya, yb, proj, *[wf[n] for n in br])
    dproj, d_ws, d_bs, d_gn, d_bn = _sgu_bwd(da, proj, dproj, sgu_norm_g, sgu_norm_b, ws, wst, bsc)
    dproj, d_dec = _ret_bwd(dr, R, sfs, sbs, proj, dproj, cos, sin, dl)
    (gw["w_in"],) = _tn("win_grad", h2, dproj, _seg_of_slot)
    gw["w_out"], sib_win = _wgrad("wo_grad", mix, dx2b, d2d("w_in"))
    small_sgu = jnp.concatenate([d_ws.reshape(G * C, C), d_bs.reshape(G * 8, C)], axis=0)
    (g_sgu, gl), token = to_chips(["w_in"], [sib_win], [small_sgu, loss_part])
    (gw["w_branch_a"],) = _wgrad("wa_grad", a, behind(dya, token))
    (gw["w_branch_b"],) = _wgrad("wb_grad", r, dyb)
    dx1, dyh1, d_mixn, d_bin, *sib_br = _proj_bwd_act(behind(dproj, token), dx2, x1, mix_norm, wf["w_in"], d2d(*br))
    small_a = jnp.concatenate([jnp.zeros((1, D), f32), d_mixn, d_gn, d_bn, d_ffn2n, d_final, jnp.zeros((2, D), f32),
                               d_bin.reshape(8, D)], axis=0)
    (ga, g_dec), token = to_chips(list(br), sib_br, [small_a, d_dec.reshape(H * 8, 128)])
    gb = jnp.concatenate([g_sgu, g_dec], axis=1)
    dxs, d_ffn1n, (gn1,), token = ffn_bwd("ffn1", f1, dx1, behind(dyh1, token), xs, g1, u1, a1, hf1, ffn1_norm, True,
                                          lambda dn: [dn])

    out = {"grad_x": dxs[None]}

    def native(name, a):
        a = a.T if name in W_TRANSPOSED else a
        return a[None]

    after = token
    for n in f2 + ("w_in",) + br + (f1[2], f1[0], f1[1]):
        res = _adamw_big("adamw_" + n, landed[n], buf_layout(n, args[n]), buf_layout(n, args["m_" + n]),
                         buf_layout(n, args["v_" + n]), after)
        after = res[0]
        for pre, val in zip(("grad_", "delta_", "new_m_", "new_v_"), res):
            out[pre + n] = native(n, val)

    def pad_decay(a):
        return jnp.zeros((8, 128), f32).at[0:2, 0:H].set(a[0])

    small = [
        ("ffn1_norm", lambda a: a, lambda a: a), ("mix_norm", lambda a: a, lambda a: a),
        ("sgu_norm_g", lambda a: a, lambda a: a), ("sgu_norm_b", lambda a: a, lambda a: a),
        ("ffn2_norm", lambda a: a, lambda a: a),
        ("final_norm", lambda a: a.reshape(1, D), lambda a: a.reshape(D)),
        ("b_in", lambda a: a.reshape(8, D), lambda a: a.reshape(1, 8 * D)),
        ("sgu_w_s", lambda a: a.reshape(G * C, C), lambda a: a.reshape(1, G, C, C)),
        ("sgu_b_s", lambda a: a[0], lambda a: a[None]),
        ("ret_decay_logit", pad_decay, lambda a: a[None, 0:2, 0:H]),
    ]
    res = _adamw_small(ga, gb, gn1, gl, [(to(args[n]), to(args["m_" + n]), to(args["v_" + n])) for n, to, _ in small])
    out["loss"] = res[40][0, 0]
    for i, (n, _, back) in enumerate(small):
        for j, pre in enumerate(("grad_", "delta_", "new_m_", "new_v_")):
            out[pre + n] = back(res[4 * i + j])

    weights = ("ffn1_norm", "ffn1_w_gate", "ffn1_w_up", "ffn1_w_down", "mix_norm", "w_in", "b_in", "sgu_norm_g",
               "sgu_norm_b", "sgu_w_s", "sgu_b_s", "ret_decay_logit", "w_branch_a", "w_branch_b", "w_out", "ffn2_norm",
               "ffn2_w_gate", "ffn2_w_up", "ffn2_w_down", "final_norm")
    return (out["loss"], out["grad_x"], *[out["grad_" + n] for n in weights], *[out["delta_" + n] for n in weights],
            *[out["new_m_" + n] for n in weights], *[out["new_v_" + n] for n in weights])
```

```python
import functools
import math

import jax
import jax.numpy as jnp
from jax import lax
from jax.experimental import pallas as pl
from jax.experimental.pallas import tpu as pltpu
from jax.experimental.pallas import tpu_sc as plsc

f32 = jnp.float32
CDT = jnp.bfloat16

D = 1024
F = 2816
C = 128
RC = 256
H = 4
DK = 256
G = 4
NDEV = 8
NCHIP = 4
EPS = 1e-6
ROPE_BASE = 10000.0
FT = 256
V7X_VMEM_BYTES = 64 * 1024 * 1024
VMEM_LIMIT = V7X_VMEM_BYTES - 8 * 1024 * 1024

ADAM_LR, ADAM_B1, ADAM_B2, ADAM_EPS, ADAM_WD, ADAM_STEP = 0.001, 0.9, 0.999, 1e-08, 0.01, 10
BC1 = 1.0 - ADAM_B1 ** ADAM_STEP
BC2 = 1.0 - ADAM_B2 ** ADAM_STEP

W_ROWS = dict(ffn1_w_gate=352, ffn1_w_up=352, ffn1_w_down=352, w_in=1024, w_branch_a=128, w_branch_b=128, w_out=128,
              ffn2_w_gate=352, ffn2_w_up=352, ffn2_w_down=352)
W_NAMES = tuple(W_ROWS)
W_TRANSPOSED = ("ffn1_w_gate", "ffn1_w_up", "ffn2_w_gate", "ffn2_w_up")

SLOT_U, SLOT_V, SLOT_GA, SLOT_GB, SLOT_Q, SLOT_K, SLOT_VR, SLOT_GR = range(8)


SEG_OF_SLOT = (0, 1, 6, 7, 2, 3, 4, 5)


def _seg_of_slot(p):
    return jnp.where(p < 2, p, jnp.where(p < 4, p + 4, p - 2))


def _mm(a, b):
    return jnp.dot(a, b, preferred_element_type=f32)


def _mm_nt(a, b):
    return lax.dot_general(a, b, (((1,), (1,)), ((), ())), preferred_element_type=f32)


def _mm_tn(a, b):
    return lax.dot_general(a, b, (((0,), (0,)), ((), ())), preferred_element_type=f32)


def _params(*sem):
    return pltpu.CompilerParams(dimension_semantics=sem, vmem_limit_bytes=VMEM_LIMIT)


def _resident(shape, index_map):
    return pl.BlockSpec(shape, index_map, pipeline_mode=pl.Buffered(1))


def _gelu(x):
    return 0.5 * x * (1.0 + lax.erf(x * (1.0 / math.sqrt(2.0))))


def _gelu_grad(x):
    return 0.5 * (1.0 + lax.erf(x * (1.0 / math.sqrt(2.0)))) + x * jnp.exp(-0.5 * x * x) * (1.0 / math.sqrt(2.0 * math.pi))


def _rms_fwd(x, n):
    r = lax.rsqrt(jnp.mean(x * x, axis=-1, keepdims=True) + EPS)
    xh = x * r
    return r, xh, xh * n


def _rms_bwd(dh, r, xh, n):
    dxh = dh * n
    dx = r * (dxh - xh * jnp.mean(dxh * xh, axis=-1, keepdims=True))
    return dx, jnp.sum(dh * xh, axis=0, keepdims=True)


MESH_ID = pl.DeviceIdType.MESH
_HBM = pl.BlockSpec(memory_space=pltpu.HBM)


def _my_place():
    return lax.axis_index("x"), lax.axis_index("y"), lax.axis_index("c")


def _ici_peers(x, y, c):
    return [((1 - x, y, c), 2 * (1 - x) + y), ((x, 1 - y, c), 2 * x + 1 - y), ((1 - x, 1 - y, c), 2 * (1 - x) + 1 - y)]


class _Hook:
    def __init__(self, operands, out_shapes, n_remote, n_local, start, finish, relay=None):
        self.operands, self.out_shapes = list(operands), list(out_shapes)
        self.n_remote, self.n_local, self.start, self.finish = n_remote, n_local, start, finish
        self.relay = relay or (lambda *a: None)


def _call(body, hooks, operands, *, in_specs, out_specs, out_shape, grid=None, scratch_shapes=(), **kw):
    hooks = tuple(hooks)
    n_in, n_out, n_scr = len(in_specs), len(out_shape), len(scratch_shapes)
    h_ops = [a for h in hooks for a in h.operands]
    h_outs = [s for h in hooks for s in h.out_shapes]
    h_sems = [pltpu.SemaphoreType.DMA((n,)) for h in hooks for n in (h.n_remote, h.n_remote, max(h.n_local, 1))]

    def wrapped(*refs):
        ins, hin = refs[:n_in], refs[n_in:n_in + len(h_ops)]
        o0 = n_in + len(h_ops)
        outs, hout = refs[o0:o0 + n_out], refs[o0 + n_out:o0 + n_out + len(h_outs)]
        s0 = o0 + n_out + len(h_outs)
        scr, hsem = refs[s0:s0 + n_scr], refs[s0 + n_scr:]

        def run(phase):
            ip = op = 0
            for i, h in enumerate(hooks):
                ssem, rsem, lsem = hsem[3 * i:3 * i + 3]

                def rcopy(k, src, dst, dev, ssem=ssem, rsem=rsem):
                    return pltpu.make_async_remote_copy(src_ref=src, dst_ref=dst, send_sem=ssem.at[k], recv_sem=rsem.at[k],
                                                        device_id=dev, device_id_type=MESH_ID)

                def lcopy(k, src, dst, lsem=lsem):
                    return pltpu.make_async_copy(src, dst, lsem.at[k])

                getattr(h, phase)(hin[ip:ip + len(h.operands)], hout[op:op + len(h.out_shapes)], rcopy, lcopy)
                ip += len(h.operands)
                op += len(h.out_shapes)

        def at_edge(phase, last):
            if not hooks:
                return
            if grid is None:
                run(phase)
                return
            cond = None
            for ax, n in enumerate(grid):
                here = pl.program_id(ax) == (n - 1 if last else 0)
                cond = here if cond is None else cond & here
            pl.when(cond)(lambda: run(phase))

        at_edge("start", False)
        at_edge("relay", True)
        body(*ins, *outs, *scr)
        at_edge("finish", True)

    if grid is not None:
        kw["grid"] = grid
    return list(pl.pallas_call(
        wrapped, out_shape=list(out_shape) + h_outs, in_specs=list(in_specs) + [_HBM] * len(h_ops),
        out_specs=list(out_specs) + [_HBM] * len(h_outs), scratch_shapes=list(scratch_shapes) + h_sems, **kw,
    )(*operands, *h_ops))


def _exchange(name, hooks):
    return _call(lambda: None, hooks, [], name=name, in_specs=[], out_specs=[], out_shape=[])


def _rows(ref, start, n):
    return ref.at[pl.ds(start, n), :]


def _ag_hook(shard, early=True):
    rows = shard.shape[0]
    half = rows // 2
    assert half % 16 == 0

    def place():
        x, y, c = _my_place()
        devs = dict(sib=(x, y, 1 - c), xn=(1 - x, y, c), yn=(x, 1 - y, c))
        chips = dict(me=2 * x + y, xn=2 * (1 - x) + y, yn=2 * x + 1 - y, dg=2 * (1 - x) + 1 - y)
        return c, devs, chips

    def block(full, chip, c):
        return _rows(full, (2 * chip + c) * rows, rows)

    def halfblock(full, chip, c, upper):
        return _rows(full, (2 * chip + c) * rows + upper * half, half)

    def start(ins, outs, rcopy, lcopy):
        c, devs, chips = place()
        src, dst = ins[0], block(outs[0], chips["me"], c)
        lcopy(0, src, dst).start()
        for k, to in enumerate(("sib", "xn", "yn")):
            rcopy(k, src, dst, devs[to]).start()

    def relay(ins, outs, rcopy, lcopy):
        c, devs, chips = place()
        full = outs[0]
        blk = block(full, chips["xn"], c)
        rcopy(1, blk, blk, devs["xn"]).wait_recv()
        low = halfblock(full, chips["xn"], c, 0)
        rcopy(3, low, low, devs["yn"]).start()
        rcopy(5, blk, blk, devs["sib"]).start()
        blk = block(full, chips["yn"], c)
        rcopy(2, blk, blk, devs["yn"]).wait_recv()
        up = halfblock(full, chips["yn"], c, 1)
        rcopy(4, up, up, devs["xn"]).start()
        rcopy(6, blk, blk, devs["sib"]).start()
        low, up = halfblock(full, chips["dg"], c, 0), halfblock(full, chips["dg"], c, 1)
        rcopy(3, low, low, devs["yn"]).wait_recv()
        rcopy(4, up, up, devs["xn"]).wait_recv()
        blk = block(full, chips["dg"], c)
        rcopy(7, blk, blk, devs["sib"]).start()

    def finish(ins, outs, rcopy, lcopy):
        if not early:
            relay(ins, outs, rcopy, lcopy)
        c, devs, chips = place()
        full, sib = outs[0], devs["sib"]
        for k, chip in ((0, "me"), (5, "xn"), (6, "yn"), (7, "dg")):
            theirs = block(full, chips[chip], 1 - c)
            rcopy(k, theirs, theirs, sib).wait_recv()
            mine = block(full, chips[chip], c)
            if k:
                rcopy(k, mine, mine, sib).wait_send()
        src, dst = ins[0], block(full, chips["me"], c)
        lcopy(0, src, dst).wait()
        for k, to in enumerate(("sib", "xn", "yn")):
            rcopy(k, src, dst, devs[to]).wait_send()
        low, up = halfblock(full, chips["xn"], c, 0), halfblock(full, chips["yn"], c, 1)
        rcopy(3, low, low, devs["yn"]).wait_send()
        rcopy(4, up, up, devs["xn"]).wait_send()

    return _Hook([shard], [jax.ShapeDtypeStruct((NDEV * rows, D), shard.dtype)], 8, 1, start, finish, relay if early else None)


SIBLING, CHIPS, NEIGHBOURS, EVERYONE = "sibling", "chips", "sibling and the two neighbour chips", "everyone"
REACH_ID = {SIBLING: 5, CHIPS: 6, EVERYONE: 7}


def _sequence(name, collective_id, reach, hooks):
    ins = [[jax.new_ref(a, memory_space=pltpu.MemorySpace.HBM) for a in h.operands] for h in hooks]
    outs = [[jax.empty_ref(s, memory_space=pltpu.MemorySpace.HBM) for s in h.out_shapes] for h in hooks]
    sems = tuple(pltpu.SemaphoreType.DMA((n,)) for h in hooks for n in (h.n_remote, h.n_remote, max(h.n_local, 1)))

    @pl.kernel(mesh=plsc.ScalarSubcoreMesh(axis_name="sequencer", num_cores=1), name=name, scratch_types=sems,
               compiler_params=pltpu.CompilerParams(collective_id=collective_id))
    def launch(*sem_refs):
        x, y, c = _my_place()
        chips = [dev for dev, _ in _ici_peers(x, y, c)]
        others = [(1 - x if dx else x, 1 - y if dy else y, 1 - c if dc else c)
                  for dx in range(2) for dy in range(2) for dc in range(2) if dx + dy + dc]
        devs = {SIBLING: [(x, y, 1 - c)], CHIPS: chips, NEIGHBOURS: [(x, y, 1 - c), (1 - x, y, c), (x, 1 - y, c)],
                EVERYONE: others}[reach]
        barrier = pltpu.get_barrier_semaphore()
        for dev in devs:
            pl.semaphore_signal(barrier, inc=1, device_id=dev, device_id_type=MESH_ID)
        pl.semaphore_wait(barrier, len(devs))
        for phase in ("start", "relay", "finish"):
            for i, h in enumerate(hooks):
                ssem, rsem, lsem = sem_refs[3 * i:3 * i + 3]

                def rcopy(k, src, dst, dev, ssem=ssem, rsem=rsem):
                    return pltpu.make_async_remote_copy(src_ref=src, dst_ref=dst, send_sem=ssem.at[k], recv_sem=rsem.at[k],
                                                        device_id=dev, device_id_type=MESH_ID)

                def lcopy(k, src, dst, lsem=lsem):
                    return pltpu.make_async_copy(src, dst, lsem.at[k])

                getattr(h, phase)(ins[i], outs[i], rcopy, lcopy)

    launch()
    return [o[...] for os in outs for o in os]


def _rs_d2d_hook(gfull):
    rows = gfull.shape[0] // NDEV

    def pairs(g, land):
        x, y, c = _my_place()
        return (x, y, 1 - c), [(k, _rows(g, (2 * k + 1 - c) * rows, rows), land.at[k]) for k in range(NCHIP)]

    def start(ins, outs, rcopy, lcopy):
        sib, cps = pairs(ins[0], outs[0])
        for i, src, dst in cps:
            rcopy(i, src, dst, sib).start()

    def finish(ins, outs, rcopy, lcopy):
        sib, cps = pairs(ins[0], outs[0])
        for i, src, dst in cps:
            rcopy(i, dst, dst, sib).wait_recv()
        for i, src, dst in cps:
            rcopy(i, src, dst, sib).wait_send()

    return _Hook([gfull], [jax.ShapeDtypeStruct((NCHIP, rows, D), gfull.dtype)], NCHIP, 0, start, finish)


def _rs_ici_hook(part):
    def start(ins, outs, rcopy, lcopy):
        x, y, c = _my_place()
        mychip = 2 * x + y
        lcopy(0, ins[0].at[mychip], outs[0].at[mychip]).start()
        for j, (dev, chip) in enumerate(_ici_peers(x, y, c)):
            rcopy(j, ins[0].at[chip], outs[0].at[mychip], dev).start()

    def finish(ins, outs, rcopy, lcopy):
        x, y, c = _my_place()
        mychip = 2 * x + y
        peers = _ici_peers(x, y, c)
        for j, (dev, chip) in enumerate(peers):
            rcopy(j, outs[0].at[chip], outs[0].at[chip], dev).wait_recv()
        for j, (dev, chip) in enumerate(peers):
            rcopy(j, ins[0].at[chip], outs[0].at[mychip], dev).wait_send()
        lcopy(0, ins[0].at[mychip], outs[0].at[mychip]).wait()

    return _Hook([part], [jax.ShapeDtypeStruct(part.shape, part.dtype)], 3, 1, start, finish)


def _small_hook(arrays):
    n = len(arrays)

    def peers():
        x, y, c = _my_place()
        out = []
        for dx in range(2):
            for dy in range(2):
                for dc in range(2):
                    if dx + dy + dc:
                        px, py, pc = (1 - x if dx else x), (1 - y if dy else y), (1 - c if dc else c)
                        out.append(((px, py, pc), 4 * px + 2 * py + pc))
        return 4 * x + 2 * y + c, out

    def start(ins, outs, rcopy, lcopy):
        me, ps = peers()
        for t in range(n):
            lcopy(t, ins[t], outs[t].at[me]).start()
            for i, (dev, _) in enumerate(ps):
                rcopy(n * i + t, ins[t], outs[t].at[me], dev).start()

    def finish(ins, outs, rcopy, lcopy):
        me, ps = peers()
        for t in range(n):
            for i, (dev, peer) in enumerate(ps):
                rcopy(n * i + t, outs[t].at[peer], outs[t].at[peer], dev).wait_recv()
            for i, (dev, _) in enumerate(ps):
                rcopy(n * i + t, ins[t], outs[t].at[me], dev).wait_send()
            lcopy(t, ins[t], outs[t].at[me]).wait()

    return _Hook(arrays, [jax.ShapeDtypeStruct((NDEV,) + a.shape, a.dtype) for a in arrays], 7 * n, n, start, finish)


def _wblock(w):
    return _resident(w.shape, lambda *_: (0, 0))


def _ffn_fwd(name, x, nrm, wg, wu, wd, hooks=(), tm=512):
    S = x.shape[0]

    def body(x_ref, n_ref, wg_ref, wu_ref, wd_ref, y_ref, g_ref, u_ref, a_ref, h_ref, acc_ref):
        xv = x_ref[...]
        _, _, h = _rms_fwd(xv, n_ref[...])
        h = h.astype(CDT)
        h_ref[...] = h
        for ci in range(F // FT):
            sl = slice(ci * FT, (ci + 1) * FT)
            g = _mm_nt(h, wg_ref[sl, :])
            u = _mm_nt(h, wu_ref[sl, :])
            g_ref[:, sl] = g.astype(CDT)
            u_ref[:, sl] = u.astype(CDT)
            a = (g * jax.nn.sigmoid(g) * u).astype(CDT)
            a_ref[:, sl] = a
            o = _mm(a, wd_ref[sl, :])
            if ci == 0:
                acc_ref[...] = o
            else:
                acc_ref[...] += o
        y_ref[...] = xv + 0.5 * acc_ref[...]

    tok = pl.BlockSpec((tm, D), lambda i: (i, 0))
    hid = pl.BlockSpec((tm, F), lambda i: (i, 0))
    hidden = jax.ShapeDtypeStruct((S, F), CDT)
    return _call(
        body, hooks, [x, nrm, wg, wu, wd], name=name, grid=(S // tm,),
        out_shape=[jax.ShapeDtypeStruct((S, D), f32), hidden, hidden, hidden, jax.ShapeDtypeStruct((S, D), CDT)],
        in_specs=[tok, _resident((1, D), lambda i: (0, 0)), _wblock(wg), _wblock(wu), _wblock(wd)],
        out_specs=[tok, hid, hid, hid, tok],
        scratch_shapes=[pltpu.VMEM((tm, D), f32)],
        compiler_params=_params("arbitrary"),
    )


def _proj_fwd(x1, nrm, wfull, b3, cos, sin, hooks=(), tm=512):
    S = x1.shape[0]

    def body(x_ref, n_ref, w_ref, b_ref, cos_ref, sin_ref, p_ref, h_ref):
        _, _, h = _rms_fwd(x_ref[...], n_ref[...])
        h = h.astype(CDT)
        h_ref[...] = h
        for p in range(8):
            seg = SEG_OF_SLOT[p]
            z = _mm(h, w_ref[seg * D:(seg + 1) * D, :]) + b_ref[seg]
            if p in (SLOT_Q, SLOT_K):
                co, si = cos_ref[...], sin_ref[...]
                for hh in range(H):
                    cs = slice(hh * DK, (hh + 1) * DK)
                    zr = _rotate(z[:, cs], co, si)
                    p_ref[p, :, cs] = (zr * K_SCALE if p == SLOT_K else zr).astype(CDT)
            else:
                p_ref[p] = z.astype(CDT)

    tab = pl.BlockSpec((tm, DK // 2), lambda i: (i, 0))
    return _call(
        body, hooks, [x1, nrm, wfull, b3, cos, sin], name="proj_fwd", grid=(S // tm,),
        out_shape=[jax.ShapeDtypeStruct((8, S, D), CDT), jax.ShapeDtypeStruct((S, D), CDT)],
        in_specs=[pl.BlockSpec((tm, D), lambda i: (i, 0)), _resident((1, D), lambda i: (0, 0)),
                  _resident((8 * D, D), lambda i: (0, 0)), _resident((8, 1, D), lambda i: (0, 0, 0)), tab, tab],
        out_specs=[pl.BlockSpec((8, tm, D), lambda i: (0, i, 0)), pl.BlockSpec((tm, D), lambda i: (i, 0))],
        compiler_params=_params("arbitrary"),
    )


def _sgu_norm(va, gn, bn):
    mu = jnp.mean(va, axis=-1, keepdims=True)
    xc = va - mu
    rstd = lax.rsqrt(jnp.mean(xc * xc, axis=-1, keepdims=True) + EPS)
    vhat = xc * rstd
    return rstd, vhat, vhat * gn + bn


def _sgu_fwd(proj, gn, bn, ws, bsc, tm=512):
    S = proj.shape[1]
    GW = D // G

    def body(p_ref, gn_ref, bn_ref, ws_ref, bs_ref, a_ref):
        ua = _gelu(p_ref[0].astype(f32))
        va = _gelu(p_ref[1].astype(f32))
        _, _, vn = _sgu_norm(va, gn_ref[...], bn_ref[...])
        vn = vn.astype(CDT)
        for ch in range(tm // C):
            rs = slice(ch * C, (ch + 1) * C)
            for gi in range(G):
                cs = slice(gi * GW, (gi + 1) * GW)
                s = _mm(ws_ref[gi], vn[rs, cs]) + bs_ref[gi]
                a_ref[rs, cs] = (ua[rs, cs] * s).astype(CDT)

    return pl.pallas_call(
        body, name="sgu_fwd", grid=(S // tm,),
        out_shape=jax.ShapeDtypeStruct((S, D), CDT),
        in_specs=[pl.BlockSpec((2, tm, D), lambda i: (0, i, 0)), _resident((1, D), lambda i: (0, 0)),
                  _resident((1, D), lambda i: (0, 0)), _resident((G, C, C), lambda i: (0, 0, 0)),
                  _resident((G, C, 1), lambda i: (0, 0, 0))],
        out_specs=pl.BlockSpec((tm, D), lambda i: (i, 0)),
        compiler_params=_params("arbitrary"),
    )(proj, gn, bn, ws, bsc)


def _decay_tables(dl_ref):
    lg = jax.nn.log_sigmoid(dl_ref[0:2, :])
    lgf, lgb = lg[0:1, :], lg[1:2, :]
    assert RC <= DK
    ri = lax.broadcasted_iota(jnp.int32, (RC, RC), 0)
    ci = lax.broadcasted_iota(jnp.int32, (RC, RC), 1)
    d = (ri - ci).astype(f32)
    lower = d >= 0
    dmat = jnp.where(lower, jnp.exp(d * lgf[:, :RC]), jnp.exp(-d * lgb[:, :RC]))
    dmat_t = jnp.where(d <= 0, jnp.exp(-d * lgf[:, :RC]), jnp.exp(d * lgb[:, :RC]))
    pos = lax.broadcasted_iota(jnp.int32, (RC, DK), 0).astype(f32)
    t = dict(
        lgf=lgf, lgb=lgb, d=d, lower=lower, dmat=dmat, dmat_t=dmat_t, pos=pos,
        fq=jnp.exp((pos + 1.0) * lgf), fk=jnp.exp((RC - 1.0 - pos) * lgf),
        bq=jnp.exp((RC - pos) * lgb), bk=jnp.exp(pos * lgb),
        lamf=jnp.exp(float(RC) * lgf), lamb=jnp.exp(float(RC) * lgb),
    )
    return t


def _rotate(t, co, si):
    t1, t2 = t[:, :DK // 2], t[:, DK // 2:]
    return jnp.concatenate([t1 * co - t2 * si, t2 * co + t1 * si], axis=-1)


def _unrotate(t, co, si):
    t1, t2 = t[:, :DK // 2], t[:, DK // 2:]
    return jnp.concatenate([t1 * co + t2 * si, t2 * co - t1 * si], axis=-1)


K_SCALE = DK ** -0.5
ROW_TILE = 256


def _ret_fwd(proj, dl, hooks=()):
    S = proj.shape[1]
    NC = S // RC

    def body(q_ref, k_ref, v_ref, g_ref, dl_ref, R_ref, r_ref, sfs_ref, sbs_ref, rb_ref, sf_ref, sb_ref):
        t = _decay_tables(dl_ref)

        def chunk(n):
            rows = pl.ds(pl.multiple_of(n * RC, RC), RC)
            return rows, q_ref[rows, :], k_ref[rows, :], v_ref[rows, :]

        sf_ref[...] = jnp.zeros_like(sf_ref)
        sb_ref[...] = jnp.zeros_like(sb_ref)

        def step(i, carry):
            rows, qn, kn, vn = chunk(i)
            sc = _mm_nt(qn, kn) * t["dmat"]
            out = _mm(sc.astype(CDT), vn)
            sf = sf_ref[...]
            sfb = sf.astype(CDT)
            sfs_ref[i] = sfb
            R_ref[rows, :] = out + _mm((qn.astype(f32) * t["fq"]).astype(CDT), sfb)
            sf_ref[...] = sf * t["lamf"] + _mm_tn((kn.astype(f32) * t["fk"]).astype(CDT), vn)
            m = NC - 1 - i
            rows, qn, kn, vn = chunk(m)
            sb = sb_ref[...]
            sbb = sb.astype(CDT)
            sbs_ref[m] = sbb
            rb_ref[rows, :] = _mm((qn.astype(f32) * t["bq"]).astype(CDT), sbb)
            sb_ref[...] = sb * t["lamb"] + _mm_tn((kn.astype(f32) * t["bk"]).astype(CDT), vn)
            return carry

        lax.fori_loop(0, NC, step, 0)

        def finish(i, carry):
            rs = pl.ds(pl.multiple_of(i * ROW_TILE, ROW_TILE), ROW_TILE)
            R = R_ref[rs, :] + rb_ref[rs, :]
            R_ref[rs, :] = R
            rn = R * lax.rsqrt(jnp.mean(R * R, axis=-1, keepdims=True) + EPS)
            g = g_ref[rs, :].astype(f32)
            r_ref[rs, :] = (rn * g * jax.nn.sigmoid(g)).astype(CDT)
            return carry

        lax.fori_loop(0, S // ROW_TILE, finish, 0)

    def seg(slot):
        return pl.BlockSpec((None, S, DK), lambda h: (slot, 0, h))

    states = jax.ShapeDtypeStruct((H, NC, DK, DK), CDT)
    state_blk = pl.BlockSpec((None, NC, DK, DK), lambda h: (h, 0, 0, 0))
    return _call(
        body, hooks, [proj, proj, proj, proj, dl], name="ret_fwd", grid=(H,),
        out_shape=[jax.ShapeDtypeStruct((S, H * DK), f32), jax.ShapeDtypeStruct((S, H * DK), CDT), states, states],
        in_specs=[seg(SLOT_Q), seg(SLOT_K), seg(SLOT_VR), seg(SLOT_GR), pl.BlockSpec((None, 8, DK), lambda h: (h, 0, 0))],
        out_specs=[pl.BlockSpec((S, DK), lambda h: (0, h)), pl.BlockSpec((S, DK), lambda h: (0, h)), state_blk, state_blk],
        scratch_shapes=[pltpu.VMEM((S, DK), f32), pltpu.VMEM((DK, DK), f32), pltpu.VMEM((DK, DK), f32)],
        compiler_params=_params("arbitrary"),
    )


def _merge_fwd(a, r, proj, x1, wa, wb, wo, hooks=(), tm=512):
    S = x1.shape[0]

    def body(a_ref, r_ref, gt_ref, x_ref, wa_ref, wb_ref, wo_ref, x2_ref, ya_ref, yb_ref):
        ya = _mm(a_ref[...], wa_ref[...])
        yb = _mm(r_ref[...], wb_ref[...])
        ya_ref[...] = ya.astype(CDT)
        yb_ref[...] = yb.astype(CDT)
        mix = jax.nn.sigmoid(gt_ref[0].astype(f32)) * ya + jax.nn.sigmoid(gt_ref[1].astype(f32)) * yb
        x2_ref[...] = x_ref[...] + _mm(mix.astype(CDT), wo_ref[...])

    tok = pl.BlockSpec((tm, D), lambda i: (i, 0))
    return _call(
        body, hooks, [a, r, proj, x1, wa, wb, wo], name="merge_fwd", grid=(S // tm,),
        out_shape=[jax.ShapeDtypeStruct((S, D), f32), jax.ShapeDtypeStruct((S, D), CDT), jax.ShapeDtypeStruct((S, D), CDT)],
        in_specs=[tok, tok, pl.BlockSpec((2, tm, D), lambda i: (SLOT_GA // 2, i, 0)), tok,
                  _wblock(wa), _wblock(wb), _wblock(wo)],
        out_specs=[tok, tok, tok],
        compiler_params=_params("arbitrary"),
    )


def _loss_head(x3, fn, target, tm=512):
    S = x3.shape[0]

    def body(x_ref, n_ref, t_ref, dx_ref, dxh_ref, dn_ref, l_ref):
        n = n_ref[...]
        r, xh, y = _rms_fwd(x_ref[...], n)
        e = y - t_ref[...]
        dy = e * (1.0 / D)
        dx, dn = _rms_bwd(dy, r, xh, n)
        dx_ref[...] = dx
        dxh_ref[...] = (0.5 * dx).astype(CDT)
        part = 0.5 * jnp.sum(jnp.sum(e * e, axis=-1, keepdims=True), axis=0, keepdims=True) * (1.0 / D)

        @pl.when(pl.program_id(0) == 0)
        def _():
            dn_ref[...] = jnp.zeros_like(dn_ref)
            l_ref[...] = jnp.zeros_like(l_ref)

        dn_ref[...] += dn
        l_ref[...] += jnp.broadcast_to(part, l_ref.shape)

    tok = pl.BlockSpec((tm, D), lambda i: (i, 0))
    return pl.pallas_call(
        body, name="loss_head", grid=(S // tm,),
        out_shape=[jax.ShapeDtypeStruct((S, D), f32), jax.ShapeDtypeStruct((S, D), CDT), jax.ShapeDtypeStruct((1, D), f32),
                   jax.ShapeDtypeStruct((8, 128), f32)],
        in_specs=[tok, _resident((1, D), lambda i: (0, 0)), tok],
        out_specs=[tok, tok, pl.BlockSpec((1, D), lambda i: (0, 0)), pl.BlockSpec((8, 128), lambda i: (0, 0))],
        compiler_params=_params("arbitrary"),
    )(x3, fn, target)


def _ffn_bwd_hidden(name, dyh, g, u, wd, hooks=(), tm=512):
    S = dyh.shape[0]

    def body(dyh_ref, g_ref, u_ref, wd_ref, dg_ref, du_ref):
        dyh = dyh_ref[...]
        for ci in range(F // FT):
            sl = slice(ci * FT, (ci + 1) * FT)
            da = _mm_nt(dyh, wd_ref[sl, :])
            gv = g_ref[:, sl].astype(f32)
            uv = u_ref[:, sl].astype(f32)
            s = jax.nn.sigmoid(gv)
            du_ref[:, sl] = (da * (gv * s)).astype(CDT)
            dg_ref[:, sl] = (da * uv * (s * (1.0 + gv * (1.0 - s)))).astype(CDT)

    hid = pl.BlockSpec((tm, F), lambda i: (i, 0))
    hidden = jax.ShapeDtypeStruct((S, F), CDT)
    return _call(
        body, hooks, [dyh, g, u, wd], name=name, grid=(S // tm,), out_shape=[hidden, hidden],
        in_specs=[pl.BlockSpec((tm, D), lambda i: (i, 0)), hid, hid, _wblock(wd)], out_specs=[hid, hid],
        compiler_params=_params("arbitrary"),
    )


def _ffn_bwd_in(name, dy, x, dg, du, nrm, wg, wu, hooks=(), part=None, prev=None, tm=512):
    S = x.shape[0]
    t0, nt = part or (0, S // tm)

    def body(dy_ref, x_ref, dg_ref, du_ref, n_ref, wg_ref, wu_ref, *rest):
        dx_ref, dn_ref, acc_ref = rest[-3:]
        n = n_ref[...]
        r, xh, _ = _rms_fwd(x_ref[...], n)
        for ci in range(F // FT):
            sl = slice(ci * FT, (ci + 1) * FT)
            dh = _mm(dg_ref[:, sl], wg_ref[sl, :]) + _mm(du_ref[:, sl], wu_ref[sl, :])
            if ci == 0:
                acc_ref[...] = dh
            else:
                acc_ref[...] += dh
        dx, dn = _rms_bwd(acc_ref[...], r, xh, n)
        dx_ref[...] = dy_ref[...] + dx

        @pl.when(pl.program_id(0) == 0)
        def _():
            dn_ref[...] = jnp.zeros_like(dn_ref) if prev is None else rest[1][...]

        dn_ref[...] += dn

    tok = pl.BlockSpec((tm, D), lambda i: (t0 + i, 0))
    hid = pl.BlockSpec((tm, F), lambda i: (t0 + i, 0))
    row = pl.BlockSpec((1, D), lambda i: (0, 0))
    in_specs = [tok, tok, hid, hid, _resident((1, D), lambda i: (0, 0)), _wblock(wg), _wblock(wu)]
    operands = [dy, x, dg, du, nrm, wg, wu]
    aliases = {}
    if prev is not None:
        in_specs += [_HBM, row]
        operands += list(prev)
        aliases = {7: 0}
    return _call(
        body, hooks, operands, name=name, grid=(nt,),
        out_shape=[jax.ShapeDtypeStruct((S, D), f32), jax.ShapeDtypeStruct((1, D), f32)],
        in_specs=in_specs, out_specs=[tok, row],
        scratch_shapes=[pltpu.VMEM((tm, D), f32)],
        input_output_aliases=aliases,
        compiler_params=_params("arbitrary"),
    )


TN_ROWS = 512


def _tn(name, xs, ys, block_of, hooks=()):
    S, M = xs.shape
    B = ys.shape[0]
    tr = TN_ROWS if M % TN_ROWS == 0 else M // 2
    assert M % tr == 0 and tr % 128 == 0
    nt = M // tr

    def body(x_ref, y_ref, o_ref):
        o_ref[...] = _mm_tn(x_ref[...], y_ref[...]).astype(CDT)

    return _call(
        body, hooks, [xs, ys], name=name, grid=(B, nt),
        out_shape=[jax.ShapeDtypeStruct((B * M, D), CDT)],
        in_specs=[pl.BlockSpec((S, tr), lambda b, i: (0, i)), pl.BlockSpec((None, S, D), lambda b, i: (b, 0, 0))],
        out_specs=[pl.BlockSpec((tr, D), lambda b, i: (block_of(b) * nt + i, 0))],
        compiler_params=_params("arbitrary", "arbitrary"),
    )


def _wgrad(name, xs, y, hooks=()):
    return _tn(name, xs, y[None], lambda b: 0, hooks)


def _merge_bwd_act(dx2, ya, yb, proj, wa, wb, wo, hooks=(), tm=512):
    S = dx2.shape[0]

    def body(dx_ref, ya_ref, yb_ref, gt_ref, wa_ref, wb_ref, wo_ref,
             dp_ref, da_ref, dr_ref, mix_ref, dxb_ref, dya_ref, dyb_ref):
        dxb = dx_ref[...].astype(CDT)
        dxb_ref[...] = dxb
        dmix = _mm_nt(dxb, wo_ref[...])
        ya = ya_ref[...].astype(f32)
        yb = yb_ref[...].astype(f32)
        sa = jax.nn.sigmoid(gt_ref[0].astype(f32))
        sb = jax.nn.sigmoid(gt_ref[1].astype(f32))
        mix_ref[...] = (sa * ya + sb * yb).astype(CDT)
        dya = (dmix * sa).astype(CDT)
        dyb = (dmix * sb).astype(CDT)
        dya_ref[...] = dya
        dyb_ref[...] = dyb
        dp_ref[0] = (dmix * ya * sa * (1.0 - sa)).astype(CDT)
        dp_ref[1] = (dmix * yb * sb * (1.0 - sb)).astype(CDT)
        da_ref[...] = _mm_nt(dya, wa_ref[...]).astype(CDT)
        dr_ref[...] = _mm_nt(dyb, wb_ref[...]).astype(CDT)

    tok = pl.BlockSpec((tm, D), lambda i: (i, 0))
    gates = pl.BlockSpec((2, tm, D), lambda i: (SLOT_GA // 2, i, 0))
    act = jax.ShapeDtypeStruct((S, D), CDT)
    return _call(
        body, hooks, [dx2, ya, yb, proj, wa, wb, wo], name="merge_bwd_act", grid=(S // tm,),
        out_shape=[jax.ShapeDtypeStruct((8, S, D), CDT), act, act, act, act, act, act],
        in_specs=[tok, tok, tok, gates, _wblock(wa), _wblock(wb), _wblock(wo)],
        out_specs=[gates, tok, tok, tok, tok, tok, tok],
        compiler_params=_params("arbitrary"),
    )


def _sgu_bwd(da, proj, dproj, gn, bn, ws, wst, bsc, hooks=(), tm=512):
    S = proj.shape[1]
    GW = D // G

    def body(da_ref, p_ref, dpin_ref, gn_ref, bn_ref, ws_ref, wst_ref, bs_ref,
             dp_ref, dws_ref, dbs_ref, dgn_ref, dbn_ref, ds_ref, dvn_ref):
        @pl.when(pl.program_id(0) == 0)
        def _():
            dws_ref[...] = jnp.zeros_like(dws_ref)
            dbs_ref[...] = jnp.zeros_like(dbs_ref)
            dgn_ref[...] = jnp.zeros_like(dgn_ref)
            dbn_ref[...] = jnp.zeros_like(dbn_ref)

        pu = p_ref[0].astype(f32)
        pv = p_ref[1].astype(f32)
        ua = _gelu(pu)
        va = _gelu(pv)
        gn = gn_ref[...]
        rstd, vhat, vn = _sgu_norm(va, gn, bn_ref[...])
        vnb = vn.astype(CDT)
        dav = da_ref[...].astype(f32)
        dsb = (dav * ua).astype(CDT)
        ones = jnp.ones((8, GW), CDT)
        for ch in range(tm // C):
            rs = slice(ch * C, (ch + 1) * C)
            for gi in range(G):
                cs = slice(gi * GW, (gi + 1) * GW)
                s = _mm(ws_ref[gi], vnb[rs, cs]) + bs_ref[gi]
                ds_ref[rs, cs] = s
                dsg = dsb[rs, cs]
                dws_ref[gi] += _mm_nt(dsg, vnb[rs, cs])
                dbs_ref[gi] += _mm_nt(ones, dsg)
                dvn_ref[rs, cs] = _mm(wst_ref[gi], dsg)
        dp_ref[0] = (dav * ds_ref[...] * _gelu_grad(pu)).astype(CDT)
        dvn = dvn_ref[...]
        dgn_ref[...] += jnp.sum(dvn * vhat, axis=0, keepdims=True)
        dbn_ref[...] += jnp.sum(dvn, axis=0, keepdims=True)
        dvh = dvn * gn
        dva = rstd * (dvh - jnp.mean(dvh, axis=-1, keepdims=True) - vhat * jnp.mean(dvh * vhat, axis=-1, keepdims=True))
        dp_ref[1] = (dva * _gelu_grad(pv)).astype(CDT)

    uv = pl.BlockSpec((2, tm, D), lambda i: (0, i, 0))
    row = _resident((1, D), lambda i: (0, 0))
    return _call(
        body, hooks, [da, proj, dproj, gn, bn, ws, wst, bsc], name="sgu_bwd", grid=(S // tm,),
        out_shape=[jax.ShapeDtypeStruct(dproj.shape, CDT), jax.ShapeDtypeStruct((G, C, C), f32),
                   jax.ShapeDtypeStruct((G, 8, C), f32), jax.ShapeDtypeStruct((1, D), f32), jax.ShapeDtypeStruct((1, D), f32)],
        in_specs=[pl.BlockSpec((tm, D), lambda i: (i, 0)), uv, _HBM, row, row,
                  _resident((G, C, C), lambda i: (0, 0, 0)), _resident((G, C, C), lambda i: (0, 0, 0)),
                  _resident((G, C, 1), lambda i: (0, 0, 0))],
        out_specs=[uv, pl.BlockSpec((G, C, C), lambda i: (0, 0, 0)), pl.BlockSpec((G, 8, C), lambda i: (0, 0, 0)),
                   pl.BlockSpec((1, D), lambda i: (0, 0)), pl.BlockSpec((1, D), lambda i: (0, 0))],
        scratch_shapes=[pltpu.VMEM((tm, D), f32), pltpu.VMEM((tm, D), f32)],
        input_output_aliases={2: 0},
        compiler_params=_params("arbitrary"),
    )


def _ret_bwd(dr, R, sfs, sbs, proj, dproj, cos, sin, dl, hooks=()):
    S = proj.shape[1]
    NC = S // RC
    assert NC % 2 == 0

    def body(dr_ref, R_ref, sf_ref, sb_ref, q_ref, k_ref, v_ref, g_ref, dpin_ref, cos_ref, sin_ref, dl_ref,
             dp_ref, dd_ref, dR_ref, gb_ref, gf_ref, acc_ref):
        t = _decay_tables(dl_ref)

        def gate_norm_bwd(i, carry):
            rs = pl.ds(pl.multiple_of(i * ROW_TILE, ROW_TILE), ROW_TILE)
            Rv = R_ref[rs, :]
            rstd = lax.rsqrt(jnp.mean(Rv * Rv, axis=-1, keepdims=True) + EPS)
            rn = Rv * rstd
            gv = g_ref[rs, :].astype(f32)
            s = jax.nn.sigmoid(gv)
            drv = dr_ref[rs, :].astype(f32)
            dp_ref[3, rs, :] = (drv * rn * (s * (1.0 + gv * (1.0 - s)))).astype(CDT)
            drn = drv * gv * s
            dR_ref[rs, :] = (rstd * (drn - rn * jnp.mean(drn * rn, axis=-1, keepdims=True))).astype(CDT)
            return carry

        lax.fori_loop(0, S // ROW_TILE, gate_norm_bwd, 0)

        def chunk(n):
            rows = pl.ds(pl.multiple_of(n * RC, RC), RC)
            return rows, q_ref[rows, :], k_ref[rows, :], v_ref[rows, :], dR_ref[rows, :]

        def emit_kv(rows, dk, dv, final):
            if not final:
                dp_ref[1, rows, :] = dk.astype(CDT)
                dp_ref[2, rows, :] = dv.astype(CDT)
            else:
                co, si = cos_ref[rows, :], sin_ref[rows, :]
                dk = dp_ref[1, rows, :].astype(f32) + dk
                dp_ref[1, rows, :] = (_unrotate(dk, co, si) * K_SCALE).astype(CDT)
                dp_ref[2, rows, :] = (dp_ref[2, rows, :].astype(f32) + dv).astype(CDT)

        gb_ref[...] = jnp.zeros_like(gb_ref)
        gf_ref[...] = jnp.zeros_like(gf_ref)
        acc_ref[...] = jnp.zeros_like(acc_ref)
        dpos = jnp.abs(t["d"])

        def ascend(n, final):
            rows, qn, kn, vn, dRn = chunk(n)
            qf, kf = qn.astype(f32), kn.astype(f32)
            sc = _mm_nt(qn, kn)
            dA = _mm_nt(dRn, vn)
            w = sc * dA * t["dmat"] * dpos
            lgf_part = jnp.sum(jnp.where(t["lower"], w, 0.0), axis=0, keepdims=True)
            lgb_part = jnp.sum(jnp.where(t["lower"], 0.0, w), axis=0, keepdims=True)
            dsc = (dA * t["dmat"]).astype(CDT)
            dq = _mm(dsc, kn)
            scT = (_mm_nt(kn, qn) * t["dmat_t"]).astype(CDT)
            dscT = (_mm_nt(vn, dRn) * t["dmat_t"]).astype(CDT)
            dk = _mm(dscT, qn)
            dv = _mm(scT, dRn)
            sfb = sf_ref[n]
            sbb = sb_ref[n]
            qdf = qf * t["fq"]
            dqdf = _mm_nt(dRn, sfb)
            dq += dqdf * t["fq"]
            lgf_row = jnp.sum(qdf * dqdf * (t["pos"] + 1.0), axis=0, keepdims=True)
            qdb = qf * t["bq"]
            dqdb = _mm_nt(dRn, sbb)
            dq += dqdb * t["bq"]
            lgb_row = jnp.sum(qdb * dqdb * (RC - t["pos"]), axis=0, keepdims=True)
            gb = gb_ref[...]
            gbb = gb.astype(CDT)
            kdb = kf * t["bk"]
            dkdb = _mm_nt(vn, gbb)
            dk += dkdb * t["bk"]
            dv += _mm(kdb.astype(CDT), gbb)
            lgb_row += jnp.sum(kdb * dkdb * t["pos"], axis=0, keepdims=True)
            lgb_row += float(RC) * t["lamb"] * jnp.sum(gb * sbb.astype(f32), axis=0, keepdims=True)
            co, si = cos_ref[rows, :], sin_ref[rows, :]
            dp_ref[0, rows, :] = _unrotate(dq, co, si).astype(CDT)
            emit_kv(rows, dk, dv, final)
            acc_ref[0:1, :] += lgf_row + lgf_part
            acc_ref[1:2, :] += lgb_row + lgb_part
            gb_ref[...] = gb * t["lamb"] + _mm_tn(qdb.astype(CDT), dRn)

        def descend(n, final):
            rows, qn, kn, vn, dRn = chunk(n)
            gf = gf_ref[...]
            gfb = gf.astype(CDT)
            kdf = kn.astype(f32) * t["fk"]
            dkdf = _mm_nt(vn, gfb)
            lgf_row = jnp.sum(kdf * dkdf * (RC - 1.0 - t["pos"]), axis=0, keepdims=True)
            lgf_row += float(RC) * t["lamf"] * jnp.sum(gf * sf_ref[n].astype(f32), axis=0, keepdims=True)
            acc_ref[0:1, :] += lgf_row
            emit_kv(rows, dkdf * t["fk"], _mm(kdf.astype(CDT), gfb), final)
            gf_ref[...] = gf * t["lamf"] + _mm_tn((qn.astype(f32) * t["fq"]).astype(CDT), dRn)

        def sweep(final):
            def step(i, carry):
                ascend(i, final)
                descend(NC - 1 - i, final)
                return carry
            return step

        lax.fori_loop(0, NC // 2, sweep(False), 0)
        lax.fori_loop(NC // 2, NC, sweep(True), 0)
        dlg = jnp.sum(acc_ref[...], axis=1, keepdims=True)
        dlogit = dlg * jax.nn.sigmoid(-dl_ref[:, 0:1])
        lane = lax.broadcasted_iota(jnp.int32, (8, 128), 1)
        dd_ref[...] = jnp.where(lane == pl.program_id(0), jnp.broadcast_to(dlogit, (8, 128)), 0.0)

    def seg(slot):
        return pl.BlockSpec((None, S, DK), lambda h: (slot, 0, h))

    head = pl.BlockSpec((S, DK), lambda h: (0, h))
    states = pl.BlockSpec((None, NC, DK, DK), lambda h: (h, 0, 0, 0))
    return _call(
        body, hooks, [dr, R, sfs, sbs, proj, proj, proj, proj, dproj, cos, sin, dl], name="ret_bwd", grid=(H,),
        out_shape=[jax.ShapeDtypeStruct(dproj.shape, CDT), jax.ShapeDtypeStruct((H, 8, 128), f32)],
        in_specs=[head, head, states, states, seg(SLOT_Q), seg(SLOT_K), seg(SLOT_VR), seg(SLOT_GR), _HBM,
                  _resident((S, DK // 2), lambda h: (0, 0)), _resident((S, DK // 2), lambda h: (0, 0)),
                  pl.BlockSpec((None, 8, DK), lambda h: (h, 0, 0))],
        out_specs=[pl.BlockSpec((4, S, DK), lambda h: (1, 0, h), pipeline_mode=pl.Buffered(1)),
                   pl.BlockSpec((None, 8, 128), lambda h: (h, 0, 0))],
        scratch_shapes=[pltpu.VMEM((S, DK), CDT),
                        pltpu.VMEM((DK, DK), f32), pltpu.VMEM((DK, DK), f32), pltpu.VMEM((8, DK), f32)],
        input_output_aliases={8: 0},
        compiler_params=_params("arbitrary"),
    )


def _proj_bwd_act(dproj, dx2, x1, nrm, wfull, hooks=(), tm=512):
    S = x1.shape[0]

    def body(dp_ref, dx2_ref, x_ref, n_ref, w_ref, dx_ref, dxh_ref, dn_ref, db_ref, acc_ref):
        @pl.when(pl.program_id(0) == 0)
        def _():
            dn_ref[...] = jnp.zeros_like(dn_ref)
            db_ref[...] = jnp.zeros_like(db_ref)

        for p in range(8):
            seg = SEG_OF_SLOT[p]
            dp = dp_ref[p]
            db_ref[seg] += jnp.sum(dp.astype(f32), axis=0, keepdims=True)
            dh = _mm_nt(dp, w_ref[seg * D:(seg + 1) * D, :])
            if p == 0:
                acc_ref[...] = dh
            else:
                acc_ref[...] += dh
        n = n_ref[...]
        r, xh, _ = _rms_fwd(x_ref[...], n)
        dx, dn = _rms_bwd(acc_ref[...], r, xh, n)
        dx = dx2_ref[...] + dx
        dx_ref[...] = dx
        dxh_ref[...] = (0.5 * dx).astype(CDT)
        dn_ref[...] += dn

    tok = pl.BlockSpec((tm, D), lambda i: (i, 0))
    return _call(
        body, hooks, [dproj, dx2, x1, nrm, wfull], name="proj_bwd_act", grid=(S // tm,),
        out_shape=[jax.ShapeDtypeStruct((S, D), f32), jax.ShapeDtypeStruct((S, D), CDT), jax.ShapeDtypeStruct((1, D), f32),
                   jax.ShapeDtypeStruct((8, 1, D), f32)],
        in_specs=[pl.BlockSpec((8, tm, D), lambda i: (0, i, 0)), tok, tok, _resident((1, D), lambda i: (0, 0)),
                  _resident((8 * D, D), lambda i: (0, 0))],
        out_specs=[tok, tok, pl.BlockSpec((1, D), lambda i: (0, 0)), pl.BlockSpec((8, 1, D), lambda i: (0, 0, 0))],
        scratch_shapes=[pltpu.VMEM((tm, D), f32)],
        compiler_params=_params("arbitrary"),
    )


def _rs_sum(name, gfulls, lands, my_c):
    n = len(gfulls)
    rows = gfulls[0].shape[0] // NDEV
    assert all(g.shape[0] == NDEV * rows for g in gfulls)

    def body(c_ref, *refs):
        for g_ref, l_ref, o_ref in zip(refs[:n], refs[n:2 * n], refs[2 * n:]):
            o_ref[...] = (g_ref[...].astype(f32) + l_ref[...].astype(f32)).astype(CDT)

    slot = pl.BlockSpec((None, rows, D), lambda k, c: (k, 0, 0))
    return pl.pallas_call(
        body, name=name,
        grid_spec=pltpu.PrefetchScalarGridSpec(
            num_scalar_prefetch=1, grid=(NCHIP,),
            in_specs=[pl.BlockSpec((rows, D), lambda k, c: (2 * k + c[0], 0))] * n + [slot] * n,
            out_specs=[slot] * n),
        out_shape=[jax.ShapeDtypeStruct((NCHIP, rows, D), CDT)] * n,
        compiler_params=_params("arbitrary"),
    )(my_c, *gfulls, *lands)


def _adamw_math(g, w, m, v):
    m2 = ADAM_B1 * m + (1.0 - ADAM_B1) * g
    v2 = ADAM_B2 * v + (1.0 - ADAM_B2) * (g * g)
    delta = -ADAM_LR * ((m2 / BC1) / (jnp.sqrt(v2 / BC2) + ADAM_EPS) + ADAM_WD * w)
    return delta, m2, v2


def _adamw_big(name, landed, w, m, v, after):
    rows = w.shape[0]
    tr = min(rows, 256) if rows % 256 == 0 else rows

    def body(l_ref, w_ref, m_ref, v_ref, after_ref, g_ref, d_ref, m2_ref, v2_ref):
        g = l_ref[0].astype(f32)
        for k in range(1, NCHIP):
            g = g + l_ref[k].astype(f32)
        g_ref[...] = g
        d_ref[...], m2_ref[...], v2_ref[...] = _adamw_math(g, w_ref[...], m_ref[...], v_ref[...])

    blk = pl.BlockSpec((tr, D), lambda i: (i, 0))
    o = jax.ShapeDtypeStruct((rows, D), f32)
    return pl.pallas_call(
        body, name=name, grid=(rows // tr,), out_shape=[o, o, o, o],
        in_specs=[pl.BlockSpec((NCHIP, tr, D), lambda i: (0, i, 0)), blk, blk, blk, _HBM],
        out_specs=[blk, blk, blk, blk],
        compiler_params=_params("arbitrary"),
    )(landed, w, m, v, after)


ROW_FFN1_NORM, ROW_MIX_NORM, ROW_SGU_G, ROW_SGU_B, ROW_FFN2_NORM, ROW_FINAL_NORM, ROW_B_IN = 0, 1, 2, 3, 4, 5, 8
ROW_WS, ROW_BS, ROW_DECAY = 0, G * C, G * C + G * 8


def _adamw_small(ga, gb, gn1, gl, params):
    def body(ga_ref, gb_ref, gn1_ref, gl_ref, *refs):
        ins, outs = refs[:30], refs[30:]

        def total(ref, r0, n):
            g = ref[0, r0:r0 + n, :]
            for j in range(1, NDEV):
                g = g + ref[j, r0:r0 + n, :]
            return g

        def apply(i, g, rows=slice(None)):
            w, m, v = ins[3 * i][rows, :], ins[3 * i + 1][rows, :], ins[3 * i + 2][rows, :]
            outs[4 * i][rows, :] = g
            outs[4 * i + 1][rows, :], outs[4 * i + 2][rows, :], outs[4 * i + 3][rows, :] = _adamw_math(g, w, m, v)

        outs[40][...] = total(gl_ref, 0, 8)
        apply(0, total(gn1_ref, 0, 1))
        for i, r in enumerate((ROW_FFN1_NORM, ROW_MIX_NORM, ROW_SGU_G, ROW_SGU_B, ROW_FFN2_NORM, ROW_FINAL_NORM)):
            if i:
                apply(i, total(ga_ref, r, 1))
        apply(6, total(ga_ref, ROW_B_IN, 8))
        apply(7, total(gb_ref, ROW_WS, G * C))
        for gi in range(G):
            apply(8, total(gb_ref, ROW_BS + 8 * gi, 1), slice(gi, gi + 1))
        dec = total(gb_ref, ROW_DECAY, 8)
        for hh in range(1, H):
            dec = dec + total(gb_ref, ROW_DECAY + 8 * hh, 8)
        apply(9, dec)

    flat = [a for p in params for a in p]
    out_shape = [jax.ShapeDtypeStruct(p[0].shape, f32) for p in params for _ in range(4)]
    out_shape.append(jax.ShapeDtypeStruct((8, 128), f32))
    vm = pl.BlockSpec(memory_space=pltpu.VMEM)
    return pl.pallas_call(
        body, name="adamw_small", out_shape=out_shape,
        in_specs=[vm] * (4 + len(flat)), out_specs=[vm] * len(out_shape),
        compiler_params=pltpu.CompilerParams(vmem_limit_bytes=VMEM_LIMIT),
    )(ga, gb, gn1, gl, *flat)


def kernel(x, ffn1_norm, ffn1_w_gate, ffn1_w_up, ffn1_w_down, mix_norm, w_in, b_in, sgu_norm_g, sgu_norm_b, sgu_w_s, sgu_b_s, ret_decay_logit, w_branch_a, w_branch_b, w_out, ffn2_norm, ffn2_w_gate, ffn2_w_up, ffn2_w_down, final_norm, loss_target, m_ffn1_norm, m_ffn1_w_gate, m_ffn1_w_up, m_ffn1_w_down, m_mix_norm, m_w_in, m_b_in, m_sgu_norm_g, m_sgu_norm_b, m_sgu_w_s, m_sgu_b_s, m_ret_decay_logit, m_w_branch_a, m_w_branch_b, m_w_out, m_ffn2_norm, m_ffn2_w_gate, m_ffn2_w_up, m_ffn2_w_down, m_final_norm, v_ffn1_norm, v_ffn1_w_gate, v_ffn1_w_up, v_ffn1_w_down, v_mix_norm, v_w_in, v_b_in, v_sgu_norm_g, v_sgu_norm_b, v_sgu_w_s, v_sgu_b_s, v_ret_decay_logit, v_w_branch_a, v_w_branch_b, v_w_out, v_ffn2_norm, v_ffn2_w_gate, v_ffn2_w_up, v_ffn2_w_down, v_final_norm):
    args = dict(locals())
    S = x.shape[1]
    xs = x[0]
    target = loss_target[0]

    def buf_layout(name, a):
        a = a[0]
        return a.T if name in W_TRANSPOSED else a

    sh = {n: buf_layout(n, args[n]).astype(CDT) for n in W_NAMES}
    wf = {}

    b3 = b_in.reshape(8, 1, D)
    ws = sgu_w_s[0].astype(CDT)
    wst = jnp.swapaxes(sgu_w_s[0], 1, 2).astype(CDT)
    bsc = sgu_b_s[0].reshape(G, C, 1)
    dl = jnp.zeros((H, 8, DK), f32).at[:, 0:2, :].set(jnp.broadcast_to(ret_decay_logit[0].T[:, :, None], (H, 2, DK)))
    theta = ROPE_BASE ** (-jnp.arange(0, DK, 2, dtype=f32) / DK)
    ang = jnp.arange(S, dtype=f32)[:, None] * theta[None, :]
    cos, sin = jnp.cos(ang), jnp.sin(ang)
    fnorm = final_norm.reshape(1, D)

    f1 = ("ffn1_w_gate", "ffn1_w_up", "ffn1_w_down")
    f2 = ("ffn2_w_gate", "ffn2_w_up", "ffn2_w_down")
    br = ("w_branch_a", "w_branch_b", "w_out")
    for cid, names in enumerate((f1, ("w_in",), br, f2)):
        wf.update(zip(names, _sequence("ag_" + names[0], 1 + cid, NEIGHBOURS, [_ag_hook(sh[n]) for n in names])))
    x1, g1, u1, a1, hf1 = _ffn_fwd("ffn1_fwd", xs, ffn1_norm, *[wf[n] for n in f1])
    proj, h2 = _proj_fwd(x1, mix_norm, wf["w_in"], b3, cos, sin)
    a = _sgu_fwd(proj, sgu_norm_g, sgu_norm_b, ws, bsc)
    R, r, sfs, sbs = _ret_fwd(proj, dl)
    x2, ya, yb = _merge_fwd(a, r, proj, x1, *[wf[n] for n in br])
    x3, g2, u2, a2, hf2 = _ffn_fwd("ffn2_fwd", x2, ffn2_norm, *[wf[n] for n in f2])
    dx3, dyh2, d_final, loss_part = _loss_head(x3, fnorm, target)

    my_c = lax.axis_index("c").astype(jnp.int32).reshape(1)
    gw, landed, sequenced = {}, {}, []

    def d2d(*names):
        return [_rs_d2d_hook(gw[n]) for n in names]

    def behind(x, token):
        return lax.optimization_barrier((x, token))[0]

    def to_chips(names, sibs, more=()):
        parts = list(_rs_sum("rs_sum_" + names[0], [gw[n] for n in names], list(sibs), my_c))
        token = parts[0]
        if sequenced:
            parts[0] = behind(parts[0], sequenced[-1])
        hooks = [_rs_ici_hook(p) for p in parts] + ([_small_hook(list(more))] if more else [])
        reach = EVERYONE if more else CHIPS
        got = _sequence("rs_chips_" + names[0], REACH_ID[reach], reach, hooks)
        sequenced.append(got[0])
        landed.update(zip(names, got))
        return got[len(names):], token

    def ffn_bwd(tag, names, dy, dyh, x, g, u, a, h, nrm, each_alone, more=()):
        wg, wu, wd = names
        (gw[wd],) = _wgrad(tag + "_wd_grad", a, dyh)
        dg, du, sib_d = _ffn_bwd_hidden(tag + "_bwd_hidden", dyh, g, u, wf[wd], d2d(wd))
        if each_alone:
            dg = behind(dg, to_chips([wd], [sib_d])[1])
        (gw[wg],) = _wgrad(tag + "_wg_grad", dg, h)
        gw[wu], sib_g = _wgrad(tag + "_wu_grad", du, h, d2d(wg))
        if each_alone:
            dy = behind(dy, to_chips([wg], [sib_g])[1])
        dx, dn, sib_u = _ffn_bwd_in(tag + "_bwd_in", dy, x, dg, du, nrm, wf[wg], wf[wu], d2d(wu))
        if each_alone:
            return (dx, dn) + to_chips([wu], [sib_u], more(dn))
        return (dx, dn) + to_chips([wd, wg, wu], [sib_d, sib_g, sib_u])

    dx2, d_ffn2n, _, token = ffn_bwd("ffn2", f2, dx3, dyh2, x2, g2, u2, a2, hf2, ffn2_norm, False)
    dproj, da, dr, mix, dx2b, dya, dyb = _merge_bwd_act(behind(dx2, token), ya, yb, proj, *[wf[n] for n in br])
    dproj, d_ws, d_bs, d_gn, d_bn = _sgu_bwd(da, proj, dproj, sgu_norm_g, sgu_norm_b, ws, wst, bsc)
    dproj, d_dec = _ret_bwd(dr, R, sfs, sbs, proj, dproj, cos, sin, dl)
    (gw["w_in"],) = _tn("win_grad", h2, dproj, _seg_of_slot)
    gw["w_out"], sib_win = _wgrad("wo_grad", mix, dx2b, d2d("w_in"))
    small_sgu = jnp.concatenate([d_ws.reshape(G * C, C), d_bs.reshape(G * 8, C)], axis=0)
    (g_sgu, gl), token = to_chips(["w_in"], [sib_win], [small_sgu, loss_part])
    (gw["w_branch_a"],) = _wgrad("wa_grad", a, behind(dya, token))
    (gw["w_branch_b"],) = _wgrad("wb_grad", r, dyb)
    dx1, dyh1, d_mixn, d_bin, *sib_br = _proj_bwd_act(behind(dproj, token), dx2, x1, mix_norm, wf["w_in"], d2d(*br))
    small_a = jnp.concatenate([jnp.zeros((1, D), f32), d_mixn, d_gn, d_bn, d_ffn2n, d_final, jnp.zeros((2, D), f32),
                               d_bin.reshape(8, D)], axis=0)
    (ga, g_dec), token = to_chips(list(br), sib_br, [small_a, d_dec.reshape(H * 8, 128)])
    gb = jnp.concatenate([g_sgu, g_dec], axis=1)
    dxs, d_ffn1n, (gn1,), token = ffn_bwd("ffn1", f1, dx1, behind(dyh1, token), xs, g1, u1, a1, hf1, ffn1_norm, True,
                                          lambda dn: [dn])

    out = {"grad_x": dxs[None]}

    def native(name, a):
        a = a.T if name in W_TRANSPOSED else a
        return a[None]

    after = token
    for n in f2 + ("w_in",) + br + (f1[2], f1[0], f1[1]):
        res = _adamw_big("adamw_" + n, landed[n], buf_layout(n, args[n]), buf_layout(n, args["m_" + n]),
                         buf_layout(n, args["v_" + n]), after)
        after = res[0]
        for pre, val in zip(("grad_", "delta_", "new_m_", "new_v_"), res):
            out[pre + n] = native(n, val)

    def pad_decay(a):
        return jnp.zeros((8, 128), f32).at[0:2, 0:H].set(a[0])

    small = [
        ("ffn1_norm", lambda a: a, lambda a: a), ("mix_norm", lambda a: a, lambda a: a),
        ("sgu_norm_g", lambda a: a, lambda a: a), ("sgu_norm_b", lambda a: a, lambda a: a),
        ("ffn2_norm", lambda a: a, lambda a: a),
        ("final_norm", lambda a: a.reshape(1, D), lambda a: a.reshape(D)),
        ("b_in", lambda a: a.reshape(8, D), lambda a: a.reshape(1, 8 * D)),
        ("sgu_w_s", lambda a: a.reshape(G * C, C), lambda a: a.reshape(1, G, C, C)),
        ("sgu_b_s", lambda a: a[0], lambda a: a[None]),
        ("ret_decay_logit", pad_decay, lambda a: a[None, 0:2, 0:H]),
    ]
    res = _adamw_small(ga, gb, gn1, gl, [(to(args[n]), to(args["m_" + n]), to(args["v_" + n])) for n, to, _ in small])
    out["loss"] = res[40][0, 0]
    for i, (n, _, back) in enumerate(small):
        for j, pre in enumerate(("grad_", "delta_", "new_m_", "new_v_")):
            out[pre + n] = back(res[4 * i + j])

    weights = ("ffn1_norm", "ffn1_w_gate", "ffn1_w_up", "ffn1_w_down", "mix_norm", "w_in", "b_in", "sgu_norm_g",
               "sgu_norm_b", "sgu_w_s", "sgu_b_s", "ret_decay_logit", "w_branch_a", "w_branch_b", "w_out", "ffn2_norm",
               "ffn2_w_gate", "ffn2_w_up", "ffn2_w_down", "final_norm")
    return (out["loss"], out["grad_x"], *[out["grad_" + n] for n in weights], *[out["delta_" + n] for n in weights],
            *[out["new_m_" + n] for n in weights], *[out["new_v_" + n] for n in weights])
```

```python
import functools
import math

import jax
import jax.numpy as jnp
from jax import lax
from jax.experimental import pallas as pl
from jax.experimental.pallas import tpu as pltpu
from jax.experimental.pallas import tpu_sc as plsc

f32 = jnp.float32
CDT = jnp.bfloat16

D = 1024
F = 2816
C = 128
RC = 256
H = 4
DK = 256
G = 4
NDEV = 8
NCHIP = 4
EPS = 1e-6
ROPE_BASE = 10000.0
FT = 256
V7X_VMEM_BYTES = 64 * 1024 * 1024
VMEM_LIMIT = V7X_VMEM_BYTES - 8 * 1024 * 1024

ADAM_LR, ADAM_B1, ADAM_B2, ADAM_EPS, ADAM_WD, ADAM_STEP = 0.001, 0.9, 0.999, 1e-08, 0.01, 10
BC1 = 1.0 - ADAM_B1 ** ADAM_STEP
BC2 = 1.0 - ADAM_B2 ** ADAM_STEP

W_ROWS = dict(ffn1_w_gate=352, ffn1_w_up=352, ffn1_w_down=352, w_in=1024, w_branch_a=128, w_branch_b=128, w_out=128,
              ffn2_w_gate=352, ffn2_w_up=352, ffn2_w_down=352)
W_NAMES = tuple(W_ROWS)
W_TRANSPOSED = ("ffn1_w_gate", "ffn1_w_up", "ffn2_w_gate", "ffn2_w_up")

SLOT_U, SLOT_V, SLOT_GA, SLOT_GB, SLOT_Q, SLOT_K, SLOT_VR, SLOT_GR = range(8)


SEG_OF_SLOT = (0, 1, 6, 7, 2, 3, 4, 5)


def _seg_of_slot(p):
    return jnp.where(p < 2, p, jnp.where(p < 4, p + 4, p - 2))


def _mm(a, b):
    return jnp.dot(a, b, preferred_element_type=f32)


def _mm_nt(a, b):
    return lax.dot_general(a, b, (((1,), (1,)), ((), ())), preferred_element_type=f32)


def _mm_tn(a, b):
    return lax.dot_general(a, b, (((0,), (0,)), ((), ())), preferred_element_type=f32)


def _params(*sem):
    return pltpu.CompilerParams(dimension_semantics=sem, vmem_limit_bytes=VMEM_LIMIT)


def _resident(shape, index_map):
    return pl.BlockSpec(shape, index_map, pipeline_mode=pl.Buffered(1))


def _gelu(x):
    return 0.5 * x * (1.0 + lax.erf(x * (1.0 / math.sqrt(2.0))))


def _gelu_grad(x):
    return 0.5 * (1.0 + lax.erf(x * (1.0 / math.sqrt(2.0)))) + x * jnp.exp(-0.5 * x * x) * (1.0 / math.sqrt(2.0 * math.pi))


def _rms_fwd(x, n):
    r = lax.rsqrt(jnp.mean(x * x, axis=-1, keepdims=True) + EPS)
    xh = x * r
    return r, xh, xh * n


def _rms_bwd(dh, r, xh, n):
    dxh = dh * n
    dx = r * (dxh - xh * jnp.mean(dxh * xh, axis=-1, keepdims=True))
    return dx, jnp.sum(dh * xh, axis=0, keepdims=True)


MESH_ID = pl.DeviceIdType.MESH
_HBM = pl.BlockSpec(memory_space=pltpu.HBM)


def _my_place():
    return lax.axis_index("x"), lax.axis_index("y"), lax.axis_index("c")


def _ici_peers(x, y, c):
    return [((1 - x, y, c), 2 * (1 - x) + y), ((x, 1 - y, c), 2 * x + 1 - y), ((1 - x, 1 - y, c), 2 * (1 - x) + 1 - y)]


class _Hook:
    def __init__(self, operands, out_shapes, n_remote, n_local, start, finish, relay=None):
        self.operands, self.out_shapes = list(operands), list(out_shapes)
        self.n_remote, self.n_local, self.start, self.finish = n_remote, n_local, start, finish
        self.relay = relay or (lambda *a: None)


def _call(body, hooks, operands, *, in_specs, out_specs, out_shape, grid=None, scratch_shapes=(), **kw):
    hooks = tuple(hooks)
    n_in, n_out, n_scr = len(in_specs), len(out_shape), len(scratch_shapes)
    h_ops = [a for h in hooks for a in h.operands]
    h_outs = [s for h in hooks for s in h.out_shapes]
    h_sems = [pltpu.SemaphoreType.DMA((n,)) for h in hooks for n in (h.n_remote, h.n_remote, max(h.n_local, 1))]

    def wrapped(*refs):
        ins, hin = refs[:n_in], refs[n_in:n_in + len(h_ops)]
        o0 = n_in + len(h_ops)
        outs, hout = refs[o0:o0 + n_out], refs[o0 + n_out:o0 + n_out + len(h_outs)]
        s0 = o0 + n_out + len(h_outs)
        scr, hsem = refs[s0:s0 + n_scr], refs[s0 + n_scr:]

        def run(phase):
            ip = op = 0
            for i, h in enumerate(hooks):
                ssem, rsem, lsem = hsem[3 * i:3 * i + 3]

                def rcopy(k, src, dst, dev, ssem=ssem, rsem=rsem):
                    return pltpu.make_async_remote_copy(src_ref=src, dst_ref=dst, send_sem=ssem.at[k], recv_sem=rsem.at[k],
                                                        device_id=dev, device_id_type=MESH_ID)

                def lcopy(k, src, dst, lsem=lsem):
                    return pltpu.make_async_copy(src, dst, lsem.at[k])

                getattr(h, phase)(hin[ip:ip + len(h.operands)], hout[op:op + len(h.out_shapes)], rcopy, lcopy)
                ip += len(h.operands)
                op += len(h.out_shapes)

        def at_edge(phase, last):
            if not hooks:
                return
            if grid is None:
                run(phase)
                return
            cond = None
            for ax, n in enumerate(grid):
                here = pl.program_id(ax) == (n - 1 if last else 0)
                cond = here if cond is None else cond & here
            pl.when(cond)(lambda: run(phase))

        at_edge("start", False)
        at_edge("relay", True)
        body(*ins, *outs, *scr)
        at_edge("finish", True)

    if grid is not None:
        kw["grid"] = grid
    return list(pl.pallas_call(
        wrapped, out_shape=list(out_shape) + h_outs, in_specs=list(in_specs) + [_HBM] * len(h_ops),
        out_specs=list(out_specs) + [_HBM] * len(h_outs), scratch_shapes=list(scratch_shapes) + h_sems, **kw,
    )(*operands, *h_ops))


def _exchange(name, hooks):
    return _call(lambda: None, hooks, [], name=name, in_specs=[], out_specs=[], out_shape=[])


def _rows(ref, start, n):
    return ref.at[pl.ds(start, n), :]


def _ag_hook(shard, early=True):
    rows = shard.shape[0]
    half = rows // 2
    assert half % 16 == 0

    def place():
        x, y, c = _my_place()
        devs = dict(sib=(x, y, 1 - c), xn=(1 - x, y, c), yn=(x, 1 - y, c))
        chips = dict(me=2 * x + y, xn=2 * (1 - x) + y, yn=2 * x + 1 - y, dg=2 * (1 - x) + 1 - y)
        return c, devs, chips

    def block(full, chip, c):
        return _rows(full, (2 * chip + c) * rows, rows)

    def halfblock(full, chip, c, upper):
        return _rows(full, (2 * chip + c) * rows + upper * half, half)

    def start(ins, outs, rcopy, lcopy):
        c, devs, chips = place()
        src, dst = ins[0], block(outs[0], chips["me"], c)
        lcopy(0, src, dst).start()
        for k, to in enumerate(("sib", "xn", "yn")):
            rcopy(k, src, dst, devs[to]).start()

    def relay(ins, outs, rcopy, lcopy):
        c, devs, chips = place()
        full = outs[0]
        blk = block(full, chips["xn"], c)
        rcopy(1, blk, blk, devs["xn"]).wait_recv()
        low = halfblock(full, chips["xn"], c, 0)
        rcopy(3, low, low, devs["yn"]).start()
        rcopy(5, blk, blk, devs["sib"]).start()
        blk = block(full, chips["yn"], c)
        rcopy(2, blk, blk, devs["yn"]).wait_recv()
        up = halfblock(full, chips["yn"], c, 1)
        rcopy(4, up, up, devs["xn"]).start()
        rcopy(6, blk, blk, devs["sib"]).start()
        low, up = halfblock(full, chips["dg"], c, 0), halfblock(full, chips["dg"], c, 1)
        rcopy(3, low, low, devs["yn"]).wait_recv()
        rcopy(4, up, up, devs["xn"]).wait_recv()
        blk = block(full, chips["dg"], c)
        rcopy(7, blk, blk, devs["sib"]).start()

    def finish(ins, outs, rcopy, lcopy):
        if not early:
            relay(ins, outs, rcopy, lcopy)
        c, devs, chips = place()
        full, sib = outs[0], devs["sib"]
        for k, chip in ((0, "me"), (5, "xn"), (6, "yn"), (7, "dg")):
            theirs = block(full, chips[chip], 1 - c)
            rcopy(k, theirs, theirs, sib).wait_recv()
            mine = block(full, chips[chip], c)
            if k:
                rcopy(k, mine, mine, sib).wait_send()
        src, dst = ins[0], block(full, chips["me"], c)
        lcopy(0, src, dst).wait()
        for k, to in enumerate(("sib", "xn", "yn")):
            rcopy(k, src, dst, devs[to]).wait_send()
        low, up = halfblock(full, chips["xn"], c, 0), halfblock(full, chips["yn"], c, 1)
        rcopy(3, low, low, devs["yn"]).wait_send()
        rcopy(4, up, up, devs["xn"]).wait_send()

    return _Hook([shard], [jax.ShapeDtypeStruct((NDEV * rows, D), shard.dtype)], 8, 1, start, finish, relay if early else None)


SIBLING, CHIPS, NEIGHBOURS, EVERYONE = "sibling", "chips", "sibling and the two neighbour chips", "everyone"
REACH_ID = {SIBLING: 5, CHIPS: 6, EVERYONE: 7}


def _sequence(name, collective_id, reach, hooks):
    ins = [[jax.new_ref(a, memory_space=pltpu.MemorySpace.HBM) for a in h.operands] for h in hooks]
    outs = [[jax.empty_ref(s, memory_space=pltpu.MemorySpace.HBM) for s in h.out_shapes] for h in hooks]
    sems = tuple(pltpu.SemaphoreType.DMA((n,)) for h in hooks for n in (h.n_remote, h.n_remote, max(h.n_local, 1)))

    @pl.kernel(mesh=plsc.ScalarSubcoreMesh(axis_name="sequencer", num_cores=1), name=name, scratch_types=sems,
               compiler_params=pltpu.CompilerParams(collective_id=collective_id))
    def launch(*sem_refs):
        x, y, c = _my_place()
        chips = [dev for dev, _ in _ici_peers(x, y, c)]
        others = [(1 - x if dx else x, 1 - y if dy else y, 1 - c if dc else c)
                  for dx in range(2) for dy in range(2) for dc in range(2) if dx + dy + dc]
        devs = {SIBLING: [(x, y, 1 - c)], CHIPS: chips, NEIGHBOURS: [(x, y, 1 - c), (1 - x, y, c), (x, 1 - y, c)],
                EVERYONE: others}[reach]
        barrier = pltpu.get_barrier_semaphore()
        for dev in devs:
            pl.semaphore_signal(barrier, inc=1, device_id=dev, device_id_type=MESH_ID)
        pl.semaphore_wait(barrier, len(devs))
        for phase in ("start", "relay", "finish"):
            for i, h in enumerate(hooks):
                ssem, rsem, lsem = sem_refs[3 * i:3 * i + 3]

                def rcopy(k, src, dst, dev, ssem=ssem, rsem=rsem):
                    return pltpu.make_async_remote_copy(src_ref=src, dst_ref=dst, send_sem=ssem.at[k], recv_sem=rsem.at[k],
                                                        device_id=dev, device_id_type=MESH_ID)

                def lcopy(k, src, dst, lsem=lsem):
                    return pltpu.make_async_copy(src, dst, lsem.at[k])

                getattr(h, phase)(ins[i], outs[i], rcopy, lcopy)

    launch()
    return [o[...] for os in outs for o in os]


def _rs_d2d_hook(gfull):
    rows = gfull.shape[0] // NDEV

    def pairs(g, land):
        x, y, c = _my_place()
        return (x, y, 1 - c), [(k, _rows(g, (2 * k + 1 - c) * rows, rows), land.at[k]) for k in range(NCHIP)]

    def start(ins, outs, rcopy, lcopy):
        sib, cps = pairs(ins[0], outs[0])
        for i, src, dst in cps:
            rcopy(i, src, dst, sib).start()

    def finish(ins, outs, rcopy, lcopy):
        sib, cps = pairs(ins[0], outs[0])
        for i, src, dst in cps:
            rcopy(i, dst, dst, sib).wait_recv()
        for i, src, dst in cps:
            rcopy(i, src, dst, sib).wait_send()

    return _Hook([gfull], [jax.ShapeDtypeStruct((NCHIP, rows, D), gfull.dtype)], NCHIP, 0, start, finish)


def _rs_ici_hook(part):
    def start(ins, outs, rcopy, lcopy):
        x, y, c = _my_place()
        mychip = 2 * x + y
        lcopy(0, ins[0].at[mychip], outs[0].at[mychip]).start()
        for j, (dev, chip) in enumerate(_ici_peers(x, y, c)):
            rcopy(j, ins[0].at[chip], outs[0].at[mychip], dev).start()

    def finish(ins, outs, rcopy, lcopy):
        x, y, c = _my_place()
        mychip = 2 * x + y
        peers = _ici_peers(x, y, c)
        for j, (dev, chip) in enumerate(peers):
            rcopy(j, outs[0].at[chip], outs[0].at[chip], dev).wait_recv()
        for j, (dev, chip) in enumerate(peers):
            rcopy(j, ins[0].at[chip], outs[0].at[mychip], dev).wait_send()
        lcopy(0, ins[0].at[mychip], outs[0].at[mychip]).wait()

    return _Hook([part], [jax.ShapeDtypeStruct(part.shape, part.dtype)], 3, 1, start, finish)


def _small_hook(arrays):
    n = len(arrays)

    def peers():
        x, y, c = _my_place()
        out = []
        for dx in range(2):
            for dy in range(2):
                for dc in range(2):
                    if dx + dy + dc:
                        px, py, pc = (1 - x if dx else x), (1 - y if dy else y), (1 - c if dc else c)
                        out.append(((px, py, pc), 4 * px + 2 * py + pc))
        return 4 * x + 2 * y + c, out

    def start(ins, outs, rcopy, lcopy):
        me, ps = peers()
        for t in range(n):
            lcopy(t, ins[t], outs[t].at[me]).start()
            for i, (dev, _) in enumerate(ps):
                rcopy(n * i + t, ins[t], outs[t].at[me], dev).start()

    def finish(ins, outs, rcopy, lcopy):
        me, ps = peers()
        for t in range(n):
            for i, (dev, peer) in enumerate(ps):
                rcopy(n * i + t, outs[t].at[peer], outs[t].at[peer], dev).wait_recv()
            for i, (dev, _) in enumerate(ps):
                rcopy(n * i + t, ins[t], outs[t].at[me], dev).wait_send()
            lcopy(t, ins[t], outs[t].at[me]).wait()

    return _Hook(arrays, [jax.ShapeDtypeStruct((NDEV,) + a.shape, a.dtype) for a in arrays], 7 * n, n, start, finish)


def _wblock(w):
    return _resident(w.shape, lambda *_: (0, 0))


def _ffn_fwd(name, x, nrm, wg, wu, wd, hooks=(), tm=512):
    S = x.shape[0]

    def body(x_ref, n_ref, wg_ref, wu_ref, wd_ref, y_ref, g_ref, u_ref, a_ref, h_ref, acc_ref):
        xv = x_ref[...]
        _, _, h = _rms_fwd(xv, n_ref[...])
        h = h.astype(CDT)
        h_ref[...] = h
        for ci in range(F // FT):
            sl = slice(ci * FT, (ci + 1) * FT)
            g = _mm_nt(h, wg_ref[sl, :])
            u = _mm_nt(h, wu_ref[sl, :])
            g_ref[:, sl] = g.astype(CDT)
            u_ref[:, sl] = u.astype(CDT)
            a = (g * jax.nn.sigmoid(g) * u).astype(CDT)
            a_ref[:, sl] = a
            o = _mm(a, wd_ref[sl, :])
            if ci == 0:
                acc_ref[...] = o
            else:
                acc_ref[...] += o
        y_ref[...] = xv + 0.5 * acc_ref[...]

    tok = pl.BlockSpec((tm, D), lambda i: (i, 0))
    hid = pl.BlockSpec((tm, F), lambda i: (i, 0))
    hidden = jax.ShapeDtypeStruct((S, F), CDT)
    return _call(
        body, hooks, [x, nrm, wg, wu, wd], name=name, grid=(S // tm,),
        out_shape=[jax.ShapeDtypeStruct((S, D), f32), hidden, hidden, hidden, jax.ShapeDtypeStruct((S, D), CDT)],
        in_specs=[tok, _resident((1, D), lambda i: (0, 0)), _wblock(wg), _wblock(wu), _wblock(wd)],
        out_specs=[tok, hid, hid, hid, tok],
        scratch_shapes=[pltpu.VMEM((tm, D), f32)],
        compiler_params=_params("arbitrary"),
    )


def _proj_fwd(x1, nrm, wfull, b3, cos, sin, hooks=(), tm=512):
    S = x1.shape[0]

    def body(x_ref, n_ref, w_ref, b_ref, cos_ref, sin_ref, p_ref, h_ref):
        _, _, h = _rms_fwd(x_ref[...], n_ref[...])
        h = h.astype(CDT)
        h_ref[...] = h
        for p in range(8):
            seg = SEG_OF_SLOT[p]
            z = _mm(h, w_ref[seg * D:(seg + 1) * D, :]) + b_ref[seg]
            if p in (SLOT_Q, SLOT_K):
                co, si = cos_ref[...], sin_ref[...]
                for hh in range(H):
                    cs = slice(hh * DK, (hh + 1) * DK)
                    zr = _rotate(z[:, cs], co, si)
                    p_ref[p, :, cs] = (zr * K_SCALE if p == SLOT_K else zr).astype(CDT)
            else:
                p_ref[p] = z.astype(CDT)

    tab = pl.BlockSpec((tm, DK // 2), lambda i: (i, 0))
    return _call(
        body, hooks, [x1, nrm, wfull, b3, cos, sin], name="proj_fwd", grid=(S // tm,),
        out_shape=[jax.ShapeDtypeStruct((8, S, D), CDT), jax.ShapeDtypeStruct((S, D), CDT)],
        in_specs=[pl.BlockSpec((tm, D), lambda i: (i, 0)), _resident((1, D), lambda i: (0, 0)),
                  _resident((8 * D, D), lambda i: (0, 0)), _resident((8, 1, D), lambda i: (0, 0, 0)), tab, tab],
        out_specs=[pl.BlockSpec((8, tm, D), lambda i: (0, i, 0)), pl.BlockSpec((tm, D), lambda i: (i, 0))],
        compiler_params=_params("arbitrary"),
    )


def _sgu_norm(va, gn, bn):
    mu = jnp.mean(va, axis=-1, keepdims=True)
    xc = va - mu
    rstd = lax.rsqrt(jnp.mean(xc * xc, axis=-1, keepdims=True) + EPS)
    vhat = xc * rstd
    return rstd, vhat, vhat * gn + bn


def _sgu_fwd(proj, gn, bn, ws, bsc, tm=512):
    S = proj.shape[1]
    GW = D // G

    def body(p_ref, gn_ref, bn_ref, ws_ref, bs_ref, a_ref):
        ua = _gelu(p_ref[0].astype(f32))
        va = _gelu(p_ref[1].astype(f32))
        _, _, vn = _sgu_norm(va, gn_ref[...], bn_ref[...])
        vn = vn.astype(CDT)
        for ch in range(tm // C):
            rs = slice(ch * C, (ch + 1) * C)
            for gi in range(G):
                cs = slice(gi * GW, (gi + 1) * GW)
                s = _mm(ws_ref[gi], vn[rs, cs]) + bs_ref[gi]
                a_ref[rs, cs] = (ua[rs, cs] * s).astype(CDT)

    return pl.pallas_call(
        body, name="sgu_fwd", grid=(S // tm,),
        out_shape=jax.ShapeDtypeStruct((S, D), CDT),
        in_specs=[pl.BlockSpec((2, tm, D), lambda i: (0, i, 0)), _resident((1, D), lambda i: (0, 0)),
                  _resident((1, D), lambda i: (0, 0)), _resident((G, C, C), lambda i: (0, 0, 0)),
                  _resident((G, C, 1), lambda i: (0, 0, 0))],
        out_specs=pl.BlockSpec((tm, D), lambda i: (i, 0)),
        compiler_params=_params("arbitrary"),
    )(proj, gn, bn, ws, bsc)


def _decay_tables(dl_ref):
    lg = jax.nn.log_sigmoid(dl_ref[0:2, :])
    lgf, lgb = lg[0:1, :], lg[1:2, :]
    assert RC <= DK
    ri = lax.broadcasted_iota(jnp.int32, (RC, RC), 0)
    ci = lax.broadcasted_iota(jnp.int32, (RC, RC), 1)
    d = (ri - ci).astype(f32)
    lower = d >= 0
    dmat = jnp.where(lower, jnp.exp(d * lgf[:, :RC]), jnp.exp(-d * lgb[:, :RC]))
    dmat_t = jnp.where(d <= 0, jnp.exp(-d * lgf[:, :RC]), jnp.exp(d * lgb[:, :RC]))
    pos = lax.broadcasted_iota(jnp.int32, (RC, DK), 0).astype(f32)
    t = dict(
        lgf=lgf, lgb=lgb, d=d, lower=lower, dmat=dmat, dmat_t=dmat_t, pos=pos,
        fq=jnp.exp((pos + 1.0) * lgf), fk=jnp.exp((RC - 1.0 - pos) * lgf),
        bq=jnp.exp((RC - pos) * lgb), bk=jnp.exp(pos * lgb),
        lamf=jnp.exp(float(RC) * lgf), lamb=jnp.exp(float(RC) * lgb),
    )
    return t


def _rotate(t, co, si):
    t1, t2 = t[:, :DK // 2], t[:, DK // 2:]
    return jnp.concatenate([t1 * co - t2 * si, t2 * co + t1 * si], axis=-1)


def _unrotate(t, co, si):
    t1, t2 = t[:, :DK // 2], t[:, DK // 2:]
    return jnp.concatenate([t1 * co + t2 * si, t2 * co - t1 * si], axis=-1)


K_SCALE = DK ** -0.5
ROW_TILE = 256


def _ret_fwd(proj, dl, hooks=()):
    S = proj.shape[1]
    NC = S // RC

    def body(q_ref, k_ref, v_ref, g_ref, dl_ref, R_ref, r_ref, sfs_ref, sbs_ref, rb_ref, sf_ref, sb_ref):
        t = _decay_tables(dl_ref)

        def chunk(n):
            rows = pl.ds(pl.multiple_of(n * RC, RC), RC)
            return rows, q_ref[rows, :], k_ref[rows, :], v_ref[rows, :]

        sf_ref[...] = jnp.zeros_like(sf_ref)
        sb_ref[...] = jnp.zeros_like(sb_ref)

        def step(i, carry):
            rows, qn, kn, vn = chunk(i)
            sc = _mm_nt(qn, kn) * t["dmat"]
            out = _mm(sc.astype(CDT), vn)
            sf = sf_ref[...]
            sfb = sf.astype(CDT)
            sfs_ref[i] = sfb
            R_ref[rows, :] = out + _mm((qn.astype(f32) * t["fq"]).astype(CDT), sfb)
            sf_ref[...] = sf * t["lamf"] + _mm_tn((kn.astype(f32) * t["fk"]).astype(CDT), vn)
            m = NC - 1 - i
            rows, qn, kn, vn = chunk(m)
            sb = sb_ref[...]
            sbb = sb.astype(CDT)
            sbs_ref[m] = sbb
            rb_ref[rows, :] = _mm((qn.astype(f32) * t["bq"]).astype(CDT), sbb)
            sb_ref[...] = sb * t["lamb"] + _mm_tn((kn.astype(f32) * t["bk"]).astype(CDT), vn)
            return carry

        lax.fori_loop(0, NC, step, 0)

        def finish(i, carry):
            rs = pl.ds(pl.multiple_of(i * ROW_TILE, ROW_TILE), ROW_TILE)
            R = R_ref[rs, :] + rb_ref[rs, :]
            R_ref[rs, :] = R
            rn = R * lax.rsqrt(jnp.mean(R * R, axis=-1, keepdims=True) + EPS)
            g = g_ref[rs, :].astype(f32)
            r_ref[rs, :] = (rn * g * jax.nn.sigmoid(g)).astype(CDT)
            return carry

        lax.fori_loop(0, S // ROW_TILE, finish, 0)

    def seg(slot):
        return pl.BlockSpec((None, S, DK), lambda h: (slot, 0, h))

    states = jax.ShapeDtypeStruct((H, NC, DK, DK), CDT)
    state_blk = pl.BlockSpec((None, NC, DK, DK), lambda h: (h, 0, 0, 0))
    return _call(
        body, hooks, [proj, proj, proj, proj, dl], name="ret_fwd", grid=(H,),
        out_shape=[jax.ShapeDtypeStruct((S, H * DK), f32), jax.ShapeDtypeStruct((S, H * DK), CDT), states, states],
        in_specs=[seg(SLOT_Q), seg(SLOT_K), seg(SLOT_VR), seg(SLOT_GR), pl.BlockSpec((None, 8, DK), lambda h: (h, 0, 0))],
        out_specs=[pl.BlockSpec((S, DK), lambda h: (0, h)), pl.BlockSpec((S, DK), lambda h: (0, h)), state_blk, state_blk],
        scratch_shapes=[pltpu.VMEM((S, DK), f32), pltpu.VMEM((DK, DK), f32), pltpu.VMEM((DK, DK), f32)],
        compiler_params=_params("arbitrary"),
    )


def _merge_fwd(a, r, proj, x1, wa, wb, wo, hooks=(), tm=512):
    S = x1.shape[0]

    def body(a_ref, r_ref, gt_ref, x_ref, wa_ref, wb_ref, wo_ref, x2_ref, ya_ref, yb_ref):
        ya = _mm(a_ref[...], wa_ref[...])
        yb = _mm(r_ref[...], wb_ref[...])
        ya_ref[...] = ya.astype(CDT)
        yb_ref[...] = yb.astype(CDT)
        mix = jax.nn.sigmoid(gt_ref[0].astype(f32)) * ya + jax.nn.sigmoid(gt_ref[1].astype(f32)) * yb
        x2_ref[...] = x_ref[...] + _mm(mix.astype(CDT), wo_ref[...])

    tok = pl.BlockSpec((tm, D), lambda i: (i, 0))
    return _call(
        body, hooks, [a, r, proj, x1, wa, wb, wo], name="merge_fwd", grid=(S // tm,),
        out_shape=[jax.ShapeDtypeStruct((S, D), f32), jax.ShapeDtypeStruct((S, D), CDT), jax.ShapeDtypeStruct((S, D), CDT)],
        in_specs=[tok, tok, pl.BlockSpec((2, tm, D), lambda i: (SLOT_GA // 2, i, 0)), tok,
                  _wblock(wa), _wblock(wb), _wblock(wo)],
        out_specs=[tok, tok, tok],
        compiler_params=_params("arbitrary"),
    )


def _loss_head(x3, fn, target, tm=512):
    S = x3.shape[0]

    def body(x_ref, n_ref, t_ref, dx_ref, dxh_ref, dn_ref, l_ref):
        n = n_ref[...]
        r, xh, y = _rms_fwd(x_ref[...], n)
        e = y - t_ref[...]
        dy = e * (1.0 / D)
        dx, dn = _rms_bwd(dy, r, xh, n)
        dx_ref[...] = dx
        dxh_ref[...] = (0.5 * dx).astype(CDT)
        part = 0.5 * jnp.sum(jnp.sum(e * e, axis=-1, keepdims=True), axis=0, keepdims=True) * (1.0 / D)

        @pl.when(pl.program_id(0) == 0)
        def _():
            dn_ref[...] = jnp.zeros_like(dn_ref)
            l_ref[...] = jnp.zeros_like(l_ref)

        dn_ref[...] += dn
        l_ref[...] += jnp.broadcast_to(part, l_ref.shape)

    tok = pl.BlockSpec((tm, D), lambda i: (i, 0))
    return pl.pallas_call(
        body, name="loss_head", grid=(S // tm,),
        out_shape=[jax.ShapeDtypeStruct((S, D), f32), jax.ShapeDtypeStruct((S, D), CDT), jax.ShapeDtypeStruct((1, D), f32),
                   jax.ShapeDtypeStruct((8, 128), f32)],
        in_specs=[tok, _resident((1, D), lambda i: (0, 0)), tok],
        out_specs=[tok, tok, pl.BlockSpec((1, D), lambda i: (0, 0)), pl.BlockSpec((8, 128), lambda i: (0, 0))],
        compiler_params=_params("arbitrary"),
    )(x3, fn, target)


def _ffn_bwd_hidden(name, dyh, g, u, wd, hooks=(), tm=512):
    S = dyh.shape[0]

    def body(dyh_ref, g_ref, u_ref, wd_ref, dg_ref, du_ref):
        dyh = dyh_ref[...]
        for ci in range(F // FT):
            sl = slice(ci * FT, (ci + 1) * FT)
            da = _mm_nt(dyh, wd_ref[sl, :])
            gv = g_ref[:, sl].astype(f32)
            uv = u_ref[:, sl].astype(f32)
            s = jax.nn.sigmoid(gv)
            du_ref[:, sl] = (da * (gv * s)).astype(CDT)
            dg_ref[:, sl] = (da * uv * (s * (1.0 + gv * (1.0 - s)))).astype(CDT)

    hid = pl.BlockSpec((tm, F), lambda i: (i, 0))
    hidden = jax.ShapeDtypeStruct((S, F), CDT)
    return _call(
        body, hooks, [dyh, g, u, wd], name=name, grid=(S // tm,), out_shape=[hidden, hidden],
        in_specs=[pl.BlockSpec((tm, D), lambda i: (i, 0)), hid, hid, _wblock(wd)], out_specs=[hid, hid],
        compiler_params=_params("arbitrary"),
    )


def _ffn_bwd_in(name, dy, x, dg, du, nrm, wg, wu, hooks=(), part=None, prev=None, tm=512):
    S = x.shape[0]
    t0, nt = part or (0, S // tm)

    def body(dy_ref, x_ref, dg_ref, du_ref, n_ref, wg_ref, wu_ref, *rest):
        dx_ref, dn_ref, acc_ref = rest[-3:]
        n = n_ref[...]
        r, xh, _ = _rms_fwd(x_ref[...], n)
        for ci in range(F // FT):
            sl = slice(ci * FT, (ci + 1) * FT)
            dh = _mm(dg_ref[:, sl], wg_ref[sl, :]) + _mm(du_ref[:, sl], wu_ref[sl, :])
            if ci == 0:
                acc_ref[...] = dh
            else:
                acc_ref[...] += dh
        dx, dn = _rms_bwd(acc_ref[...], r, xh, n)
        dx_ref[...] = dy_ref[...] + dx

        @pl.when(pl.program_id(0) == 0)
        def _():
            dn_ref[...] = jnp.zeros_like(dn_ref) if prev is None else rest[1][...]

        dn_ref[...] += dn

    tok = pl.BlockSpec((tm, D), lambda i: (t0 + i, 0))
    hid = pl.BlockSpec((tm, F), lambda i: (t0 + i, 0))
    row = pl.BlockSpec((1, D), lambda i: (0, 0))
    in_specs = [tok, tok, hid, hid, _resident((1, D), lambda i: (0, 0)), _wblock(wg), _wblock(wu)]
    operands = [dy, x, dg, du, nrm, wg, wu]
    aliases = {}
    if prev is not None:
        in_specs += [_HBM, row]
        operands += list(prev)
        aliases = {7: 0}
    return _call(
        body, hooks, operands, name=name, grid=(nt,),
        out_shape=[jax.ShapeDtypeStruct((S, D), f32), jax.ShapeDtypeStruct((1, D), f32)],
        in_specs=in_specs, out_specs=[tok, row],
        scratch_shapes=[pltpu.VMEM((tm, D), f32)],
        input_output_aliases=aliases,
        compiler_params=_params("arbitrary"),
    )


TN_ROWS = 512


def _tn(name, xs, ys, block_of, hooks=()):
    S, M = xs.shape
    B = ys.shape[0]
    tr = TN_ROWS if M % TN_ROWS == 0 else M // 2
    assert M % tr == 0 and tr % 128 == 0
    nt = M // tr

    def body(x_ref, y_ref, o_ref):
        o_ref[...] = _mm_tn(x_ref[...], y_ref[...]).astype(CDT)

    return _call(
        body, hooks, [xs, ys], name=name, grid=(B, nt),
        out_shape=[jax.ShapeDtypeStruct((B * M, D), CDT)],
        in_specs=[pl.BlockSpec((S, tr), lambda b, i: (0, i)), pl.BlockSpec((None, S, D), lambda b, i: (b, 0, 0))],
        out_specs=[pl.BlockSpec((tr, D), lambda b, i: (block_of(b) * nt + i, 0))],
        compiler_params=_params("arbitrary", "arbitrary"),
    )


def _wgrad(name, xs, y, hooks=()):
    return _tn(name, xs, y[None], lambda b: 0, hooks)


def _merge_bwd_act(dx2, ya, yb, proj, wa, wb, wo, hooks=(), tm=512):
    S = dx2.shape[0]

    def body(dx_ref, ya_ref, yb_ref, gt_ref, wa_ref, wb_ref, wo_ref,
             dp_ref, da_ref, dr_ref, mix_ref, dxb_ref, dya_ref, dyb_ref):
        dxb = dx_ref[...].astype(CDT)
        dxb_ref[...] = dxb
        dmix = _mm_nt(dxb, wo_ref[...])
        ya = ya_ref[...].astype(f32)
        yb = yb_ref[...].astype(f32)
        sa = jax.nn.sigmoid(gt_ref[0].astype(f32))
        sb = jax.nn.sigmoid(gt_ref[1].astype(f32))
        mix_ref[...] = (sa * ya + sb * yb).astype(CDT)
        dya = (dmix * sa).astype(CDT)
        dyb = (dmix * sb).astype(CDT)
        dya_ref[...] = dya
        dyb_ref[...] = dyb
        dp_ref[0] = (dmix * ya * sa * (1.0 - sa)).astype(CDT)
        dp_ref[1] = (dmix * yb * sb * (1.0 - sb)).astype(CDT)
        da_ref[...] = _mm_nt(dya, wa_ref[...]).astype(CDT)
        dr_ref[...] = _mm_nt(dyb, wb_ref[...]).astype(CDT)

    tok = pl.BlockSpec((tm, D), lambda i: (i, 0))
    gates = pl.BlockSpec((2, tm, D), lambda i: (SLOT_GA // 2, i, 0))
    act = jax.ShapeDtypeStruct((S, D), CDT)
    return _call(
        body, hooks, [dx2, ya, yb, proj, wa, wb, wo], name="merge_bwd_act", grid=(S // tm,),
        out_shape=[jax.ShapeDtypeStruct((8, S, D), CDT), act, act, act, act, act, act],
        in_specs=[tok, tok, tok, gates, _wblock(wa), _wblock(wb), _wblock(wo)],
        out_specs=[gates, tok, tok, tok, tok, tok, tok],
        compiler_params=_params("arbitrary"),
    )


def _sgu_bwd(da, proj, dproj, gn, bn, ws, wst, bsc, hooks=(), tm=512):
    S = proj.shape[1]
    GW = D // G

    def body(da_ref, p_ref, dpin_ref, gn_ref, bn_ref, ws_ref, wst_ref, bs_ref,
             dp_ref, dws_ref, dbs_ref, dgn_ref, dbn_ref, ds_ref, dvn_ref):
        @pl.when(pl.program_id(0) == 0)
        def _():
            dws_ref[...] = jnp.zeros_like(dws_ref)
            dbs_ref[...] = jnp.zeros_like(dbs_ref)
            dgn_ref[...] = jnp.zeros_like(dgn_ref)
            dbn_ref[...] = jnp.zeros_like(dbn_ref)

        pu = p_ref[0].astype(f32)
        pv = p_ref[1].astype(f32)
        ua = _gelu(pu)
        va = _gelu(pv)
        gn = gn_ref[...]
        rstd, vhat, vn = _sgu_norm(va, gn, bn_ref[...])
        vnb = vn.astype(CDT)
        dav = da_ref[...].astype(f32)
        dsb = (dav * ua).astype(CDT)
        ones = jnp.ones((8, GW), CDT)
        for ch in range(tm // C):
            rs = slice(ch * C, (ch + 1) * C)
            for gi in range(G):
                cs = slice(gi * GW, (gi + 1) * GW)
                s = _mm(ws_ref[gi], vnb[rs, cs]) + bs_ref[gi]
                ds_ref[rs, cs] = s
                dsg = dsb[rs, cs]
                dws_ref[gi] += _mm_nt(dsg, vnb[rs, cs])
                dbs_ref[gi] += _mm_nt(ones, dsg)
                dvn_ref[rs, cs] = _mm(wst_ref[gi], dsg)
        dp_ref[0] = (dav * ds_ref[...] * _gelu_grad(pu)).astype(CDT)
        dvn = dvn_ref[...]
        dgn_ref[...] += jnp.sum(dvn * vhat, axis=0, keepdims=True)
        dbn_ref[...] += jnp.sum(dvn, axis=0, keepdims=True)
        dvh = dvn * gn
        dva = rstd * (dvh - jnp.mean(dvh, axis=-1, keepdims=True) - vhat * jnp.mean(dvh * vhat, axis=-1, keepdims=True))
        dp_ref[1] = (dva * _gelu_grad(pv)).astype(CDT)

    uv = pl.BlockSpec((2, tm, D), lambda i: (0, i, 0))
    row = _resident((1, D), lambda i: (0, 0))
    return _call(
        body, hooks, [da, proj, dproj, gn, bn, ws, wst, bsc], name="sgu_bwd", grid=(S // tm,),
        out_shape=[jax.ShapeDtypeStruct(dproj.shape, CDT), jax.ShapeDtypeStruct((G, C, C), f32),
                   jax.ShapeDtypeStruct((G, 8, C), f32), jax.ShapeDtypeStruct((1, D), f32), jax.ShapeDtypeStruct((1, D), f32)],
        in_specs=[pl.BlockSpec((tm, D), lambda i: (i, 0)), uv, _HBM, row, row,
                  _resident((G, C, C), lambda i: (0, 0, 0)), _resident((G, C, C), lambda i: (0, 0, 0)),
                  _resident((G, C, 1), lambda i: (0, 0, 0))],
        out_specs=[uv, pl.BlockSpec((G, C, C), lambda i: (0, 0, 0)), pl.BlockSpec((G, 8, C), lambda i: (0, 0, 0)),
                   pl.BlockSpec((1, D), lambda i: (0, 0)), pl.BlockSpec((1, D), lambda i: (0, 0))],
        scratch_shapes=[pltpu.VMEM((tm, D), f32), pltpu.VMEM((tm, D), f32)],
        input_output_aliases={2: 0},
        compiler_params=_params("arbitrary"),
    )


def _ret_bwd(dr, R, sfs, sbs, proj, dproj, cos, sin, dl, hooks=()):
    S = proj.shape[1]
    NC = S // RC
    assert NC % 2 == 0

    def body(dr_ref, R_ref, sf_ref, sb_ref, q_ref, k_ref, v_ref, g_ref, dpin_ref, cos_ref, sin_ref, dl_ref,
             dp_ref, dd_ref, dR_ref, gb_ref, gf_ref, acc_ref):
        t = _decay_tables(dl_ref)

        def gate_norm_bwd(i, carry):
            rs = pl.ds(pl.multiple_of(i * ROW_TILE, ROW_TILE), ROW_TILE)
            Rv = R_ref[rs, :]
            rstd = lax.rsqrt(jnp.mean(Rv * Rv, axis=-1, keepdims=True) + EPS)
            rn = Rv * rstd
            gv = g_ref[rs, :].astype(f32)
            s = jax.nn.sigmoid(gv)
            drv = dr_ref[rs, :].astype(f32)
            dp_ref[3, rs, :] = (drv * rn * (s * (1.0 + gv * (1.0 - s)))).astype(CDT)
            drn = drv * gv * s
            dR_ref[rs, :] = (rstd * (drn - rn * jnp.mean(drn * rn, axis=-1, keepdims=True))).astype(CDT)
            return carry

        lax.fori_loop(0, S // ROW_TILE, gate_norm_bwd, 0)

        def chunk(n):
            rows = pl.ds(pl.multiple_of(n * RC, RC), RC)
            return rows, q_ref[rows, :], k_ref[rows, :], v_ref[rows, :], dR_ref[rows, :]

        def emit_kv(rows, dk, dv, final):
            if not final:
                dp_ref[1, rows, :] = dk.astype(CDT)
                dp_ref[2, rows, :] = dv.astype(CDT)
            else:
                co, si = cos_ref[rows, :], sin_ref[rows, :]
                dk = dp_ref[1, rows, :].astype(f32) + dk
                dp_ref[1, rows, :] = (_unrotate(dk, co, si) * K_SCALE).astype(CDT)
                dp_ref[2, rows, :] = (dp_ref[2, rows, :].astype(f32) + dv).astype(CDT)

        gb_ref[...] = jnp.zeros_like(gb_ref)
        gf_ref[...] = jnp.zeros_like(gf_ref)
        acc_ref[...] = jnp.zeros_like(acc_ref)
        dpos = jnp.abs(t["d"])

        def ascend(n, final):
            rows, qn, kn, vn, dRn = chunk(n)
            qf, kf = qn.astype(f32), kn.astype(f32)
            sc = _mm_nt(qn, kn)
            dA = _mm_nt(dRn, vn)
            w = sc * dA * t["dmat"] * dpos
            lgf_part = jnp.sum(jnp.where(t["lower"], w, 0.0), axis=0, keepdims=True)
            lgb_part = jnp.sum(jnp.where(t["lower"], 0.0, w), axis=0, keepdims=True)
            dsc = (dA * t["dmat"]).astype(CDT)
            dq = _mm(dsc, kn)
            scT = (_mm_nt(kn, qn) * t["dmat_t"]).astype(CDT)
            dscT = (_mm_nt(vn, dRn) * t["dmat_t"]).astype(CDT)
            dk = _mm(dscT, qn)
            dv = _mm(scT, dRn)
            sfb = sf_ref[n]
            sbb = sb_ref[n]
            qdf = qf * t["fq"]
            dqdf = _mm_nt(dRn, sfb)
            dq += dqdf * t["fq"]
            lgf_row = jnp.sum(qdf * dqdf * (t["pos"] + 1.0), axis=0, keepdims=True)
            qdb = qf * t["bq"]
            dqdb = _mm_nt(dRn, sbb)
            dq += dqdb * t["bq"]
            lgb_row = jnp.sum(qdb * dqdb * (RC - t["pos"]), axis=0, keepdims=True)
            gb = gb_ref[...]
            gbb = gb.astype(CDT)
            kdb = kf * t["bk"]
            dkdb = _mm_nt(vn, gbb)
            dk += dkdb * t["bk"]
            dv += _mm(kdb.astype(CDT), gbb)
            lgb_row += jnp.sum(kdb * dkdb * t["pos"], axis=0, keepdims=True)
            lgb_row += float(RC) * t["lamb"] * jnp.sum(gb * sbb.astype(f32), axis=0, keepdims=True)
            co, si = cos_ref[rows, :], sin_ref[rows, :]
            dp_ref[0, rows, :] = _unrotate(dq, co, si).astype(CDT)
            emit_kv(rows, dk, dv, final)
            acc_ref[0:1, :] += lgf_row + lgf_part
            acc_ref[1:2, :] += lgb_row + lgb_part
            gb_ref[...] = gb * t["lamb"] + _mm_tn(qdb.astype(CDT), dRn)

        def descend(n, final):
            rows, qn, kn, vn, dRn = chunk(n)
            gf = gf_ref[...]
            gfb = gf.astype(CDT)
            kdf = kn.astype(f32) * t["fk"]
            dkdf = _mm_nt(vn, gfb)
            lgf_row = jnp.sum(kdf * dkdf * (RC - 1.0 - t["pos"]), axis=0, keepdims=True)
            lgf_row += float(RC) * t["lamf"] * jnp.sum(gf * sf_ref[n].astype(f32), axis=0, keepdims=True)
            acc_ref[0:1, :] += lgf_row
            emit_kv(rows, dkdf * t["fk"], _mm(kdf.astype(CDT), gfb), final)
            gf_ref[...] = gf * t["lamf"] + _mm_tn((qn.astype(f32) * t["fq"]).astype(CDT), dRn)

        def sweep(final):
            def step(i, carry):
                ascend(i, final)
                descend(NC - 1 - i, final)
                return carry
            return step

        lax.fori_loop(0, NC // 2, sweep(False), 0)
        lax.fori_loop(NC // 2, NC, sweep(True), 0)
        dlg = jnp.sum(acc_ref[...], axis=1, keepdims=True)
        dlogit = dlg * jax.nn.sigmoid(-dl_ref[:, 0:1])
        lane = lax.broadcasted_iota(jnp.int32, (8, 128), 1)
        dd_ref[...] = jnp.where(lane == pl.program_id(0), jnp.broadcast_to(dlogit, (8, 128)), 0.0)

    def seg(slot):
        return pl.BlockSpec((None, S, DK), lambda h: (slot, 0, h))

    head = pl.BlockSpec((S, DK), lambda h: (0, h))
    states = pl.BlockSpec((None, NC, DK, DK), lambda h: (h, 0, 0, 0))
    return _call(
        body, hooks, [dr, R, sfs, sbs, proj, proj, proj, proj, dproj, cos, sin, dl], name="ret_bwd", grid=(H,),
        out_shape=[jax.ShapeDtypeStruct(dproj.shape, CDT), jax.ShapeDtypeStruct((H, 8, 128), f32)],
        in_specs=[head, head, states, states, seg(SLOT_Q), seg(SLOT_K), seg(SLOT_VR), seg(SLOT_GR), _HBM,
                  _resident((S, DK // 2), lambda h: (0, 0)), _resident((S, DK // 2), lambda h: (0, 0)),
                  pl.BlockSpec((None, 8, DK), lambda h: (h, 0, 0))],
        out_specs=[pl.BlockSpec((4, S, DK), lambda h: (1, 0, h), pipeline_mode=pl.Buffered(1)),
                   pl.BlockSpec((None, 8, 128), lambda h: (h, 0, 0))],
        scratch_shapes=[pltpu.VMEM((S, DK), CDT),
                        pltpu.VMEM((DK, DK), f32), pltpu.VMEM((DK, DK), f32), pltpu.VMEM((8, DK), f32)],
        input_output_aliases={8: 0},
        compiler_params=_params("arbitrary"),
    )


def _proj_bwd_act(dproj, dx2, x1, nrm, wfull, hooks=(), tm=512):
    S = x1.shape[0]

    def body(dp_ref, dx2_ref, x_ref, n_ref, w_ref, dx_ref, dxh_ref, dn_ref, db_ref, acc_ref):
        @pl.when(pl.program_id(0) == 0)
        def _():
            dn_ref[...] = jnp.zeros_like(dn_ref)
            db_ref[...] = jnp.zeros_like(db_ref)

        for p in range(8):
            seg = SEG_OF_SLOT[p]
            dp = dp_ref[p]
            db_ref[seg] += jnp.sum(dp.astype(f32), axis=0, keepdims=True)
            dh = _mm_nt(dp, w_ref[seg * D:(seg + 1) * D, :])
            if p == 0:
                acc_ref[...] = dh
            else:
                acc_ref[...] += dh
        n = n_ref[...]
        r, xh, _ = _rms_fwd(x_ref[...], n)
        dx, dn = _rms_bwd(acc_ref[...], r, xh, n)
        dx = dx2_ref[...] + dx
        dx_ref[...] = dx
        dxh_ref[...] = (0.5 * dx).astype(CDT)
        dn_ref[...] += dn

    tok = pl.BlockSpec((tm, D), lambda i: (i, 0))
    return _call(
        body, hooks, [dproj, dx2, x1, nrm, wfull], name="proj_bwd_act", grid=(S // tm,),
        out_shape=[jax.ShapeDtypeStruct((S, D), f32), jax.ShapeDtypeStruct((S, D), CDT), jax.ShapeDtypeStruct((1, D), f32),
                   jax.ShapeDtypeStruct((8, 1, D), f32)],
        in_specs=[pl.BlockSpec((8, tm, D), lambda i: (0, i, 0)), tok, tok, _resident((1, D), lambda i: (0, 0)),
                  _resident((8 * D, D), lambda i: (0, 0))],
        out_specs=[tok, tok, pl.BlockSpec((1, D), lambda i: (0, 0)), pl.BlockSpec((8, 1, D), lambda i: (0, 0, 0))],
        scratch_shapes=[pltpu.VMEM((tm, D), f32)],
        compiler_params=_params("arbitrary"),
    )


def _rs_sum(name, gfulls, lands, my_c):
    n = len(gfulls)
    rows = gfulls[0].shape[0] // NDEV
    assert all(g.shape[0] == NDEV * rows for g in gfulls)

    def body(c_ref, *refs):
        for g_ref, l_ref, o_ref in zip(refs[:n], refs[n:2 * n], refs[2 * n:]):
            o_ref[...] = (g_ref[...].astype(f32) + l_ref[...].astype(f32)).astype(CDT)

    slot = pl.BlockSpec((None, rows, D), lambda k, c: (k, 0, 0))
    return pl.pallas_call(
        body, name=name,
        grid_spec=pltpu.PrefetchScalarGridSpec(
            num_scalar_prefetch=1, grid=(NCHIP,),
            in_specs=[pl.BlockSpec((rows, D), lambda k, c: (2 * k + c[0], 0))] * n + [slot] * n,
            out_specs=[slot] * n),
        out_shape=[jax.ShapeDtypeStruct((NCHIP, rows, D), CDT)] * n,
        compiler_params=_params("arbitrary"),
    )(my_c, *gfulls, *lands)


def _adamw_math(g, w, m, v):
    m2 = ADAM_B1 * m + (1.0 - ADAM_B1) * g
    v2 = ADAM_B2 * v + (1.0 - ADAM_B2) * (g * g)
    delta = -ADAM_LR * ((m2 / BC1) / (jnp.sqrt(v2 / BC2) + ADAM_EPS) + ADAM_WD * w)
    return delta, m2, v2


def _adamw_big(name, landed, w, m, v, after):
    n = len(w)
    rows = w[0].shape[0]
    tr = 256 if rows % 256 == 0 else (rows // 2 if rows > 256 else rows)
    nt = rows // tr

    def body(*refs):
        ins, outs = refs[:4 * n], refs[4 * n + 1:]
        for j in range(n):
            @pl.when(pl.program_id(0) == j)
            def _(j=j):
                l_ref, w_ref, m_ref, v_ref = ins[j], ins[n + j], ins[2 * n + j], ins[3 * n + j]
                g = l_ref[0].astype(f32)
                for k in range(1, NCHIP):
                    g = g + l_ref[k].astype(f32)
                outs[4 * j][...] = g
                outs[4 * j + 1][...], outs[4 * j + 2][...], outs[4 * j + 3][...] = _adamw_math(g, w_ref[...], m_ref[...], v_ref[...])

    def tile(j):
        return lambda jj, i: jnp.clip((jj - j) * nt + i, 0, nt - 1)

    blk = [pl.BlockSpec((tr, D), lambda jj, i, t=tile(j): (t(jj, i), 0)) for j in range(n)]
    lnd = [pl.BlockSpec((NCHIP, tr, D), lambda jj, i, t=tile(j): (0, t(jj, i), 0)) for j in range(n)]
    o = jax.ShapeDtypeStruct((rows, D), f32)
    res = pl.pallas_call(
        body, name=name, grid=(n, nt), out_shape=[o] * (4 * n),
        in_specs=lnd + blk + blk + blk + [_HBM],
        out_specs=[blk[j] for j in range(n) for _ in range(4)],
        compiler_params=_params("arbitrary", "arbitrary"),
    )(*landed, *w, *m, *v, after)
    return [res[4 * j:4 * j + 4] for j in range(n)]


ROW_FFN1_NORM, ROW_MIX_NORM, ROW_SGU_G, ROW_SGU_B, ROW_FFN2_NORM, ROW_FINAL_NORM, ROW_B_IN = 0, 1, 2, 3, 4, 5, 8
ROW_WS, ROW_BS, ROW_DECAY = 0, G * C, G * C + G * 8


def _adamw_small(ga, gb, gn1, gl, params):
    def body(ga_ref, gb_ref, gn1_ref, gl_ref, *refs):
        ins, outs = refs[:30], refs[30:]

        def total(ref, r0, n):
            g = ref[0, r0:r0 + n, :]
            for j in range(1, NDEV):
                g = g + ref[j, r0:r0 + n, :]
            return g

        def apply(i, g, rows=slice(None)):
            w, m, v = ins[3 * i][rows, :], ins[3 * i + 1][rows, :], ins[3 * i + 2][rows, :]
            outs[4 * i][rows, :] = g
            outs[4 * i + 1][rows, :], outs[4 * i + 2][rows, :], outs[4 * i + 3][rows, :] = _adamw_math(g, w, m, v)

        outs[40][...] = total(gl_ref, 0, 8)
        apply(0, total(gn1_ref, 0, 1))
        for i, r in enumerate((ROW_FFN1_NORM, ROW_MIX_NORM, ROW_SGU_G, ROW_SGU_B, ROW_FFN2_NORM, ROW_FINAL_NORM)):
            if i:
                apply(i, total(ga_ref, r, 1))
        apply(6, total(ga_ref, ROW_B_IN, 8))
        apply(7, total(gb_ref, ROW_WS, G * C))
        for gi in range(G):
            apply(8, total(gb_ref, ROW_BS + 8 * gi, 1), slice(gi, gi + 1))
        dec = total(gb_ref, ROW_DECAY, 8)
        for hh in range(1, H):
            dec = dec + total(gb_ref, ROW_DECAY + 8 * hh, 8)
        apply(9, dec)

    flat = [a for p in params for a in p]
    out_shape = [jax.ShapeDtypeStruct(p[0].shape, f32) for p in params for _ in range(4)]
    out_shape.append(jax.ShapeDtypeStruct((8, 128), f32))
    vm = pl.BlockSpec(memory_space=pltpu.VMEM)
    return pl.pallas_call(
        body, name="adamw_small", out_shape=out_shape,
        in_specs=[vm] * (4 + len(flat)), out_specs=[vm] * len(out_shape),
        compiler_params=pltpu.CompilerParams(vmem_limit_bytes=VMEM_LIMIT),
    )(ga, gb, gn1, gl, *flat)


def kernel(x, ffn1_norm, ffn1_w_gate, ffn1_w_up, ffn1_w_down, mix_norm, w_in, b_in, sgu_norm_g, sgu_norm_b, sgu_w_s, sgu_b_s, ret_decay_logit, w_branch_a, w_branch_b, w_out, ffn2_norm, ffn2_w_gate, ffn2_w_up, ffn2_w_down, final_norm, loss_target, m_ffn1_norm, m_ffn1_w_gate, m_ffn1_w_up, m_ffn1_w_down, m_mix_norm, m_w_in, m_b_in, m_sgu_norm_g, m_sgu_norm_b, m_sgu_w_s, m_sgu_b_s, m_ret_decay_logit, m_w_branch_a, m_w_branch_b, m_w_out, m_ffn2_norm, m_ffn2_w_gate, m_ffn2_w_up, m_ffn2_w_down, m_final_norm, v_ffn1_norm, v_ffn1_w_gate, v_ffn1_w_up, v_ffn1_w_down, v_mix_norm, v_w_in, v_b_in, v_sgu_norm_g, v_sgu_norm_b, v_sgu_w_s, v_sgu_b_s, v_ret_decay_logit, v_w_branch_a, v_w_branch_b, v_w_out, v_ffn2_norm, v_ffn2_w_gate, v_ffn2_w_up, v_ffn2_w_down, v_final_norm):
    args = dict(locals())
    S = x.shape[1]
    xs = x[0]
    target = loss_target[0]

    def buf_layout(name, a):
        a = a[0]
        return a.T if name in W_TRANSPOSED else a

    sh = {n: buf_layout(n, args[n]).astype(CDT) for n in W_NAMES}
    wf = {}

    b3 = b_in.reshape(8, 1, D)
    ws = sgu_w_s[0].astype(CDT)
    wst = jnp.swapaxes(sgu_w_s[0], 1, 2).astype(CDT)
    bsc = sgu_b_s[0].reshape(G, C, 1)
    dl = jnp.zeros((H, 8, DK), f32).at[:, 0:2, :].set(jnp.broadcast_to(ret_decay_logit[0].T[:, :, None], (H, 2, DK)))
    theta = ROPE_BASE ** (-jnp.arange(0, DK, 2, dtype=f32) / DK)
    ang = jnp.arange(S, dtype=f32)[:, None] * theta[None, :]
    cos, sin = jnp.cos(ang), jnp.sin(ang)
    fnorm = final_norm.reshape(1, D)

    f1 = ("ffn1_w_gate", "ffn1_w_up", "ffn1_w_down")
    f2 = ("ffn2_w_gate", "ffn2_w_up", "ffn2_w_down")
    br = ("w_branch_a", "w_branch_b", "w_out")
    for cid, names in enumerate((f1, ("w_in",), br, f2)):
        wf.update(zip(names, _sequence("ag_" + names[0], 1 + cid, NEIGHBOURS, [_ag_hook(sh[n]) for n in names])))
    x1, g1, u1, a1, hf1 = _ffn_fwd("ffn1_fwd", xs, ffn1_norm, *[wf[n] for n in f1])
    proj, h2 = _proj_fwd(x1, mix_norm, wf["w_in"], b3, cos, sin)
    a = _sgu_fwd(proj, sgu_norm_g, sgu_norm_b, ws, bsc)
    R, r, sfs, sbs = _ret_fwd(proj, dl)
    x2, ya, yb = _merge_fwd(a, r, proj, x1, *[wf[n] for n in br])
    x3, g2, u2, a2, hf2 = _ffn_fwd("ffn2_fwd", x2, ffn2_norm, *[wf[n] for n in f2])
    dx3, dyh2, d_final, loss_part = _loss_head(x3, fnorm, target)

    my_c = lax.axis_index("c").astype(jnp.int32).reshape(1)
    gw, landed, sequenced = {}, {}, []

    def d2d(*names):
        return [_rs_d2d_hook(gw[n]) for n in names]

    def behind(x, token):
        return lax.optimization_barrier((x, token))[0]

    def to_chips(names, sibs, more=()):
        parts = list(_rs_sum("rs_sum_" + names[0], [gw[n] for n in names], list(sibs), my_c))
        token = parts[0]
        if sequenced:
            parts[0] = behind(parts[0], sequenced[-1])
        hooks = [_rs_ici_hook(p) for p in parts] + ([_small_hook(list(more))] if more else [])
        reach = EVERYONE if more else CHIPS
        got = _sequence("rs_chips_" + names[0], REACH_ID[reach], reach, hooks)
        sequenced.append(got[0])
        landed.update(zip(names, got))
        return got[len(names):], token

    def ffn_bwd(tag, names, dy, dyh, x, g, u, a, h, nrm, each_alone, more=()):
        wg, wu, wd = names
        (gw[wd],) = _wgrad(tag + "_wd_grad", a, dyh)
        dg, du, sib_d = _ffn_bwd_hidden(tag + "_bwd_hidden", dyh, g, u, wf[wd], d2d(wd))
        if each_alone:
            dg = behind(dg, to_chips([wd], [sib_d])[1])
        (gw[wg],) = _wgrad(tag + "_wg_grad", dg, h)
        gw[wu], sib_g = _wgrad(tag + "_wu_grad", du, h, d2d(wg))
        if each_alone:
            dy = behind(dy, to_chips([wg], [sib_g])[1])
        dx, dn, sib_u = _ffn_bwd_in(tag + "_bwd_in", dy, x, dg, du, nrm, wf[wg], wf[wu], d2d(wu))
        if each_alone:
            return (dx, dn) + to_chips([wu], [sib_u], more(dn))
        return (dx, dn) + to_chips([wd, wg, wu], [sib_d, sib_g, sib_u])

    dx2, d_ffn2n, _, token = ffn_bwd("ffn2", f2, dx3, dyh2, x2, g2, u2, a2, hf2, ffn2_norm, False)
    dproj, da, dr, mix, dx2b, dya, dyb = _merge_bwd_act(behind(dx2, token), ya, yb, proj, *[wf[n] for n in br])
    dproj, d_ws, d_bs, d_gn, d_bn = _sgu_bwd(da, proj, dproj, sgu_norm_g, sgu_norm_b, ws, wst, bsc)
    dproj, d_dec = _ret_bwd(dr, R, sfs, sbs, proj, dproj, cos, sin, dl)
    (gw["w_in"],) = _tn("win_grad", h2, dproj, _seg_of_slot)
    gw["w_out"], sib_win = _wgrad("wo_grad", mix, dx2b, d2d("w_in"))
    small_sgu = jnp.concatenate([d_ws.reshape(G * C, C), d_bs.reshape(G * 8, C)], axis=0)
    (g_sgu, gl), token = to_chips(["w_in"], [sib_win], [small_sgu, loss_part])
    (gw["w_branch_a"],) = _wgrad("wa_grad", a, behind(dya, token))
    (gw["w_branch_b"],) = _wgrad("wb_grad", r, dyb)
    dx1, dyh1, d_mixn, d_bin, *sib_br = _proj_bwd_act(behind(dproj, token), dx2, x1, mix_norm, wf["w_in"], d2d(*br))
    small_a = jnp.concatenate([jnp.zeros((1, D), f32), d_mixn, d_gn, d_bn, d_ffn2n, d_final, jnp.zeros((2, D), f32),
                               d_bin.reshape(8, D)], axis=0)
    (ga, g_dec), token = to_chips(list(br), sib_br, [small_a, d_dec.reshape(H * 8, 128)])
    gb = jnp.concatenate([g_sgu, g_dec], axis=1)
    dxs, d_ffn1n, (gn1,), token = ffn_bwd("ffn1", f1, dx1, behind(dyh1, token), xs, g1, u1, a1, hf1, ffn1_norm, True,
                                          lambda dn: [dn])

    out = {"grad_x": dxs[None]}

    def native(name, a):
        a = a.T if name in W_TRANSPOSED else a
        return a[None]

    after = token
    for names in (f2, ("w_in",), br, (f1[2], f1[0], f1[1])):
        res = _adamw_big("adamw_" + names[0], [landed[n] for n in names], [buf_layout(n, args[n]) for n in names],
                         [buf_layout(n, args["m_" + n]) for n in names], [buf_layout(n, args["v_" + n]) for n in names], after)
        after = res[-1][0]
        for n, four in zip(names, res):
            for pre, val in zip(("grad_", "delta_", "new_m_", "new_v_"), four):
                out[pre + n] = native(n, val)

    def pad_decay(a):
        return jnp.zeros((8, 128), f32).at[0:2, 0:H].set(a[0])

    small = [
        ("ffn1_norm", lambda a: a, lambda a: a), ("mix_norm", lambda a: a, lambda a: a),
        ("sgu_norm_g", lambda a: a, lambda a: a), ("sgu_norm_b", lambda a: a, lambda a: a),
        ("ffn2_norm", lambda a: a, lambda a: a),
        ("final_norm", lambda a: a.reshape(1, D), lambda a: a.reshape(D)),
        ("b_in", lambda a: a.reshape(8, D), lambda a: a.reshape(1, 8 * D)),
        ("sgu_w_s", lambda a: a.reshape(G * C, C), lambda a: a.reshape(1, G, C, C)),
        ("sgu_b_s", lambda a: a[0], lambda a: a[None]),
        ("ret_decay_logit", pad_decay, lambda a: a[None, 0:2, 0:H]),
    ]
    res = _adamw_small(ga, gb, gn1, gl, [(to(args[n]), to(args["m_" + n]), to(args["v_" + n])) for n, to, _ in small])
    out["loss"] = res[40][0, 0]
    for i, (n, _, back) in enumerate(small):
        for j, pre in enumerate(("grad_", "delta_", "new_m_", "new_v_")):
            out[pre + n] = back(res[4 * i + j])

    weights = ("ffn1_norm", "ffn1_w_gate", "ffn1_w_up", "ffn1_w_down", "mix_norm", "w_in", "b_in", "sgu_norm_g",
               "sgu_norm_b", "sgu_w_s", "sgu_b_s", "ret_decay_logit", "w_branch_a", "w_branch_b", "w_out", "ffn2_norm",
               "ffn2_w_gate", "ffn2_w_up", "ffn2_w_down", "final_norm")
    return (out["loss"], out["grad_x"], *[out["grad_" + n] for n in weights], *[out["delta_" + n] for n in weights],
            *[out["new_m_" + n] for n in weights], *[out["new_v_" + n] for n in weights])
```

```python
import math

import jax
import jax.numpy as jnp
from jax import lax
from jax.experimental import pallas as pl
from jax.experimental.pallas import tpu as pltpu
from jax.experimental.pallas import tpu_sc as plsc

f32 = jnp.float32
CDT = jnp.bfloat16

D = 1024
F = 2816
C = 128
RC = 256
H = 4
DK = 256
G = 4
NDEV = 8
NCHIP = 4
EPS = 1e-6
ROPE_BASE = 10000.0
FT = 256
V7X_VMEM_BYTES = 64 * 1024 * 1024
VMEM_LIMIT = V7X_VMEM_BYTES - 8 * 1024 * 1024

ADAM_LR, ADAM_B1, ADAM_B2, ADAM_EPS, ADAM_WD, ADAM_STEP = 0.001, 0.9, 0.999, 1e-08, 0.01, 10
BC1 = 1.0 - ADAM_B1 ** ADAM_STEP
BC2 = 1.0 - ADAM_B2 ** ADAM_STEP

W_ROWS = dict(ffn1_w_gate=352, ffn1_w_up=352, ffn1_w_down=352, w_in=1024, w_branch_a=128, w_branch_b=128, w_out=128,
              ffn2_w_gate=352, ffn2_w_up=352, ffn2_w_down=352)
W_NAMES = tuple(W_ROWS)
W_TRANSPOSED = ("ffn1_w_gate", "ffn1_w_up", "ffn2_w_gate", "ffn2_w_up")

SLOT_U, SLOT_V, SLOT_GA, SLOT_GB, SLOT_Q, SLOT_K, SLOT_VR, SLOT_GR = range(8)


SEG_OF_SLOT = (0, 1, 6, 7, 2, 3, 4, 5)


def _seg_of_slot(p):
    return jnp.where(p < 2, p, jnp.where(p < 4, p + 4, p - 2))


def _mm(a, b):
    return jnp.dot(a, b, preferred_element_type=f32)


def _mm_nt(a, b):
    return lax.dot_general(a, b, (((1,), (1,)), ((), ())), preferred_element_type=f32)


def _mm_tn(a, b):
    return lax.dot_general(a, b, (((0,), (0,)), ((), ())), preferred_element_type=f32)


def _params(*sem):
    return pltpu.CompilerParams(dimension_semantics=sem, vmem_limit_bytes=VMEM_LIMIT)


def _resident(shape, index_map):
    return pl.BlockSpec(shape, index_map, pipeline_mode=pl.Buffered(1))


def _gelu(x):
    return 0.5 * x * (1.0 + lax.erf(x * (1.0 / math.sqrt(2.0))))


def _gelu_grad(x):
    return 0.5 * (1.0 + lax.erf(x * (1.0 / math.sqrt(2.0)))) + x * jnp.exp(-0.5 * x * x) * (1.0 / math.sqrt(2.0 * math.pi))


def _rms_fwd(x, n):
    r = lax.rsqrt(jnp.mean(x * x, axis=-1, keepdims=True) + EPS)
    xh = x * r
    return r, xh, xh * n


def _rms_bwd(dh, r, xh, n):
    dxh = dh * n
    dx = r * (dxh - xh * jnp.mean(dxh * xh, axis=-1, keepdims=True))
    return dx, jnp.sum(dh * xh, axis=0, keepdims=True)


MESH_ID = pl.DeviceIdType.MESH
_HBM = pl.BlockSpec(memory_space=pltpu.HBM)


def _my_place():
    return lax.axis_index("x"), lax.axis_index("y"), lax.axis_index("c")


def _ici_peers(x, y, c):
    return [((1 - x, y, c), 2 * (1 - x) + y), ((x, 1 - y, c), 2 * x + 1 - y), ((1 - x, 1 - y, c), 2 * (1 - x) + 1 - y)]


class _Hook:
    def __init__(self, operands, out_shapes, n_remote, n_local, start, finish, relay=None):
        self.operands, self.out_shapes = list(operands), list(out_shapes)
        self.n_remote, self.n_local, self.start, self.finish = n_remote, n_local, start, finish
        self.relay = relay or (lambda *a: None)


def _call(body, hooks, operands, *, in_specs, out_specs, out_shape, grid=None, scratch_shapes=(), **kw):
    hooks = tuple(hooks)
    n_in, n_out, n_scr = len(in_specs), len(out_shape), len(scratch_shapes)
    h_ops = [a for h in hooks for a in h.operands]
    h_outs = [s for h in hooks for s in h.out_shapes]
    h_sems = [pltpu.SemaphoreType.DMA((n,)) for h in hooks for n in (h.n_remote, h.n_remote, max(h.n_local, 1))]

    def wrapped(*refs):
        ins, hin = refs[:n_in], refs[n_in:n_in + len(h_ops)]
        o0 = n_in + len(h_ops)
        outs, hout = refs[o0:o0 + n_out], refs[o0 + n_out:o0 + n_out + len(h_outs)]
        s0 = o0 + n_out + len(h_outs)
        scr, hsem = refs[s0:s0 + n_scr], refs[s0 + n_scr:]

        def run(phase):
            ip = op = 0
            for i, h in enumerate(hooks):
                ssem, rsem, lsem = hsem[3 * i:3 * i + 3]

                def rcopy(k, src, dst, dev, ssem=ssem, rsem=rsem):
                    return pltpu.make_async_remote_copy(src_ref=src, dst_ref=dst, send_sem=ssem.at[k], recv_sem=rsem.at[k],
                                                        device_id=dev, device_id_type=MESH_ID)

                def lcopy(k, src, dst, lsem=lsem):
                    return pltpu.make_async_copy(src, dst, lsem.at[k])

                getattr(h, phase)(hin[ip:ip + len(h.operands)], hout[op:op + len(h.out_shapes)], rcopy, lcopy)
                ip += len(h.operands)
                op += len(h.out_shapes)

        def at_edge(phase, last):
            if not hooks:
                return
            if grid is None:
                run(phase)
                return
            cond = None
            for ax, n in enumerate(grid):
                here = pl.program_id(ax) == (n - 1 if last else 0)
                cond = here if cond is None else cond & here
            pl.when(cond)(lambda: run(phase))

        at_edge("start", False)
        at_edge("relay", True)
        body(*ins, *outs, *scr)
        at_edge("finish", True)

    if grid is not None:
        kw["grid"] = grid
    return list(pl.pallas_call(
        wrapped, out_shape=list(out_shape) + h_outs, in_specs=list(in_specs) + [_HBM] * len(h_ops),
        out_specs=list(out_specs) + [_HBM] * len(h_outs), scratch_shapes=list(scratch_shapes) + h_sems, **kw,
    )(*operands, *h_ops))


def _rows(ref, start, n):
    return ref.at[pl.ds(start, n), :]


def _ag_hook(shard):
    rows = shard.shape[0]
    half = rows // 2
    assert half % 16 == 0

    def place():
        x, y, c = _my_place()
        devs = dict(sib=(x, y, 1 - c), xn=(1 - x, y, c), yn=(x, 1 - y, c))
        chips = dict(me=2 * x + y, xn=2 * (1 - x) + y, yn=2 * x + 1 - y, dg=2 * (1 - x) + 1 - y)
        return c, devs, chips

    def block(full, chip, c):
        return _rows(full, (2 * chip + c) * rows, rows)

    def halfblock(full, chip, c, upper):
        return _rows(full, (2 * chip + c) * rows + upper * half, half)

    def start(ins, outs, rcopy, lcopy):
        c, devs, chips = place()
        src, dst = ins[0], block(outs[0], chips["me"], c)
        lcopy(0, src, dst).start()
        for k, to in enumerate(("sib", "xn", "yn")):
            rcopy(k, src, dst, devs[to]).start()

    def relay(ins, outs, rcopy, lcopy):
        c, devs, chips = place()
        full = outs[0]
        blk = block(full, chips["xn"], c)
        rcopy(1, blk, blk, devs["xn"]).wait_recv()
        low = halfblock(full, chips["xn"], c, 0)
        rcopy(3, low, low, devs["yn"]).start()
        rcopy(5, blk, blk, devs["sib"]).start()
        blk = block(full, chips["yn"], c)
        rcopy(2, blk, blk, devs["yn"]).wait_recv()
        up = halfblock(full, chips["yn"], c, 1)
        rcopy(4, up, up, devs["xn"]).start()
        rcopy(6, blk, blk, devs["sib"]).start()
        low, up = halfblock(full, chips["dg"], c, 0), halfblock(full, chips["dg"], c, 1)
        rcopy(3, low, low, devs["yn"]).wait_recv()
        rcopy(4, up, up, devs["xn"]).wait_recv()
        blk = block(full, chips["dg"], c)
        rcopy(7, blk, blk, devs["sib"]).start()

    def finish(ins, outs, rcopy, lcopy):
        c, devs, chips = place()
        full, sib = outs[0], devs["sib"]
        for k, chip in ((0, "me"), (5, "xn"), (6, "yn"), (7, "dg")):
            theirs = block(full, chips[chip], 1 - c)
            rcopy(k, theirs, theirs, sib).wait_recv()
            mine = block(full, chips[chip], c)
            if k:
                rcopy(k, mine, mine, sib).wait_send()
        src, dst = ins[0], block(full, chips["me"], c)
        lcopy(0, src, dst).wait()
        for k, to in enumerate(("sib", "xn", "yn")):
            rcopy(k, src, dst, devs[to]).wait_send()
        low, up = halfblock(full, chips["xn"], c, 0), halfblock(full, chips["yn"], c, 1)
        rcopy(3, low, low, devs["yn"]).wait_send()
        rcopy(4, up, up, devs["xn"]).wait_send()

    return _Hook([shard], [jax.ShapeDtypeStruct((NDEV * rows, D), shard.dtype)], 8, 1, start, finish, relay)


SIBLING, CHIPS, NEIGHBOURS, EVERYONE = "sibling", "chips", "sibling and the two neighbour chips", "everyone"
REACH_ID = {CHIPS: 6, EVERYONE: 7}


def _sequence(name, collective_id, reach, hooks):
    ins = [[jax.new_ref(a, memory_space=pltpu.MemorySpace.HBM) for a in h.operands] for h in hooks]
    outs = [[jax.empty_ref(s, memory_space=pltpu.MemorySpace.HBM) for s in h.out_shapes] for h in hooks]
    sems = tuple(pltpu.SemaphoreType.DMA((n,)) for h in hooks for n in (h.n_remote, h.n_remote, max(h.n_local, 1)))

    @pl.kernel(mesh=plsc.ScalarSubcoreMesh(axis_name="sequencer", num_cores=1), name=name, scratch_types=sems,
               compiler_params=pltpu.CompilerParams(collective_id=collective_id))
    def launch(*sem_refs):
        x, y, c = _my_place()
        chips = [dev for dev, _ in _ici_peers(x, y, c)]
        others = [(1 - x if dx else x, 1 - y if dy else y, 1 - c if dc else c)
                  for dx in range(2) for dy in range(2) for dc in range(2) if dx + dy + dc]
        devs = {SIBLING: [(x, y, 1 - c)], CHIPS: chips, NEIGHBOURS: [(x, y, 1 - c), (1 - x, y, c), (x, 1 - y, c)],
                EVERYONE: others}[reach]
        barrier = pltpu.get_barrier_semaphore()
        for dev in devs:
            pl.semaphore_signal(barrier, inc=1, device_id=dev, device_id_type=MESH_ID)
        pl.semaphore_wait(barrier, len(devs))
        for phase in ("start", "relay", "finish"):
            for i, h in enumerate(hooks):
                ssem, rsem, lsem = sem_refs[3 * i:3 * i + 3]

                def rcopy(k, src, dst, dev, ssem=ssem, rsem=rsem):
                    return pltpu.make_async_remote_copy(src_ref=src, dst_ref=dst, send_sem=ssem.at[k], recv_sem=rsem.at[k],
                                                        device_id=dev, device_id_type=MESH_ID)

                def lcopy(k, src, dst, lsem=lsem):
                    return pltpu.make_async_copy(src, dst, lsem.at[k])

                getattr(h, phase)(ins[i], outs[i], rcopy, lcopy)

    launch()
    return [o[...] for os in outs for o in os]


def _rs_d2d_hook(gfull):
    rows = gfull.shape[0] // NDEV

    def pairs(g, land):
        x, y, c = _my_place()
        return (x, y, 1 - c), [(k, _rows(g, (2 * k + 1 - c) * rows, rows), land.at[k]) for k in range(NCHIP)]

    def start(ins, outs, rcopy, lcopy):
        sib, cps = pairs(ins[0], outs[0])
        for i, src, dst in cps:
            rcopy(i, src, dst, sib).start()

    def finish(ins, outs, rcopy, lcopy):
        sib, cps = pairs(ins[0], outs[0])
        for i, src, dst in cps:
            rcopy(i, dst, dst, sib).wait_recv()
        for i, src, dst in cps:
            rcopy(i, src, dst, sib).wait_send()

    return _Hook([gfull], [jax.ShapeDtypeStruct((NCHIP, rows, D), gfull.dtype)], NCHIP, 0, start, finish)


def _rs_ici_hook(part):
    def start(ins, outs, rcopy, lcopy):
        x, y, c = _my_place()
        mychip = 2 * x + y
        lcopy(0, ins[0].at[mychip], outs[0].at[mychip]).start()
        for j, (dev, chip) in enumerate(_ici_peers(x, y, c)):
            rcopy(j, ins[0].at[chip], outs[0].at[mychip], dev).start()

    def finish(ins, outs, rcopy, lcopy):
        x, y, c = _my_place()
        mychip = 2 * x + y
        peers = _ici_peers(x, y, c)
        for j, (dev, chip) in enumerate(peers):
            rcopy(j, outs[0].at[chip], outs[0].at[chip], dev).wait_recv()
        for j, (dev, chip) in enumerate(peers):
            rcopy(j, ins[0].at[chip], outs[0].at[mychip], dev).wait_send()
        lcopy(0, ins[0].at[mychip], outs[0].at[mychip]).wait()

    return _Hook([part], [jax.ShapeDtypeStruct(part.shape, part.dtype)], 3, 1, start, finish)


def _small_hook(arrays):
    n = len(arrays)

    def peers():
        x, y, c = _my_place()
        out = []
        for dx in range(2):
            for dy in range(2):
                for dc in range(2):
                    if dx + dy + dc:
                        px, py, pc = (1 - x if dx else x), (1 - y if dy else y), (1 - c if dc else c)
                        out.append(((px, py, pc), 4 * px + 2 * py + pc))
        return 4 * x + 2 * y + c, out

    def start(ins, outs, rcopy, lcopy):
        me, ps = peers()
        for t in range(n):
            lcopy(t, ins[t], outs[t].at[me]).start()
            for i, (dev, _) in enumerate(ps):
                rcopy(n * i + t, ins[t], outs[t].at[me], dev).start()

    def finish(ins, outs, rcopy, lcopy):
        me, ps = peers()
        for t in range(n):
            for i, (dev, peer) in enumerate(ps):
                rcopy(n * i + t, outs[t].at[peer], outs[t].at[peer], dev).wait_recv()
            for i, (dev, _) in enumerate(ps):
                rcopy(n * i + t, ins[t], outs[t].at[me], dev).wait_send()
            lcopy(t, ins[t], outs[t].at[me]).wait()

    return _Hook(arrays, [jax.ShapeDtypeStruct((NDEV,) + a.shape, a.dtype) for a in arrays], 7 * n, n, start, finish)


def _wblock(w):
    return _resident(w.shape, lambda *_: (0, 0))


def _ffn_fwd(name, x, nrm, wg, wu, wd, hooks=(), tm=512):
    S = x.shape[0]

    def body(x_ref, n_ref, wg_ref, wu_ref, wd_ref, y_ref, g_ref, u_ref, a_ref, h_ref, acc_ref):
        xv = x_ref[...]
        _, _, h = _rms_fwd(xv, n_ref[...])
        h = h.astype(CDT)
        h_ref[...] = h
        for ci in range(F // FT):
            sl = slice(ci * FT, (ci + 1) * FT)
            g = _mm_nt(h, wg_ref[sl, :])
            u = _mm_nt(h, wu_ref[sl, :])
            g_ref[:, sl] = g.astype(CDT)
            u_ref[:, sl] = u.astype(CDT)
            a = (g * jax.nn.sigmoid(g) * u).astype(CDT)
            a_ref[:, sl] = a
            o = _mm(a, wd_ref[sl, :])
            if ci == 0:
                acc_ref[...] = o
            else:
                acc_ref[...] += o
        y_ref[...] = xv + 0.5 * acc_ref[...]

    tok = pl.BlockSpec((tm, D), lambda i: (i, 0))
    hid = pl.BlockSpec((tm, F), lambda i: (i, 0))
    hidden = jax.ShapeDtypeStruct((S, F), CDT)
    return _call(
        body, hooks, [x, nrm, wg, wu, wd], name=name, grid=(S // tm,),
        out_shape=[jax.ShapeDtypeStruct((S, D), f32), hidden, hidden, hidden, jax.ShapeDtypeStruct((S, D), CDT)],
        in_specs=[tok, _resident((1, D), lambda i: (0, 0)), _wblock(wg), _wblock(wu), _wblock(wd)],
        out_specs=[tok, hid, hid, hid, tok],
        scratch_shapes=[pltpu.VMEM((tm, D), f32)],
        compiler_params=_params("arbitrary"),
    )


def _proj_fwd(x1, nrm, wfull, b3, cos, sin, hooks=(), tm=512):
    S = x1.shape[0]

    def body(x_ref, n_ref, w_ref, b_ref, cos_ref, sin_ref, p_ref, h_ref):
        _, _, h = _rms_fwd(x_ref[...], n_ref[...])
        h = h.astype(CDT)
        h_ref[...] = h
        for p in range(8):
            seg = SEG_OF_SLOT[p]
            z = _mm(h, w_ref[seg * D:(seg + 1) * D, :]) + b_ref[seg]
            if p in (SLOT_Q, SLOT_K):
                co, si = cos_ref[...], sin_ref[...]
                for hh in range(H):
                    cs = slice(hh * DK, (hh + 1) * DK)
                    zr = _rotate(z[:, cs], co, si)
                    p_ref[p, :, cs] = (zr * K_SCALE if p == SLOT_K else zr).astype(CDT)
            else:
                p_ref[p] = z.astype(CDT)

    tab = pl.BlockSpec((tm, DK // 2), lambda i: (i, 0))
    return _call(
        body, hooks, [x1, nrm, wfull, b3, cos, sin], name="proj_fwd", grid=(S // tm,),
        out_shape=[jax.ShapeDtypeStruct((8, S, D), CDT), jax.ShapeDtypeStruct((S, D), CDT)],
        in_specs=[pl.BlockSpec((tm, D), lambda i: (i, 0)), _resident((1, D), lambda i: (0, 0)),
                  _resident((8 * D, D), lambda i: (0, 0)), _resident((8, 1, D), lambda i: (0, 0, 0)), tab, tab],
        out_specs=[pl.BlockSpec((8, tm, D), lambda i: (0, i, 0)), pl.BlockSpec((tm, D), lambda i: (i, 0))],
        compiler_params=_params("arbitrary"),
    )


def _sgu_norm(va, gn, bn):
    mu = jnp.mean(va, axis=-1, keepdims=True)
    xc = va - mu
    rstd = lax.rsqrt(jnp.mean(xc * xc, axis=-1, keepdims=True) + EPS)
    vhat = xc * rstd
    return rstd, vhat, vhat * gn + bn


def _sgu_fwd(proj, gn, bn, ws, bsc, tm=512):
    S = proj.shape[1]
    GW = D // G

    def body(p_ref, gn_ref, bn_ref, ws_ref, bs_ref, a_ref):
        ua = _gelu(p_ref[0].astype(f32))
        va = _gelu(p_ref[1].astype(f32))
        _, _, vn = _sgu_norm(va, gn_ref[...], bn_ref[...])
        vn = vn.astype(CDT)
        for ch in range(tm // C):
            rs = slice(ch * C, (ch + 1) * C)
            for gi in range(G):
                cs = slice(gi * GW, (gi + 1) * GW)
                s = _mm(ws_ref[gi], vn[rs, cs]) + bs_ref[gi]
                a_ref[rs, cs] = (ua[rs, cs] * s).astype(CDT)

    return pl.pallas_call(
        body, name="sgu_fwd", grid=(S // tm,),
        out_shape=jax.ShapeDtypeStruct((S, D), CDT),
        in_specs=[pl.BlockSpec((2, tm, D), lambda i: (0, i, 0)), _resident((1, D), lambda i: (0, 0)),
                  _resident((1, D), lambda i: (0, 0)), _resident((G, C, C), lambda i: (0, 0, 0)),
                  _resident((G, C, 1), lambda i: (0, 0, 0))],
        out_specs=pl.BlockSpec((tm, D), lambda i: (i, 0)),
        compiler_params=_params("arbitrary"),
    )(proj, gn, bn, ws, bsc)


def _decay_tables(dl_ref):
    lg = jax.nn.log_sigmoid(dl_ref[0:2, :])
    lgf, lgb = lg[0:1, :], lg[1:2, :]
    assert RC <= DK
    ri = lax.broadcasted_iota(jnp.int32, (RC, RC), 0)
    ci = lax.broadcasted_iota(jnp.int32, (RC, RC), 1)
    d = (ri - ci).astype(f32)
    lower = d >= 0
    dmat = jnp.where(lower, jnp.exp(d * lgf[:, :RC]), jnp.exp(-d * lgb[:, :RC]))
    dmat_t = jnp.where(d <= 0, jnp.exp(-d * lgf[:, :RC]), jnp.exp(d * lgb[:, :RC]))
    pos = lax.broadcasted_iota(jnp.int32, (RC, DK), 0).astype(f32)
    t = dict(
        lgf=lgf, lgb=lgb, d=d, lower=lower, dmat=dmat, dmat_t=dmat_t, pos=pos,
        fq=jnp.exp((pos + 1.0) * lgf), fk=jnp.exp((RC - 1.0 - pos) * lgf),
        bq=jnp.exp((RC - pos) * lgb), bk=jnp.exp(pos * lgb),
        lamf=jnp.exp(float(RC) * lgf), lamb=jnp.exp(float(RC) * lgb),
    )
    return t


def _rotate(t, co, si):
    t1, t2 = t[:, :DK // 2], t[:, DK // 2:]
    return jnp.concatenate([t1 * co - t2 * si, t2 * co + t1 * si], axis=-1)


def _unrotate(t, co, si):
    t1, t2 = t[:, :DK // 2], t[:, DK // 2:]
    return jnp.concatenate([t1 * co + t2 * si, t2 * co - t1 * si], axis=-1)


K_SCALE = DK ** -0.5
ROW_TILE = 256


def _ret_fwd(proj, dl, hooks=()):
    S = proj.shape[1]
    NC = S // RC

    def body(q_ref, k_ref, v_ref, g_ref, dl_ref, R_ref, r_ref, sfs_ref, sbs_ref, rb_ref, sf_ref, sb_ref):
        t = _decay_tables(dl_ref)

        def chunk(n):
            rows = pl.ds(pl.multiple_of(n * RC, RC), RC)
            return rows, q_ref[rows, :], k_ref[rows, :], v_ref[rows, :]

        sf_ref[...] = jnp.zeros_like(sf_ref)
        sb_ref[...] = jnp.zeros_like(sb_ref)

        def step(i, carry):
            rows, qn, kn, vn = chunk(i)
            sc = _mm_nt(qn, kn) * t["dmat"]
            out = _mm(sc.astype(CDT), vn)
            sf = sf_ref[...]
            sfb = sf.astype(CDT)
            sfs_ref[i] = sfb
            R_ref[rows, :] = out + _mm((qn.astype(f32) * t["fq"]).astype(CDT), sfb)
            sf_ref[...] = sf * t["lamf"] + _mm_tn((kn.astype(f32) * t["fk"]).astype(CDT), vn)
            m = NC - 1 - i
            rows, qn, kn, vn = chunk(m)
            sb = sb_ref[...]
            sbb = sb.astype(CDT)
            sbs_ref[m] = sbb
            rb_ref[rows, :] = _mm((qn.astype(f32) * t["bq"]).astype(CDT), sbb)
            sb_ref[...] = sb * t["lamb"] + _mm_tn((kn.astype(f32) * t["bk"]).astype(CDT), vn)
            return carry

        lax.fori_loop(0, NC, step, 0)

        def finish(i, carry):
            rs = pl.ds(pl.multiple_of(i * ROW_TILE, ROW_TILE), ROW_TILE)
            R = R_ref[rs, :] + rb_ref[rs, :]
            R_ref[rs, :] = R
            rn = R * lax.rsqrt(jnp.mean(R * R, axis=-1, keepdims=True) + EPS)
            g = g_ref[rs, :].astype(f32)
            r_ref[rs, :] = (rn * g * jax.nn.sigmoid(g)).astype(CDT)
            return carry

        lax.fori_loop(0, S // ROW_TILE, finish, 0)

    def seg(slot):
        return pl.BlockSpec((None, S, DK), lambda h: (slot, 0, h))

    states = jax.ShapeDtypeStruct((H, NC, DK, DK), CDT)
    state_blk = pl.BlockSpec((None, NC, DK, DK), lambda h: (h, 0, 0, 0))
    return _call(
        body, hooks, [proj, proj, proj, proj, dl], name="ret_fwd", grid=(H,),
        out_shape=[jax.ShapeDtypeStruct((S, H * DK), f32), jax.ShapeDtypeStruct((S, H * DK), CDT), states, states],
        in_specs=[seg(SLOT_Q), seg(SLOT_K), seg(SLOT_VR), seg(SLOT_GR), pl.BlockSpec((None, 8, DK), lambda h: (h, 0, 0))],
        out_specs=[pl.BlockSpec((S, DK), lambda h: (0, h)), pl.BlockSpec((S, DK), lambda h: (0, h)), state_blk, state_blk],
        scratch_shapes=[pltpu.VMEM((S, DK), f32), pltpu.VMEM((DK, DK), f32), pltpu.VMEM((DK, DK), f32)],
        compiler_params=_params("arbitrary"),
    )


def _merge_fwd(a, r, proj, x1, wa, wb, wo, hooks=(), tm=512):
    S = x1.shape[0]

    def body(a_ref, r_ref, gt_ref, x_ref, wa_ref, wb_ref, wo_ref, x2_ref, ya_ref, yb_ref):
        ya = _mm(a_ref[...], wa_ref[...])
        yb = _mm(r_ref[...], wb_ref[...])
        ya_ref[...] = ya.astype(CDT)
        yb_ref[...] = yb.astype(CDT)
        mix = jax.nn.sigmoid(gt_ref[0].astype(f32)) * ya + jax.nn.sigmoid(gt_ref[1].astype(f32)) * yb
        x2_ref[...] = x_ref[...] + _mm(mix.astype(CDT), wo_ref[...])

    tok = pl.BlockSpec((tm, D), lambda i: (i, 0))
    return _call(
        body, hooks, [a, r, proj, x1, wa, wb, wo], name="merge_fwd", grid=(S // tm,),
        out_shape=[jax.ShapeDtypeStruct((S, D), f32), jax.ShapeDtypeStruct((S, D), CDT), jax.ShapeDtypeStruct((S, D), CDT)],
        in_specs=[tok, tok, pl.BlockSpec((2, tm, D), lambda i: (SLOT_GA // 2, i, 0)), tok,
                  _wblock(wa), _wblock(wb), _wblock(wo)],
        out_specs=[tok, tok, tok],
        compiler_params=_params("arbitrary"),
    )


def _loss_head(x3, fn, target, tm=512):
    S = x3.shape[0]

    def body(x_ref, n_ref, t_ref, dx_ref, dxh_ref, dn_ref, l_ref):
        n = n_ref[...]
        r, xh, y = _rms_fwd(x_ref[...], n)
        e = y - t_ref[...]
        dy = e * (1.0 / D)
        dx, dn = _rms_bwd(dy, r, xh, n)
        dx_ref[...] = dx
        dxh_ref[...] = (0.5 * dx).astype(CDT)
        part = 0.5 * jnp.sum(jnp.sum(e * e, axis=-1, keepdims=True), axis=0, keepdims=True) * (1.0 / D)

        @pl.when(pl.program_id(0) == 0)
        def _():
            dn_ref[...] = jnp.zeros_like(dn_ref)
            l_ref[...] = jnp.zeros_like(l_ref)

        dn_ref[...] += dn
        l_ref[...] += jnp.broadcast_to(part, l_ref.shape)

    tok = pl.BlockSpec((tm, D), lambda i: (i, 0))
    return pl.pallas_call(
        body, name="loss_head", grid=(S // tm,),
        out_shape=[jax.ShapeDtypeStruct((S, D), f32), jax.ShapeDtypeStruct((S, D), CDT), jax.ShapeDtypeStruct((1, D), f32),
                   jax.ShapeDtypeStruct((8, 128), f32)],
        in_specs=[tok, _resident((1, D), lambda i: (0, 0)), tok],
        out_specs=[tok, tok, pl.BlockSpec((1, D), lambda i: (0, 0)), pl.BlockSpec((8, 128), lambda i: (0, 0))],
        compiler_params=_params("arbitrary"),
    )(x3, fn, target)


def _ffn_bwd_hidden(name, dyh, g, u, wd, hooks=(), tm=512):
    S = dyh.shape[0]

    def body(dyh_ref, g_ref, u_ref, wd_ref, dg_ref, du_ref):
        dyh = dyh_ref[...]
        for ci in range(F // FT):
            sl = slice(ci * FT, (ci + 1) * FT)
            da = _mm_nt(dyh, wd_ref[sl, :])
            gv = g_ref[:, sl].astype(f32)
            uv = u_ref[:, sl].astype(f32)
            s = jax.nn.sigmoid(gv)
            silu = gv * s
            du_ref[:, sl] = (da * silu).astype(CDT)
            dg_ref[:, sl] = (da * uv * (s + silu - silu * s)).astype(CDT)

    hid = pl.BlockSpec((tm, F), lambda i: (i, 0))
    hidden = jax.ShapeDtypeStruct((S, F), CDT)
    return _call(
        body, hooks, [dyh, g, u, wd], name=name, grid=(S // tm,), out_shape=[hidden, hidden],
        in_specs=[pl.BlockSpec((tm, D), lambda i: (i, 0)), hid, hid, _wblock(wd)], out_specs=[hid, hid],
        compiler_params=_params("arbitrary"),
    )


def _ffn_bwd_in(name, dy, x, dg, du, nrm, wg, wu, hooks=(), tm=512):
    S = x.shape[0]

    def body(dy_ref, x_ref, dg_ref, du_ref, n_ref, wg_ref, wu_ref, dx_ref, dn_ref, acc_ref):
        n = n_ref[...]
        r, xh, _ = _rms_fwd(x_ref[...], n)
        for ci in range(F // FT):
            sl = slice(ci * FT, (ci + 1) * FT)
            dh = _mm(dg_ref[:, sl], wg_ref[sl, :]) + _mm(du_ref[:, sl], wu_ref[sl, :])
            if ci == 0:
                acc_ref[...] = dh
            else:
                acc_ref[...] += dh
        dx, dn = _rms_bwd(acc_ref[...], r, xh, n)
        dx_ref[...] = dy_ref[...] + dx

        @pl.when(pl.program_id(0) == 0)
        def _():
            dn_ref[...] = jnp.zeros_like(dn_ref)

        dn_ref[...] += dn

    tok = pl.BlockSpec((tm, D), lambda i: (i, 0))
    hid = pl.BlockSpec((tm, F), lambda i: (i, 0))
    return _call(
        body, hooks, [dy, x, dg, du, nrm, wg, wu], name=name, grid=(S // tm,),
        out_shape=[jax.ShapeDtypeStruct((S, D), f32), jax.ShapeDtypeStruct((1, D), f32)],
        in_specs=[tok, tok, hid, hid, _resident((1, D), lambda i: (0, 0)), _wblock(wg), _wblock(wu)],
        out_specs=[tok, pl.BlockSpec((1, D), lambda i: (0, 0))],
        scratch_shapes=[pltpu.VMEM((tm, D), f32)],
        compiler_params=_params("arbitrary"),
    )


TN_ROWS = 512


def _tn(name, xs, ys, block_of, hooks=()):
    S, M = xs.shape
    B = ys.shape[0]
    tr = TN_ROWS if M % TN_ROWS == 0 else M // 2
    assert M % tr == 0 and tr % 128 == 0
    nt = M // tr

    def body(x_ref, y_ref, o_ref):
        o_ref[...] = _mm_tn(x_ref[...], y_ref[...]).astype(CDT)

    return _call(
        body, hooks, [xs, ys], name=name, grid=(B, nt),
        out_shape=[jax.ShapeDtypeStruct((B * M, D), CDT)],
        in_specs=[pl.BlockSpec((S, tr), lambda b, i: (0, i)), pl.BlockSpec((None, S, D), lambda b, i: (b, 0, 0))],
        out_specs=[pl.BlockSpec((tr, D), lambda b, i: (block_of(b) * nt + i, 0))],
        compiler_params=_params("arbitrary", "arbitrary"),
    )


def _wgrad(name, xs, y, hooks=()):
    return _tn(name, xs, y[None], lambda b: 0, hooks)


def _merge_bwd_act(dx2, ya, yb, proj, wa, wb, wo, hooks=(), tm=512):
    S = dx2.shape[0]

    def body(dx_ref, ya_ref, yb_ref, gt_ref, wa_ref, wb_ref, wo_ref,
             dp_ref, da_ref, dr_ref, mix_ref, dxb_ref, dya_ref, dyb_ref):
        dxb = dx_ref[...].astype(CDT)
        dxb_ref[...] = dxb
        dmix = _mm_nt(dxb, wo_ref[...])
        ya = ya_ref[...].astype(f32)
        yb = yb_ref[...].astype(f32)
        sa = jax.nn.sigmoid(gt_ref[0].astype(f32))
        sb = jax.nn.sigmoid(gt_ref[1].astype(f32))
        mix_ref[...] = (sa * ya + sb * yb).astype(CDT)
        dya = (dmix * sa).astype(CDT)
        dyb = (dmix * sb).astype(CDT)
        dya_ref[...] = dya
        dyb_ref[...] = dyb
        dp_ref[0] = (dmix * ya * sa * (1.0 - sa)).astype(CDT)
        dp_ref[1] = (dmix * yb * sb * (1.0 - sb)).astype(CDT)
        da_ref[...] = _mm_nt(dya, wa_ref[...]).astype(CDT)
        dr_ref[...] = _mm_nt(dyb, wb_ref[...]).astype(CDT)

    tok = pl.BlockSpec((tm, D), lambda i: (i, 0))
    gates = pl.BlockSpec((2, tm, D), lambda i: (SLOT_GA // 2, i, 0))
    act = jax.ShapeDtypeStruct((S, D), CDT)
    return _call(
        body, hooks, [dx2, ya, yb, proj, wa, wb, wo], name="merge_bwd_act", grid=(S // tm,),
        out_shape=[jax.ShapeDtypeStruct((8, S, D), CDT), act, act, act, act, act, act],
        in_specs=[tok, tok, tok, gates, _wblock(wa), _wblock(wb), _wblock(wo)],
        out_specs=[gates, tok, tok, tok, tok, tok, tok],
        compiler_params=_params("arbitrary"),
    )


def _sgu_bwd(da, proj, dproj, gn, bn, ws, wst, bsc, hooks=(), tm=512):
    S = proj.shape[1]
    GW = D // G

    def body(da_ref, p_ref, dpin_ref, gn_ref, bn_ref, ws_ref, wst_ref, bs_ref,
             dp_ref, dws_ref, dbs_ref, dgn_ref, dbn_ref, ds_ref, dvn_ref):
        @pl.when(pl.program_id(0) == 0)
        def _():
            dws_ref[...] = jnp.zeros_like(dws_ref)
            dbs_ref[...] = jnp.zeros_like(dbs_ref)
            dgn_ref[...] = jnp.zeros_like(dgn_ref)
            dbn_ref[...] = jnp.zeros_like(dbn_ref)

        pu = p_ref[0].astype(f32)
        pv = p_ref[1].astype(f32)
        ua = _gelu(pu)
        va = _gelu(pv)
        gn = gn_ref[...]
        rstd, vhat, vn = _sgu_norm(va, gn, bn_ref[...])
        vnb = vn.astype(CDT)
        dav = da_ref[...].astype(f32)
        dsb = (dav * ua).astype(CDT)
        ones = jnp.ones((8, GW), CDT)
        for ch in range(tm // C):
            rs = slice(ch * C, (ch + 1) * C)
            for gi in range(G):
                cs = slice(gi * GW, (gi + 1) * GW)
                s = _mm(ws_ref[gi], vnb[rs, cs]) + bs_ref[gi]
                ds_ref[rs, cs] = s
                dsg = dsb[rs, cs]
                dws_ref[gi] += _mm_nt(dsg, vnb[rs, cs])
                dbs_ref[gi] += _mm_nt(ones, dsg)
                dvn_ref[rs, cs] = _mm(wst_ref[gi], dsg)
        dp_ref[0] = (dav * ds_ref[...] * _gelu_grad(pu)).astype(CDT)
        dvn = dvn_ref[...]
        dgn_ref[...] += jnp.sum(dvn * vhat, axis=0, keepdims=True)
        dbn_ref[...] += jnp.sum(dvn, axis=0, keepdims=True)
        dvh = dvn * gn
        dva = rstd * (dvh - jnp.mean(dvh, axis=-1, keepdims=True) - vhat * jnp.mean(dvh * vhat, axis=-1, keepdims=True))
        dp_ref[1] = (dva * _gelu_grad(pv)).astype(CDT)

    uv = pl.BlockSpec((2, tm, D), lambda i: (0, i, 0))
    row = _resident((1, D), lambda i: (0, 0))
    return _call(
        body, hooks, [da, proj, dproj, gn, bn, ws, wst, bsc], name="sgu_bwd", grid=(S // tm,),
        out_shape=[jax.ShapeDtypeStruct(dproj.shape, CDT), jax.ShapeDtypeStruct((G, C, C), f32),
                   jax.ShapeDtypeStruct((G, 8, C), f32), jax.ShapeDtypeStruct((1, D), f32), jax.ShapeDtypeStruct((1, D), f32)],
        in_specs=[pl.BlockSpec((tm, D), lambda i: (i, 0)), uv, _HBM, row, row,
                  _resident((G, C, C), lambda i: (0, 0, 0)), _resident((G, C, C), lambda i: (0, 0, 0)),
                  _resident((G, C, 1), lambda i: (0, 0, 0))],
        out_specs=[uv, pl.BlockSpec((G, C, C), lambda i: (0, 0, 0)), pl.BlockSpec((G, 8, C), lambda i: (0, 0, 0)),
                   pl.BlockSpec((1, D), lambda i: (0, 0)), pl.BlockSpec((1, D), lambda i: (0, 0))],
        scratch_shapes=[pltpu.VMEM((tm, D), f32), pltpu.VMEM((tm, D), f32)],
        input_output_aliases={2: 0},
        compiler_params=_params("arbitrary"),
    )


def _ret_bwd(dr, R, sfs, sbs, proj, dproj, cos, sin, dl, hooks=()):
    S = proj.shape[1]
    NC = S // RC
    assert NC % 2 == 0

    def body(dr_ref, R_ref, sf_ref, sb_ref, q_ref, k_ref, v_ref, g_ref, dpin_ref, cos_ref, sin_ref, dl_ref,
             dp_ref, dd_ref, dR_ref, gb_ref, gf_ref, acc_ref):
        t = _decay_tables(dl_ref)

        def gate_norm_bwd(i, carry):
            rs = pl.ds(pl.multiple_of(i * ROW_TILE, ROW_TILE), ROW_TILE)
            Rv = R_ref[rs, :]
            rstd = lax.rsqrt(jnp.mean(Rv * Rv, axis=-1, keepdims=True) + EPS)
            rn = Rv * rstd
            gv = g_ref[rs, :].astype(f32)
            s = jax.nn.sigmoid(gv)
            drv = dr_ref[rs, :].astype(f32)
            dp_ref[3, rs, :] = (drv * rn * (s * (1.0 + gv * (1.0 - s)))).astype(CDT)
            drn = drv * gv * s
            dR_ref[rs, :] = (rstd * (drn - rn * jnp.mean(drn * rn, axis=-1, keepdims=True))).astype(CDT)
            return carry

        lax.fori_loop(0, S // ROW_TILE, gate_norm_bwd, 0)

        def chunk(n):
            rows = pl.ds(pl.multiple_of(n * RC, RC), RC)
            return rows, q_ref[rows, :], k_ref[rows, :], v_ref[rows, :], dR_ref[rows, :]

        def emit_kv(rows, dk, dv, final):
            if not final:
                dp_ref[1, rows, :] = dk.astype(CDT)
                dp_ref[2, rows, :] = dv.astype(CDT)
            else:
                co, si = cos_ref[rows, :], sin_ref[rows, :]
                dk = dp_ref[1, rows, :].astype(f32) + dk
                dp_ref[1, rows, :] = (_unrotate(dk, co, si) * K_SCALE).astype(CDT)
                dp_ref[2, rows, :] = (dp_ref[2, rows, :].astype(f32) + dv).astype(CDT)

        gb_ref[...] = jnp.zeros_like(gb_ref)
        gf_ref[...] = jnp.zeros_like(gf_ref)
        acc_ref[...] = jnp.zeros_like(acc_ref)
        span = t["dmat"] * jnp.abs(t["d"])
        span_f, span_b = jnp.where(t["lower"], span, 0.0), jnp.where(t["lower"], 0.0, span)

        def ascend(n, final):
            rows, qn, kn, vn, dRn = chunk(n)
            qf, kf = qn.astype(f32), kn.astype(f32)
            sc = _mm_nt(qn, kn)
            dA = _mm_nt(dRn, vn)
            prod = sc * dA
            lgf_part = jnp.sum(prod * span_f, axis=0, keepdims=True)
            lgb_part = jnp.sum(prod * span_b, axis=0, keepdims=True)
            dsc = (dA * t["dmat"]).astype(CDT)
            dq = _mm(dsc, kn)
            scT = (_mm_nt(kn, qn) * t["dmat_t"]).astype(CDT)
            dscT = (_mm_nt(vn, dRn) * t["dmat_t"]).astype(CDT)
            dk = _mm(dscT, qn)
            dv = _mm(scT, dRn)
            sfb = sf_ref[n]
            sbb = sb_ref[n]
            qdf = qf * t["fq"]
            dqdf = _mm_nt(dRn, sfb)
            dq += dqdf * t["fq"]
            lgf_row = jnp.sum(qdf * dqdf * (t["pos"] + 1.0), axis=0, keepdims=True)
            qdb = qf * t["bq"]
            dqdb = _mm_nt(dRn, sbb)
            dq += dqdb * t["bq"]
            lgb_row = jnp.sum(qdb * dqdb * (RC - t["pos"]), axis=0, keepdims=True)
            gb = gb_ref[...]
            gbb = gb.astype(CDT)
            kdb = kf * t["bk"]
            dkdb = _mm_nt(vn, gbb)
            dk += dkdb * t["bk"]
            dv += _mm(kdb.astype(CDT), gbb)
            lgb_row += jnp.sum(kdb * dkdb * t["pos"], axis=0, keepdims=True)
            lgb_row += float(RC) * t["lamb"] * jnp.sum(gb * sbb.astype(f32), axis=0, keepdims=True)
            co, si = cos_ref[rows, :], sin_ref[rows, :]
            dp_ref[0, rows, :] = _unrotate(dq, co, si).astype(CDT)
            emit_kv(rows, dk, dv, final)
            acc_ref[0:1, :] += lgf_row + lgf_part
            acc_ref[1:2, :] += lgb_row + lgb_part
            gb_ref[...] = gb * t["lamb"] + _mm_tn(qdb.astype(CDT), dRn)

        def descend(n, final):
            rows, qn, kn, vn, dRn = chunk(n)
            gf = gf_ref[...]
            gfb = gf.astype(CDT)
            kdf = kn.astype(f32) * t["fk"]
            dkdf = _mm_nt(vn, gfb)
            lgf_row = jnp.sum(kdf * dkdf * (RC - 1.0 - t["pos"]), axis=0, keepdims=True)
            lgf_row += float(RC) * t["lamf"] * jnp.sum(gf * sf_ref[n].astype(f32), axis=0, keepdims=True)
            acc_ref[0:1, :] += lgf_row
            emit_kv(rows, dkdf * t["fk"], _mm(kdf.astype(CDT), gfb), final)
            gf_ref[...] = gf * t["lamf"] + _mm_tn((qn.astype(f32) * t["fq"]).astype(CDT), dRn)

        def sweep(final):
            def step(i, carry):
                ascend(i, final)
                descend(NC - 1 - i, final)
                return carry
            return step

        lax.fori_loop(0, NC // 2, sweep(False), 0)
        lax.fori_loop(NC // 2, NC, sweep(True), 0)
        dlg = jnp.sum(acc_ref[...], axis=1, keepdims=True)
        dlogit = dlg * jax.nn.sigmoid(-dl_ref[:, 0:1])
        lane = lax.broadcasted_iota(jnp.int32, (8, 128), 1)
        dd_ref[...] = jnp.where(lane == pl.program_id(0), jnp.broadcast_to(dlogit, (8, 128)), 0.0)

    def seg(slot):
        return pl.BlockSpec((None, S, DK), lambda h: (slot, 0, h))

    head = pl.BlockSpec((S, DK), lambda h: (0, h))
    states = pl.BlockSpec((None, NC, DK, DK), lambda h: (h, 0, 0, 0))
    return _call(
        body, hooks, [dr, R, sfs, sbs, proj, proj, proj, proj, dproj, cos, sin, dl], name="ret_bwd", grid=(H,),
        out_shape=[jax.ShapeDtypeStruct(dproj.shape, CDT), jax.ShapeDtypeStruct((H, 8, 128), f32)],
        in_specs=[head, head, states, states, seg(SLOT_Q), seg(SLOT_K), seg(SLOT_VR), seg(SLOT_GR), _HBM,
                  _resident((S, DK // 2), lambda h: (0, 0)), _resident((S, DK // 2), lambda h: (0, 0)),
                  pl.BlockSpec((None, 8, DK), lambda h: (h, 0, 0))],
        out_specs=[pl.BlockSpec((4, S, DK), lambda h: (1, 0, h), pipeline_mode=pl.Buffered(1)),
                   pl.BlockSpec((None, 8, 128), lambda h: (h, 0, 0))],
        scratch_shapes=[pltpu.VMEM((S, DK), CDT),
                        pltpu.VMEM((DK, DK), f32), pltpu.VMEM((DK, DK), f32), pltpu.VMEM((8, DK), f32)],
        input_output_aliases={8: 0},
        compiler_params=_params("arbitrary"),
    )


def _proj_bwd_act(dproj, dx2, x1, nrm, wfull, hooks=(), tm=512):
    S = x1.shape[0]

    def body(dp_ref, dx2_ref, x_ref, n_ref, w_ref, dx_ref, dxh_ref, dn_ref, db_ref, acc_ref):
        @pl.when(pl.program_id(0) == 0)
        def _():
            dn_ref[...] = jnp.zeros_like(dn_ref)
            db_ref[...] = jnp.zeros_like(db_ref)

        for p in range(8):
            seg = SEG_OF_SLOT[p]
            dp = dp_ref[p]
            db_ref[seg] += jnp.sum(dp.astype(f32), axis=0, keepdims=True)
            dh = _mm_nt(dp, w_ref[seg * D:(seg + 1) * D, :])
            if p == 0:
                acc_ref[...] = dh
            else:
                acc_ref[...] += dh
        n = n_ref[...]
        r, xh, _ = _rms_fwd(x_ref[...], n)
        dx, dn = _rms_bwd(acc_ref[...], r, xh, n)
        dx = dx2_ref[...] + dx
        dx_ref[...] = dx
        dxh_ref[...] = (0.5 * dx).astype(CDT)
        dn_ref[...] += dn

    tok = pl.BlockSpec((tm, D), lambda i: (i, 0))
    return _call(
        body, hooks, [dproj, dx2, x1, nrm, wfull], name="proj_bwd_act", grid=(S // tm,),
        out_shape=[jax.ShapeDtypeStruct((S, D), f32), jax.ShapeDtypeStruct((S, D), CDT), jax.ShapeDtypeStruct((1, D), f32),
                   jax.ShapeDtypeStruct((8, 1, D), f32)],
        in_specs=[pl.BlockSpec((8, tm, D), lambda i: (0, i, 0)), tok, tok, _resident((1, D), lambda i: (0, 0)),
                  _resident((8 * D, D), lambda i: (0, 0))],
        out_specs=[tok, tok, pl.BlockSpec((1, D), lambda i: (0, 0)), pl.BlockSpec((8, 1, D), lambda i: (0, 0, 0))],
        scratch_shapes=[pltpu.VMEM((tm, D), f32)],
        compiler_params=_params("arbitrary"),
    )


def _rs_sum(name, gfulls, lands, my_c):
    n = len(gfulls)
    rows = gfulls[0].shape[0] // NDEV
    assert all(g.shape[0] == NDEV * rows for g in gfulls)

    def body(c_ref, *refs):
        for g_ref, l_ref, o_ref in zip(refs[:n], refs[n:2 * n], refs[2 * n:]):
            o_ref[...] = (g_ref[...].astype(f32) + l_ref[...].astype(f32)).astype(CDT)

    slot = pl.BlockSpec((None, rows, D), lambda k, c: (k, 0, 0))
    return pl.pallas_call(
        body, name=name,
        grid_spec=pltpu.PrefetchScalarGridSpec(
            num_scalar_prefetch=1, grid=(NCHIP,),
            in_specs=[pl.BlockSpec((rows, D), lambda k, c: (2 * k + c[0], 0))] * n + [slot] * n,
            out_specs=[slot] * n),
        out_shape=[jax.ShapeDtypeStruct((NCHIP, rows, D), CDT)] * n,
        compiler_params=_params("arbitrary"),
    )(my_c, *gfulls, *lands)


def _adamw_math(g, w, m, v):
    m2 = ADAM_B1 * m + (1.0 - ADAM_B1) * g
    v2 = ADAM_B2 * v + (1.0 - ADAM_B2) * (g * g)
    delta = -ADAM_LR * ((m2 / BC1) / (jnp.sqrt(v2 / BC2) + ADAM_EPS) + ADAM_WD * w)
    return delta, m2, v2


def _adamw_big(name, landed, w, m, v, after):
    n = len(w)
    rows = w[0].shape[0]
    tr = 256 if rows % 256 == 0 else (rows // 2 if rows > 256 else rows)
    nt = rows // tr

    def body(*refs):
        ins, outs = refs[:4 * n], refs[4 * n + 1:]
        for j in range(n):
            @pl.when(pl.program_id(0) == j)
            def _(j=j):
                l_ref, w_ref, m_ref, v_ref = ins[j], ins[n + j], ins[2 * n + j], ins[3 * n + j]
                g = l_ref[0].astype(f32)
                for k in range(1, NCHIP):
                    g = g + l_ref[k].astype(f32)
                outs[4 * j][...] = g
                outs[4 * j + 1][...], outs[4 * j + 2][...], outs[4 * j + 3][...] = _adamw_math(g, w_ref[...], m_ref[...], v_ref[...])

    def tile(j):
        return lambda jj, i: jnp.clip((jj - j) * nt + i, 0, nt - 1)

    blk = [pl.BlockSpec((tr, D), lambda jj, i, t=tile(j): (t(jj, i), 0)) for j in range(n)]
    lnd = [pl.BlockSpec((NCHIP, tr, D), lambda jj, i, t=tile(j): (0, t(jj, i), 0)) for j in range(n)]
    o = jax.ShapeDtypeStruct((rows, D), f32)
    res = pl.pallas_call(
        body, name=name, grid=(n, nt), out_shape=[o] * (4 * n),
        in_specs=lnd + blk + blk + blk + [_HBM],
        out_specs=[blk[j] for j in range(n) for _ in range(4)],
        compiler_params=_params("arbitrary", "arbitrary"),
    )(*landed, *w, *m, *v, after)
    return [res[4 * j:4 * j + 4] for j in range(n)]


ROW_FFN1_NORM, ROW_MIX_NORM, ROW_SGU_G, ROW_SGU_B, ROW_FFN2_NORM, ROW_FINAL_NORM, ROW_B_IN = 0, 1, 2, 3, 4, 5, 8
ROW_WS, ROW_BS, ROW_DECAY = 0, G * C, G * C + G * 8


def _adamw_small(ga, gb, gn1, gl, params):
    def body(ga_ref, gb_ref, gn1_ref, gl_ref, *refs):
        ins, outs = refs[:30], refs[30:]

        def total(ref, r0, n):
            g = ref[0, r0:r0 + n, :]
            for j in range(1, NDEV):
                g = g + ref[j, r0:r0 + n, :]
            return g

        def apply(i, g, rows=slice(None)):
            w, m, v = ins[3 * i][rows, :], ins[3 * i + 1][rows, :], ins[3 * i + 2][rows, :]
            outs[4 * i][rows, :] = g
            outs[4 * i + 1][rows, :], outs[4 * i + 2][rows, :], outs[4 * i + 3][rows, :] = _adamw_math(g, w, m, v)

        outs[40][...] = total(gl_ref, 0, 8)
        apply(0, total(gn1_ref, 0, 1))
        for i, r in enumerate((ROW_FFN1_NORM, ROW_MIX_NORM, ROW_SGU_G, ROW_SGU_B, ROW_FFN2_NORM, ROW_FINAL_NORM)):
            if i:
                apply(i, total(ga_ref, r, 1))
        apply(6, total(ga_ref, ROW_B_IN, 8))
        apply(7, total(gb_ref, ROW_WS, G * C))
        for gi in range(G):
            apply(8, total(gb_ref, ROW_BS + 8 * gi, 1), slice(gi, gi + 1))
        dec = total(gb_ref, ROW_DECAY, 8)
        for hh in range(1, H):
            dec = dec + total(gb_ref, ROW_DECAY + 8 * hh, 8)
        apply(9, dec)

    flat = [a for p in params for a in p]
    out_shape = [jax.ShapeDtypeStruct(p[0].shape, f32) for p in params for _ in range(4)]
    out_shape.append(jax.ShapeDtypeStruct((8, 128), f32))
    vm = pl.BlockSpec(memory_space=pltpu.VMEM)
    return pl.pallas_call(
        body, name="adamw_small", out_shape=out_shape,
        in_specs=[vm] * (4 + len(flat)), out_specs=[vm] * len(out_shape),
        compiler_params=pltpu.CompilerParams(vmem_limit_bytes=VMEM_LIMIT),
    )(ga, gb, gn1, gl, *flat)


def kernel(x, ffn1_norm, ffn1_w_gate, ffn1_w_up, ffn1_w_down, mix_norm, w_in, b_in, sgu_norm_g, sgu_norm_b, sgu_w_s, sgu_b_s, ret_decay_logit, w_branch_a, w_branch_b, w_out, ffn2_norm, ffn2_w_gate, ffn2_w_up, ffn2_w_down, final_norm, loss_target, m_ffn1_norm, m_ffn1_w_gate, m_ffn1_w_up, m_ffn1_w_down, m_mix_norm, m_w_in, m_b_in, m_sgu_norm_g, m_sgu_norm_b, m_sgu_w_s, m_sgu_b_s, m_ret_decay_logit, m_w_branch_a, m_w_branch_b, m_w_out, m_ffn2_norm, m_ffn2_w_gate, m_ffn2_w_up, m_ffn2_w_down, m_final_norm, v_ffn1_norm, v_ffn1_w_gate, v_ffn1_w_up, v_ffn1_w_down, v_mix_norm, v_w_in, v_b_in, v_sgu_norm_g, v_sgu_norm_b, v_sgu_w_s, v_sgu_b_s, v_ret_decay_logit, v_w_branch_a, v_w_branch_b, v_w_out, v_ffn2_norm, v_ffn2_w_gate, v_ffn2_w_up, v_ffn2_w_down, v_final_norm):
    args = dict(locals())
    S = x.shape[1]
    xs = x[0]
    target = loss_target[0]

    def buf_layout(name, a):
        a = a[0]
        return a.T if name in W_TRANSPOSED else a

    sh = {n: buf_layout(n, args[n]).astype(CDT) for n in W_NAMES}
    wf = {}

    b3 = b_in.reshape(8, 1, D)
    ws = sgu_w_s[0].astype(CDT)
    wst = jnp.swapaxes(sgu_w_s[0], 1, 2).astype(CDT)
    bsc = sgu_b_s[0].reshape(G, C, 1)
    dl = jnp.zeros((H, 8, DK), f32).at[:, 0:2, :].set(jnp.broadcast_to(ret_decay_logit[0].T[:, :, None], (H, 2, DK)))
    theta = ROPE_BASE ** (-jnp.arange(0, DK, 2, dtype=f32) / DK)
    ang = jnp.arange(S, dtype=f32)[:, None] * theta[None, :]
    cos, sin = jnp.cos(ang), jnp.sin(ang)
    fnorm = final_norm.reshape(1, D)

    f1 = ("ffn1_w_gate", "ffn1_w_up", "ffn1_w_down")
    f2 = ("ffn2_w_gate", "ffn2_w_up", "ffn2_w_down")
    br = ("w_branch_a", "w_branch_b", "w_out")
    for cid, names in enumerate((f1, ("w_in",), br, f2)):
        wf.update(zip(names, _sequence("ag_" + names[0], 1 + cid, NEIGHBOURS, [_ag_hook(sh[n]) for n in names])))
    x1, g1, u1, a1, hf1 = _ffn_fwd("ffn1_fwd", xs, ffn1_norm, *[wf[n] for n in f1])
    proj, h2 = _proj_fwd(x1, mix_norm, wf["w_in"], b3, cos, sin)
    a = _sgu_fwd(proj, sgu_norm_g, sgu_norm_b, ws, bsc)
    R, r, sfs, sbs = _ret_fwd(proj, dl)
    x2, ya, yb = _merge_fwd(a, r, proj, x1, *[wf[n] for n in br])
    x3, g2, u2, a2, hf2 = _ffn_fwd("ffn2_fwd", x2, ffn2_norm, *[wf[n] for n in f2])
    dx3, dyh2, d_final, loss_part = _loss_head(x3, fnorm, target)

    my_c = lax.axis_index("c").astype(jnp.int32).reshape(1)
    gw, landed, sequenced = {}, {}, []

    def d2d(*names):
        return [_rs_d2d_hook(gw[n]) for n in names]

    def behind(x, token):
        return lax.optimization_barrier((x, token))[0]

    def to_chips(names, sibs, more=()):
        parts = list(_rs_sum("rs_sum_" + names[0], [gw[n] for n in names], list(sibs), my_c))
        token = parts[0]
        if sequenced:
            parts[0] = behind(parts[0], sequenced[-1])
        hooks = [_rs_ici_hook(p) for p in parts] + ([_small_hook(list(more))] if more else [])
        reach = EVERYONE if more else CHIPS
        got = _sequence("rs_chips_" + names[0], REACH_ID[reach], reach, hooks)
        sequenced.append(got[0])
        landed.update(zip(names, got))
        return got[len(names):], token

    def ffn_bwd(tag, names, dy, dyh, x, g, u, a, h, nrm, each_alone, more=()):
        wg, wu, wd = names
        (gw[wd],) = _wgrad(tag + "_wd_grad", a, dyh)
        dg, du, sib_d = _ffn_bwd_hidden(tag + "_bwd_hidden", dyh, g, u, wf[wd], d2d(wd))
        if each_alone:
            dg = behind(dg, to_chips([wd], [sib_d])[1])
        (gw[wg],) = _wgrad(tag + "_wg_grad", dg, h)
        gw[wu], sib_g = _wgrad(tag + "_wu_grad", du, h, d2d(wg))
        if each_alone:
            dy = behind(dy, to_chips([wg], [sib_g])[1])
        dx, dn, sib_u = _ffn_bwd_in(tag + "_bwd_in", dy, x, dg, du, nrm, wf[wg], wf[wu], d2d(wu))
        if each_alone:
            return (dx, dn) + to_chips([wu], [sib_u], more(dn))
        return (dx, dn) + to_chips([wd, wg, wu], [sib_d, sib_g, sib_u])

    dx2, d_ffn2n, _, token = ffn_bwd("ffn2", f2, dx3, dyh2, x2, g2, u2, a2, hf2, ffn2_norm, False)
    dproj, da, dr, mix, dx2b, dya, dyb = _merge_bwd_act(behind(dx2, token), ya, yb, proj, *[wf[n] for n in br])
    dproj, d_ws, d_bs, d_gn, d_bn = _sgu_bwd(da, proj, dproj, sgu_norm_g, sgu_norm_b, ws, wst, bsc)
    dproj, d_dec = _ret_bwd(dr, R, sfs, sbs, proj, dproj, cos, sin, dl)
    (gw["w_in"],) = _tn("win_grad", h2, dproj, _seg_of_slot)
    gw["w_out"], sib_win = _wgrad("wo_grad", mix, dx2b, d2d("w_in"))
    small_sgu = jnp.concatenate([d_ws.reshape(G * C, C), d_bs.reshape(G * 8, C)], axis=0)
    (g_sgu, gl), token = to_chips(["w_in"], [sib_win], [small_sgu, loss_part])
    (gw["w_branch_a"],) = _wgrad("wa_grad", a, behind(dya, token))
    (gw["w_branch_b"],) = _wgrad("wb_grad", r, dyb)
    dx1, dyh1, d_mixn, d_bin, *sib_br = _proj_bwd_act(behind(dproj, token), dx2, x1, mix_norm, wf["w_in"], d2d(*br))
    small_a = jnp.concatenate([jnp.zeros((1, D), f32), d_mixn, d_gn, d_bn, d_ffn2n, d_final, jnp.zeros((2, D), f32),
                               d_bin.reshape(8, D)], axis=0)
    (ga, g_dec), token = to_chips(list(br), sib_br, [small_a, d_dec.reshape(H * 8, 128)])
    gb = jnp.concatenate([g_sgu, g_dec], axis=1)
    dxs, d_ffn1n, (gn1,), token = ffn_bwd("ffn1", f1, dx1, behind(dyh1, token), xs, g1, u1, a1, hf1, ffn1_norm, True,
                                          lambda dn: [dn])

    out = {"grad_x": dxs[None]}

    def native(name, a):
        a = a.T if name in W_TRANSPOSED else a
        return a[None]

    after = token
    for names in (f2, ("w_in",), br, (f1[2], f1[0], f1[1])):
        res = _adamw_big("adamw_" + names[0], [landed[n] for n in names], [buf_layout(n, args[n]) for n in names],
                         [buf_layout(n, args["m_" + n]) for n in names], [buf_layout(n, args["v_" + n]) for n in names], after)
        after = res[-1][0]
        for n, four in zip(names, res):
            for pre, val in zip(("grad_", "delta_", "new_m_", "new_v_"), four):
                out[pre + n] = native(n, val)

    def pad_decay(a):
        return jnp.zeros((8, 128), f32).at[0:2, 0:H].set(a[0])

    small = [
        ("ffn1_norm", lambda a: a, lambda a: a), ("mix_norm", lambda a: a, lambda a: a),
        ("sgu_norm_g", lambda a: a, lambda a: a), ("sgu_norm_b", lambda a: a, lambda a: a),
        ("ffn2_norm", lambda a: a, lambda a: a),
        ("final_norm", lambda a: a.reshape(1, D), lambda a: a.reshape(D)),
        ("b_in", lambda a: a.reshape(8, D), lambda a: a.reshape(1, 8 * D)),
        ("sgu_w_s", lambda a: a.reshape(G * C, C), lambda a: a.reshape(1, G, C, C)),
        ("sgu_b_s", lambda a: a[0], lambda a: a[None]),
        ("ret_decay_logit", pad_decay, lambda a: a[None, 0:2, 0:H]),
    ]
    res = _adamw_small(ga, gb, gn1, gl, [(to(args[n]), to(args["m_" + n]), to(args["v_" + n])) for n, to, _ in small])
    out["loss"] = res[40][0, 0]
    for i, (n, _, back) in enumerate(small):
        for j, pre in enumerate(("grad_", "delta_", "new_m_", "new_v_")):
            out[pre + n] = back(res[4 * i + j])

    weights = ("ffn1_norm", "ffn1_w_gate", "ffn1_w_up", "ffn1_w_down", "mix_norm", "w_in", "b_in", "sgu_norm_g",
               "sgu_norm_b", "sgu_w_s", "sgu_b_s", "ret_decay_logit", "w_branch_a", "w_branch_b", "w_out", "ffn2_norm",
               "ffn2_w_gate", "ffn2_w_up", "ffn2_w_down", "final_norm")
    return (out["loss"], out["grad_x"], *[out["grad_" + n] for n in weights], *[out["delta_" + n] for n in weights],
            *[out["new_m_" + n] for n in weights], *[out["new_v_" + n] for n in weights])
```

```python
import math

import jax
import jax.numpy as jnp
from jax import lax
from jax.experimental import pallas as pl
from jax.experimental.pallas import tpu as pltpu
from jax.experimental.pallas import tpu_sc as plsc

f32 = jnp.float32
CDT = jnp.bfloat16

D = 1024
F = 2816
C = 128
RC = 256
H = 4
DK = 256
G = 4
NDEV = 8
NCHIP = 4
EPS = 1e-6
ROPE_BASE = 10000.0
FT = 256
V7X_VMEM_BYTES = 64 * 1024 * 1024
VMEM_LIMIT = V7X_VMEM_BYTES - 8 * 1024 * 1024

ADAM_LR, ADAM_B1, ADAM_B2, ADAM_EPS, ADAM_WD, ADAM_STEP = 0.001, 0.9, 0.999, 1e-08, 0.01, 10
BC1 = 1.0 - ADAM_B1 ** ADAM_STEP
BC2 = 1.0 - ADAM_B2 ** ADAM_STEP

W_ROWS = dict(ffn1_w_gate=352, ffn1_w_up=352, ffn1_w_down=352, w_in=1024, w_branch_a=128, w_branch_b=128, w_out=128,
              ffn2_w_gate=352, ffn2_w_up=352, ffn2_w_down=352)
W_NAMES = tuple(W_ROWS)
W_TRANSPOSED = ("ffn1_w_gate", "ffn1_w_up", "ffn2_w_gate", "ffn2_w_up")

SLOT_U, SLOT_V, SLOT_GA, SLOT_GB, SLOT_Q, SLOT_K, SLOT_VR, SLOT_GR = range(8)


SEG_OF_SLOT = (0, 1, 6, 7, 2, 3, 4, 5)


def _seg_of_slot(p):
    return jnp.where(p < 2, p, jnp.where(p < 4, p + 4, p - 2))


def _mm(a, b):
    return jnp.dot(a, b, preferred_element_type=f32)


def _mm_nt(a, b):
    return lax.dot_general(a, b, (((1,), (1,)), ((), ())), preferred_element_type=f32)


def _mm_tn(a, b):
    return lax.dot_general(a, b, (((0,), (0,)), ((), ())), preferred_element_type=f32)


def _params(*sem):
    return pltpu.CompilerParams(dimension_semantics=sem, vmem_limit_bytes=VMEM_LIMIT)


def _resident(shape, index_map):
    return pl.BlockSpec(shape, index_map, pipeline_mode=pl.Buffered(1))


def _gelu(x):
    return 0.5 * x * (1.0 + lax.erf(x * (1.0 / math.sqrt(2.0))))


def _gelu_grad(x):
    return 0.5 * (1.0 + lax.erf(x * (1.0 / math.sqrt(2.0)))) + x * jnp.exp(-0.5 * x * x) * (1.0 / math.sqrt(2.0 * math.pi))


def _rms_fwd(x, n):
    r = lax.rsqrt(jnp.mean(x * x, axis=-1, keepdims=True) + EPS)
    xh = x * r
    return r, xh, xh * n


def _rms_bwd(dh, r, xh, n):
    dxh = dh * n
    dx = r * (dxh - xh * jnp.mean(dxh * xh, axis=-1, keepdims=True))
    return dx, jnp.sum(dh * xh, axis=0, keepdims=True)


MESH_ID = pl.DeviceIdType.MESH
_HBM = pl.BlockSpec(memory_space=pltpu.HBM)


def _my_place():
    return lax.axis_index("x"), lax.axis_index("y"), lax.axis_index("c")


def _ici_peers(x, y, c):
    return [((1 - x, y, c), 2 * (1 - x) + y), ((x, 1 - y, c), 2 * x + 1 - y), ((1 - x, 1 - y, c), 2 * (1 - x) + 1 - y)]


class _Hook:
    def __init__(self, operands, out_shapes, n_remote, n_local, start, finish, relay=None):
        self.operands, self.out_shapes = list(operands), list(out_shapes)
        self.n_remote, self.n_local, self.start, self.finish = n_remote, n_local, start, finish
        self.relay = relay or (lambda *a: None)


def _call(body, hooks, operands, *, in_specs, out_specs, out_shape, grid=None, scratch_shapes=(), side=None, **kw):
    hooks = tuple(hooks)
    side_fn, s_ops, s_in, s_out, s_shape = side(grid) if side else (None, [], [], [], [])
    operands, in_specs = list(operands) + list(s_ops), list(in_specs) + list(s_in)
    n_main_out = len(out_shape)
    out_specs, out_shape = list(out_specs) + list(s_out), list(out_shape) + list(s_shape)
    n_in, n_out, n_scr = len(in_specs), len(out_shape), len(scratch_shapes)
    h_ops = [a for h in hooks for a in h.operands]
    h_outs = [s for h in hooks for s in h.out_shapes]
    h_sems = [pltpu.SemaphoreType.DMA((n,)) for h in hooks for n in (h.n_remote, h.n_remote, max(h.n_local, 1))]

    def wrapped(*refs):
        ins, hin = refs[:n_in], refs[n_in:n_in + len(h_ops)]
        o0 = n_in + len(h_ops)
        outs, hout = refs[o0:o0 + n_out], refs[o0 + n_out:o0 + n_out + len(h_outs)]
        s0 = o0 + n_out + len(h_outs)
        scr, hsem = refs[s0:s0 + n_scr], refs[s0 + n_scr:]
        ins, side_ins = ins[:n_in - len(s_ops)], ins[n_in - len(s_ops):]
        outs, side_outs = outs[:n_main_out], outs[n_main_out:]

        def run(phase):
            ip = op = 0
            for i, h in enumerate(hooks):
                ssem, rsem, lsem = hsem[3 * i:3 * i + 3]

                def rcopy(k, src, dst, dev, ssem=ssem, rsem=rsem):
                    return pltpu.make_async_remote_copy(src_ref=src, dst_ref=dst, send_sem=ssem.at[k], recv_sem=rsem.at[k],
                                                        device_id=dev, device_id_type=MESH_ID)

                def lcopy(k, src, dst, lsem=lsem):
                    return pltpu.make_async_copy(src, dst, lsem.at[k])

                getattr(h, phase)(hin[ip:ip + len(h.operands)], hout[op:op + len(h.out_shapes)], rcopy, lcopy)
                ip += len(h.operands)
                op += len(h.out_shapes)

        def at_edge(phase, last):
            if not hooks:
                return
            if grid is None:
                run(phase)
                return
            cond = None
            for ax, n in enumerate(grid):
                here = pl.program_id(ax) == (n - 1 if last else 0)
                cond = here if cond is None else cond & here
            pl.when(cond)(lambda: run(phase))

        at_edge("start", False)
        at_edge("relay", True)
        body(*ins, *outs, *scr)
        if side_fn is not None:
            side_fn(tuple(pl.program_id(ax) for ax in range(len(grid))), side_ins, side_outs)
        at_edge("finish", True)

    if grid is not None:
        kw["grid"] = grid
    return list(pl.pallas_call(
        wrapped, out_shape=out_shape + h_outs, in_specs=in_specs + [_HBM] * len(h_ops),
        out_specs=out_specs + [_HBM] * len(h_outs), scratch_shapes=list(scratch_shapes) + h_sems, **kw,
    )(*operands, *h_ops))


def _rows(ref, start, n):
    return ref.at[pl.ds(start, n), :]


def _ag_hook(shard):
    rows = shard.shape[0]
    half = rows // 2
    assert half % 16 == 0

    def place():
        x, y, c = _my_place()
        devs = dict(sib=(x, y, 1 - c), xn=(1 - x, y, c), yn=(x, 1 - y, c))
        chips = dict(me=2 * x + y, xn=2 * (1 - x) + y, yn=2 * x + 1 - y, dg=2 * (1 - x) + 1 - y)
        return c, devs, chips

    def block(full, chip, c):
        return _rows(full, (2 * chip + c) * rows, rows)

    def halfblock(full, chip, c, upper):
        return _rows(full, (2 * chip + c) * rows + upper * half, half)

    def start(ins, outs, rcopy, lcopy):
        c, devs, chips = place()
        src, dst = ins[0], block(outs[0], chips["me"], c)
        lcopy(0, src, dst).start()
        for k, to in enumerate(("sib", "xn", "yn")):
            rcopy(k, src, dst, devs[to]).start()

    def relay(ins, outs, rcopy, lcopy):
        c, devs, chips = place()
        full = outs[0]
        blk = block(full, chips["xn"], c)
        rcopy(1, blk, blk, devs["xn"]).wait_recv()
        low = halfblock(full, chips["xn"], c, 0)
        rcopy(3, low, low, devs["yn"]).start()
        rcopy(5, blk, blk, devs["sib"]).start()
        blk = block(full, chips["yn"], c)
        rcopy(2, blk, blk, devs["yn"]).wait_recv()
        up = halfblock(full, chips["yn"], c, 1)
        rcopy(4, up, up, devs["xn"]).start()
        rcopy(6, blk, blk, devs["sib"]).start()
        low, up = halfblock(full, chips["dg"], c, 0), halfblock(full, chips["dg"], c, 1)
        rcopy(3, low, low, devs["yn"]).wait_recv()
        rcopy(4, up, up, devs["xn"]).wait_recv()
        blk = block(full, chips["dg"], c)
        rcopy(7, blk, blk, devs["sib"]).start()

    def finish(ins, outs, rcopy, lcopy):
        c, devs, chips = place()
        full, sib = outs[0], devs["sib"]
        for k, chip in ((0, "me"), (5, "xn"), (6, "yn"), (7, "dg")):
            theirs = block(full, chips[chip], 1 - c)
            rcopy(k, theirs, theirs, sib).wait_recv()
            mine = block(full, chips[chip], c)
            if k:
                rcopy(k, mine, mine, sib).wait_send()
        src, dst = ins[0], block(full, chips["me"], c)
        lcopy(0, src, dst).wait()
        for k, to in enumerate(("sib", "xn", "yn")):
            rcopy(k, src, dst, devs[to]).wait_send()
        low, up = halfblock(full, chips["xn"], c, 0), halfblock(full, chips["yn"], c, 1)
        rcopy(3, low, low, devs["yn"]).wait_send()
        rcopy(4, up, up, devs["xn"]).wait_send()

    return _Hook([shard], [jax.ShapeDtypeStruct((NDEV * rows, D), shard.dtype)], 8, 1, start, finish, relay)


SIBLING, CHIPS, NEIGHBOURS, EVERYONE = "sibling", "chips", "sibling and the two neighbour chips", "everyone"
REACH_ID = {CHIPS: 6, EVERYONE: 7}


def _sequence(name, collective_id, reach, hooks):
    ins = [[jax.new_ref(a, memory_space=pltpu.MemorySpace.HBM) for a in h.operands] for h in hooks]
    outs = [[jax.empty_ref(s, memory_space=pltpu.MemorySpace.HBM) for s in h.out_shapes] for h in hooks]
    sems = tuple(pltpu.SemaphoreType.DMA((n,)) for h in hooks for n in (h.n_remote, h.n_remote, max(h.n_local, 1)))

    @pl.kernel(mesh=plsc.ScalarSubcoreMesh(axis_name="sequencer", num_cores=1), name=name, scratch_types=sems,
               compiler_params=pltpu.CompilerParams(collective_id=collective_id))
    def launch(*sem_refs):
        x, y, c = _my_place()
        chips = [dev for dev, _ in _ici_peers(x, y, c)]
        others = [(1 - x if dx else x, 1 - y if dy else y, 1 - c if dc else c)
                  for dx in range(2) for dy in range(2) for dc in range(2) if dx + dy + dc]
        devs = {SIBLING: [(x, y, 1 - c)], CHIPS: chips, NEIGHBOURS: [(x, y, 1 - c), (1 - x, y, c), (x, 1 - y, c)],
                EVERYONE: others}[reach]
        barrier = pltpu.get_barrier_semaphore()
        for dev in devs:
            pl.semaphore_signal(barrier, inc=1, device_id=dev, device_id_type=MESH_ID)
        pl.semaphore_wait(barrier, len(devs))
        for phase in ("start", "relay", "finish"):
            for i, h in enumerate(hooks):
                ssem, rsem, lsem = sem_refs[3 * i:3 * i + 3]

                def rcopy(k, src, dst, dev, ssem=ssem, rsem=rsem):
                    return pltpu.make_async_remote_copy(src_ref=src, dst_ref=dst, send_sem=ssem.at[k], recv_sem=rsem.at[k],
                                                        device_id=dev, device_id_type=MESH_ID)

                def lcopy(k, src, dst, lsem=lsem):
                    return pltpu.make_async_copy(src, dst, lsem.at[k])

                getattr(h, phase)(ins[i], outs[i], rcopy, lcopy)

    launch()
    return [o[...] for os in outs for o in os]


def _rs_d2d_hook(gfull):
    rows = gfull.shape[0] // NDEV

    def pairs(g, land, mine):
        x, y, c = _my_place()
        return ((x, y, 1 - c), [(k, _rows(g, (2 * k + 1 - c) * rows, rows), land.at[k]) for k in range(NCHIP)],
                [(k, _rows(g, (2 * k + c) * rows, rows), mine.at[k]) for k in range(NCHIP)])

    def start(ins, outs, rcopy, lcopy):
        sib, cps, own = pairs(ins[0], *outs)
        for i, src, dst in cps:
            rcopy(i, src, dst, sib).start()
        for i, src, dst in own:
            lcopy(i, src, dst).start()

    def finish(ins, outs, rcopy, lcopy):
        sib, cps, own = pairs(ins[0], *outs)
        for i, src, dst in cps:
            rcopy(i, dst, dst, sib).wait_recv()
        for i, src, dst in cps:
            rcopy(i, src, dst, sib).wait_send()
        for i, src, dst in own:
            lcopy(i, src, dst).wait()

    slots = jax.ShapeDtypeStruct((NCHIP, rows, D), gfull.dtype)
    return _Hook([gfull], [slots, slots], NCHIP, NCHIP, start, finish)


def _rs_ici_hook(part):
    def start(ins, outs, rcopy, lcopy):
        x, y, c = _my_place()
        mychip = 2 * x + y
        lcopy(0, ins[0].at[mychip], outs[0].at[mychip]).start()
        for j, (dev, chip) in enumerate(_ici_peers(x, y, c)):
            rcopy(j, ins[0].at[chip], outs[0].at[mychip], dev).start()

    def finish(ins, outs, rcopy, lcopy):
        x, y, c = _my_place()
        mychip = 2 * x + y
        peers = _ici_peers(x, y, c)
        for j, (dev, chip) in enumerate(peers):
            rcopy(j, outs[0].at[chip], outs[0].at[chip], dev).wait_recv()
        for j, (dev, chip) in enumerate(peers):
            rcopy(j, ins[0].at[chip], outs[0].at[mychip], dev).wait_send()
        lcopy(0, ins[0].at[mychip], outs[0].at[mychip]).wait()

    return _Hook([part], [jax.ShapeDtypeStruct(part.shape, part.dtype)], 3, 1, start, finish)


def _small_hook(arrays):
    n = len(arrays)

    def peers():
        x, y, c = _my_place()
        out = []
        for dx in range(2):
            for dy in range(2):
                for dc in range(2):
                    if dx + dy + dc:
                        px, py, pc = (1 - x if dx else x), (1 - y if dy else y), (1 - c if dc else c)
                        out.append(((px, py, pc), 4 * px + 2 * py + pc))
        return 4 * x + 2 * y + c, out

    def start(ins, outs, rcopy, lcopy):
        me, ps = peers()
        for t in range(n):
            lcopy(t, ins[t], outs[t].at[me]).start()
            for i, (dev, _) in enumerate(ps):
                rcopy(n * i + t, ins[t], outs[t].at[me], dev).start()

    def finish(ins, outs, rcopy, lcopy):
        me, ps = peers()
        for t in range(n):
            for i, (dev, peer) in enumerate(ps):
                rcopy(n * i + t, outs[t].at[peer], outs[t].at[peer], dev).wait_recv()
            for i, (dev, _) in enumerate(ps):
                rcopy(n * i + t, ins[t], outs[t].at[me], dev).wait_send()
            lcopy(t, ins[t], outs[t].at[me]).wait()

    return _Hook(arrays, [jax.ShapeDtypeStruct((NDEV,) + a.shape, a.dtype) for a in arrays], 7 * n, n, start, finish)


def _wblock(w):
    return _resident(w.shape, lambda *_: (0, 0))


def _ffn_fwd(name, x, nrm, wg, wu, wd, hooks=(), tm=512):
    S = x.shape[0]

    def body(x_ref, n_ref, wg_ref, wu_ref, wd_ref, y_ref, g_ref, u_ref, a_ref, h_ref, acc_ref):
        xv = x_ref[...]
        _, _, h = _rms_fwd(xv, n_ref[...])
        h = h.astype(CDT)
        h_ref[...] = h
        for ci in range(F // FT):
            sl = slice(ci * FT, (ci + 1) * FT)
            g = _mm_nt(h, wg_ref[sl, :])
            u = _mm_nt(h, wu_ref[sl, :])
            g_ref[:, sl] = g.astype(CDT)
            u_ref[:, sl] = u.astype(CDT)
            a = (g * jax.nn.sigmoid(g) * u).astype(CDT)
            a_ref[:, sl] = a
            o = _mm(a, wd_ref[sl, :])
            if ci == 0:
                acc_ref[...] = o
            else:
                acc_ref[...] += o
        y_ref[...] = xv + 0.5 * acc_ref[...]

    tok = pl.BlockSpec((tm, D), lambda i: (i, 0))
    hid = pl.BlockSpec((tm, F), lambda i: (i, 0))
    hidden = jax.ShapeDtypeStruct((S, F), CDT)
    return _call(
        body, hooks, [x, nrm, wg, wu, wd], name=name, grid=(S // tm,),
        out_shape=[jax.ShapeDtypeStruct((S, D), f32), hidden, hidden, hidden, jax.ShapeDtypeStruct((S, D), CDT)],
        in_specs=[tok, _resident((1, D), lambda i: (0, 0)), _wblock(wg), _wblock(wu), _wblock(wd)],
        out_specs=[tok, hid, hid, hid, tok],
        scratch_shapes=[pltpu.VMEM((tm, D), f32)],
        compiler_params=_params("arbitrary"),
    )


def _proj_fwd(x1, nrm, wfull, b3, cos, sin, hooks=(), tm=512):
    S = x1.shape[0]

    def body(x_ref, n_ref, w_ref, b_ref, cos_ref, sin_ref, p_ref, h_ref):
        _, _, h = _rms_fwd(x_ref[...], n_ref[...])
        h = h.astype(CDT)
        h_ref[...] = h
        for p in range(8):
            seg = SEG_OF_SLOT[p]
            z = _mm(h, w_ref[seg * D:(seg + 1) * D, :]) + b_ref[seg]
            if p in (SLOT_Q, SLOT_K):
                co, si = cos_ref[...], sin_ref[...]
                for hh in range(H):
                    cs = slice(hh * DK, (hh + 1) * DK)
                    zr = _rotate(z[:, cs], co, si)
                    p_ref[p, :, cs] = (zr * K_SCALE if p == SLOT_K else zr).astype(CDT)
            else:
                p_ref[p] = z.astype(CDT)

    tab = pl.BlockSpec((tm, DK // 2), lambda i: (i, 0))
    return _call(
        body, hooks, [x1, nrm, wfull, b3, cos, sin], name="proj_fwd", grid=(S // tm,),
        out_shape=[jax.ShapeDtypeStruct((8, S, D), CDT), jax.ShapeDtypeStruct((S, D), CDT)],
        in_specs=[pl.BlockSpec((tm, D), lambda i: (i, 0)), _resident((1, D), lambda i: (0, 0)),
                  _resident((8 * D, D), lambda i: (0, 0)), _resident((8, 1, D), lambda i: (0, 0, 0)), tab, tab],
        out_specs=[pl.BlockSpec((8, tm, D), lambda i: (0, i, 0)), pl.BlockSpec((tm, D), lambda i: (i, 0))],
        compiler_params=_params("arbitrary"),
    )


def _sgu_norm(va, gn, bn):
    mu = jnp.mean(va, axis=-1, keepdims=True)
    xc = va - mu
    rstd = lax.rsqrt(jnp.mean(xc * xc, axis=-1, keepdims=True) + EPS)
    vhat = xc * rstd
    return rstd, vhat, vhat * gn + bn


def _sgu_fwd(proj, gn, bn, ws, bsc, tm=512):
    S = proj.shape[1]
    GW = D // G

    def body(p_ref, gn_ref, bn_ref, ws_ref, bs_ref, a_ref):
        ua = _gelu(p_ref[0].astype(f32))
        va = _gelu(p_ref[1].astype(f32))
        _, _, vn = _sgu_norm(va, gn_ref[...], bn_ref[...])
        vn = vn.astype(CDT)
        for ch in range(tm // C):
            rs = slice(ch * C, (ch + 1) * C)
            for gi in range(G):
                cs = slice(gi * GW, (gi + 1) * GW)
                s = _mm(ws_ref[gi], vn[rs, cs]) + bs_ref[gi]
                a_ref[rs, cs] = (ua[rs, cs] * s).astype(CDT)

    return pl.pallas_call(
        body, name="sgu_fwd", grid=(S // tm,),
        out_shape=jax.ShapeDtypeStruct((S, D), CDT),
        in_specs=[pl.BlockSpec((2, tm, D), lambda i: (0, i, 0)), _resident((1, D), lambda i: (0, 0)),
                  _resident((1, D), lambda i: (0, 0)), _resident((G, C, C), lambda i: (0, 0, 0)),
                  _resident((G, C, 1), lambda i: (0, 0, 0))],
        out_specs=pl.BlockSpec((tm, D), lambda i: (i, 0)),
        compiler_params=_params("arbitrary"),
    )(proj, gn, bn, ws, bsc)


def _decay_tables(dl_ref):
    lg = jax.nn.log_sigmoid(dl_ref[0:2, :])
    lgf, lgb = lg[0:1, :], lg[1:2, :]
    assert RC <= DK
    ri = lax.broadcasted_iota(jnp.int32, (RC, RC), 0)
    ci = lax.broadcasted_iota(jnp.int32, (RC, RC), 1)
    d = (ri - ci).astype(f32)
    lower = d >= 0
    dmat = jnp.where(lower, jnp.exp(d * lgf[:, :RC]), jnp.exp(-d * lgb[:, :RC]))
    dmat_t = jnp.where(d <= 0, jnp.exp(-d * lgf[:, :RC]), jnp.exp(d * lgb[:, :RC]))
    pos = lax.broadcasted_iota(jnp.int32, (RC, DK), 0).astype(f32)
    t = dict(
        lgf=lgf, lgb=lgb, d=d, lower=lower, dmat=dmat, dmat_t=dmat_t, pos=pos,
        fq=jnp.exp((pos + 1.0) * lgf), fk=jnp.exp((RC - 1.0 - pos) * lgf),
        bq=jnp.exp((RC - pos) * lgb), bk=jnp.exp(pos * lgb),
        lamf=jnp.exp(float(RC) * lgf), lamb=jnp.exp(float(RC) * lgb),
    )
    return t


def _rotate(t, co, si):
    t1, t2 = t[:, :DK // 2], t[:, DK // 2:]
    return jnp.concatenate([t1 * co - t2 * si, t2 * co + t1 * si], axis=-1)


def _unrotate(t, co, si):
    t1, t2 = t[:, :DK // 2], t[:, DK // 2:]
    return jnp.concatenate([t1 * co + t2 * si, t2 * co - t1 * si], axis=-1)


K_SCALE = DK ** -0.5
ROW_TILE = 256


def _ret_fwd(proj, dl, hooks=()):
    S = proj.shape[1]
    NC = S // RC

    def body(q_ref, k_ref, v_ref, g_ref, dl_ref, R_ref, r_ref, sfs_ref, sbs_ref, rb_ref, sf_ref, sb_ref):
        t = _decay_tables(dl_ref)

        def chunk(n):
            rows = pl.ds(pl.multiple_of(n * RC, RC), RC)
            return rows, q_ref[rows, :], k_ref[rows, :], v_ref[rows, :]

        sf_ref[...] = jnp.zeros_like(sf_ref)
        sb_ref[...] = jnp.zeros_like(sb_ref)

        def step(i, carry):
            rows, qn, kn, vn = chunk(i)
            sc = _mm_nt(qn, kn) * t["dmat"]
            out = _mm(sc.astype(CDT), vn)
            sf = sf_ref[...]
            sfb = sf.astype(CDT)
            sfs_ref[i] = sfb
            R_ref[rows, :] = out + _mm((qn.astype(f32) * t["fq"]).astype(CDT), sfb)
            sf_ref[...] = sf * t["lamf"] + _mm_tn((kn.astype(f32) * t["fk"]).astype(CDT), vn)
            m = NC - 1 - i
            rows, qn, kn, vn = chunk(m)
            sb = sb_ref[...]
            sbb = sb.astype(CDT)
            sbs_ref[m] = sbb
            rb_ref[rows, :] = _mm((qn.astype(f32) * t["bq"]).astype(CDT), sbb)
            sb_ref[...] = sb * t["lamb"] + _mm_tn((kn.astype(f32) * t["bk"]).astype(CDT), vn)
            return carry

        lax.fori_loop(0, NC, step, 0)

        def finish(i, carry):
            rs = pl.ds(pl.multiple_of(i * ROW_TILE, ROW_TILE), ROW_TILE)
            R = R_ref[rs, :] + rb_ref[rs, :]
            R_ref[rs, :] = R
            rn = R * lax.rsqrt(jnp.mean(R * R, axis=-1, keepdims=True) + EPS)
            g = g_ref[rs, :].astype(f32)
            r_ref[rs, :] = (rn * g * jax.nn.sigmoid(g)).astype(CDT)
            return carry

        lax.fori_loop(0, S // ROW_TILE, finish, 0)

    def seg(slot):
        return pl.BlockSpec((None, S, DK), lambda h: (slot, 0, h))

    states = jax.ShapeDtypeStruct((H, NC, DK, DK), CDT)
    state_blk = pl.BlockSpec((None, NC, DK, DK), lambda h: (h, 0, 0, 0))
    return _call(
        body, hooks, [proj, proj, proj, proj, dl], name="ret_fwd", grid=(H,),
        out_shape=[jax.ShapeDtypeStruct((S, H * DK), f32), jax.ShapeDtypeStruct((S, H * DK), CDT), states, states],
        in_specs=[seg(SLOT_Q), seg(SLOT_K), seg(SLOT_VR), seg(SLOT_GR), pl.BlockSpec((None, 8, DK), lambda h: (h, 0, 0))],
        out_specs=[pl.BlockSpec((S, DK), lambda h: (0, h)), pl.BlockSpec((S, DK), lambda h: (0, h)), state_blk, state_blk],
        scratch_shapes=[pltpu.VMEM((S, DK), f32), pltpu.VMEM((DK, DK), f32), pltpu.VMEM((DK, DK), f32)],
        compiler_params=_params("arbitrary"),
    )


def _merge_fwd(a, r, proj, x1, wa, wb, wo, hooks=(), tm=512):
    S = x1.shape[0]

    def body(a_ref, r_ref, gt_ref, x_ref, wa_ref, wb_ref, wo_ref, x2_ref, ya_ref, yb_ref):
        ya = _mm(a_ref[...], wa_ref[...])
        yb = _mm(r_ref[...], wb_ref[...])
        ya_ref[...] = ya.astype(CDT)
        yb_ref[...] = yb.astype(CDT)
        mix = jax.nn.sigmoid(gt_ref[0].astype(f32)) * ya + jax.nn.sigmoid(gt_ref[1].astype(f32)) * yb
        x2_ref[...] = x_ref[...] + _mm(mix.astype(CDT), wo_ref[...])

    tok = pl.BlockSpec((tm, D), lambda i: (i, 0))
    return _call(
        body, hooks, [a, r, proj, x1, wa, wb, wo], name="merge_fwd", grid=(S // tm,),
        out_shape=[jax.ShapeDtypeStruct((S, D), f32), jax.ShapeDtypeStruct((S, D), CDT), jax.ShapeDtypeStruct((S, D), CDT)],
        in_specs=[tok, tok, pl.BlockSpec((2, tm, D), lambda i: (SLOT_GA // 2, i, 0)), tok,
                  _wblock(wa), _wblock(wb), _wblock(wo)],
        out_specs=[tok, tok, tok],
        compiler_params=_params("arbitrary"),
    )


def _loss_head(x3, fn, target, tm=512):
    S = x3.shape[0]

    def body(x_ref, n_ref, t_ref, dx_ref, dxh_ref, dn_ref, l_ref):
        n = n_ref[...]
        r, xh, y = _rms_fwd(x_ref[...], n)
        e = y - t_ref[...]
        dy = e * (1.0 / D)
        dx, dn = _rms_bwd(dy, r, xh, n)
        dx_ref[...] = dx
        dxh_ref[...] = (0.5 * dx).astype(CDT)
        part = 0.5 * jnp.sum(jnp.sum(e * e, axis=-1, keepdims=True), axis=0, keepdims=True) * (1.0 / D)

        @pl.when(pl.program_id(0) == 0)
        def _():
            dn_ref[...] = jnp.zeros_like(dn_ref)
            l_ref[...] = jnp.zeros_like(l_ref)

        dn_ref[...] += dn
        l_ref[...] += jnp.broadcast_to(part, l_ref.shape)

    tok = pl.BlockSpec((tm, D), lambda i: (i, 0))
    return pl.pallas_call(
        body, name="loss_head", grid=(S // tm,),
        out_shape=[jax.ShapeDtypeStruct((S, D), f32), jax.ShapeDtypeStruct((S, D), CDT), jax.ShapeDtypeStruct((1, D), f32),
                   jax.ShapeDtypeStruct((8, 128), f32)],
        in_specs=[tok, _resident((1, D), lambda i: (0, 0)), tok],
        out_specs=[tok, tok, pl.BlockSpec((1, D), lambda i: (0, 0)), pl.BlockSpec((8, 128), lambda i: (0, 0))],
        compiler_params=_params("arbitrary"),
    )(x3, fn, target)


def _ffn_bwd_hidden(name, dyh, g, u, wd, hooks=(), tm=512):
    S = dyh.shape[0]

    def body(dyh_ref, g_ref, u_ref, wd_ref, dg_ref, du_ref):
        dyh = dyh_ref[...]
        for ci in range(F // FT):
            sl = slice(ci * FT, (ci + 1) * FT)
            da = _mm_nt(dyh, wd_ref[sl, :])
            gv = g_ref[:, sl].astype(f32)
            uv = u_ref[:, sl].astype(f32)
            s = jax.nn.sigmoid(gv)
            silu = gv * s
            du_ref[:, sl] = (da * silu).astype(CDT)
            dg_ref[:, sl] = (da * uv * (s + silu - silu * s)).astype(CDT)

    hid = pl.BlockSpec((tm, F), lambda i: (i, 0))
    hidden = jax.ShapeDtypeStruct((S, F), CDT)
    return _call(
        body, hooks, [dyh, g, u, wd], name=name, grid=(S // tm,), out_shape=[hidden, hidden],
        in_specs=[pl.BlockSpec((tm, D), lambda i: (i, 0)), hid, hid, _wblock(wd)], out_specs=[hid, hid],
        compiler_params=_params("arbitrary"),
    )


def _ffn_bwd_in(name, dy, x, dg, du, nrm, wg, wu, hooks=(), tm=512):
    S = x.shape[0]

    def body(dy_ref, x_ref, dg_ref, du_ref, n_ref, wg_ref, wu_ref, dx_ref, dn_ref, acc_ref):
        n = n_ref[...]
        r, xh, _ = _rms_fwd(x_ref[...], n)
        for ci in range(F // FT):
            sl = slice(ci * FT, (ci + 1) * FT)
            dh = _mm(dg_ref[:, sl], wg_ref[sl, :]) + _mm(du_ref[:, sl], wu_ref[sl, :])
            if ci == 0:
                acc_ref[...] = dh
            else:
                acc_ref[...] += dh
        dx, dn = _rms_bwd(acc_ref[...], r, xh, n)
        dx_ref[...] = dy_ref[...] + dx

        @pl.when(pl.program_id(0) == 0)
        def _():
            dn_ref[...] = jnp.zeros_like(dn_ref)

        dn_ref[...] += dn

    tok = pl.BlockSpec((tm, D), lambda i: (i, 0))
    hid = pl.BlockSpec((tm, F), lambda i: (i, 0))
    return _call(
        body, hooks, [dy, x, dg, du, nrm, wg, wu], name=name, grid=(S // tm,),
        out_shape=[jax.ShapeDtypeStruct((S, D), f32), jax.ShapeDtypeStruct((1, D), f32)],
        in_specs=[tok, tok, hid, hid, _resident((1, D), lambda i: (0, 0)), _wblock(wg), _wblock(wu)],
        out_specs=[tok, pl.BlockSpec((1, D), lambda i: (0, 0))],
        scratch_shapes=[pltpu.VMEM((tm, D), f32)],
        compiler_params=_params("arbitrary"),
    )


TN_ROWS = 512


def _tn(name, xs, ys, block_of, hooks=(), side=None):
    S, M = xs.shape
    B = ys.shape[0]
    tr = TN_ROWS if M % TN_ROWS == 0 else M // 2
    assert M % tr == 0 and tr % 128 == 0
    nt = M // tr

    def body(x_ref, y_ref, o_ref):
        o_ref[...] = _mm_tn(x_ref[...], y_ref[...]).astype(CDT)

    return _call(
        body, hooks, [xs, ys], name=name, grid=(B, nt), side=side,
        out_shape=[jax.ShapeDtypeStruct((B * M, D), CDT)],
        in_specs=[pl.BlockSpec((S, tr), lambda b, i: (0, i)), pl.BlockSpec((None, S, D), lambda b, i: (b, 0, 0))],
        out_specs=[pl.BlockSpec((tr, D), lambda b, i: (block_of(b) * nt + i, 0))],
        compiler_params=_params("arbitrary", "arbitrary"),
    )


def _wgrad(name, xs, y, hooks=(), side=None):
    return _tn(name, xs, y[None], lambda b: 0, hooks, side)


def _merge_bwd_act(dx2, ya, yb, proj, wa, wb, wo, hooks=(), side=None, tm=512):
    S = dx2.shape[0]

    def body(dx_ref, ya_ref, yb_ref, gt_ref, wa_ref, wb_ref, wo_ref,
             dp_ref, da_ref, dr_ref, mix_ref, dxb_ref, dya_ref, dyb_ref):
        dxb = dx_ref[...].astype(CDT)
        dxb_ref[...] = dxb
        dmix = _mm_nt(dxb, wo_ref[...])
        ya = ya_ref[...].astype(f32)
        yb = yb_ref[...].astype(f32)
        sa = jax.nn.sigmoid(gt_ref[0].astype(f32))
        sb = jax.nn.sigmoid(gt_ref[1].astype(f32))
        mix_ref[...] = (sa * ya + sb * yb).astype(CDT)
        dya = (dmix * sa).astype(CDT)
        dyb = (dmix * sb).astype(CDT)
        dya_ref[...] = dya
        dyb_ref[...] = dyb
        dp_ref[0] = (dmix * ya * sa * (1.0 - sa)).astype(CDT)
        dp_ref[1] = (dmix * yb * sb * (1.0 - sb)).astype(CDT)
        da_ref[...] = _mm_nt(dya, wa_ref[...]).astype(CDT)
        dr_ref[...] = _mm_nt(dyb, wb_ref[...]).astype(CDT)

    tok = pl.BlockSpec((tm, D), lambda i: (i, 0))
    gates = pl.BlockSpec((2, tm, D), lambda i: (SLOT_GA // 2, i, 0))
    act = jax.ShapeDtypeStruct((S, D), CDT)
    return _call(
        body, hooks, [dx2, ya, yb, proj, wa, wb, wo], name="merge_bwd_act", grid=(S // tm,), side=side,
        out_shape=[jax.ShapeDtypeStruct((8, S, D), CDT), act, act, act, act, act, act],
        in_specs=[tok, tok, tok, gates, _wblock(wa), _wblock(wb), _wblock(wo)],
        out_specs=[gates, tok, tok, tok, tok, tok, tok],
        compiler_params=_params("arbitrary"),
    )


def _sgu_bwd(da, proj, dproj, gn, bn, ws, wst, bsc, hooks=(), tm=512):
    S = proj.shape[1]
    GW = D // G

    def body(da_ref, p_ref, dpin_ref, gn_ref, bn_ref, ws_ref, wst_ref, bs_ref,
             dp_ref, dws_ref, dbs_ref, dgn_ref, dbn_ref, ds_ref, dvn_ref):
        @pl.when(pl.program_id(0) == 0)
        def _():
            dws_ref[...] = jnp.zeros_like(dws_ref)
            dbs_ref[...] = jnp.zeros_like(dbs_ref)
            dgn_ref[...] = jnp.zeros_like(dgn_ref)
            dbn_ref[...] = jnp.zeros_like(dbn_ref)

        pu = p_ref[0].astype(f32)
        pv = p_ref[1].astype(f32)
        ua = _gelu(pu)
        va = _gelu(pv)
        gn = gn_ref[...]
        rstd, vhat, vn = _sgu_norm(va, gn, bn_ref[...])
        vnb = vn.astype(CDT)
        dav = da_ref[...].astype(f32)
        dsb = (dav * ua).astype(CDT)
        ones = jnp.ones((8, GW), CDT)
        for ch in range(tm // C):
            rs = slice(ch * C, (ch + 1) * C)
            for gi in range(G):
                cs = slice(gi * GW, (gi + 1) * GW)
                s = _mm(ws_ref[gi], vnb[rs, cs]) + bs_ref[gi]
                ds_ref[rs, cs] = s
                dsg = dsb[rs, cs]
                dws_ref[gi] += _mm_nt(dsg, vnb[rs, cs])
                dbs_ref[gi] += _mm_nt(ones, dsg)
                dvn_ref[rs, cs] = _mm(wst_ref[gi], dsg)
        dp_ref[0] = (dav * ds_ref[...] * _gelu_grad(pu)).astype(CDT)
        dvn = dvn_ref[...]
        dgn_ref[...] += jnp.sum(dvn * vhat, axis=0, keepdims=True)
        dbn_ref[...] += jnp.sum(dvn, axis=0, keepdims=True)
        dvh = dvn * gn
        dva = rstd * (dvh - jnp.mean(dvh, axis=-1, keepdims=True) - vhat * jnp.mean(dvh * vhat, axis=-1, keepdims=True))
        dp_ref[1] = (dva * _gelu_grad(pv)).astype(CDT)

    uv = pl.BlockSpec((2, tm, D), lambda i: (0, i, 0))
    row = _resident((1, D), lambda i: (0, 0))
    return _call(
        body, hooks, [da, proj, dproj, gn, bn, ws, wst, bsc], name="sgu_bwd", grid=(S // tm,),
        out_shape=[jax.ShapeDtypeStruct(dproj.shape, CDT), jax.ShapeDtypeStruct((G, C, C), f32),
                   jax.ShapeDtypeStruct((G, 8, C), f32), jax.ShapeDtypeStruct((1, D), f32), jax.ShapeDtypeStruct((1, D), f32)],
        in_specs=[pl.BlockSpec((tm, D), lambda i: (i, 0)), uv, _HBM, row, row,
                  _resident((G, C, C), lambda i: (0, 0, 0)), _resident((G, C, C), lambda i: (0, 0, 0)),
                  _resident((G, C, 1), lambda i: (0, 0, 0))],
        out_specs=[uv, pl.BlockSpec((G, C, C), lambda i: (0, 0, 0)), pl.BlockSpec((G, 8, C), lambda i: (0, 0, 0)),
                   pl.BlockSpec((1, D), lambda i: (0, 0)), pl.BlockSpec((1, D), lambda i: (0, 0))],
        scratch_shapes=[pltpu.VMEM((tm, D), f32), pltpu.VMEM((tm, D), f32)],
        input_output_aliases={2: 0},
        compiler_params=_params("arbitrary"),
    )


def _ret_bwd(dr, R, sfs, sbs, proj, dproj, cos, sin, dl, hooks=()):
    S = proj.shape[1]
    NC = S // RC
    assert NC % 2 == 0

    def body(dr_ref, R_ref, sf_ref, sb_ref, q_ref, k_ref, v_ref, g_ref, dpin_ref, cos_ref, sin_ref, dl_ref,
             dp_ref, dd_ref, dR_ref, gb_ref, gf_ref, acc_ref):
        t = _decay_tables(dl_ref)

        def gate_norm_bwd(i, carry):
            rs = pl.ds(pl.multiple_of(i * ROW_TILE, ROW_TILE), ROW_TILE)
            Rv = R_ref[rs, :]
            rstd = lax.rsqrt(jnp.mean(Rv * Rv, axis=-1, keepdims=True) + EPS)
            rn = Rv * rstd
            gv = g_ref[rs, :].astype(f32)
            s = jax.nn.sigmoid(gv)
            drv = dr_ref[rs, :].astype(f32)
            dp_ref[3, rs, :] = (drv * rn * (s * (1.0 + gv * (1.0 - s)))).astype(CDT)
            drn = drv * gv * s
            dR_ref[rs, :] = (rstd * (drn - rn * jnp.mean(drn * rn, axis=-1, keepdims=True))).astype(CDT)
            return carry

        lax.fori_loop(0, S // ROW_TILE, gate_norm_bwd, 0)

        def chunk(n):
            rows = pl.ds(pl.multiple_of(n * RC, RC), RC)
            return rows, q_ref[rows, :], k_ref[rows, :], v_ref[rows, :], dR_ref[rows, :]

        def emit_kv(rows, dk, dv, final):
            if not final:
                dp_ref[1, rows, :] = dk.astype(CDT)
                dp_ref[2, rows, :] = dv.astype(CDT)
            else:
                co, si = cos_ref[rows, :], sin_ref[rows, :]
                dk = dp_ref[1, rows, :].astype(f32) + dk
                dp_ref[1, rows, :] = (_unrotate(dk, co, si) * K_SCALE).astype(CDT)
                dp_ref[2, rows, :] = (dp_ref[2, rows, :].astype(f32) + dv).astype(CDT)

        gb_ref[...] = jnp.zeros_like(gb_ref)
        gf_ref[...] = jnp.zeros_like(gf_ref)
        acc_ref[...] = jnp.zeros_like(acc_ref)
        span = t["dmat"] * jnp.abs(t["d"])
        span_f, span_b = jnp.where(t["lower"], span, 0.0), jnp.where(t["lower"], 0.0, span)

        def ascend(n, final):
            rows, qn, kn, vn, dRn = chunk(n)
            qf, kf = qn.astype(f32), kn.astype(f32)
            sc = _mm_nt(qn, kn)
            dA = _mm_nt(dRn, vn)
            prod = sc * dA
            lgf_part = jnp.sum(prod * span_f, axis=0, keepdims=True)
            lgb_part = jnp.sum(prod * span_b, axis=0, keepdims=True)
            dsc = (dA * t["dmat"]).astype(CDT)
            dq = _mm(dsc, kn)
            scT = (_mm_nt(kn, qn) * t["dmat_t"]).astype(CDT)
            dscT = (_mm_nt(vn, dRn) * t["dmat_t"]).astype(CDT)
            dk = _mm(dscT, qn)
            dv = _mm(scT, dRn)
            sfb = sf_ref[n]
            sbb = sb_ref[n]
            qdf = qf * t["fq"]
            dqdf = _mm_nt(dRn, sfb)
            dq += dqdf * t["fq"]
            lgf_row = jnp.sum(qdf * dqdf * (t["pos"] + 1.0), axis=0, keepdims=True)
            qdb = qf * t["bq"]
            dqdb = _mm_nt(dRn, sbb)
            dq += dqdb * t["bq"]
            lgb_row = jnp.sum(qdb * dqdb * (RC - t["pos"]), axis=0, keepdims=True)
            gb = gb_ref[...]
            gbb = gb.astype(CDT)
            kdb = kf * t["bk"]
            dkdb = _mm_nt(vn, gbb)
            dk += dkdb * t["bk"]
            dv += _mm(kdb.astype(CDT), gbb)
            lgb_row += jnp.sum(kdb * dkdb * t["pos"], axis=0, keepdims=True)
            lgb_row += float(RC) * t["lamb"] * jnp.sum(gb * sbb.astype(f32), axis=0, keepdims=True)
            co, si = cos_ref[rows, :], sin_ref[rows, :]
            dp_ref[0, rows, :] = _unrotate(dq, co, si).astype(CDT)
            emit_kv(rows, dk, dv, final)
            acc_ref[0:1, :] += lgf_row + lgf_part
            acc_ref[1:2, :] += lgb_row + lgb_part
            gb_ref[...] = gb * t["lamb"] + _mm_tn(qdb.astype(CDT), dRn)

        def descend(n, final):
            rows, qn, kn, vn, dRn = chunk(n)
            gf = gf_ref[...]
            gfb = gf.astype(CDT)
            kdf = kn.astype(f32) * t["fk"]
            dkdf = _mm_nt(vn, gfb)
            lgf_row = jnp.sum(kdf * dkdf * (RC - 1.0 - t["pos"]), axis=0, keepdims=True)
            lgf_row += float(RC) * t["lamf"] * jnp.sum(gf * sf_ref[n].astype(f32), axis=0, keepdims=True)
            acc_ref[0:1, :] += lgf_row
            emit_kv(rows, dkdf * t["fk"], _mm(kdf.astype(CDT), gfb), final)
            gf_ref[...] = gf * t["lamf"] + _mm_tn((qn.astype(f32) * t["fq"]).astype(CDT), dRn)

        def sweep(final):
            def step(i, carry):
                ascend(i, final)
                descend(NC - 1 - i, final)
                return carry
            return step

        lax.fori_loop(0, NC // 2, sweep(False), 0)
        lax.fori_loop(NC // 2, NC, sweep(True), 0)
        dlg = jnp.sum(acc_ref[...], axis=1, keepdims=True)
        dlogit = dlg * jax.nn.sigmoid(-dl_ref[:, 0:1])
        lane = lax.broadcasted_iota(jnp.int32, (8, 128), 1)
        dd_ref[...] = jnp.where(lane == pl.program_id(0), jnp.broadcast_to(dlogit, (8, 128)), 0.0)

    def seg(slot):
        return pl.BlockSpec((None, S, DK), lambda h: (slot, 0, h))

    head = pl.BlockSpec((S, DK), lambda h: (0, h))
    states = pl.BlockSpec((None, NC, DK, DK), lambda h: (h, 0, 0, 0))
    return _call(
        body, hooks, [dr, R, sfs, sbs, proj, proj, proj, proj, dproj, cos, sin, dl], name="ret_bwd", grid=(H,),
        out_shape=[jax.ShapeDtypeStruct(dproj.shape, CDT), jax.ShapeDtypeStruct((H, 8, 128), f32)],
        in_specs=[head, head, states, states, seg(SLOT_Q), seg(SLOT_K), seg(SLOT_VR), seg(SLOT_GR), _HBM,
                  _resident((S, DK // 2), lambda h: (0, 0)), _resident((S, DK // 2), lambda h: (0, 0)),
                  pl.BlockSpec((None, 8, DK), lambda h: (h, 0, 0))],
        out_specs=[pl.BlockSpec((4, S, DK), lambda h: (1, 0, h), pipeline_mode=pl.Buffered(1)),
                   pl.BlockSpec((None, 8, 128), lambda h: (h, 0, 0))],
        scratch_shapes=[pltpu.VMEM((S, DK), CDT),
                        pltpu.VMEM((DK, DK), f32), pltpu.VMEM((DK, DK), f32), pltpu.VMEM((8, DK), f32)],
        input_output_aliases={8: 0},
        compiler_params=_params("arbitrary"),
    )


def _proj_bwd_act(dproj, dx2, x1, nrm, wfull, hooks=(), tm=512):
    S = x1.shape[0]

    def body(dp_ref, dx2_ref, x_ref, n_ref, w_ref, dx_ref, dxh_ref, dn_ref, db_ref, acc_ref):
        @pl.when(pl.program_id(0) == 0)
        def _():
            dn_ref[...] = jnp.zeros_like(dn_ref)
            db_ref[...] = jnp.zeros_like(db_ref)

        for p in range(8):
            seg = SEG_OF_SLOT[p]
            dp = dp_ref[p]
            db_ref[seg] += jnp.sum(dp.astype(f32), axis=0, keepdims=True)
            dh = _mm_nt(dp, w_ref[seg * D:(seg + 1) * D, :])
            if p == 0:
                acc_ref[...] = dh
            else:
                acc_ref[...] += dh
        n = n_ref[...]
        r, xh, _ = _rms_fwd(x_ref[...], n)
        dx, dn = _rms_bwd(acc_ref[...], r, xh, n)
        dx = dx2_ref[...] + dx
        dx_ref[...] = dx
        dxh_ref[...] = (0.5 * dx).astype(CDT)
        dn_ref[...] += dn

    tok = pl.BlockSpec((tm, D), lambda i: (i, 0))
    return _call(
        body, hooks, [dproj, dx2, x1, nrm, wfull], name="proj_bwd_act", grid=(S // tm,),
        out_shape=[jax.ShapeDtypeStruct((S, D), f32), jax.ShapeDtypeStruct((S, D), CDT), jax.ShapeDtypeStruct((1, D), f32),
                   jax.ShapeDtypeStruct((8, 1, D), f32)],
        in_specs=[pl.BlockSpec((8, tm, D), lambda i: (0, i, 0)), tok, tok, _resident((1, D), lambda i: (0, 0)),
                  _resident((8 * D, D), lambda i: (0, 0))],
        out_specs=[tok, tok, pl.BlockSpec((1, D), lambda i: (0, 0)), pl.BlockSpec((8, 1, D), lambda i: (0, 0, 0))],
        scratch_shapes=[pltpu.VMEM((tm, D), f32)],
        compiler_params=_params("arbitrary"),
    )


def _sum_side(mines, sibs):
    n = len(mines)

    def make(grid):
        per = NCHIP // min(math.prod(grid), NCHIP)
        groups = NCHIP // per

        def step(*ids):
            t = 0
            for a, size in zip(ids, grid):
                t = t * size + a
            return t

        specs = [pl.BlockSpec((per, m.shape[1], D), lambda *ids: (jnp.minimum(step(*ids), groups - 1), 0, 0)) for m in mines]

        def fn(ids, ins, outs):
            @pl.when(step(*ids) < groups)
            def _():
                for m_ref, s_ref, o_ref in zip(ins[:n], ins[n:], outs):
                    o_ref[...] = (m_ref[...].astype(f32) + s_ref[...].astype(f32)).astype(CDT)

        return fn, list(mines) + list(sibs), specs + specs, specs, [jax.ShapeDtypeStruct(m.shape, CDT) for m in mines]

    return make


def _rs_sum(name, mines, sibs):
    return _call(lambda: None, (), [], name=name, grid=(NCHIP,), in_specs=[], out_specs=[], out_shape=[],
                 side=_sum_side(mines, sibs), compiler_params=_params("arbitrary"))


def _adamw_math(g, w, m, v):
    m2 = ADAM_B1 * m + (1.0 - ADAM_B1) * g
    v2 = ADAM_B2 * v + (1.0 - ADAM_B2) * (g * g)
    delta = -ADAM_LR * ((m2 / BC1) / (jnp.sqrt(v2 / BC2) + ADAM_EPS) + ADAM_WD * w)
    return delta, m2, v2


def _adamw_big(name, landed, w, m, v, after):
    n = len(w)
    rows = w[0].shape[0]
    tr = 256 if rows % 256 == 0 else (rows // 2 if rows > 256 else rows)
    nt = rows // tr

    def body(*refs):
        ins, outs = refs[:4 * n], refs[4 * n + 1:]
        for j in range(n):
            @pl.when(pl.program_id(0) == j)
            def _(j=j):
                l_ref, w_ref, m_ref, v_ref = ins[j], ins[n + j], ins[2 * n + j], ins[3 * n + j]
                g = l_ref[0].astype(f32)
                for k in range(1, NCHIP):
                    g = g + l_ref[k].astype(f32)
                outs[4 * j][...] = g
                outs[4 * j + 1][...], outs[4 * j + 2][...], outs[4 * j + 3][...] = _adamw_math(g, w_ref[...], m_ref[...], v_ref[...])

    def tile(j):
        return lambda jj, i: jnp.clip((jj - j) * nt + i, 0, nt - 1)

    blk = [pl.BlockSpec((tr, D), lambda jj, i, t=tile(j): (t(jj, i), 0)) for j in range(n)]
    lnd = [pl.BlockSpec((NCHIP, tr, D), lambda jj, i, t=tile(j): (0, t(jj, i), 0)) for j in range(n)]
    o = jax.ShapeDtypeStruct((rows, D), f32)
    res = pl.pallas_call(
        body, name=name, grid=(n, nt), out_shape=[o] * (4 * n),
        in_specs=lnd + blk + blk + blk + [_HBM],
        out_specs=[blk[j] for j in range(n) for _ in range(4)],
        compiler_params=_params("arbitrary", "arbitrary"),
    )(*landed, *w, *m, *v, after)
    return [res[4 * j:4 * j + 4] for j in range(n)]


ROW_FFN1_NORM, ROW_MIX_NORM, ROW_SGU_G, ROW_SGU_B, ROW_FFN2_NORM, ROW_FINAL_NORM, ROW_B_IN = 0, 1, 2, 3, 4, 5, 8
ROW_WS, ROW_BS, ROW_DECAY = 0, G * C, G * C + G * 8


def _adamw_small(ga, gb, gn1, gl, params):
    def body(ga_ref, gb_ref, gn1_ref, gl_ref, *refs):
        ins, outs = refs[:30], refs[30:]

        def total(ref, r0, n):
            g = ref[0, r0:r0 + n, :]
            for j in range(1, NDEV):
                g = g + ref[j, r0:r0 + n, :]
            return g

        def apply(i, g, rows=slice(None)):
            w, m, v = ins[3 * i][rows, :], ins[3 * i + 1][rows, :], ins[3 * i + 2][rows, :]
            outs[4 * i][rows, :] = g
            outs[4 * i + 1][rows, :], outs[4 * i + 2][rows, :], outs[4 * i + 3][rows, :] = _adamw_math(g, w, m, v)

        outs[40][...] = total(gl_ref, 0, 8)
        apply(0, total(gn1_ref, 0, 1))
        for i, r in enumerate((ROW_FFN1_NORM, ROW_MIX_NORM, ROW_SGU_G, ROW_SGU_B, ROW_FFN2_NORM, ROW_FINAL_NORM)):
            if i:
                apply(i, total(ga_ref, r, 1))
        apply(6, total(ga_ref, ROW_B_IN, 8))
        apply(7, total(gb_ref, ROW_WS, G * C))
        for gi in range(G):
            apply(8, total(gb_ref, ROW_BS + 8 * gi, 1), slice(gi, gi + 1))
        dec = total(gb_ref, ROW_DECAY, 8)
        for hh in range(1, H):
            dec = dec + total(gb_ref, ROW_DECAY + 8 * hh, 8)
        apply(9, dec)

    flat = [a for p in params for a in p]
    out_shape = [jax.ShapeDtypeStruct(p[0].shape, f32) for p in params for _ in range(4)]
    out_shape.append(jax.ShapeDtypeStruct((8, 128), f32))
    vm = pl.BlockSpec(memory_space=pltpu.VMEM)
    return pl.pallas_call(
        body, name="adamw_small", out_shape=out_shape,
        in_specs=[vm] * (4 + len(flat)), out_specs=[vm] * len(out_shape),
        compiler_params=pltpu.CompilerParams(vmem_limit_bytes=VMEM_LIMIT),
    )(ga, gb, gn1, gl, *flat)


def kernel(x, ffn1_norm, ffn1_w_gate, ffn1_w_up, ffn1_w_down, mix_norm, w_in, b_in, sgu_norm_g, sgu_norm_b, sgu_w_s, sgu_b_s, ret_decay_logit, w_branch_a, w_branch_b, w_out, ffn2_norm, ffn2_w_gate, ffn2_w_up, ffn2_w_down, final_norm, loss_target, m_ffn1_norm, m_ffn1_w_gate, m_ffn1_w_up, m_ffn1_w_down, m_mix_norm, m_w_in, m_b_in, m_sgu_norm_g, m_sgu_norm_b, m_sgu_w_s, m_sgu_b_s, m_ret_decay_logit, m_w_branch_a, m_w_branch_b, m_w_out, m_ffn2_norm, m_ffn2_w_gate, m_ffn2_w_up, m_ffn2_w_down, m_final_norm, v_ffn1_norm, v_ffn1_w_gate, v_ffn1_w_up, v_ffn1_w_down, v_mix_norm, v_w_in, v_b_in, v_sgu_norm_g, v_sgu_norm_b, v_sgu_w_s, v_sgu_b_s, v_ret_decay_logit, v_w_branch_a, v_w_branch_b, v_w_out, v_ffn2_norm, v_ffn2_w_gate, v_ffn2_w_up, v_ffn2_w_down, v_final_norm):
    args = dict(locals())
    S = x.shape[1]
    xs = x[0]
    target = loss_target[0]

    def buf_layout(name, a):
        a = a[0]
        return a.T if name in W_TRANSPOSED else a

    sh = {n: buf_layout(n, args[n]).astype(CDT) for n in W_NAMES}
    wf = {}

    b3 = b_in.reshape(8, 1, D)
    ws = sgu_w_s[0].astype(CDT)
    wst = jnp.swapaxes(sgu_w_s[0], 1, 2).astype(CDT)
    bsc = sgu_b_s[0].reshape(G, C, 1)
    dl = jnp.zeros((H, 8, DK), f32).at[:, 0:2, :].set(jnp.broadcast_to(ret_decay_logit[0].T[:, :, None], (H, 2, DK)))
    theta = ROPE_BASE ** (-jnp.arange(0, DK, 2, dtype=f32) / DK)
    ang = jnp.arange(S, dtype=f32)[:, None] * theta[None, :]
    cos, sin = jnp.cos(ang), jnp.sin(ang)
    fnorm = final_norm.reshape(1, D)

    f1 = ("ffn1_w_gate", "ffn1_w_up", "ffn1_w_down")
    f2 = ("ffn2_w_gate", "ffn2_w_up", "ffn2_w_down")
    br = ("w_branch_a", "w_branch_b", "w_out")
    for cid, names in enumerate((f1, ("w_in",), br, f2)):
        wf.update(zip(names, _sequence("ag_" + names[0], 1 + cid, NEIGHBOURS, [_ag_hook(sh[n]) for n in names])))
    x1, g1, u1, a1, hf1 = _ffn_fwd("ffn1_fwd", xs, ffn1_norm, *[wf[n] for n in f1])
    proj, h2 = _proj_fwd(x1, mix_norm, wf["w_in"], b3, cos, sin)
    a = _sgu_fwd(proj, sgu_norm_g, sgu_norm_b, ws, bsc)
    R, r, sfs, sbs = _ret_fwd(proj, dl)
    x2, ya, yb = _merge_fwd(a, r, proj, x1, *[wf[n] for n in br])
    x3, g2, u2, a2, hf2 = _ffn_fwd("ffn2_fwd", x2, ffn2_norm, *[wf[n] for n in f2])
    dx3, dyh2, d_final, loss_part = _loss_head(x3, fnorm, target)

    my_c = lax.axis_index("c").astype(jnp.int32).reshape(1)
    gw, landed, sequenced = {}, {}, []

    def d2d(*names):
        return [_rs_d2d_hook(gw[n]) for n in names]

    def behind(x, token):
        return lax.optimization_barrier((x, token))[0]

    def sums(pairs):
        return _sum_side([mine for _, mine in pairs], [sib for sib, _ in pairs])

    def to_chips(names, parts, more=()):
        parts = list(parts)
        token = parts[0]
        if sequenced:
            parts[0] = behind(parts[0], sequenced[-1])
        hooks = [_rs_ici_hook(p) for p in parts] + ([_small_hook(list(more))] if more else [])
        reach = EVERYONE if more else CHIPS
        got = _sequence("rs_chips_" + names[0], REACH_ID[reach], reach, hooks)
        sequenced.append(got[0])
        landed.update(zip(names, got))
        return got[len(names):], token

    wg2, wu2, wd2 = f2
    (gw[wd2],) = _wgrad("ffn2_wd_grad", a2, dyh2)
    dg2, du2, *pair_d = _ffn_bwd_hidden("ffn2_bwd_hidden", dyh2, g2, u2, wf[wd2], d2d(wd2))
    (gw[wg2],) = _wgrad("ffn2_wg_grad", dg2, hf2)
    gw[wu2], *pair_g = _wgrad("ffn2_wu_grad", du2, hf2, d2d(wg2))
    dx2, d_ffn2n, *pair_u = _ffn_bwd_in("ffn2_bwd_in", dx3, x2, dg2, du2, ffn2_norm, wf[wg2], wf[wu2], d2d(wu2))
    dproj, da, dr, mix, dx2b, dya, dyb, *parts = _merge_bwd_act(dx2, ya, yb, proj, *[wf[n] for n in br],
                                                                side=sums([pair_d, pair_g, pair_u]))
    _, token = to_chips([wd2, wg2, wu2], parts)
    dproj, d_ws, d_bs, d_gn, d_bn = _sgu_bwd(behind(da, token), proj, dproj, sgu_norm_g, sgu_norm_b, ws, wst, bsc)
    dproj, d_dec = _ret_bwd(dr, R, sfs, sbs, proj, dproj, cos, sin, dl)
    (gw["w_in"],) = _tn("win_grad", h2, dproj, _seg_of_slot)
    gw["w_out"], *pair_win = _wgrad("wo_grad", mix, dx2b, d2d("w_in"))
    gw["w_branch_a"], part_win = _wgrad("wa_grad", a, dya, side=sums([pair_win]))
    small_sgu = jnp.concatenate([d_ws.reshape(G * C, C), d_bs.reshape(G * 8, C)], axis=0)
    (g_sgu, gl), token = to_chips(["w_in"], [part_win], [small_sgu, loss_part])
    (gw["w_branch_b"],) = _wgrad("wb_grad", r, behind(dyb, token))
    dx1, dyh1, d_mixn, d_bin, *flat = _proj_bwd_act(behind(dproj, token), dx2, x1, mix_norm, wf["w_in"], d2d(*br))
    pairs_br = [flat[0:2], flat[2:4], flat[4:6]]
    small_a = jnp.concatenate([jnp.zeros((1, D), f32), d_mixn, d_gn, d_bn, d_ffn2n, d_final, jnp.zeros((2, D), f32),
                               d_bin.reshape(8, D)], axis=0)
    parts_br = _rs_sum("rs_sum_" + br[0], [p[1] for p in pairs_br], [p[0] for p in pairs_br])
    (ga, g_dec), token = to_chips(list(br), parts_br, [small_a, d_dec.reshape(H * 8, 128)])
    gb = jnp.concatenate([g_sgu, g_dec], axis=1)
    wg1, wu1, wd1 = f1
    (gw[wd1],) = _wgrad("ffn1_wd_grad", a1, behind(dyh1, token))
    dg1, du1, *pair_d = _ffn_bwd_hidden("ffn1_bwd_hidden", dyh1, g1, u1, wf[wd1], d2d(wd1))
    dg1 = behind(dg1, to_chips([wd1], _rs_sum("rs_sum_" + wd1, [pair_d[1]], [pair_d[0]]))[1])
    (gw[wg1],) = _wgrad("ffn1_wg_grad", dg1, hf1)
    gw[wu1], *pair_g = _wgrad("ffn1_wu_grad", du1, hf1, d2d(wg1))
    dx1 = behind(dx1, to_chips([wg1], _rs_sum("rs_sum_" + wg1, [pair_g[1]], [pair_g[0]]))[1])
    dxs, d_ffn1n, *pair_u = _ffn_bwd_in("ffn1_bwd_in", dx1, xs, dg1, du1, ffn1_norm, wf[wg1], wf[wu1], d2d(wu1))
    (gn1,), token = to_chips([wu1], _rs_sum("rs_sum_" + wu1, [pair_u[1]], [pair_u[0]]), [d_ffn1n])

    out = {"grad_x": dxs[None]}

    def native(name, a):
        a = a.T if name in W_TRANSPOSED else a
        return a[None]

    after = token
    for names in (f2, ("w_in",), br, (f1[2], f1[0], f1[1])):
        res = _adamw_big("adamw_" + names[0], [landed[n] for n in names], [buf_layout(n, args[n]) for n in names],
                         [buf_layout(n, args["m_" + n]) for n in names], [buf_layout(n, args["v_" + n]) for n in names], after)
        after = res[-1][0]
        for n, four in zip(names, res):
            for pre, val in zip(("grad_", "delta_", "new_m_", "new_v_"), four):
                out[pre + n] = native(n, val)

    def pad_decay(a):
        return jnp.zeros((8, 128), f32).at[0:2, 0:H].set(a[0])

    small = [
        ("ffn1_norm", lambda a: a, lambda a: a), ("mix_norm", lambda a: a, lambda a: a),
        ("sgu_norm_g", lambda a: a, lambda a: a), ("sgu_norm_b", lambda a: a, lambda a: a),
        ("ffn2_norm", lambda a: a, lambda a: a),
        ("final_norm", lambda a: a.reshape(1, D), lambda a: a.reshape(D)),
        ("b_in", lambda a: a.reshape(8, D), lambda a: a.reshape(1, 8 * D)),
        ("sgu_w_s", lambda a: a.reshape(G * C, C), lambda a: a.reshape(1, G, C, C)),
        ("sgu_b_s", lambda a: a[0], lambda a: a[None]),
        ("ret_decay_logit", pad_decay, lambda a: a[None, 0:2, 0:H]),
    ]
    res = _adamw_small(ga, gb, gn1, gl, [(to(args[n]), to(args["m_" + n]), to(args["v_" + n])) for n, to, _ in small])
    out["loss"] = res[40][0, 0]
    for i, (n, _, back) in enumerate(small):
        for j, pre in enumerate(("grad_", "delta_", "new_m_", "new_v_")):
            out[pre + n] = back(res[4 * i + j])

    weights = ("ffn1_norm", "ffn1_w_gate", "ffn1_w_up", "ffn1_w_down", "mix_norm", "w_in", "b_in", "sgu_norm_g",
               "sgu_norm_b", "sgu_w_s", "sgu_b_s", "ret_decay_logit", "w_branch_a", "w_branch_b", "w_out", "ffn2_norm",
               "ffn2_w_gate", "ffn2_w_up", "ffn2_w_down", "final_norm")
    return (out["loss"], out["grad_x"], *[out["grad_" + n] for n in weights], *[out["delta_" + n] for n in weights],
            *[out["new_m_" + n] for n in weights], *[out["new_v_" + n] for n in weights])
```

```python
import math

import jax
import jax.numpy as jnp
from jax import lax
from jax.experimental import pallas as pl
from jax.experimental.pallas import tpu as pltpu
from jax.experimental.pallas import tpu_sc as plsc

f32 = jnp.float32
CDT = jnp.bfloat16

D = 1024
F = 2816
C = 128
RC = 256
H = 4
DK = 256
G = 4
NDEV = 8
NCHIP = 4
EPS = 1e-6
ROPE_BASE = 10000.0
FT = 256
V7X_VMEM_BYTES = 64 * 1024 * 1024
VMEM_LIMIT = V7X_VMEM_BYTES - 8 * 1024 * 1024

ADAM_LR, ADAM_B1, ADAM_B2, ADAM_EPS, ADAM_WD, ADAM_STEP = 0.001, 0.9, 0.999, 1e-08, 0.01, 10
BC1 = 1.0 - ADAM_B1 ** ADAM_STEP
BC2 = 1.0 - ADAM_B2 ** ADAM_STEP

W_ROWS = dict(ffn1_w_gate=352, ffn1_w_up=352, ffn1_w_down=352, w_in=1024, w_branch_a=128, w_branch_b=128, w_out=128,
              ffn2_w_gate=352, ffn2_w_up=352, ffn2_w_down=352)
W_NAMES = tuple(W_ROWS)
W_TRANSPOSED = ("ffn1_w_gate", "ffn1_w_up", "ffn2_w_gate", "ffn2_w_up")

SLOT_U, SLOT_V, SLOT_GA, SLOT_GB, SLOT_Q, SLOT_K, SLOT_VR, SLOT_GR = range(8)


SEG_OF_SLOT = (0, 1, 6, 7, 2, 3, 4, 5)


def _seg_of_slot(p):
    return jnp.where(p < 2, p, jnp.where(p < 4, p + 4, p - 2))


def _mm(a, b):
    return jnp.dot(a, b, preferred_element_type=f32)


def _mm_nt(a, b):
    return lax.dot_general(a, b, (((1,), (1,)), ((), ())), preferred_element_type=f32)


def _mm_tn(a, b):
    return lax.dot_general(a, b, (((0,), (0,)), ((), ())), preferred_element_type=f32)


def _params(*sem):
    return pltpu.CompilerParams(dimension_semantics=sem, vmem_limit_bytes=VMEM_LIMIT)


def _resident(shape, index_map):
    return pl.BlockSpec(shape, index_map, pipeline_mode=pl.Buffered(1))


def _gelu(x):
    return 0.5 * x * (1.0 + lax.erf(x * (1.0 / math.sqrt(2.0))))


def _gelu_grad(x):
    return 0.5 * (1.0 + lax.erf(x * (1.0 / math.sqrt(2.0)))) + x * jnp.exp(-0.5 * x * x) * (1.0 / math.sqrt(2.0 * math.pi))


def _rms_fwd(x, n):
    r = lax.rsqrt(jnp.mean(x * x, axis=-1, keepdims=True) + EPS)
    xh = x * r
    return r, xh, xh * n


def _rms_bwd(dh, r, xh, n):
    dxh = dh * n
    dx = r * (dxh - xh * jnp.mean(dxh * xh, axis=-1, keepdims=True))
    return dx, jnp.sum(dh * xh, axis=0, keepdims=True)


MESH_ID = pl.DeviceIdType.MESH
_HBM = pl.BlockSpec(memory_space=pltpu.HBM)


def _my_place():
    return lax.axis_index("x"), lax.axis_index("y"), lax.axis_index("c")


def _ici_peers(x, y, c):
    return [((1 - x, y, c), 2 * (1 - x) + y), ((x, 1 - y, c), 2 * x + 1 - y), ((1 - x, 1 - y, c), 2 * (1 - x) + 1 - y)]


class _Hook:
    def __init__(self, operands, out_shapes, n_remote, n_local, start, finish, relay=None):
        self.operands, self.out_shapes = list(operands), list(out_shapes)
        self.n_remote, self.n_local, self.start, self.finish = n_remote, n_local, start, finish
        self.relay = relay or (lambda *a: None)


def _call(body, hooks, operands, *, in_specs, out_specs, out_shape, grid=None, scratch_shapes=(), **kw):
    hooks = tuple(hooks)
    n_in, n_out, n_scr = len(in_specs), len(out_shape), len(scratch_shapes)
    h_ops = [a for h in hooks for a in h.operands]
    h_outs = [s for h in hooks for s in h.out_shapes]
    h_sems = [pltpu.SemaphoreType.DMA((n,)) for h in hooks for n in (h.n_remote, h.n_remote, max(h.n_local, 1))]

    def wrapped(*refs):
        ins, hin = refs[:n_in], refs[n_in:n_in + len(h_ops)]
        o0 = n_in + len(h_ops)
        outs, hout = refs[o0:o0 + n_out], refs[o0 + n_out:o0 + n_out + len(h_outs)]
        s0 = o0 + n_out + len(h_outs)
        scr, hsem = refs[s0:s0 + n_scr], refs[s0 + n_scr:]

        def run(phase):
            ip = op = 0
            for i, h in enumerate(hooks):
                ssem, rsem, lsem = hsem[3 * i:3 * i + 3]

                def rcopy(k, src, dst, dev, ssem=ssem, rsem=rsem):
                    return pltpu.make_async_remote_copy(src_ref=src, dst_ref=dst, send_sem=ssem.at[k], recv_sem=rsem.at[k],
                                                        device_id=dev, device_id_type=MESH_ID)

                def lcopy(k, src, dst, lsem=lsem):
                    return pltpu.make_async_copy(src, dst, lsem.at[k])

                getattr(h, phase)(hin[ip:ip + len(h.operands)], hout[op:op + len(h.out_shapes)], rcopy, lcopy)
                ip += len(h.operands)
                op += len(h.out_shapes)

        def at_edge(phase, last):
            if not hooks:
                return
            if grid is None:
                run(phase)
                return
            cond = None
            for ax, n in enumerate(grid):
                here = pl.program_id(ax) == (n - 1 if last else 0)
                cond = here if cond is None else cond & here
            pl.when(cond)(lambda: run(phase))

        at_edge("start", False)
        at_edge("relay", True)
        body(*ins, *outs, *scr)
        at_edge("finish", True)

    if grid is not None:
        kw["grid"] = grid
    return list(pl.pallas_call(
        wrapped, out_shape=list(out_shape) + h_outs, in_specs=list(in_specs) + [_HBM] * len(h_ops),
        out_specs=list(out_specs) + [_HBM] * len(h_outs), scratch_shapes=list(scratch_shapes) + h_sems, **kw,
    )(*operands, *h_ops))


def _rows(ref, start, n):
    return ref.at[pl.ds(start, n), :]


def _ag_hook(shard):
    rows = shard.shape[0]
    half = rows // 2
    assert half % 16 == 0

    def place():
        x, y, c = _my_place()
        devs = dict(sib=(x, y, 1 - c), xn=(1 - x, y, c), yn=(x, 1 - y, c))
        chips = dict(me=2 * x + y, xn=2 * (1 - x) + y, yn=2 * x + 1 - y, dg=2 * (1 - x) + 1 - y)
        return c, devs, chips

    def block(full, chip, c):
        return _rows(full, (2 * chip + c) * rows, rows)

    def halfblock(full, chip, c, upper):
        return _rows(full, (2 * chip + c) * rows + upper * half, half)

    def start(ins, outs, rcopy, lcopy):
        c, devs, chips = place()
        src, dst = ins[0], block(outs[0], chips["me"], c)
        lcopy(0, src, dst).start()
        for k, to in enumerate(("sib", "xn", "yn")):
            rcopy(k, src, dst, devs[to]).start()

    def relay(ins, outs, rcopy, lcopy):
        c, devs, chips = place()
        full = outs[0]
        blk = block(full, chips["xn"], c)
        rcopy(1, blk, blk, devs["xn"]).wait_recv()
        low = halfblock(full, chips["xn"], c, 0)
        rcopy(3, low, low, devs["yn"]).start()
        rcopy(5, blk, blk, devs["sib"]).start()
        blk = block(full, chips["yn"], c)
        rcopy(2, blk, blk, devs["yn"]).wait_recv()
        up = halfblock(full, chips["yn"], c, 1)
        rcopy(4, up, up, devs["xn"]).start()
        rcopy(6, blk, blk, devs["sib"]).start()
        low, up = halfblock(full, chips["dg"], c, 0), halfblock(full, chips["dg"], c, 1)
        rcopy(3, low, low, devs["yn"]).wait_recv()
        rcopy(4, up, up, devs["xn"]).wait_recv()
        blk = block(full, chips["dg"], c)
        rcopy(7, blk, blk, devs["sib"]).start()

    def finish(ins, outs, rcopy, lcopy):
        c, devs, chips = place()
        full, sib = outs[0], devs["sib"]
        for k, chip in ((0, "me"), (5, "xn"), (6, "yn"), (7, "dg")):
            theirs = block(full, chips[chip], 1 - c)
            rcopy(k, theirs, theirs, sib).wait_recv()
            mine = block(full, chips[chip], c)
            if k:
                rcopy(k, mine, mine, sib).wait_send()
        src, dst = ins[0], block(full, chips["me"], c)
        lcopy(0, src, dst).wait()
        for k, to in enumerate(("sib", "xn", "yn")):
            rcopy(k, src, dst, devs[to]).wait_send()
        low, up = halfblock(full, chips["xn"], c, 0), halfblock(full, chips["yn"], c, 1)
        rcopy(3, low, low, devs["yn"]).wait_send()
        rcopy(4, up, up, devs["xn"]).wait_send()

    return _Hook([shard], [jax.ShapeDtypeStruct((NDEV * rows, D), shard.dtype)], 8, 1, start, finish, relay)


SIBLING, CHIPS, NEIGHBOURS, EVERYONE = "sibling", "chips", "sibling and the two neighbour chips", "everyone"
REACH_ID = {CHIPS: 6, EVERYONE: 7}


def _sequence(name, collective_id, reach, hooks):
    ins = [[jax.new_ref(a, memory_space=pltpu.MemorySpace.HBM) for a in h.operands] for h in hooks]
    outs = [[jax.empty_ref(s, memory_space=pltpu.MemorySpace.HBM) for s in h.out_shapes] for h in hooks]
    sems = tuple(pltpu.SemaphoreType.DMA((n,)) for h in hooks for n in (h.n_remote, h.n_remote, max(h.n_local, 1)))

    @pl.kernel(mesh=plsc.ScalarSubcoreMesh(axis_name="sequencer", num_cores=1), name=name, scratch_types=sems,
               compiler_params=pltpu.CompilerParams(collective_id=collective_id))
    def launch(*sem_refs):
        x, y, c = _my_place()
        chips = [dev for dev, _ in _ici_peers(x, y, c)]
        others = [(1 - x if dx else x, 1 - y if dy else y, 1 - c if dc else c)
                  for dx in range(2) for dy in range(2) for dc in range(2) if dx + dy + dc]
        devs = {SIBLING: [(x, y, 1 - c)], CHIPS: chips, NEIGHBOURS: [(x, y, 1 - c), (1 - x, y, c), (x, 1 - y, c)],
                EVERYONE: others}[reach]
        barrier = pltpu.get_barrier_semaphore()
        for dev in devs:
            pl.semaphore_signal(barrier, inc=1, device_id=dev, device_id_type=MESH_ID)
        pl.semaphore_wait(barrier, len(devs))
        for phase in ("start", "relay", "finish"):
            for i, h in enumerate(hooks):
                ssem, rsem, lsem = sem_refs[3 * i:3 * i + 3]

                def rcopy(k, src, dst, dev, ssem=ssem, rsem=rsem):
                    return pltpu.make_async_remote_copy(src_ref=src, dst_ref=dst, send_sem=ssem.at[k], recv_sem=rsem.at[k],
                                                        device_id=dev, device_id_type=MESH_ID)

                def lcopy(k, src, dst, lsem=lsem):
                    return pltpu.make_async_copy(src, dst, lsem.at[k])

                getattr(h, phase)(ins[i], outs[i], rcopy, lcopy)

    launch()
    return [o[...] for os in outs for o in os]


def _rs_d2d_hook(gfull):
    rows = gfull.shape[0] // NDEV

    def pairs(g, land):
        x, y, c = _my_place()
        return (x, y, 1 - c), [(k, _rows(g, (2 * k + 1 - c) * rows, rows), land.at[k]) for k in range(NCHIP)]

    def start(ins, outs, rcopy, lcopy):
        sib, cps = pairs(ins[0], outs[0])
        for i, src, dst in cps:
            rcopy(i, src, dst, sib).start()

    def finish(ins, outs, rcopy, lcopy):
        sib, cps = pairs(ins[0], outs[0])
        for i, src, dst in cps:
            rcopy(i, dst, dst, sib).wait_recv()
        for i, src, dst in cps:
            rcopy(i, src, dst, sib).wait_send()

    return _Hook([gfull], [jax.ShapeDtypeStruct((NCHIP, rows, D), gfull.dtype)], NCHIP, 0, start, finish)


def _rs_ici_hook(part):
    def start(ins, outs, rcopy, lcopy):
        x, y, c = _my_place()
        mychip = 2 * x + y
        lcopy(0, ins[0].at[mychip], outs[0].at[mychip]).start()
        for j, (dev, chip) in enumerate(_ici_peers(x, y, c)):
            rcopy(j, ins[0].at[chip], outs[0].at[mychip], dev).start()

    def finish(ins, outs, rcopy, lcopy):
        x, y, c = _my_place()
        mychip = 2 * x + y
        peers = _ici_peers(x, y, c)
        for j, (dev, chip) in enumerate(peers):
            rcopy(j, outs[0].at[chip], outs[0].at[chip], dev).wait_recv()
        for j, (dev, chip) in enumerate(peers):
            rcopy(j, ins[0].at[chip], outs[0].at[mychip], dev).wait_send()
        lcopy(0, ins[0].at[mychip], outs[0].at[mychip]).wait()

    return _Hook([part], [jax.ShapeDtypeStruct(part.shape, part.dtype)], 3, 1, start, finish)


def _small_hook(arrays):
    n = len(arrays)

    def peers():
        x, y, c = _my_place()
        out = []
        for dx in range(2):
            for dy in range(2):
                for dc in range(2):
                    if dx + dy + dc:
                        px, py, pc = (1 - x if dx else x), (1 - y if dy else y), (1 - c if dc else c)
                        out.append(((px, py, pc), 4 * px + 2 * py + pc))
        return 4 * x + 2 * y + c, out

    def start(ins, outs, rcopy, lcopy):
        me, ps = peers()
        for t in range(n):
            lcopy(t, ins[t], outs[t].at[me]).start()
            for i, (dev, _) in enumerate(ps):
                rcopy(n * i + t, ins[t], outs[t].at[me], dev).start()

    def finish(ins, outs, rcopy, lcopy):
        me, ps = peers()
        for t in range(n):
            for i, (dev, peer) in enumerate(ps):
                rcopy(n * i + t, outs[t].at[peer], outs[t].at[peer], dev).wait_recv()
            for i, (dev, _) in enumerate(ps):
                rcopy(n * i + t, ins[t], outs[t].at[me], dev).wait_send()
            lcopy(t, ins[t], outs[t].at[me]).wait()

    return _Hook(arrays, [jax.ShapeDtypeStruct((NDEV,) + a.shape, a.dtype) for a in arrays], 7 * n, n, start, finish)


def _wblock(w):
    return _resident(w.shape, lambda *_: (0, 0))


def _ffn_fwd(name, x, nrm, wg, wu, wd, hooks=(), tm=512):
    S = x.shape[0]

    def body(x_ref, n_ref, wg_ref, wu_ref, wd_ref, y_ref, g_ref, u_ref, a_ref, h_ref, acc_ref):
        xv = x_ref[...]
        _, _, h = _rms_fwd(xv, n_ref[...])
        h = h.astype(CDT)
        h_ref[...] = h
        for ci in range(F // FT):
            sl = slice(ci * FT, (ci + 1) * FT)
            g = _mm_nt(h, wg_ref[sl, :])
            u = _mm_nt(h, wu_ref[sl, :])
            g_ref[:, sl] = g.astype(CDT)
            u_ref[:, sl] = u.astype(CDT)
            a = (g * jax.nn.sigmoid(g) * u).astype(CDT)
            a_ref[:, sl] = a
            o = _mm(a, wd_ref[sl, :])
            if ci == 0:
                acc_ref[...] = o
            else:
                acc_ref[...] += o
        y_ref[...] = xv + 0.5 * acc_ref[...]

    tok = pl.BlockSpec((tm, D), lambda i: (i, 0))
    hid = pl.BlockSpec((tm, F), lambda i: (i, 0))
    hidden = jax.ShapeDtypeStruct((S, F), CDT)
    return _call(
        body, hooks, [x, nrm, wg, wu, wd], name=name, grid=(S // tm,),
        out_shape=[jax.ShapeDtypeStruct((S, D), f32), hidden, hidden, hidden, jax.ShapeDtypeStruct((S, D), CDT)],
        in_specs=[tok, _resident((1, D), lambda i: (0, 0)), _wblock(wg), _wblock(wu), _wblock(wd)],
        out_specs=[tok, hid, hid, hid, tok],
        scratch_shapes=[pltpu.VMEM((tm, D), f32)],
        compiler_params=_params("arbitrary"),
    )


def _proj_fwd(x1, nrm, wfull, b3, cos, sin, hooks=(), tm=512):
    S = x1.shape[0]

    def body(x_ref, n_ref, w_ref, b_ref, cos_ref, sin_ref, p_ref, h_ref):
        _, _, h = _rms_fwd(x_ref[...], n_ref[...])
        h = h.astype(CDT)
        h_ref[...] = h
        for p in range(8):
            seg = SEG_OF_SLOT[p]
            z = _mm(h, w_ref[seg * D:(seg + 1) * D, :]) + b_ref[seg]
            if p in (SLOT_Q, SLOT_K):
                co, si = cos_ref[...], sin_ref[...]
                for hh in range(H):
                    cs = slice(hh * DK, (hh + 1) * DK)
                    zr = _rotate(z[:, cs], co, si)
                    p_ref[p, :, cs] = (zr * K_SCALE if p == SLOT_K else zr).astype(CDT)
            else:
                p_ref[p] = z.astype(CDT)

    tab = pl.BlockSpec((tm, DK // 2), lambda i: (i, 0))
    return _call(
        body, hooks, [x1, nrm, wfull, b3, cos, sin], name="proj_fwd", grid=(S // tm,),
        out_shape=[jax.ShapeDtypeStruct((8, S, D), CDT), jax.ShapeDtypeStruct((S, D), CDT)],
        in_specs=[pl.BlockSpec((tm, D), lambda i: (i, 0)), _resident((1, D), lambda i: (0, 0)),
                  _resident((8 * D, D), lambda i: (0, 0)), _resident((8, 1, D), lambda i: (0, 0, 0)), tab, tab],
        out_specs=[pl.BlockSpec((8, tm, D), lambda i: (0, i, 0)), pl.BlockSpec((tm, D), lambda i: (i, 0))],
        compiler_params=_params("arbitrary"),
    )


def _sgu_norm(va, gn, bn):
    mu = jnp.mean(va, axis=-1, keepdims=True)
    xc = va - mu
    rstd = lax.rsqrt(jnp.mean(xc * xc, axis=-1, keepdims=True) + EPS)
    vhat = xc * rstd
    return rstd, vhat, vhat * gn + bn


def _sgu_fwd(proj, gn, bn, ws, bsc, tm=512):
    S = proj.shape[1]
    GW = D // G

    def body(p_ref, gn_ref, bn_ref, ws_ref, bs_ref, a_ref):
        ua = _gelu(p_ref[0].astype(f32))
        va = _gelu(p_ref[1].astype(f32))
        _, _, vn = _sgu_norm(va, gn_ref[...], bn_ref[...])
        vn = vn.astype(CDT)
        for ch in range(tm // C):
            rs = slice(ch * C, (ch + 1) * C)
            for gi in range(G):
                cs = slice(gi * GW, (gi + 1) * GW)
                s = _mm(ws_ref[gi], vn[rs, cs]) + bs_ref[gi]
                a_ref[rs, cs] = (ua[rs, cs] * s).astype(CDT)

    return pl.pallas_call(
        body, name="sgu_fwd", grid=(S // tm,),
        out_shape=jax.ShapeDtypeStruct((S, D), CDT),
        in_specs=[pl.BlockSpec((2, tm, D), lambda i: (0, i, 0)), _resident((1, D), lambda i: (0, 0)),
                  _resident((1, D), lambda i: (0, 0)), _resident((G, C, C), lambda i: (0, 0, 0)),
                  _resident((G, C, 1), lambda i: (0, 0, 0))],
        out_specs=pl.BlockSpec((tm, D), lambda i: (i, 0)),
        compiler_params=_params("arbitrary"),
    )(proj, gn, bn, ws, bsc)


def _decay_tables(dl_ref):
    lg = jax.nn.log_sigmoid(dl_ref[0:2, :])
    lgf, lgb = lg[0:1, :], lg[1:2, :]
    assert RC <= DK
    ri = lax.broadcasted_iota(jnp.int32, (RC, RC), 0)
    ci = lax.broadcasted_iota(jnp.int32, (RC, RC), 1)
    d = (ri - ci).astype(f32)
    lower = d >= 0
    dmat = jnp.where(lower, jnp.exp(d * lgf[:, :RC]), jnp.exp(-d * lgb[:, :RC]))
    dmat_t = jnp.where(d <= 0, jnp.exp(-d * lgf[:, :RC]), jnp.exp(d * lgb[:, :RC]))
    pos = lax.broadcasted_iota(jnp.int32, (RC, DK), 0).astype(f32)
    t = dict(
        lgf=lgf, lgb=lgb, d=d, lower=lower, dmat=dmat, dmat_t=dmat_t, pos=pos,
        fq=jnp.exp((pos + 1.0) * lgf), fk=jnp.exp((RC - 1.0 - pos) * lgf),
        bq=jnp.exp((RC - pos) * lgb), bk=jnp.exp(pos * lgb),
        lamf=jnp.exp(float(RC) * lgf), lamb=jnp.exp(float(RC) * lgb),
    )
    return t


def _rotate(t, co, si):
    t1, t2 = t[:, :DK // 2], t[:, DK // 2:]
    return jnp.concatenate([t1 * co - t2 * si, t2 * co + t1 * si], axis=-1)


def _unrotate(t, co, si):
    t1, t2 = t[:, :DK // 2], t[:, DK // 2:]
    return jnp.concatenate([t1 * co + t2 * si, t2 * co - t1 * si], axis=-1)


K_SCALE = DK ** -0.5
ROW_TILE = 256


def _ret_fwd(proj, dl, hooks=()):
    S = proj.shape[1]
    NC = S // RC

    def body(q_ref, k_ref, v_ref, g_ref, dl_ref, R_ref, r_ref, sfs_ref, sbs_ref, rb_ref, sf_ref, sb_ref):
        t = _decay_tables(dl_ref)

        def chunk(n):
            rows = pl.ds(pl.multiple_of(n * RC, RC), RC)
            return rows, q_ref[rows, :], k_ref[rows, :], v_ref[rows, :]

        sf_ref[...] = jnp.zeros_like(sf_ref)
        sb_ref[...] = jnp.zeros_like(sb_ref)

        def step(i, carry):
            rows, qn, kn, vn = chunk(i)
            sc = _mm_nt(qn, kn) * t["dmat"]
            out = _mm(sc.astype(CDT), vn)
            sf = sf_ref[...]
            sfb = sf.astype(CDT)
            sfs_ref[i] = sfb
            R_ref[rows, :] = out + _mm((qn.astype(f32) * t["fq"]).astype(CDT), sfb)
            sf_ref[...] = sf * t["lamf"] + _mm_tn((kn.astype(f32) * t["fk"]).astype(CDT), vn)
            m = NC - 1 - i
            rows, qn, kn, vn = chunk(m)
            sb = sb_ref[...]
            sbb = sb.astype(CDT)
            sbs_ref[m] = sbb
            rb_ref[rows, :] = _mm((qn.astype(f32) * t["bq"]).astype(CDT), sbb)
            sb_ref[...] = sb * t["lamb"] + _mm_tn((kn.astype(f32) * t["bk"]).astype(CDT), vn)
            return carry

        lax.fori_loop(0, NC, step, 0, unroll=4)

        def finish(i, carry):
            rs = pl.ds(pl.multiple_of(i * ROW_TILE, ROW_TILE), ROW_TILE)
            R = R_ref[rs, :] + rb_ref[rs, :]
            R_ref[rs, :] = R
            rn = R * lax.rsqrt(jnp.mean(R * R, axis=-1, keepdims=True) + EPS)
            g = g_ref[rs, :].astype(f32)
            r_ref[rs, :] = (rn * g * jax.nn.sigmoid(g)).astype(CDT)
            return carry

        lax.fori_loop(0, S // ROW_TILE, finish, 0)

    def seg(slot):
        return pl.BlockSpec((None, S, DK), lambda h: (slot, 0, h))

    states = jax.ShapeDtypeStruct((H, NC, DK, DK), CDT)
    state_blk = pl.BlockSpec((None, NC, DK, DK), lambda h: (h, 0, 0, 0))
    return _call(
        body, hooks, [proj, proj, proj, proj, dl], name="ret_fwd", grid=(H,),
        out_shape=[jax.ShapeDtypeStruct((S, H * DK), f32), jax.ShapeDtypeStruct((S, H * DK), CDT), states, states],
        in_specs=[seg(SLOT_Q), seg(SLOT_K), seg(SLOT_VR), seg(SLOT_GR), pl.BlockSpec((None, 8, DK), lambda h: (h, 0, 0))],
        out_specs=[pl.BlockSpec((S, DK), lambda h: (0, h)), pl.BlockSpec((S, DK), lambda h: (0, h)), state_blk, state_blk],
        scratch_shapes=[pltpu.VMEM((S, DK), f32), pltpu.VMEM((DK, DK), f32), pltpu.VMEM((DK, DK), f32)],
        compiler_params=_params("arbitrary"),
    )


def _merge_fwd(a, r, proj, x1, wa, wb, wo, hooks=(), tm=512):
    S = x1.shape[0]

    def body(a_ref, r_ref, gt_ref, x_ref, wa_ref, wb_ref, wo_ref, x2_ref, ya_ref, yb_ref):
        ya = _mm(a_ref[...], wa_ref[...])
        yb = _mm(r_ref[...], wb_ref[...])
        ya_ref[...] = ya.astype(CDT)
        yb_ref[...] = yb.astype(CDT)
        mix = jax.nn.sigmoid(gt_ref[0].astype(f32)) * ya + jax.nn.sigmoid(gt_ref[1].astype(f32)) * yb
        x2_ref[...] = x_ref[...] + _mm(mix.astype(CDT), wo_ref[...])

    tok = pl.BlockSpec((tm, D), lambda i: (i, 0))
    return _call(
        body, hooks, [a, r, proj, x1, wa, wb, wo], name="merge_fwd", grid=(S // tm,),
        out_shape=[jax.ShapeDtypeStruct((S, D), f32), jax.ShapeDtypeStruct((S, D), CDT), jax.ShapeDtypeStruct((S, D), CDT)],
        in_specs=[tok, tok, pl.BlockSpec((2, tm, D), lambda i: (SLOT_GA // 2, i, 0)), tok,
                  _wblock(wa), _wblock(wb), _wblock(wo)],
        out_specs=[tok, tok, tok],
        compiler_params=_params("arbitrary"),
    )


def _loss_head(x3, fn, target, tm=512):
    S = x3.shape[0]

    def body(x_ref, n_ref, t_ref, dx_ref, dxh_ref, dn_ref, l_ref):
        n = n_ref[...]
        r, xh, y = _rms_fwd(x_ref[...], n)
        e = y - t_ref[...]
        dy = e * (1.0 / D)
        dx, dn = _rms_bwd(dy, r, xh, n)
        dx_ref[...] = dx
        dxh_ref[...] = (0.5 * dx).astype(CDT)
        part = 0.5 * jnp.sum(jnp.sum(e * e, axis=-1, keepdims=True), axis=0, keepdims=True) * (1.0 / D)

        @pl.when(pl.program_id(0) == 0)
        def _():
            dn_ref[...] = jnp.zeros_like(dn_ref)
            l_ref[...] = jnp.zeros_like(l_ref)

        dn_ref[...] += dn
        l_ref[...] += jnp.broadcast_to(part, l_ref.shape)

    tok = pl.BlockSpec((tm, D), lambda i: (i, 0))
    return pl.pallas_call(
        body, name="loss_head", grid=(S // tm,),
        out_shape=[jax.ShapeDtypeStruct((S, D), f32), jax.ShapeDtypeStruct((S, D), CDT), jax.ShapeDtypeStruct((1, D), f32),
                   jax.ShapeDtypeStruct((8, 128), f32)],
        in_specs=[tok, _resident((1, D), lambda i: (0, 0)), tok],
        out_specs=[tok, tok, pl.BlockSpec((1, D), lambda i: (0, 0)), pl.BlockSpec((8, 128), lambda i: (0, 0))],
        compiler_params=_params("arbitrary"),
    )(x3, fn, target)


def _ffn_bwd_hidden(name, dyh, g, u, wd, hooks=(), tm=512):
    S = dyh.shape[0]

    def body(dyh_ref, g_ref, u_ref, wd_ref, dg_ref, du_ref):
        dyh = dyh_ref[...]
        for ci in range(F // FT):
            sl = slice(ci * FT, (ci + 1) * FT)
            da = _mm_nt(dyh, wd_ref[sl, :])
            gv = g_ref[:, sl].astype(f32)
            uv = u_ref[:, sl].astype(f32)
            s = jax.nn.sigmoid(gv)
            silu = gv * s
            du_ref[:, sl] = (da * silu).astype(CDT)
            dg_ref[:, sl] = (da * uv * (s + silu - silu * s)).astype(CDT)

    hid = pl.BlockSpec((tm, F), lambda i: (i, 0))
    hidden = jax.ShapeDtypeStruct((S, F), CDT)
    return _call(
        body, hooks, [dyh, g, u, wd], name=name, grid=(S // tm,), out_shape=[hidden, hidden],
        in_specs=[pl.BlockSpec((tm, D), lambda i: (i, 0)), hid, hid, _wblock(wd)], out_specs=[hid, hid],
        compiler_params=_params("arbitrary"),
    )


def _ffn_bwd_in(name, dy, x, dg, du, nrm, wg, wu, hooks=(), tm=512):
    S = x.shape[0]

    def body(dy_ref, x_ref, dg_ref, du_ref, n_ref, wg_ref, wu_ref, dx_ref, dn_ref, acc_ref):
        n = n_ref[...]
        r, xh, _ = _rms_fwd(x_ref[...], n)
        for ci in range(F // FT):
            sl = slice(ci * FT, (ci + 1) * FT)
            dh = _mm(dg_ref[:, sl], wg_ref[sl, :]) + _mm(du_ref[:, sl], wu_ref[sl, :])
            if ci == 0:
                acc_ref[...] = dh
            else:
                acc_ref[...] += dh
        dx, dn = _rms_bwd(acc_ref[...], r, xh, n)
        dx_ref[...] = dy_ref[...] + dx

        @pl.when(pl.program_id(0) == 0)
        def _():
            dn_ref[...] = jnp.zeros_like(dn_ref)

        dn_ref[...] += dn

    tok = pl.BlockSpec((tm, D), lambda i: (i, 0))
    hid = pl.BlockSpec((tm, F), lambda i: (i, 0))
    return _call(
        body, hooks, [dy, x, dg, du, nrm, wg, wu], name=name, grid=(S // tm,),
        out_shape=[jax.ShapeDtypeStruct((S, D), f32), jax.ShapeDtypeStruct((1, D), f32)],
        in_specs=[tok, tok, hid, hid, _resident((1, D), lambda i: (0, 0)), _wblock(wg), _wblock(wu)],
        out_specs=[tok, pl.BlockSpec((1, D), lambda i: (0, 0))],
        scratch_shapes=[pltpu.VMEM((tm, D), f32)],
        compiler_params=_params("arbitrary"),
    )


TN_ROWS = 512


def _tn(name, xs, ys, block_of, hooks=()):
    S, M = xs.shape
    B = ys.shape[0]
    tr = TN_ROWS if M % TN_ROWS == 0 else M // 2
    assert M % tr == 0 and tr % 128 == 0
    nt = M // tr

    def body(x_ref, y_ref, o_ref):
        o_ref[...] = _mm_tn(x_ref[...], y_ref[...]).astype(CDT)

    return _call(
        body, hooks, [xs, ys], name=name, grid=(B, nt),
        out_shape=[jax.ShapeDtypeStruct((B * M, D), CDT)],
        in_specs=[pl.BlockSpec((S, tr), lambda b, i: (0, i)), pl.BlockSpec((None, S, D), lambda b, i: (b, 0, 0))],
        out_specs=[pl.BlockSpec((tr, D), lambda b, i: (block_of(b) * nt + i, 0))],
        compiler_params=_params("arbitrary", "arbitrary"),
    )


def _wgrad(name, xs, y, hooks=()):
    return _tn(name, xs, y[None], lambda b: 0, hooks)


def _merge_bwd_act(dx2, ya, yb, proj, wa, wb, wo, hooks=(), tm=512):
    S = dx2.shape[0]

    def body(dx_ref, ya_ref, yb_ref, gt_ref, wa_ref, wb_ref, wo_ref,
             dp_ref, da_ref, dr_ref, mix_ref, dxb_ref, dya_ref, dyb_ref):
        dxb = dx_ref[...].astype(CDT)
        dxb_ref[...] = dxb
        dmix = _mm_nt(dxb, wo_ref[...])
        ya = ya_ref[...].astype(f32)
        yb = yb_ref[...].astype(f32)
        sa = jax.nn.sigmoid(gt_ref[0].astype(f32))
        sb = jax.nn.sigmoid(gt_ref[1].astype(f32))
        mix_ref[...] = (sa * ya + sb * yb).astype(CDT)
        dya = (dmix * sa).astype(CDT)
        dyb = (dmix * sb).astype(CDT)
        dya_ref[...] = dya
        dyb_ref[...] = dyb
        dp_ref[0] = (dmix * ya * sa * (1.0 - sa)).astype(CDT)
        dp_ref[1] = (dmix * yb * sb * (1.0 - sb)).astype(CDT)
        da_ref[...] = _mm_nt(dya, wa_ref[...]).astype(CDT)
        dr_ref[...] = _mm_nt(dyb, wb_ref[...]).astype(CDT)

    tok = pl.BlockSpec((tm, D), lambda i: (i, 0))
    gates = pl.BlockSpec((2, tm, D), lambda i: (SLOT_GA // 2, i, 0))
    act = jax.ShapeDtypeStruct((S, D), CDT)
    return _call(
        body, hooks, [dx2, ya, yb, proj, wa, wb, wo], name="merge_bwd_act", grid=(S // tm,),
        out_shape=[jax.ShapeDtypeStruct((8, S, D), CDT), act, act, act, act, act, act],
        in_specs=[tok, tok, tok, gates, _wblock(wa), _wblock(wb), _wblock(wo)],
        out_specs=[gates, tok, tok, tok, tok, tok, tok],
        compiler_params=_params("arbitrary"),
    )


def _sgu_bwd(da, proj, dproj, gn, bn, ws, wst, bsc, hooks=(), tm=512):
    S = proj.shape[1]
    GW = D // G

    def body(da_ref, p_ref, dpin_ref, gn_ref, bn_ref, ws_ref, wst_ref, bs_ref,
             dp_ref, dws_ref, dbs_ref, dgn_ref, dbn_ref, ds_ref, dvn_ref):
        @pl.when(pl.program_id(0) == 0)
        def _():
            dws_ref[...] = jnp.zeros_like(dws_ref)
            dbs_ref[...] = jnp.zeros_like(dbs_ref)
            dgn_ref[...] = jnp.zeros_like(dgn_ref)
            dbn_ref[...] = jnp.zeros_like(dbn_ref)

        pu = p_ref[0].astype(f32)
        pv = p_ref[1].astype(f32)
        ua = _gelu(pu)
        va = _gelu(pv)
        gn = gn_ref[...]
        rstd, vhat, vn = _sgu_norm(va, gn, bn_ref[...])
        vnb = vn.astype(CDT)
        dav = da_ref[...].astype(f32)
        dsb = (dav * ua).astype(CDT)
        ones = jnp.ones((8, GW), CDT)
        for ch in range(tm // C):
            rs = slice(ch * C, (ch + 1) * C)
            for gi in range(G):
                cs = slice(gi * GW, (gi + 1) * GW)
                s = _mm(ws_ref[gi], vnb[rs, cs]) + bs_ref[gi]
                ds_ref[rs, cs] = s
                dsg = dsb[rs, cs]
                dws_ref[gi] += _mm_nt(dsg, vnb[rs, cs])
                dbs_ref[gi] += _mm_nt(ones, dsg)
                dvn_ref[rs, cs] = _mm(wst_ref[gi], dsg)
        dp_ref[0] = (dav * ds_ref[...] * _gelu_grad(pu)).astype(CDT)
        dvn = dvn_ref[...]
        dgn_ref[...] += jnp.sum(dvn * vhat, axis=0, keepdims=True)
        dbn_ref[...] += jnp.sum(dvn, axis=0, keepdims=True)
        dvh = dvn * gn
        dva = rstd * (dvh - jnp.mean(dvh, axis=-1, keepdims=True) - vhat * jnp.mean(dvh * vhat, axis=-1, keepdims=True))
        dp_ref[1] = (dva * _gelu_grad(pv)).astype(CDT)

    uv = pl.BlockSpec((2, tm, D), lambda i: (0, i, 0))
    row = _resident((1, D), lambda i: (0, 0))
    return _call(
        body, hooks, [da, proj, dproj, gn, bn, ws, wst, bsc], name="sgu_bwd", grid=(S // tm,),
        out_shape=[jax.ShapeDtypeStruct(dproj.shape, CDT), jax.ShapeDtypeStruct((G, C, C), f32),
                   jax.ShapeDtypeStruct((G, 8, C), f32), jax.ShapeDtypeStruct((1, D), f32), jax.ShapeDtypeStruct((1, D), f32)],
        in_specs=[pl.BlockSpec((tm, D), lambda i: (i, 0)), uv, _HBM, row, row,
                  _resident((G, C, C), lambda i: (0, 0, 0)), _resident((G, C, C), lambda i: (0, 0, 0)),
                  _resident((G, C, 1), lambda i: (0, 0, 0))],
        out_specs=[uv, pl.BlockSpec((G, C, C), lambda i: (0, 0, 0)), pl.BlockSpec((G, 8, C), lambda i: (0, 0, 0)),
                   pl.BlockSpec((1, D), lambda i: (0, 0)), pl.BlockSpec((1, D), lambda i: (0, 0))],
        scratch_shapes=[pltpu.VMEM((tm, D), f32), pltpu.VMEM((tm, D), f32)],
        input_output_aliases={2: 0},
        compiler_params=_params("arbitrary"),
    )


def _ret_bwd(dr, R, sfs, sbs, proj, dproj, cos, sin, dl, hooks=()):
    S = proj.shape[1]
    NC = S // RC
    assert NC % 2 == 0

    def body(dr_ref, R_ref, sf_ref, sb_ref, q_ref, k_ref, v_ref, g_ref, dpin_ref, cos_ref, sin_ref, dl_ref,
             dp_ref, dd_ref, dR_ref, gb_ref, gf_ref, acc_ref):
        t = _decay_tables(dl_ref)

        def gate_norm_bwd(i, carry):
            rs = pl.ds(pl.multiple_of(i * ROW_TILE, ROW_TILE), ROW_TILE)
            Rv = R_ref[rs, :]
            rstd = lax.rsqrt(jnp.mean(Rv * Rv, axis=-1, keepdims=True) + EPS)
            rn = Rv * rstd
            gv = g_ref[rs, :].astype(f32)
            s = jax.nn.sigmoid(gv)
            drv = dr_ref[rs, :].astype(f32)
            dp_ref[3, rs, :] = (drv * rn * (s * (1.0 + gv * (1.0 - s)))).astype(CDT)
            drn = drv * gv * s
            dR_ref[rs, :] = (rstd * (drn - rn * jnp.mean(drn * rn, axis=-1, keepdims=True))).astype(CDT)
            return carry

        lax.fori_loop(0, S // ROW_TILE, gate_norm_bwd, 0)

        def chunk(n):
            rows = pl.ds(pl.multiple_of(n * RC, RC), RC)
            return rows, q_ref[rows, :], k_ref[rows, :], v_ref[rows, :], dR_ref[rows, :]

        def emit_kv(rows, dk, dv, final):
            if not final:
                dp_ref[1, rows, :] = dk.astype(CDT)
                dp_ref[2, rows, :] = dv.astype(CDT)
            else:
                co, si = cos_ref[rows, :], sin_ref[rows, :]
                dk = dp_ref[1, rows, :].astype(f32) + dk
                dp_ref[1, rows, :] = (_unrotate(dk, co, si) * K_SCALE).astype(CDT)
                dp_ref[2, rows, :] = (dp_ref[2, rows, :].astype(f32) + dv).astype(CDT)

        gb_ref[...] = jnp.zeros_like(gb_ref)
        gf_ref[...] = jnp.zeros_like(gf_ref)
        acc_ref[...] = jnp.zeros_like(acc_ref)
        span = t["dmat"] * jnp.abs(t["d"])
        span_f, span_b = jnp.where(t["lower"], span, 0.0), jnp.where(t["lower"], 0.0, span)

        def ascend(n, final):
            rows, qn, kn, vn, dRn = chunk(n)
            qf, kf = qn.astype(f32), kn.astype(f32)
            sc = _mm_nt(qn, kn)
            dA = _mm_nt(dRn, vn)
            prod = sc * dA
            lgf_part = jnp.sum(prod * span_f, axis=0, keepdims=True)
            lgb_part = jnp.sum(prod * span_b, axis=0, keepdims=True)
            dsc = (dA * t["dmat"]).astype(CDT)
            dq = _mm(dsc, kn)
            scT = (_mm_nt(kn, qn) * t["dmat_t"]).astype(CDT)
            dscT = (_mm_nt(vn, dRn) * t["dmat_t"]).astype(CDT)
            dk = _mm(dscT, qn)
            dv = _mm(scT, dRn)
            sfb = sf_ref[n]
            sbb = sb_ref[n]
            qdf = qf * t["fq"]
            dqdf = _mm_nt(dRn, sfb)
            dq += dqdf * t["fq"]
            lgf_row = jnp.sum(qdf * dqdf * (t["pos"] + 1.0), axis=0, keepdims=True)
            qdb = qf * t["bq"]
            dqdb = _mm_nt(dRn, sbb)
            dq += dqdb * t["bq"]
            lgb_row = jnp.sum(qdb * dqdb * (RC - t["pos"]), axis=0, keepdims=True)
            gb = gb_ref[...]
            gbb = gb.astype(CDT)
            kdb = kf * t["bk"]
            dkdb = _mm_nt(vn, gbb)
            dk += dkdb * t["bk"]
            dv += _mm(kdb.astype(CDT), gbb)
            lgb_row += jnp.sum(kdb * dkdb * t["pos"], axis=0, keepdims=True)
            lgb_row += float(RC) * t["lamb"] * jnp.sum(gb * sbb.astype(f32), axis=0, keepdims=True)
            co, si = cos_ref[rows, :], sin_ref[rows, :]
            dp_ref[0, rows, :] = _unrotate(dq, co, si).astype(CDT)
            emit_kv(rows, dk, dv, final)
            acc_ref[0:1, :] += lgf_row + lgf_part
            acc_ref[1:2, :] += lgb_row + lgb_part
            gb_ref[...] = gb * t["lamb"] + _mm_tn(qdb.astype(CDT), dRn)

        def descend(n, final):
            rows, qn, kn, vn, dRn = chunk(n)
            gf = gf_ref[...]
            gfb = gf.astype(CDT)
            kdf = kn.astype(f32) * t["fk"]
            dkdf = _mm_nt(vn, gfb)
            lgf_row = jnp.sum(kdf * dkdf * (RC - 1.0 - t["pos"]), axis=0, keepdims=True)
            lgf_row += float(RC) * t["lamf"] * jnp.sum(gf * sf_ref[n].astype(f32), axis=0, keepdims=True)
            acc_ref[0:1, :] += lgf_row
            emit_kv(rows, dkdf * t["fk"], _mm(kdf.astype(CDT), gfb), final)
            gf_ref[...] = gf * t["lamf"] + _mm_tn((qn.astype(f32) * t["fq"]).astype(CDT), dRn)

        def sweep(final):
            def step(i, carry):
                ascend(i, final)
                descend(NC - 1 - i, final)
                return carry
            return step

        lax.fori_loop(0, NC // 2, sweep(False), 0, unroll=2)
        lax.fori_loop(NC // 2, NC, sweep(True), 0, unroll=2)
        dlg = jnp.sum(acc_ref[...], axis=1, keepdims=True)
        dlogit = dlg * jax.nn.sigmoid(-dl_ref[:, 0:1])
        lane = lax.broadcasted_iota(jnp.int32, (8, 128), 1)
        dd_ref[...] = jnp.where(lane == pl.program_id(0), jnp.broadcast_to(dlogit, (8, 128)), 0.0)

    def seg(slot):
        return pl.BlockSpec((None, S, DK), lambda h: (slot, 0, h))

    head = pl.BlockSpec((S, DK), lambda h: (0, h))
    states = pl.BlockSpec((None, NC, DK, DK), lambda h: (h, 0, 0, 0))
    return _call(
        body, hooks, [dr, R, sfs, sbs, proj, proj, proj, proj, dproj, cos, sin, dl], name="ret_bwd", grid=(H,),
        out_shape=[jax.ShapeDtypeStruct(dproj.shape, CDT), jax.ShapeDtypeStruct((H, 8, 128), f32)],
        in_specs=[head, head, states, states, seg(SLOT_Q), seg(SLOT_K), seg(SLOT_VR), seg(SLOT_GR), _HBM,
                  _resident((S, DK // 2), lambda h: (0, 0)), _resident((S, DK // 2), lambda h: (0, 0)),
                  pl.BlockSpec((None, 8, DK), lambda h: (h, 0, 0))],
        out_specs=[pl.BlockSpec((4, S, DK), lambda h: (1, 0, h), pipeline_mode=pl.Buffered(1)),
                   pl.BlockSpec((None, 8, 128), lambda h: (h, 0, 0))],
        scratch_shapes=[pltpu.VMEM((S, DK), CDT),
                        pltpu.VMEM((DK, DK), f32), pltpu.VMEM((DK, DK), f32), pltpu.VMEM((8, DK), f32)],
        input_output_aliases={8: 0},
        compiler_params=_params("arbitrary"),
    )


def _proj_bwd_act(dproj, dx2, x1, nrm, wfull, hooks=(), tm=512):
    S = x1.shape[0]

    def body(dp_ref, dx2_ref, x_ref, n_ref, w_ref, dx_ref, dxh_ref, dn_ref, db_ref, acc_ref):
        @pl.when(pl.program_id(0) == 0)
        def _():
            dn_ref[...] = jnp.zeros_like(dn_ref)
            db_ref[...] = jnp.zeros_like(db_ref)

        for p in range(8):
            seg = SEG_OF_SLOT[p]
            dp = dp_ref[p]
            db_ref[seg] += jnp.sum(dp.astype(f32), axis=0, keepdims=True)
            dh = _mm_nt(dp, w_ref[seg * D:(seg + 1) * D, :])
            if p == 0:
                acc_ref[...] = dh
            else:
                acc_ref[...] += dh
        n = n_ref[...]
        r, xh, _ = _rms_fwd(x_ref[...], n)
        dx, dn = _rms_bwd(acc_ref[...], r, xh, n)
        dx = dx2_ref[...] + dx
        dx_ref[...] = dx
        dxh_ref[...] = (0.5 * dx).astype(CDT)
        dn_ref[...] += dn

    tok = pl.BlockSpec((tm, D), lambda i: (i, 0))
    return _call(
        body, hooks, [dproj, dx2, x1, nrm, wfull], name="proj_bwd_act", grid=(S // tm,),
        out_shape=[jax.ShapeDtypeStruct((S, D), f32), jax.ShapeDtypeStruct((S, D), CDT), jax.ShapeDtypeStruct((1, D), f32),
                   jax.ShapeDtypeStruct((8, 1, D), f32)],
        in_specs=[pl.BlockSpec((8, tm, D), lambda i: (0, i, 0)), tok, tok, _resident((1, D), lambda i: (0, 0)),
                  _resident((8 * D, D), lambda i: (0, 0))],
        out_specs=[tok, tok, pl.BlockSpec((1, D), lambda i: (0, 0)), pl.BlockSpec((8, 1, D), lambda i: (0, 0, 0))],
        scratch_shapes=[pltpu.VMEM((tm, D), f32)],
        compiler_params=_params("arbitrary"),
    )


def _rs_sum(name, gfulls, lands, my_c):
    n = len(gfulls)
    rows = gfulls[0].shape[0] // NDEV
    assert all(g.shape[0] == NDEV * rows for g in gfulls)

    def body(c_ref, *refs):
        for g_ref, l_ref, o_ref in zip(refs[:n], refs[n:2 * n], refs[2 * n:]):
            o_ref[...] = (g_ref[...].astype(f32) + l_ref[...].astype(f32)).astype(CDT)

    slot = pl.BlockSpec((None, rows, D), lambda k, c: (k, 0, 0))
    return pl.pallas_call(
        body, name=name,
        grid_spec=pltpu.PrefetchScalarGridSpec(
            num_scalar_prefetch=1, grid=(NCHIP,),
            in_specs=[pl.BlockSpec((rows, D), lambda k, c: (2 * k + c[0], 0))] * n + [slot] * n,
            out_specs=[slot] * n),
        out_shape=[jax.ShapeDtypeStruct((NCHIP, rows, D), CDT)] * n,
        compiler_params=_params("arbitrary"),
    )(my_c, *gfulls, *lands)


def _adamw_math(g, w, m, v):
    m2 = ADAM_B1 * m + (1.0 - ADAM_B1) * g
    v2 = ADAM_B2 * v + (1.0 - ADAM_B2) * (g * g)
    delta = -ADAM_LR * ((m2 / BC1) / (jnp.sqrt(v2 / BC2) + ADAM_EPS) + ADAM_WD * w)
    return delta, m2, v2


def _adamw_big(name, landed, w, m, v, after):
    n = len(w)
    rows = w[0].shape[0]
    tr = 256 if rows % 256 == 0 else (rows // 2 if rows > 256 else rows)
    nt = rows // tr

    def body(*refs):
        ins, outs = refs[:4 * n], refs[4 * n + 1:]
        for j in range(n):
            @pl.when(pl.program_id(0) == j)
            def _(j=j):
                l_ref, w_ref, m_ref, v_ref = ins[j], ins[n + j], ins[2 * n + j], ins[3 * n + j]
                g = l_ref[0].astype(f32)
                for k in range(1, NCHIP):
                    g = g + l_ref[k].astype(f32)
                outs[4 * j][...] = g
                outs[4 * j + 1][...], outs[4 * j + 2][...], outs[4 * j + 3][...] = _adamw_math(g, w_ref[...], m_ref[...], v_ref[...])

    def tile(j):
        return lambda jj, i: jnp.clip((jj - j) * nt + i, 0, nt - 1)

    blk = [pl.BlockSpec((tr, D), lambda jj, i, t=tile(j): (t(jj, i), 0)) for j in range(n)]
    lnd = [pl.BlockSpec((NCHIP, tr, D), lambda jj, i, t=tile(j): (0, t(jj, i), 0)) for j in range(n)]
    o = jax.ShapeDtypeStruct((rows, D), f32)
    res = pl.pallas_call(
        body, name=name, grid=(n, nt), out_shape=[o] * (4 * n),
        in_specs=lnd + blk + blk + blk + [_HBM],
        out_specs=[blk[j] for j in range(n) for _ in range(4)],
        compiler_params=_params("arbitrary", "arbitrary"),
    )(*landed, *w, *m, *v, after)
    return [res[4 * j:4 * j + 4] for j in range(n)]


ROW_FFN1_NORM, ROW_MIX_NORM, ROW_SGU_G, ROW_SGU_B, ROW_FFN2_NORM, ROW_FINAL_NORM, ROW_B_IN = 0, 1, 2, 3, 4, 5, 8
ROW_WS, ROW_BS, ROW_DECAY = 0, G * C, G * C + G * 8


def _adamw_small(ga, gb, gn1, gl, params):
    def body(ga_ref, gb_ref, gn1_ref, gl_ref, *refs):
        ins, outs = refs[:30], refs[30:]

        def total(ref, r0, n):
            g = ref[0, r0:r0 + n, :]
            for j in range(1, NDEV):
                g = g + ref[j, r0:r0 + n, :]
            return g

        def apply(i, g, rows=slice(None)):
            w, m, v = ins[3 * i][rows, :], ins[3 * i + 1][rows, :], ins[3 * i + 2][rows, :]
            outs[4 * i][rows, :] = g
            outs[4 * i + 1][rows, :], outs[4 * i + 2][rows, :], outs[4 * i + 3][rows, :] = _adamw_math(g, w, m, v)

        outs[40][...] = total(gl_ref, 0, 8)
        apply(0, total(gn1_ref, 0, 1))
        for i, r in enumerate((ROW_FFN1_NORM, ROW_MIX_NORM, ROW_SGU_G, ROW_SGU_B, ROW_FFN2_NORM, ROW_FINAL_NORM)):
            if i:
                apply(i, total(ga_ref, r, 1))
        apply(6, total(ga_ref, ROW_B_IN, 8))
        apply(7, total(gb_ref, ROW_WS, G * C))
        for gi in range(G):
            apply(8, total(gb_ref, ROW_BS + 8 * gi, 1), slice(gi, gi + 1))
        dec = total(gb_ref, ROW_DECAY, 8)
        for hh in range(1, H):
            dec = dec + total(gb_ref, ROW_DECAY + 8 * hh, 8)
        apply(9, dec)

    flat = [a for p in params for a in p]
    out_shape = [jax.ShapeDtypeStruct(p[0].shape, f32) for p in params for _ in range(4)]
    out_shape.append(jax.ShapeDtypeStruct((8, 128), f32))
    vm = pl.BlockSpec(memory_space=pltpu.VMEM)
    return pl.pallas_call(
        body, name="adamw_small", out_shape=out_shape,
        in_specs=[vm] * (4 + len(flat)), out_specs=[vm] * len(out_shape),
        compiler_params=pltpu.CompilerParams(vmem_limit_bytes=VMEM_LIMIT),
    )(ga, gb, gn1, gl, *flat)


def kernel(x, ffn1_norm, ffn1_w_gate, ffn1_w_up, ffn1_w_down, mix_norm, w_in, b_in, sgu_norm_g, sgu_norm_b, sgu_w_s, sgu_b_s, ret_decay_logit, w_branch_a, w_branch_b, w_out, ffn2_norm, ffn2_w_gate, ffn2_w_up, ffn2_w_down, final_norm, loss_target, m_ffn1_norm, m_ffn1_w_gate, m_ffn1_w_up, m_ffn1_w_down, m_mix_norm, m_w_in, m_b_in, m_sgu_norm_g, m_sgu_norm_b, m_sgu_w_s, m_sgu_b_s, m_ret_decay_logit, m_w_branch_a, m_w_branch_b, m_w_out, m_ffn2_norm, m_ffn2_w_gate, m_ffn2_w_up, m_ffn2_w_down, m_final_norm, v_ffn1_norm, v_ffn1_w_gate, v_ffn1_w_up, v_ffn1_w_down, v_mix_norm, v_w_in, v_b_in, v_sgu_norm_g, v_sgu_norm_b, v_sgu_w_s, v_sgu_b_s, v_ret_decay_logit, v_w_branch_a, v_w_branch_b, v_w_out, v_ffn2_norm, v_ffn2_w_gate, v_ffn2_w_up, v_ffn2_w_down, v_final_norm):
    args = dict(locals())
    S = x.shape[1]
    xs = x[0]
    target = loss_target[0]

    def buf_layout(name, a):
        a = a[0]
        return a.T if name in W_TRANSPOSED else a

    sh = {n: buf_layout(n, args[n]).astype(CDT) for n in W_NAMES}
    wf = {}

    b3 = b_in.reshape(8, 1, D)
    ws = sgu_w_s[0].astype(CDT)
    wst = jnp.swapaxes(sgu_w_s[0], 1, 2).astype(CDT)
    bsc = sgu_b_s[0].reshape(G, C, 1)
    dl = jnp.zeros((H, 8, DK), f32).at[:, 0:2, :].set(jnp.broadcast_to(ret_decay_logit[0].T[:, :, None], (H, 2, DK)))
    theta = ROPE_BASE ** (-jnp.arange(0, DK, 2, dtype=f32) / DK)
    ang = jnp.arange(S, dtype=f32)[:, None] * theta[None, :]
    cos, sin = jnp.cos(ang), jnp.sin(ang)
    fnorm = final_norm.reshape(1, D)

    f1 = ("ffn1_w_gate", "ffn1_w_up", "ffn1_w_down")
    f2 = ("ffn2_w_gate", "ffn2_w_up", "ffn2_w_down")
    br = ("w_branch_a", "w_branch_b", "w_out")
    for cid, names in enumerate((f1, ("w_in",), br, f2)):
        wf.update(zip(names, _sequence("ag_" + names[0], 1 + cid, NEIGHBOURS, [_ag_hook(sh[n]) for n in names])))
    x1, g1, u1, a1, hf1 = _ffn_fwd("ffn1_fwd", xs, ffn1_norm, *[wf[n] for n in f1])
    proj, h2 = _proj_fwd(x1, mix_norm, wf["w_in"], b3, cos, sin)
    a = _sgu_fwd(proj, sgu_norm_g, sgu_norm_b, ws, bsc)
    R, r, sfs, sbs = _ret_fwd(proj, dl)
    x2, ya, yb = _merge_fwd(a, r, proj, x1, *[wf[n] for n in br])
    x3, g2, u2, a2, hf2 = _ffn_fwd("ffn2_fwd", x2, ffn2_norm, *[wf[n] for n in f2])
    dx3, dyh2, d_final, loss_part = _loss_head(x3, fnorm, target)

    my_c = lax.axis_index("c").astype(jnp.int32).reshape(1)
    gw, landed, sequenced = {}, {}, []

    def d2d(*names):
        return [_rs_d2d_hook(gw[n]) for n in names]

    def behind(x, token):
        return lax.optimization_barrier((x, token))[0]

    def to_chips(names, sibs, more=()):
        parts = list(_rs_sum("rs_sum_" + names[0], [gw[n] for n in names], list(sibs), my_c))
        token = parts[0]
        if sequenced:
            parts[0] = behind(parts[0], sequenced[-1])
        hooks = [_rs_ici_hook(p) for p in parts] + ([_small_hook(list(more))] if more else [])
        reach = EVERYONE if more else CHIPS
        got = _sequence("rs_chips_" + names[0], REACH_ID[reach], reach, hooks)
        sequenced.append(got[0])
        landed.update(zip(names, got))
        return got[len(names):], token

    def ffn_bwd(tag, names, dy, dyh, x, g, u, a, h, nrm, each_alone, more=()):
        wg, wu, wd = names
        (gw[wd],) = _wgrad(tag + "_wd_grad", a, dyh)
        dg, du, sib_d = _ffn_bwd_hidden(tag + "_bwd_hidden", dyh, g, u, wf[wd], d2d(wd))
        if each_alone:
            dg = behind(dg, to_chips([wd], [sib_d])[1])
        (gw[wg],) = _wgrad(tag + "_wg_grad", dg, h)
        gw[wu], sib_g = _wgrad(tag + "_wu_grad", du, h, d2d(wg))
        if each_alone:
            dy = behind(dy, to_chips([wg], [sib_g])[1])
        dx, dn, sib_u = _ffn_bwd_in(tag + "_bwd_in", dy, x, dg, du, nrm, wf[wg], wf[wu], d2d(wu))
        if each_alone:
            return (dx, dn) + to_chips([wu], [sib_u], more(dn))
        return (dx, dn) + to_chips([wd, wg, wu], [sib_d, sib_g, sib_u])

    dx2, d_ffn2n, _, token = ffn_bwd("ffn2", f2, dx3, dyh2, x2, g2, u2, a2, hf2, ffn2_norm, False)
    dproj, da, dr, mix, dx2b, dya, dyb = _merge_bwd_act(behind(dx2, token), ya, yb, proj, *[wf[n] for n in br])
    dproj, d_ws, d_bs, d_gn, d_bn = _sgu_bwd(da, proj, dproj, sgu_norm_g, sgu_norm_b, ws, wst, bsc)
    dproj, d_dec = _ret_bwd(dr, R, sfs, sbs, proj, dproj, cos, sin, dl)
    (gw["w_in"],) = _tn("win_grad", h2, dproj, _seg_of_slot)
    gw["w_out"], sib_win = _wgrad("wo_grad", mix, dx2b, d2d("w_in"))
    small_sgu = jnp.concatenate([d_ws.reshape(G * C, C), d_bs.reshape(G * 8, C)], axis=0)
    (g_sgu, gl), token = to_chips(["w_in"], [sib_win], [small_sgu, loss_part])
    (gw["w_branch_a"],) = _wgrad("wa_grad", a, behind(dya, token))
    (gw["w_branch_b"],) = _wgrad("wb_grad", r, dyb)
    dx1, dyh1, d_mixn, d_bin, *sib_br = _proj_bwd_act(behind(dproj, token), dx2, x1, mix_norm, wf["w_in"], d2d(*br))
    small_a = jnp.concatenate([jnp.zeros((1, D), f32), d_mixn, d_gn, d_bn, d_ffn2n, d_final, jnp.zeros((2, D), f32),
                               d_bin.reshape(8, D)], axis=0)
    (ga, g_dec), token = to_chips(list(br), sib_br, [small_a, d_dec.reshape(H * 8, 128)])
    gb = jnp.concatenate([g_sgu, g_dec], axis=1)
    dxs, d_ffn1n, (gn1,), token = ffn_bwd("ffn1", f1, dx1, behind(dyh1, token), xs, g1, u1, a1, hf1, ffn1_norm, True,
                                          lambda dn: [dn])

    out = {"grad_x": dxs[None]}

    def native(name, a):
        a = a.T if name in W_TRANSPOSED else a
        return a[None]

    after = token
    for names in (f2, ("w_in",), br, (f1[2], f1[0], f1[1])):
        res = _adamw_big("adamw_" + names[0], [landed[n] for n in names], [buf_layout(n, args[n]) for n in names],
                         [buf_layout(n, args["m_" + n]) for n in names], [buf_layout(n, args["v_" + n]) for n in names], after)
        after = res[-1][0]
        for n, four in zip(names, res):
            for pre, val in zip(("grad_", "delta_", "new_m_", "new_v_"), four):
                out[pre + n] = native(n, val)

    def pad_decay(a):
        return jnp.zeros((8, 128), f32).at[0:2, 0:H].set(a[0])

    small = [
        ("ffn1_norm", lambda a: a, lambda a: a), ("mix_norm", lambda a: a, lambda a: a),
        ("sgu_norm_g", lambda a: a, lambda a: a), ("sgu_norm_b", lambda a: a, lambda a: a),
        ("ffn2_norm", lambda a: a, lambda a: a),
        ("final_norm", lambda a: a.reshape(1, D), lambda a: a.reshape(D)),
        ("b_in", lambda a: a.reshape(8, D), lambda a: a.reshape(1, 8 * D)),
        ("sgu_w_s", lambda a: a.reshape(G * C, C), lambda a: a.reshape(1, G, C, C)),
        ("sgu_b_s", lambda a: a[0], lambda a: a[None]),
        ("ret_decay_logit", pad_decay, lambda a: a[None, 0:2, 0:H]),
    ]
    res = _adamw_small(ga, gb, gn1, gl, [(to(args[n]), to(args["m_" + n]), to(args["v_" + n])) for n, to, _ in small])
    out["loss"] = res[40][0, 0]
    for i, (n, _, back) in enumerate(small):
        for j, pre in enumerate(("grad_", "delta_", "new_m_", "new_v_")):
            out[pre + n] = back(res[4 * i + j])

    weights = ("ffn1_norm", "ffn1_w_gate", "ffn1_w_up", "ffn1_w_down", "mix_norm", "w_in", "b_in", "sgu_norm_g",
               "sgu_norm_b", "sgu_w_s", "sgu_b_s", "ret_decay_logit", "w_branch_a", "w_branch_b", "w_out", "ffn2_norm",
               "ffn2_w_gate", "ffn2_w_up", "ffn2_w_down", "final_norm")
    return (out["loss"], out["grad_x"], *[out["grad_" + n] for n in weights], *[out["delta_" + n] for n in weights],
            *[out["new_m_" + n] for n in weights], *[out["new_v_" + n] for n in weights])
```

```python
import math

import jax
import jax.numpy as jnp
from jax import lax
from jax.experimental import pallas as pl
from jax.experimental.pallas import tpu as pltpu
from jax.experimental.pallas import tpu_sc as plsc

f32 = jnp.float32
CDT = jnp.bfloat16

D = 1024
F = 2816
C = 128
RC = 256
H = 4
DK = 256
G = 4
NDEV = 8
NCHIP = 4
EPS = 1e-6
ROPE_BASE = 10000.0
FT = 256
V7X_VMEM_BYTES = 64 * 1024 * 1024
VMEM_LIMIT = V7X_VMEM_BYTES - 8 * 1024 * 1024

ADAM_LR, ADAM_B1, ADAM_B2, ADAM_EPS, ADAM_WD, ADAM_STEP = 0.001, 0.9, 0.999, 1e-08, 0.01, 10
BC1 = 1.0 - ADAM_B1 ** ADAM_STEP
BC2 = 1.0 - ADAM_B2 ** ADAM_STEP

W_ROWS = dict(ffn1_w_gate=352, ffn1_w_up=352, ffn1_w_down=352, w_in=1024, w_branch_a=128, w_branch_b=128, w_out=128,
              ffn2_w_gate=352, ffn2_w_up=352, ffn2_w_down=352)
W_NAMES = tuple(W_ROWS)
W_TRANSPOSED = ("ffn1_w_gate", "ffn1_w_up", "ffn2_w_gate", "ffn2_w_up")

SLOT_U, SLOT_V, SLOT_GA, SLOT_GB, SLOT_Q, SLOT_K, SLOT_VR, SLOT_GR = range(8)


SEG_OF_SLOT = (0, 1, 6, 7, 2, 3, 4, 5)


def _seg_of_slot(p):
    return jnp.where(p < 2, p, jnp.where(p < 4, p + 4, p - 2))


def _mm(a, b):
    return jnp.dot(a, b, preferred_element_type=f32)


def _mm_nt(a, b):
    return lax.dot_general(a, b, (((1,), (1,)), ((), ())), preferred_element_type=f32)


def _mm_tn(a, b):
    return lax.dot_general(a, b, (((0,), (0,)), ((), ())), preferred_element_type=f32)


def _params(*sem):
    return pltpu.CompilerParams(dimension_semantics=sem, vmem_limit_bytes=VMEM_LIMIT)


def _resident(shape, index_map):
    return pl.BlockSpec(shape, index_map, pipeline_mode=pl.Buffered(1))


def _gelu(x):
    return 0.5 * x * (1.0 + lax.erf(x * (1.0 / math.sqrt(2.0))))


def _gelu_grad(x):
    return 0.5 * (1.0 + lax.erf(x * (1.0 / math.sqrt(2.0)))) + x * jnp.exp(-0.5 * x * x) * (1.0 / math.sqrt(2.0 * math.pi))


def _rms_fwd(x, n):
    r = lax.rsqrt(jnp.mean(x * x, axis=-1, keepdims=True) + EPS)
    xh = x * r
    return r, xh, xh * n


def _rms_bwd(dh, r, xh, n):
    dxh = dh * n
    dx = r * (dxh - xh * jnp.mean(dxh * xh, axis=-1, keepdims=True))
    return dx, jnp.sum(dh * xh, axis=0, keepdims=True)


MESH_ID = pl.DeviceIdType.MESH
_HBM = pl.BlockSpec(memory_space=pltpu.HBM)


def _my_place():
    return lax.axis_index("x"), lax.axis_index("y"), lax.axis_index("c")


def _ici_peers(x, y, c):
    return [((1 - x, y, c), 2 * (1 - x) + y), ((x, 1 - y, c), 2 * x + 1 - y), ((1 - x, 1 - y, c), 2 * (1 - x) + 1 - y)]


class _Hook:
    def __init__(self, operands, out_shapes, n_remote, n_local, start, finish, relay=None):
        self.operands, self.out_shapes = list(operands), list(out_shapes)
        self.n_remote, self.n_local, self.start, self.finish = n_remote, n_local, start, finish
        self.relay = relay or (lambda *a: None)


def _call(body, hooks, operands, *, in_specs, out_specs, out_shape, grid=None, scratch_shapes=(), **kw):
    hooks = tuple(hooks)
    n_in, n_out, n_scr = len(in_specs), len(out_shape), len(scratch_shapes)
    h_ops = [a for h in hooks for a in h.operands]
    h_outs = [s for h in hooks for s in h.out_shapes]
    h_sems = [pltpu.SemaphoreType.DMA((n,)) for h in hooks for n in (h.n_remote, h.n_remote, max(h.n_local, 1))]

    def wrapped(*refs):
        ins, hin = refs[:n_in], refs[n_in:n_in + len(h_ops)]
        o0 = n_in + len(h_ops)
        outs, hout = refs[o0:o0 + n_out], refs[o0 + n_out:o0 + n_out + len(h_outs)]
        s0 = o0 + n_out + len(h_outs)
        scr, hsem = refs[s0:s0 + n_scr], refs[s0 + n_scr:]

        def run(phase):
            ip = op = 0
            for i, h in enumerate(hooks):
                ssem, rsem, lsem = hsem[3 * i:3 * i + 3]

                def rcopy(k, src, dst, dev, ssem=ssem, rsem=rsem):
                    return pltpu.make_async_remote_copy(src_ref=src, dst_ref=dst, send_sem=ssem.at[k], recv_sem=rsem.at[k],
                                                        device_id=dev, device_id_type=MESH_ID)

                def lcopy(k, src, dst, lsem=lsem):
                    return pltpu.make_async_copy(src, dst, lsem.at[k])

                getattr(h, phase)(hin[ip:ip + len(h.operands)], hout[op:op + len(h.out_shapes)], rcopy, lcopy)
                ip += len(h.operands)
                op += len(h.out_shapes)

        def at_edge(phase, last):
            if not hooks:
                return
            if grid is None:
                run(phase)
                return
            cond = None
            for ax, n in enumerate(grid):
                here = pl.program_id(ax) == (n - 1 if last else 0)
                cond = here if cond is None else cond & here
            pl.when(cond)(lambda: run(phase))

        at_edge("start", False)
        at_edge("relay", True)
        body(*ins, *outs, *scr)
        at_edge("finish", True)

    if grid is not None:
        kw["grid"] = grid
    return list(pl.pallas_call(
        wrapped, out_shape=list(out_shape) + h_outs, in_specs=list(in_specs) + [_HBM] * len(h_ops),
        out_specs=list(out_specs) + [_HBM] * len(h_outs), scratch_shapes=list(scratch_shapes) + h_sems, **kw,
    )(*operands, *h_ops))


def _rows(ref, start, n):
    return ref.at[pl.ds(start, n), :]


def _ag_hook(shard):
    rows = shard.shape[0]
    half = rows // 2
    assert half % 16 == 0

    def place():
        x, y, c = _my_place()
        devs = dict(sib=(x, y, 1 - c), xn=(1 - x, y, c), yn=(x, 1 - y, c))
        chips = dict(me=2 * x + y, xn=2 * (1 - x) + y, yn=2 * x + 1 - y, dg=2 * (1 - x) + 1 - y)
        return c, devs, chips

    def block(full, chip, c):
        return _rows(full, (2 * chip + c) * rows, rows)

    def halfblock(full, chip, c, upper):
        return _rows(full, (2 * chip + c) * rows + upper * half, half)

    def start(ins, outs, rcopy, lcopy):
        c, devs, chips = place()
        src, dst = ins[0], block(outs[0], chips["me"], c)
        lcopy(0, src, dst).start()
        for k, to in enumerate(("sib", "xn", "yn")):
            rcopy(k, src, dst, devs[to]).start()

    def relay(ins, outs, rcopy, lcopy):
        c, devs, chips = place()
        full = outs[0]
        blk = block(full, chips["xn"], c)
        rcopy(1, blk, blk, devs["xn"]).wait_recv()
        low = halfblock(full, chips["xn"], c, 0)
        rcopy(3, low, low, devs["yn"]).start()
        rcopy(5, blk, blk, devs["sib"]).start()
        blk = block(full, chips["yn"], c)
        rcopy(2, blk, blk, devs["yn"]).wait_recv()
        up = halfblock(full, chips["yn"], c, 1)
        rcopy(4, up, up, devs["xn"]).start()
        rcopy(6, blk, blk, devs["sib"]).start()
        low, up = halfblock(full, chips["dg"], c, 0), halfblock(full, chips["dg"], c, 1)
        rcopy(3, low, low, devs["yn"]).wait_recv()
        rcopy(4, up, up, devs["xn"]).wait_recv()
        blk = block(full, chips["dg"], c)
        rcopy(7, blk, blk, devs["sib"]).start()

    def finish(ins, outs, rcopy, lcopy):
        c, devs, chips = place()
        full, sib = outs[0], devs["sib"]
        for k, chip in ((0, "me"), (5, "xn"), (6, "yn"), (7, "dg")):
            theirs = block(full, chips[chip], 1 - c)
            rcopy(k, theirs, theirs, sib).wait_recv()
            mine = block(full, chips[chip], c)
            if k:
                rcopy(k, mine, mine, sib).wait_send()
        src, dst = ins[0], block(full, chips["me"], c)
        lcopy(0, src, dst).wait()
        for k, to in enumerate(("sib", "xn", "yn")):
            rcopy(k, src, dst, devs[to]).wait_send()
        low, up = halfblock(full, chips["xn"], c, 0), halfblock(full, chips["yn"], c, 1)
        rcopy(3, low, low, devs["yn"]).wait_send()
        rcopy(4, up, up, devs["xn"]).wait_send()

    return _Hook([shard], [jax.ShapeDtypeStruct((NDEV * rows, D), shard.dtype)], 8, 1, start, finish, relay)


SIBLING, CHIPS, NEIGHBOURS, EVERYONE = "sibling", "chips", "sibling and the two neighbour chips", "everyone"
REACH_ID = {CHIPS: 6, EVERYONE: 7}
SIBLING_ID = 5


def _sequence(name, collective_id, reach, hooks):
    ins = [[jax.new_ref(a, memory_space=pltpu.MemorySpace.HBM) for a in h.operands] for h in hooks]
    outs = [[jax.empty_ref(s, memory_space=pltpu.MemorySpace.HBM) for s in h.out_shapes] for h in hooks]
    sems = tuple(pltpu.SemaphoreType.DMA((n,)) for h in hooks for n in (h.n_remote, h.n_remote, max(h.n_local, 1)))

    @pl.kernel(mesh=plsc.ScalarSubcoreMesh(axis_name="sequencer", num_cores=1), name=name, scratch_types=sems,
               compiler_params=pltpu.CompilerParams(collective_id=collective_id))
    def launch(*sem_refs):
        x, y, c = _my_place()
        chips = [dev for dev, _ in _ici_peers(x, y, c)]
        others = [(1 - x if dx else x, 1 - y if dy else y, 1 - c if dc else c)
                  for dx in range(2) for dy in range(2) for dc in range(2) if dx + dy + dc]
        devs = {SIBLING: [(x, y, 1 - c)], CHIPS: chips, NEIGHBOURS: [(x, y, 1 - c), (1 - x, y, c), (x, 1 - y, c)],
                EVERYONE: others}[reach]
        barrier = pltpu.get_barrier_semaphore()
        for dev in devs:
            pl.semaphore_signal(barrier, inc=1, device_id=dev, device_id_type=MESH_ID)
        pl.semaphore_wait(barrier, len(devs))
        for phase in ("start", "relay", "finish"):
            for i, h in enumerate(hooks):
                ssem, rsem, lsem = sem_refs[3 * i:3 * i + 3]

                def rcopy(k, src, dst, dev, ssem=ssem, rsem=rsem):
                    return pltpu.make_async_remote_copy(src_ref=src, dst_ref=dst, send_sem=ssem.at[k], recv_sem=rsem.at[k],
                                                        device_id=dev, device_id_type=MESH_ID)

                def lcopy(k, src, dst, lsem=lsem):
                    return pltpu.make_async_copy(src, dst, lsem.at[k])

                getattr(h, phase)(ins[i], outs[i], rcopy, lcopy)

    launch()
    return [o[...] for os in outs for o in os]


def _rs_d2d_hook(gfull):
    rows = gfull.shape[0] // NDEV

    def pairs(g, land):
        x, y, c = _my_place()
        return (x, y, 1 - c), [(k, _rows(g, (2 * k + 1 - c) * rows, rows), land.at[k]) for k in range(NCHIP)]

    def start(ins, outs, rcopy, lcopy):
        sib, cps = pairs(ins[0], outs[0])
        for i, src, dst in cps:
            rcopy(i, src, dst, sib).start()

    def finish(ins, outs, rcopy, lcopy):
        sib, cps = pairs(ins[0], outs[0])
        for i, src, dst in cps:
            rcopy(i, dst, dst, sib).wait_recv()
        for i, src, dst in cps:
            rcopy(i, src, dst, sib).wait_send()

    return _Hook([gfull], [jax.ShapeDtypeStruct((NCHIP, rows, D), gfull.dtype)], NCHIP, 0, start, finish)


def _rs_ici_hook(part):
    def start(ins, outs, rcopy, lcopy):
        x, y, c = _my_place()
        mychip = 2 * x + y
        lcopy(0, ins[0].at[mychip], outs[0].at[mychip]).start()
        for j, (dev, chip) in enumerate(_ici_peers(x, y, c)):
            rcopy(j, ins[0].at[chip], outs[0].at[mychip], dev).start()

    def finish(ins, outs, rcopy, lcopy):
        x, y, c = _my_place()
        mychip = 2 * x + y
        peers = _ici_peers(x, y, c)
        for j, (dev, chip) in enumerate(peers):
            rcopy(j, outs[0].at[chip], outs[0].at[chip], dev).wait_recv()
        for j, (dev, chip) in enumerate(peers):
            rcopy(j, ins[0].at[chip], outs[0].at[mychip], dev).wait_send()
        lcopy(0, ins[0].at[mychip], outs[0].at[mychip]).wait()

    return _Hook([part], [jax.ShapeDtypeStruct(part.shape, part.dtype)], 3, 1, start, finish)


def _small_hook(arrays):
    n = len(arrays)

    def peers():
        x, y, c = _my_place()
        out = []
        for dx in range(2):
            for dy in range(2):
                for dc in range(2):
                    if dx + dy + dc:
                        px, py, pc = (1 - x if dx else x), (1 - y if dy else y), (1 - c if dc else c)
                        out.append(((px, py, pc), 4 * px + 2 * py + pc))
        return 4 * x + 2 * y + c, out

    def start(ins, outs, rcopy, lcopy):
        me, ps = peers()
        for t in range(n):
            lcopy(t, ins[t], outs[t].at[me]).start()
            for i, (dev, _) in enumerate(ps):
                rcopy(n * i + t, ins[t], outs[t].at[me], dev).start()

    def finish(ins, outs, rcopy, lcopy):
        me, ps = peers()
        for t in range(n):
            for i, (dev, peer) in enumerate(ps):
                rcopy(n * i + t, outs[t].at[peer], outs[t].at[peer], dev).wait_recv()
            for i, (dev, _) in enumerate(ps):
                rcopy(n * i + t, ins[t], outs[t].at[me], dev).wait_send()
            lcopy(t, ins[t], outs[t].at[me]).wait()

    return _Hook(arrays, [jax.ShapeDtypeStruct((NDEV,) + a.shape, a.dtype) for a in arrays], 7 * n, n, start, finish)


def _wblock(w):
    return _resident(w.shape, lambda *_: (0, 0))


def _ffn_fwd(name, x, nrm, wg, wu, wd, hooks=(), tm=512):
    S = x.shape[0]

    def body(x_ref, n_ref, wg_ref, wu_ref, wd_ref, y_ref, g_ref, u_ref, a_ref, h_ref, acc_ref):
        xv = x_ref[...]
        _, _, h = _rms_fwd(xv, n_ref[...])
        h = h.astype(CDT)
        h_ref[...] = h
        for ci in range(F // FT):
            sl = slice(ci * FT, (ci + 1) * FT)
            g = _mm_nt(h, wg_ref[sl, :])
            u = _mm_nt(h, wu_ref[sl, :])
            g_ref[:, sl] = g.astype(CDT)
            u_ref[:, sl] = u.astype(CDT)
            a = (g * jax.nn.sigmoid(g) * u).astype(CDT)
            a_ref[:, sl] = a
            o = _mm(a, wd_ref[sl, :])
            if ci == 0:
                acc_ref[...] = o
            else:
                acc_ref[...] += o
        y_ref[...] = xv + 0.5 * acc_ref[...]

    tok = pl.BlockSpec((tm, D), lambda i: (i, 0))
    hid = pl.BlockSpec((tm, F), lambda i: (i, 0))
    hidden = jax.ShapeDtypeStruct((S, F), CDT)
    return _call(
        body, hooks, [x, nrm, wg, wu, wd], name=name, grid=(S // tm,),
        out_shape=[jax.ShapeDtypeStruct((S, D), f32), hidden, hidden, hidden, jax.ShapeDtypeStruct((S, D), CDT)],
        in_specs=[tok, _resident((1, D), lambda i: (0, 0)), _wblock(wg), _wblock(wu), _wblock(wd)],
        out_specs=[tok, hid, hid, hid, tok],
        scratch_shapes=[pltpu.VMEM((tm, D), f32)],
        compiler_params=_params("arbitrary"),
    )


def _proj_fwd(x1, nrm, wfull, b3, cos, sin, hooks=(), tm=512):
    S = x1.shape[0]

    def body(x_ref, n_ref, w_ref, b_ref, cos_ref, sin_ref, p_ref, h_ref):
        _, _, h = _rms_fwd(x_ref[...], n_ref[...])
        h = h.astype(CDT)
        h_ref[...] = h
        for p in range(8):
            seg = SEG_OF_SLOT[p]
            z = _mm(h, w_ref[seg * D:(seg + 1) * D, :]) + b_ref[seg]
            if p in (SLOT_Q, SLOT_K):
                co, si = cos_ref[...], sin_ref[...]
                for hh in range(H):
                    cs = slice(hh * DK, (hh + 1) * DK)
                    zr = _rotate(z[:, cs], co, si)
                    p_ref[p, :, cs] = (zr * K_SCALE if p == SLOT_K else zr).astype(CDT)
            else:
                p_ref[p] = z.astype(CDT)

    tab = pl.BlockSpec((tm, DK // 2), lambda i: (i, 0))
    return _call(
        body, hooks, [x1, nrm, wfull, b3, cos, sin], name="proj_fwd", grid=(S // tm,),
        out_shape=[jax.ShapeDtypeStruct((8, S, D), CDT), jax.ShapeDtypeStruct((S, D), CDT)],
        in_specs=[pl.BlockSpec((tm, D), lambda i: (i, 0)), _resident((1, D), lambda i: (0, 0)),
                  _resident((8 * D, D), lambda i: (0, 0)), _resident((8, 1, D), lambda i: (0, 0, 0)), tab, tab],
        out_specs=[pl.BlockSpec((8, tm, D), lambda i: (0, i, 0)), pl.BlockSpec((tm, D), lambda i: (i, 0))],
        compiler_params=_params("arbitrary"),
    )


def _sgu_norm(va, gn, bn):
    mu = jnp.mean(va, axis=-1, keepdims=True)
    xc = va - mu
    rstd = lax.rsqrt(jnp.mean(xc * xc, axis=-1, keepdims=True) + EPS)
    vhat = xc * rstd
    return rstd, vhat, vhat * gn + bn


def _sgu_fwd(proj, gn, bn, ws, bsc, tm=512):
    S = proj.shape[1]
    GW = D // G

    def body(p_ref, gn_ref, bn_ref, ws_ref, bs_ref, a_ref):
        ua = _gelu(p_ref[0].astype(f32))
        va = _gelu(p_ref[1].astype(f32))
        _, _, vn = _sgu_norm(va, gn_ref[...], bn_ref[...])
        vn = vn.astype(CDT)
        for ch in range(tm // C):
            rs = slice(ch * C, (ch + 1) * C)
            for gi in range(G):
                cs = slice(gi * GW, (gi + 1) * GW)
                s = _mm(ws_ref[gi], vn[rs, cs]) + bs_ref[gi]
                a_ref[rs, cs] = (ua[rs, cs] * s).astype(CDT)

    return pl.pallas_call(
        body, name="sgu_fwd", grid=(S // tm,),
        out_shape=jax.ShapeDtypeStruct((S, D), CDT),
        in_specs=[pl.BlockSpec((2, tm, D), lambda i: (0, i, 0)), _resident((1, D), lambda i: (0, 0)),
                  _resident((1, D), lambda i: (0, 0)), _resident((G, C, C), lambda i: (0, 0, 0)),
                  _resident((G, C, 1), lambda i: (0, 0, 0))],
        out_specs=pl.BlockSpec((tm, D), lambda i: (i, 0)),
        compiler_params=_params("arbitrary"),
    )(proj, gn, bn, ws, bsc)


def _decay_tables(dl_ref):
    lg = jax.nn.log_sigmoid(dl_ref[0:2, :])
    lgf, lgb = lg[0:1, :], lg[1:2, :]
    assert RC <= DK
    ri = lax.broadcasted_iota(jnp.int32, (RC, RC), 0)
    ci = lax.broadcasted_iota(jnp.int32, (RC, RC), 1)
    d = (ri - ci).astype(f32)
    lower = d >= 0
    dmat = jnp.where(lower, jnp.exp(d * lgf[:, :RC]), jnp.exp(-d * lgb[:, :RC]))
    dmat_t = jnp.where(d <= 0, jnp.exp(-d * lgf[:, :RC]), jnp.exp(d * lgb[:, :RC]))
    pos = lax.broadcasted_iota(jnp.int32, (RC, DK), 0).astype(f32)
    t = dict(
        lgf=lgf, lgb=lgb, d=d, lower=lower, dmat=dmat, dmat_t=dmat_t, pos=pos,
        fq=jnp.exp((pos + 1.0) * lgf), fk=jnp.exp((RC - 1.0 - pos) * lgf),
        bq=jnp.exp((RC - pos) * lgb), bk=jnp.exp(pos * lgb),
        lamf=jnp.exp(float(RC) * lgf), lamb=jnp.exp(float(RC) * lgb),
    )
    return t


def _rotate(t, co, si):
    t1, t2 = t[:, :DK // 2], t[:, DK // 2:]
    return jnp.concatenate([t1 * co - t2 * si, t2 * co + t1 * si], axis=-1)


def _unrotate(t, co, si):
    t1, t2 = t[:, :DK // 2], t[:, DK // 2:]
    return jnp.concatenate([t1 * co + t2 * si, t2 * co - t1 * si], axis=-1)


K_SCALE = DK ** -0.5
ROW_TILE = 256


def _ret_fwd(proj, dl, hooks=()):
    S = proj.shape[1]
    NC = S // RC

    def body(q_ref, k_ref, v_ref, g_ref, dl_ref, R_ref, r_ref, sfs_ref, sbs_ref, rb_ref, sf_ref, sb_ref):
        t = _decay_tables(dl_ref)

        def chunk(n):
            rows = pl.ds(pl.multiple_of(n * RC, RC), RC)
            return rows, q_ref[rows, :], k_ref[rows, :], v_ref[rows, :]

        sf_ref[...] = jnp.zeros_like(sf_ref)
        sb_ref[...] = jnp.zeros_like(sb_ref)

        def step(i, carry):
            rows, qn, kn, vn = chunk(i)
            sc = _mm_nt(qn, kn) * t["dmat"]
            out = _mm(sc.astype(CDT), vn)
            sf = sf_ref[...]
            sfb = sf.astype(CDT)
            sfs_ref[i] = sfb
            R_ref[rows, :] = out + _mm((qn.astype(f32) * t["fq"]).astype(CDT), sfb)
            sf_ref[...] = sf * t["lamf"] + _mm_tn((kn.astype(f32) * t["fk"]).astype(CDT), vn)
            m = NC - 1 - i
            rows, qn, kn, vn = chunk(m)
            sb = sb_ref[...]
            sbb = sb.astype(CDT)
            sbs_ref[m] = sbb
            rb_ref[rows, :] = _mm((qn.astype(f32) * t["bq"]).astype(CDT), sbb)
            sb_ref[...] = sb * t["lamb"] + _mm_tn((kn.astype(f32) * t["bk"]).astype(CDT), vn)
            return carry

        lax.fori_loop(0, NC, step, 0, unroll=4)

        def finish(i, carry):
            rs = pl.ds(pl.multiple_of(i * ROW_TILE, ROW_TILE), ROW_TILE)
            R = R_ref[rs, :] + rb_ref[rs, :]
            R_ref[rs, :] = R
            rn = R * lax.rsqrt(jnp.mean(R * R, axis=-1, keepdims=True) + EPS)
            g = g_ref[rs, :].astype(f32)
            r_ref[rs, :] = (rn * g * jax.nn.sigmoid(g)).astype(CDT)
            return carry

        lax.fori_loop(0, S // ROW_TILE, finish, 0)

    def seg(slot):
        return pl.BlockSpec((None, S, DK), lambda h: (slot, 0, h))

    states = jax.ShapeDtypeStruct((H, NC, DK, DK), CDT)
    state_blk = pl.BlockSpec((None, NC, DK, DK), lambda h: (h, 0, 0, 0))
    return _call(
        body, hooks, [proj, proj, proj, proj, dl], name="ret_fwd", grid=(H,),
        out_shape=[jax.ShapeDtypeStruct((S, H * DK), f32), jax.ShapeDtypeStruct((S, H * DK), CDT), states, states],
        in_specs=[seg(SLOT_Q), seg(SLOT_K), seg(SLOT_VR), seg(SLOT_GR), pl.BlockSpec((None, 8, DK), lambda h: (h, 0, 0))],
        out_specs=[pl.BlockSpec((S, DK), lambda h: (0, h)), pl.BlockSpec((S, DK), lambda h: (0, h)), state_blk, state_blk],
        scratch_shapes=[pltpu.VMEM((S, DK), f32), pltpu.VMEM((DK, DK), f32), pltpu.VMEM((DK, DK), f32)],
        compiler_params=_params("arbitrary"),
    )


def _merge_fwd(a, r, proj, x1, wa, wb, wo, hooks=(), tm=512):
    S = x1.shape[0]

    def body(a_ref, r_ref, gt_ref, x_ref, wa_ref, wb_ref, wo_ref, x2_ref, ya_ref, yb_ref):
        ya = _mm(a_ref[...], wa_ref[...])
        yb = _mm(r_ref[...], wb_ref[...])
        ya_ref[...] = ya.astype(CDT)
        yb_ref[...] = yb.astype(CDT)
        mix = jax.nn.sigmoid(gt_ref[0].astype(f32)) * ya + jax.nn.sigmoid(gt_ref[1].astype(f32)) * yb
        x2_ref[...] = x_ref[...] + _mm(mix.astype(CDT), wo_ref[...])

    tok = pl.BlockSpec((tm, D), lambda i: (i, 0))
    return _call(
        body, hooks, [a, r, proj, x1, wa, wb, wo], name="merge_fwd", grid=(S // tm,),
        out_shape=[jax.ShapeDtypeStruct((S, D), f32), jax.ShapeDtypeStruct((S, D), CDT), jax.ShapeDtypeStruct((S, D), CDT)],
        in_specs=[tok, tok, pl.BlockSpec((2, tm, D), lambda i: (SLOT_GA // 2, i, 0)), tok,
                  _wblock(wa), _wblock(wb), _wblock(wo)],
        out_specs=[tok, tok, tok],
        compiler_params=_params("arbitrary"),
    )


def _loss_head(x3, fn, target, tm=512):
    S = x3.shape[0]

    def body(x_ref, n_ref, t_ref, dx_ref, dxh_ref, dn_ref, l_ref):
        n = n_ref[...]
        r, xh, y = _rms_fwd(x_ref[...], n)
        e = y - t_ref[...]
        dy = e * (1.0 / D)
        dx, dn = _rms_bwd(dy, r, xh, n)
        dx_ref[...] = dx
        dxh_ref[...] = (0.5 * dx).astype(CDT)
        part = 0.5 * jnp.sum(jnp.sum(e * e, axis=-1, keepdims=True), axis=0, keepdims=True) * (1.0 / D)

        @pl.when(pl.program_id(0) == 0)
        def _():
            dn_ref[...] = jnp.zeros_like(dn_ref)
            l_ref[...] = jnp.zeros_like(l_ref)

        dn_ref[...] += dn
        l_ref[...] += jnp.broadcast_to(part, l_ref.shape)

    tok = pl.BlockSpec((tm, D), lambda i: (i, 0))
    return pl.pallas_call(
        body, name="loss_head", grid=(S // tm,),
        out_shape=[jax.ShapeDtypeStruct((S, D), f32), jax.ShapeDtypeStruct((S, D), CDT), jax.ShapeDtypeStruct((1, D), f32),
                   jax.ShapeDtypeStruct((8, 128), f32)],
        in_specs=[tok, _resident((1, D), lambda i: (0, 0)), tok],
        out_specs=[tok, tok, pl.BlockSpec((1, D), lambda i: (0, 0)), pl.BlockSpec((8, 128), lambda i: (0, 0))],
        compiler_params=_params("arbitrary"),
    )(x3, fn, target)


def _ffn_bwd_hidden(name, dyh, g, u, wd, hooks=(), tm=512):
    S = dyh.shape[0]

    def body(dyh_ref, g_ref, u_ref, wd_ref, dg_ref, du_ref):
        dyh = dyh_ref[...]
        for ci in range(F // FT):
            sl = slice(ci * FT, (ci + 1) * FT)
            da = _mm_nt(dyh, wd_ref[sl, :])
            gv = g_ref[:, sl].astype(f32)
            uv = u_ref[:, sl].astype(f32)
            s = jax.nn.sigmoid(gv)
            silu = gv * s
            du_ref[:, sl] = (da * silu).astype(CDT)
            dg_ref[:, sl] = (da * uv * (s + silu - silu * s)).astype(CDT)

    hid = pl.BlockSpec((tm, F), lambda i: (i, 0))
    hidden = jax.ShapeDtypeStruct((S, F), CDT)
    return _call(
        body, hooks, [dyh, g, u, wd], name=name, grid=(S // tm,), out_shape=[hidden, hidden],
        in_specs=[pl.BlockSpec((tm, D), lambda i: (i, 0)), hid, hid, _wblock(wd)], out_specs=[hid, hid],
        compiler_params=_params("arbitrary"),
    )


def _ffn_bwd_in(name, dy, x, dg, du, nrm, wg, wu, hooks=(), tm=512):
    S = x.shape[0]

    def body(dy_ref, x_ref, dg_ref, du_ref, n_ref, wg_ref, wu_ref, dx_ref, dn_ref, acc_ref):
        n = n_ref[...]
        r, xh, _ = _rms_fwd(x_ref[...], n)
        for ci in range(F // FT):
            sl = slice(ci * FT, (ci + 1) * FT)
            dh = _mm(dg_ref[:, sl], wg_ref[sl, :]) + _mm(du_ref[:, sl], wu_ref[sl, :])
            if ci == 0:
                acc_ref[...] = dh
            else:
                acc_ref[...] += dh
        dx, dn = _rms_bwd(acc_ref[...], r, xh, n)
        dx_ref[...] = dy_ref[...] + dx

        @pl.when(pl.program_id(0) == 0)
        def _():
            dn_ref[...] = jnp.zeros_like(dn_ref)

        dn_ref[...] += dn

    tok = pl.BlockSpec((tm, D), lambda i: (i, 0))
    hid = pl.BlockSpec((tm, F), lambda i: (i, 0))
    return _call(
        body, hooks, [dy, x, dg, du, nrm, wg, wu], name=name, grid=(S // tm,),
        out_shape=[jax.ShapeDtypeStruct((S, D), f32), jax.ShapeDtypeStruct((1, D), f32)],
        in_specs=[tok, tok, hid, hid, _resident((1, D), lambda i: (0, 0)), _wblock(wg), _wblock(wu)],
        out_specs=[tok, pl.BlockSpec((1, D), lambda i: (0, 0))],
        scratch_shapes=[pltpu.VMEM((tm, D), f32)],
        compiler_params=_params("arbitrary"),
    )


TN_ROWS = 512


def _tn(name, xs, ys, block_of, hooks=()):
    S, M = xs.shape
    B = ys.shape[0]
    tr = TN_ROWS if M % TN_ROWS == 0 else M // 2
    assert M % tr == 0 and tr % 128 == 0
    nt = M // tr

    def body(x_ref, y_ref, o_ref):
        o_ref[...] = _mm_tn(x_ref[...], y_ref[...]).astype(CDT)

    return _call(
        body, hooks, [xs, ys], name=name, grid=(B, nt),
        out_shape=[jax.ShapeDtypeStruct((B * M, D), CDT)],
        in_specs=[pl.BlockSpec((S, tr), lambda b, i: (0, i)), pl.BlockSpec((None, S, D), lambda b, i: (b, 0, 0))],
        out_specs=[pl.BlockSpec((tr, D), lambda b, i: (block_of(b) * nt + i, 0))],
        compiler_params=_params("arbitrary", "arbitrary"),
    )


def _wgrad(name, xs, y, hooks=()):
    return _tn(name, xs, y[None], lambda b: 0, hooks)


def _merge_bwd_act(dx2, ya, yb, proj, wa, wb, wo, hooks=(), tm=512):
    S = dx2.shape[0]

    def body(dx_ref, ya_ref, yb_ref, gt_ref, wa_ref, wb_ref, wo_ref,
             dp_ref, da_ref, dr_ref, mix_ref, dxb_ref, dya_ref, dyb_ref):
        dxb = dx_ref[...].astype(CDT)
        dxb_ref[...] = dxb
        dmix = _mm_nt(dxb, wo_ref[...])
        ya = ya_ref[...].astype(f32)
        yb = yb_ref[...].astype(f32)
        sa = jax.nn.sigmoid(gt_ref[0].astype(f32))
        sb = jax.nn.sigmoid(gt_ref[1].astype(f32))
        mix_ref[...] = (sa * ya + sb * yb).astype(CDT)
        dya = (dmix * sa).astype(CDT)
        dyb = (dmix * sb).astype(CDT)
        dya_ref[...] = dya
        dyb_ref[...] = dyb
        dp_ref[0] = (dmix * ya * sa * (1.0 - sa)).astype(CDT)
        dp_ref[1] = (dmix * yb * sb * (1.0 - sb)).astype(CDT)
        da_ref[...] = _mm_nt(dya, wa_ref[...]).astype(CDT)
        dr_ref[...] = _mm_nt(dyb, wb_ref[...]).astype(CDT)

    tok = pl.BlockSpec((tm, D), lambda i: (i, 0))
    gates = pl.BlockSpec((2, tm, D), lambda i: (SLOT_GA // 2, i, 0))
    act = jax.ShapeDtypeStruct((S, D), CDT)
    return _call(
        body, hooks, [dx2, ya, yb, proj, wa, wb, wo], name="merge_bwd_act", grid=(S // tm,),
        out_shape=[jax.ShapeDtypeStruct((8, S, D), CDT), act, act, act, act, act, act],
        in_specs=[tok, tok, tok, gates, _wblock(wa), _wblock(wb), _wblock(wo)],
        out_specs=[gates, tok, tok, tok, tok, tok, tok],
        compiler_params=_params("arbitrary"),
    )


def _sgu_bwd(da, proj, dproj, gn, bn, ws, wst, bsc, hooks=(), tm=512):
    S = proj.shape[1]
    GW = D // G

    def body(da_ref, p_ref, dpin_ref, gn_ref, bn_ref, ws_ref, wst_ref, bs_ref,
             dp_ref, dws_ref, dbs_ref, dgn_ref, dbn_ref, ds_ref, dvn_ref):
        @pl.when(pl.program_id(0) == 0)
        def _():
            dws_ref[...] = jnp.zeros_like(dws_ref)
            dbs_ref[...] = jnp.zeros_like(dbs_ref)
            dgn_ref[...] = jnp.zeros_like(dgn_ref)
            dbn_ref[...] = jnp.zeros_like(dbn_ref)

        pu = p_ref[0].astype(f32)
        pv = p_ref[1].astype(f32)
        ua = _gelu(pu)
        va = _gelu(pv)
        gn = gn_ref[...]
        rstd, vhat, vn = _sgu_norm(va, gn, bn_ref[...])
        vnb = vn.astype(CDT)
        dav = da_ref[...].astype(f32)
        dsb = (dav * ua).astype(CDT)
        ones = jnp.ones((8, GW), CDT)
        for ch in range(tm // C):
            rs = slice(ch * C, (ch + 1) * C)
            for gi in range(G):
                cs = slice(gi * GW, (gi + 1) * GW)
                s = _mm(ws_ref[gi], vnb[rs, cs]) + bs_ref[gi]
                ds_ref[rs, cs] = s
                dsg = dsb[rs, cs]
                dws_ref[gi] += _mm_nt(dsg, vnb[rs, cs])
                dbs_ref[gi] += _mm_nt(ones, dsg)
                dvn_ref[rs, cs] = _mm(wst_ref[gi], dsg)
        dp_ref[0] = (dav * ds_ref[...] * _gelu_grad(pu)).astype(CDT)
        dvn = dvn_ref[...]
        dgn_ref[...] += jnp.sum(dvn * vhat, axis=0, keepdims=True)
        dbn_ref[...] += jnp.sum(dvn, axis=0, keepdims=True)
        dvh = dvn * gn
        dva = rstd * (dvh - jnp.mean(dvh, axis=-1, keepdims=True) - vhat * jnp.mean(dvh * vhat, axis=-1, keepdims=True))
        dp_ref[1] = (dva * _gelu_grad(pv)).astype(CDT)

    uv = pl.BlockSpec((2, tm, D), lambda i: (0, i, 0))
    row = _resident((1, D), lambda i: (0, 0))
    return _call(
        body, hooks, [da, proj, dproj, gn, bn, ws, wst, bsc], name="sgu_bwd", grid=(S // tm,),
        out_shape=[jax.ShapeDtypeStruct(dproj.shape, CDT), jax.ShapeDtypeStruct((G, C, C), f32),
                   jax.ShapeDtypeStruct((G, 8, C), f32), jax.ShapeDtypeStruct((1, D), f32), jax.ShapeDtypeStruct((1, D), f32)],
        in_specs=[pl.BlockSpec((tm, D), lambda i: (i, 0)), uv, _HBM, row, row,
                  _resident((G, C, C), lambda i: (0, 0, 0)), _resident((G, C, C), lambda i: (0, 0, 0)),
                  _resident((G, C, 1), lambda i: (0, 0, 0))],
        out_specs=[uv, pl.BlockSpec((G, C, C), lambda i: (0, 0, 0)), pl.BlockSpec((G, 8, C), lambda i: (0, 0, 0)),
                   pl.BlockSpec((1, D), lambda i: (0, 0)), pl.BlockSpec((1, D), lambda i: (0, 0))],
        scratch_shapes=[pltpu.VMEM((tm, D), f32), pltpu.VMEM((tm, D), f32)],
        input_output_aliases={2: 0},
        compiler_params=_params("arbitrary"),
    )


def _ret_bwd(dr, R, sfs, sbs, proj, dproj, cos, sin, dl, hooks=()):
    S = proj.shape[1]
    NC = S // RC
    assert NC % 2 == 0

    def body(dr_ref, R_ref, sf_ref, sb_ref, q_ref, k_ref, v_ref, g_ref, dpin_ref, cos_ref, sin_ref, dl_ref,
             dp_ref, dd_ref, dR_ref, gb_ref, gf_ref, acc_ref):
        t = _decay_tables(dl_ref)

        def gate_norm_bwd(i, carry):
            rs = pl.ds(pl.multiple_of(i * ROW_TILE, ROW_TILE), ROW_TILE)
            Rv = R_ref[rs, :]
            rstd = lax.rsqrt(jnp.mean(Rv * Rv, axis=-1, keepdims=True) + EPS)
            rn = Rv * rstd
            gv = g_ref[rs, :].astype(f32)
            s = jax.nn.sigmoid(gv)
            drv = dr_ref[rs, :].astype(f32)
            dp_ref[3, rs, :] = (drv * rn * (s * (1.0 + gv * (1.0 - s)))).astype(CDT)
            drn = drv * gv * s
            dR_ref[rs, :] = (rstd * (drn - rn * jnp.mean(drn * rn, axis=-1, keepdims=True))).astype(CDT)
            return carry

        lax.fori_loop(0, S // ROW_TILE, gate_norm_bwd, 0)

        def chunk(n):
            rows = pl.ds(pl.multiple_of(n * RC, RC), RC)
            return rows, q_ref[rows, :], k_ref[rows, :], v_ref[rows, :], dR_ref[rows, :]

        def emit_kv(rows, dk, dv, final):
            if not final:
                dp_ref[1, rows, :] = dk.astype(CDT)
                dp_ref[2, rows, :] = dv.astype(CDT)
            else:
                co, si = cos_ref[rows, :], sin_ref[rows, :]
                dk = dp_ref[1, rows, :].astype(f32) + dk
                dp_ref[1, rows, :] = (_unrotate(dk, co, si) * K_SCALE).astype(CDT)
                dp_ref[2, rows, :] = (dp_ref[2, rows, :].astype(f32) + dv).astype(CDT)

        gb_ref[...] = jnp.zeros_like(gb_ref)
        gf_ref[...] = jnp.zeros_like(gf_ref)
        acc_ref[...] = jnp.zeros_like(acc_ref)
        span = t["dmat"] * jnp.abs(t["d"])
        span_f, span_b = jnp.where(t["lower"], span, 0.0), jnp.where(t["lower"], 0.0, span)

        def ascend(n, final):
            rows, qn, kn, vn, dRn = chunk(n)
            qf, kf = qn.astype(f32), kn.astype(f32)
            sc = _mm_nt(qn, kn)
            dA = _mm_nt(dRn, vn)
            prod = sc * dA
            lgf_part = jnp.sum(prod * span_f, axis=0, keepdims=True)
            lgb_part = jnp.sum(prod * span_b, axis=0, keepdims=True)
            dsc = (dA * t["dmat"]).astype(CDT)
            dq = _mm(dsc, kn)
            scT = (_mm_nt(kn, qn) * t["dmat_t"]).astype(CDT)
            dscT = (_mm_nt(vn, dRn) * t["dmat_t"]).astype(CDT)
            dk = _mm(dscT, qn)
            dv = _mm(scT, dRn)
            sfb = sf_ref[n]
            sbb = sb_ref[n]
            qdf = qf * t["fq"]
            dqdf = _mm_nt(dRn, sfb)
            dq += dqdf * t["fq"]
            lgf_row = jnp.sum(qdf * dqdf * (t["pos"] + 1.0), axis=0, keepdims=True)
            qdb = qf * t["bq"]
            dqdb = _mm_nt(dRn, sbb)
            dq += dqdb * t["bq"]
            lgb_row = jnp.sum(qdb * dqdb * (RC - t["pos"]), axis=0, keepdims=True)
            gb = gb_ref[...]
            gbb = gb.astype(CDT)
            kdb = kf * t["bk"]
            dkdb = _mm_nt(vn, gbb)
            dk += dkdb * t["bk"]
            dv += _mm(kdb.astype(CDT), gbb)
            lgb_row += jnp.sum(kdb * dkdb * t["pos"], axis=0, keepdims=True)
            lgb_row += float(RC) * t["lamb"] * jnp.sum(gb * sbb.astype(f32), axis=0, keepdims=True)
            co, si = cos_ref[rows, :], sin_ref[rows, :]
            dp_ref[0, rows, :] = _unrotate(dq, co, si).astype(CDT)
            emit_kv(rows, dk, dv, final)
            acc_ref[0:1, :] += lgf_row + lgf_part
            acc_ref[1:2, :] += lgb_row + lgb_part
            gb_ref[...] = gb * t["lamb"] + _mm_tn(qdb.astype(CDT), dRn)

        def descend(n, final):
            rows, qn, kn, vn, dRn = chunk(n)
            gf = gf_ref[...]
            gfb = gf.astype(CDT)
            kdf = kn.astype(f32) * t["fk"]
            dkdf = _mm_nt(vn, gfb)
            lgf_row = jnp.sum(kdf * dkdf * (RC - 1.0 - t["pos"]), axis=0, keepdims=True)
            lgf_row += float(RC) * t["lamf"] * jnp.sum(gf * sf_ref[n].astype(f32), axis=0, keepdims=True)
            acc_ref[0:1, :] += lgf_row
            emit_kv(rows, dkdf * t["fk"], _mm(kdf.astype(CDT), gfb), final)
            gf_ref[...] = gf * t["lamf"] + _mm_tn((qn.astype(f32) * t["fq"]).astype(CDT), dRn)

        def sweep(final):
            def step(i, carry):
                ascend(i, final)
                descend(NC - 1 - i, final)
                return carry
            return step

        lax.fori_loop(0, NC // 2, sweep(False), 0, unroll=2)
        lax.fori_loop(NC // 2, NC, sweep(True), 0, unroll=2)
        dlg = jnp.sum(acc_ref[...], axis=1, keepdims=True)
        dlogit = dlg * jax.nn.sigmoid(-dl_ref[:, 0:1])
        lane = lax.broadcasted_iota(jnp.int32, (8, 128), 1)
        dd_ref[...] = jnp.where(lane == pl.program_id(0), jnp.broadcast_to(dlogit, (8, 128)), 0.0)

    def seg(slot):
        return pl.BlockSpec((None, S, DK), lambda h: (slot, 0, h))

    head = pl.BlockSpec((S, DK), lambda h: (0, h))
    states = pl.BlockSpec((None, NC, DK, DK), lambda h: (h, 0, 0, 0))
    return _call(
        body, hooks, [dr, R, sfs, sbs, proj, proj, proj, proj, dproj, cos, sin, dl], name="ret_bwd", grid=(H,),
        out_shape=[jax.ShapeDtypeStruct(dproj.shape, CDT), jax.ShapeDtypeStruct((H, 8, 128), f32)],
        in_specs=[head, head, states, states, seg(SLOT_Q), seg(SLOT_K), seg(SLOT_VR), seg(SLOT_GR), _HBM,
                  _resident((S, DK // 2), lambda h: (0, 0)), _resident((S, DK // 2), lambda h: (0, 0)),
                  pl.BlockSpec((None, 8, DK), lambda h: (h, 0, 0))],
        out_specs=[pl.BlockSpec((4, S, DK), lambda h: (1, 0, h), pipeline_mode=pl.Buffered(1)),
                   pl.BlockSpec((None, 8, 128), lambda h: (h, 0, 0))],
        scratch_shapes=[pltpu.VMEM((S, DK), CDT),
                        pltpu.VMEM((DK, DK), f32), pltpu.VMEM((DK, DK), f32), pltpu.VMEM((8, DK), f32)],
        input_output_aliases={8: 0},
        compiler_params=_params("arbitrary"),
    )


def _proj_bwd_act(dproj, dx2, x1, nrm, wfull, hooks=(), tm=512):
    S = x1.shape[0]

    def body(dp_ref, dx2_ref, x_ref, n_ref, w_ref, dx_ref, dxh_ref, dn_ref, db_ref, acc_ref):
        @pl.when(pl.program_id(0) == 0)
        def _():
            dn_ref[...] = jnp.zeros_like(dn_ref)
            db_ref[...] = jnp.zeros_like(db_ref)

        for p in range(8):
            seg = SEG_OF_SLOT[p]
            dp = dp_ref[p]
            db_ref[seg] += jnp.sum(dp.astype(f32), axis=0, keepdims=True)
            dh = _mm_nt(dp, w_ref[seg * D:(seg + 1) * D, :])
            if p == 0:
                acc_ref[...] = dh
            else:
                acc_ref[...] += dh
        n = n_ref[...]
        r, xh, _ = _rms_fwd(x_ref[...], n)
        dx, dn = _rms_bwd(acc_ref[...], r, xh, n)
        dx = dx2_ref[...] + dx
        dx_ref[...] = dx
        dxh_ref[...] = (0.5 * dx).astype(CDT)
        dn_ref[...] += dn

    tok = pl.BlockSpec((tm, D), lambda i: (i, 0))
    return _call(
        body, hooks, [dproj, dx2, x1, nrm, wfull], name="proj_bwd_act", grid=(S // tm,),
        out_shape=[jax.ShapeDtypeStruct((S, D), f32), jax.ShapeDtypeStruct((S, D), CDT), jax.ShapeDtypeStruct((1, D), f32),
                   jax.ShapeDtypeStruct((8, 1, D), f32)],
        in_specs=[pl.BlockSpec((8, tm, D), lambda i: (0, i, 0)), tok, tok, _resident((1, D), lambda i: (0, 0)),
                  _resident((8 * D, D), lambda i: (0, 0))],
        out_specs=[tok, tok, pl.BlockSpec((1, D), lambda i: (0, 0)), pl.BlockSpec((8, 1, D), lambda i: (0, 0, 0))],
        scratch_shapes=[pltpu.VMEM((tm, D), f32)],
        compiler_params=_params("arbitrary"),
    )


def _rs_sum(name, gfulls, lands, my_c):
    n = len(gfulls)
    rows = gfulls[0].shape[0] // NDEV
    assert all(g.shape[0] == NDEV * rows for g in gfulls)

    def body(c_ref, *refs):
        for g_ref, l_ref, o_ref in zip(refs[:n], refs[n:2 * n], refs[2 * n:]):
            o_ref[...] = (g_ref[...].astype(f32) + l_ref[...].astype(f32)).astype(CDT)

    slot = pl.BlockSpec((None, rows, D), lambda k, c: (k, 0, 0))
    return pl.pallas_call(
        body, name=name,
        grid_spec=pltpu.PrefetchScalarGridSpec(
            num_scalar_prefetch=1, grid=(NCHIP,),
            in_specs=[pl.BlockSpec((rows, D), lambda k, c: (2 * k + c[0], 0))] * n + [slot] * n,
            out_specs=[slot] * n),
        out_shape=[jax.ShapeDtypeStruct((NCHIP, rows, D), CDT)] * n,
        compiler_params=_params("arbitrary"),
    )(my_c, *gfulls, *lands)


def _adamw_math(g, w, m, v):
    m2 = ADAM_B1 * m + (1.0 - ADAM_B1) * g
    v2 = ADAM_B2 * v + (1.0 - ADAM_B2) * (g * g)
    delta = -ADAM_LR * ((m2 / BC1) / (jnp.sqrt(v2 / BC2) + ADAM_EPS) + ADAM_WD * w)
    return delta, m2, v2


def _adamw_big(name, landed, w, m, v, after):
    n = len(w)
    rows = w[0].shape[0]
    tr = 256 if rows % 256 == 0 else (rows // 2 if rows > 256 else rows)
    nt = rows // tr

    def body(*refs):
        ins, outs = refs[:4 * n], refs[4 * n + 1:]
        for j in range(n):
            @pl.when(pl.program_id(0) == j)
            def _(j=j):
                l_ref, w_ref, m_ref, v_ref = ins[j], ins[n + j], ins[2 * n + j], ins[3 * n + j]
                g = l_ref[0].astype(f32)
                for k in range(1, NCHIP):
                    g = g + l_ref[k].astype(f32)
                outs[4 * j][...] = g
                outs[4 * j + 1][...], outs[4 * j + 2][...], outs[4 * j + 3][...] = _adamw_math(g, w_ref[...], m_ref[...], v_ref[...])

    def tile(j):
        return lambda jj, i: jnp.clip((jj - j) * nt + i, 0, nt - 1)

    blk = [pl.BlockSpec((tr, D), lambda jj, i, t=tile(j): (t(jj, i), 0)) for j in range(n)]
    lnd = [pl.BlockSpec((NCHIP, tr, D), lambda jj, i, t=tile(j): (0, t(jj, i), 0)) for j in range(n)]
    o = jax.ShapeDtypeStruct((rows, D), f32)
    res = pl.pallas_call(
        body, name=name, grid=(n, nt), out_shape=[o] * (4 * n),
        in_specs=lnd + blk + blk + blk + [_HBM],
        out_specs=[blk[j] for j in range(n) for _ in range(4)],
        compiler_params=_params("arbitrary", "arbitrary"),
    )(*landed, *w, *m, *v, after)
    return [res[4 * j:4 * j + 4] for j in range(n)]


ROW_FFN1_NORM, ROW_MIX_NORM, ROW_SGU_G, ROW_SGU_B, ROW_FFN2_NORM, ROW_FINAL_NORM, ROW_B_IN = 0, 1, 2, 3, 4, 5, 8
ROW_WS, ROW_BS, ROW_DECAY = 0, G * C, G * C + G * 8


def _adamw_small(ga, gb, gn1, gl, params):
    def body(ga_ref, gb_ref, gn1_ref, gl_ref, *refs):
        ins, outs = refs[:30], refs[30:]

        def total(ref, r0, n):
            g = ref[0, r0:r0 + n, :]
            for j in range(1, NDEV):
                g = g + ref[j, r0:r0 + n, :]
            return g

        def apply(i, g, rows=slice(None)):
            w, m, v = ins[3 * i][rows, :], ins[3 * i + 1][rows, :], ins[3 * i + 2][rows, :]
            outs[4 * i][rows, :] = g
            outs[4 * i + 1][rows, :], outs[4 * i + 2][rows, :], outs[4 * i + 3][rows, :] = _adamw_math(g, w, m, v)

        outs[40][...] = total(gl_ref, 0, 8)
        apply(0, total(gn1_ref, 0, 1))
        for i, r in enumerate((ROW_FFN1_NORM, ROW_MIX_NORM, ROW_SGU_G, ROW_SGU_B, ROW_FFN2_NORM, ROW_FINAL_NORM)):
            if i:
                apply(i, total(ga_ref, r, 1))
        apply(6, total(ga_ref, ROW_B_IN, 8))
        apply(7, total(gb_ref, ROW_WS, G * C))
        for gi in range(G):
            apply(8, total(gb_ref, ROW_BS + 8 * gi, 1), slice(gi, gi + 1))
        dec = total(gb_ref, ROW_DECAY, 8)
        for hh in range(1, H):
            dec = dec + total(gb_ref, ROW_DECAY + 8 * hh, 8)
        apply(9, dec)

    flat = [a for p in params for a in p]
    out_shape = [jax.ShapeDtypeStruct(p[0].shape, f32) for p in params for _ in range(4)]
    out_shape.append(jax.ShapeDtypeStruct((8, 128), f32))
    vm = pl.BlockSpec(memory_space=pltpu.VMEM)
    return pl.pallas_call(
        body, name="adamw_small", out_shape=out_shape,
        in_specs=[vm] * (4 + len(flat)), out_specs=[vm] * len(out_shape),
        compiler_params=pltpu.CompilerParams(vmem_limit_bytes=VMEM_LIMIT),
    )(ga, gb, gn1, gl, *flat)


def kernel(x, ffn1_norm, ffn1_w_gate, ffn1_w_up, ffn1_w_down, mix_norm, w_in, b_in, sgu_norm_g, sgu_norm_b, sgu_w_s, sgu_b_s, ret_decay_logit, w_branch_a, w_branch_b, w_out, ffn2_norm, ffn2_w_gate, ffn2_w_up, ffn2_w_down, final_norm, loss_target, m_ffn1_norm, m_ffn1_w_gate, m_ffn1_w_up, m_ffn1_w_down, m_mix_norm, m_w_in, m_b_in, m_sgu_norm_g, m_sgu_norm_b, m_sgu_w_s, m_sgu_b_s, m_ret_decay_logit, m_w_branch_a, m_w_branch_b, m_w_out, m_ffn2_norm, m_ffn2_w_gate, m_ffn2_w_up, m_ffn2_w_down, m_final_norm, v_ffn1_norm, v_ffn1_w_gate, v_ffn1_w_up, v_ffn1_w_down, v_mix_norm, v_w_in, v_b_in, v_sgu_norm_g, v_sgu_norm_b, v_sgu_w_s, v_sgu_b_s, v_ret_decay_logit, v_w_branch_a, v_w_branch_b, v_w_out, v_ffn2_norm, v_ffn2_w_gate, v_ffn2_w_up, v_ffn2_w_down, v_final_norm):
    args = dict(locals())
    S = x.shape[1]
    xs = x[0]
    target = loss_target[0]

    def buf_layout(name, a):
        a = a[0]
        return a.T if name in W_TRANSPOSED else a

    sh = {n: buf_layout(n, args[n]).astype(CDT) for n in W_NAMES}
    wf = {}

    b3 = b_in.reshape(8, 1, D)
    ws = sgu_w_s[0].astype(CDT)
    wst = jnp.swapaxes(sgu_w_s[0], 1, 2).astype(CDT)
    bsc = sgu_b_s[0].reshape(G, C, 1)
    dl = jnp.zeros((H, 8, DK), f32).at[:, 0:2, :].set(jnp.broadcast_to(ret_decay_logit[0].T[:, :, None], (H, 2, DK)))
    theta = ROPE_BASE ** (-jnp.arange(0, DK, 2, dtype=f32) / DK)
    ang = jnp.arange(S, dtype=f32)[:, None] * theta[None, :]
    cos, sin = jnp.cos(ang), jnp.sin(ang)
    fnorm = final_norm.reshape(1, D)

    f1 = ("ffn1_w_gate", "ffn1_w_up", "ffn1_w_down")
    f2 = ("ffn2_w_gate", "ffn2_w_up", "ffn2_w_down")
    br = ("w_branch_a", "w_branch_b", "w_out")
    for cid, names in enumerate((f1, ("w_in",), br, f2)):
        wf.update(zip(names, _sequence("ag_" + names[0], 1 + cid, NEIGHBOURS, [_ag_hook(sh[n]) for n in names])))
    x1, g1, u1, a1, hf1 = _ffn_fwd("ffn1_fwd", xs, ffn1_norm, *[wf[n] for n in f1])
    proj, h2 = _proj_fwd(x1, mix_norm, wf["w_in"], b3, cos, sin)
    a = _sgu_fwd(proj, sgu_norm_g, sgu_norm_b, ws, bsc)
    R, r, sfs, sbs = _ret_fwd(proj, dl)
    x2, ya, yb = _merge_fwd(a, r, proj, x1, *[wf[n] for n in br])
    x3, g2, u2, a2, hf2 = _ffn_fwd("ffn2_fwd", x2, ffn2_norm, *[wf[n] for n in f2])
    dx3, dyh2, d_final, loss_part = _loss_head(x3, fnorm, target)

    my_c = lax.axis_index("c").astype(jnp.int32).reshape(1)
    gw, landed, sequenced = {}, {}, []

    def d2d(*names):
        return [_rs_d2d_hook(gw[n]) for n in names]

    def behind(x, token):
        return lax.optimization_barrier((x, token))[0]

    def to_chips(names, sibs, more=()):
        parts = list(_rs_sum("rs_sum_" + names[0], [gw[n] for n in names], list(sibs), my_c))
        token = parts[0]
        if sequenced:
            parts[0] = behind(parts[0], sequenced[-1])
        hooks = [_rs_ici_hook(p) for p in parts] + ([_small_hook(list(more))] if more else [])
        reach = EVERYONE if more else CHIPS
        got = _sequence("rs_chips_" + names[0], REACH_ID[reach], reach, hooks)
        sequenced.append(got[0])
        landed.update(zip(names, got))
        return got[len(names):], token

    def ffn_bwd(tag, names, dy, dyh, x, g, u, a, h, nrm, each_alone, more=()):
        wg, wu, wd = names
        (gw[wd],) = _wgrad(tag + "_wd_grad", a, dyh)
        dg, du, sib_d = _ffn_bwd_hidden(tag + "_bwd_hidden", dyh, g, u, wf[wd], d2d(wd))
        if each_alone:
            dg = behind(dg, to_chips([wd], [sib_d])[1])
        (gw[wg],) = _wgrad(tag + "_wg_grad", dg, h)
        gw[wu], sib_g = _wgrad(tag + "_wu_grad", du, h, d2d(wg))
        if each_alone:
            (sib_u,) = _sequence("rs_sib_" + wu, SIBLING_ID, SIBLING, [_rs_d2d_hook(behind(gw[wu], sequenced[-1]))])
            sequenced.append(sib_u)
            dy = behind(dy, to_chips([wg], [sib_g])[1])
            dx, dn = _ffn_bwd_in(tag + "_bwd_in", dy, x, dg, du, nrm, wf[wg], wf[wu])
            return (dx, dn) + to_chips([wu], [sib_u], more(dn))
        dx, dn, sib_u = _ffn_bwd_in(tag + "_bwd_in", dy, x, dg, du, nrm, wf[wg], wf[wu], d2d(wu))
        return (dx, dn) + to_chips([wd, wg, wu], [sib_d, sib_g, sib_u])

    dx2, d_ffn2n, _, token = ffn_bwd("ffn2", f2, dx3, dyh2, x2, g2, u2, a2, hf2, ffn2_norm, False)
    dproj, da, dr, mix, dx2b, dya, dyb = _merge_bwd_act(behind(dx2, token), ya, yb, proj, *[wf[n] for n in br])
    dproj, d_ws, d_bs, d_gn, d_bn = _sgu_bwd(da, proj, dproj, sgu_norm_g, sgu_norm_b, ws, wst, bsc)
    dproj, d_dec = _ret_bwd(dr, R, sfs, sbs, proj, dproj, cos, sin, dl)
    (gw["w_in"],) = _tn("win_grad", h2, dproj, _seg_of_slot)
    gw["w_out"], sib_win = _wgrad("wo_grad", mix, dx2b, d2d("w_in"))
    small_sgu = jnp.concatenate([d_ws.reshape(G * C, C), d_bs.reshape(G * 8, C)], axis=0)
    (g_sgu, gl), token = to_chips(["w_in"], [sib_win], [small_sgu, loss_part])
    (gw["w_branch_a"],) = _wgrad("wa_grad", a, behind(dya, token))
    (gw["w_branch_b"],) = _wgrad("wb_grad", r, dyb)
    dx1, dyh1, d_mixn, d_bin, *sib_br = _proj_bwd_act(behind(dproj, token), dx2, x1, mix_norm, wf["w_in"], d2d(*br))
    small_a = jnp.concatenate([jnp.zeros((1, D), f32), d_mixn, d_gn, d_bn, d_ffn2n, d_final, jnp.zeros((2, D), f32),
                               d_bin.reshape(8, D)], axis=0)
    (ga, g_dec), token = to_chips(list(br), sib_br, [small_a, d_dec.reshape(H * 8, 128)])
    gb = jnp.concatenate([g_sgu, g_dec], axis=1)
    dxs, d_ffn1n, (gn1,), token = ffn_bwd("ffn1", f1, dx1, behind(dyh1, token), xs, g1, u1, a1, hf1, ffn1_norm, True,
                                          lambda dn: [dn])

    out = {"grad_x": dxs[None]}

    def native(name, a):
        a = a.T if name in W_TRANSPOSED else a
        return a[None]

    after = token
    for names in (f2, ("w_in",), br, (f1[2], f1[0], f1[1])):
        res = _adamw_big("adamw_" + names[0], [landed[n] for n in names], [buf_layout(n, args[n]) for n in names],
                         [buf_layout(n, args["m_" + n]) for n in names], [buf_layout(n, args["v_" + n]) for n in names], after)
        after = res[-1][0]
        for n, four in zip(names, res):
            for pre, val in zip(("grad_", "delta_", "new_m_", "new_v_"), four):
                out[pre + n] = native(n, val)

    def pad_decay(a):
        return jnp.zeros((8, 128), f32).at[0:2, 0:H].set(a[0])

    small = [
        ("ffn1_norm", lambda a: a, lambda a: a), ("mix_norm", lambda a: a, lambda a: a),
        ("sgu_norm_g", lambda a: a, lambda a: a), ("sgu_norm_b", lambda a: a, lambda a: a),
        ("ffn2_norm", lambda a: a, lambda a: a),
        ("final_norm", lambda a: a.reshape(1, D), lambda a: a.reshape(D)),
        ("b_in", lambda a: a.reshape(8, D), lambda a: a.reshape(1, 8 * D)),
        ("sgu_w_s", lambda a: a.reshape(G * C, C), lambda a: a.reshape(1, G, C, C)),
        ("sgu_b_s", lambda a: a[0], lambda a: a[None]),
        ("ret_decay_logit", pad_decay, lambda a: a[None, 0:2, 0:H]),
    ]
    res = _adamw_small(ga, gb, gn1, gl, [(to(args[n]), to(args["m_" + n]), to(args["v_" + n])) for n, to, _ in small])
    out["loss"] = res[40][0, 0]
    for i, (n, _, back) in enumerate(small):
        for j, pre in enumerate(("grad_", "delta_", "new_m_", "new_v_")):
            out[pre + n] = back(res[4 * i + j])

    weights = ("ffn1_norm", "ffn1_w_gate", "ffn1_w_up", "ffn1_w_down", "mix_norm", "w_in", "b_in", "sgu_norm_g",
               "sgu_norm_b", "sgu_w_s", "sgu_b_s", "ret_decay_logit", "w_branch_a", "w_branch_b", "w_out", "ffn2_norm",
               "ffn2_w_gate", "ffn2_w_up", "ffn2_w_down", "final_norm")
    return (out["loss"], out["grad_x"], *[out["grad_" + n] for n in weights], *[out["delta_" + n] for n in weights],
            *[out["new_m_" + n] for n in weights], *[out["new_v_" + n] for n in weights])
```

```python
import math

import jax
import jax.numpy as jnp
from jax import lax
from jax.experimental import pallas as pl
from jax.experimental.pallas import tpu as pltpu
from jax.experimental.pallas import tpu_sc as plsc

f32 = jnp.float32
CDT = jnp.bfloat16

D = 1024
F = 2816
C = 128
RC = 256
H = 4
DK = 256
G = 4
NDEV = 8
NCHIP = 4
EPS = 1e-6
ROPE_BASE = 10000.0
FT = 256
V7X_VMEM_BYTES = 64 * 1024 * 1024
VMEM_LIMIT = V7X_VMEM_BYTES - 8 * 1024 * 1024

ADAM_LR, ADAM_B1, ADAM_B2, ADAM_EPS, ADAM_WD, ADAM_STEP = 0.001, 0.9, 0.999, 1e-08, 0.01, 10
BC1 = 1.0 - ADAM_B1 ** ADAM_STEP
BC2 = 1.0 - ADAM_B2 ** ADAM_STEP

W_ROWS = dict(ffn1_w_gate=352, ffn1_w_up=352, ffn1_w_down=352, w_in=1024, w_branch_a=128, w_branch_b=128, w_out=128,
              ffn2_w_gate=352, ffn2_w_up=352, ffn2_w_down=352)
W_NAMES = tuple(W_ROWS)
W_TRANSPOSED = ("ffn1_w_gate", "ffn1_w_up", "ffn2_w_gate", "ffn2_w_up")

SLOT_U, SLOT_V, SLOT_GA, SLOT_GB, SLOT_Q, SLOT_K, SLOT_VR, SLOT_GR = range(8)


SEG_OF_SLOT = (0, 1, 6, 7, 2, 3, 4, 5)


def _seg_of_slot(p):
    return jnp.where(p < 2, p, jnp.where(p < 4, p + 4, p - 2))


def _mm(a, b):
    return jnp.dot(a, b, preferred_element_type=f32)


def _mm_nt(a, b):
    return lax.dot_general(a, b, (((1,), (1,)), ((), ())), preferred_element_type=f32)


def _mm_tn(a, b):
    return lax.dot_general(a, b, (((0,), (0,)), ((), ())), preferred_element_type=f32)


def _params(*sem):
    return pltpu.CompilerParams(dimension_semantics=sem, vmem_limit_bytes=VMEM_LIMIT)


def _resident(shape, index_map):
    return pl.BlockSpec(shape, index_map, pipeline_mode=pl.Buffered(1))


def _gelu(x):
    return 0.5 * x * (1.0 + lax.erf(x * (1.0 / math.sqrt(2.0))))


def _gelu_and_grad(x):
    cdf = 0.5 * (1.0 + lax.erf(x * (1.0 / math.sqrt(2.0))))
    return x * cdf, cdf + x * jnp.exp(-0.5 * x * x) * (1.0 / math.sqrt(2.0 * math.pi))


def _rms_fwd(x, n):
    r = lax.rsqrt(jnp.mean(x * x, axis=-1, keepdims=True) + EPS)
    xh = x * r
    return r, xh, xh * n


def _rms_bwd(dh, r, xh, n):
    dxh = dh * n
    dx = r * (dxh - xh * jnp.mean(dxh * xh, axis=-1, keepdims=True))
    return dx, jnp.sum(dh * xh, axis=0, keepdims=True)


MESH_ID = pl.DeviceIdType.MESH
_HBM = pl.BlockSpec(memory_space=pltpu.HBM)


def _my_place():
    return lax.axis_index("x"), lax.axis_index("y"), lax.axis_index("c")


def _ici_peers(x, y, c):
    return [((1 - x, y, c), 2 * (1 - x) + y), ((x, 1 - y, c), 2 * x + 1 - y), ((1 - x, 1 - y, c), 2 * (1 - x) + 1 - y)]


class _Hook:
    def __init__(self, operands, out_shapes, n_remote, n_local, start, finish, relay=None):
        self.operands, self.out_shapes = list(operands), list(out_shapes)
        self.n_remote, self.n_local, self.start, self.finish = n_remote, n_local, start, finish
        self.relay = relay or (lambda *a: None)


def _call(body, hooks, operands, *, in_specs, out_specs, out_shape, grid=None, scratch_shapes=(), **kw):
    hooks = tuple(hooks)
    n_in, n_out, n_scr = len(in_specs), len(out_shape), len(scratch_shapes)
    h_ops = [a for h in hooks for a in h.operands]
    h_outs = [s for h in hooks for s in h.out_shapes]
    h_sems = [pltpu.SemaphoreType.DMA((n,)) for h in hooks for n in (h.n_remote, h.n_remote, max(h.n_local, 1))]

    def wrapped(*refs):
        ins, hin = refs[:n_in], refs[n_in:n_in + len(h_ops)]
        o0 = n_in + len(h_ops)
        outs, hout = refs[o0:o0 + n_out], refs[o0 + n_out:o0 + n_out + len(h_outs)]
        s0 = o0 + n_out + len(h_outs)
        scr, hsem = refs[s0:s0 + n_scr], refs[s0 + n_scr:]

        def run(phase):
            ip = op = 0
            for i, h in enumerate(hooks):
                ssem, rsem, lsem = hsem[3 * i:3 * i + 3]

                def rcopy(k, src, dst, dev, ssem=ssem, rsem=rsem):
                    return pltpu.make_async_remote_copy(src_ref=src, dst_ref=dst, send_sem=ssem.at[k], recv_sem=rsem.at[k],
                                                        device_id=dev, device_id_type=MESH_ID)

                def lcopy(k, src, dst, lsem=lsem):
                    return pltpu.make_async_copy(src, dst, lsem.at[k])

                getattr(h, phase)(hin[ip:ip + len(h.operands)], hout[op:op + len(h.out_shapes)], rcopy, lcopy)
                ip += len(h.operands)
                op += len(h.out_shapes)

        def at_edge(phase, last):
            if not hooks:
                return
            if grid is None:
                run(phase)
                return
            cond = None
            for ax, n in enumerate(grid):
                here = pl.program_id(ax) == (n - 1 if last else 0)
                cond = here if cond is None else cond & here
            pl.when(cond)(lambda: run(phase))

        at_edge("start", False)
        at_edge("relay", True)
        body(*ins, *outs, *scr)
        at_edge("finish", True)

    if grid is not None:
        kw["grid"] = grid
    return list(pl.pallas_call(
        wrapped, out_shape=list(out_shape) + h_outs, in_specs=list(in_specs) + [_HBM] * len(h_ops),
        out_specs=list(out_specs) + [_HBM] * len(h_outs), scratch_shapes=list(scratch_shapes) + h_sems, **kw,
    )(*operands, *h_ops))


def _rows(ref, start, n):
    return ref.at[pl.ds(start, n), :]


def _ag_hook(shard):
    rows = shard.shape[0]
    half = rows // 2
    assert half % 16 == 0

    def place():
        x, y, c = _my_place()
        devs = dict(sib=(x, y, 1 - c), xn=(1 - x, y, c), yn=(x, 1 - y, c))
        chips = dict(me=2 * x + y, xn=2 * (1 - x) + y, yn=2 * x + 1 - y, dg=2 * (1 - x) + 1 - y)
        return c, devs, chips

    def block(full, chip, c):
        return _rows(full, (2 * chip + c) * rows, rows)

    def halfblock(full, chip, c, upper):
        return _rows(full, (2 * chip + c) * rows + upper * half, half)

    def start(ins, outs, rcopy, lcopy):
        c, devs, chips = place()
        src, dst = ins[0], block(outs[0], chips["me"], c)
        lcopy(0, src, dst).start()
        for k, to in enumerate(("sib", "xn", "yn")):
            rcopy(k, src, dst, devs[to]).start()

    def relay(ins, outs, rcopy, lcopy):
        c, devs, chips = place()
        full = outs[0]
        blk = block(full, chips["xn"], c)
        rcopy(1, blk, blk, devs["xn"]).wait_recv()
        low = halfblock(full, chips["xn"], c, 0)
        rcopy(3, low, low, devs["yn"]).start()
        rcopy(5, blk, blk, devs["sib"]).start()
        blk = block(full, chips["yn"], c)
        rcopy(2, blk, blk, devs["yn"]).wait_recv()
        up = halfblock(full, chips["yn"], c, 1)
        rcopy(4, up, up, devs["xn"]).start()
        rcopy(6, blk, blk, devs["sib"]).start()
        low, up = halfblock(full, chips["dg"], c, 0), halfblock(full, chips["dg"], c, 1)
        rcopy(3, low, low, devs["yn"]).wait_recv()
        rcopy(4, up, up, devs["xn"]).wait_recv()
        blk = block(full, chips["dg"], c)
        rcopy(7, blk, blk, devs["sib"]).start()

    def finish(ins, outs, rcopy, lcopy):
        c, devs, chips = place()
        full, sib = outs[0], devs["sib"]
        for k, chip in ((0, "me"), (5, "xn"), (6, "yn"), (7, "dg")):
            theirs = block(full, chips[chip], 1 - c)
            rcopy(k, theirs, theirs, sib).wait_recv()
            mine = block(full, chips[chip], c)
            if k:
                rcopy(k, mine, mine, sib).wait_send()
        src, dst = ins[0], block(full, chips["me"], c)
        lcopy(0, src, dst).wait()
        for k, to in enumerate(("sib", "xn", "yn")):
            rcopy(k, src, dst, devs[to]).wait_send()
        low, up = halfblock(full, chips["xn"], c, 0), halfblock(full, chips["yn"], c, 1)
        rcopy(3, low, low, devs["yn"]).wait_send()
        rcopy(4, up, up, devs["xn"]).wait_send()

    return _Hook([shard], [jax.ShapeDtypeStruct((NDEV * rows, D), shard.dtype)], 8, 1, start, finish, relay)


SIBLING, CHIPS, NEIGHBOURS, EVERYONE = "sibling", "chips", "sibling and the two neighbour chips", "everyone"
REACH_ID = {CHIPS: 6, EVERYONE: 7}
SIBLING_ID = 5


def _sequence(name, collective_id, reach, hooks):
    ins = [[jax.new_ref(a, memory_space=pltpu.MemorySpace.HBM) for a in h.operands] for h in hooks]
    outs = [[jax.empty_ref(s, memory_space=pltpu.MemorySpace.HBM) for s in h.out_shapes] for h in hooks]
    sems = tuple(pltpu.SemaphoreType.DMA((n,)) for h in hooks for n in (h.n_remote, h.n_remote, max(h.n_local, 1)))

    @pl.kernel(mesh=plsc.ScalarSubcoreMesh(axis_name="sequencer", num_cores=1), name=name, scratch_types=sems,
               compiler_params=pltpu.CompilerParams(collective_id=collective_id))
    def launch(*sem_refs):
        x, y, c = _my_place()
        chips = [dev for dev, _ in _ici_peers(x, y, c)]
        others = [(1 - x if dx else x, 1 - y if dy else y, 1 - c if dc else c)
                  for dx in range(2) for dy in range(2) for dc in range(2) if dx + dy + dc]
        devs = {SIBLING: [(x, y, 1 - c)], CHIPS: chips, NEIGHBOURS: [(x, y, 1 - c), (1 - x, y, c), (x, 1 - y, c)],
                EVERYONE: others}[reach]
        barrier = pltpu.get_barrier_semaphore()
        for dev in devs:
            pl.semaphore_signal(barrier, inc=1, device_id=dev, device_id_type=MESH_ID)
        pl.semaphore_wait(barrier, len(devs))
        for phase in ("start", "relay", "finish"):
            for i, h in enumerate(hooks):
                ssem, rsem, lsem = sem_refs[3 * i:3 * i + 3]

                def rcopy(k, src, dst, dev, ssem=ssem, rsem=rsem):
                    return pltpu.make_async_remote_copy(src_ref=src, dst_ref=dst, send_sem=ssem.at[k], recv_sem=rsem.at[k],
                                                        device_id=dev, device_id_type=MESH_ID)

                def lcopy(k, src, dst, lsem=lsem):
                    return pltpu.make_async_copy(src, dst, lsem.at[k])

                getattr(h, phase)(ins[i], outs[i], rcopy, lcopy)

    launch()
    return [o[...] for os in outs for o in os]


def _rs_d2d_hook(gfull):
    rows = gfull.shape[0] // NDEV

    def pairs(g, land):
        x, y, c = _my_place()
        return (x, y, 1 - c), [(k, _rows(g, (2 * k + 1 - c) * rows, rows), land.at[k]) for k in range(NCHIP)]

    def start(ins, outs, rcopy, lcopy):
        sib, cps = pairs(ins[0], outs[0])
        for i, src, dst in cps:
            rcopy(i, src, dst, sib).start()

    def finish(ins, outs, rcopy, lcopy):
        sib, cps = pairs(ins[0], outs[0])
        for i, src, dst in cps:
            rcopy(i, dst, dst, sib).wait_recv()
        for i, src, dst in cps:
            rcopy(i, src, dst, sib).wait_send()

    return _Hook([gfull], [jax.ShapeDtypeStruct((NCHIP, rows, D), gfull.dtype)], NCHIP, 0, start, finish)


def _rs_ici_hook(part):
    def start(ins, outs, rcopy, lcopy):
        x, y, c = _my_place()
        mychip = 2 * x + y
        lcopy(0, ins[0].at[mychip], outs[0].at[mychip]).start()
        for j, (dev, chip) in enumerate(_ici_peers(x, y, c)):
            rcopy(j, ins[0].at[chip], outs[0].at[mychip], dev).start()

    def finish(ins, outs, rcopy, lcopy):
        x, y, c = _my_place()
        mychip = 2 * x + y
        peers = _ici_peers(x, y, c)
        for j, (dev, chip) in enumerate(peers):
            rcopy(j, outs[0].at[chip], outs[0].at[chip], dev).wait_recv()
        for j, (dev, chip) in enumerate(peers):
            rcopy(j, ins[0].at[chip], outs[0].at[mychip], dev).wait_send()
        lcopy(0, ins[0].at[mychip], outs[0].at[mychip]).wait()

    return _Hook([part], [jax.ShapeDtypeStruct(part.shape, part.dtype)], 3, 1, start, finish)


def _small_hook(arrays):
    n = len(arrays)

    def peers():
        x, y, c = _my_place()
        out = []
        for dx in range(2):
            for dy in range(2):
                for dc in range(2):
                    if dx + dy + dc:
                        px, py, pc = (1 - x if dx else x), (1 - y if dy else y), (1 - c if dc else c)
                        out.append(((px, py, pc), 4 * px + 2 * py + pc))
        return 4 * x + 2 * y + c, out

    def start(ins, outs, rcopy, lcopy):
        me, ps = peers()
        for t in range(n):
            lcopy(t, ins[t], outs[t].at[me]).start()
            for i, (dev, _) in enumerate(ps):
                rcopy(n * i + t, ins[t], outs[t].at[me], dev).start()

    def finish(ins, outs, rcopy, lcopy):
        me, ps = peers()
        for t in range(n):
            for i, (dev, peer) in enumerate(ps):
                rcopy(n * i + t, outs[t].at[peer], outs[t].at[peer], dev).wait_recv()
            for i, (dev, _) in enumerate(ps):
                rcopy(n * i + t, ins[t], outs[t].at[me], dev).wait_send()
            lcopy(t, ins[t], outs[t].at[me]).wait()

    return _Hook(arrays, [jax.ShapeDtypeStruct((NDEV,) + a.shape, a.dtype) for a in arrays], 7 * n, n, start, finish)


def _wblock(w):
    return _resident(w.shape, lambda *_: (0, 0))


def _ffn_fwd(name, x, nrm, wg, wu, wd, hooks=(), tm=512):
    S = x.shape[0]

    def body(x_ref, n_ref, wg_ref, wu_ref, wd_ref, y_ref, g_ref, u_ref, a_ref, h_ref, acc_ref):
        xv = x_ref[...]
        _, _, h = _rms_fwd(xv, n_ref[...])
        h = h.astype(CDT)
        h_ref[...] = h
        for ci in range(F // FT):
            sl = slice(ci * FT, (ci + 1) * FT)
            g = _mm_nt(h, wg_ref[sl, :])
            u = _mm_nt(h, wu_ref[sl, :])
            g_ref[:, sl] = g.astype(CDT)
            u_ref[:, sl] = u.astype(CDT)
            a = (g * jax.nn.sigmoid(g) * u).astype(CDT)
            a_ref[:, sl] = a
            o = _mm(a, wd_ref[sl, :])
            if ci == 0:
                acc_ref[...] = o
            else:
                acc_ref[...] += o
        y_ref[...] = xv + 0.5 * acc_ref[...]

    tok = pl.BlockSpec((tm, D), lambda i: (i, 0))
    hid = pl.BlockSpec((tm, F), lambda i: (i, 0))
    hidden = jax.ShapeDtypeStruct((S, F), CDT)
    return _call(
        body, hooks, [x, nrm, wg, wu, wd], name=name, grid=(S // tm,),
        out_shape=[jax.ShapeDtypeStruct((S, D), f32), hidden, hidden, hidden, jax.ShapeDtypeStruct((S, D), CDT)],
        in_specs=[tok, _resident((1, D), lambda i: (0, 0)), _wblock(wg), _wblock(wu), _wblock(wd)],
        out_specs=[tok, hid, hid, hid, tok],
        scratch_shapes=[pltpu.VMEM((tm, D), f32)],
        compiler_params=_params("arbitrary"),
    )


def _proj_fwd(x1, nrm, wfull, b3, cos, sin, hooks=(), tm=512):
    S = x1.shape[0]

    def body(x_ref, n_ref, w_ref, b_ref, cos_ref, sin_ref, p_ref, h_ref):
        _, _, h = _rms_fwd(x_ref[...], n_ref[...])
        h = h.astype(CDT)
        h_ref[...] = h
        for p in range(8):
            seg = SEG_OF_SLOT[p]
            z = _mm(h, w_ref[seg * D:(seg + 1) * D, :]) + b_ref[seg]
            if p in (SLOT_Q, SLOT_K):
                co, si = cos_ref[...], sin_ref[...]
                for hh in range(H):
                    cs = slice(hh * DK, (hh + 1) * DK)
                    zr = _rotate(z[:, cs], co, si)
                    p_ref[p, :, cs] = (zr * K_SCALE if p == SLOT_K else zr).astype(CDT)
            else:
                p_ref[p] = z.astype(CDT)

    tab = pl.BlockSpec((tm, DK // 2), lambda i: (i, 0))
    return _call(
        body, hooks, [x1, nrm, wfull, b3, cos, sin], name="proj_fwd", grid=(S // tm,),
        out_shape=[jax.ShapeDtypeStruct((8, S, D), CDT), jax.ShapeDtypeStruct((S, D), CDT)],
        in_specs=[pl.BlockSpec((tm, D), lambda i: (i, 0)), _resident((1, D), lambda i: (0, 0)),
                  _resident((8 * D, D), lambda i: (0, 0)), _resident((8, 1, D), lambda i: (0, 0, 0)), tab, tab],
        out_specs=[pl.BlockSpec((8, tm, D), lambda i: (0, i, 0)), pl.BlockSpec((tm, D), lambda i: (i, 0))],
        compiler_params=_params("arbitrary"),
    )


def _sgu_norm(va, gn, bn):
    mu = jnp.mean(va, axis=-1, keepdims=True)
    xc = va - mu
    rstd = lax.rsqrt(jnp.mean(xc * xc, axis=-1, keepdims=True) + EPS)
    vhat = xc * rstd
    return rstd, vhat, vhat * gn + bn


def _sgu_fwd(proj, gn, bn, ws, bsc, tm=512):
    S = proj.shape[1]
    GW = D // G

    def body(p_ref, gn_ref, bn_ref, ws_ref, bs_ref, a_ref):
        ua = _gelu(p_ref[0].astype(f32))
        va = _gelu(p_ref[1].astype(f32))
        _, _, vn = _sgu_norm(va, gn_ref[...], bn_ref[...])
        vn = vn.astype(CDT)
        for ch in range(tm // C):
            rs = slice(ch * C, (ch + 1) * C)
            for gi in range(G):
                cs = slice(gi * GW, (gi + 1) * GW)
                s = _mm(ws_ref[gi], vn[rs, cs]) + bs_ref[gi]
                a_ref[rs, cs] = (ua[rs, cs] * s).astype(CDT)

    return pl.pallas_call(
        body, name="sgu_fwd", grid=(S // tm,),
        out_shape=jax.ShapeDtypeStruct((S, D), CDT),
        in_specs=[pl.BlockSpec((2, tm, D), lambda i: (0, i, 0)), _resident((1, D), lambda i: (0, 0)),
                  _resident((1, D), lambda i: (0, 0)), _resident((G, C, C), lambda i: (0, 0, 0)),
                  _resident((G, C, 1), lambda i: (0, 0, 0))],
        out_specs=pl.BlockSpec((tm, D), lambda i: (i, 0)),
        compiler_params=_params("arbitrary"),
    )(proj, gn, bn, ws, bsc)


def _decay_tables(dl_ref):
    lg = jax.nn.log_sigmoid(dl_ref[0:2, :])
    lgf, lgb = lg[0:1, :], lg[1:2, :]
    assert RC <= DK
    ri = lax.broadcasted_iota(jnp.int32, (RC, RC), 0)
    ci = lax.broadcasted_iota(jnp.int32, (RC, RC), 1)
    d = (ri - ci).astype(f32)
    lower = d >= 0
    dmat = jnp.where(lower, jnp.exp(d * lgf[:, :RC]), jnp.exp(-d * lgb[:, :RC]))
    dmat_t = jnp.where(d <= 0, jnp.exp(-d * lgf[:, :RC]), jnp.exp(d * lgb[:, :RC]))
    pos = lax.broadcasted_iota(jnp.int32, (RC, DK), 0).astype(f32)
    t = dict(
        lgf=lgf, lgb=lgb, d=d, lower=lower, dmat=dmat, dmat_t=dmat_t, pos=pos,
        fq=jnp.exp((pos + 1.0) * lgf), fk=jnp.exp((RC - 1.0 - pos) * lgf),
        bq=jnp.exp((RC - pos) * lgb), bk=jnp.exp(pos * lgb),
        lamf=jnp.exp(float(RC) * lgf), lamb=jnp.exp(float(RC) * lgb),
    )
    return t


def _rotate(t, co, si):
    t1, t2 = t[:, :DK // 2], t[:, DK // 2:]
    return jnp.concatenate([t1 * co - t2 * si, t2 * co + t1 * si], axis=-1)


def _unrotate(t, co, si):
    t1, t2 = t[:, :DK // 2], t[:, DK // 2:]
    return jnp.concatenate([t1 * co + t2 * si, t2 * co - t1 * si], axis=-1)


K_SCALE = DK ** -0.5
ROW_TILE = 256


def _ret_fwd(proj, dl, hooks=()):
    S = proj.shape[1]
    NC = S // RC

    def body(q_ref, k_ref, v_ref, g_ref, dl_ref, R_ref, r_ref, sfs_ref, sbs_ref, rb_ref, sf_ref, sb_ref):
        t = _decay_tables(dl_ref)

        def chunk(n):
            rows = pl.ds(pl.multiple_of(n * RC, RC), RC)
            return rows, q_ref[rows, :], k_ref[rows, :], v_ref[rows, :]

        sf_ref[...] = jnp.zeros_like(sf_ref)
        sb_ref[...] = jnp.zeros_like(sb_ref)

        def step(i, carry):
            rows, qn, kn, vn = chunk(i)
            sc = _mm_nt(qn, kn) * t["dmat"]
            out = _mm(sc.astype(CDT), vn)
            sf = sf_ref[...]
            sfb = sf.astype(CDT)
            sfs_ref[i] = sfb
            R_ref[rows, :] = out + _mm((qn.astype(f32) * t["fq"]).astype(CDT), sfb)
            sf_ref[...] = sf * t["lamf"] + _mm_tn((kn.astype(f32) * t["fk"]).astype(CDT), vn)
            m = NC - 1 - i
            rows, qn, kn, vn = chunk(m)
            sb = sb_ref[...]
            sbb = sb.astype(CDT)
            sbs_ref[m] = sbb
            rb_ref[rows, :] = _mm((qn.astype(f32) * t["bq"]).astype(CDT), sbb)
            sb_ref[...] = sb * t["lamb"] + _mm_tn((kn.astype(f32) * t["bk"]).astype(CDT), vn)
            return carry

        lax.fori_loop(0, NC, step, 0, unroll=4)

        def finish(i, carry):
            rs = pl.ds(pl.multiple_of(i * ROW_TILE, ROW_TILE), ROW_TILE)
            R = R_ref[rs, :] + rb_ref[rs, :]
            R_ref[rs, :] = R
            rn = R * lax.rsqrt(jnp.mean(R * R, axis=-1, keepdims=True) + EPS)
            g = g_ref[rs, :].astype(f32)
            r_ref[rs, :] = (rn * g * jax.nn.sigmoid(g)).astype(CDT)
            return carry

        lax.fori_loop(0, S // ROW_TILE, finish, 0)

    def seg(slot):
        return pl.BlockSpec((None, S, DK), lambda h: (slot, 0, h))

    states = jax.ShapeDtypeStruct((H, NC, DK, DK), CDT)
    state_blk = pl.BlockSpec((None, NC, DK, DK), lambda h: (h, 0, 0, 0))
    return _call(
        body, hooks, [proj, proj, proj, proj, dl], name="ret_fwd", grid=(H,),
        out_shape=[jax.ShapeDtypeStruct((S, H * DK), f32), jax.ShapeDtypeStruct((S, H * DK), CDT), states, states],
        in_specs=[seg(SLOT_Q), seg(SLOT_K), seg(SLOT_VR), seg(SLOT_GR), pl.BlockSpec((None, 8, DK), lambda h: (h, 0, 0))],
        out_specs=[pl.BlockSpec((S, DK), lambda h: (0, h)), pl.BlockSpec((S, DK), lambda h: (0, h)), state_blk, state_blk],
        scratch_shapes=[pltpu.VMEM((S, DK), f32), pltpu.VMEM((DK, DK), f32), pltpu.VMEM((DK, DK), f32)],
        compiler_params=_params("arbitrary"),
    )


def _merge_fwd(a, r, proj, x1, wa, wb, wo, hooks=(), tm=512):
    S = x1.shape[0]

    def body(a_ref, r_ref, gt_ref, x_ref, wa_ref, wb_ref, wo_ref, x2_ref, ya_ref, yb_ref):
        ya = _mm(a_ref[...], wa_ref[...])
        yb = _mm(r_ref[...], wb_ref[...])
        ya_ref[...] = ya.astype(CDT)
        yb_ref[...] = yb.astype(CDT)
        mix = jax.nn.sigmoid(gt_ref[0].astype(f32)) * ya + jax.nn.sigmoid(gt_ref[1].astype(f32)) * yb
        x2_ref[...] = x_ref[...] + _mm(mix.astype(CDT), wo_ref[...])

    tok = pl.BlockSpec((tm, D), lambda i: (i, 0))
    return _call(
        body, hooks, [a, r, proj, x1, wa, wb, wo], name="merge_fwd", grid=(S // tm,),
        out_shape=[jax.ShapeDtypeStruct((S, D), f32), jax.ShapeDtypeStruct((S, D), CDT), jax.ShapeDtypeStruct((S, D), CDT)],
        in_specs=[tok, tok, pl.BlockSpec((2, tm, D), lambda i: (SLOT_GA // 2, i, 0)), tok,
                  _wblock(wa), _wblock(wb), _wblock(wo)],
        out_specs=[tok, tok, tok],
        compiler_params=_params("arbitrary"),
    )


def _loss_head(x3, fn, target, tm=512):
    S = x3.shape[0]

    def body(x_ref, n_ref, t_ref, dx_ref, dxh_ref, dn_ref, l_ref):
        n = n_ref[...]
        r, xh, y = _rms_fwd(x_ref[...], n)
        e = y - t_ref[...]
        dy = e * (1.0 / D)
        dx, dn = _rms_bwd(dy, r, xh, n)
        dx_ref[...] = dx
        dxh_ref[...] = (0.5 * dx).astype(CDT)
        part = 0.5 * jnp.sum(jnp.sum(e * e, axis=-1, keepdims=True), axis=0, keepdims=True) * (1.0 / D)

        @pl.when(pl.program_id(0) == 0)
        def _():
            dn_ref[...] = jnp.zeros_like(dn_ref)
            l_ref[...] = jnp.zeros_like(l_ref)

        dn_ref[...] += dn
        l_ref[...] += jnp.broadcast_to(part, l_ref.shape)

    tok = pl.BlockSpec((tm, D), lambda i: (i, 0))
    return pl.pallas_call(
        body, name="loss_head", grid=(S // tm,),
        out_shape=[jax.ShapeDtypeStruct((S, D), f32), jax.ShapeDtypeStruct((S, D), CDT), jax.ShapeDtypeStruct((1, D), f32),
                   jax.ShapeDtypeStruct((8, 128), f32)],
        in_specs=[tok, _resident((1, D), lambda i: (0, 0)), tok],
        out_specs=[tok, tok, pl.BlockSpec((1, D), lambda i: (0, 0)), pl.BlockSpec((8, 128), lambda i: (0, 0))],
        compiler_params=_params("arbitrary"),
    )(x3, fn, target)


def _ffn_bwd_hidden(name, dyh, g, u, wd, hooks=(), tm=512):
    S = dyh.shape[0]

    def body(dyh_ref, g_ref, u_ref, wd_ref, dg_ref, du_ref):
        dyh = dyh_ref[...]
        for ci in range(F // FT):
            sl = slice(ci * FT, (ci + 1) * FT)
            da = _mm_nt(dyh, wd_ref[sl, :])
            gv = g_ref[:, sl].astype(f32)
            uv = u_ref[:, sl].astype(f32)
            s = jax.nn.sigmoid(gv)
            silu = gv * s
            du_ref[:, sl] = (da * silu).astype(CDT)
            dg_ref[:, sl] = (da * uv * (s + silu - silu * s)).astype(CDT)

    hid = pl.BlockSpec((tm, F), lambda i: (i, 0))
    hidden = jax.ShapeDtypeStruct((S, F), CDT)
    return _call(
        body, hooks, [dyh, g, u, wd], name=name, grid=(S // tm,), out_shape=[hidden, hidden],
        in_specs=[pl.BlockSpec((tm, D), lambda i: (i, 0)), hid, hid, _wblock(wd)], out_specs=[hid, hid],
        compiler_params=_params("arbitrary"),
    )


def _ffn_bwd_in(name, dy, x, dg, du, nrm, wg, wu, hooks=(), tm=512):
    S = x.shape[0]

    def body(dy_ref, x_ref, dg_ref, du_ref, n_ref, wg_ref, wu_ref, dx_ref, dn_ref, acc_ref):
        n = n_ref[...]
        r, xh, _ = _rms_fwd(x_ref[...], n)
        for ci in range(F // FT):
            sl = slice(ci * FT, (ci + 1) * FT)
            dh = _mm(dg_ref[:, sl], wg_ref[sl, :]) + _mm(du_ref[:, sl], wu_ref[sl, :])
            if ci == 0:
                acc_ref[...] = dh
            else:
                acc_ref[...] += dh
        dx, dn = _rms_bwd(acc_ref[...], r, xh, n)
        dx_ref[...] = dy_ref[...] + dx

        @pl.when(pl.program_id(0) == 0)
        def _():
            dn_ref[...] = jnp.zeros_like(dn_ref)

        dn_ref[...] += dn

    tok = pl.BlockSpec((tm, D), lambda i: (i, 0))
    hid = pl.BlockSpec((tm, F), lambda i: (i, 0))
    return _call(
        body, hooks, [dy, x, dg, du, nrm, wg, wu], name=name, grid=(S // tm,),
        out_shape=[jax.ShapeDtypeStruct((S, D), f32), jax.ShapeDtypeStruct((1, D), f32)],
        in_specs=[tok, tok, hid, hid, _resident((1, D), lambda i: (0, 0)), _wblock(wg), _wblock(wu)],
        out_specs=[tok, pl.BlockSpec((1, D), lambda i: (0, 0))],
        scratch_shapes=[pltpu.VMEM((tm, D), f32)],
        compiler_params=_params("arbitrary"),
    )


TN_ROWS = 512


def _tn(name, xs, ys, block_of, hooks=()):
    S, M = xs.shape
    B = ys.shape[0]
    tr = TN_ROWS if M % TN_ROWS == 0 else M // 2
    assert M % tr == 0 and tr % 128 == 0
    nt = M // tr

    def body(x_ref, y_ref, o_ref):
        o_ref[...] = _mm_tn(x_ref[...], y_ref[...]).astype(CDT)

    return _call(
        body, hooks, [xs, ys], name=name, grid=(B, nt),
        out_shape=[jax.ShapeDtypeStruct((B * M, D), CDT)],
        in_specs=[pl.BlockSpec((S, tr), lambda b, i: (0, i)), pl.BlockSpec((None, S, D), lambda b, i: (b, 0, 0))],
        out_specs=[pl.BlockSpec((tr, D), lambda b, i: (block_of(b) * nt + i, 0))],
        compiler_params=_params("arbitrary", "arbitrary"),
    )


def _wgrad(name, xs, y, hooks=()):
    return _tn(name, xs, y[None], lambda b: 0, hooks)


def _merge_bwd_act(dx2, ya, yb, proj, wa, wb, wo, hooks=(), tm=512):
    S = dx2.shape[0]

    def body(dx_ref, ya_ref, yb_ref, gt_ref, wa_ref, wb_ref, wo_ref,
             dp_ref, da_ref, dr_ref, mix_ref, dxb_ref, dya_ref, dyb_ref):
        dxb = dx_ref[...].astype(CDT)
        dxb_ref[...] = dxb
        dmix = _mm_nt(dxb, wo_ref[...])
        ya = ya_ref[...].astype(f32)
        yb = yb_ref[...].astype(f32)
        sa = jax.nn.sigmoid(gt_ref[0].astype(f32))
        sb = jax.nn.sigmoid(gt_ref[1].astype(f32))
        mix_ref[...] = (sa * ya + sb * yb).astype(CDT)
        dya = (dmix * sa).astype(CDT)
        dyb = (dmix * sb).astype(CDT)
        dya_ref[...] = dya
        dyb_ref[...] = dyb
        dp_ref[0] = (dmix * ya * sa * (1.0 - sa)).astype(CDT)
        dp_ref[1] = (dmix * yb * sb * (1.0 - sb)).astype(CDT)
        da_ref[...] = _mm_nt(dya, wa_ref[...]).astype(CDT)
        dr_ref[...] = _mm_nt(dyb, wb_ref[...]).astype(CDT)

    tok = pl.BlockSpec((tm, D), lambda i: (i, 0))
    gates = pl.BlockSpec((2, tm, D), lambda i: (SLOT_GA // 2, i, 0))
    act = jax.ShapeDtypeStruct((S, D), CDT)
    return _call(
        body, hooks, [dx2, ya, yb, proj, wa, wb, wo], name="merge_bwd_act", grid=(S // tm,),
        out_shape=[jax.ShapeDtypeStruct((8, S, D), CDT), act, act, act, act, act, act],
        in_specs=[tok, tok, tok, gates, _wblock(wa), _wblock(wb), _wblock(wo)],
        out_specs=[gates, tok, tok, tok, tok, tok, tok],
        compiler_params=_params("arbitrary"),
    )


def _sgu_bwd(da, proj, dproj, gn, bn, ws, wst, bsc, hooks=(), tm=512):
    S = proj.shape[1]
    GW = D // G

    def body(da_ref, p_ref, dpin_ref, gn_ref, bn_ref, ws_ref, wst_ref, bs_ref,
             dp_ref, dws_ref, dbs_ref, dgn_ref, dbn_ref, ds_ref, dvn_ref):
        @pl.when(pl.program_id(0) == 0)
        def _():
            dws_ref[...] = jnp.zeros_like(dws_ref)
            dbs_ref[...] = jnp.zeros_like(dbs_ref)
            dgn_ref[...] = jnp.zeros_like(dgn_ref)
            dbn_ref[...] = jnp.zeros_like(dbn_ref)

        pu = p_ref[0].astype(f32)
        pv = p_ref[1].astype(f32)
        ua, dua = _gelu_and_grad(pu)
        va, dva_dpv = _gelu_and_grad(pv)
        gn = gn_ref[...]
        rstd, vhat, vn = _sgu_norm(va, gn, bn_ref[...])
        vnb = vn.astype(CDT)
        dav = da_ref[...].astype(f32)
        dsb = (dav * ua).astype(CDT)
        ones = jnp.ones((8, GW), CDT)
        for ch in range(tm // C):
            rs = slice(ch * C, (ch + 1) * C)
            for gi in range(G):
                cs = slice(gi * GW, (gi + 1) * GW)
                s = _mm(ws_ref[gi], vnb[rs, cs]) + bs_ref[gi]
                ds_ref[rs, cs] = s
                dsg = dsb[rs, cs]
                dws_ref[gi] += _mm_nt(dsg, vnb[rs, cs])
                dbs_ref[gi] += _mm_nt(ones, dsg)
                dvn_ref[rs, cs] = _mm(wst_ref[gi], dsg)
        dp_ref[0] = (dav * ds_ref[...] * dua).astype(CDT)
        dvn = dvn_ref[...]
        dgn_ref[...] += jnp.sum(dvn * vhat, axis=0, keepdims=True)
        dbn_ref[...] += jnp.sum(dvn, axis=0, keepdims=True)
        dvh = dvn * gn
        dva = rstd * (dvh - jnp.mean(dvh, axis=-1, keepdims=True) - vhat * jnp.mean(dvh * vhat, axis=-1, keepdims=True))
        dp_ref[1] = (dva * dva_dpv).astype(CDT)

    uv = pl.BlockSpec((2, tm, D), lambda i: (0, i, 0))
    row = _resident((1, D), lambda i: (0, 0))
    return _call(
        body, hooks, [da, proj, dproj, gn, bn, ws, wst, bsc], name="sgu_bwd", grid=(S // tm,),
        out_shape=[jax.ShapeDtypeStruct(dproj.shape, CDT), jax.ShapeDtypeStruct((G, C, C), f32),
                   jax.ShapeDtypeStruct((G, 8, C), f32), jax.ShapeDtypeStruct((1, D), f32), jax.ShapeDtypeStruct((1, D), f32)],
        in_specs=[pl.BlockSpec((tm, D), lambda i: (i, 0)), uv, _HBM, row, row,
                  _resident((G, C, C), lambda i: (0, 0, 0)), _resident((G, C, C), lambda i: (0, 0, 0)),
                  _resident((G, C, 1), lambda i: (0, 0, 0))],
        out_specs=[uv, pl.BlockSpec((G, C, C), lambda i: (0, 0, 0)), pl.BlockSpec((G, 8, C), lambda i: (0, 0, 0)),
                   pl.BlockSpec((1, D), lambda i: (0, 0)), pl.BlockSpec((1, D), lambda i: (0, 0))],
        scratch_shapes=[pltpu.VMEM((tm, D), f32), pltpu.VMEM((tm, D), f32)],
        input_output_aliases={2: 0},
        compiler_params=_params("arbitrary"),
    )


def _ret_bwd(dr, R, sfs, sbs, proj, dproj, cos, sin, dl, hooks=()):
    S = proj.shape[1]
    NC = S // RC
    assert NC % 2 == 0

    def body(dr_ref, R_ref, sf_ref, sb_ref, q_ref, k_ref, v_ref, g_ref, dpin_ref, cos_ref, sin_ref, dl_ref,
             dp_ref, dd_ref, dR_ref, gb_ref, gf_ref, acc_ref):
        t = _decay_tables(dl_ref)

        def gate_norm_bwd(i, carry):
            rs = pl.ds(pl.multiple_of(i * ROW_TILE, ROW_TILE), ROW_TILE)
            Rv = R_ref[rs, :]
            rstd = lax.rsqrt(jnp.mean(Rv * Rv, axis=-1, keepdims=True) + EPS)
            rn = Rv * rstd
            gv = g_ref[rs, :].astype(f32)
            s = jax.nn.sigmoid(gv)
            drv = dr_ref[rs, :].astype(f32)
            dp_ref[3, rs, :] = (drv * rn * (s * (1.0 + gv * (1.0 - s)))).astype(CDT)
            drn = drv * gv * s
            dR_ref[rs, :] = (rstd * (drn - rn * jnp.mean(drn * rn, axis=-1, keepdims=True))).astype(CDT)
            return carry

        lax.fori_loop(0, S // ROW_TILE, gate_norm_bwd, 0)

        def chunk(n):
            rows = pl.ds(pl.multiple_of(n * RC, RC), RC)
            return rows, q_ref[rows, :], k_ref[rows, :], v_ref[rows, :], dR_ref[rows, :]

        def emit_kv(rows, dk, dv, final):
            if not final:
                dp_ref[1, rows, :] = dk.astype(CDT)
                dp_ref[2, rows, :] = dv.astype(CDT)
            else:
                co, si = cos_ref[rows, :], sin_ref[rows, :]
                dk = dp_ref[1, rows, :].astype(f32) + dk
                dp_ref[1, rows, :] = (_unrotate(dk, co, si) * K_SCALE).astype(CDT)
                dp_ref[2, rows, :] = (dp_ref[2, rows, :].astype(f32) + dv).astype(CDT)

        gb_ref[...] = jnp.zeros_like(gb_ref)
        gf_ref[...] = jnp.zeros_like(gf_ref)
        acc_ref[...] = jnp.zeros_like(acc_ref)
        span = t["dmat"] * jnp.abs(t["d"])
        span_f, span_b = jnp.where(t["lower"], span, 0.0), jnp.where(t["lower"], 0.0, span)

        def ascend(n, final):
            rows, qn, kn, vn, dRn = chunk(n)
            qf, kf = qn.astype(f32), kn.astype(f32)
            sc = _mm_nt(qn, kn)
            dA = _mm_nt(dRn, vn)
            prod = sc * dA
            lgf_part = jnp.sum(prod * span_f, axis=0, keepdims=True)
            lgb_part = jnp.sum(prod * span_b, axis=0, keepdims=True)
            dsc = (dA * t["dmat"]).astype(CDT)
            dq = _mm(dsc, kn)
            scT = (_mm_nt(kn, qn) * t["dmat_t"]).astype(CDT)
            dscT = (_mm_nt(vn, dRn) * t["dmat_t"]).astype(CDT)
            dk = _mm(dscT, qn)
            dv = _mm(scT, dRn)
            sfb = sf_ref[n]
            sbb = sb_ref[n]
            qdf = qf * t["fq"]
            dqdf = _mm_nt(dRn, sfb)
            dq += dqdf * t["fq"]
            lgf_row = jnp.sum(qdf * dqdf * (t["pos"] + 1.0), axis=0, keepdims=True)
            qdb = qf * t["bq"]
            dqdb = _mm_nt(dRn, sbb)
            dq += dqdb * t["bq"]
            lgb_row = jnp.sum(qdb * dqdb * (RC - t["pos"]), axis=0, keepdims=True)
            gb = gb_ref[...]
            gbb = gb.astype(CDT)
            kdb = kf * t["bk"]
            dkdb = _mm_nt(vn, gbb)
            dk += dkdb * t["bk"]
            dv += _mm(kdb.astype(CDT), gbb)
            lgb_row += jnp.sum(kdb * dkdb * t["pos"], axis=0, keepdims=True)
            lgb_row += float(RC) * t["lamb"] * jnp.sum(gb * sbb.astype(f32), axis=0, keepdims=True)
            co, si = cos_ref[rows, :], sin_ref[rows, :]
            dp_ref[0, rows, :] = _unrotate(dq, co, si).astype(CDT)
            emit_kv(rows, dk, dv, final)
            acc_ref[0:1, :] += lgf_row + lgf_part
            acc_ref[1:2, :] += lgb_row + lgb_part
            gb_ref[...] = gb * t["lamb"] + _mm_tn(qdb.astype(CDT), dRn)

        def descend(n, final):
            rows, qn, kn, vn, dRn = chunk(n)
            gf = gf_ref[...]
            gfb = gf.astype(CDT)
            kdf = kn.astype(f32) * t["fk"]
            dkdf = _mm_nt(vn, gfb)
            lgf_row = jnp.sum(kdf * dkdf * (RC - 1.0 - t["pos"]), axis=0, keepdims=True)
            lgf_row += float(RC) * t["lamf"] * jnp.sum(gf * sf_ref[n].astype(f32), axis=0, keepdims=True)
            acc_ref[0:1, :] += lgf_row
            emit_kv(rows, dkdf * t["fk"], _mm(kdf.astype(CDT), gfb), final)
            gf_ref[...] = gf * t["lamf"] + _mm_tn((qn.astype(f32) * t["fq"]).astype(CDT), dRn)

        def sweep(final):
            def step(i, carry):
                ascend(i, final)
                descend(NC - 1 - i, final)
                return carry
            return step

        lax.fori_loop(0, NC // 2, sweep(False), 0, unroll=2)
        lax.fori_loop(NC // 2, NC, sweep(True), 0, unroll=2)
        dlg = jnp.sum(acc_ref[...], axis=1, keepdims=True)
        dlogit = dlg * jax.nn.sigmoid(-dl_ref[:, 0:1])
        lane = lax.broadcasted_iota(jnp.int32, (8, 128), 1)
        dd_ref[...] = jnp.where(lane == pl.program_id(0), jnp.broadcast_to(dlogit, (8, 128)), 0.0)

    def seg(slot):
        return pl.BlockSpec((None, S, DK), lambda h: (slot, 0, h))

    head = pl.BlockSpec((S, DK), lambda h: (0, h))
    states = pl.BlockSpec((None, NC, DK, DK), lambda h: (h, 0, 0, 0))
    return _call(
        body, hooks, [dr, R, sfs, sbs, proj, proj, proj, proj, dproj, cos, sin, dl], name="ret_bwd", grid=(H,),
        out_shape=[jax.ShapeDtypeStruct(dproj.shape, CDT), jax.ShapeDtypeStruct((H, 8, 128), f32)],
        in_specs=[head, head, states, states, seg(SLOT_Q), seg(SLOT_K), seg(SLOT_VR), seg(SLOT_GR), _HBM,
                  _resident((S, DK // 2), lambda h: (0, 0)), _resident((S, DK // 2), lambda h: (0, 0)),
                  pl.BlockSpec((None, 8, DK), lambda h: (h, 0, 0))],
        out_specs=[pl.BlockSpec((4, S, DK), lambda h: (1, 0, h), pipeline_mode=pl.Buffered(1)),
                   pl.BlockSpec((None, 8, 128), lambda h: (h, 0, 0))],
        scratch_shapes=[pltpu.VMEM((S, DK), CDT),
                        pltpu.VMEM((DK, DK), f32), pltpu.VMEM((DK, DK), f32), pltpu.VMEM((8, DK), f32)],
        input_output_aliases={8: 0},
        compiler_params=_params("arbitrary"),
    )


def _proj_bwd_act(dproj, dx2, x1, nrm, wfull, hooks=(), tm=512):
    S = x1.shape[0]

    def body(dp_ref, dx2_ref, x_ref, n_ref, w_ref, dx_ref, dxh_ref, dn_ref, db_ref, acc_ref):
        @pl.when(pl.program_id(0) == 0)
        def _():
            dn_ref[...] = jnp.zeros_like(dn_ref)
            db_ref[...] = jnp.zeros_like(db_ref)

        for p in range(8):
            seg = SEG_OF_SLOT[p]
            dp = dp_ref[p]
            db_ref[seg] += jnp.sum(dp.astype(f32), axis=0, keepdims=True)
            dh = _mm_nt(dp, w_ref[seg * D:(seg + 1) * D, :])
            if p == 0:
                acc_ref[...] = dh
            else:
                acc_ref[...] += dh
        n = n_ref[...]
        r, xh, _ = _rms_fwd(x_ref[...], n)
        dx, dn = _rms_bwd(acc_ref[...], r, xh, n)
        dx = dx2_ref[...] + dx
        dx_ref[...] = dx
        dxh_ref[...] = (0.5 * dx).astype(CDT)
        dn_ref[...] += dn

    tok = pl.BlockSpec((tm, D), lambda i: (i, 0))
    return _call(
        body, hooks, [dproj, dx2, x1, nrm, wfull], name="proj_bwd_act", grid=(S // tm,),
        out_shape=[jax.ShapeDtypeStruct((S, D), f32), jax.ShapeDtypeStruct((S, D), CDT), jax.ShapeDtypeStruct((1, D), f32),
                   jax.ShapeDtypeStruct((8, 1, D), f32)],
        in_specs=[pl.BlockSpec((8, tm, D), lambda i: (0, i, 0)), tok, tok, _resident((1, D), lambda i: (0, 0)),
                  _resident((8 * D, D), lambda i: (0, 0))],
        out_specs=[tok, tok, pl.BlockSpec((1, D), lambda i: (0, 0)), pl.BlockSpec((8, 1, D), lambda i: (0, 0, 0))],
        scratch_shapes=[pltpu.VMEM((tm, D), f32)],
        compiler_params=_params("arbitrary"),
    )


def _rs_sum(name, gfulls, lands, my_c):
    n = len(gfulls)
    rows = gfulls[0].shape[0] // NDEV
    assert all(g.shape[0] == NDEV * rows for g in gfulls)

    def body(c_ref, *refs):
        for g_ref, l_ref, o_ref in zip(refs[:n], refs[n:2 * n], refs[2 * n:]):
            o_ref[...] = (g_ref[...].astype(f32) + l_ref[...].astype(f32)).astype(CDT)

    slot = pl.BlockSpec((None, rows, D), lambda k, c: (k, 0, 0))
    return pl.pallas_call(
        body, name=name,
        grid_spec=pltpu.PrefetchScalarGridSpec(
            num_scalar_prefetch=1, grid=(NCHIP,),
            in_specs=[pl.BlockSpec((rows, D), lambda k, c: (2 * k + c[0], 0))] * n + [slot] * n,
            out_specs=[slot] * n),
        out_shape=[jax.ShapeDtypeStruct((NCHIP, rows, D), CDT)] * n,
        compiler_params=_params("arbitrary"),
    )(my_c, *gfulls, *lands)


def _adamw_math(g, w, m, v):
    m2 = ADAM_B1 * m + (1.0 - ADAM_B1) * g
    v2 = ADAM_B2 * v + (1.0 - ADAM_B2) * (g * g)
    delta = -ADAM_LR * ((m2 / BC1) / (jnp.sqrt(v2 / BC2) + ADAM_EPS) + ADAM_WD * w)
    return delta, m2, v2


def _adamw_big(name, landed, w, m, v, after):
    n = len(w)
    rows = w[0].shape[0]
    tr = 256 if rows % 256 == 0 else (rows // 2 if rows > 256 else rows)
    nt = rows // tr

    def body(*refs):
        ins, outs = refs[:4 * n], refs[4 * n + 1:]
        for j in range(n):
            @pl.when(pl.program_id(0) == j)
            def _(j=j):
                l_ref, w_ref, m_ref, v_ref = ins[j], ins[n + j], ins[2 * n + j], ins[3 * n + j]
                g = l_ref[0].astype(f32)
                for k in range(1, NCHIP):
                    g = g + l_ref[k].astype(f32)
                outs[4 * j][...] = g
                outs[4 * j + 1][...], outs[4 * j + 2][...], outs[4 * j + 3][...] = _adamw_math(g, w_ref[...], m_ref[...], v_ref[...])

    def tile(j):
        return lambda jj, i: jnp.clip((jj - j) * nt + i, 0, nt - 1)

    blk = [pl.BlockSpec((tr, D), lambda jj, i, t=tile(j): (t(jj, i), 0)) for j in range(n)]
    lnd = [pl.BlockSpec((NCHIP, tr, D), lambda jj, i, t=tile(j): (0, t(jj, i), 0)) for j in range(n)]
    o = jax.ShapeDtypeStruct((rows, D), f32)
    res = pl.pallas_call(
        body, name=name, grid=(n, nt), out_shape=[o] * (4 * n),
        in_specs=lnd + blk + blk + blk + [_HBM],
        out_specs=[blk[j] for j in range(n) for _ in range(4)],
        compiler_params=_params("arbitrary", "arbitrary"),
    )(*landed, *w, *m, *v, after)
    return [res[4 * j:4 * j + 4] for j in range(n)]


ROW_FFN1_NORM, ROW_MIX_NORM, ROW_SGU_G, ROW_SGU_B, ROW_FFN2_NORM, ROW_FINAL_NORM, ROW_B_IN = 0, 1, 2, 3, 4, 5, 8
ROW_WS, ROW_BS, ROW_DECAY = 0, G * C, G * C + G * 8


def _adamw_small(ga, gb, gn1, gl, params):
    def body(ga_ref, gb_ref, gn1_ref, gl_ref, *refs):
        ins, outs = refs[:30], refs[30:]

        def total(ref, r0, n):
            g = ref[0, r0:r0 + n, :]
            for j in range(1, NDEV):
                g = g + ref[j, r0:r0 + n, :]
            return g

        def apply(i, g, rows=slice(None)):
            w, m, v = ins[3 * i][rows, :], ins[3 * i + 1][rows, :], ins[3 * i + 2][rows, :]
            outs[4 * i][rows, :] = g
            outs[4 * i + 1][rows, :], outs[4 * i + 2][rows, :], outs[4 * i + 3][rows, :] = _adamw_math(g, w, m, v)

        outs[40][...] = total(gl_ref, 0, 8)
        apply(0, total(gn1_ref, 0, 1))
        for i, r in enumerate((ROW_FFN1_NORM, ROW_MIX_NORM, ROW_SGU_G, ROW_SGU_B, ROW_FFN2_NORM, ROW_FINAL_NORM)):
            if i:
                apply(i, total(ga_ref, r, 1))
        apply(6, total(ga_ref, ROW_B_IN, 8))
        apply(7, total(gb_ref, ROW_WS, G * C))
        for gi in range(G):
            apply(8, total(gb_ref, ROW_BS + 8 * gi, 1), slice(gi, gi + 1))
        dec = total(gb_ref, ROW_DECAY, 8)
        for hh in range(1, H):
            dec = dec + total(gb_ref, ROW_DECAY + 8 * hh, 8)
        apply(9, dec)

    flat = [a for p in params for a in p]
    out_shape = [jax.ShapeDtypeStruct(p[0].shape, f32) for p in params for _ in range(4)]
    out_shape.append(jax.ShapeDtypeStruct((8, 128), f32))
    vm = pl.BlockSpec(memory_space=pltpu.VMEM)
    return pl.pallas_call(
        body, name="adamw_small", out_shape=out_shape,
        in_specs=[vm] * (4 + len(flat)), out_specs=[vm] * len(out_shape),
        compiler_params=pltpu.CompilerParams(vmem_limit_bytes=VMEM_LIMIT),
    )(ga, gb, gn1, gl, *flat)


def kernel(x, ffn1_norm, ffn1_w_gate, ffn1_w_up, ffn1_w_down, mix_norm, w_in, b_in, sgu_norm_g, sgu_norm_b, sgu_w_s, sgu_b_s, ret_decay_logit, w_branch_a, w_branch_b, w_out, ffn2_norm, ffn2_w_gate, ffn2_w_up, ffn2_w_down, final_norm, loss_target, m_ffn1_norm, m_ffn1_w_gate, m_ffn1_w_up, m_ffn1_w_down, m_mix_norm, m_w_in, m_b_in, m_sgu_norm_g, m_sgu_norm_b, m_sgu_w_s, m_sgu_b_s, m_ret_decay_logit, m_w_branch_a, m_w_branch_b, m_w_out, m_ffn2_norm, m_ffn2_w_gate, m_ffn2_w_up, m_ffn2_w_down, m_final_norm, v_ffn1_norm, v_ffn1_w_gate, v_ffn1_w_up, v_ffn1_w_down, v_mix_norm, v_w_in, v_b_in, v_sgu_norm_g, v_sgu_norm_b, v_sgu_w_s, v_sgu_b_s, v_ret_decay_logit, v_w_branch_a, v_w_branch_b, v_w_out, v_ffn2_norm, v_ffn2_w_gate, v_ffn2_w_up, v_ffn2_w_down, v_final_norm):
    args = dict(locals())
    S = x.shape[1]
    xs = x[0]
    target = loss_target[0]

    def buf_layout(name, a):
        a = a[0]
        return a.T if name in W_TRANSPOSED else a

    sh = {n: buf_layout(n, args[n]).astype(CDT) for n in W_NAMES}
    wf = {}

    b3 = b_in.reshape(8, 1, D)
    ws = sgu_w_s[0].astype(CDT)
    wst = jnp.swapaxes(sgu_w_s[0], 1, 2).astype(CDT)
    bsc = sgu_b_s[0].reshape(G, C, 1)
    dl = jnp.zeros((H, 8, DK), f32).at[:, 0:2, :].set(jnp.broadcast_to(ret_decay_logit[0].T[:, :, None], (H, 2, DK)))
    theta = ROPE_BASE ** (-jnp.arange(0, DK, 2, dtype=f32) / DK)
    ang = jnp.arange(S, dtype=f32)[:, None] * theta[None, :]
    cos, sin = jnp.cos(ang), jnp.sin(ang)
    fnorm = final_norm.reshape(1, D)

    f1 = ("ffn1_w_gate", "ffn1_w_up", "ffn1_w_down")
    f2 = ("ffn2_w_gate", "ffn2_w_up", "ffn2_w_down")
    br = ("w_branch_a", "w_branch_b", "w_out")
    for cid, names in enumerate((f1, ("w_in",), br, f2)):
        wf.update(zip(names, _sequence("ag_" + names[0], 1 + cid, NEIGHBOURS, [_ag_hook(sh[n]) for n in names])))
    x1, g1, u1, a1, hf1 = _ffn_fwd("ffn1_fwd", xs, ffn1_norm, *[wf[n] for n in f1])
    proj, h2 = _proj_fwd(x1, mix_norm, wf["w_in"], b3, cos, sin)
    a = _sgu_fwd(proj, sgu_norm_g, sgu_norm_b, ws, bsc)
    R, r, sfs, sbs = _ret_fwd(proj, dl)
    x2, ya, yb = _merge_fwd(a, r, proj, x1, *[wf[n] for n in br])
    x3, g2, u2, a2, hf2 = _ffn_fwd("ffn2_fwd", x2, ffn2_norm, *[wf[n] for n in f2])
    dx3, dyh2, d_final, loss_part = _loss_head(x3, fnorm, target)

    my_c = lax.axis_index("c").astype(jnp.int32).reshape(1)
    gw, landed, sequenced = {}, {}, []

    def d2d(*names):
        return [_rs_d2d_hook(gw[n]) for n in names]

    def behind(x, token):
        return lax.optimization_barrier((x, token))[0]

    def to_chips(names, sibs, more=()):
        parts = list(_rs_sum("rs_sum_" + names[0], [gw[n] for n in names], list(sibs), my_c))
        token = parts[0]
        if sequenced:
            parts[0] = behind(parts[0], sequenced[-1])
        hooks = [_rs_ici_hook(p) for p in parts] + ([_small_hook(list(more))] if more else [])
        reach = EVERYONE if more else CHIPS
        got = _sequence("rs_chips_" + names[0], REACH_ID[reach], reach, hooks)
        sequenced.append(got[0])
        landed.update(zip(names, got))
        return got[len(names):], token

    def ffn_bwd(tag, names, dy, dyh, x, g, u, a, h, nrm, each_alone, more=()):
        wg, wu, wd = names
        (gw[wd],) = _wgrad(tag + "_wd_grad", a, dyh)
        dg, du, sib_d = _ffn_bwd_hidden(tag + "_bwd_hidden", dyh, g, u, wf[wd], d2d(wd))
        if each_alone:
            dg = behind(dg, to_chips([wd], [sib_d])[1])
        (gw[wg],) = _wgrad(tag + "_wg_grad", dg, h)
        gw[wu], sib_g = _wgrad(tag + "_wu_grad", du, h, d2d(wg))
        if each_alone:
            (sib_u,) = _sequence("rs_sib_" + wu, SIBLING_ID, SIBLING, [_rs_d2d_hook(behind(gw[wu], sequenced[-1]))])
            sequenced.append(sib_u)
            dy = behind(dy, to_chips([wg], [sib_g])[1])
            dx, dn = _ffn_bwd_in(tag + "_bwd_in", dy, x, dg, du, nrm, wf[wg], wf[wu])
            return (dx, dn) + to_chips([wu], [sib_u], more(dn))
        dx, dn, sib_u = _ffn_bwd_in(tag + "_bwd_in", dy, x, dg, du, nrm, wf[wg], wf[wu], d2d(wu))
        return (dx, dn) + to_chips([wd, wg, wu], [sib_d, sib_g, sib_u])

    dx2, d_ffn2n, _, token = ffn_bwd("ffn2", f2, dx3, dyh2, x2, g2, u2, a2, hf2, ffn2_norm, False)
    dproj, da, dr, mix, dx2b, dya, dyb = _merge_bwd_act(behind(dx2, token), ya, yb, proj, *[wf[n] for n in br])
    dproj, d_ws, d_bs, d_gn, d_bn = _sgu_bwd(da, proj, dproj, sgu_norm_g, sgu_norm_b, ws, wst, bsc)
    dproj, d_dec = _ret_bwd(dr, R, sfs, sbs, proj, dproj, cos, sin, dl)
    (gw["w_in"],) = _tn("win_grad", h2, dproj, _seg_of_slot)
    gw["w_out"], sib_win = _wgrad("wo_grad", mix, dx2b, d2d("w_in"))
    small_sgu = jnp.concatenate([d_ws.reshape(G * C, C), d_bs.reshape(G * 8, C)], axis=0)
    (g_sgu, gl), token = to_chips(["w_in"], [sib_win], [small_sgu, loss_part])
    (gw["w_branch_a"],) = _wgrad("wa_grad", a, behind(dya, token))
    (gw["w_branch_b"],) = _wgrad("wb_grad", r, dyb)
    dx1, dyh1, d_mixn, d_bin, *sib_br = _proj_bwd_act(behind(dproj, token), dx2, x1, mix_norm, wf["w_in"], d2d(*br))
    small_a = jnp.concatenate([jnp.zeros((1, D), f32), d_mixn, d_gn, d_bn, d_ffn2n, d_final, jnp.zeros((2, D), f32),
                               d_bin.reshape(8, D)], axis=0)
    (ga, g_dec), token = to_chips(list(br), sib_br, [small_a, d_dec.reshape(H * 8, 128)])
    gb = jnp.concatenate([g_sgu, g_dec], axis=1)
    dxs, d_ffn1n, (gn1,), token = ffn_bwd("ffn1", f1, dx1, behind(dyh1, token), xs, g1, u1, a1, hf1, ffn1_norm, True,
                                          lambda dn: [dn])

    out = {"grad_x": dxs[None]}

    def native(name, a):
        a = a.T if name in W_TRANSPOSED else a
        return a[None]

    after = token
    for names in (f2, ("w_in",), br, (f1[2], f1[0], f1[1])):
        res = _adamw_big("adamw_" + names[0], [landed[n] for n in names], [buf_layout(n, args[n]) for n in names],
                         [buf_layout(n, args["m_" + n]) for n in names], [buf_layout(n, args["v_" + n]) for n in names], after)
        after = res[-1][0]
        for n, four in zip(names, res):
            for pre, val in zip(("grad_", "delta_", "new_m_", "new_v_"), four):
                out[pre + n] = native(n, val)

    def pad_decay(a):
        return jnp.zeros((8, 128), f32).at[0:2, 0:H].set(a[0])

    small = [
        ("ffn1_norm", lambda a: a, lambda a: a), ("mix_norm", lambda a: a, lambda a: a),
        ("sgu_norm_g", lambda a: a, lambda a: a), ("sgu_norm_b", lambda a: a, lambda a: a),
        ("ffn2_norm", lambda a: a, lambda a: a),
        ("final_norm", lambda a: a.reshape(1, D), lambda a: a.reshape(D)),
        ("b_in", lambda a: a.reshape(8, D), lambda a: a.reshape(1, 8 * D)),
        ("sgu_w_s", lambda a: a.reshape(G * C, C), lambda a: a.reshape(1, G, C, C)),
        ("sgu_b_s", lambda a: a[0], lambda a: a[None]),
        ("ret_decay_logit", pad_decay, lambda a: a[None, 0:2, 0:H]),
    ]
    res = _adamw_small(ga, gb, gn1, gl, [(to(args[n]), to(args["m_" + n]), to(args["v_" + n])) for n, to, _ in small])
    out["loss"] = res[40][0, 0]
    for i, (n, _, back) in enumerate(small):
        for j, pre in enumerate(("grad_", "delta_", "new_m_", "new_v_")):
            out[pre + n] = back(res[4 * i + j])

    weights = ("ffn1_norm", "ffn1_w_gate", "ffn1_w_up", "ffn1_w_down", "mix_norm", "w_in", "b_in", "sgu_norm_g",
               "sgu_norm_b", "sgu_w_s", "sgu_b_s", "ret_decay_logit", "w_branch_a", "w_branch_b", "w_out", "ffn2_norm",
               "ffn2_w_gate", "ffn2_w_up", "ffn2_w_down", "final_norm")
    return (out["loss"], out["grad_x"], *[out["grad_" + n] for n in weights], *[out["delta_" + n] for n in weights],
            *[out["new_m_" + n] for n in weights], *[out["new_v_" + n] for n in weights])
```

```python
import math

import jax
import jax.numpy as jnp
from jax import lax
from jax.experimental import pallas as pl
from jax.experimental.pallas import tpu as pltpu
from jax.experimental.pallas import tpu_sc as plsc

f32 = jnp.float32
CDT = jnp.bfloat16

D = 1024
F = 2816
C = 128
RC = 256
H = 4
DK = 256
G = 4
NDEV = 8
NCHIP = 4
EPS = 1e-6
ROPE_BASE = 10000.0
FT = 256
V7X_VMEM_BYTES = 64 * 1024 * 1024
VMEM_LIMIT = V7X_VMEM_BYTES - 8 * 1024 * 1024

ADAM_LR, ADAM_B1, ADAM_B2, ADAM_EPS, ADAM_WD, ADAM_STEP = 0.001, 0.9, 0.999, 1e-08, 0.01, 10
BC1 = 1.0 - ADAM_B1 ** ADAM_STEP
BC2 = 1.0 - ADAM_B2 ** ADAM_STEP

W_ROWS = dict(ffn1_w_gate=352, ffn1_w_up=352, ffn1_w_down=352, w_in=1024, w_branch_a=128, w_branch_b=128, w_out=128,
              ffn2_w_gate=352, ffn2_w_up=352, ffn2_w_down=352)
W_NAMES = tuple(W_ROWS)
W_TRANSPOSED = ("ffn1_w_gate", "ffn1_w_up", "ffn2_w_gate", "ffn2_w_up")

SLOT_U, SLOT_V, SLOT_GA, SLOT_GB, SLOT_Q, SLOT_K, SLOT_VR, SLOT_GR = range(8)


SEG_OF_SLOT = (0, 1, 6, 7, 2, 3, 4, 5)


def _seg_of_slot(p):
    return jnp.where(p < 2, p, jnp.where(p < 4, p + 4, p - 2))


def _mm(a, b):
    return jnp.dot(a, b, preferred_element_type=f32)


def _mm_nt(a, b):
    return lax.dot_general(a, b, (((1,), (1,)), ((), ())), preferred_element_type=f32)


def _mm_tn(a, b):
    return lax.dot_general(a, b, (((0,), (0,)), ((), ())), preferred_element_type=f32)


def _params(*sem):
    return pltpu.CompilerParams(dimension_semantics=sem, vmem_limit_bytes=VMEM_LIMIT)


def _resident(shape, index_map):
    return pl.BlockSpec(shape, index_map, pipeline_mode=pl.Buffered(1))


def _gelu(x):
    return 0.5 * x * (1.0 + lax.erf(x * (1.0 / math.sqrt(2.0))))


def _gelu_and_grad(x):
    cdf = 0.5 * (1.0 + lax.erf(x * (1.0 / math.sqrt(2.0))))
    return x * cdf, cdf + x * jnp.exp(-0.5 * x * x) * (1.0 / math.sqrt(2.0 * math.pi))


def _rms_fwd(x, n):
    r = lax.rsqrt(jnp.mean(x * x, axis=-1, keepdims=True) + EPS)
    xh = x * r
    return r, xh, xh * n


def _rms_bwd(dh, r, xh, n):
    dxh = dh * n
    dx = r * (dxh - xh * jnp.mean(dxh * xh, axis=-1, keepdims=True))
    return dx, jnp.sum(dh * xh, axis=0, keepdims=True)


MESH_ID = pl.DeviceIdType.MESH
_HBM = pl.BlockSpec(memory_space=pltpu.HBM)


def _my_place():
    return lax.axis_index("x"), lax.axis_index("y"), lax.axis_index("c")


def _ici_peers(x, y, c):
    return [((1 - x, y, c), 2 * (1 - x) + y), ((x, 1 - y, c), 2 * x + 1 - y), ((1 - x, 1 - y, c), 2 * (1 - x) + 1 - y)]


class _Hook:
    def __init__(self, operands, out_shapes, n_remote, n_local, start, finish, relay=None):
        self.operands, self.out_shapes = list(operands), list(out_shapes)
        self.n_remote, self.n_local, self.start, self.finish = n_remote, n_local, start, finish
        self.relay = relay or (lambda *a: None)


def _call(body, hooks, operands, *, in_specs, out_specs, out_shape, grid=None, scratch_shapes=(), **kw):
    hooks = tuple(hooks)
    n_in, n_out, n_scr = len(in_specs), len(out_shape), len(scratch_shapes)
    h_ops = [a for h in hooks for a in h.operands]
    h_outs = [s for h in hooks for s in h.out_shapes]
    h_sems = [pltpu.SemaphoreType.DMA((n,)) for h in hooks for n in (h.n_remote, h.n_remote, max(h.n_local, 1))]

    def wrapped(*refs):
        ins, hin = refs[:n_in], refs[n_in:n_in + len(h_ops)]
        o0 = n_in + len(h_ops)
        outs, hout = refs[o0:o0 + n_out], refs[o0 + n_out:o0 + n_out + len(h_outs)]
        s0 = o0 + n_out + len(h_outs)
        scr, hsem = refs[s0:s0 + n_scr], refs[s0 + n_scr:]

        def run(phase):
            ip = op = 0
            for i, h in enumerate(hooks):
                ssem, rsem, lsem = hsem[3 * i:3 * i + 3]

                def rcopy(k, src, dst, dev, ssem=ssem, rsem=rsem):
                    return pltpu.make_async_remote_copy(src_ref=src, dst_ref=dst, send_sem=ssem.at[k], recv_sem=rsem.at[k],
                                                        device_id=dev, device_id_type=MESH_ID)

                def lcopy(k, src, dst, lsem=lsem):
                    return pltpu.make_async_copy(src, dst, lsem.at[k])

                getattr(h, phase)(hin[ip:ip + len(h.operands)], hout[op:op + len(h.out_shapes)], rcopy, lcopy)
                ip += len(h.operands)
                op += len(h.out_shapes)

        def at_edge(phase, last):
            if not hooks:
                return
            if grid is None:
                run(phase)
                return
            cond = None
            for ax, n in enumerate(grid):
                here = pl.program_id(ax) == (n - 1 if last else 0)
                cond = here if cond is None else cond & here
            pl.when(cond)(lambda: run(phase))

        at_edge("start", False)
        at_edge("relay", True)
        body(*ins, *outs, *scr)
        at_edge("finish", True)

    if grid is not None:
        kw["grid"] = grid
    return list(pl.pallas_call(
        wrapped, out_shape=list(out_shape) + h_outs, in_specs=list(in_specs) + [_HBM] * len(h_ops),
        out_specs=list(out_specs) + [_HBM] * len(h_outs), scratch_shapes=list(scratch_shapes) + h_sems, **kw,
    )(*operands, *h_ops))


def _rows(ref, start, n):
    return ref.at[pl.ds(start, n), :]


def _ag_hook(shard):
    rows = shard.shape[0]
    half = rows // 2
    assert half % 16 == 0

    def place():
        x, y, c = _my_place()
        devs = dict(sib=(x, y, 1 - c), xn=(1 - x, y, c), yn=(x, 1 - y, c))
        chips = dict(me=2 * x + y, xn=2 * (1 - x) + y, yn=2 * x + 1 - y, dg=2 * (1 - x) + 1 - y)
        return c, devs, chips

    def block(full, chip, c):
        return _rows(full, (2 * chip + c) * rows, rows)

    def halfblock(full, chip, c, upper):
        return _rows(full, (2 * chip + c) * rows + upper * half, half)

    def start(ins, outs, rcopy, lcopy):
        c, devs, chips = place()
        src, dst = ins[0], block(outs[0], chips["me"], c)
        lcopy(0, src, dst).start()
        for k, to in enumerate(("sib", "xn", "yn")):
            rcopy(k, src, dst, devs[to]).start()

    def relay(ins, outs, rcopy, lcopy):
        c, devs, chips = place()
        full = outs[0]
        blk = block(full, chips["xn"], c)
        rcopy(1, blk, blk, devs["xn"]).wait_recv()
        low = halfblock(full, chips["xn"], c, 0)
        rcopy(3, low, low, devs["yn"]).start()
        rcopy(5, blk, blk, devs["sib"]).start()
        blk = block(full, chips["yn"], c)
        rcopy(2, blk, blk, devs["yn"]).wait_recv()
        up = halfblock(full, chips["yn"], c, 1)
        rcopy(4, up, up, devs["xn"]).start()
        rcopy(6, blk, blk, devs["sib"]).start()
        low, up = halfblock(full, chips["dg"], c, 0), halfblock(full, chips["dg"], c, 1)
        rcopy(3, low, low, devs["yn"]).wait_recv()
        rcopy(4, up, up, devs["xn"]).wait_recv()
        blk = block(full, chips["dg"], c)
        rcopy(7, blk, blk, devs["sib"]).start()

    def finish(ins, outs, rcopy, lcopy):
        c, devs, chips = place()
        full, sib = outs[0], devs["sib"]
        for k, chip in ((0, "me"), (5, "xn"), (6, "yn"), (7, "dg")):
            theirs = block(full, chips[chip], 1 - c)
            rcopy(k, theirs, theirs, sib).wait_recv()
            mine = block(full, chips[chip], c)
            if k:
                rcopy(k, mine, mine, sib).wait_send()
        src, dst = ins[0], block(full, chips["me"], c)
        lcopy(0, src, dst).wait()
        for k, to in enumerate(("sib", "xn", "yn")):
            rcopy(k, src, dst, devs[to]).wait_send()
        low, up = halfblock(full, chips["xn"], c, 0), halfblock(full, chips["yn"], c, 1)
        rcopy(3, low, low, devs["yn"]).wait_send()
        rcopy(4, up, up, devs["xn"]).wait_send()

    return _Hook([shard], [jax.ShapeDtypeStruct((NDEV * rows, D), shard.dtype)], 8, 1, start, finish, relay)


SIBLING, CHIPS, NEIGHBOURS, EVERYONE = "sibling", "chips", "sibling and the two neighbour chips", "everyone"
REACH_ID = {CHIPS: 6, EVERYONE: 7}
SIBLING_ID = 5


def _sequence(name, collective_id, reach, hooks):
    ins = [[jax.new_ref(a, memory_space=pltpu.MemorySpace.HBM) for a in h.operands] for h in hooks]
    outs = [[jax.empty_ref(s, memory_space=pltpu.MemorySpace.HBM) for s in h.out_shapes] for h in hooks]
    sems = tuple(pltpu.SemaphoreType.DMA((n,)) for h in hooks for n in (h.n_remote, h.n_remote, max(h.n_local, 1)))

    @pl.kernel(mesh=plsc.ScalarSubcoreMesh(axis_name="sequencer", num_cores=1), name=name, scratch_types=sems,
               compiler_params=pltpu.CompilerParams(collective_id=collective_id))
    def launch(*sem_refs):
        x, y, c = _my_place()
        chips = [dev for dev, _ in _ici_peers(x, y, c)]
        others = [(1 - x if dx else x, 1 - y if dy else y, 1 - c if dc else c)
                  for dx in range(2) for dy in range(2) for dc in range(2) if dx + dy + dc]
        devs = {SIBLING: [(x, y, 1 - c)], CHIPS: chips, NEIGHBOURS: [(x, y, 1 - c), (1 - x, y, c), (x, 1 - y, c)],
                EVERYONE: others}[reach]
        barrier = pltpu.get_barrier_semaphore()
        for dev in devs:
            pl.semaphore_signal(barrier, inc=1, device_id=dev, device_id_type=MESH_ID)
        pl.semaphore_wait(barrier, len(devs))
        for phase in ("start", "relay", "finish"):
            for i, h in enumerate(hooks):
                ssem, rsem, lsem = sem_refs[3 * i:3 * i + 3]

                def rcopy(k, src, dst, dev, ssem=ssem, rsem=rsem):
                    return pltpu.make_async_remote_copy(src_ref=src, dst_ref=dst, send_sem=ssem.at[k], recv_sem=rsem.at[k],
                                                        device_id=dev, device_id_type=MESH_ID)

                def lcopy(k, src, dst, lsem=lsem):
                    return pltpu.make_async_copy(src, dst, lsem.at[k])

                getattr(h, phase)(ins[i], outs[i], rcopy, lcopy)

    launch()
    return [o[...] for os in outs for o in os]


def _rs_d2d_hook(gfull):
    rows = gfull.shape[0] // NDEV

    def pairs(g, land):
        x, y, c = _my_place()
        return (x, y, 1 - c), [(k, _rows(g, (2 * k + 1 - c) * rows, rows), land.at[k]) for k in range(NCHIP)]

    def start(ins, outs, rcopy, lcopy):
        sib, cps = pairs(ins[0], outs[0])
        for i, src, dst in cps:
            rcopy(i, src, dst, sib).start()

    def finish(ins, outs, rcopy, lcopy):
        sib, cps = pairs(ins[0], outs[0])
        for i, src, dst in cps:
            rcopy(i, dst, dst, sib).wait_recv()
        for i, src, dst in cps:
            rcopy(i, src, dst, sib).wait_send()

    return _Hook([gfull], [jax.ShapeDtypeStruct((NCHIP, rows, D), gfull.dtype)], NCHIP, 0, start, finish)


def _rs_ici_hook(part):
    def start(ins, outs, rcopy, lcopy):
        x, y, c = _my_place()
        mychip = 2 * x + y
        lcopy(0, ins[0].at[mychip], outs[0].at[mychip]).start()
        for j, (dev, chip) in enumerate(_ici_peers(x, y, c)):
            rcopy(j, ins[0].at[chip], outs[0].at[mychip], dev).start()

    def finish(ins, outs, rcopy, lcopy):
        x, y, c = _my_place()
        mychip = 2 * x + y
        peers = _ici_peers(x, y, c)
        for j, (dev, chip) in enumerate(peers):
            rcopy(j, outs[0].at[chip], outs[0].at[chip], dev).wait_recv()
        for j, (dev, chip) in enumerate(peers):
            rcopy(j, ins[0].at[chip], outs[0].at[mychip], dev).wait_send()
        lcopy(0, ins[0].at[mychip], outs[0].at[mychip]).wait()

    return _Hook([part], [jax.ShapeDtypeStruct(part.shape, part.dtype)], 3, 1, start, finish)


def _small_hook(arrays):
    n = len(arrays)

    def peers():
        x, y, c = _my_place()
        out = []
        for dx in range(2):
            for dy in range(2):
                for dc in range(2):
                    if dx + dy + dc:
                        px, py, pc = (1 - x if dx else x), (1 - y if dy else y), (1 - c if dc else c)
                        out.append(((px, py, pc), 4 * px + 2 * py + pc))
        return 4 * x + 2 * y + c, out

    def start(ins, outs, rcopy, lcopy):
        me, ps = peers()
        for t in range(n):
            lcopy(t, ins[t], outs[t].at[me]).start()
            for i, (dev, _) in enumerate(ps):
                rcopy(n * i + t, ins[t], outs[t].at[me], dev).start()

    def finish(ins, outs, rcopy, lcopy):
        me, ps = peers()
        for t in range(n):
            for i, (dev, peer) in enumerate(ps):
                rcopy(n * i + t, outs[t].at[peer], outs[t].at[peer], dev).wait_recv()
            for i, (dev, _) in enumerate(ps):
                rcopy(n * i + t, ins[t], outs[t].at[me], dev).wait_send()
            lcopy(t, ins[t], outs[t].at[me]).wait()

    return _Hook(arrays, [jax.ShapeDtypeStruct((NDEV,) + a.shape, a.dtype) for a in arrays], 7 * n, n, start, finish)


def _wblock(w):
    return _resident(w.shape, lambda *_: (0, 0))


def _ffn_fwd(name, x, nrm, wg, wu, wd, hooks=(), tm=512):
    S = x.shape[0]

    def body(x_ref, n_ref, wg_ref, wu_ref, wd_ref, y_ref, g_ref, u_ref, a_ref, h_ref, acc_ref):
        xv = x_ref[...]
        _, _, h = _rms_fwd(xv, n_ref[...])
        h = h.astype(CDT)
        h_ref[...] = h
        for ci in range(F // FT):
            sl = slice(ci * FT, (ci + 1) * FT)
            g = _mm_nt(h, wg_ref[sl, :])
            u = _mm_nt(h, wu_ref[sl, :])
            g_ref[:, sl] = g.astype(CDT)
            u_ref[:, sl] = u.astype(CDT)
            a = (g * jax.nn.sigmoid(g) * u).astype(CDT)
            a_ref[:, sl] = a
            o = _mm(a, wd_ref[sl, :])
            if ci == 0:
                acc_ref[...] = o
            else:
                acc_ref[...] += o
        y_ref[...] = xv + 0.5 * acc_ref[...]

    tok = pl.BlockSpec((tm, D), lambda i: (i, 0))
    hid = pl.BlockSpec((tm, F), lambda i: (i, 0))
    hidden = jax.ShapeDtypeStruct((S, F), CDT)
    return _call(
        body, hooks, [x, nrm, wg, wu, wd], name=name, grid=(S // tm,),
        out_shape=[jax.ShapeDtypeStruct((S, D), f32), hidden, hidden, hidden, jax.ShapeDtypeStruct((S, D), CDT)],
        in_specs=[tok, _resident((1, D), lambda i: (0, 0)), _wblock(wg), _wblock(wu), _wblock(wd)],
        out_specs=[tok, hid, hid, hid, tok],
        scratch_shapes=[pltpu.VMEM((tm, D), f32)],
        compiler_params=_params("arbitrary"),
    )


def _proj_fwd(x1, nrm, wfull, b3, cos, sin, hooks=(), tm=512):
    S = x1.shape[0]

    def body(x_ref, n_ref, w_ref, b_ref, cos_ref, sin_ref, p_ref, h_ref):
        _, _, h = _rms_fwd(x_ref[...], n_ref[...])
        h = h.astype(CDT)
        h_ref[...] = h
        for p in range(8):
            seg = SEG_OF_SLOT[p]
            z = _mm(h, w_ref[seg * D:(seg + 1) * D, :]) + b_ref[seg]
            if p in (SLOT_Q, SLOT_K):
                co, si = cos_ref[...], sin_ref[...]
                for hh in range(H):
                    cs = slice(hh * DK, (hh + 1) * DK)
                    zr = _rotate(z[:, cs], co, si)
                    p_ref[p, :, cs] = (zr * K_SCALE if p == SLOT_K else zr).astype(CDT)
            else:
                p_ref[p] = z.astype(CDT)

    tab = pl.BlockSpec((tm, DK // 2), lambda i: (i, 0))
    return _call(
        body, hooks, [x1, nrm, wfull, b3, cos, sin], name="proj_fwd", grid=(S // tm,),
        out_shape=[jax.ShapeDtypeStruct((8, S, D), CDT), jax.ShapeDtypeStruct((S, D), CDT)],
        in_specs=[pl.BlockSpec((tm, D), lambda i: (i, 0)), _resident((1, D), lambda i: (0, 0)),
                  _resident((8 * D, D), lambda i: (0, 0)), _resident((8, 1, D), lambda i: (0, 0, 0)), tab, tab],
        out_specs=[pl.BlockSpec((8, tm, D), lambda i: (0, i, 0)), pl.BlockSpec((tm, D), lambda i: (i, 0))],
        compiler_params=_params("arbitrary"),
    )


def _sgu_norm(va, gn, bn):
    mu = jnp.mean(va, axis=-1, keepdims=True)
    xc = va - mu
    rstd = lax.rsqrt(jnp.mean(xc * xc, axis=-1, keepdims=True) + EPS)
    vhat = xc * rstd
    return rstd, vhat, vhat * gn + bn


def _sgu_fwd(proj, gn, bn, ws, bsc, tm=512):
    S = proj.shape[1]
    GW = D // G

    def body(p_ref, gn_ref, bn_ref, ws_ref, bs_ref, a_ref):
        ua = _gelu(p_ref[0].astype(f32))
        va = _gelu(p_ref[1].astype(f32))
        _, _, vn = _sgu_norm(va, gn_ref[...], bn_ref[...])
        vn = vn.astype(CDT)
        for ch in range(tm // C):
            rs = slice(ch * C, (ch + 1) * C)
            for gi in range(G):
                cs = slice(gi * GW, (gi + 1) * GW)
                s = _mm(ws_ref[gi], vn[rs, cs]) + bs_ref[gi]
                a_ref[rs, cs] = (ua[rs, cs] * s).astype(CDT)

    return pl.pallas_call(
        body, name="sgu_fwd", grid=(S // tm,),
        out_shape=jax.ShapeDtypeStruct((S, D), CDT),
        in_specs=[pl.BlockSpec((2, tm, D), lambda i: (0, i, 0)), _resident((1, D), lambda i: (0, 0)),
                  _resident((1, D), lambda i: (0, 0)), _resident((G, C, C), lambda i: (0, 0, 0)),
                  _resident((G, C, 1), lambda i: (0, 0, 0))],
        out_specs=pl.BlockSpec((tm, D), lambda i: (i, 0)),
        compiler_params=_params("arbitrary"),
    )(proj, gn, bn, ws, bsc)


def _decay_tables(dl_ref):
    lg = jax.nn.log_sigmoid(dl_ref[0:2, :])
    lgf, lgb = lg[0:1, :], lg[1:2, :]
    assert RC <= DK
    ri = lax.broadcasted_iota(jnp.int32, (RC, RC), 0)
    ci = lax.broadcasted_iota(jnp.int32, (RC, RC), 1)
    d = (ri - ci).astype(f32)
    lower = d >= 0
    dmat = jnp.where(lower, jnp.exp(d * lgf[:, :RC]), jnp.exp(-d * lgb[:, :RC]))
    dmat_t = jnp.where(d <= 0, jnp.exp(-d * lgf[:, :RC]), jnp.exp(d * lgb[:, :RC]))
    pos = lax.broadcasted_iota(jnp.int32, (RC, DK), 0).astype(f32)
    t = dict(
        lgf=lgf, lgb=lgb, d=d, lower=lower, dmat=dmat, dmat_t=dmat_t, pos=pos,
        fq=jnp.exp((pos + 1.0) * lgf), fk=jnp.exp((RC - 1.0 - pos) * lgf),
        bq=jnp.exp((RC - pos) * lgb), bk=jnp.exp(pos * lgb),
        lamf=jnp.exp(float(RC) * lgf), lamb=jnp.exp(float(RC) * lgb),
    )
    return t


def _rotate(t, co, si):
    t1, t2 = t[:, :DK // 2], t[:, DK // 2:]
    return jnp.concatenate([t1 * co - t2 * si, t2 * co + t1 * si], axis=-1)


def _unrotate(t, co, si):
    t1, t2 = t[:, :DK // 2], t[:, DK // 2:]
    return jnp.concatenate([t1 * co + t2 * si, t2 * co - t1 * si], axis=-1)


K_SCALE = DK ** -0.5
ROW_TILE = 256


def _ret_fwd(proj, dl, hooks=()):
    S = proj.shape[1]
    NC = S // RC

    def body(q_ref, k_ref, v_ref, g_ref, dl_ref, R_ref, r_ref, sfs_ref, sbs_ref, rb_ref, sf_ref, sb_ref):
        t = _decay_tables(dl_ref)

        def chunk(n):
            rows = pl.ds(pl.multiple_of(n * RC, RC), RC)
            return rows, q_ref[rows, :], k_ref[rows, :], v_ref[rows, :]

        sf_ref[...] = jnp.zeros_like(sf_ref)
        sb_ref[...] = jnp.zeros_like(sb_ref)

        def step(i, carry):
            rows, qn, kn, vn = chunk(i)
            sc = _mm_nt(qn, kn) * t["dmat"]
            out = _mm(sc.astype(CDT), vn)
            sf = sf_ref[...]
            sfb = sf.astype(CDT)
            sfs_ref[i] = sfb
            R_ref[rows, :] = out + _mm((qn.astype(f32) * t["fq"]).astype(CDT), sfb)
            sf_ref[...] = sf * t["lamf"] + _mm_tn((kn.astype(f32) * t["fk"]).astype(CDT), vn)
            m = NC - 1 - i
            rows, qn, kn, vn = chunk(m)
            sb = sb_ref[...]
            sbb = sb.astype(CDT)
            sbs_ref[m] = sbb
            rb_ref[rows, :] = _mm((qn.astype(f32) * t["bq"]).astype(CDT), sbb)
            sb_ref[...] = sb * t["lamb"] + _mm_tn((kn.astype(f32) * t["bk"]).astype(CDT), vn)
            return carry

        lax.fori_loop(0, NC, step, 0, unroll=4)

        def finish(i, carry):
            rs = pl.ds(pl.multiple_of(i * ROW_TILE, ROW_TILE), ROW_TILE)
            R = R_ref[rs, :] + rb_ref[rs, :]
            R_ref[rs, :] = R
            rn = R * lax.rsqrt(jnp.mean(R * R, axis=-1, keepdims=True) + EPS)
            g = g_ref[rs, :].astype(f32)
            r_ref[rs, :] = (rn * g * jax.nn.sigmoid(g)).astype(CDT)
            return carry

        lax.fori_loop(0, S // ROW_TILE, finish, 0)

    def seg(slot):
        return pl.BlockSpec((None, S, DK), lambda h: (slot, 0, h))

    states = jax.ShapeDtypeStruct((H, NC, DK, DK), CDT)
    state_blk = pl.BlockSpec((None, NC, DK, DK), lambda h: (h, 0, 0, 0))
    return _call(
        body, hooks, [proj, proj, proj, proj, dl], name="ret_fwd", grid=(H,),
        out_shape=[jax.ShapeDtypeStruct((S, H * DK), f32), jax.ShapeDtypeStruct((S, H * DK), CDT), states, states],
        in_specs=[seg(SLOT_Q), seg(SLOT_K), seg(SLOT_VR), seg(SLOT_GR), pl.BlockSpec((None, 8, DK), lambda h: (h, 0, 0))],
        out_specs=[pl.BlockSpec((S, DK), lambda h: (0, h)), pl.BlockSpec((S, DK), lambda h: (0, h)), state_blk, state_blk],
        scratch_shapes=[pltpu.VMEM((S, DK), f32), pltpu.VMEM((DK, DK), f32), pltpu.VMEM((DK, DK), f32)],
        compiler_params=_params("arbitrary"),
    )


def _merge_fwd(a, r, proj, x1, wa, wb, wo, hooks=(), tm=512):
    S = x1.shape[0]

    def body(a_ref, r_ref, gt_ref, x_ref, wa_ref, wb_ref, wo_ref, x2_ref, ya_ref, yb_ref):
        ya = _mm(a_ref[...], wa_ref[...])
        yb = _mm(r_ref[...], wb_ref[...])
        ya_ref[...] = ya.astype(CDT)
        yb_ref[...] = yb.astype(CDT)
        mix = jax.nn.sigmoid(gt_ref[0].astype(f32)) * ya + jax.nn.sigmoid(gt_ref[1].astype(f32)) * yb
        x2_ref[...] = x_ref[...] + _mm(mix.astype(CDT), wo_ref[...])

    tok = pl.BlockSpec((tm, D), lambda i: (i, 0))
    return _call(
        body, hooks, [a, r, proj, x1, wa, wb, wo], name="merge_fwd", grid=(S // tm,),
        out_shape=[jax.ShapeDtypeStruct((S, D), f32), jax.ShapeDtypeStruct((S, D), CDT), jax.ShapeDtypeStruct((S, D), CDT)],
        in_specs=[tok, tok, pl.BlockSpec((2, tm, D), lambda i: (SLOT_GA // 2, i, 0)), tok,
                  _wblock(wa), _wblock(wb), _wblock(wo)],
        out_specs=[tok, tok, tok],
        compiler_params=_params("arbitrary"),
    )


def _loss_head(x3, fn, target, tm=512):
    S = x3.shape[0]

    def body(x_ref, n_ref, t_ref, dx_ref, dxh_ref, dn_ref, l_ref):
        n = n_ref[...]
        r, xh, y = _rms_fwd(x_ref[...], n)
        e = y - t_ref[...]
        dy = e * (1.0 / D)
        dx, dn = _rms_bwd(dy, r, xh, n)
        dx_ref[...] = dx
        dxh_ref[...] = (0.5 * dx).astype(CDT)
        part = 0.5 * jnp.sum(jnp.sum(e * e, axis=-1, keepdims=True), axis=0, keepdims=True) * (1.0 / D)

        @pl.when(pl.program_id(0) == 0)
        def _():
            dn_ref[...] = jnp.zeros_like(dn_ref)
            l_ref[...] = jnp.zeros_like(l_ref)

        dn_ref[...] += dn
        l_ref[...] += jnp.broadcast_to(part, l_ref.shape)

    tok = pl.BlockSpec((tm, D), lambda i: (i, 0))
    return pl.pallas_call(
        body, name="loss_head", grid=(S // tm,),
        out_shape=[jax.ShapeDtypeStruct((S, D), f32), jax.ShapeDtypeStruct((S, D), CDT), jax.ShapeDtypeStruct((1, D), f32),
                   jax.ShapeDtypeStruct((8, 128), f32)],
        in_specs=[tok, _resident((1, D), lambda i: (0, 0)), tok],
        out_specs=[tok, tok, pl.BlockSpec((1, D), lambda i: (0, 0)), pl.BlockSpec((8, 128), lambda i: (0, 0))],
        compiler_params=_params("arbitrary"),
    )(x3, fn, target)


def _ffn_bwd_hidden(name, dyh, g, u, wd, hooks=(), tm=512):
    S = dyh.shape[0]

    def body(dyh_ref, g_ref, u_ref, wd_ref, dg_ref, du_ref):
        dyh = dyh_ref[...]
        for ci in range(F // FT):
            sl = slice(ci * FT, (ci + 1) * FT)
            da = _mm_nt(dyh, wd_ref[sl, :])
            gv = g_ref[:, sl].astype(f32)
            uv = u_ref[:, sl].astype(f32)
            s = jax.nn.sigmoid(gv)
            silu = gv * s
            du_ref[:, sl] = (da * silu).astype(CDT)
            dg_ref[:, sl] = (da * uv * (s + silu - silu * s)).astype(CDT)

    hid = pl.BlockSpec((tm, F), lambda i: (i, 0))
    hidden = jax.ShapeDtypeStruct((S, F), CDT)
    return _call(
        body, hooks, [dyh, g, u, wd], name=name, grid=(S // tm,), out_shape=[hidden, hidden],
        in_specs=[pl.BlockSpec((tm, D), lambda i: (i, 0)), hid, hid, _wblock(wd)], out_specs=[hid, hid],
        compiler_params=_params("arbitrary"),
    )


def _ffn_bwd_in(name, dy, x, dg, du, nrm, wg, wu, hooks=(), tm=512):
    S = x.shape[0]

    def body(dy_ref, x_ref, dg_ref, du_ref, n_ref, wg_ref, wu_ref, dx_ref, dn_ref, acc_ref):
        n = n_ref[...]
        r, xh, _ = _rms_fwd(x_ref[...], n)
        for ci in range(F // FT):
            sl = slice(ci * FT, (ci + 1) * FT)
            dh = _mm(dg_ref[:, sl], wg_ref[sl, :]) + _mm(du_ref[:, sl], wu_ref[sl, :])
            if ci == 0:
                acc_ref[...] = dh
            else:
                acc_ref[...] += dh
        dx, dn = _rms_bwd(acc_ref[...], r, xh, n)
        dx_ref[...] = dy_ref[...] + dx

        @pl.when(pl.program_id(0) == 0)
        def _():
            dn_ref[...] = jnp.zeros_like(dn_ref)

        dn_ref[...] += dn

    tok = pl.BlockSpec((tm, D), lambda i: (i, 0))
    hid = pl.BlockSpec((tm, F), lambda i: (i, 0))
    return _call(
        body, hooks, [dy, x, dg, du, nrm, wg, wu], name=name, grid=(S // tm,),
        out_shape=[jax.ShapeDtypeStruct((S, D), f32), jax.ShapeDtypeStruct((1, D), f32)],
        in_specs=[tok, tok, hid, hid, _resident((1, D), lambda i: (0, 0)), _wblock(wg), _wblock(wu)],
        out_specs=[tok, pl.BlockSpec((1, D), lambda i: (0, 0))],
        scratch_shapes=[pltpu.VMEM((tm, D), f32)],
        compiler_params=_params("arbitrary"),
    )


TN_ROWS = 512


def _tn(name, xs, ys, block_of, hooks=()):
    S, M = xs.shape
    B = ys.shape[0]
    tr = TN_ROWS if M % TN_ROWS == 0 else M // 2
    assert M % tr == 0 and tr % 128 == 0
    nt = M // tr

    def body(x_ref, y_ref, o_ref):
        o_ref[...] = _mm_tn(x_ref[...], y_ref[...]).astype(CDT)

    return _call(
        body, hooks, [xs, ys], name=name, grid=(B, nt),
        out_shape=[jax.ShapeDtypeStruct((B * M, D), CDT)],
        in_specs=[pl.BlockSpec((S, tr), lambda b, i: (0, i)), pl.BlockSpec((None, S, D), lambda b, i: (b, 0, 0))],
        out_specs=[pl.BlockSpec((tr, D), lambda b, i: (block_of(b) * nt + i, 0))],
        compiler_params=_params("arbitrary", "arbitrary"),
    )


def _wgrad(name, xs, y, hooks=()):
    return _tn(name, xs, y[None], lambda b: 0, hooks)


def _merge_bwd_act(dx2, ya, yb, proj, wa, wb, wo, hooks=(), tm=512):
    S = dx2.shape[0]

    def body(dx_ref, ya_ref, yb_ref, gt_ref, wa_ref, wb_ref, wo_ref,
             dp_ref, da_ref, dr_ref, mix_ref, dxb_ref, dya_ref, dyb_ref):
        dxb = dx_ref[...].astype(CDT)
        dxb_ref[...] = dxb
        dmix = _mm_nt(dxb, wo_ref[...])
        ya = ya_ref[...].astype(f32)
        yb = yb_ref[...].astype(f32)
        sa = jax.nn.sigmoid(gt_ref[0].astype(f32))
        sb = jax.nn.sigmoid(gt_ref[1].astype(f32))
        mix_ref[...] = (sa * ya + sb * yb).astype(CDT)
        dya = (dmix * sa).astype(CDT)
        dyb = (dmix * sb).astype(CDT)
        dya_ref[...] = dya
        dyb_ref[...] = dyb
        dp_ref[0] = (dmix * ya * sa * (1.0 - sa)).astype(CDT)
        dp_ref[1] = (dmix * yb * sb * (1.0 - sb)).astype(CDT)
        da_ref[...] = _mm_nt(dya, wa_ref[...]).astype(CDT)
        dr_ref[...] = _mm_nt(dyb, wb_ref[...]).astype(CDT)

    tok = pl.BlockSpec((tm, D), lambda i: (i, 0))
    gates = pl.BlockSpec((2, tm, D), lambda i: (SLOT_GA // 2, i, 0))
    act = jax.ShapeDtypeStruct((S, D), CDT)
    return _call(
        body, hooks, [dx2, ya, yb, proj, wa, wb, wo], name="merge_bwd_act", grid=(S // tm,),
        out_shape=[jax.ShapeDtypeStruct((8, S, D), CDT), act, act, act, act, act, act],
        in_specs=[tok, tok, tok, gates, _wblock(wa), _wblock(wb), _wblock(wo)],
        out_specs=[gates, tok, tok, tok, tok, tok, tok],
        compiler_params=_params("arbitrary"),
    )


def _sgu_bwd(da, proj, dproj, gn, bn, ws, wst, bsc, hooks=(), tm=512):
    S = proj.shape[1]
    GW = D // G

    def body(da_ref, p_ref, dpin_ref, gn_ref, bn_ref, ws_ref, wst_ref, bs_ref,
             dp_ref, dws_ref, dbs_ref, dgn_ref, dbn_ref, ds_ref, dvn_ref):
        @pl.when(pl.program_id(0) == 0)
        def _():
            dws_ref[...] = jnp.zeros_like(dws_ref)
            dbs_ref[...] = jnp.zeros_like(dbs_ref)
            dgn_ref[...] = jnp.zeros_like(dgn_ref)
            dbn_ref[...] = jnp.zeros_like(dbn_ref)

        pu = p_ref[0].astype(f32)
        pv = p_ref[1].astype(f32)
        ua, dua = _gelu_and_grad(pu)
        va, dva_dpv = _gelu_and_grad(pv)
        gn = gn_ref[...]
        rstd, vhat, vn = _sgu_norm(va, gn, bn_ref[...])
        vnb = vn.astype(CDT)
        dav = da_ref[...].astype(f32)
        dsb = (dav * ua).astype(CDT)
        ones = jnp.ones((8, GW), CDT)
        for ch in range(tm // C):
            rs = slice(ch * C, (ch + 1) * C)
            for gi in range(G):
                cs = slice(gi * GW, (gi + 1) * GW)
                s = _mm(ws_ref[gi], vnb[rs, cs]) + bs_ref[gi]
                ds_ref[rs, cs] = s
                dsg = dsb[rs, cs]
                dws_ref[gi] += _mm_nt(dsg, vnb[rs, cs])
                dbs_ref[gi] += _mm_nt(ones, dsg)
                dvn_ref[rs, cs] = _mm(wst_ref[gi], dsg)
        dp_ref[0] = (dav * ds_ref[...] * dua).astype(CDT)
        dvn = dvn_ref[...]
        dgn_ref[...] += jnp.sum(dvn * vhat, axis=0, keepdims=True)
        dbn_ref[...] += jnp.sum(dvn, axis=0, keepdims=True)
        dvh = dvn * gn
        dva = rstd * (dvh - jnp.mean(dvh, axis=-1, keepdims=True) - vhat * jnp.mean(dvh * vhat, axis=-1, keepdims=True))
        dp_ref[1] = (dva * dva_dpv).astype(CDT)

    uv = pl.BlockSpec((2, tm, D), lambda i: (0, i, 0))
    row = _resident((1, D), lambda i: (0, 0))
    return _call(
        body, hooks, [da, proj, dproj, gn, bn, ws, wst, bsc], name="sgu_bwd", grid=(S // tm,),
        out_shape=[jax.ShapeDtypeStruct(dproj.shape, CDT), jax.ShapeDtypeStruct((G, C, C), f32),
                   jax.ShapeDtypeStruct((G, 8, C), f32), jax.ShapeDtypeStruct((1, D), f32), jax.ShapeDtypeStruct((1, D), f32)],
        in_specs=[pl.BlockSpec((tm, D), lambda i: (i, 0)), uv, _HBM, row, row,
                  _resident((G, C, C), lambda i: (0, 0, 0)), _resident((G, C, C), lambda i: (0, 0, 0)),
                  _resident((G, C, 1), lambda i: (0, 0, 0))],
        out_specs=[uv, pl.BlockSpec((G, C, C), lambda i: (0, 0, 0)), pl.BlockSpec((G, 8, C), lambda i: (0, 0, 0)),
                   pl.BlockSpec((1, D), lambda i: (0, 0)), pl.BlockSpec((1, D), lambda i: (0, 0))],
        scratch_shapes=[pltpu.VMEM((tm, D), f32), pltpu.VMEM((tm, D), f32)],
        input_output_aliases={2: 0},
        compiler_params=_params("arbitrary"),
    )


def _ret_bwd(dr, R, sfs, sbs, proj, dproj, cos, sin, dl, hooks=()):
    S = proj.shape[1]
    NC = S // RC
    assert NC % 2 == 0

    def body(dr_ref, R_ref, sf_ref, sb_ref, q_ref, k_ref, v_ref, g_ref, dpin_ref, cos_ref, sin_ref, dl_ref,
             dp_ref, dd_ref, dR_ref, gb_ref, gf_ref, acc_ref):
        t = _decay_tables(dl_ref)

        def gate_norm_bwd(i, carry):
            rs = pl.ds(pl.multiple_of(i * ROW_TILE, ROW_TILE), ROW_TILE)
            Rv = R_ref[rs, :]
            rstd = lax.rsqrt(jnp.mean(Rv * Rv, axis=-1, keepdims=True) + EPS)
            rn = Rv * rstd
            gv = g_ref[rs, :].astype(f32)
            s = jax.nn.sigmoid(gv)
            drv = dr_ref[rs, :].astype(f32)
            dp_ref[3, rs, :] = (drv * rn * (s * (1.0 + gv * (1.0 - s)))).astype(CDT)
            drn = drv * gv * s
            dR_ref[rs, :] = (rstd * (drn - rn * jnp.mean(drn * rn, axis=-1, keepdims=True))).astype(CDT)
            return carry

        lax.fori_loop(0, S // ROW_TILE, gate_norm_bwd, 0)

        def chunk(n):
            rows = pl.ds(pl.multiple_of(n * RC, RC), RC)
            return rows, q_ref[rows, :], k_ref[rows, :], v_ref[rows, :], dR_ref[rows, :]

        def emit_kv(rows, dk, dv, final):
            if not final:
                dp_ref[1, rows, :] = dk.astype(CDT)
                dp_ref[2, rows, :] = dv.astype(CDT)
            else:
                co, si = cos_ref[rows, :], sin_ref[rows, :]
                dk = dp_ref[1, rows, :].astype(f32) + dk
                dp_ref[1, rows, :] = (_unrotate(dk, co, si) * K_SCALE).astype(CDT)
                dp_ref[2, rows, :] = (dp_ref[2, rows, :].astype(f32) + dv).astype(CDT)

        gb_ref[...] = jnp.zeros_like(gb_ref)
        gf_ref[...] = jnp.zeros_like(gf_ref)
        acc_ref[...] = jnp.zeros_like(acc_ref)
        span = t["dmat"] * jnp.abs(t["d"])
        span_f, span_b = jnp.where(t["lower"], span, 0.0), jnp.where(t["lower"], 0.0, span)

        def ascend(n, final):
            rows, qn, kn, vn, dRn = chunk(n)
            qf, kf = qn.astype(f32), kn.astype(f32)
            sc = _mm_nt(qn, kn)
            dA = _mm_nt(dRn, vn)
            prod = sc * dA
            lgf_part = jnp.sum(prod * span_f, axis=0, keepdims=True)
            lgb_part = jnp.sum(prod * span_b, axis=0, keepdims=True)
            dsc = (dA * t["dmat"]).astype(CDT)
            dq = _mm(dsc, kn)
            scT = (_mm_nt(kn, qn) * t["dmat_t"]).astype(CDT)
            dscT = (_mm_nt(vn, dRn) * t["dmat_t"]).astype(CDT)
            dk = _mm(dscT, qn)
            dv = _mm(scT, dRn)
            sfb = sf_ref[n]
            sbb = sb_ref[n]
            qdf = qf * t["fq"]
            dqdf = _mm_nt(dRn, sfb)
            dq += dqdf * t["fq"]
            lgf_row = jnp.sum(qdf * dqdf * (t["pos"] + 1.0), axis=0, keepdims=True)
            qdb = qf * t["bq"]
            dqdb = _mm_nt(dRn, sbb)
            dq += dqdb * t["bq"]
            lgb_row = jnp.sum(qdb * dqdb * (RC - t["pos"]), axis=0, keepdims=True)
            gb = gb_ref[...]
            gbb = gb.astype(CDT)
            kdb = kf * t["bk"]
            dkdb = _mm_nt(vn, gbb)
            dk += dkdb * t["bk"]
            dv += _mm(kdb.astype(CDT), gbb)
            lgb_row += jnp.sum(kdb * dkdb * t["pos"], axis=0, keepdims=True)
            lgb_row += float(RC) * t["lamb"] * jnp.sum(gb * sbb.astype(f32), axis=0, keepdims=True)
            co, si = cos_ref[rows, :], sin_ref[rows, :]
            dp_ref[0, rows, :] = _unrotate(dq, co, si).astype(CDT)
            emit_kv(rows, dk, dv, final)
            acc_ref[0:1, :] += lgf_row + lgf_part
            acc_ref[1:2, :] += lgb_row + lgb_part
            gb_ref[...] = gb * t["lamb"] + _mm_tn(qdb.astype(CDT), dRn)

        def descend(n, final):
            rows, qn, kn, vn, dRn = chunk(n)
            gf = gf_ref[...]
            gfb = gf.astype(CDT)
            kdf = kn.astype(f32) * t["fk"]
            dkdf = _mm_nt(vn, gfb)
            lgf_row = jnp.sum(kdf * dkdf * (RC - 1.0 - t["pos"]), axis=0, keepdims=True)
            lgf_row += float(RC) * t["lamf"] * jnp.sum(gf * sf_ref[n].astype(f32), axis=0, keepdims=True)
            acc_ref[0:1, :] += lgf_row
            emit_kv(rows, dkdf * t["fk"], _mm(kdf.astype(CDT), gfb), final)
            gf_ref[...] = gf * t["lamf"] + _mm_tn((qn.astype(f32) * t["fq"]).astype(CDT), dRn)

        def sweep(final):
            def step(i, carry):
                ascend(i, final)
                descend(NC - 1 - i, final)
                return carry
            return step

        lax.fori_loop(0, NC // 2, sweep(False), 0, unroll=4)
        lax.fori_loop(NC // 2, NC, sweep(True), 0, unroll=4)
        dlg = jnp.sum(acc_ref[...], axis=1, keepdims=True)
        dlogit = dlg * jax.nn.sigmoid(-dl_ref[:, 0:1])
        lane = lax.broadcasted_iota(jnp.int32, (8, 128), 1)
        dd_ref[...] = jnp.where(lane == pl.program_id(0), jnp.broadcast_to(dlogit, (8, 128)), 0.0)

    def seg(slot):
        return pl.BlockSpec((None, S, DK), lambda h: (slot, 0, h))

    head = pl.BlockSpec((S, DK), lambda h: (0, h))
    states = pl.BlockSpec((None, NC, DK, DK), lambda h: (h, 0, 0, 0))
    return _call(
        body, hooks, [dr, R, sfs, sbs, proj, proj, proj, proj, dproj, cos, sin, dl], name="ret_bwd", grid=(H,),
        out_shape=[jax.ShapeDtypeStruct(dproj.shape, CDT), jax.ShapeDtypeStruct((H, 8, 128), f32)],
        in_specs=[head, head, states, states, seg(SLOT_Q), seg(SLOT_K), seg(SLOT_VR), seg(SLOT_GR), _HBM,
                  _resident((S, DK // 2), lambda h: (0, 0)), _resident((S, DK // 2), lambda h: (0, 0)),
                  pl.BlockSpec((None, 8, DK), lambda h: (h, 0, 0))],
        out_specs=[pl.BlockSpec((4, S, DK), lambda h: (1, 0, h), pipeline_mode=pl.Buffered(1)),
                   pl.BlockSpec((None, 8, 128), lambda h: (h, 0, 0))],
        scratch_shapes=[pltpu.VMEM((S, DK), CDT),
                        pltpu.VMEM((DK, DK), f32), pltpu.VMEM((DK, DK), f32), pltpu.VMEM((8, DK), f32)],
        input_output_aliases={8: 0},
        compiler_params=_params("arbitrary"),
    )


def _proj_bwd_act(dproj, dx2, x1, nrm, wfull, hooks=(), tm=512):
    S = x1.shape[0]

    def body(dp_ref, dx2_ref, x_ref, n_ref, w_ref, dx_ref, dxh_ref, dn_ref, db_ref, acc_ref):
        @pl.when(pl.program_id(0) == 0)
        def _():
            dn_ref[...] = jnp.zeros_like(dn_ref)
            db_ref[...] = jnp.zeros_like(db_ref)

        for p in range(8):
            seg = SEG_OF_SLOT[p]
            dp = dp_ref[p]
            db_ref[seg] += jnp.sum(dp.astype(f32), axis=0, keepdims=True)
            dh = _mm_nt(dp, w_ref[seg * D:(seg + 1) * D, :])
            if p == 0:
                acc_ref[...] = dh
            else:
                acc_ref[...] += dh
        n = n_ref[...]
        r, xh, _ = _rms_fwd(x_ref[...], n)
        dx, dn = _rms_bwd(acc_ref[...], r, xh, n)
        dx = dx2_ref[...] + dx
        dx_ref[...] = dx
        dxh_ref[...] = (0.5 * dx).astype(CDT)
        dn_ref[...] += dn

    tok = pl.BlockSpec((tm, D), lambda i: (i, 0))
    return _call(
        body, hooks, [dproj, dx2, x1, nrm, wfull], name="proj_bwd_act", grid=(S // tm,),
        out_shape=[jax.ShapeDtypeStruct((S, D), f32), jax.ShapeDtypeStruct((S, D), CDT), jax.ShapeDtypeStruct((1, D), f32),
                   jax.ShapeDtypeStruct((8, 1, D), f32)],
        in_specs=[pl.BlockSpec((8, tm, D), lambda i: (0, i, 0)), tok, tok, _resident((1, D), lambda i: (0, 0)),
                  _resident((8 * D, D), lambda i: (0, 0))],
        out_specs=[tok, tok, pl.BlockSpec((1, D), lambda i: (0, 0)), pl.BlockSpec((8, 1, D), lambda i: (0, 0, 0))],
        scratch_shapes=[pltpu.VMEM((tm, D), f32)],
        compiler_params=_params("arbitrary"),
    )


def _rs_sum(name, gfulls, lands, my_c):
    n = len(gfulls)
    rows = gfulls[0].shape[0] // NDEV
    assert all(g.shape[0] == NDEV * rows for g in gfulls)

    def body(c_ref, *refs):
        for g_ref, l_ref, o_ref in zip(refs[:n], refs[n:2 * n], refs[2 * n:]):
            o_ref[...] = (g_ref[...].astype(f32) + l_ref[...].astype(f32)).astype(CDT)

    slot = pl.BlockSpec((None, rows, D), lambda k, c: (k, 0, 0))
    return pl.pallas_call(
        body, name=name,
        grid_spec=pltpu.PrefetchScalarGridSpec(
            num_scalar_prefetch=1, grid=(NCHIP,),
            in_specs=[pl.BlockSpec((rows, D), lambda k, c: (2 * k + c[0], 0))] * n + [slot] * n,
            out_specs=[slot] * n),
        out_shape=[jax.ShapeDtypeStruct((NCHIP, rows, D), CDT)] * n,
        compiler_params=_params("arbitrary"),
    )(my_c, *gfulls, *lands)


def _adamw_math(g, w, m, v):
    m2 = ADAM_B1 * m + (1.0 - ADAM_B1) * g
    v2 = ADAM_B2 * v + (1.0 - ADAM_B2) * (g * g)
    delta = -ADAM_LR * ((m2 / BC1) / (jnp.sqrt(v2 / BC2) + ADAM_EPS) + ADAM_WD * w)
    return delta, m2, v2


def _adamw_big(name, landed, w, m, v, after):
    n = len(w)
    rows = w[0].shape[0]
    tr = 256 if rows % 256 == 0 else (rows // 2 if rows > 256 else rows)
    nt = rows // tr

    def body(*refs):
        ins, outs = refs[:4 * n], refs[4 * n + 1:]
        for j in range(n):
            @pl.when(pl.program_id(0) == j)
            def _(j=j):
                l_ref, w_ref, m_ref, v_ref = ins[j], ins[n + j], ins[2 * n + j], ins[3 * n + j]
                g = l_ref[0].astype(f32)
                for k in range(1, NCHIP):
                    g = g + l_ref[k].astype(f32)
                outs[4 * j][...] = g
                outs[4 * j + 1][...], outs[4 * j + 2][...], outs[4 * j + 3][...] = _adamw_math(g, w_ref[...], m_ref[...], v_ref[...])

    def tile(j):
        return lambda jj, i: jnp.clip((jj - j) * nt + i, 0, nt - 1)

    blk = [pl.BlockSpec((tr, D), lambda jj, i, t=tile(j): (t(jj, i), 0)) for j in range(n)]
    lnd = [pl.BlockSpec((NCHIP, tr, D), lambda jj, i, t=tile(j): (0, t(jj, i), 0)) for j in range(n)]
    o = jax.ShapeDtypeStruct((rows, D), f32)
    res = pl.pallas_call(
        body, name=name, grid=(n, nt), out_shape=[o] * (4 * n),
        in_specs=lnd + blk + blk + blk + [_HBM],
        out_specs=[blk[j] for j in range(n) for _ in range(4)],
        compiler_params=_params("arbitrary", "arbitrary"),
    )(*landed, *w, *m, *v, after)
    return [res[4 * j:4 * j + 4] for j in range(n)]


ROW_FFN1_NORM, ROW_MIX_NORM, ROW_SGU_G, ROW_SGU_B, ROW_FFN2_NORM, ROW_FINAL_NORM, ROW_B_IN = 0, 1, 2, 3, 4, 5, 8
ROW_WS, ROW_BS = 0, G * C


def _adamw_small(ga, gs, gd, gn1, gl, params):
    def body(ga_ref, gs_ref, gd_ref, gn1_ref, gl_ref, *refs):
        ins, outs = refs[:30], refs[30:]

        def total(ref, r0, n):
            g = ref[0, r0:r0 + n, :]
            for j in range(1, NDEV):
                g = g + ref[j, r0:r0 + n, :]
            return g

        def apply(i, g, rows=slice(None), cols=slice(None)):
            w, m, v = ins[3 * i][rows, cols], ins[3 * i + 1][rows, cols], ins[3 * i + 2][rows, cols]
            outs[4 * i][rows, cols] = g
            outs[4 * i + 1][rows, cols], outs[4 * i + 2][rows, cols], outs[4 * i + 3][rows, cols] = _adamw_math(g, w, m, v)

        outs[40][...] = total(gl_ref, 0, 8)[0:1, 0:1]
        apply(0, total(gn1_ref, 0, 1))
        for i, r in enumerate((ROW_FFN1_NORM, ROW_MIX_NORM, ROW_SGU_G, ROW_SGU_B, ROW_FFN2_NORM, ROW_FINAL_NORM)):
            if i:
                apply(i, total(ga_ref, r, 1))
        for k in range(8):
            apply(6, total(ga_ref, ROW_B_IN + k, 1), cols=slice(k * D, (k + 1) * D))
        apply(7, total(gs_ref, ROW_WS, G * C))
        for gi in range(G):
            apply(8, total(gs_ref, ROW_BS + 8 * gi, 1), slice(gi, gi + 1))
        dec = total(gd_ref, 0, 8)
        for hh in range(1, H):
            dec = dec + total(gd_ref, 8 * hh, 8)
        apply(9, dec[0:2, 0:H])

    flat = [a for p in params for a in p]
    out_shape = [jax.ShapeDtypeStruct(p[0].shape, f32) for p in params for _ in range(4)]
    out_shape.append(jax.ShapeDtypeStruct((1, 1), f32))
    vm = pl.BlockSpec(memory_space=pltpu.VMEM)
    return pl.pallas_call(
        body, name="adamw_small", out_shape=out_shape,
        in_specs=[vm] * (5 + len(flat)), out_specs=[vm] * len(out_shape),
        compiler_params=pltpu.CompilerParams(vmem_limit_bytes=VMEM_LIMIT),
    )(ga, gs, gd, gn1, gl, *flat)


def kernel(x, ffn1_norm, ffn1_w_gate, ffn1_w_up, ffn1_w_down, mix_norm, w_in, b_in, sgu_norm_g, sgu_norm_b, sgu_w_s, sgu_b_s, ret_decay_logit, w_branch_a, w_branch_b, w_out, ffn2_norm, ffn2_w_gate, ffn2_w_up, ffn2_w_down, final_norm, loss_target, m_ffn1_norm, m_ffn1_w_gate, m_ffn1_w_up, m_ffn1_w_down, m_mix_norm, m_w_in, m_b_in, m_sgu_norm_g, m_sgu_norm_b, m_sgu_w_s, m_sgu_b_s, m_ret_decay_logit, m_w_branch_a, m_w_branch_b, m_w_out, m_ffn2_norm, m_ffn2_w_gate, m_ffn2_w_up, m_ffn2_w_down, m_final_norm, v_ffn1_norm, v_ffn1_w_gate, v_ffn1_w_up, v_ffn1_w_down, v_mix_norm, v_w_in, v_b_in, v_sgu_norm_g, v_sgu_norm_b, v_sgu_w_s, v_sgu_b_s, v_ret_decay_logit, v_w_branch_a, v_w_branch_b, v_w_out, v_ffn2_norm, v_ffn2_w_gate, v_ffn2_w_up, v_ffn2_w_down, v_final_norm):
    args = dict(locals())
    S = x.shape[1]
    xs = x[0]
    target = loss_target[0]

    def buf_layout(name, a):
        a = a[0]
        return a.T if name in W_TRANSPOSED else a

    sh = {n: buf_layout(n, args[n]).astype(CDT) for n in W_NAMES}
    wf = {}

    b3 = b_in.reshape(8, 1, D)
    ws = sgu_w_s[0].astype(CDT)
    wst = jnp.swapaxes(sgu_w_s[0], 1, 2).astype(CDT)
    bsc = sgu_b_s[0].reshape(G, C, 1)
    dl = jnp.zeros((H, 8, DK), f32).at[:, 0:2, :].set(jnp.broadcast_to(ret_decay_logit[0].T[:, :, None], (H, 2, DK)))
    theta = ROPE_BASE ** (-jnp.arange(0, DK, 2, dtype=f32) / DK)
    ang = jnp.arange(S, dtype=f32)[:, None] * theta[None, :]
    cos, sin = jnp.cos(ang), jnp.sin(ang)
    fnorm = final_norm.reshape(1, D)

    f1 = ("ffn1_w_gate", "ffn1_w_up", "ffn1_w_down")
    f2 = ("ffn2_w_gate", "ffn2_w_up", "ffn2_w_down")
    br = ("w_branch_a", "w_branch_b", "w_out")
    for cid, names in enumerate((f1, ("w_in",), br, f2)):
        wf.update(zip(names, _sequence("ag_" + names[0], 1 + cid, NEIGHBOURS, [_ag_hook(sh[n]) for n in names])))
    x1, g1, u1, a1, hf1 = _ffn_fwd("ffn1_fwd", xs, ffn1_norm, *[wf[n] for n in f1])
    proj, h2 = _proj_fwd(x1, mix_norm, wf["w_in"], b3, cos, sin)
    a = _sgu_fwd(proj, sgu_norm_g, sgu_norm_b, ws, bsc)
    R, r, sfs, sbs = _ret_fwd(proj, dl)
    x2, ya, yb = _merge_fwd(a, r, proj, x1, *[wf[n] for n in br])
    x3, g2, u2, a2, hf2 = _ffn_fwd("ffn2_fwd", x2, ffn2_norm, *[wf[n] for n in f2])
    dx3, dyh2, d_final, loss_part = _loss_head(x3, fnorm, target)

    my_c = lax.axis_index("c").astype(jnp.int32).reshape(1)
    gw, landed, sequenced = {}, {}, []

    def d2d(*names):
        return [_rs_d2d_hook(gw[n]) for n in names]

    def behind(x, token):
        return lax.optimization_barrier((x, token))[0]

    def to_chips(names, sibs, more=()):
        parts = list(_rs_sum("rs_sum_" + names[0], [gw[n] for n in names], list(sibs), my_c))
        token = parts[0]
        if sequenced:
            parts[0] = behind(parts[0], sequenced[-1])
        hooks = [_rs_ici_hook(p) for p in parts] + ([_small_hook(list(more))] if more else [])
        reach = EVERYONE if more else CHIPS
        got = _sequence("rs_chips_" + names[0], REACH_ID[reach], reach, hooks)
        sequenced.append(got[0])
        landed.update(zip(names, got))
        return got[len(names):], token

    def ffn_bwd(tag, names, dy, dyh, x, g, u, a, h, nrm, each_alone, more=()):
        wg, wu, wd = names
        (gw[wd],) = _wgrad(tag + "_wd_grad", a, dyh)
        dg, du, sib_d = _ffn_bwd_hidden(tag + "_bwd_hidden", dyh, g, u, wf[wd], d2d(wd))
        if each_alone:
            dg = behind(dg, to_chips([wd], [sib_d])[1])
        (gw[wg],) = _wgrad(tag + "_wg_grad", dg, h)
        gw[wu], sib_g = _wgrad(tag + "_wu_grad", du, h, d2d(wg))
        if each_alone:
            (sib_u,) = _sequence("rs_sib_" + wu, SIBLING_ID, SIBLING, [_rs_d2d_hook(behind(gw[wu], sequenced[-1]))])
            sequenced.append(sib_u)
            dy = behind(dy, to_chips([wg], [sib_g])[1])
            dx, dn = _ffn_bwd_in(tag + "_bwd_in", dy, x, dg, du, nrm, wf[wg], wf[wu])
            return (dx, dn) + to_chips([wu], [sib_u], more(dn))
        dx, dn, sib_u = _ffn_bwd_in(tag + "_bwd_in", dy, x, dg, du, nrm, wf[wg], wf[wu], d2d(wu))
        return (dx, dn) + to_chips([wd, wg, wu], [sib_d, sib_g, sib_u])

    dx2, d_ffn2n, _, token = ffn_bwd("ffn2", f2, dx3, dyh2, x2, g2, u2, a2, hf2, ffn2_norm, False)
    dproj, da, dr, mix, dx2b, dya, dyb = _merge_bwd_act(behind(dx2, token), ya, yb, proj, *[wf[n] for n in br])
    dproj, d_ws, d_bs, d_gn, d_bn = _sgu_bwd(da, proj, dproj, sgu_norm_g, sgu_norm_b, ws, wst, bsc)
    dproj, d_dec = _ret_bwd(dr, R, sfs, sbs, proj, dproj, cos, sin, dl)
    (gw["w_in"],) = _tn("win_grad", h2, dproj, _seg_of_slot)
    gw["w_out"], sib_win = _wgrad("wo_grad", mix, dx2b, d2d("w_in"))
    small_sgu = jnp.concatenate([d_ws.reshape(G * C, C), d_bs.reshape(G * 8, C)], axis=0)
    (g_sgu, gl), token = to_chips(["w_in"], [sib_win], [small_sgu, loss_part])
    (gw["w_branch_a"],) = _wgrad("wa_grad", a, behind(dya, token))
    (gw["w_branch_b"],) = _wgrad("wb_grad", r, dyb)
    dx1, dyh1, d_mixn, d_bin, *sib_br = _proj_bwd_act(behind(dproj, token), dx2, x1, mix_norm, wf["w_in"], d2d(*br))
    small_a = jnp.concatenate([jnp.zeros((1, D), f32), d_mixn, d_gn, d_bn, d_ffn2n, d_final, jnp.zeros((2, D), f32),
                               d_bin.reshape(8, D)], axis=0)
    (ga, g_dec), token = to_chips(list(br), sib_br, [small_a, d_dec.reshape(H * 8, 128)])
    dxs, d_ffn1n, (gn1,), token = ffn_bwd("ffn1", f1, dx1, behind(dyh1, token), xs, g1, u1, a1, hf1, ffn1_norm, True,
                                          lambda dn: [dn])

    out = {"grad_x": dxs[None]}

    def native(name, a):
        a = a.T if name in W_TRANSPOSED else a
        return a[None]

    after = token
    for names in (f2, ("w_in",), br, (f1[2], f1[0], f1[1])):
        res = _adamw_big("adamw_" + names[0], [landed[n] for n in names], [buf_layout(n, args[n]) for n in names],
                         [buf_layout(n, args["m_" + n]) for n in names], [buf_layout(n, args["v_" + n]) for n in names], after)
        after = res[-1][0]
        for n, four in zip(names, res):
            for pre, val in zip(("grad_", "delta_", "new_m_", "new_v_"), four):
                out[pre + n] = native(n, val)

    small = [
        ("ffn1_norm", lambda a: a, lambda a: a), ("mix_norm", lambda a: a, lambda a: a),
        ("sgu_norm_g", lambda a: a, lambda a: a), ("sgu_norm_b", lambda a: a, lambda a: a),
        ("ffn2_norm", lambda a: a, lambda a: a),
        ("final_norm", lambda a: a.reshape(1, D), lambda a: a.reshape(D)),
        ("b_in", lambda a: a, lambda a: a),
        ("sgu_w_s", lambda a: a.reshape(G * C, C), lambda a: a.reshape(1, G, C, C)),
        ("sgu_b_s", lambda a: a[0], lambda a: a[None]),
        ("ret_decay_logit", lambda a: a[0], lambda a: a[None]),
    ]
    res = _adamw_small(ga, g_sgu, g_dec, gn1, gl,
                       [(to(args[n]), to(args["m_" + n]), to(args["v_" + n])) for n, to, _ in small])
    out["loss"] = res[40].reshape(())
    for i, (n, _, back) in enumerate(small):
        for j, pre in enumerate(("grad_", "delta_", "new_m_", "new_v_")):
            out[pre + n] = back(res[4 * i + j])

    weights = ("ffn1_norm", "ffn1_w_gate", "ffn1_w_up", "ffn1_w_down", "mix_norm", "w_in", "b_in", "sgu_norm_g",
               "sgu_norm_b", "sgu_w_s", "sgu_b_s", "ret_decay_logit", "w_branch_a", "w_branch_b", "w_out", "ffn2_norm",
               "ffn2_w_gate", "ffn2_w_up", "ffn2_w_down", "final_norm")
    return (out["loss"], out["grad_x"], *[out["grad_" + n] for n in weights], *[out["delta_" + n] for n in weights],
            *[out["new_m_" + n] for n in weights], *[out["new_v_" + n] for n in weights])
```

```python
import math

import jax
import jax.numpy as jnp
from jax import lax
from jax.experimental import pallas as pl
from jax.experimental.pallas import tpu as pltpu
from jax.experimental.pallas import tpu_sc as plsc

f32 = jnp.float32
CDT = jnp.bfloat16

D = 1024
F = 2816
C = 128
RC = 256
H = 4
DK = 256
G = 4
NDEV = 8
NCHIP = 4
EPS = 1e-6
ROPE_BASE = 10000.0
FT = 256
V7X_VMEM_BYTES = 64 * 1024 * 1024
VMEM_LIMIT = V7X_VMEM_BYTES - 8 * 1024 * 1024

ADAM_LR, ADAM_B1, ADAM_B2, ADAM_EPS, ADAM_WD, ADAM_STEP = 0.001, 0.9, 0.999, 1e-08, 0.01, 10
BC1 = 1.0 - ADAM_B1 ** ADAM_STEP
BC2 = 1.0 - ADAM_B2 ** ADAM_STEP

W_ROWS = dict(ffn1_w_gate=352, ffn1_w_up=352, ffn1_w_down=352, w_in=1024, w_branch_a=128, w_branch_b=128, w_out=128,
              ffn2_w_gate=352, ffn2_w_up=352, ffn2_w_down=352)
W_NAMES = tuple(W_ROWS)
W_TRANSPOSED = ("ffn1_w_gate", "ffn1_w_up", "ffn2_w_gate", "ffn2_w_up")

SLOT_U, SLOT_V, SLOT_GA, SLOT_GB, SLOT_Q, SLOT_K, SLOT_VR, SLOT_GR = range(8)


SEG_OF_SLOT = (0, 1, 6, 7, 2, 3, 4, 5)


def _seg_of_slot(p):
    return jnp.where(p < 2, p, jnp.where(p < 4, p + 4, p - 2))


def _mm(a, b):
    return jnp.dot(a, b, preferred_element_type=f32)


def _mm_nt(a, b):
    return lax.dot_general(a, b, (((1,), (1,)), ((), ())), preferred_element_type=f32)


def _mm_tn(a, b):
    return lax.dot_general(a, b, (((0,), (0,)), ((), ())), preferred_element_type=f32)


def _params(*sem):
    return pltpu.CompilerParams(dimension_semantics=sem, vmem_limit_bytes=VMEM_LIMIT)


def _resident(shape, index_map):
    return pl.BlockSpec(shape, index_map, pipeline_mode=pl.Buffered(1))


def _gelu(x):
    return 0.5 * x * (1.0 + lax.erf(x * (1.0 / math.sqrt(2.0))))


def _gelu_and_grad(x):
    cdf = 0.5 * (1.0 + lax.erf(x * (1.0 / math.sqrt(2.0))))
    return x * cdf, cdf + x * jnp.exp(-0.5 * x * x) * (1.0 / math.sqrt(2.0 * math.pi))


def _rms_fwd(x, n):
    r = lax.rsqrt(jnp.mean(x * x, axis=-1, keepdims=True) + EPS)
    xh = x * r
    return r, xh, xh * n


def _rms_bwd(dh, r, xh, n):
    dxh = dh * n
    dx = r * (dxh - xh * jnp.mean(dxh * xh, axis=-1, keepdims=True))
    return dx, jnp.sum(dh * xh, axis=0, keepdims=True)


MESH_ID = pl.DeviceIdType.MESH
_HBM = pl.BlockSpec(memory_space=pltpu.HBM)


def _my_place():
    return lax.axis_index("x"), lax.axis_index("y"), lax.axis_index("c")


def _ici_peers(x, y, c):
    return [((1 - x, y, c), 2 * (1 - x) + y), ((x, 1 - y, c), 2 * x + 1 - y), ((1 - x, 1 - y, c), 2 * (1 - x) + 1 - y)]


class _Hook:
    def __init__(self, operands, out_shapes, n_remote, n_local, start, finish, relay=None):
        self.operands, self.out_shapes = list(operands), list(out_shapes)
        self.n_remote, self.n_local, self.start, self.finish = n_remote, n_local, start, finish
        self.relay = relay or (lambda *a: None)


def _call(body, hooks, operands, *, in_specs, out_specs, out_shape, grid=None, scratch_shapes=(), **kw):
    hooks = tuple(hooks)
    n_in, n_out, n_scr = len(in_specs), len(out_shape), len(scratch_shapes)
    h_ops = [a for h in hooks for a in h.operands]
    h_outs = [s for h in hooks for s in h.out_shapes]
    h_sems = [pltpu.SemaphoreType.DMA((n,)) for h in hooks for n in (h.n_remote, h.n_remote, max(h.n_local, 1))]

    def wrapped(*refs):
        ins, hin = refs[:n_in], refs[n_in:n_in + len(h_ops)]
        o0 = n_in + len(h_ops)
        outs, hout = refs[o0:o0 + n_out], refs[o0 + n_out:o0 + n_out + len(h_outs)]
        s0 = o0 + n_out + len(h_outs)
        scr, hsem = refs[s0:s0 + n_scr], refs[s0 + n_scr:]

        def run(phase):
            ip = op = 0
            for i, h in enumerate(hooks):
                ssem, rsem, lsem = hsem[3 * i:3 * i + 3]

                def rcopy(k, src, dst, dev, ssem=ssem, rsem=rsem):
                    return pltpu.make_async_remote_copy(src_ref=src, dst_ref=dst, send_sem=ssem.at[k], recv_sem=rsem.at[k],
                                                        device_id=dev, device_id_type=MESH_ID)

                def lcopy(k, src, dst, lsem=lsem):
                    return pltpu.make_async_copy(src, dst, lsem.at[k])

                getattr(h, phase)(hin[ip:ip + len(h.operands)], hout[op:op + len(h.out_shapes)], rcopy, lcopy)
                ip += len(h.operands)
                op += len(h.out_shapes)

        def at_edge(phase, last):
            if not hooks:
                return
            if grid is None:
                run(phase)
                return
            cond = None
            for ax, n in enumerate(grid):
                here = pl.program_id(ax) == (n - 1 if last else 0)
                cond = here if cond is None else cond & here
            pl.when(cond)(lambda: run(phase))

        at_edge("start", False)
        at_edge("relay", True)
        body(*ins, *outs, *scr)
        at_edge("finish", True)

    if grid is not None:
        kw["grid"] = grid
    return list(pl.pallas_call(
        wrapped, out_shape=list(out_shape) + h_outs, in_specs=list(in_specs) + [_HBM] * len(h_ops),
        out_specs=list(out_specs) + [_HBM] * len(h_outs), scratch_shapes=list(scratch_shapes) + h_sems, **kw,
    )(*operands, *h_ops))


def _rows(ref, start, n):
    return ref.at[pl.ds(start, n), :]


def _ag_hook(shard):
    rows = shard.shape[0]
    half = rows // 2
    assert half % 16 == 0

    def place():
        x, y, c = _my_place()
        devs = dict(sib=(x, y, 1 - c), xn=(1 - x, y, c), yn=(x, 1 - y, c))
        chips = dict(me=2 * x + y, xn=2 * (1 - x) + y, yn=2 * x + 1 - y, dg=2 * (1 - x) + 1 - y)
        return c, devs, chips

    def block(full, chip, c):
        return _rows(full, (2 * chip + c) * rows, rows)

    def halfblock(full, chip, c, upper):
        return _rows(full, (2 * chip + c) * rows + upper * half, half)

    def start(ins, outs, rcopy, lcopy):
        c, devs, chips = place()
        src, dst = ins[0], block(outs[0], chips["me"], c)
        lcopy(0, src, dst).start()
        for k, to in enumerate(("sib", "xn", "yn")):
            rcopy(k, src, dst, devs[to]).start()

    def relay(ins, outs, rcopy, lcopy):
        c, devs, chips = place()
        full = outs[0]
        blk = block(full, chips["xn"], c)
        rcopy(1, blk, blk, devs["xn"]).wait_recv()
        low = halfblock(full, chips["xn"], c, 0)
        rcopy(3, low, low, devs["yn"]).start()
        rcopy(5, blk, blk, devs["sib"]).start()
        blk = block(full, chips["yn"], c)
        rcopy(2, blk, blk, devs["yn"]).wait_recv()
        up = halfblock(full, chips["yn"], c, 1)
        rcopy(4, up, up, devs["xn"]).start()
        rcopy(6, blk, blk, devs["sib"]).start()
        low, up = halfblock(full, chips["dg"], c, 0), halfblock(full, chips["dg"], c, 1)
        rcopy(3, low, low, devs["yn"]).wait_recv()
        rcopy(4, up, up, devs["xn"]).wait_recv()
        blk = block(full, chips["dg"], c)
        rcopy(7, blk, blk, devs["sib"]).start()

    def finish(ins, outs, rcopy, lcopy):
        c, devs, chips = place()
        full, sib = outs[0], devs["sib"]
        for k, chip in ((0, "me"), (5, "xn"), (6, "yn"), (7, "dg")):
            theirs = block(full, chips[chip], 1 - c)
            rcopy(k, theirs, theirs, sib).wait_recv()
            mine = block(full, chips[chip], c)
            if k:
                rcopy(k, mine, mine, sib).wait_send()
        src, dst = ins[0], block(full, chips["me"], c)
        lcopy(0, src, dst).wait()
        for k, to in enumerate(("sib", "xn", "yn")):
            rcopy(k, src, dst, devs[to]).wait_send()
        low, up = halfblock(full, chips["xn"], c, 0), halfblock(full, chips["yn"], c, 1)
        rcopy(3, low, low, devs["yn"]).wait_send()
        rcopy(4, up, up, devs["xn"]).wait_send()

    return _Hook([shard], [jax.ShapeDtypeStruct((NDEV * rows, D), shard.dtype)], 8, 1, start, finish, relay)


SIBLING, CHIPS, NEIGHBOURS, EVERYONE = "sibling", "chips", "sibling and the two neighbour chips", "everyone"
REACH_ID = {CHIPS: 6, EVERYONE: 7}
SIBLING_ID = 5


def _sequence(name, collective_id, reach, hooks):
    ins = [[jax.new_ref(a, memory_space=pltpu.MemorySpace.HBM) for a in h.operands] for h in hooks]
    outs = [[jax.empty_ref(s, memory_space=pltpu.MemorySpace.HBM) for s in h.out_shapes] for h in hooks]
    sems = tuple(pltpu.SemaphoreType.DMA((n,)) for h in hooks for n in (h.n_remote, h.n_remote, max(h.n_local, 1)))

    @pl.kernel(mesh=plsc.ScalarSubcoreMesh(axis_name="sequencer", num_cores=1), name=name, scratch_types=sems,
               compiler_params=pltpu.CompilerParams(collective_id=collective_id))
    def launch(*sem_refs):
        x, y, c = _my_place()
        chips = [dev for dev, _ in _ici_peers(x, y, c)]
        others = [(1 - x if dx else x, 1 - y if dy else y, 1 - c if dc else c)
                  for dx in range(2) for dy in range(2) for dc in range(2) if dx + dy + dc]
        devs = {SIBLING: [(x, y, 1 - c)], CHIPS: chips, NEIGHBOURS: [(x, y, 1 - c), (1 - x, y, c), (x, 1 - y, c)],
                EVERYONE: others}[reach]
        barrier = pltpu.get_barrier_semaphore()
        for dev in devs:
            pl.semaphore_signal(barrier, inc=1, device_id=dev, device_id_type=MESH_ID)
        pl.semaphore_wait(barrier, len(devs))
        for phase in ("start", "relay", "finish"):
            for i, h in enumerate(hooks):
                ssem, rsem, lsem = sem_refs[3 * i:3 * i + 3]

                def rcopy(k, src, dst, dev, ssem=ssem, rsem=rsem):
                    return pltpu.make_async_remote_copy(src_ref=src, dst_ref=dst, send_sem=ssem.at[k], recv_sem=rsem.at[k],
                                                        device_id=dev, device_id_type=MESH_ID)

                def lcopy(k, src, dst, lsem=lsem):
                    return pltpu.make_async_copy(src, dst, lsem.at[k])

                getattr(h, phase)(ins[i], outs[i], rcopy, lcopy)

    launch()
    return [o[...] for os in outs for o in os]


def _rs_d2d_hook(gfull):
    rows = gfull.shape[0] // NDEV

    def pairs(g, land):
        x, y, c = _my_place()
        return (x, y, 1 - c), [(k, _rows(g, (2 * k + 1 - c) * rows, rows), land.at[k]) for k in range(NCHIP)]

    def start(ins, outs, rcopy, lcopy):
        sib, cps = pairs(ins[0], outs[0])
        for i, src, dst in cps:
            rcopy(i, src, dst, sib).start()

    def finish(ins, outs, rcopy, lcopy):
        sib, cps = pairs(ins[0], outs[0])
        for i, src, dst in cps:
            rcopy(i, dst, dst, sib).wait_recv()
        for i, src, dst in cps:
            rcopy(i, src, dst, sib).wait_send()

    return _Hook([gfull], [jax.ShapeDtypeStruct((NCHIP, rows, D), gfull.dtype)], NCHIP, 0, start, finish)


def _rs_ici_hook(part):
    def start(ins, outs, rcopy, lcopy):
        x, y, c = _my_place()
        mychip = 2 * x + y
        lcopy(0, ins[0].at[mychip], outs[0].at[mychip]).start()
        for j, (dev, chip) in enumerate(_ici_peers(x, y, c)):
            rcopy(j, ins[0].at[chip], outs[0].at[mychip], dev).start()

    def finish(ins, outs, rcopy, lcopy):
        x, y, c = _my_place()
        mychip = 2 * x + y
        peers = _ici_peers(x, y, c)
        for j, (dev, chip) in enumerate(peers):
            rcopy(j, outs[0].at[chip], outs[0].at[chip], dev).wait_recv()
        for j, (dev, chip) in enumerate(peers):
            rcopy(j, ins[0].at[chip], outs[0].at[mychip], dev).wait_send()
        lcopy(0, ins[0].at[mychip], outs[0].at[mychip]).wait()

    return _Hook([part], [jax.ShapeDtypeStruct(part.shape, part.dtype)], 3, 1, start, finish)


def _small_hook(arrays):
    n = len(arrays)

    def peers():
        x, y, c = _my_place()
        out = []
        for dx in range(2):
            for dy in range(2):
                for dc in range(2):
                    if dx + dy + dc:
                        px, py, pc = (1 - x if dx else x), (1 - y if dy else y), (1 - c if dc else c)
                        out.append(((px, py, pc), 4 * px + 2 * py + pc))
        return 4 * x + 2 * y + c, out

    def start(ins, outs, rcopy, lcopy):
        me, ps = peers()
        for t in range(n):
            lcopy(t, ins[t], outs[t].at[me]).start()
            for i, (dev, _) in enumerate(ps):
                rcopy(n * i + t, ins[t], outs[t].at[me], dev).start()

    def finish(ins, outs, rcopy, lcopy):
        me, ps = peers()
        for t in range(n):
            for i, (dev, peer) in enumerate(ps):
                rcopy(n * i + t, outs[t].at[peer], outs[t].at[peer], dev).wait_recv()
            for i, (dev, _) in enumerate(ps):
                rcopy(n * i + t, ins[t], outs[t].at[me], dev).wait_send()
            lcopy(t, ins[t], outs[t].at[me]).wait()

    return _Hook(arrays, [jax.ShapeDtypeStruct((NDEV,) + a.shape, a.dtype) for a in arrays], 7 * n, n, start, finish)


def _wblock(w):
    return _resident(w.shape, lambda *_: (0, 0))


def _ffn_fwd(name, x, nrm, wg, wu, wd, hooks=(), tm=512):
    S = x.shape[0]

    def body(x_ref, n_ref, wg_ref, wu_ref, wd_ref, y_ref, g_ref, u_ref, a_ref, h_ref, acc_ref):
        xv = x_ref[...]
        _, _, h = _rms_fwd(xv, n_ref[...])
        h = h.astype(CDT)
        h_ref[...] = h
        for ci in range(F // FT):
            sl = slice(ci * FT, (ci + 1) * FT)
            g = _mm_nt(h, wg_ref[sl, :])
            u = _mm_nt(h, wu_ref[sl, :])
            g_ref[:, sl] = g.astype(CDT)
            u_ref[:, sl] = u.astype(CDT)
            a = (g * jax.nn.sigmoid(g) * u).astype(CDT)
            a_ref[:, sl] = a
            o = _mm(a, wd_ref[sl, :])
            if ci == 0:
                acc_ref[...] = o
            else:
                acc_ref[...] += o
        y_ref[...] = xv + 0.5 * acc_ref[...]

    tok = pl.BlockSpec((tm, D), lambda i: (i, 0))
    hid = pl.BlockSpec((tm, F), lambda i: (i, 0))
    hidden = jax.ShapeDtypeStruct((S, F), CDT)
    return _call(
        body, hooks, [x, nrm, wg, wu, wd], name=name, grid=(S // tm,),
        out_shape=[jax.ShapeDtypeStruct((S, D), f32), hidden, hidden, hidden, jax.ShapeDtypeStruct((S, D), CDT)],
        in_specs=[tok, _resident((1, D), lambda i: (0, 0)), _wblock(wg), _wblock(wu), _wblock(wd)],
        out_specs=[tok, hid, hid, hid, tok],
        scratch_shapes=[pltpu.VMEM((tm, D), f32)],
        compiler_params=_params("arbitrary"),
    )


def _proj_fwd(x1, nrm, wfull, b3, cos, sin, hooks=(), tm=512):
    S = x1.shape[0]

    def body(x_ref, n_ref, w_ref, b_ref, cos_ref, sin_ref, p_ref, h_ref):
        _, _, h = _rms_fwd(x_ref[...], n_ref[...])
        h = h.astype(CDT)
        h_ref[...] = h
        for p in range(8):
            seg = SEG_OF_SLOT[p]
            z = _mm(h, w_ref[seg * D:(seg + 1) * D, :]) + b_ref[seg]
            if p in (SLOT_Q, SLOT_K):
                co, si = cos_ref[...], sin_ref[...]
                for hh in range(H):
                    cs = slice(hh * DK, (hh + 1) * DK)
                    zr = _rotate(z[:, cs], co, si)
                    p_ref[p, :, cs] = (zr * K_SCALE if p == SLOT_K else zr).astype(CDT)
            else:
                p_ref[p] = z.astype(CDT)

    tab = pl.BlockSpec((tm, DK // 2), lambda i: (i, 0))
    return _call(
        body, hooks, [x1, nrm, wfull, b3, cos, sin], name="proj_fwd", grid=(S // tm,),
        out_shape=[jax.ShapeDtypeStruct((8, S, D), CDT), jax.ShapeDtypeStruct((S, D), CDT)],
        in_specs=[pl.BlockSpec((tm, D), lambda i: (i, 0)), _resident((1, D), lambda i: (0, 0)),
                  _resident((8 * D, D), lambda i: (0, 0)), _resident((8, 1, D), lambda i: (0, 0, 0)), tab, tab],
        out_specs=[pl.BlockSpec((8, tm, D), lambda i: (0, i, 0)), pl.BlockSpec((tm, D), lambda i: (i, 0))],
        compiler_params=_params("arbitrary"),
    )


def _sgu_norm(va, gn, bn):
    mu = jnp.mean(va, axis=-1, keepdims=True)
    xc = va - mu
    rstd = lax.rsqrt(jnp.mean(xc * xc, axis=-1, keepdims=True) + EPS)
    vhat = xc * rstd
    return rstd, vhat, vhat * gn + bn


def _sgu_fwd(proj, gn, bn, ws, bsc, tm=512):
    S = proj.shape[1]
    GW = D // G

    def body(p_ref, gn_ref, bn_ref, ws_ref, bs_ref, a_ref):
        ua = _gelu(p_ref[0].astype(f32))
        va = _gelu(p_ref[1].astype(f32))
        _, _, vn = _sgu_norm(va, gn_ref[...], bn_ref[...])
        vn = vn.astype(CDT)
        for ch in range(tm // C):
            rs = slice(ch * C, (ch + 1) * C)
            for gi in range(G):
                cs = slice(gi * GW, (gi + 1) * GW)
                s = _mm(ws_ref[gi], vn[rs, cs]) + bs_ref[gi]
                a_ref[rs, cs] = (ua[rs, cs] * s).astype(CDT)

    return pl.pallas_call(
        body, name="sgu_fwd", grid=(S // tm,),
        out_shape=jax.ShapeDtypeStruct((S, D), CDT),
        in_specs=[pl.BlockSpec((2, tm, D), lambda i: (0, i, 0)), _resident((1, D), lambda i: (0, 0)),
                  _resident((1, D), lambda i: (0, 0)), _resident((G, C, C), lambda i: (0, 0, 0)),
                  _resident((G, C, 1), lambda i: (0, 0, 0))],
        out_specs=pl.BlockSpec((tm, D), lambda i: (i, 0)),
        compiler_params=_params("arbitrary"),
    )(proj, gn, bn, ws, bsc)


def _decay_tables(dl_ref):
    lg = jax.nn.log_sigmoid(dl_ref[0:2, :])
    lgf, lgb = lg[0:1, :], lg[1:2, :]
    assert RC <= DK
    ri = lax.broadcasted_iota(jnp.int32, (RC, RC), 0)
    ci = lax.broadcasted_iota(jnp.int32, (RC, RC), 1)
    d = (ri - ci).astype(f32)
    lower = d >= 0
    dmat = jnp.where(lower, jnp.exp(d * lgf[:, :RC]), jnp.exp(-d * lgb[:, :RC]))
    dmat_t = jnp.where(d <= 0, jnp.exp(-d * lgf[:, :RC]), jnp.exp(d * lgb[:, :RC]))
    pos = lax.broadcasted_iota(jnp.int32, (RC, DK), 0).astype(f32)
    t = dict(
        lgf=lgf, lgb=lgb, d=d, lower=lower, dmat=dmat, dmat_t=dmat_t, pos=pos,
        fq=jnp.exp((pos + 1.0) * lgf), fk=jnp.exp((RC - 1.0 - pos) * lgf),
        bq=jnp.exp((RC - pos) * lgb), bk=jnp.exp(pos * lgb),
        lamf=jnp.exp(float(RC) * lgf), lamb=jnp.exp(float(RC) * lgb),
    )
    return t


def _rotate(t, co, si):
    t1, t2 = t[:, :DK // 2], t[:, DK // 2:]
    return jnp.concatenate([t1 * co - t2 * si, t2 * co + t1 * si], axis=-1)


def _unrotate(t, co, si):
    t1, t2 = t[:, :DK // 2], t[:, DK // 2:]
    return jnp.concatenate([t1 * co + t2 * si, t2 * co - t1 * si], axis=-1)


K_SCALE = DK ** -0.5
ROW_TILE = 256


def _ret_fwd(proj, dl, hooks=()):
    S = proj.shape[1]
    NC = S // RC

    def body(q_ref, k_ref, v_ref, g_ref, dl_ref, R_ref, r_ref, sfs_ref, sbs_ref, rb_ref, sf_ref, sb_ref):
        t = _decay_tables(dl_ref)

        def chunk(n):
            rows = pl.ds(pl.multiple_of(n * RC, RC), RC)
            return rows, q_ref[rows, :], k_ref[rows, :], v_ref[rows, :]

        sf_ref[...] = jnp.zeros_like(sf_ref)
        sb_ref[...] = jnp.zeros_like(sb_ref)

        def step(i, carry):
            rows, qn, kn, vn = chunk(i)
            sc = _mm_nt(qn, kn) * t["dmat"]
            out = _mm(sc.astype(CDT), vn)
            sf = sf_ref[...]
            sfb = sf.astype(CDT)
            sfs_ref[i] = sfb
            R_ref[rows, :] = out + _mm((qn.astype(f32) * t["fq"]).astype(CDT), sfb)
            sf_ref[...] = sf * t["lamf"] + _mm_tn((kn.astype(f32) * t["fk"]).astype(CDT), vn)
            m = NC - 1 - i
            rows, qn, kn, vn = chunk(m)
            sb = sb_ref[...]
            sbb = sb.astype(CDT)
            sbs_ref[m] = sbb
            rb_ref[rows, :] = _mm((qn.astype(f32) * t["bq"]).astype(CDT), sbb)
            sb_ref[...] = sb * t["lamb"] + _mm_tn((kn.astype(f32) * t["bk"]).astype(CDT), vn)
            return carry

        lax.fori_loop(0, NC, step, 0, unroll=8)

        def finish(i, carry):
            rs = pl.ds(pl.multiple_of(i * ROW_TILE, ROW_TILE), ROW_TILE)
            R = R_ref[rs, :] + rb_ref[rs, :]
            R_ref[rs, :] = R
            rn = R * lax.rsqrt(jnp.mean(R * R, axis=-1, keepdims=True) + EPS)
            g = g_ref[rs, :].astype(f32)
            r_ref[rs, :] = (rn * g * jax.nn.sigmoid(g)).astype(CDT)
            return carry

        lax.fori_loop(0, S // ROW_TILE, finish, 0, unroll=2)

    def seg(slot):
        return pl.BlockSpec((None, S, DK), lambda h: (slot, 0, h))

    states = jax.ShapeDtypeStruct((H, NC, DK, DK), CDT)
    state_blk = pl.BlockSpec((None, NC, DK, DK), lambda h: (h, 0, 0, 0))
    return _call(
        body, hooks, [proj, proj, proj, proj, dl], name="ret_fwd", grid=(H,),
        out_shape=[jax.ShapeDtypeStruct((S, H * DK), f32), jax.ShapeDtypeStruct((S, H * DK), CDT), states, states],
        in_specs=[seg(SLOT_Q), seg(SLOT_K), seg(SLOT_VR), seg(SLOT_GR), pl.BlockSpec((None, 8, DK), lambda h: (h, 0, 0))],
        out_specs=[pl.BlockSpec((S, DK), lambda h: (0, h)), pl.BlockSpec((S, DK), lambda h: (0, h)), state_blk, state_blk],
        scratch_shapes=[pltpu.VMEM((S, DK), f32), pltpu.VMEM((DK, DK), f32), pltpu.VMEM((DK, DK), f32)],
        compiler_params=_params("arbitrary"),
    )


def _merge_fwd(a, r, proj, x1, wa, wb, wo, hooks=(), tm=512):
    S = x1.shape[0]

    def body(a_ref, r_ref, gt_ref, x_ref, wa_ref, wb_ref, wo_ref, x2_ref, ya_ref, yb_ref):
        ya = _mm(a_ref[...], wa_ref[...])
        yb = _mm(r_ref[...], wb_ref[...])
        ya_ref[...] = ya.astype(CDT)
        yb_ref[...] = yb.astype(CDT)
        mix = jax.nn.sigmoid(gt_ref[0].astype(f32)) * ya + jax.nn.sigmoid(gt_ref[1].astype(f32)) * yb
        x2_ref[...] = x_ref[...] + _mm(mix.astype(CDT), wo_ref[...])

    tok = pl.BlockSpec((tm, D), lambda i: (i, 0))
    return _call(
        body, hooks, [a, r, proj, x1, wa, wb, wo], name="merge_fwd", grid=(S // tm,),
        out_shape=[jax.ShapeDtypeStruct((S, D), f32), jax.ShapeDtypeStruct((S, D), CDT), jax.ShapeDtypeStruct((S, D), CDT)],
        in_specs=[tok, tok, pl.BlockSpec((2, tm, D), lambda i: (SLOT_GA // 2, i, 0)), tok,
                  _wblock(wa), _wblock(wb), _wblock(wo)],
        out_specs=[tok, tok, tok],
        compiler_params=_params("arbitrary"),
    )


def _loss_head(x3, fn, target, tm=512):
    S = x3.shape[0]

    def body(x_ref, n_ref, t_ref, dx_ref, dxh_ref, dn_ref, l_ref):
        n = n_ref[...]
        r, xh, y = _rms_fwd(x_ref[...], n)
        e = y - t_ref[...]
        dy = e * (1.0 / D)
        dx, dn = _rms_bwd(dy, r, xh, n)
        dx_ref[...] = dx
        dxh_ref[...] = (0.5 * dx).astype(CDT)
        part = 0.5 * jnp.sum(jnp.sum(e * e, axis=-1, keepdims=True), axis=0, keepdims=True) * (1.0 / D)

        @pl.when(pl.program_id(0) == 0)
        def _():
            dn_ref[...] = jnp.zeros_like(dn_ref)
            l_ref[...] = jnp.zeros_like(l_ref)

        dn_ref[...] += dn
        l_ref[...] += jnp.broadcast_to(part, l_ref.shape)

    tok = pl.BlockSpec((tm, D), lambda i: (i, 0))
    return pl.pallas_call(
        body, name="loss_head", grid=(S // tm,),
        out_shape=[jax.ShapeDtypeStruct((S, D), f32), jax.ShapeDtypeStruct((S, D), CDT), jax.ShapeDtypeStruct((1, D), f32),
                   jax.ShapeDtypeStruct((8, 128), f32)],
        in_specs=[tok, _resident((1, D), lambda i: (0, 0)), tok],
        out_specs=[tok, tok, pl.BlockSpec((1, D), lambda i: (0, 0)), pl.BlockSpec((8, 128), lambda i: (0, 0))],
        compiler_params=_params("arbitrary"),
    )(x3, fn, target)


def _ffn_bwd_hidden(name, dyh, g, u, wd, hooks=(), tm=512):
    S = dyh.shape[0]

    def body(dyh_ref, g_ref, u_ref, wd_ref, dg_ref, du_ref):
        dyh = dyh_ref[...]
        for ci in range(F // FT):
            sl = slice(ci * FT, (ci + 1) * FT)
            da = _mm_nt(dyh, wd_ref[sl, :])
            gv = g_ref[:, sl].astype(f32)
            uv = u_ref[:, sl].astype(f32)
            s = jax.nn.sigmoid(gv)
            silu = gv * s
            du_ref[:, sl] = (da * silu).astype(CDT)
            dg_ref[:, sl] = (da * uv * (s + silu - silu * s)).astype(CDT)

    hid = pl.BlockSpec((tm, F), lambda i: (i, 0))
    hidden = jax.ShapeDtypeStruct((S, F), CDT)
    return _call(
        body, hooks, [dyh, g, u, wd], name=name, grid=(S // tm,), out_shape=[hidden, hidden],
        in_specs=[pl.BlockSpec((tm, D), lambda i: (i, 0)), hid, hid, _wblock(wd)], out_specs=[hid, hid],
        compiler_params=_params("arbitrary"),
    )


def _ffn_bwd_in(name, dy, x, dg, du, nrm, wg, wu, hooks=(), tm=512):
    S = x.shape[0]

    def body(dy_ref, x_ref, dg_ref, du_ref, n_ref, wg_ref, wu_ref, dx_ref, dn_ref, acc_ref):
        n = n_ref[...]
        r, xh, _ = _rms_fwd(x_ref[...], n)
        for ci in range(F // FT):
            sl = slice(ci * FT, (ci + 1) * FT)
            dh = _mm(dg_ref[:, sl], wg_ref[sl, :]) + _mm(du_ref[:, sl], wu_ref[sl, :])
            if ci == 0:
                acc_ref[...] = dh
            else:
                acc_ref[...] += dh
        dx, dn = _rms_bwd(acc_ref[...], r, xh, n)
        dx_ref[...] = dy_ref[...] + dx

        @pl.when(pl.program_id(0) == 0)
        def _():
            dn_ref[...] = jnp.zeros_like(dn_ref)

        dn_ref[...] += dn

    tok = pl.BlockSpec((tm, D), lambda i: (i, 0))
    hid = pl.BlockSpec((tm, F), lambda i: (i, 0))
    return _call(
        body, hooks, [dy, x, dg, du, nrm, wg, wu], name=name, grid=(S // tm,),
        out_shape=[jax.ShapeDtypeStruct((S, D), f32), jax.ShapeDtypeStruct((1, D), f32)],
        in_specs=[tok, tok, hid, hid, _resident((1, D), lambda i: (0, 0)), _wblock(wg), _wblock(wu)],
        out_specs=[tok, pl.BlockSpec((1, D), lambda i: (0, 0))],
        scratch_shapes=[pltpu.VMEM((tm, D), f32)],
        compiler_params=_params("arbitrary"),
    )


TN_ROWS = 512


def _tn(name, xs, ys, block_of, hooks=()):
    S, M = xs.shape
    B = ys.shape[0]
    tr = TN_ROWS if M % TN_ROWS == 0 else M // 2
    assert M % tr == 0 and tr % 128 == 0
    nt = M // tr

    def body(x_ref, y_ref, o_ref):
        o_ref[...] = _mm_tn(x_ref[...], y_ref[...]).astype(CDT)

    return _call(
        body, hooks, [xs, ys], name=name, grid=(B, nt),
        out_shape=[jax.ShapeDtypeStruct((B * M, D), CDT)],
        in_specs=[pl.BlockSpec((S, tr), lambda b, i: (0, i)), pl.BlockSpec((None, S, D), lambda b, i: (b, 0, 0))],
        out_specs=[pl.BlockSpec((tr, D), lambda b, i: (block_of(b) * nt + i, 0))],
        compiler_params=_params("arbitrary", "arbitrary"),
    )


def _wgrad(name, xs, y, hooks=()):
    return _tn(name, xs, y[None], lambda b: 0, hooks)


def _merge_bwd_act(dx2, ya, yb, proj, wa, wb, wo, hooks=(), tm=512):
    S = dx2.shape[0]

    def body(dx_ref, ya_ref, yb_ref, gt_ref, wa_ref, wb_ref, wo_ref,
             dp_ref, da_ref, dr_ref, mix_ref, dxb_ref, dya_ref, dyb_ref):
        dxb = dx_ref[...].astype(CDT)
        dxb_ref[...] = dxb
        dmix = _mm_nt(dxb, wo_ref[...])
        ya = ya_ref[...].astype(f32)
        yb = yb_ref[...].astype(f32)
        sa = jax.nn.sigmoid(gt_ref[0].astype(f32))
        sb = jax.nn.sigmoid(gt_ref[1].astype(f32))
        mix_ref[...] = (sa * ya + sb * yb).astype(CDT)
        dya = (dmix * sa).astype(CDT)
        dyb = (dmix * sb).astype(CDT)
        dya_ref[...] = dya
        dyb_ref[...] = dyb
        dp_ref[0] = (dmix * ya * sa * (1.0 - sa)).astype(CDT)
        dp_ref[1] = (dmix * yb * sb * (1.0 - sb)).astype(CDT)
        da_ref[...] = _mm_nt(dya, wa_ref[...]).astype(CDT)
        dr_ref[...] = _mm_nt(dyb, wb_ref[...]).astype(CDT)

    tok = pl.BlockSpec((tm, D), lambda i: (i, 0))
    gates = pl.BlockSpec((2, tm, D), lambda i: (SLOT_GA // 2, i, 0))
    act = jax.ShapeDtypeStruct((S, D), CDT)
    return _call(
        body, hooks, [dx2, ya, yb, proj, wa, wb, wo], name="merge_bwd_act", grid=(S // tm,),
        out_shape=[jax.ShapeDtypeStruct((8, S, D), CDT), act, act, act, act, act, act],
        in_specs=[tok, tok, tok, gates, _wblock(wa), _wblock(wb), _wblock(wo)],
        out_specs=[gates, tok, tok, tok, tok, tok, tok],
        compiler_params=_params("arbitrary"),
    )


def _sgu_bwd(da, proj, dproj, gn, bn, ws, wst, bsc, hooks=(), tm=512):
    S = proj.shape[1]
    GW = D // G

    def body(da_ref, p_ref, dpin_ref, gn_ref, bn_ref, ws_ref, wst_ref, bs_ref,
             dp_ref, dws_ref, dbs_ref, dgn_ref, dbn_ref, ds_ref, dvn_ref):
        @pl.when(pl.program_id(0) == 0)
        def _():
            dws_ref[...] = jnp.zeros_like(dws_ref)
            dbs_ref[...] = jnp.zeros_like(dbs_ref)
            dgn_ref[...] = jnp.zeros_like(dgn_ref)
            dbn_ref[...] = jnp.zeros_like(dbn_ref)

        pu = p_ref[0].astype(f32)
        pv = p_ref[1].astype(f32)
        ua, dua = _gelu_and_grad(pu)
        va, dva_dpv = _gelu_and_grad(pv)
        gn = gn_ref[...]
        rstd, vhat, vn = _sgu_norm(va, gn, bn_ref[...])
        vnb = vn.astype(CDT)
        dav = da_ref[...].astype(f32)
        dsb = (dav * ua).astype(CDT)
        ones = jnp.ones((8, GW), CDT)
        for ch in range(tm // C):
            rs = slice(ch * C, (ch + 1) * C)
            for gi in range(G):
                cs = slice(gi * GW, (gi + 1) * GW)
                s = _mm(ws_ref[gi], vnb[rs, cs]) + bs_ref[gi]
                ds_ref[rs, cs] = s
                dsg = dsb[rs, cs]
                dws_ref[gi] += _mm_nt(dsg, vnb[rs, cs])
                dbs_ref[gi] += _mm_nt(ones, dsg)
                dvn_ref[rs, cs] = _mm(wst_ref[gi], dsg)
        dp_ref[0] = (dav * ds_ref[...] * dua).astype(CDT)
        dvn = dvn_ref[...]
        dgn_ref[...] += jnp.sum(dvn * vhat, axis=0, keepdims=True)
        dbn_ref[...] += jnp.sum(dvn, axis=0, keepdims=True)
        dvh = dvn * gn
        dva = rstd * (dvh - jnp.mean(dvh, axis=-1, keepdims=True) - vhat * jnp.mean(dvh * vhat, axis=-1, keepdims=True))
        dp_ref[1] = (dva * dva_dpv).astype(CDT)

    uv = pl.BlockSpec((2, tm, D), lambda i: (0, i, 0))
    row = _resident((1, D), lambda i: (0, 0))
    return _call(
        body, hooks, [da, proj, dproj, gn, bn, ws, wst, bsc], name="sgu_bwd", grid=(S // tm,),
        out_shape=[jax.ShapeDtypeStruct(dproj.shape, CDT), jax.ShapeDtypeStruct((G, C, C), f32),
                   jax.ShapeDtypeStruct((G, 8, C), f32), jax.ShapeDtypeStruct((1, D), f32), jax.ShapeDtypeStruct((1, D), f32)],
        in_specs=[pl.BlockSpec((tm, D), lambda i: (i, 0)), uv, _HBM, row, row,
                  _resident((G, C, C), lambda i: (0, 0, 0)), _resident((G, C, C), lambda i: (0, 0, 0)),
                  _resident((G, C, 1), lambda i: (0, 0, 0))],
        out_specs=[uv, pl.BlockSpec((G, C, C), lambda i: (0, 0, 0)), pl.BlockSpec((G, 8, C), lambda i: (0, 0, 0)),
                   pl.BlockSpec((1, D), lambda i: (0, 0)), pl.BlockSpec((1, D), lambda i: (0, 0))],
        scratch_shapes=[pltpu.VMEM((tm, D), f32), pltpu.VMEM((tm, D), f32)],
        input_output_aliases={2: 0},
        compiler_params=_params("arbitrary"),
    )


def _ret_bwd(dr, R, sfs, sbs, proj, dproj, cos, sin, dl, hooks=()):
    S = proj.shape[1]
    NC = S // RC
    assert NC % 2 == 0

    def body(dr_ref, R_ref, sf_ref, sb_ref, q_ref, k_ref, v_ref, g_ref, dpin_ref, cos_ref, sin_ref, dl_ref,
             dp_ref, dd_ref, dR_ref, gb_ref, gf_ref, acc_ref):
        t = _decay_tables(dl_ref)

        def gate_norm_bwd(i, carry):
            rs = pl.ds(pl.multiple_of(i * ROW_TILE, ROW_TILE), ROW_TILE)
            Rv = R_ref[rs, :]
            rstd = lax.rsqrt(jnp.mean(Rv * Rv, axis=-1, keepdims=True) + EPS)
            rn = Rv * rstd
            gv = g_ref[rs, :].astype(f32)
            s = jax.nn.sigmoid(gv)
            drv = dr_ref[rs, :].astype(f32)
            dp_ref[3, rs, :] = (drv * rn * (s * (1.0 + gv * (1.0 - s)))).astype(CDT)
            drn = drv * gv * s
            dR_ref[rs, :] = (rstd * (drn - rn * jnp.mean(drn * rn, axis=-1, keepdims=True))).astype(CDT)
            return carry

        lax.fori_loop(0, S // ROW_TILE, gate_norm_bwd, 0)

        def chunk(n):
            rows = pl.ds(pl.multiple_of(n * RC, RC), RC)
            return rows, q_ref[rows, :], k_ref[rows, :], v_ref[rows, :], dR_ref[rows, :]

        def emit_kv(rows, dk, dv, final):
            if not final:
                dp_ref[1, rows, :] = dk.astype(CDT)
                dp_ref[2, rows, :] = dv.astype(CDT)
            else:
                co, si = cos_ref[rows, :], sin_ref[rows, :]
                dk = dp_ref[1, rows, :].astype(f32) + dk
                dp_ref[1, rows, :] = (_unrotate(dk, co, si) * K_SCALE).astype(CDT)
                dp_ref[2, rows, :] = (dp_ref[2, rows, :].astype(f32) + dv).astype(CDT)

        gb_ref[...] = jnp.zeros_like(gb_ref)
        gf_ref[...] = jnp.zeros_like(gf_ref)
        acc_ref[...] = jnp.zeros_like(acc_ref)
        span = t["dmat"] * jnp.abs(t["d"])
        span_f, span_b = jnp.where(t["lower"], span, 0.0), jnp.where(t["lower"], 0.0, span)

        def ascend(n, final):
            rows, qn, kn, vn, dRn = chunk(n)
            qf, kf = qn.astype(f32), kn.astype(f32)
            sc = _mm_nt(qn, kn)
            dA = _mm_nt(dRn, vn)
            prod = sc * dA
            lgf_part = jnp.sum(prod * span_f, axis=0, keepdims=True)
            lgb_part = jnp.sum(prod * span_b, axis=0, keepdims=True)
            dsc = (dA * t["dmat"]).astype(CDT)
            dq = _mm(dsc, kn)
            scT = (_mm_nt(kn, qn) * t["dmat_t"]).astype(CDT)
            dscT = (_mm_nt(vn, dRn) * t["dmat_t"]).astype(CDT)
            dk = _mm(dscT, qn)
            dv = _mm(scT, dRn)
            sfb = sf_ref[n]
            sbb = sb_ref[n]
            qdf = qf * t["fq"]
            dqdf = _mm_nt(dRn, sfb)
            dq += dqdf * t["fq"]
            lgf_row = jnp.sum(qdf * dqdf * (t["pos"] + 1.0), axis=0, keepdims=True)
            qdb = qf * t["bq"]
            dqdb = _mm_nt(dRn, sbb)
            dq += dqdb * t["bq"]
            lgb_row = jnp.sum(qdb * dqdb * (RC - t["pos"]), axis=0, keepdims=True)
            gb = gb_ref[...]
            gbb = gb.astype(CDT)
            kdb = kf * t["bk"]
            dkdb = _mm_nt(vn, gbb)
            dk += dkdb * t["bk"]
            dv += _mm(kdb.astype(CDT), gbb)
            lgb_row += jnp.sum(kdb * dkdb * t["pos"], axis=0, keepdims=True)
            lgb_row += float(RC) * t["lamb"] * jnp.sum(gb * sbb.astype(f32), axis=0, keepdims=True)
            co, si = cos_ref[rows, :], sin_ref[rows, :]
            dp_ref[0, rows, :] = _unrotate(dq, co, si).astype(CDT)
            emit_kv(rows, dk, dv, final)
            acc_ref[0:1, :] += lgf_row + lgf_part
            acc_ref[1:2, :] += lgb_row + lgb_part
            gb_ref[...] = gb * t["lamb"] + _mm_tn(qdb.astype(CDT), dRn)

        def descend(n, final):
            rows, qn, kn, vn, dRn = chunk(n)
            gf = gf_ref[...]
            gfb = gf.astype(CDT)
            kdf = kn.astype(f32) * t["fk"]
            dkdf = _mm_nt(vn, gfb)
            lgf_row = jnp.sum(kdf * dkdf * (RC - 1.0 - t["pos"]), axis=0, keepdims=True)
            lgf_row += float(RC) * t["lamf"] * jnp.sum(gf * sf_ref[n].astype(f32), axis=0, keepdims=True)
            acc_ref[0:1, :] += lgf_row
            emit_kv(rows, dkdf * t["fk"], _mm(kdf.astype(CDT), gfb), final)
            gf_ref[...] = gf * t["lamf"] + _mm_tn((qn.astype(f32) * t["fq"]).astype(CDT), dRn)

        def sweep(final):
            def step(i, carry):
                ascend(i, final)
                descend(NC - 1 - i, final)
                return carry
            return step

        lax.fori_loop(0, NC // 2, sweep(False), 0, unroll=4)
        lax.fori_loop(NC // 2, NC, sweep(True), 0, unroll=4)
        dlg = jnp.sum(acc_ref[...], axis=1, keepdims=True)
        dlogit = dlg * jax.nn.sigmoid(-dl_ref[:, 0:1])
        lane = lax.broadcasted_iota(jnp.int32, (8, 128), 1)
        dd_ref[...] = jnp.where(lane == pl.program_id(0), jnp.broadcast_to(dlogit, (8, 128)), 0.0)

    def seg(slot):
        return pl.BlockSpec((None, S, DK), lambda h: (slot, 0, h))

    head = pl.BlockSpec((S, DK), lambda h: (0, h))
    states = pl.BlockSpec((None, NC, DK, DK), lambda h: (h, 0, 0, 0))
    return _call(
        body, hooks, [dr, R, sfs, sbs, proj, proj, proj, proj, dproj, cos, sin, dl], name="ret_bwd", grid=(H,),
        out_shape=[jax.ShapeDtypeStruct(dproj.shape, CDT), jax.ShapeDtypeStruct((H, 8, 128), f32)],
        in_specs=[head, head, states, states, seg(SLOT_Q), seg(SLOT_K), seg(SLOT_VR), seg(SLOT_GR), _HBM,
                  _resident((S, DK // 2), lambda h: (0, 0)), _resident((S, DK // 2), lambda h: (0, 0)),
                  pl.BlockSpec((None, 8, DK), lambda h: (h, 0, 0))],
        out_specs=[pl.BlockSpec((4, S, DK), lambda h: (1, 0, h), pipeline_mode=pl.Buffered(1)),
                   pl.BlockSpec((None, 8, 128), lambda h: (h, 0, 0))],
        scratch_shapes=[pltpu.VMEM((S, DK), CDT),
                        pltpu.VMEM((DK, DK), f32), pltpu.VMEM((DK, DK), f32), pltpu.VMEM((8, DK), f32)],
        input_output_aliases={8: 0},
        compiler_params=_params("arbitrary"),
    )


def _proj_bwd_act(dproj, dx2, x1, nrm, wfull, hooks=(), tm=512):
    S = x1.shape[0]

    def body(dp_ref, dx2_ref, x_ref, n_ref, w_ref, dx_ref, dxh_ref, dn_ref, db_ref, acc_ref):
        @pl.when(pl.program_id(0) == 0)
        def _():
            dn_ref[...] = jnp.zeros_like(dn_ref)
            db_ref[...] = jnp.zeros_like(db_ref)

        for p in range(8):
            seg = SEG_OF_SLOT[p]
            dp = dp_ref[p]
            db_ref[seg] += jnp.sum(dp.astype(f32), axis=0, keepdims=True)
            dh = _mm_nt(dp, w_ref[seg * D:(seg + 1) * D, :])
            if p == 0:
                acc_ref[...] = dh
            else:
                acc_ref[...] += dh
        n = n_ref[...]
        r, xh, _ = _rms_fwd(x_ref[...], n)
        dx, dn = _rms_bwd(acc_ref[...], r, xh, n)
        dx = dx2_ref[...] + dx
        dx_ref[...] = dx
        dxh_ref[...] = (0.5 * dx).astype(CDT)
        dn_ref[...] += dn

    tok = pl.BlockSpec((tm, D), lambda i: (i, 0))
    return _call(
        body, hooks, [dproj, dx2, x1, nrm, wfull], name="proj_bwd_act", grid=(S // tm,),
        out_shape=[jax.ShapeDtypeStruct((S, D), f32), jax.ShapeDtypeStruct((S, D), CDT), jax.ShapeDtypeStruct((1, D), f32),
                   jax.ShapeDtypeStruct((8, 1, D), f32)],
        in_specs=[pl.BlockSpec((8, tm, D), lambda i: (0, i, 0)), tok, tok, _resident((1, D), lambda i: (0, 0)),
                  _resident((8 * D, D), lambda i: (0, 0))],
        out_specs=[tok, tok, pl.BlockSpec((1, D), lambda i: (0, 0)), pl.BlockSpec((8, 1, D), lambda i: (0, 0, 0))],
        scratch_shapes=[pltpu.VMEM((tm, D), f32)],
        compiler_params=_params("arbitrary"),
    )


def _rs_sum(name, gfulls, lands, my_c):
    n = len(gfulls)
    rows = gfulls[0].shape[0] // NDEV
    assert all(g.shape[0] == NDEV * rows for g in gfulls)

    def body(c_ref, *refs):
        for g_ref, l_ref, o_ref in zip(refs[:n], refs[n:2 * n], refs[2 * n:]):
            o_ref[...] = (g_ref[...].astype(f32) + l_ref[...].astype(f32)).astype(CDT)

    slot = pl.BlockSpec((None, rows, D), lambda k, c: (k, 0, 0))
    return pl.pallas_call(
        body, name=name,
        grid_spec=pltpu.PrefetchScalarGridSpec(
            num_scalar_prefetch=1, grid=(NCHIP,),
            in_specs=[pl.BlockSpec((rows, D), lambda k, c: (2 * k + c[0], 0))] * n + [slot] * n,
            out_specs=[slot] * n),
        out_shape=[jax.ShapeDtypeStruct((NCHIP, rows, D), CDT)] * n,
        compiler_params=_params("arbitrary"),
    )(my_c, *gfulls, *lands)


def _adamw_math(g, w, m, v):
    m2 = ADAM_B1 * m + (1.0 - ADAM_B1) * g
    v2 = ADAM_B2 * v + (1.0 - ADAM_B2) * (g * g)
    delta = -ADAM_LR * ((m2 / BC1) / (jnp.sqrt(v2 / BC2) + ADAM_EPS) + ADAM_WD * w)
    return delta, m2, v2


def _adamw_big(name, landed, w, m, v, after):
    n = len(w)
    rows = w[0].shape[0]
    tr = 256 if rows % 256 == 0 else (rows // 2 if rows > 256 else rows)
    nt = rows // tr

    def body(*refs):
        ins, outs = refs[:4 * n], refs[4 * n + 1:]
        for j in range(n):
            @pl.when(pl.program_id(0) == j)
            def _(j=j):
                l_ref, w_ref, m_ref, v_ref = ins[j], ins[n + j], ins[2 * n + j], ins[3 * n + j]
                g = l_ref[0].astype(f32)
                for k in range(1, NCHIP):
                    g = g + l_ref[k].astype(f32)
                outs[4 * j][...] = g
                outs[4 * j + 1][...], outs[4 * j + 2][...], outs[4 * j + 3][...] = _adamw_math(g, w_ref[...], m_ref[...], v_ref[...])

    def tile(j):
        return lambda jj, i: jnp.clip((jj - j) * nt + i, 0, nt - 1)

    blk = [pl.BlockSpec((tr, D), lambda jj, i, t=tile(j): (t(jj, i), 0)) for j in range(n)]
    lnd = [pl.BlockSpec((NCHIP, tr, D), lambda jj, i, t=tile(j): (0, t(jj, i), 0)) for j in range(n)]
    o = jax.ShapeDtypeStruct((rows, D), f32)
    res = pl.pallas_call(
        body, name=name, grid=(n, nt), out_shape=[o] * (4 * n),
        in_specs=lnd + blk + blk + blk + [_HBM],
        out_specs=[blk[j] for j in range(n) for _ in range(4)],
        compiler_params=_params("arbitrary", "arbitrary"),
    )(*landed, *w, *m, *v, after)
    return [res[4 * j:4 * j + 4] for j in range(n)]


ROW_FFN1_NORM, ROW_MIX_NORM, ROW_SGU_G, ROW_SGU_B, ROW_FFN2_NORM, ROW_FINAL_NORM, ROW_B_IN = 0, 1, 2, 3, 4, 5, 8
ROW_WS, ROW_BS = 0, G * C


def _adamw_small(ga, gs, gd, gn1, gl, params):
    def body(ga_ref, gs_ref, gd_ref, gn1_ref, gl_ref, *refs):
        ins, outs = refs[:30], refs[30:]

        def total(ref, r0, n):
            g = ref[0, r0:r0 + n, :]
            for j in range(1, NDEV):
                g = g + ref[j, r0:r0 + n, :]
            return g

        def apply(i, g, rows=slice(None), cols=slice(None)):
            w, m, v = ins[3 * i][rows, cols], ins[3 * i + 1][rows, cols], ins[3 * i + 2][rows, cols]
            outs[4 * i][rows, cols] = g
            outs[4 * i + 1][rows, cols], outs[4 * i + 2][rows, cols], outs[4 * i + 3][rows, cols] = _adamw_math(g, w, m, v)

        outs[40][...] = total(gl_ref, 0, 8)[0:1, 0:1]
        apply(0, total(gn1_ref, 0, 1))
        for i, r in enumerate((ROW_FFN1_NORM, ROW_MIX_NORM, ROW_SGU_G, ROW_SGU_B, ROW_FFN2_NORM, ROW_FINAL_NORM)):
            if i:
                apply(i, total(ga_ref, r, 1))
        for k in range(8):
            apply(6, total(ga_ref, ROW_B_IN + k, 1), cols=slice(k * D, (k + 1) * D))
        apply(7, total(gs_ref, ROW_WS, G * C))
        for gi in range(G):
            apply(8, total(gs_ref, ROW_BS + 8 * gi, 1), slice(gi, gi + 1))
        dec = total(gd_ref, 0, 8)
        for hh in range(1, H):
            dec = dec + total(gd_ref, 8 * hh, 8)
        apply(9, dec[0:2, 0:H])

    flat = [a for p in params for a in p]
    out_shape = [jax.ShapeDtypeStruct(p[0].shape, f32) for p in params for _ in range(4)]
    out_shape.append(jax.ShapeDtypeStruct((1, 1), f32))
    vm = pl.BlockSpec(memory_space=pltpu.VMEM)
    return pl.pallas_call(
        body, name="adamw_small", out_shape=out_shape,
        in_specs=[vm] * (5 + len(flat)), out_specs=[vm] * len(out_shape),
        compiler_params=pltpu.CompilerParams(vmem_limit_bytes=VMEM_LIMIT),
    )(ga, gs, gd, gn1, gl, *flat)


def kernel(x, ffn1_norm, ffn1_w_gate, ffn1_w_up, ffn1_w_down, mix_norm, w_in, b_in, sgu_norm_g, sgu_norm_b, sgu_w_s, sgu_b_s, ret_decay_logit, w_branch_a, w_branch_b, w_out, ffn2_norm, ffn2_w_gate, ffn2_w_up, ffn2_w_down, final_norm, loss_target, m_ffn1_norm, m_ffn1_w_gate, m_ffn1_w_up, m_ffn1_w_down, m_mix_norm, m_w_in, m_b_in, m_sgu_norm_g, m_sgu_norm_b, m_sgu_w_s, m_sgu_b_s, m_ret_decay_logit, m_w_branch_a, m_w_branch_b, m_w_out, m_ffn2_norm, m_ffn2_w_gate, m_ffn2_w_up, m_ffn2_w_down, m_final_norm, v_ffn1_norm, v_ffn1_w_gate, v_ffn1_w_up, v_ffn1_w_down, v_mix_norm, v_w_in, v_b_in, v_sgu_norm_g, v_sgu_norm_b, v_sgu_w_s, v_sgu_b_s, v_ret_decay_logit, v_w_branch_a, v_w_branch_b, v_w_out, v_ffn2_norm, v_ffn2_w_gate, v_ffn2_w_up, v_ffn2_w_down, v_final_norm):
    args = dict(locals())
    S = x.shape[1]
    xs = x[0]
    target = loss_target[0]

    def buf_layout(name, a):
        a = a[0]
        return a.T if name in W_TRANSPOSED else a

    sh = {n: buf_layout(n, args[n]).astype(CDT) for n in W_NAMES}
    wf = {}

    b3 = b_in.reshape(8, 1, D)
    ws = sgu_w_s[0].astype(CDT)
    wst = jnp.swapaxes(sgu_w_s[0], 1, 2).astype(CDT)
    bsc = sgu_b_s[0].reshape(G, C, 1)
    dl = jnp.zeros((H, 8, DK), f32).at[:, 0:2, :].set(jnp.broadcast_to(ret_decay_logit[0].T[:, :, None], (H, 2, DK)))
    theta = ROPE_BASE ** (-jnp.arange(0, DK, 2, dtype=f32) / DK)
    ang = jnp.arange(S, dtype=f32)[:, None] * theta[None, :]
    cos, sin = jnp.cos(ang), jnp.sin(ang)
    fnorm = final_norm.reshape(1, D)

    f1 = ("ffn1_w_gate", "ffn1_w_up", "ffn1_w_down")
    f2 = ("ffn2_w_gate", "ffn2_w_up", "ffn2_w_down")
    br = ("w_branch_a", "w_branch_b", "w_out")
    for cid, names in enumerate((f1, ("w_in",), br, f2)):
        wf.update(zip(names, _sequence("ag_" + names[0], 1 + cid, NEIGHBOURS, [_ag_hook(sh[n]) for n in names])))
    x1, g1, u1, a1, hf1 = _ffn_fwd("ffn1_fwd", xs, ffn1_norm, *[wf[n] for n in f1])
    proj, h2 = _proj_fwd(x1, mix_norm, wf["w_in"], b3, cos, sin)
    a = _sgu_fwd(proj, sgu_norm_g, sgu_norm_b, ws, bsc)
    R, r, sfs, sbs = _ret_fwd(proj, dl)
    x2, ya, yb = _merge_fwd(a, r, proj, x1, *[wf[n] for n in br])
    x3, g2, u2, a2, hf2 = _ffn_fwd("ffn2_fwd", x2, ffn2_norm, *[wf[n] for n in f2])
    dx3, dyh2, d_final, loss_part = _loss_head(x3, fnorm, target)

    my_c = lax.axis_index("c").astype(jnp.int32).reshape(1)
    gw, landed, sequenced = {}, {}, []

    def d2d(*names):
        return [_rs_d2d_hook(gw[n]) for n in names]

    def behind(x, token):
        return lax.optimization_barrier((x, token))[0]

    def to_chips(names, sibs, more=()):
        parts = list(_rs_sum("rs_sum_" + names[0], [gw[n] for n in names], list(sibs), my_c))
        token = parts[0]
        if sequenced:
            parts[0] = behind(parts[0], sequenced[-1])
        hooks = [_rs_ici_hook(p) for p in parts] + ([_small_hook(list(more))] if more else [])
        reach = EVERYONE if more else CHIPS
        got = _sequence("rs_chips_" + names[0], REACH_ID[reach], reach, hooks)
        sequenced.append(got[0])
        landed.update(zip(names, got))
        return got[len(names):], token

    def ffn_bwd(tag, names, dy, dyh, x, g, u, a, h, nrm, each_alone, more=()):
        wg, wu, wd = names
        (gw[wd],) = _wgrad(tag + "_wd_grad", a, dyh)
        dg, du, sib_d = _ffn_bwd_hidden(tag + "_bwd_hidden", dyh, g, u, wf[wd], d2d(wd))
        if each_alone:
            dg = behind(dg, to_chips([wd], [sib_d])[1])
        (gw[wg],) = _wgrad(tag + "_wg_grad", dg, h)
        gw[wu], sib_g = _wgrad(tag + "_wu_grad", du, h, d2d(wg))
        if each_alone:
            (sib_u,) = _sequence("rs_sib_" + wu, SIBLING_ID, SIBLING, [_rs_d2d_hook(behind(gw[wu], sequenced[-1]))])
            sequenced.append(sib_u)
            dy = behind(dy, to_chips([wg], [sib_g])[1])
            dx, dn = _ffn_bwd_in(tag + "_bwd_in", dy, x, dg, du, nrm, wf[wg], wf[wu])
            return (dx, dn) + to_chips([wu], [sib_u], more(dn))
        dx, dn, sib_u = _ffn_bwd_in(tag + "_bwd_in", dy, x, dg, du, nrm, wf[wg], wf[wu], d2d(wu))
        return (dx, dn) + to_chips([wd, wg, wu], [sib_d, sib_g, sib_u])

    dx2, d_ffn2n, _, token = ffn_bwd("ffn2", f2, dx3, dyh2, x2, g2, u2, a2, hf2, ffn2_norm, False)
    dproj, da, dr, mix, dx2b, dya, dyb = _merge_bwd_act(behind(dx2, token), ya, yb, proj, *[wf[n] for n in br])
    dproj, d_ws, d_bs, d_gn, d_bn = _sgu_bwd(da, proj, dproj, sgu_norm_g, sgu_norm_b, ws, wst, bsc)
    dproj, d_dec = _ret_bwd(dr, R, sfs, sbs, proj, dproj, cos, sin, dl)
    (gw["w_in"],) = _tn("win_grad", h2, dproj, _seg_of_slot)
    gw["w_out"], sib_win = _wgrad("wo_grad", mix, dx2b, d2d("w_in"))
    small_sgu = jnp.concatenate([d_ws.reshape(G * C, C), d_bs.reshape(G * 8, C)], axis=0)
    (g_sgu, gl), token = to_chips(["w_in"], [sib_win], [small_sgu, loss_part])
    (gw["w_branch_a"],) = _wgrad("wa_grad", a, behind(dya, token))
    (gw["w_branch_b"],) = _wgrad("wb_grad", r, dyb)
    dx1, dyh1, d_mixn, d_bin, *sib_br = _proj_bwd_act(behind(dproj, token), dx2, x1, mix_norm, wf["w_in"], d2d(*br))
    small_a = jnp.concatenate([jnp.zeros((1, D), f32), d_mixn, d_gn, d_bn, d_ffn2n, d_final, jnp.zeros((2, D), f32),
                               d_bin.reshape(8, D)], axis=0)
    (ga, g_dec), token = to_chips(list(br), sib_br, [small_a, d_dec.reshape(H * 8, 128)])
    dxs, d_ffn1n, (gn1,), token = ffn_bwd("ffn1", f1, dx1, behind(dyh1, token), xs, g1, u1, a1, hf1, ffn1_norm, True,
                                          lambda dn: [dn])

    out = {"grad_x": dxs[None]}

    def native(name, a):
        a = a.T if name in W_TRANSPOSED else a
        return a[None]

    after = token
    for names in (f2, ("w_in",), br, (f1[2], f1[0], f1[1])):
        res = _adamw_big("adamw_" + names[0], [landed[n] for n in names], [buf_layout(n, args[n]) for n in names],
                         [buf_layout(n, args["m_" + n]) for n in names], [buf_layout(n, args["v_" + n]) for n in names], after)
        after = res[-1][0]
        for n, four in zip(names, res):
            for pre, val in zip(("grad_", "delta_", "new_m_", "new_v_"), four):
                out[pre + n] = native(n, val)

    small = [
        ("ffn1_norm", lambda a: a, lambda a: a), ("mix_norm", lambda a: a, lambda a: a),
        ("sgu_norm_g", lambda a: a, lambda a: a), ("sgu_norm_b", lambda a: a, lambda a: a),
        ("ffn2_norm", lambda a: a, lambda a: a),
        ("final_norm", lambda a: a.reshape(1, D), lambda a: a.reshape(D)),
        ("b_in", lambda a: a, lambda a: a),
        ("sgu_w_s", lambda a: a.reshape(G * C, C), lambda a: a.reshape(1, G, C, C)),
        ("sgu_b_s", lambda a: a[0], lambda a: a[None]),
        ("ret_decay_logit", lambda a: a[0], lambda a: a[None]),
    ]
    res = _adamw_small(ga, g_sgu, g_dec, gn1, gl,
                       [(to(args[n]), to(args["m_" + n]), to(args["v_" + n])) for n, to, _ in small])
    out["loss"] = res[40].reshape(())
    for i, (n, _, back) in enumerate(small):
        for j, pre in enumerate(("grad_", "delta_", "new_m_", "new_v_")):
            out[pre + n] = back(res[4 * i + j])

    weights = ("ffn1_norm", "ffn1_w_gate", "ffn1_w_up", "ffn1_w_down", "mix_norm", "w_in", "b_in", "sgu_norm_g",
               "sgu_norm_b", "sgu_w_s", "sgu_b_s", "ret_decay_logit", "w_branch_a", "w_branch_b", "w_out", "ffn2_norm",
               "ffn2_w_gate", "ffn2_w_up", "ffn2_w_down", "final_norm")
    return (out["loss"], out["grad_x"], *[out["grad_" + n] for n in weights], *[out["delta_" + n] for n in weights],
            *[out["new_m_" + n] for n in weights], *[out["new_v_" + n] for n in weights])
```

```python
import math

import jax
import jax.numpy as jnp
from jax import lax
from jax.experimental import pallas as pl
from jax.experimental.pallas import tpu as pltpu
from jax.experimental.pallas import tpu_sc as plsc

f32 = jnp.float32
CDT = jnp.bfloat16

D = 1024
F = 2816
C = 128
RC = 256
H = 4
DK = 256
G = 4
NDEV = 8
NCHIP = 4
EPS = 1e-6
ROPE_BASE = 10000.0
FT = 256
V7X_VMEM_BYTES = 64 * 1024 * 1024
VMEM_LIMIT = V7X_VMEM_BYTES - 8 * 1024 * 1024

ADAM_LR, ADAM_B1, ADAM_B2, ADAM_EPS, ADAM_WD, ADAM_STEP = 0.001, 0.9, 0.999, 1e-08, 0.01, 10
BC1 = 1.0 - ADAM_B1 ** ADAM_STEP
BC2 = 1.0 - ADAM_B2 ** ADAM_STEP

W_ROWS = dict(ffn1_w_gate=352, ffn1_w_up=352, ffn1_w_down=352, w_in=1024, w_branch_a=128, w_branch_b=128, w_out=128,
              ffn2_w_gate=352, ffn2_w_up=352, ffn2_w_down=352)
W_NAMES = tuple(W_ROWS)
W_TRANSPOSED = ("ffn1_w_gate", "ffn1_w_up", "ffn2_w_gate", "ffn2_w_up")

SLOT_U, SLOT_V, SLOT_GA, SLOT_GB, SLOT_Q, SLOT_K, SLOT_VR, SLOT_GR = range(8)


SEG_OF_SLOT = (0, 1, 6, 7, 2, 3, 4, 5)


def _seg_of_slot(p):
    return jnp.where(p < 2, p, jnp.where(p < 4, p + 4, p - 2))


def _mm(a, b):
    return jnp.dot(a, b, preferred_element_type=f32)


def _mm_nt(a, b):
    return lax.dot_general(a, b, (((1,), (1,)), ((), ())), preferred_element_type=f32)


def _mm_tn(a, b):
    return lax.dot_general(a, b, (((0,), (0,)), ((), ())), preferred_element_type=f32)


def _params(*sem):
    return pltpu.CompilerParams(dimension_semantics=sem, vmem_limit_bytes=VMEM_LIMIT)


def _resident(shape, index_map):
    return pl.BlockSpec(shape, index_map, pipeline_mode=pl.Buffered(1))


def _gelu(x):
    return 0.5 * x * (1.0 + lax.erf(x * (1.0 / math.sqrt(2.0))))


def _gelu_and_grad(x):
    cdf = 0.5 * (1.0 + lax.erf(x * (1.0 / math.sqrt(2.0))))
    return x * cdf, cdf + x * jnp.exp(-0.5 * x * x) * (1.0 / math.sqrt(2.0 * math.pi))


def _rms_fwd(x, n):
    r = lax.rsqrt(jnp.mean(x * x, axis=-1, keepdims=True) + EPS)
    xh = x * r
    return r, xh, xh * n


def _rms_bwd(dh, r, xh, n):
    dxh = dh * n
    dx = r * (dxh - xh * jnp.mean(dxh * xh, axis=-1, keepdims=True))
    return dx, jnp.sum(dh * xh, axis=0, keepdims=True)


MESH_ID = pl.DeviceIdType.MESH
_HBM = pl.BlockSpec(memory_space=pltpu.HBM)


def _my_place():
    return lax.axis_index("x"), lax.axis_index("y"), lax.axis_index("c")


def _ici_peers(x, y, c):
    return [((1 - x, y, c), 2 * (1 - x) + y), ((x, 1 - y, c), 2 * x + 1 - y), ((1 - x, 1 - y, c), 2 * (1 - x) + 1 - y)]


class _Hook:
    def __init__(self, operands, out_shapes, n_remote, n_local, start, finish, relay=None):
        self.operands, self.out_shapes = list(operands), list(out_shapes)
        self.n_remote, self.n_local, self.start, self.finish = n_remote, n_local, start, finish
        self.relay = relay or (lambda *a: None)


def _call(body, hooks, operands, *, in_specs, out_specs, out_shape, grid=None, scratch_shapes=(), **kw):
    hooks = tuple(hooks)
    n_in, n_out, n_scr = len(in_specs), len(out_shape), len(scratch_shapes)
    h_ops = [a for h in hooks for a in h.operands]
    h_outs = [s for h in hooks for s in h.out_shapes]
    h_sems = [pltpu.SemaphoreType.DMA((n,)) for h in hooks for n in (h.n_remote, h.n_remote, max(h.n_local, 1))]

    def wrapped(*refs):
        ins, hin = refs[:n_in], refs[n_in:n_in + len(h_ops)]
        o0 = n_in + len(h_ops)
        outs, hout = refs[o0:o0 + n_out], refs[o0 + n_out:o0 + n_out + len(h_outs)]
        s0 = o0 + n_out + len(h_outs)
        scr, hsem = refs[s0:s0 + n_scr], refs[s0 + n_scr:]

        def run(phase):
            ip = op = 0
            for i, h in enumerate(hooks):
                ssem, rsem, lsem = hsem[3 * i:3 * i + 3]

                def rcopy(k, src, dst, dev, ssem=ssem, rsem=rsem):
                    return pltpu.make_async_remote_copy(src_ref=src, dst_ref=dst, send_sem=ssem.at[k], recv_sem=rsem.at[k],
                                                        device_id=dev, device_id_type=MESH_ID)

                def lcopy(k, src, dst, lsem=lsem):
                    return pltpu.make_async_copy(src, dst, lsem.at[k])

                getattr(h, phase)(hin[ip:ip + len(h.operands)], hout[op:op + len(h.out_shapes)], rcopy, lcopy)
                ip += len(h.operands)
                op += len(h.out_shapes)

        def at_edge(phase, last):
            if not hooks:
                return
            if grid is None:
                run(phase)
                return
            cond = None
            for ax, n in enumerate(grid):
                here = pl.program_id(ax) == (n - 1 if last else 0)
                cond = here if cond is None else cond & here
            pl.when(cond)(lambda: run(phase))

        at_edge("start", False)
        at_edge("relay", True)
        body(*ins, *outs, *scr)
        at_edge("finish", True)

    if grid is not None:
        kw["grid"] = grid
    return list(pl.pallas_call(
        wrapped, out_shape=list(out_shape) + h_outs, in_specs=list(in_specs) + [_HBM] * len(h_ops),
        out_specs=list(out_specs) + [_HBM] * len(h_outs), scratch_shapes=list(scratch_shapes) + h_sems, **kw,
    )(*operands, *h_ops))


def _rows(ref, start, n):
    return ref.at[pl.ds(start, n), :]


def _ag_hook(shard):
    rows = shard.shape[0]
    half = rows // 2
    assert half % 16 == 0

    def place():
        x, y, c = _my_place()
        devs = dict(sib=(x, y, 1 - c), xn=(1 - x, y, c), yn=(x, 1 - y, c))
        chips = dict(me=2 * x + y, xn=2 * (1 - x) + y, yn=2 * x + 1 - y, dg=2 * (1 - x) + 1 - y)
        return c, devs, chips

    def block(full, chip, c):
        return _rows(full, (2 * chip + c) * rows, rows)

    def halfblock(full, chip, c, upper):
        return _rows(full, (2 * chip + c) * rows + upper * half, half)

    def start(ins, outs, rcopy, lcopy):
        c, devs, chips = place()
        src, dst = ins[0], block(outs[0], chips["me"], c)
        lcopy(0, src, dst).start()
        for k, to in enumerate(("sib", "xn", "yn")):
            rcopy(k, src, dst, devs[to]).start()

    def relay(ins, outs, rcopy, lcopy):
        c, devs, chips = place()
        full = outs[0]
        blk = block(full, chips["xn"], c)
        rcopy(1, blk, blk, devs["xn"]).wait_recv()
        low = halfblock(full, chips["xn"], c, 0)
        rcopy(3, low, low, devs["yn"]).start()
        rcopy(5, blk, blk, devs["sib"]).start()
        blk = block(full, chips["yn"], c)
        rcopy(2, blk, blk, devs["yn"]).wait_recv()
        up = halfblock(full, chips["yn"], c, 1)
        rcopy(4, up, up, devs["xn"]).start()
        rcopy(6, blk, blk, devs["sib"]).start()
        low, up = halfblock(full, chips["dg"], c, 0), halfblock(full, chips["dg"], c, 1)
        rcopy(3, low, low, devs["yn"]).wait_recv()
        rcopy(4, up, up, devs["xn"]).wait_recv()
        blk = block(full, chips["dg"], c)
        rcopy(7, blk, blk, devs["sib"]).start()

    def finish(ins, outs, rcopy, lcopy):
        c, devs, chips = place()
        full, sib = outs[0], devs["sib"]
        for k, chip in ((0, "me"), (5, "xn"), (6, "yn"), (7, "dg")):
            theirs = block(full, chips[chip], 1 - c)
            rcopy(k, theirs, theirs, sib).wait_recv()
            mine = block(full, chips[chip], c)
            if k:
                rcopy(k, mine, mine, sib).wait_send()
        src, dst = ins[0], block(full, chips["me"], c)
        lcopy(0, src, dst).wait()
        for k, to in enumerate(("sib", "xn", "yn")):
            rcopy(k, src, dst, devs[to]).wait_send()
        low, up = halfblock(full, chips["xn"], c, 0), halfblock(full, chips["yn"], c, 1)
        rcopy(3, low, low, devs["yn"]).wait_send()
        rcopy(4, up, up, devs["xn"]).wait_send()

    return _Hook([shard], [jax.ShapeDtypeStruct((NDEV * rows, D), shard.dtype)], 8, 1, start, finish, relay)


SIBLING, CHIPS, NEIGHBOURS, EVERYONE = "sibling", "chips", "sibling and the two neighbour chips", "everyone"
REACH_ID = {CHIPS: 6, EVERYONE: 7}
SIBLING_ID = 5


def _sequence(name, collective_id, reach, hooks):
    ins = [[jax.new_ref(a, memory_space=pltpu.MemorySpace.HBM) for a in h.operands] for h in hooks]
    outs = [[jax.empty_ref(s, memory_space=pltpu.MemorySpace.HBM) for s in h.out_shapes] for h in hooks]
    sems = tuple(pltpu.SemaphoreType.DMA((n,)) for h in hooks for n in (h.n_remote, h.n_remote, max(h.n_local, 1)))

    @pl.kernel(mesh=plsc.ScalarSubcoreMesh(axis_name="sequencer", num_cores=1), name=name, scratch_types=sems,
               compiler_params=pltpu.CompilerParams(collective_id=collective_id))
    def launch(*sem_refs):
        x, y, c = _my_place()
        chips = [dev for dev, _ in _ici_peers(x, y, c)]
        others = [(1 - x if dx else x, 1 - y if dy else y, 1 - c if dc else c)
                  for dx in range(2) for dy in range(2) for dc in range(2) if dx + dy + dc]
        devs = {SIBLING: [(x, y, 1 - c)], CHIPS: chips, NEIGHBOURS: [(x, y, 1 - c), (1 - x, y, c), (x, 1 - y, c)],
                EVERYONE: others}[reach]
        barrier = pltpu.get_barrier_semaphore()
        for dev in devs:
            pl.semaphore_signal(barrier, inc=1, device_id=dev, device_id_type=MESH_ID)
        pl.semaphore_wait(barrier, len(devs))
        for phase in ("start", "relay", "finish"):
            for i, h in enumerate(hooks):
                ssem, rsem, lsem = sem_refs[3 * i:3 * i + 3]

                def rcopy(k, src, dst, dev, ssem=ssem, rsem=rsem):
                    return pltpu.make_async_remote_copy(src_ref=src, dst_ref=dst, send_sem=ssem.at[k], recv_sem=rsem.at[k],
                                                        device_id=dev, device_id_type=MESH_ID)

                def lcopy(k, src, dst, lsem=lsem):
                    return pltpu.make_async_copy(src, dst, lsem.at[k])

                getattr(h, phase)(ins[i], outs[i], rcopy, lcopy)

    launch()
    return [o[...] for os in outs for o in os]


def _rs_d2d_hook(gfull):
    rows = gfull.shape[0] // NDEV

    def pairs(g, land):
        x, y, c = _my_place()
        return (x, y, 1 - c), [(k, _rows(g, (2 * k + 1 - c) * rows, rows), land.at[k]) for k in range(NCHIP)]

    def start(ins, outs, rcopy, lcopy):
        sib, cps = pairs(ins[0], outs[0])
        for i, src, dst in cps:
            rcopy(i, src, dst, sib).start()

    def finish(ins, outs, rcopy, lcopy):
        sib, cps = pairs(ins[0], outs[0])
        for i, src, dst in cps:
            rcopy(i, dst, dst, sib).wait_recv()
        for i, src, dst in cps:
            rcopy(i, src, dst, sib).wait_send()

    return _Hook([gfull], [jax.ShapeDtypeStruct((NCHIP, rows, D), gfull.dtype)], NCHIP, 0, start, finish)


def _rs_ici_hook(part):
    def start(ins, outs, rcopy, lcopy):
        x, y, c = _my_place()
        mychip = 2 * x + y
        lcopy(0, ins[0].at[mychip], outs[0].at[mychip]).start()
        for j, (dev, chip) in enumerate(_ici_peers(x, y, c)):
            rcopy(j, ins[0].at[chip], outs[0].at[mychip], dev).start()

    def finish(ins, outs, rcopy, lcopy):
        x, y, c = _my_place()
        mychip = 2 * x + y
        peers = _ici_peers(x, y, c)
        for j, (dev, chip) in enumerate(peers):
            rcopy(j, outs[0].at[chip], outs[0].at[chip], dev).wait_recv()
        for j, (dev, chip) in enumerate(peers):
            rcopy(j, ins[0].at[chip], outs[0].at[mychip], dev).wait_send()
        lcopy(0, ins[0].at[mychip], outs[0].at[mychip]).wait()

    return _Hook([part], [jax.ShapeDtypeStruct(part.shape, part.dtype)], 3, 1, start, finish)


def _small_hook(arrays):
    n = len(arrays)

    def peers():
        x, y, c = _my_place()
        out = []
        for dx in range(2):
            for dy in range(2):
                for dc in range(2):
                    if dx + dy + dc:
                        px, py, pc = (1 - x if dx else x), (1 - y if dy else y), (1 - c if dc else c)
                        out.append(((px, py, pc), 4 * px + 2 * py + pc))
        return 4 * x + 2 * y + c, out

    def start(ins, outs, rcopy, lcopy):
        me, ps = peers()
        for t in range(n):
            lcopy(t, ins[t], outs[t].at[me]).start()
            for i, (dev, _) in enumerate(ps):
                rcopy(n * i + t, ins[t], outs[t].at[me], dev).start()

    def finish(ins, outs, rcopy, lcopy):
        me, ps = peers()
        for t in range(n):
            for i, (dev, peer) in enumerate(ps):
                rcopy(n * i + t, outs[t].at[peer], outs[t].at[peer], dev).wait_recv()
            for i, (dev, _) in enumerate(ps):
                rcopy(n * i + t, ins[t], outs[t].at[me], dev).wait_send()
            lcopy(t, ins[t], outs[t].at[me]).wait()

    return _Hook(arrays, [jax.ShapeDtypeStruct((NDEV,) + a.shape, a.dtype) for a in arrays], 7 * n, n, start, finish)


def _wblock(w):
    return _resident(w.shape, lambda *_: (0, 0))


def _ffn_fwd(name, x, nrm, wg, wu, wd, hooks=(), tm=512):
    S = x.shape[0]

    def body(x_ref, n_ref, wg_ref, wu_ref, wd_ref, y_ref, g_ref, u_ref, a_ref, h_ref, acc_ref):
        xv = x_ref[...]
        _, _, h = _rms_fwd(xv, n_ref[...])
        h = h.astype(CDT)
        h_ref[...] = h
        for ci in range(F // FT):
            sl = slice(ci * FT, (ci + 1) * FT)
            g = _mm_nt(h, wg_ref[sl, :])
            u = _mm_nt(h, wu_ref[sl, :])
            g_ref[:, sl] = g.astype(CDT)
            u_ref[:, sl] = u.astype(CDT)
            a = (g * jax.nn.sigmoid(g) * u).astype(CDT)
            a_ref[:, sl] = a
            o = _mm(a, wd_ref[sl, :])
            if ci == 0:
                acc_ref[...] = o
            else:
                acc_ref[...] += o
        y_ref[...] = xv + 0.5 * acc_ref[...]

    tok = pl.BlockSpec((tm, D), lambda i: (i, 0))
    hid = pl.BlockSpec((tm, F), lambda i: (i, 0))
    hidden = jax.ShapeDtypeStruct((S, F), CDT)
    return _call(
        body, hooks, [x, nrm, wg, wu, wd], name=name, grid=(S // tm,),
        out_shape=[jax.ShapeDtypeStruct((S, D), f32), hidden, hidden, hidden, jax.ShapeDtypeStruct((S, D), CDT)],
        in_specs=[tok, _resident((1, D), lambda i: (0, 0)), _wblock(wg), _wblock(wu), _wblock(wd)],
        out_specs=[tok, hid, hid, hid, tok],
        scratch_shapes=[pltpu.VMEM((tm, D), f32)],
        compiler_params=_params("arbitrary"),
    )


def _proj_fwd(x1, nrm, wfull, b3, cos, sin, hooks=(), tm=512):
    S = x1.shape[0]

    def body(x_ref, n_ref, w_ref, b_ref, cos_ref, sin_ref, p_ref, h_ref):
        _, _, h = _rms_fwd(x_ref[...], n_ref[...])
        h = h.astype(CDT)
        h_ref[...] = h
        for p in range(8):
            seg = SEG_OF_SLOT[p]
            z = _mm(h, w_ref[seg * D:(seg + 1) * D, :]) + b_ref[seg]
            if p in (SLOT_Q, SLOT_K):
                co, si = cos_ref[...], sin_ref[...]
                for hh in range(H):
                    cs = slice(hh * DK, (hh + 1) * DK)
                    zr = _rotate(z[:, cs], co, si)
                    p_ref[p, :, cs] = (zr * K_SCALE if p == SLOT_K else zr).astype(CDT)
            else:
                p_ref[p] = z.astype(CDT)

    tab = pl.BlockSpec((tm, DK // 2), lambda i: (i, 0))
    return _call(
        body, hooks, [x1, nrm, wfull, b3, cos, sin], name="proj_fwd", grid=(S // tm,),
        out_shape=[jax.ShapeDtypeStruct((8, S, D), CDT), jax.ShapeDtypeStruct((S, D), CDT)],
        in_specs=[pl.BlockSpec((tm, D), lambda i: (i, 0)), _resident((1, D), lambda i: (0, 0)),
                  _resident((8 * D, D), lambda i: (0, 0)), _resident((8, 1, D), lambda i: (0, 0, 0)), tab, tab],
        out_specs=[pl.BlockSpec((8, tm, D), lambda i: (0, i, 0)), pl.BlockSpec((tm, D), lambda i: (i, 0))],
        compiler_params=_params("arbitrary"),
    )


def _sgu_norm(va, gn, bn):
    mu = jnp.mean(va, axis=-1, keepdims=True)
    xc = va - mu
    rstd = lax.rsqrt(jnp.mean(xc * xc, axis=-1, keepdims=True) + EPS)
    vhat = xc * rstd
    return rstd, vhat, vhat * gn + bn


def _sgu_fwd(proj, gn, bn, ws, bsc, tm=512):
    S = proj.shape[1]
    GW = D // G

    def body(p_ref, gn_ref, bn_ref, ws_ref, bs_ref, a_ref):
        ua = _gelu(p_ref[0].astype(f32))
        va = _gelu(p_ref[1].astype(f32))
        _, _, vn = _sgu_norm(va, gn_ref[...], bn_ref[...])
        vn = vn.astype(CDT)
        for ch in range(tm // C):
            rs = slice(ch * C, (ch + 1) * C)
            for gi in range(G):
                cs = slice(gi * GW, (gi + 1) * GW)
                s = _mm(ws_ref[gi], vn[rs, cs]) + bs_ref[gi]
                a_ref[rs, cs] = (ua[rs, cs] * s).astype(CDT)

    return pl.pallas_call(
        body, name="sgu_fwd", grid=(S // tm,),
        out_shape=jax.ShapeDtypeStruct((S, D), CDT),
        in_specs=[pl.BlockSpec((2, tm, D), lambda i: (0, i, 0)), _resident((1, D), lambda i: (0, 0)),
                  _resident((1, D), lambda i: (0, 0)), _resident((G, C, C), lambda i: (0, 0, 0)),
                  _resident((G, C, 1), lambda i: (0, 0, 0))],
        out_specs=pl.BlockSpec((tm, D), lambda i: (i, 0)),
        compiler_params=_params("arbitrary"),
    )(proj, gn, bn, ws, bsc)


def _decay_tables(dl_ref):
    lg = jax.nn.log_sigmoid(dl_ref[0:2, :])
    lgf, lgb = lg[0:1, :], lg[1:2, :]
    assert RC <= DK
    ri = lax.broadcasted_iota(jnp.int32, (RC, RC), 0)
    ci = lax.broadcasted_iota(jnp.int32, (RC, RC), 1)
    d = (ri - ci).astype(f32)
    lower = d >= 0
    dmat = jnp.where(lower, jnp.exp(d * lgf[:, :RC]), jnp.exp(-d * lgb[:, :RC]))
    dmat_t = jnp.where(d <= 0, jnp.exp(-d * lgf[:, :RC]), jnp.exp(d * lgb[:, :RC]))
    pos = lax.broadcasted_iota(jnp.int32, (RC, DK), 0).astype(f32)
    t = dict(
        lgf=lgf, lgb=lgb, d=d, lower=lower, dmat=dmat, dmat_t=dmat_t, pos=pos,
        fq=jnp.exp((pos + 1.0) * lgf), fk=jnp.exp((RC - 1.0 - pos) * lgf),
        bq=jnp.exp((RC - pos) * lgb), bk=jnp.exp(pos * lgb),
        lamf=jnp.exp(float(RC) * lgf), lamb=jnp.exp(float(RC) * lgb),
    )
    return t


def _rotate(t, co, si):
    t1, t2 = t[:, :DK // 2], t[:, DK // 2:]
    return jnp.concatenate([t1 * co - t2 * si, t2 * co + t1 * si], axis=-1)


def _unrotate(t, co, si):
    t1, t2 = t[:, :DK // 2], t[:, DK // 2:]
    return jnp.concatenate([t1 * co + t2 * si, t2 * co - t1 * si], axis=-1)


K_SCALE = DK ** -0.5
ROW_TILE = 256


def _ret_fwd(proj, dl, hooks=()):
    S = proj.shape[1]
    NC = S // RC

    def body(q_ref, k_ref, v_ref, g_ref, dl_ref, R_ref, r_ref, sfs_ref, sbs_ref, rb_ref, sf_ref, sb_ref):
        t = _decay_tables(dl_ref)

        def chunk(n):
            rows = pl.ds(pl.multiple_of(n * RC, RC), RC)
            return rows, q_ref[rows, :], k_ref[rows, :], v_ref[rows, :]

        sf_ref[...] = jnp.zeros_like(sf_ref)
        sb_ref[...] = jnp.zeros_like(sb_ref)

        def step(i, carry):
            rows, qn, kn, vn = chunk(i)
            sc = _mm_nt(qn, kn) * t["dmat"]
            out = _mm(sc.astype(CDT), vn)
            sf = sf_ref[...]
            sfb = sf.astype(CDT)
            sfs_ref[i] = sfb
            R_ref[rows, :] = out + _mm((qn.astype(f32) * t["fq"]).astype(CDT), sfb)
            sf_ref[...] = sf * t["lamf"] + _mm_tn((kn.astype(f32) * t["fk"]).astype(CDT), vn)
            m = NC - 1 - i
            rows, qn, kn, vn = chunk(m)
            sb = sb_ref[...]
            sbb = sb.astype(CDT)
            sbs_ref[m] = sbb
            rb_ref[rows, :] = _mm((qn.astype(f32) * t["bq"]).astype(CDT), sbb)
            sb_ref[...] = sb * t["lamb"] + _mm_tn((kn.astype(f32) * t["bk"]).astype(CDT), vn)
            return carry

        lax.fori_loop(0, NC, step, 0, unroll=8)

        def finish(i, carry):
            rs = pl.ds(pl.multiple_of(i * ROW_TILE, ROW_TILE), ROW_TILE)
            R = R_ref[rs, :] + rb_ref[rs, :]
            R_ref[rs, :] = R
            rn = R * lax.rsqrt(jnp.mean(R * R, axis=-1, keepdims=True) + EPS)
            g = g_ref[rs, :].astype(f32)
            r_ref[rs, :] = (rn * g * jax.nn.sigmoid(g)).astype(CDT)
            return carry

        lax.fori_loop(0, S // ROW_TILE, finish, 0, unroll=2)

    def seg(slot):
        return pl.BlockSpec((None, S, DK), lambda h: (slot, 0, h))

    states = jax.ShapeDtypeStruct((H, NC, DK, DK), CDT)
    state_blk = pl.BlockSpec((None, NC, DK, DK), lambda h: (h, 0, 0, 0))
    return _call(
        body, hooks, [proj, proj, proj, proj, dl], name="ret_fwd", grid=(H,),
        out_shape=[jax.ShapeDtypeStruct((S, H * DK), f32), jax.ShapeDtypeStruct((S, H * DK), CDT), states, states],
        in_specs=[seg(SLOT_Q), seg(SLOT_K), seg(SLOT_VR), seg(SLOT_GR), pl.BlockSpec((None, 8, DK), lambda h: (h, 0, 0))],
        out_specs=[pl.BlockSpec((S, DK), lambda h: (0, h)), pl.BlockSpec((S, DK), lambda h: (0, h)), state_blk, state_blk],
        scratch_shapes=[pltpu.VMEM((S, DK), f32), pltpu.VMEM((DK, DK), f32), pltpu.VMEM((DK, DK), f32)],
        compiler_params=_params("arbitrary"),
    )


def _merge_fwd(a, r, proj, x1, wa, wb, wo, hooks=(), tm=512):
    S = x1.shape[0]

    def body(a_ref, r_ref, gt_ref, x_ref, wa_ref, wb_ref, wo_ref, x2_ref, ya_ref, yb_ref):
        ya = _mm(a_ref[...], wa_ref[...])
        yb = _mm(r_ref[...], wb_ref[...])
        ya_ref[...] = ya.astype(CDT)
        yb_ref[...] = yb.astype(CDT)
        mix = jax.nn.sigmoid(gt_ref[0].astype(f32)) * ya + jax.nn.sigmoid(gt_ref[1].astype(f32)) * yb
        x2_ref[...] = x_ref[...] + _mm(mix.astype(CDT), wo_ref[...])

    tok = pl.BlockSpec((tm, D), lambda i: (i, 0))
    return _call(
        body, hooks, [a, r, proj, x1, wa, wb, wo], name="merge_fwd", grid=(S // tm,),
        out_shape=[jax.ShapeDtypeStruct((S, D), f32), jax.ShapeDtypeStruct((S, D), CDT), jax.ShapeDtypeStruct((S, D), CDT)],
        in_specs=[tok, tok, pl.BlockSpec((2, tm, D), lambda i: (SLOT_GA // 2, i, 0)), tok,
                  _wblock(wa), _wblock(wb), _wblock(wo)],
        out_specs=[tok, tok, tok],
        compiler_params=_params("arbitrary"),
    )


def _loss_head(x3, fn, target, tm=512):
    S = x3.shape[0]

    def body(x_ref, n_ref, t_ref, dx_ref, dxh_ref, dn_ref, l_ref):
        n = n_ref[...]
        r, xh, y = _rms_fwd(x_ref[...], n)
        e = y - t_ref[...]
        dy = e * (1.0 / D)
        dx, dn = _rms_bwd(dy, r, xh, n)
        dx_ref[...] = dx
        dxh_ref[...] = (0.5 * dx).astype(CDT)
        part = 0.5 * jnp.sum(jnp.sum(e * e, axis=-1, keepdims=True), axis=0, keepdims=True) * (1.0 / D)

        @pl.when(pl.program_id(0) == 0)
        def _():
            dn_ref[...] = jnp.zeros_like(dn_ref)
            l_ref[...] = jnp.zeros_like(l_ref)

        dn_ref[...] += dn
        l_ref[...] += jnp.broadcast_to(part, l_ref.shape)

    tok = pl.BlockSpec((tm, D), lambda i: (i, 0))
    return pl.pallas_call(
        body, name="loss_head", grid=(S // tm,),
        out_shape=[jax.ShapeDtypeStruct((S, D), f32), jax.ShapeDtypeStruct((S, D), CDT), jax.ShapeDtypeStruct((1, D), f32),
                   jax.ShapeDtypeStruct((8, 128), f32)],
        in_specs=[tok, _resident((1, D), lambda i: (0, 0)), tok],
        out_specs=[tok, tok, pl.BlockSpec((1, D), lambda i: (0, 0)), pl.BlockSpec((8, 128), lambda i: (0, 0))],
        compiler_params=_params("arbitrary"),
    )(x3, fn, target)


def _ffn_bwd_hidden(name, dyh, g, u, wd, hooks=(), tm=512):
    S = dyh.shape[0]

    def body(dyh_ref, g_ref, u_ref, wd_ref, dg_ref, du_ref):
        dyh = dyh_ref[...]
        for ci in range(F // FT):
            sl = slice(ci * FT, (ci + 1) * FT)
            da = _mm_nt(dyh, wd_ref[sl, :])
            gv = g_ref[:, sl].astype(f32)
            uv = u_ref[:, sl].astype(f32)
            s = jax.nn.sigmoid(gv)
            silu = gv * s
            du_ref[:, sl] = (da * silu).astype(CDT)
            dg_ref[:, sl] = (da * uv * (s + silu - silu * s)).astype(CDT)

    hid = pl.BlockSpec((tm, F), lambda i: (i, 0))
    hidden = jax.ShapeDtypeStruct((S, F), CDT)
    return _call(
        body, hooks, [dyh, g, u, wd], name=name, grid=(S // tm,), out_shape=[hidden, hidden],
        in_specs=[pl.BlockSpec((tm, D), lambda i: (i, 0)), hid, hid, _wblock(wd)], out_specs=[hid, hid],
        compiler_params=_params("arbitrary"),
    )


def _ffn_bwd_in(name, dy, x, dg, du, nrm, wg, wu, hooks=(), tm=512):
    S = x.shape[0]

    def body(dy_ref, x_ref, dg_ref, du_ref, n_ref, wg_ref, wu_ref, dx_ref, dn_ref, acc_ref):
        n = n_ref[...]
        r, xh, _ = _rms_fwd(x_ref[...], n)
        for ci in range(F // FT):
            sl = slice(ci * FT, (ci + 1) * FT)
            dh = _mm(dg_ref[:, sl], wg_ref[sl, :]) + _mm(du_ref[:, sl], wu_ref[sl, :])
            if ci == 0:
                acc_ref[...] = dh
            else:
                acc_ref[...] += dh
        dx, dn = _rms_bwd(acc_ref[...], r, xh, n)
        dx_ref[...] = dy_ref[...] + dx

        @pl.when(pl.program_id(0) == 0)
        def _():
            dn_ref[...] = jnp.zeros_like(dn_ref)

        dn_ref[...] += dn

    tok = pl.BlockSpec((tm, D), lambda i: (i, 0))
    hid = pl.BlockSpec((tm, F), lambda i: (i, 0))
    return _call(
        body, hooks, [dy, x, dg, du, nrm, wg, wu], name=name, grid=(S // tm,),
        out_shape=[jax.ShapeDtypeStruct((S, D), f32), jax.ShapeDtypeStruct((1, D), f32)],
        in_specs=[tok, tok, hid, hid, _resident((1, D), lambda i: (0, 0)), _wblock(wg), _wblock(wu)],
        out_specs=[tok, pl.BlockSpec((1, D), lambda i: (0, 0))],
        scratch_shapes=[pltpu.VMEM((tm, D), f32)],
        compiler_params=_params("arbitrary"),
    )


TN_ROWS = 512


def _tn(name, xs, ys, block_of, hooks=()):
    S, M = xs.shape
    B = ys.shape[0]
    tr = TN_ROWS if M % TN_ROWS == 0 else M // 2
    assert M % tr == 0 and tr % 128 == 0
    nt = M // tr

    def body(x_ref, y_ref, o_ref):
        o_ref[...] = _mm_tn(x_ref[...], y_ref[...]).astype(CDT)

    return _call(
        body, hooks, [xs, ys], name=name, grid=(B, nt),
        out_shape=[jax.ShapeDtypeStruct((B * M, D), CDT)],
        in_specs=[pl.BlockSpec((S, tr), lambda b, i: (0, i)), pl.BlockSpec((None, S, D), lambda b, i: (b, 0, 0))],
        out_specs=[pl.BlockSpec((tr, D), lambda b, i: (block_of(b) * nt + i, 0))],
        compiler_params=_params("arbitrary", "arbitrary"),
    )


def _wgrad(name, xs, y, hooks=()):
    return _tn(name, xs, y[None], lambda b: 0, hooks)


def _merge_bwd_act(dx2, ya, yb, proj, wa, wb, wo, hooks=(), tm=512):
    S = dx2.shape[0]

    def body(dx_ref, ya_ref, yb_ref, gt_ref, wa_ref, wb_ref, wo_ref,
             dp_ref, da_ref, dr_ref, mix_ref, dxb_ref, dya_ref, dyb_ref):
        dxb = dx_ref[...].astype(CDT)
        dxb_ref[...] = dxb
        dmix = _mm_nt(dxb, wo_ref[...])
        ya = ya_ref[...].astype(f32)
        yb = yb_ref[...].astype(f32)
        sa = jax.nn.sigmoid(gt_ref[0].astype(f32))
        sb = jax.nn.sigmoid(gt_ref[1].astype(f32))
        mix_ref[...] = (sa * ya + sb * yb).astype(CDT)
        dya = (dmix * sa).astype(CDT)
        dyb = (dmix * sb).astype(CDT)
        dya_ref[...] = dya
        dyb_ref[...] = dyb
        dp_ref[0] = (dmix * ya * sa * (1.0 - sa)).astype(CDT)
        dp_ref[1] = (dmix * yb * sb * (1.0 - sb)).astype(CDT)
        da_ref[...] = _mm_nt(dya, wa_ref[...]).astype(CDT)
        dr_ref[...] = _mm_nt(dyb, wb_ref[...]).astype(CDT)

    tok = pl.BlockSpec((tm, D), lambda i: (i, 0))
    gates = pl.BlockSpec((2, tm, D), lambda i: (SLOT_GA // 2, i, 0))
    act = jax.ShapeDtypeStruct((S, D), CDT)
    return _call(
        body, hooks, [dx2, ya, yb, proj, wa, wb, wo], name="merge_bwd_act", grid=(S // tm,),
        out_shape=[jax.ShapeDtypeStruct((8, S, D), CDT), act, act, act, act, act, act],
        in_specs=[tok, tok, tok, gates, _wblock(wa), _wblock(wb), _wblock(wo)],
        out_specs=[gates, tok, tok, tok, tok, tok, tok],
        compiler_params=_params("arbitrary"),
    )


def _sgu_bwd(da, proj, dproj, gn, bn, ws, wst, bsc, hooks=(), tm=512):
    S = proj.shape[1]
    GW = D // G

    def body(da_ref, p_ref, dpin_ref, gn_ref, bn_ref, ws_ref, wst_ref, bs_ref,
             dp_ref, dws_ref, dbs_ref, dgn_ref, dbn_ref, ds_ref, dvn_ref):
        @pl.when(pl.program_id(0) == 0)
        def _():
            dws_ref[...] = jnp.zeros_like(dws_ref)
            dbs_ref[...] = jnp.zeros_like(dbs_ref)
            dgn_ref[...] = jnp.zeros_like(dgn_ref)
            dbn_ref[...] = jnp.zeros_like(dbn_ref)

        pu = p_ref[0].astype(f32)
        pv = p_ref[1].astype(f32)
        ua, dua = _gelu_and_grad(pu)
        va, dva_dpv = _gelu_and_grad(pv)
        gn = gn_ref[...]
        rstd, vhat, vn = _sgu_norm(va, gn, bn_ref[...])
        vnb = vn.astype(CDT)
        dav = da_ref[...].astype(f32)
        dsb = (dav * ua).astype(CDT)
        ones = jnp.ones((8, GW), CDT)
        for ch in range(tm // C):
            rs = slice(ch * C, (ch + 1) * C)
            for gi in range(G):
                cs = slice(gi * GW, (gi + 1) * GW)
                s = _mm(ws_ref[gi], vnb[rs, cs]) + bs_ref[gi]
                ds_ref[rs, cs] = s
                dsg = dsb[rs, cs]
                dws_ref[gi] += _mm_nt(dsg, vnb[rs, cs])
                dbs_ref[gi] += _mm_nt(ones, dsg)
                dvn_ref[rs, cs] = _mm(wst_ref[gi], dsg)
        dp_ref[0] = (dav * ds_ref[...] * dua).astype(CDT)
        dvn = dvn_ref[...]
        dgn_ref[...] += jnp.sum(dvn * vhat, axis=0, keepdims=True)
        dbn_ref[...] += jnp.sum(dvn, axis=0, keepdims=True)
        dvh = dvn * gn
        dva = rstd * (dvh - jnp.mean(dvh, axis=-1, keepdims=True) - vhat * jnp.mean(dvh * vhat, axis=-1, keepdims=True))
        dp_ref[1] = (dva * dva_dpv).astype(CDT)

    uv = pl.BlockSpec((2, tm, D), lambda i: (0, i, 0))
    row = _resident((1, D), lambda i: (0, 0))
    return _call(
        body, hooks, [da, proj, dproj, gn, bn, ws, wst, bsc], name="sgu_bwd", grid=(S // tm,),
        out_shape=[jax.ShapeDtypeStruct(dproj.shape, CDT), jax.ShapeDtypeStruct((G, C, C), f32),
                   jax.ShapeDtypeStruct((G, 8, C), f32), jax.ShapeDtypeStruct((1, D), f32), jax.ShapeDtypeStruct((1, D), f32)],
        in_specs=[pl.BlockSpec((tm, D), lambda i: (i, 0)), uv, _HBM, row, row,
                  _resident((G, C, C), lambda i: (0, 0, 0)), _resident((G, C, C), lambda i: (0, 0, 0)),
                  _resident((G, C, 1), lambda i: (0, 0, 0))],
        out_specs=[uv, pl.BlockSpec((G, C, C), lambda i: (0, 0, 0)), pl.BlockSpec((G, 8, C), lambda i: (0, 0, 0)),
                   pl.BlockSpec((1, D), lambda i: (0, 0)), pl.BlockSpec((1, D), lambda i: (0, 0))],
        scratch_shapes=[pltpu.VMEM((tm, D), f32), pltpu.VMEM((tm, D), f32)],
        input_output_aliases={2: 0},
        compiler_params=_params("arbitrary"),
    )


def _ret_bwd(dr, R, sfs, sbs, proj, dproj, cos, sin, dl, hooks=()):
    S = proj.shape[1]
    NC = S // RC
    assert NC % 2 == 0

    def body(dr_ref, R_ref, sf_ref, sb_ref, q_ref, k_ref, v_ref, g_ref, dpin_ref, cos_ref, sin_ref, dl_ref,
             dp_ref, dd_ref, dR_ref, gb_ref, gf_ref, acc_ref):
        t = _decay_tables(dl_ref)

        def gate_norm_bwd(i, carry):
            rs = pl.ds(pl.multiple_of(i * ROW_TILE, ROW_TILE), ROW_TILE)
            Rv = R_ref[rs, :]
            rstd = lax.rsqrt(jnp.mean(Rv * Rv, axis=-1, keepdims=True) + EPS)
            rn = Rv * rstd
            gv = g_ref[rs, :].astype(f32)
            s = jax.nn.sigmoid(gv)
            drv = dr_ref[rs, :].astype(f32)
            dp_ref[3, rs, :] = (drv * rn * (s * (1.0 + gv * (1.0 - s)))).astype(CDT)
            drn = drv * gv * s
            dR_ref[rs, :] = (rstd * (drn - rn * jnp.mean(drn * rn, axis=-1, keepdims=True))).astype(CDT)
            return carry

        lax.fori_loop(0, S // ROW_TILE, gate_norm_bwd, 0, unroll=2)

        def chunk(n):
            rows = pl.ds(pl.multiple_of(n * RC, RC), RC)
            return rows, q_ref[rows, :], k_ref[rows, :], v_ref[rows, :], dR_ref[rows, :]

        def emit_kv(rows, dk, dv, final):
            if not final:
                dp_ref[1, rows, :] = dk.astype(CDT)
                dp_ref[2, rows, :] = dv.astype(CDT)
            else:
                co, si = cos_ref[rows, :], sin_ref[rows, :]
                dk = dp_ref[1, rows, :].astype(f32) + dk
                dp_ref[1, rows, :] = (_unrotate(dk, co, si) * K_SCALE).astype(CDT)
                dp_ref[2, rows, :] = (dp_ref[2, rows, :].astype(f32) + dv).astype(CDT)

        gb_ref[...] = jnp.zeros_like(gb_ref)
        gf_ref[...] = jnp.zeros_like(gf_ref)
        acc_ref[...] = jnp.zeros_like(acc_ref)
        span = t["dmat"] * jnp.abs(t["d"])
        span_f, span_b = jnp.where(t["lower"], span, 0.0), jnp.where(t["lower"], 0.0, span)

        def ascend(n, final):
            rows, qn, kn, vn, dRn = chunk(n)
            qf, kf = qn.astype(f32), kn.astype(f32)
            sc = _mm_nt(qn, kn)
            dA = _mm_nt(dRn, vn)
            prod = sc * dA
            lgf_part = jnp.sum(prod * span_f, axis=0, keepdims=True)
            lgb_part = jnp.sum(prod * span_b, axis=0, keepdims=True)
            dsc = (dA * t["dmat"]).astype(CDT)
            dq = _mm(dsc, kn)
            scT = (_mm_nt(kn, qn) * t["dmat_t"]).astype(CDT)
            dscT = (_mm_nt(vn, dRn) * t["dmat_t"]).astype(CDT)
            dk = _mm(dscT, qn)
            dv = _mm(scT, dRn)
            sfb = sf_ref[n]
            sbb = sb_ref[n]
            qdf = qf * t["fq"]
            dqdf = _mm_nt(dRn, sfb)
            dq += dqdf * t["fq"]
            lgf_row = jnp.sum(qdf * dqdf * (t["pos"] + 1.0), axis=0, keepdims=True)
            qdb = qf * t["bq"]
            dqdb = _mm_nt(dRn, sbb)
            dq += dqdb * t["bq"]
            lgb_row = jnp.sum(qdb * dqdb * (RC - t["pos"]), axis=0, keepdims=True)
            gb = gb_ref[...]
            gbb = gb.astype(CDT)
            kdb = kf * t["bk"]
            dkdb = _mm_nt(vn, gbb)
            dk += dkdb * t["bk"]
            dv += _mm(kdb.astype(CDT), gbb)
            lgb_row += jnp.sum(kdb * dkdb * t["pos"], axis=0, keepdims=True)
            lgb_row += float(RC) * t["lamb"] * jnp.sum(gb * sbb.astype(f32), axis=0, keepdims=True)
            co, si = cos_ref[rows, :], sin_ref[rows, :]
            dp_ref[0, rows, :] = _unrotate(dq, co, si).astype(CDT)
            emit_kv(rows, dk, dv, final)
            acc_ref[0:1, :] += lgf_row + lgf_part
            acc_ref[1:2, :] += lgb_row + lgb_part
            gb_ref[...] = gb * t["lamb"] + _mm_tn(qdb.astype(CDT), dRn)

        def descend(n, final):
            rows, qn, kn, vn, dRn = chunk(n)
            gf = gf_ref[...]
            gfb = gf.astype(CDT)
            kdf = kn.astype(f32) * t["fk"]
            dkdf = _mm_nt(vn, gfb)
            lgf_row = jnp.sum(kdf * dkdf * (RC - 1.0 - t["pos"]), axis=0, keepdims=True)
            lgf_row += float(RC) * t["lamf"] * jnp.sum(gf * sf_ref[n].astype(f32), axis=0, keepdims=True)
            acc_ref[0:1, :] += lgf_row
            emit_kv(rows, dkdf * t["fk"], _mm(kdf.astype(CDT), gfb), final)
            gf_ref[...] = gf * t["lamf"] + _mm_tn((qn.astype(f32) * t["fq"]).astype(CDT), dRn)

        def sweep(final):
            def step(i, carry):
                ascend(i, final)
                descend(NC - 1 - i, final)
                return carry
            return step

        lax.fori_loop(0, NC // 2, sweep(False), 0, unroll=8)
        lax.fori_loop(NC // 2, NC, sweep(True), 0, unroll=8)
        dlg = jnp.sum(acc_ref[...], axis=1, keepdims=True)
        dlogit = dlg * jax.nn.sigmoid(-dl_ref[:, 0:1])
        lane = lax.broadcasted_iota(jnp.int32, (8, 128), 1)
        dd_ref[...] = jnp.where(lane == pl.program_id(0), jnp.broadcast_to(dlogit, (8, 128)), 0.0)

    def seg(slot):
        return pl.BlockSpec((None, S, DK), lambda h: (slot, 0, h))

    head = pl.BlockSpec((S, DK), lambda h: (0, h))
    states = pl.BlockSpec((None, NC, DK, DK), lambda h: (h, 0, 0, 0))
    return _call(
        body, hooks, [dr, R, sfs, sbs, proj, proj, proj, proj, dproj, cos, sin, dl], name="ret_bwd", grid=(H,),
        out_shape=[jax.ShapeDtypeStruct(dproj.shape, CDT), jax.ShapeDtypeStruct((H, 8, 128), f32)],
        in_specs=[head, head, states, states, seg(SLOT_Q), seg(SLOT_K), seg(SLOT_VR), seg(SLOT_GR), _HBM,
                  _resident((S, DK // 2), lambda h: (0, 0)), _resident((S, DK // 2), lambda h: (0, 0)),
                  pl.BlockSpec((None, 8, DK), lambda h: (h, 0, 0))],
        out_specs=[pl.BlockSpec((4, S, DK), lambda h: (1, 0, h), pipeline_mode=pl.Buffered(1)),
                   pl.BlockSpec((None, 8, 128), lambda h: (h, 0, 0))],
        scratch_shapes=[pltpu.VMEM((S, DK), CDT),
                        pltpu.VMEM((DK, DK), f32), pltpu.VMEM((DK, DK), f32), pltpu.VMEM((8, DK), f32)],
        input_output_aliases={8: 0},
        compiler_params=_params("arbitrary"),
    )


def _proj_bwd_act(dproj, dx2, x1, nrm, wfull, hooks=(), tm=512):
    S = x1.shape[0]

    def body(dp_ref, dx2_ref, x_ref, n_ref, w_ref, dx_ref, dxh_ref, dn_ref, db_ref, acc_ref):
        @pl.when(pl.program_id(0) == 0)
        def _():
            dn_ref[...] = jnp.zeros_like(dn_ref)
            db_ref[...] = jnp.zeros_like(db_ref)

        for p in range(8):
            seg = SEG_OF_SLOT[p]
            dp = dp_ref[p]
            db_ref[seg] += jnp.sum(dp.astype(f32), axis=0, keepdims=True)
            dh = _mm_nt(dp, w_ref[seg * D:(seg + 1) * D, :])
            if p == 0:
                acc_ref[...] = dh
            else:
                acc_ref[...] += dh
        n = n_ref[...]
        r, xh, _ = _rms_fwd(x_ref[...], n)
        dx, dn = _rms_bwd(acc_ref[...], r, xh, n)
        dx = dx2_ref[...] + dx
        dx_ref[...] = dx
        dxh_ref[...] = (0.5 * dx).astype(CDT)
        dn_ref[...] += dn

    tok = pl.BlockSpec((tm, D), lambda i: (i, 0))
    return _call(
        body, hooks, [dproj, dx2, x1, nrm, wfull], name="proj_bwd_act", grid=(S // tm,),
        out_shape=[jax.ShapeDtypeStruct((S, D), f32), jax.ShapeDtypeStruct((S, D), CDT), jax.ShapeDtypeStruct((1, D), f32),
                   jax.ShapeDtypeStruct((8, 1, D), f32)],
        in_specs=[pl.BlockSpec((8, tm, D), lambda i: (0, i, 0)), tok, tok, _resident((1, D), lambda i: (0, 0)),
                  _resident((8 * D, D), lambda i: (0, 0))],
        out_specs=[tok, tok, pl.BlockSpec((1, D), lambda i: (0, 0)), pl.BlockSpec((8, 1, D), lambda i: (0, 0, 0))],
        scratch_shapes=[pltpu.VMEM((tm, D), f32)],
        compiler_params=_params("arbitrary"),
    )


def _rs_sum(name, gfulls, lands, my_c):
    n = len(gfulls)
    rows = gfulls[0].shape[0] // NDEV
    assert all(g.shape[0] == NDEV * rows for g in gfulls)

    def body(c_ref, *refs):
        for g_ref, l_ref, o_ref in zip(refs[:n], refs[n:2 * n], refs[2 * n:]):
            o_ref[...] = (g_ref[...].astype(f32) + l_ref[...].astype(f32)).astype(CDT)

    slot = pl.BlockSpec((None, rows, D), lambda k, c: (k, 0, 0))
    return pl.pallas_call(
        body, name=name,
        grid_spec=pltpu.PrefetchScalarGridSpec(
            num_scalar_prefetch=1, grid=(NCHIP,),
            in_specs=[pl.BlockSpec((rows, D), lambda k, c: (2 * k + c[0], 0))] * n + [slot] * n,
            out_specs=[slot] * n),
        out_shape=[jax.ShapeDtypeStruct((NCHIP, rows, D), CDT)] * n,
        compiler_params=_params("arbitrary"),
    )(my_c, *gfulls, *lands)


def _adamw_math(g, w, m, v):
    m2 = ADAM_B1 * m + (1.0 - ADAM_B1) * g
    v2 = ADAM_B2 * v + (1.0 - ADAM_B2) * (g * g)
    delta = -ADAM_LR * ((m2 / BC1) / (jnp.sqrt(v2 / BC2) + ADAM_EPS) + ADAM_WD * w)
    return delta, m2, v2


def _adamw_big(name, landed, w, m, v, after):
    n = len(w)
    rows = w[0].shape[0]
    tr = 256 if rows % 256 == 0 else (rows // 2 if rows > 256 else rows)
    nt = rows // tr

    def body(*refs):
        ins, outs = refs[:4 * n], refs[4 * n + 1:]
        for j in range(n):
            @pl.when(pl.program_id(0) == j)
            def _(j=j):
                l_ref, w_ref, m_ref, v_ref = ins[j], ins[n + j], ins[2 * n + j], ins[3 * n + j]
                g = l_ref[0].astype(f32)
                for k in range(1, NCHIP):
                    g = g + l_ref[k].astype(f32)
                outs[4 * j][...] = g
                outs[4 * j + 1][...], outs[4 * j + 2][...], outs[4 * j + 3][...] = _adamw_math(g, w_ref[...], m_ref[...], v_ref[...])

    def tile(j):
        return lambda jj, i: jnp.clip((jj - j) * nt + i, 0, nt - 1)

    blk = [pl.BlockSpec((tr, D), lambda jj, i, t=tile(j): (t(jj, i), 0)) for j in range(n)]
    lnd = [pl.BlockSpec((NCHIP, tr, D), lambda jj, i, t=tile(j): (0, t(jj, i), 0)) for j in range(n)]
    o = jax.ShapeDtypeStruct((rows, D), f32)
    res = pl.pallas_call(
        body, name=name, grid=(n, nt), out_shape=[o] * (4 * n),
        in_specs=lnd + blk + blk + blk + [_HBM],
        out_specs=[blk[j] for j in range(n) for _ in range(4)],
        compiler_params=_params("arbitrary", "arbitrary"),
    )(*landed, *w, *m, *v, after)
    return [res[4 * j:4 * j + 4] for j in range(n)]


ROW_FFN1_NORM, ROW_MIX_NORM, ROW_SGU_G, ROW_SGU_B, ROW_FFN2_NORM, ROW_FINAL_NORM, ROW_B_IN = 0, 1, 2, 3, 4, 5, 8
ROW_WS, ROW_BS = 0, G * C


def _adamw_small(ga, gs, gd, gn1, gl, params):
    def body(ga_ref, gs_ref, gd_ref, gn1_ref, gl_ref, *refs):
        ins, outs = refs[:30], refs[30:]

        def total(ref, r0, n):
            g = ref[0, r0:r0 + n, :]
            for j in range(1, NDEV):
                g = g + ref[j, r0:r0 + n, :]
            return g

        def apply(i, g, rows=slice(None), cols=slice(None)):
            w, m, v = ins[3 * i][rows, cols], ins[3 * i + 1][rows, cols], ins[3 * i + 2][rows, cols]
            outs[4 * i][rows, cols] = g
            outs[4 * i + 1][rows, cols], outs[4 * i + 2][rows, cols], outs[4 * i + 3][rows, cols] = _adamw_math(g, w, m, v)

        outs[40][...] = total(gl_ref, 0, 8)[0:1, 0:1]
        apply(0, total(gn1_ref, 0, 1))
        for i, r in enumerate((ROW_FFN1_NORM, ROW_MIX_NORM, ROW_SGU_G, ROW_SGU_B, ROW_FFN2_NORM, ROW_FINAL_NORM)):
            if i:
                apply(i, total(ga_ref, r, 1))
        for k in range(8):
            apply(6, total(ga_ref, ROW_B_IN + k, 1), cols=slice(k * D, (k + 1) * D))
        apply(7, total(gs_ref, ROW_WS, G * C))
        for gi in range(G):
            apply(8, total(gs_ref, ROW_BS + 8 * gi, 1), slice(gi, gi + 1))
        dec = total(gd_ref, 0, 8)
        for hh in range(1, H):
            dec = dec + total(gd_ref, 8 * hh, 8)
        apply(9, dec[0:2, 0:H])

    flat = [a for p in params for a in p]
    out_shape = [jax.ShapeDtypeStruct(p[0].shape, f32) for p in params for _ in range(4)]
    out_shape.append(jax.ShapeDtypeStruct((1, 1), f32))
    vm = pl.BlockSpec(memory_space=pltpu.VMEM)
    return pl.pallas_call(
        body, name="adamw_small", out_shape=out_shape,
        in_specs=[vm] * (5 + len(flat)), out_specs=[vm] * len(out_shape),
        compiler_params=pltpu.CompilerParams(vmem_limit_bytes=VMEM_LIMIT),
    )(ga, gs, gd, gn1, gl, *flat)


def kernel(x, ffn1_norm, ffn1_w_gate, ffn1_w_up, ffn1_w_down, mix_norm, w_in, b_in, sgu_norm_g, sgu_norm_b, sgu_w_s, sgu_b_s, ret_decay_logit, w_branch_a, w_branch_b, w_out, ffn2_norm, ffn2_w_gate, ffn2_w_up, ffn2_w_down, final_norm, loss_target, m_ffn1_norm, m_ffn1_w_gate, m_ffn1_w_up, m_ffn1_w_down, m_mix_norm, m_w_in, m_b_in, m_sgu_norm_g, m_sgu_norm_b, m_sgu_w_s, m_sgu_b_s, m_ret_decay_logit, m_w_branch_a, m_w_branch_b, m_w_out, m_ffn2_norm, m_ffn2_w_gate, m_ffn2_w_up, m_ffn2_w_down, m_final_norm, v_ffn1_norm, v_ffn1_w_gate, v_ffn1_w_up, v_ffn1_w_down, v_mix_norm, v_w_in, v_b_in, v_sgu_norm_g, v_sgu_norm_b, v_sgu_w_s, v_sgu_b_s, v_ret_decay_logit, v_w_branch_a, v_w_branch_b, v_w_out, v_ffn2_norm, v_ffn2_w_gate, v_ffn2_w_up, v_ffn2_w_down, v_final_norm):
    args = dict(locals())
    S = x.shape[1]
    xs = x[0]
    target = loss_target[0]

    def buf_layout(name, a):
        a = a[0]
        return a.T if name in W_TRANSPOSED else a

    sh = {n: buf_layout(n, args[n]).astype(CDT) for n in W_NAMES}
    wf = {}

    b3 = b_in.reshape(8, 1, D)
    ws = sgu_w_s[0].astype(CDT)
    wst = jnp.swapaxes(sgu_w_s[0], 1, 2).astype(CDT)
    bsc = sgu_b_s[0].reshape(G, C, 1)
    dl = jnp.zeros((H, 8, DK), f32).at[:, 0:2, :].set(jnp.broadcast_to(ret_decay_logit[0].T[:, :, None], (H, 2, DK)))
    theta = ROPE_BASE ** (-jnp.arange(0, DK, 2, dtype=f32) / DK)
    ang = jnp.arange(S, dtype=f32)[:, None] * theta[None, :]
    cos, sin = jnp.cos(ang), jnp.sin(ang)
    fnorm = final_norm.reshape(1, D)

    f1 = ("ffn1_w_gate", "ffn1_w_up", "ffn1_w_down")
    f2 = ("ffn2_w_gate", "ffn2_w_up", "ffn2_w_down")
    br = ("w_branch_a", "w_branch_b", "w_out")
    for cid, names in enumerate((f1, ("w_in",), br, f2)):
        wf.update(zip(names, _sequence("ag_" + names[0], 1 + cid, NEIGHBOURS, [_ag_hook(sh[n]) for n in names])))
    x1, g1, u1, a1, hf1 = _ffn_fwd("ffn1_fwd", xs, ffn1_norm, *[wf[n] for n in f1])
    proj, h2 = _proj_fwd(x1, mix_norm, wf["w_in"], b3, cos, sin)
    a = _sgu_fwd(proj, sgu_norm_g, sgu_norm_b, ws, bsc)
    R, r, sfs, sbs = _ret_fwd(proj, dl)
    x2, ya, yb = _merge_fwd(a, r, proj, x1, *[wf[n] for n in br])
    x3, g2, u2, a2, hf2 = _ffn_fwd("ffn2_fwd", x2, ffn2_norm, *[wf[n] for n in f2])
    dx3, dyh2, d_final, loss_part = _loss_head(x3, fnorm, target)

    my_c = lax.axis_index("c").astype(jnp.int32).reshape(1)
    gw, landed, sequenced = {}, {}, []

    def d2d(*names):
        return [_rs_d2d_hook(gw[n]) for n in names]

    def behind(x, token):
        return lax.optimization_barrier((x, token))[0]

    def to_chips(names, sibs, more=()):
        parts = list(_rs_sum("rs_sum_" + names[0], [gw[n] for n in names], list(sibs), my_c))
        token = parts[0]
        if sequenced:
            parts[0] = behind(parts[0], sequenced[-1])
        hooks = [_rs_ici_hook(p) for p in parts] + ([_small_hook(list(more))] if more else [])
        reach = EVERYONE if more else CHIPS
        got = _sequence("rs_chips_" + names[0], REACH_ID[reach], reach, hooks)
        sequenced.append(got[0])
        landed.update(zip(names, got))
        return got[len(names):], token

    def ffn_bwd(tag, names, dy, dyh, x, g, u, a, h, nrm, each_alone, more=()):
        wg, wu, wd = names
        (gw[wd],) = _wgrad(tag + "_wd_grad", a, dyh)
        dg, du, sib_d = _ffn_bwd_hidden(tag + "_bwd_hidden", dyh, g, u, wf[wd], d2d(wd))
        if each_alone:
            dg = behind(dg, to_chips([wd], [sib_d])[1])
        (gw[wg],) = _wgrad(tag + "_wg_grad", dg, h)
        gw[wu], sib_g = _wgrad(tag + "_wu_grad", du, h, d2d(wg))
        if each_alone:
            (sib_u,) = _sequence("rs_sib_" + wu, SIBLING_ID, SIBLING, [_rs_d2d_hook(behind(gw[wu], sequenced[-1]))])
            sequenced.append(sib_u)
            dy = behind(dy, to_chips([wg], [sib_g])[1])
            dx, dn = _ffn_bwd_in(tag + "_bwd_in", dy, x, dg, du, nrm, wf[wg], wf[wu])
            return (dx, dn) + to_chips([wu], [sib_u], more(dn))
        dx, dn, sib_u = _ffn_bwd_in(tag + "_bwd_in", dy, x, dg, du, nrm, wf[wg], wf[wu], d2d(wu))
        return (dx, dn) + to_chips([wd, wg, wu], [sib_d, sib_g, sib_u])

    dx2, d_ffn2n, _, token = ffn_bwd("ffn2", f2, dx3, dyh2, x2, g2, u2, a2, hf2, ffn2_norm, False)
    dproj, da, dr, mix, dx2b, dya, dyb = _merge_bwd_act(behind(dx2, token), ya, yb, proj, *[wf[n] for n in br])
    dproj, d_ws, d_bs, d_gn, d_bn = _sgu_bwd(da, proj, dproj, sgu_norm_g, sgu_norm_b, ws, wst, bsc)
    dproj, d_dec = _ret_bwd(dr, R, sfs, sbs, proj, dproj, cos, sin, dl)
    (gw["w_in"],) = _tn("win_grad", h2, dproj, _seg_of_slot)
    gw["w_out"], sib_win = _wgrad("wo_grad", mix, dx2b, d2d("w_in"))
    small_sgu = jnp.concatenate([d_ws.reshape(G * C, C), d_bs.reshape(G * 8, C)], axis=0)
    (g_sgu, gl), token = to_chips(["w_in"], [sib_win], [small_sgu, loss_part])
    (gw["w_branch_a"],) = _wgrad("wa_grad", a, behind(dya, token))
    (gw["w_branch_b"],) = _wgrad("wb_grad", r, dyb)
    dx1, dyh1, d_mixn, d_bin, *sib_br = _proj_bwd_act(behind(dproj, token), dx2, x1, mix_norm, wf["w_in"], d2d(*br))
    small_a = jnp.concatenate([jnp.zeros((1, D), f32), d_mixn, d_gn, d_bn, d_ffn2n, d_final, jnp.zeros((2, D), f32),
                               d_bin.reshape(8, D)], axis=0)
    (ga, g_dec), token = to_chips(list(br), sib_br, [small_a, d_dec.reshape(H * 8, 128)])
    dxs, d_ffn1n, (gn1,), token = ffn_bwd("ffn1", f1, dx1, behind(dyh1, token), xs, g1, u1, a1, hf1, ffn1_norm, True,
                                          lambda dn: [dn])

    out = {"grad_x": dxs[None]}

    def native(name, a):
        a = a.T if name in W_TRANSPOSED else a
        return a[None]

    after = token
    for names in (f2, ("w_in",), br, (f1[2], f1[0], f1[1])):
        res = _adamw_big("adamw_" + names[0], [landed[n] for n in names], [buf_layout(n, args[n]) for n in names],
                         [buf_layout(n, args["m_" + n]) for n in names], [buf_layout(n, args["v_" + n]) for n in names], after)
        after = res[-1][0]
        for n, four in zip(names, res):
            for pre, val in zip(("grad_", "delta_", "new_m_", "new_v_"), four):
                out[pre + n] = native(n, val)

    small = [
        ("ffn1_norm", lambda a: a, lambda a: a), ("mix_norm", lambda a: a, lambda a: a),
        ("sgu_norm_g", lambda a: a, lambda a: a), ("sgu_norm_b", lambda a: a, lambda a: a),
        ("ffn2_norm", lambda a: a, lambda a: a),
        ("final_norm", lambda a: a.reshape(1, D), lambda a: a.reshape(D)),
        ("b_in", lambda a: a, lambda a: a),
        ("sgu_w_s", lambda a: a.reshape(G * C, C), lambda a: a.reshape(1, G, C, C)),
        ("sgu_b_s", lambda a: a[0], lambda a: a[None]),
        ("ret_decay_logit", lambda a: a[0], lambda a: a[None]),
    ]
    res = _adamw_small(ga, g_sgu, g_dec, gn1, gl,
                       [(to(args[n]), to(args["m_" + n]), to(args["v_" + n])) for n, to, _ in small])
    out["loss"] = res[40].reshape(())
    for i, (n, _, back) in enumerate(small):
        for j, pre in enumerate(("grad_", "delta_", "new_m_", "new_v_")):
            out[pre + n] = back(res[4 * i + j])

    weights = ("ffn1_norm", "ffn1_w_gate", "ffn1_w_up", "ffn1_w_down", "mix_norm", "w_in", "b_in", "sgu_norm_g",
               "sgu_norm_b", "sgu_w_s", "sgu_b_s", "ret_decay_logit", "w_branch_a", "w_branch_b", "w_out", "ffn2_norm",
               "ffn2_w_gate", "ffn2_w_up", "ffn2_w_down", "final_norm")
    return (out["loss"], out["grad_x"], *[out["grad_" + n] for n in weights], *[out["delta_" + n] for n in weights],
            *[out["new_m_" + n] for n in weights], *[out["new_v_" + n] for n in weights])
```

```python
import math

import jax
import jax.numpy as jnp
from jax import lax
from jax.experimental import pallas as pl
from jax.experimental.pallas import tpu as pltpu
from jax.experimental.pallas import tpu_sc as plsc

f32 = jnp.float32
CDT = jnp.bfloat16

D = 1024
F = 2816
C = 128
RC = 256
H = 4
DK = 256
G = 4
NDEV = 8
NCHIP = 4
EPS = 1e-6
ROPE_BASE = 10000.0
FT = 256
V7X_VMEM_BYTES = 64 * 1024 * 1024
VMEM_LIMIT = V7X_VMEM_BYTES - 8 * 1024 * 1024

ADAM_LR, ADAM_B1, ADAM_B2, ADAM_EPS, ADAM_WD, ADAM_STEP = 0.001, 0.9, 0.999, 1e-08, 0.01, 10
BC1 = 1.0 - ADAM_B1 ** ADAM_STEP
BC2 = 1.0 - ADAM_B2 ** ADAM_STEP

W_ROWS = dict(ffn1_w_gate=352, ffn1_w_up=352, ffn1_w_down=352, w_in=1024, w_branch_a=128, w_branch_b=128, w_out=128,
              ffn2_w_gate=352, ffn2_w_up=352, ffn2_w_down=352)
W_NAMES = tuple(W_ROWS)
W_TRANSPOSED = ("ffn1_w_gate", "ffn1_w_up", "ffn2_w_gate", "ffn2_w_up")

SLOT_U, SLOT_V, SLOT_GA, SLOT_GB, SLOT_Q, SLOT_K, SLOT_VR, SLOT_GR = range(8)


SEG_OF_SLOT = (0, 1, 6, 7, 2, 3, 4, 5)


def _seg_of_slot(p):
    return jnp.where(p < 2, p, jnp.where(p < 4, p + 4, p - 2))


def _mm(a, b):
    return jnp.dot(a, b, preferred_element_type=f32)


def _mm_nt(a, b):
    return lax.dot_general(a, b, (((1,), (1,)), ((), ())), preferred_element_type=f32)


def _mm_tn(a, b):
    return lax.dot_general(a, b, (((0,), (0,)), ((), ())), preferred_element_type=f32)


def _params(*sem):
    return pltpu.CompilerParams(dimension_semantics=sem, vmem_limit_bytes=VMEM_LIMIT)


def _resident(shape, index_map):
    return pl.BlockSpec(shape, index_map, pipeline_mode=pl.Buffered(1))


def _gelu(x):
    return 0.5 * x * (1.0 + lax.erf(x * (1.0 / math.sqrt(2.0))))


def _gelu_and_grad(x):
    cdf = 0.5 * (1.0 + lax.erf(x * (1.0 / math.sqrt(2.0))))
    return x * cdf, cdf + x * jnp.exp(-0.5 * x * x) * (1.0 / math.sqrt(2.0 * math.pi))


def _rms_fwd(x, n):
    r = lax.rsqrt(jnp.mean(x * x, axis=-1, keepdims=True) + EPS)
    xh = x * r
    return r, xh, xh * n


def _rms_bwd(dh, r, xh, n):
    dxh = dh * n
    dx = r * (dxh - xh * jnp.mean(dxh * xh, axis=-1, keepdims=True))
    return dx, jnp.sum(dh * xh, axis=0, keepdims=True)


MESH_ID = pl.DeviceIdType.MESH
_HBM = pl.BlockSpec(memory_space=pltpu.HBM)


def _my_place():
    return lax.axis_index("x"), lax.axis_index("y"), lax.axis_index("c")


def _ici_peers(x, y, c):
    return [((1 - x, y, c), 2 * (1 - x) + y), ((x, 1 - y, c), 2 * x + 1 - y), ((1 - x, 1 - y, c), 2 * (1 - x) + 1 - y)]


class _Hook:
    def __init__(self, operands, out_shapes, n_remote, n_local, start, finish, relay=None):
        self.operands, self.out_shapes = list(operands), list(out_shapes)
        self.n_remote, self.n_local, self.start, self.finish = n_remote, n_local, start, finish
        self.relay = relay or (lambda *a: None)


def _call(body, hooks, operands, *, in_specs, out_specs, out_shape, grid=None, scratch_shapes=(), **kw):
    hooks = tuple(hooks)
    n_in, n_out, n_scr = len(in_specs), len(out_shape), len(scratch_shapes)
    h_ops = [a for h in hooks for a in h.operands]
    h_outs = [s for h in hooks for s in h.out_shapes]
    h_sems = [pltpu.SemaphoreType.DMA((n,)) for h in hooks for n in (h.n_remote, h.n_remote, max(h.n_local, 1))]

    def wrapped(*refs):
        ins, hin = refs[:n_in], refs[n_in:n_in + len(h_ops)]
        o0 = n_in + len(h_ops)
        outs, hout = refs[o0:o0 + n_out], refs[o0 + n_out:o0 + n_out + len(h_outs)]
        s0 = o0 + n_out + len(h_outs)
        scr, hsem = refs[s0:s0 + n_scr], refs[s0 + n_scr:]

        def run(phase):
            ip = op = 0
            for i, h in enumerate(hooks):
                ssem, rsem, lsem = hsem[3 * i:3 * i + 3]

                def rcopy(k, src, dst, dev, ssem=ssem, rsem=rsem):
                    return pltpu.make_async_remote_copy(src_ref=src, dst_ref=dst, send_sem=ssem.at[k], recv_sem=rsem.at[k],
                                                        device_id=dev, device_id_type=MESH_ID)

                def lcopy(k, src, dst, lsem=lsem):
                    return pltpu.make_async_copy(src, dst, lsem.at[k])

                getattr(h, phase)(hin[ip:ip + len(h.operands)], hout[op:op + len(h.out_shapes)], rcopy, lcopy)
                ip += len(h.operands)
                op += len(h.out_shapes)

        def at_edge(phase, last):
            if not hooks:
                return
            if grid is None:
                run(phase)
                return
            cond = None
            for ax, n in enumerate(grid):
                here = pl.program_id(ax) == (n - 1 if last else 0)
                cond = here if cond is None else cond & here
            pl.when(cond)(lambda: run(phase))

        at_edge("start", False)
        at_edge("relay", True)
        body(*ins, *outs, *scr)
        at_edge("finish", True)

    if grid is not None:
        kw["grid"] = grid
    return list(pl.pallas_call(
        wrapped, out_shape=list(out_shape) + h_outs, in_specs=list(in_specs) + [_HBM] * len(h_ops),
        out_specs=list(out_specs) + [_HBM] * len(h_outs), scratch_shapes=list(scratch_shapes) + h_sems, **kw,
    )(*operands, *h_ops))


def _rows(ref, start, n):
    return ref.at[pl.ds(start, n), :]


def _ag_hook(shard):
    rows = shard.shape[0]
    half = rows // 2
    assert half % 16 == 0

    def place():
        x, y, c = _my_place()
        devs = dict(sib=(x, y, 1 - c), xn=(1 - x, y, c), yn=(x, 1 - y, c))
        chips = dict(me=2 * x + y, xn=2 * (1 - x) + y, yn=2 * x + 1 - y, dg=2 * (1 - x) + 1 - y)
        return c, devs, chips

    def block(full, chip, c):
        return _rows(full, (2 * chip + c) * rows, rows)

    def halfblock(full, chip, c, upper):
        return _rows(full, (2 * chip + c) * rows + upper * half, half)

    def start(ins, outs, rcopy, lcopy):
        c, devs, chips = place()
        src, dst = ins[0], block(outs[0], chips["me"], c)
        lcopy(0, src, dst).start()
        for k, to in enumerate(("sib", "xn", "yn")):
            rcopy(k, src, dst, devs[to]).start()

    def relay(ins, outs, rcopy, lcopy):
        c, devs, chips = place()
        full = outs[0]
        blk = block(full, chips["xn"], c)
        rcopy(1, blk, blk, devs["xn"]).wait_recv()
        low = halfblock(full, chips["xn"], c, 0)
        rcopy(3, low, low, devs["yn"]).start()
        rcopy(5, blk, blk, devs["sib"]).start()
        blk = block(full, chips["yn"], c)
        rcopy(2, blk, blk, devs["yn"]).wait_recv()
        up = halfblock(full, chips["yn"], c, 1)
        rcopy(4, up, up, devs["xn"]).start()
        rcopy(6, blk, blk, devs["sib"]).start()
        low, up = halfblock(full, chips["dg"], c, 0), halfblock(full, chips["dg"], c, 1)
        rcopy(3, low, low, devs["yn"]).wait_recv()
        rcopy(4, up, up, devs["xn"]).wait_recv()
        blk = block(full, chips["dg"], c)
        rcopy(7, blk, blk, devs["sib"]).start()

    def finish(ins, outs, rcopy, lcopy):
        c, devs, chips = place()
        full, sib = outs[0], devs["sib"]
        for k, chip in ((0, "me"), (5, "xn"), (6, "yn"), (7, "dg")):
            theirs = block(full, chips[chip], 1 - c)
            rcopy(k, theirs, theirs, sib).wait_recv()
            mine = block(full, chips[chip], c)
            if k:
                rcopy(k, mine, mine, sib).wait_send()
        src, dst = ins[0], block(full, chips["me"], c)
        lcopy(0, src, dst).wait()
        for k, to in enumerate(("sib", "xn", "yn")):
            rcopy(k, src, dst, devs[to]).wait_send()
        low, up = halfblock(full, chips["xn"], c, 0), halfblock(full, chips["yn"], c, 1)
        rcopy(3, low, low, devs["yn"]).wait_send()
        rcopy(4, up, up, devs["xn"]).wait_send()

    return _Hook([shard], [jax.ShapeDtypeStruct((NDEV * rows, D), shard.dtype)], 8, 1, start, finish, relay)


SIBLING, CHIPS, NEIGHBOURS, EVERYONE = "sibling", "chips", "sibling and the two neighbour chips", "everyone"
REACH_ID = {CHIPS: 6, EVERYONE: 7}
SIBLING_ID = 5


def _sequence(name, collective_id, reach, hooks):
    ins = [[jax.new_ref(a, memory_space=pltpu.MemorySpace.HBM) for a in h.operands] for h in hooks]
    outs = [[jax.empty_ref(s, memory_space=pltpu.MemorySpace.HBM) for s in h.out_shapes] for h in hooks]
    sems = tuple(pltpu.SemaphoreType.DMA((n,)) for h in hooks for n in (h.n_remote, h.n_remote, max(h.n_local, 1)))

    @pl.kernel(mesh=plsc.ScalarSubcoreMesh(axis_name="sequencer", num_cores=1), name=name, scratch_types=sems,
               compiler_params=pltpu.CompilerParams(collective_id=collective_id))
    def launch(*sem_refs):
        x, y, c = _my_place()
        chips = [dev for dev, _ in _ici_peers(x, y, c)]
        others = [(1 - x if dx else x, 1 - y if dy else y, 1 - c if dc else c)
                  for dx in range(2) for dy in range(2) for dc in range(2) if dx + dy + dc]
        devs = {SIBLING: [(x, y, 1 - c)], CHIPS: chips, NEIGHBOURS: [(x, y, 1 - c), (1 - x, y, c), (x, 1 - y, c)],
                EVERYONE: others}[reach]
        barrier = pltpu.get_barrier_semaphore()
        for dev in devs:
            pl.semaphore_signal(barrier, inc=1, device_id=dev, device_id_type=MESH_ID)
        pl.semaphore_wait(barrier, len(devs))
        for phase in ("start", "relay", "finish"):
            for i, h in enumerate(hooks):
                ssem, rsem, lsem = sem_refs[3 * i:3 * i + 3]

                def rcopy(k, src, dst, dev, ssem=ssem, rsem=rsem):
                    return pltpu.make_async_remote_copy(src_ref=src, dst_ref=dst, send_sem=ssem.at[k], recv_sem=rsem.at[k],
                                                        device_id=dev, device_id_type=MESH_ID)

                def lcopy(k, src, dst, lsem=lsem):
                    return pltpu.make_async_copy(src, dst, lsem.at[k])

                getattr(h, phase)(ins[i], outs[i], rcopy, lcopy)

    launch()
    return [o[...] for os in outs for o in os]


def _rs_d2d_hook(gfull):
    rows = gfull.shape[0] // NDEV

    def pairs(g, land):
        x, y, c = _my_place()
        return (x, y, 1 - c), [(k, _rows(g, (2 * k + 1 - c) * rows, rows), land.at[k]) for k in range(NCHIP)]

    def start(ins, outs, rcopy, lcopy):
        sib, cps = pairs(ins[0], outs[0])
        for i, src, dst in cps:
            rcopy(i, src, dst, sib).start()

    def finish(ins, outs, rcopy, lcopy):
        sib, cps = pairs(ins[0], outs[0])
        for i, src, dst in cps:
            rcopy(i, dst, dst, sib).wait_recv()
        for i, src, dst in cps:
            rcopy(i, src, dst, sib).wait_send()

    return _Hook([gfull], [jax.ShapeDtypeStruct((NCHIP, rows, D), gfull.dtype)], NCHIP, 0, start, finish)


def _rs_ici_hook(part):
    def start(ins, outs, rcopy, lcopy):
        x, y, c = _my_place()
        mychip = 2 * x + y
        lcopy(0, ins[0].at[mychip], outs[0].at[mychip]).start()
        for j, (dev, chip) in enumerate(_ici_peers(x, y, c)):
            rcopy(j, ins[0].at[chip], outs[0].at[mychip], dev).start()

    def finish(ins, outs, rcopy, lcopy):
        x, y, c = _my_place()
        mychip = 2 * x + y
        peers = _ici_peers(x, y, c)
        for j, (dev, chip) in enumerate(peers):
            rcopy(j, outs[0].at[chip], outs[0].at[chip], dev).wait_recv()
        for j, (dev, chip) in enumerate(peers):
            rcopy(j, ins[0].at[chip], outs[0].at[mychip], dev).wait_send()
        lcopy(0, ins[0].at[mychip], outs[0].at[mychip]).wait()

    return _Hook([part], [jax.ShapeDtypeStruct(part.shape, part.dtype)], 3, 1, start, finish)


def _small_hook(arrays):
    n = len(arrays)

    def peers():
        x, y, c = _my_place()
        out = []
        for dx in range(2):
            for dy in range(2):
                for dc in range(2):
                    if dx + dy + dc:
                        px, py, pc = (1 - x if dx else x), (1 - y if dy else y), (1 - c if dc else c)
                        out.append(((px, py, pc), 4 * px + 2 * py + pc))
        return 4 * x + 2 * y + c, out

    def start(ins, outs, rcopy, lcopy):
        me, ps = peers()
        for t in range(n):
            lcopy(t, ins[t], outs[t].at[me]).start()
            for i, (dev, _) in enumerate(ps):
                rcopy(n * i + t, ins[t], outs[t].at[me], dev).start()

    def finish(ins, outs, rcopy, lcopy):
        me, ps = peers()
        for t in range(n):
            for i, (dev, peer) in enumerate(ps):
                rcopy(n * i + t, outs[t].at[peer], outs[t].at[peer], dev).wait_recv()
            for i, (dev, _) in enumerate(ps):
                rcopy(n * i + t, ins[t], outs[t].at[me], dev).wait_send()
            lcopy(t, ins[t], outs[t].at[me]).wait()

    return _Hook(arrays, [jax.ShapeDtypeStruct((NDEV,) + a.shape, a.dtype) for a in arrays], 7 * n, n, start, finish)


def _wblock(w):
    return _resident(w.shape, lambda *_: (0, 0))


def _ffn_fwd(name, x, nrm, wg, wu, wd, hooks=(), tm=512):
    S = x.shape[0]

    def body(x_ref, n_ref, wg_ref, wu_ref, wd_ref, y_ref, g_ref, u_ref, a_ref, h_ref, acc_ref):
        xv = x_ref[...]
        _, _, h = _rms_fwd(xv, n_ref[...])
        h = h.astype(CDT)
        h_ref[...] = h
        for ci in range(F // FT):
            sl = slice(ci * FT, (ci + 1) * FT)
            g = _mm_nt(h, wg_ref[sl, :])
            u = _mm_nt(h, wu_ref[sl, :])
            g_ref[:, sl] = g.astype(CDT)
            u_ref[:, sl] = u.astype(CDT)
            a = (g * jax.nn.sigmoid(g) * u).astype(CDT)
            a_ref[:, sl] = a
            o = _mm(a, wd_ref[sl, :])
            if ci == 0:
                acc_ref[...] = o
            else:
                acc_ref[...] += o
        y_ref[...] = xv + 0.5 * acc_ref[...]

    tok = pl.BlockSpec((tm, D), lambda i: (i, 0))
    hid = pl.BlockSpec((tm, F), lambda i: (i, 0))
    hidden = jax.ShapeDtypeStruct((S, F), CDT)
    return _call(
        body, hooks, [x, nrm, wg, wu, wd], name=name, grid=(S // tm,),
        out_shape=[jax.ShapeDtypeStruct((S, D), f32), hidden, hidden, hidden, jax.ShapeDtypeStruct((S, D), CDT)],
        in_specs=[tok, _resident((1, D), lambda i: (0, 0)), _wblock(wg), _wblock(wu), _wblock(wd)],
        out_specs=[tok, hid, hid, hid, tok],
        scratch_shapes=[pltpu.VMEM((tm, D), f32)],
        compiler_params=_params("arbitrary"),
    )


def _proj_fwd(x1, nrm, wfull, b3, cos, sin, hooks=(), tm=512):
    S = x1.shape[0]

    def body(x_ref, n_ref, w_ref, b_ref, cos_ref, sin_ref, p_ref, h_ref):
        _, _, h = _rms_fwd(x_ref[...], n_ref[...])
        h = h.astype(CDT)
        h_ref[...] = h
        for p in range(8):
            seg = SEG_OF_SLOT[p]
            z = _mm(h, w_ref[seg * D:(seg + 1) * D, :]) + b_ref[seg]
            if p in (SLOT_Q, SLOT_K):
                co, si = cos_ref[...], sin_ref[...]
                for hh in range(H):
                    cs = slice(hh * DK, (hh + 1) * DK)
                    zr = _rotate(z[:, cs], co, si)
                    p_ref[p, :, cs] = (zr * K_SCALE if p == SLOT_K else zr).astype(CDT)
            else:
                p_ref[p] = z.astype(CDT)

    tab = pl.BlockSpec((tm, DK // 2), lambda i: (i, 0))
    return _call(
        body, hooks, [x1, nrm, wfull, b3, cos, sin], name="proj_fwd", grid=(S // tm,),
        out_shape=[jax.ShapeDtypeStruct((8, S, D), CDT), jax.ShapeDtypeStruct((S, D), CDT)],
        in_specs=[pl.BlockSpec((tm, D), lambda i: (i, 0)), _resident((1, D), lambda i: (0, 0)),
                  _resident((8 * D, D), lambda i: (0, 0)), _resident((8, 1, D), lambda i: (0, 0, 0)), tab, tab],
        out_specs=[pl.BlockSpec((8, tm, D), lambda i: (0, i, 0)), pl.BlockSpec((tm, D), lambda i: (i, 0))],
        compiler_params=_params("arbitrary"),
    )


def _sgu_norm(va, gn, bn):
    mu = jnp.mean(va, axis=-1, keepdims=True)
    xc = va - mu
    rstd = lax.rsqrt(jnp.mean(xc * xc, axis=-1, keepdims=True) + EPS)
    vhat = xc * rstd
    return rstd, vhat, vhat * gn + bn


def _sgu_fwd(proj, gn, bn, ws, bsc, tm=512):
    S = proj.shape[1]
    GW = D // G

    def body(p_ref, gn_ref, bn_ref, ws_ref, bs_ref, a_ref):
        ua = _gelu(p_ref[0].astype(f32))
        va = _gelu(p_ref[1].astype(f32))
        _, _, vn = _sgu_norm(va, gn_ref[...], bn_ref[...])
        vn = vn.astype(CDT)
        for ch in range(tm // C):
            rs = slice(ch * C, (ch + 1) * C)
            for gi in range(G):
                cs = slice(gi * GW, (gi + 1) * GW)
                s = _mm(ws_ref[gi], vn[rs, cs]) + bs_ref[gi]
                a_ref[rs, cs] = (ua[rs, cs] * s).astype(CDT)

    return pl.pallas_call(
        body, name="sgu_fwd", grid=(S // tm,),
        out_shape=jax.ShapeDtypeStruct((S, D), CDT),
        in_specs=[pl.BlockSpec((2, tm, D), lambda i: (0, i, 0)), _resident((1, D), lambda i: (0, 0)),
                  _resident((1, D), lambda i: (0, 0)), _resident((G, C, C), lambda i: (0, 0, 0)),
                  _resident((G, C, 1), lambda i: (0, 0, 0))],
        out_specs=pl.BlockSpec((tm, D), lambda i: (i, 0)),
        compiler_params=_params("arbitrary"),
    )(proj, gn, bn, ws, bsc)


def _decay_tables(dl_ref):
    lg = jax.nn.log_sigmoid(dl_ref[0:2, :])
    lgf, lgb = lg[0:1, :], lg[1:2, :]
    assert RC <= DK
    ri = lax.broadcasted_iota(jnp.int32, (RC, RC), 0)
    ci = lax.broadcasted_iota(jnp.int32, (RC, RC), 1)
    d = (ri - ci).astype(f32)
    lower = d >= 0
    dmat = jnp.where(lower, jnp.exp(d * lgf[:, :RC]), jnp.exp(-d * lgb[:, :RC]))
    dmat_t = jnp.where(d <= 0, jnp.exp(-d * lgf[:, :RC]), jnp.exp(d * lgb[:, :RC]))
    pos = lax.broadcasted_iota(jnp.int32, (RC, DK), 0).astype(f32)
    t = dict(
        lgf=lgf, lgb=lgb, d=d, lower=lower, dmat=dmat, dmat_t=dmat_t, pos=pos,
        fq=jnp.exp((pos + 1.0) * lgf), fk=jnp.exp((RC - 1.0 - pos) * lgf),
        bq=jnp.exp((RC - pos) * lgb), bk=jnp.exp(pos * lgb),
        lamf=jnp.exp(float(RC) * lgf), lamb=jnp.exp(float(RC) * lgb),
    )
    return t


def _rotate(t, co, si):
    t1, t2 = t[:, :DK // 2], t[:, DK // 2:]
    return jnp.concatenate([t1 * co - t2 * si, t2 * co + t1 * si], axis=-1)


def _unrotate(t, co, si):
    t1, t2 = t[:, :DK // 2], t[:, DK // 2:]
    return jnp.concatenate([t1 * co + t2 * si, t2 * co - t1 * si], axis=-1)


K_SCALE = DK ** -0.5
ROW_TILE = 256


def _ret_fwd(proj, dl, hooks=()):
    S = proj.shape[1]
    NC = S // RC

    def body(q_ref, k_ref, v_ref, g_ref, dl_ref, R_ref, r_ref, sfs_ref, sbs_ref, rb_ref, sf_ref, sb_ref):
        t = _decay_tables(dl_ref)

        def chunk(n):
            rows = pl.ds(pl.multiple_of(n * RC, RC), RC)
            return rows, q_ref[rows, :], k_ref[rows, :], v_ref[rows, :]

        sf_ref[...] = jnp.zeros_like(sf_ref)
        sb_ref[...] = jnp.zeros_like(sb_ref)

        def step(i, carry):
            rows, qn, kn, vn = chunk(i)
            sc = _mm_nt(qn, kn) * t["dmat"]
            out = _mm(sc.astype(CDT), vn)
            sf = sf_ref[...]
            sfb = sf.astype(CDT)
            sfs_ref[i] = sfb
            R_ref[rows, :] = out + _mm((qn.astype(f32) * t["fq"]).astype(CDT), sfb)
            sf_ref[...] = sf * t["lamf"] + _mm_tn((kn.astype(f32) * t["fk"]).astype(CDT), vn)
            m = NC - 1 - i
            rows, qn, kn, vn = chunk(m)
            sb = sb_ref[...]
            sbb = sb.astype(CDT)
            sbs_ref[m] = sbb
            rb_ref[rows, :] = _mm((qn.astype(f32) * t["bq"]).astype(CDT), sbb)
            sb_ref[...] = sb * t["lamb"] + _mm_tn((kn.astype(f32) * t["bk"]).astype(CDT), vn)
            return carry

        lax.fori_loop(0, NC, step, 0, unroll=16)

        def finish(i, carry):
            rs = pl.ds(pl.multiple_of(i * ROW_TILE, ROW_TILE), ROW_TILE)
            R = R_ref[rs, :] + rb_ref[rs, :]
            R_ref[rs, :] = R
            rn = R * lax.rsqrt(jnp.mean(R * R, axis=-1, keepdims=True) + EPS)
            g = g_ref[rs, :].astype(f32)
            r_ref[rs, :] = (rn * g * jax.nn.sigmoid(g)).astype(CDT)
            return carry

        lax.fori_loop(0, S // ROW_TILE, finish, 0, unroll=2)

    def seg(slot):
        return pl.BlockSpec((None, S, DK), lambda h: (slot, 0, h))

    states = jax.ShapeDtypeStruct((H, NC, DK, DK), CDT)
    state_blk = pl.BlockSpec((None, NC, DK, DK), lambda h: (h, 0, 0, 0))
    return _call(
        body, hooks, [proj, proj, proj, proj, dl], name="ret_fwd", grid=(H,),
        out_shape=[jax.ShapeDtypeStruct((S, H * DK), f32), jax.ShapeDtypeStruct((S, H * DK), CDT), states, states],
        in_specs=[seg(SLOT_Q), seg(SLOT_K), seg(SLOT_VR), seg(SLOT_GR), pl.BlockSpec((None, 8, DK), lambda h: (h, 0, 0))],
        out_specs=[pl.BlockSpec((S, DK), lambda h: (0, h)), pl.BlockSpec((S, DK), lambda h: (0, h)), state_blk, state_blk],
        scratch_shapes=[pltpu.VMEM((S, DK), f32), pltpu.VMEM((DK, DK), f32), pltpu.VMEM((DK, DK), f32)],
        compiler_params=_params("arbitrary"),
    )


def _merge_fwd(a, r, proj, x1, wa, wb, wo, hooks=(), tm=512):
    S = x1.shape[0]

    def body(a_ref, r_ref, gt_ref, x_ref, wa_ref, wb_ref, wo_ref, x2_ref, ya_ref, yb_ref):
        ya = _mm(a_ref[...], wa_ref[...])
        yb = _mm(r_ref[...], wb_ref[...])
        ya_ref[...] = ya.astype(CDT)
        yb_ref[...] = yb.astype(CDT)
        mix = jax.nn.sigmoid(gt_ref[0].astype(f32)) * ya + jax.nn.sigmoid(gt_ref[1].astype(f32)) * yb
        x2_ref[...] = x_ref[...] + _mm(mix.astype(CDT), wo_ref[...])

    tok = pl.BlockSpec((tm, D), lambda i: (i, 0))
    return _call(
        body, hooks, [a, r, proj, x1, wa, wb, wo], name="merge_fwd", grid=(S // tm,),
        out_shape=[jax.ShapeDtypeStruct((S, D), f32), jax.ShapeDtypeStruct((S, D), CDT), jax.ShapeDtypeStruct((S, D), CDT)],
        in_specs=[tok, tok, pl.BlockSpec((2, tm, D), lambda i: (SLOT_GA // 2, i, 0)), tok,
                  _wblock(wa), _wblock(wb), _wblock(wo)],
        out_specs=[tok, tok, tok],
        compiler_params=_params("arbitrary"),
    )


def _loss_head(x3, fn, target, tm=512):
    S = x3.shape[0]

    def body(x_ref, n_ref, t_ref, dx_ref, dxh_ref, dn_ref, l_ref):
        n = n_ref[...]
        r, xh, y = _rms_fwd(x_ref[...], n)
        e = y - t_ref[...]
        dy = e * (1.0 / D)
        dx, dn = _rms_bwd(dy, r, xh, n)
        dx_ref[...] = dx
        dxh_ref[...] = (0.5 * dx).astype(CDT)
        part = 0.5 * jnp.sum(jnp.sum(e * e, axis=-1, keepdims=True), axis=0, keepdims=True) * (1.0 / D)

        @pl.when(pl.program_id(0) == 0)
        def _():
            dn_ref[...] = jnp.zeros_like(dn_ref)
            l_ref[...] = jnp.zeros_like(l_ref)

        dn_ref[...] += dn
        l_ref[...] += jnp.broadcast_to(part, l_ref.shape)

    tok = pl.BlockSpec((tm, D), lambda i: (i, 0))
    return pl.pallas_call(
        body, name="loss_head", grid=(S // tm,),
        out_shape=[jax.ShapeDtypeStruct((S, D), f32), jax.ShapeDtypeStruct((S, D), CDT), jax.ShapeDtypeStruct((1, D), f32),
                   jax.ShapeDtypeStruct((8, 128), f32)],
        in_specs=[tok, _resident((1, D), lambda i: (0, 0)), tok],
        out_specs=[tok, tok, pl.BlockSpec((1, D), lambda i: (0, 0)), pl.BlockSpec((8, 128), lambda i: (0, 0))],
        compiler_params=_params("arbitrary"),
    )(x3, fn, target)


def _ffn_bwd_hidden(name, dyh, g, u, wd, hooks=(), tm=512):
    S = dyh.shape[0]

    def body(dyh_ref, g_ref, u_ref, wd_ref, dg_ref, du_ref):
        dyh = dyh_ref[...]
        for ci in range(F // FT):
            sl = slice(ci * FT, (ci + 1) * FT)
            da = _mm_nt(dyh, wd_ref[sl, :])
            gv = g_ref[:, sl].astype(f32)
            uv = u_ref[:, sl].astype(f32)
            s = jax.nn.sigmoid(gv)
            silu = gv * s
            du_ref[:, sl] = (da * silu).astype(CDT)
            dg_ref[:, sl] = (da * uv * (s + silu - silu * s)).astype(CDT)

    hid = pl.BlockSpec((tm, F), lambda i: (i, 0))
    hidden = jax.ShapeDtypeStruct((S, F), CDT)
    return _call(
        body, hooks, [dyh, g, u, wd], name=name, grid=(S // tm,), out_shape=[hidden, hidden],
        in_specs=[pl.BlockSpec((tm, D), lambda i: (i, 0)), hid, hid, _wblock(wd)], out_specs=[hid, hid],
        compiler_params=_params("arbitrary"),
    )


def _ffn_bwd_in(name, dy, x, dg, du, nrm, wg, wu, hooks=(), tm=512):
    S = x.shape[0]

    def body(dy_ref, x_ref, dg_ref, du_ref, n_ref, wg_ref, wu_ref, dx_ref, dn_ref, acc_ref):
        n = n_ref[...]
        r, xh, _ = _rms_fwd(x_ref[...], n)
        for ci in range(F // FT):
            sl = slice(ci * FT, (ci + 1) * FT)
            dh = _mm(dg_ref[:, sl], wg_ref[sl, :]) + _mm(du_ref[:, sl], wu_ref[sl, :])
            if ci == 0:
                acc_ref[...] = dh
            else:
                acc_ref[...] += dh
        dx, dn = _rms_bwd(acc_ref[...], r, xh, n)
        dx_ref[...] = dy_ref[...] + dx

        @pl.when(pl.program_id(0) == 0)
        def _():
            dn_ref[...] = jnp.zeros_like(dn_ref)

        dn_ref[...] += dn

    tok = pl.BlockSpec((tm, D), lambda i: (i, 0))
    hid = pl.BlockSpec((tm, F), lambda i: (i, 0))
    return _call(
        body, hooks, [dy, x, dg, du, nrm, wg, wu], name=name, grid=(S // tm,),
        out_shape=[jax.ShapeDtypeStruct((S, D), f32), jax.ShapeDtypeStruct((1, D), f32)],
        in_specs=[tok, tok, hid, hid, _resident((1, D), lambda i: (0, 0)), _wblock(wg), _wblock(wu)],
        out_specs=[tok, pl.BlockSpec((1, D), lambda i: (0, 0))],
        scratch_shapes=[pltpu.VMEM((tm, D), f32)],
        compiler_params=_params("arbitrary"),
    )


TN_ROWS = 512


def _tn(name, xs, ys, block_of, hooks=()):
    S, M = xs.shape
    B = ys.shape[0]
    tr = TN_ROWS if M % TN_ROWS == 0 else M // 2
    assert M % tr == 0 and tr % 128 == 0
    nt = M // tr

    def body(x_ref, y_ref, o_ref):
        o_ref[...] = _mm_tn(x_ref[...], y_ref[...]).astype(CDT)

    return _call(
        body, hooks, [xs, ys], name=name, grid=(B, nt),
        out_shape=[jax.ShapeDtypeStruct((B * M, D), CDT)],
        in_specs=[pl.BlockSpec((S, tr), lambda b, i: (0, i)), pl.BlockSpec((None, S, D), lambda b, i: (b, 0, 0))],
        out_specs=[pl.BlockSpec((tr, D), lambda b, i: (block_of(b) * nt + i, 0))],
        compiler_params=_params("arbitrary", "arbitrary"),
    )


def _wgrad(name, xs, y, hooks=()):
    return _tn(name, xs, y[None], lambda b: 0, hooks)


def _merge_bwd_act(dx2, ya, yb, proj, wa, wb, wo, hooks=(), tm=512):
    S = dx2.shape[0]

    def body(dx_ref, ya_ref, yb_ref, gt_ref, wa_ref, wb_ref, wo_ref,
             dp_ref, da_ref, dr_ref, mix_ref, dxb_ref, dya_ref, dyb_ref):
        dxb = dx_ref[...].astype(CDT)
        dxb_ref[...] = dxb
        dmix = _mm_nt(dxb, wo_ref[...])
        ya = ya_ref[...].astype(f32)
        yb = yb_ref[...].astype(f32)
        sa = jax.nn.sigmoid(gt_ref[0].astype(f32))
        sb = jax.nn.sigmoid(gt_ref[1].astype(f32))
        mix_ref[...] = (sa * ya + sb * yb).astype(CDT)
        dya = (dmix * sa).astype(CDT)
        dyb = (dmix * sb).astype(CDT)
        dya_ref[...] = dya
        dyb_ref[...] = dyb
        dp_ref[0] = (dmix * ya * sa * (1.0 - sa)).astype(CDT)
        dp_ref[1] = (dmix * yb * sb * (1.0 - sb)).astype(CDT)
        da_ref[...] = _mm_nt(dya, wa_ref[...]).astype(CDT)
        dr_ref[...] = _mm_nt(dyb, wb_ref[...]).astype(CDT)

    tok = pl.BlockSpec((tm, D), lambda i: (i, 0))
    gates = pl.BlockSpec((2, tm, D), lambda i: (SLOT_GA // 2, i, 0))
    act = jax.ShapeDtypeStruct((S, D), CDT)
    return _call(
        body, hooks, [dx2, ya, yb, proj, wa, wb, wo], name="merge_bwd_act", grid=(S // tm,),
        out_shape=[jax.ShapeDtypeStruct((8, S, D), CDT), act, act, act, act, act, act],
        in_specs=[tok, tok, tok, gates, _wblock(wa), _wblock(wb), _wblock(wo)],
        out_specs=[gates, tok, tok, tok, tok, tok, tok],
        compiler_params=_params("arbitrary"),
    )


def _sgu_bwd(da, proj, dproj, gn, bn, ws, wst, bsc, hooks=(), tm=512):
    S = proj.shape[1]
    GW = D // G

    def body(da_ref, p_ref, dpin_ref, gn_ref, bn_ref, ws_ref, wst_ref, bs_ref,
             dp_ref, dws_ref, dbs_ref, dgn_ref, dbn_ref, ds_ref, dvn_ref):
        @pl.when(pl.program_id(0) == 0)
        def _():
            dws_ref[...] = jnp.zeros_like(dws_ref)
            dbs_ref[...] = jnp.zeros_like(dbs_ref)
            dgn_ref[...] = jnp.zeros_like(dgn_ref)
            dbn_ref[...] = jnp.zeros_like(dbn_ref)

        pu = p_ref[0].astype(f32)
        pv = p_ref[1].astype(f32)
        ua, dua = _gelu_and_grad(pu)
        va, dva_dpv = _gelu_and_grad(pv)
        gn = gn_ref[...]
        rstd, vhat, vn = _sgu_norm(va, gn, bn_ref[...])
        vnb = vn.astype(CDT)
        dav = da_ref[...].astype(f32)
        dsb = (dav * ua).astype(CDT)
        ones = jnp.ones((8, GW), CDT)
        for ch in range(tm // C):
            rs = slice(ch * C, (ch + 1) * C)
            for gi in range(G):
                cs = slice(gi * GW, (gi + 1) * GW)
                s = _mm(ws_ref[gi], vnb[rs, cs]) + bs_ref[gi]
                ds_ref[rs, cs] = s
                dsg = dsb[rs, cs]
                dws_ref[gi] += _mm_nt(dsg, vnb[rs, cs])
                dbs_ref[gi] += _mm_nt(ones, dsg)
                dvn_ref[rs, cs] = _mm(wst_ref[gi], dsg)
        dp_ref[0] = (dav * ds_ref[...] * dua).astype(CDT)
        dvn = dvn_ref[...]
        dgn_ref[...] += jnp.sum(dvn * vhat, axis=0, keepdims=True)
        dbn_ref[...] += jnp.sum(dvn, axis=0, keepdims=True)
        dvh = dvn * gn
        dva = rstd * (dvh - jnp.mean(dvh, axis=-1, keepdims=True) - vhat * jnp.mean(dvh * vhat, axis=-1, keepdims=True))
        dp_ref[1] = (dva * dva_dpv).astype(CDT)

    uv = pl.BlockSpec((2, tm, D), lambda i: (0, i, 0))
    row = _resident((1, D), lambda i: (0, 0))
    return _call(
        body, hooks, [da, proj, dproj, gn, bn, ws, wst, bsc], name="sgu_bwd", grid=(S // tm,),
        out_shape=[jax.ShapeDtypeStruct(dproj.shape, CDT), jax.ShapeDtypeStruct((G, C, C), f32),
                   jax.ShapeDtypeStruct((G, 8, C), f32), jax.ShapeDtypeStruct((1, D), f32), jax.ShapeDtypeStruct((1, D), f32)],
        in_specs=[pl.BlockSpec((tm, D), lambda i: (i, 0)), uv, _HBM, row, row,
                  _resident((G, C, C), lambda i: (0, 0, 0)), _resident((G, C, C), lambda i: (0, 0, 0)),
                  _resident((G, C, 1), lambda i: (0, 0, 0))],
        out_specs=[uv, pl.BlockSpec((G, C, C), lambda i: (0, 0, 0)), pl.BlockSpec((G, 8, C), lambda i: (0, 0, 0)),
                   pl.BlockSpec((1, D), lambda i: (0, 0)), pl.BlockSpec((1, D), lambda i: (0, 0))],
        scratch_shapes=[pltpu.VMEM((tm, D), f32), pltpu.VMEM((tm, D), f32)],
        input_output_aliases={2: 0},
        compiler_params=_params("arbitrary"),
    )


def _ret_bwd(dr, R, sfs, sbs, proj, dproj, cos, sin, dl, hooks=()):
    S = proj.shape[1]
    NC = S // RC
    assert NC % 2 == 0

    def body(dr_ref, R_ref, sf_ref, sb_ref, q_ref, k_ref, v_ref, g_ref, dpin_ref, cos_ref, sin_ref, dl_ref,
             dp_ref, dd_ref, dR_ref, gb_ref, gf_ref, acc_ref):
        t = _decay_tables(dl_ref)

        def gate_norm_bwd(i, carry):
            rs = pl.ds(pl.multiple_of(i * ROW_TILE, ROW_TILE), ROW_TILE)
            Rv = R_ref[rs, :]
            rstd = lax.rsqrt(jnp.mean(Rv * Rv, axis=-1, keepdims=True) + EPS)
            rn = Rv * rstd
            gv = g_ref[rs, :].astype(f32)
            s = jax.nn.sigmoid(gv)
            drv = dr_ref[rs, :].astype(f32)
            dp_ref[3, rs, :] = (drv * rn * (s * (1.0 + gv * (1.0 - s)))).astype(CDT)
            drn = drv * gv * s
            dR_ref[rs, :] = (rstd * (drn - rn * jnp.mean(drn * rn, axis=-1, keepdims=True))).astype(CDT)
            return carry

        lax.fori_loop(0, S // ROW_TILE, gate_norm_bwd, 0, unroll=2)

        def chunk(n):
            rows = pl.ds(pl.multiple_of(n * RC, RC), RC)
            return rows, q_ref[rows, :], k_ref[rows, :], v_ref[rows, :], dR_ref[rows, :]

        def emit_kv(rows, dk, dv, final):
            if not final:
                dp_ref[1, rows, :] = dk.astype(CDT)
                dp_ref[2, rows, :] = dv.astype(CDT)
            else:
                co, si = cos_ref[rows, :], sin_ref[rows, :]
                dk = dp_ref[1, rows, :].astype(f32) + dk
                dp_ref[1, rows, :] = (_unrotate(dk, co, si) * K_SCALE).astype(CDT)
                dp_ref[2, rows, :] = (dp_ref[2, rows, :].astype(f32) + dv).astype(CDT)

        gb_ref[...] = jnp.zeros_like(gb_ref)
        gf_ref[...] = jnp.zeros_like(gf_ref)
        acc_ref[...] = jnp.zeros_like(acc_ref)
        span = t["dmat"] * jnp.abs(t["d"])
        span_f, span_b = jnp.where(t["lower"], span, 0.0), jnp.where(t["lower"], 0.0, span)

        def ascend(n, final):
            rows, qn, kn, vn, dRn = chunk(n)
            qf, kf = qn.astype(f32), kn.astype(f32)
            sc = _mm_nt(qn, kn)
            dA = _mm_nt(dRn, vn)
            prod = sc * dA
            lgf_part = jnp.sum(prod * span_f, axis=0, keepdims=True)
            lgb_part = jnp.sum(prod * span_b, axis=0, keepdims=True)
            dsc = (dA * t["dmat"]).astype(CDT)
            dq = _mm(dsc, kn)
            scT = (_mm_nt(kn, qn) * t["dmat_t"]).astype(CDT)
            dscT = (_mm_nt(vn, dRn) * t["dmat_t"]).astype(CDT)
            dk = _mm(dscT, qn)
            dv = _mm(scT, dRn)
            sfb = sf_ref[n]
            sbb = sb_ref[n]
            qdf = qf * t["fq"]
            dqdf = _mm_nt(dRn, sfb)
            dq += dqdf * t["fq"]
            lgf_row = jnp.sum(qdf * dqdf * (t["pos"] + 1.0), axis=0, keepdims=True)
            qdb = qf * t["bq"]
            dqdb = _mm_nt(dRn, sbb)
            dq += dqdb * t["bq"]
            lgb_row = jnp.sum(qdb * dqdb * (RC - t["pos"]), axis=0, keepdims=True)
            gb = gb_ref[...]
            gbb = gb.astype(CDT)
            kdb = kf * t["bk"]
            dkdb = _mm_nt(vn, gbb)
            dk += dkdb * t["bk"]
            dv += _mm(kdb.astype(CDT), gbb)
            lgb_row += jnp.sum(kdb * dkdb * t["pos"], axis=0, keepdims=True)
            lgb_row += float(RC) * t["lamb"] * jnp.sum(gb * sbb.astype(f32), axis=0, keepdims=True)
            co, si = cos_ref[rows, :], sin_ref[rows, :]
            dp_ref[0, rows, :] = _unrotate(dq, co, si).astype(CDT)
            emit_kv(rows, dk, dv, final)
            acc_ref[0:1, :] += lgf_row + lgf_part
            acc_ref[1:2, :] += lgb_row + lgb_part
            gb_ref[...] = gb * t["lamb"] + _mm_tn(qdb.astype(CDT), dRn)

        def descend(n, final):
            rows, qn, kn, vn, dRn = chunk(n)
            gf = gf_ref[...]
            gfb = gf.astype(CDT)
            kdf = kn.astype(f32) * t["fk"]
            dkdf = _mm_nt(vn, gfb)
            lgf_row = jnp.sum(kdf * dkdf * (RC - 1.0 - t["pos"]), axis=0, keepdims=True)
            lgf_row += float(RC) * t["lamf"] * jnp.sum(gf * sf_ref[n].astype(f32), axis=0, keepdims=True)
            acc_ref[0:1, :] += lgf_row
            emit_kv(rows, dkdf * t["fk"], _mm(kdf.astype(CDT), gfb), final)
            gf_ref[...] = gf * t["lamf"] + _mm_tn((qn.astype(f32) * t["fq"]).astype(CDT), dRn)

        def sweep(final):
            def step(i, carry):
                ascend(i, final)
                descend(NC - 1 - i, final)
                return carry
            return step

        lax.fori_loop(0, NC // 2, sweep(False), 0, unroll=8)
        lax.fori_loop(NC // 2, NC, sweep(True), 0, unroll=8)
        dlg = jnp.sum(acc_ref[...], axis=1, keepdims=True)
        dlogit = dlg * jax.nn.sigmoid(-dl_ref[:, 0:1])
        lane = lax.broadcasted_iota(jnp.int32, (8, 128), 1)
        dd_ref[...] = jnp.where(lane == pl.program_id(0), jnp.broadcast_to(dlogit, (8, 128)), 0.0)

    def seg(slot):
        return pl.BlockSpec((None, S, DK), lambda h: (slot, 0, h))

    head = pl.BlockSpec((S, DK), lambda h: (0, h))
    states = pl.BlockSpec((None, NC, DK, DK), lambda h: (h, 0, 0, 0))
    return _call(
        body, hooks, [dr, R, sfs, sbs, proj, proj, proj, proj, dproj, cos, sin, dl], name="ret_bwd", grid=(H,),
        out_shape=[jax.ShapeDtypeStruct(dproj.shape, CDT), jax.ShapeDtypeStruct((H, 8, 128), f32)],
        in_specs=[head, head, states, states, seg(SLOT_Q), seg(SLOT_K), seg(SLOT_VR), seg(SLOT_GR), _HBM,
                  _resident((S, DK // 2), lambda h: (0, 0)), _resident((S, DK // 2), lambda h: (0, 0)),
                  pl.BlockSpec((None, 8, DK), lambda h: (h, 0, 0))],
        out_specs=[pl.BlockSpec((4, S, DK), lambda h: (1, 0, h), pipeline_mode=pl.Buffered(1)),
                   pl.BlockSpec((None, 8, 128), lambda h: (h, 0, 0))],
        scratch_shapes=[pltpu.VMEM((S, DK), CDT),
                        pltpu.VMEM((DK, DK), f32), pltpu.VMEM((DK, DK), f32), pltpu.VMEM((8, DK), f32)],
        input_output_aliases={8: 0},
        compiler_params=_params("arbitrary"),
    )


def _proj_bwd_act(dproj, dx2, x1, nrm, wfull, hooks=(), tm=512):
    S = x1.shape[0]

    def body(dp_ref, dx2_ref, x_ref, n_ref, w_ref, dx_ref, dxh_ref, dn_ref, db_ref, acc_ref):
        @pl.when(pl.program_id(0) == 0)
        def _():
            dn_ref[...] = jnp.zeros_like(dn_ref)
            db_ref[...] = jnp.zeros_like(db_ref)

        for p in range(8):
            seg = SEG_OF_SLOT[p]
            dp = dp_ref[p]
            db_ref[seg] += jnp.sum(dp.astype(f32), axis=0, keepdims=True)
            dh = _mm_nt(dp, w_ref[seg * D:(seg + 1) * D, :])
            if p == 0:
                acc_ref[...] = dh
            else:
                acc_ref[...] += dh
        n = n_ref[...]
        r, xh, _ = _rms_fwd(x_ref[...], n)
        dx, dn = _rms_bwd(acc_ref[...], r, xh, n)
        dx = dx2_ref[...] + dx
        dx_ref[...] = dx
        dxh_ref[...] = (0.5 * dx).astype(CDT)
        dn_ref[...] += dn

    tok = pl.BlockSpec((tm, D), lambda i: (i, 0))
    return _call(
        body, hooks, [dproj, dx2, x1, nrm, wfull], name="proj_bwd_act", grid=(S // tm,),
        out_shape=[jax.ShapeDtypeStruct((S, D), f32), jax.ShapeDtypeStruct((S, D), CDT), jax.ShapeDtypeStruct((1, D), f32),
                   jax.ShapeDtypeStruct((8, 1, D), f32)],
        in_specs=[pl.BlockSpec((8, tm, D), lambda i: (0, i, 0)), tok, tok, _resident((1, D), lambda i: (0, 0)),
                  _resident((8 * D, D), lambda i: (0, 0))],
        out_specs=[tok, tok, pl.BlockSpec((1, D), lambda i: (0, 0)), pl.BlockSpec((8, 1, D), lambda i: (0, 0, 0))],
        scratch_shapes=[pltpu.VMEM((tm, D), f32)],
        compiler_params=_params("arbitrary"),
    )


def _rs_sum(name, gfulls, lands, my_c):
    n = len(gfulls)
    rows = gfulls[0].shape[0] // NDEV
    assert all(g.shape[0] == NDEV * rows for g in gfulls)

    def body(c_ref, *refs):
        for g_ref, l_ref, o_ref in zip(refs[:n], refs[n:2 * n], refs[2 * n:]):
            o_ref[...] = (g_ref[...].astype(f32) + l_ref[...].astype(f32)).astype(CDT)

    slot = pl.BlockSpec((None, rows, D), lambda k, c: (k, 0, 0))
    return pl.pallas_call(
        body, name=name,
        grid_spec=pltpu.PrefetchScalarGridSpec(
            num_scalar_prefetch=1, grid=(NCHIP,),
            in_specs=[pl.BlockSpec((rows, D), lambda k, c: (2 * k + c[0], 0))] * n + [slot] * n,
            out_specs=[slot] * n),
        out_shape=[jax.ShapeDtypeStruct((NCHIP, rows, D), CDT)] * n,
        compiler_params=_params("arbitrary"),
    )(my_c, *gfulls, *lands)


def _adamw_math(g, w, m, v):
    m2 = ADAM_B1 * m + (1.0 - ADAM_B1) * g
    v2 = ADAM_B2 * v + (1.0 - ADAM_B2) * (g * g)
    delta = -ADAM_LR * ((m2 / BC1) / (jnp.sqrt(v2 / BC2) + ADAM_EPS) + ADAM_WD * w)
    return delta, m2, v2


def _adamw_big(name, landed, w, m, v, after):
    n = len(w)
    rows = w[0].shape[0]
    tr = 256 if rows % 256 == 0 else (rows // 2 if rows > 256 else rows)
    nt = rows // tr

    def body(*refs):
        ins, outs = refs[:4 * n], refs[4 * n + 1:]
        for j in range(n):
            @pl.when(pl.program_id(0) == j)
            def _(j=j):
                l_ref, w_ref, m_ref, v_ref = ins[j], ins[n + j], ins[2 * n + j], ins[3 * n + j]
                g = l_ref[0].astype(f32)
                for k in range(1, NCHIP):
                    g = g + l_ref[k].astype(f32)
                outs[4 * j][...] = g
                outs[4 * j + 1][...], outs[4 * j + 2][...], outs[4 * j + 3][...] = _adamw_math(g, w_ref[...], m_ref[...], v_ref[...])

    def tile(j):
        return lambda jj, i: jnp.clip((jj - j) * nt + i, 0, nt - 1)

    blk = [pl.BlockSpec((tr, D), lambda jj, i, t=tile(j): (t(jj, i), 0)) for j in range(n)]
    lnd = [pl.BlockSpec((NCHIP, tr, D), lambda jj, i, t=tile(j): (0, t(jj, i), 0)) for j in range(n)]
    o = jax.ShapeDtypeStruct((rows, D), f32)
    res = pl.pallas_call(
        body, name=name, grid=(n, nt), out_shape=[o] * (4 * n),
        in_specs=lnd + blk + blk + blk + [_HBM],
        out_specs=[blk[j] for j in range(n) for _ in range(4)],
        compiler_params=_params("arbitrary", "arbitrary"),
    )(*landed, *w, *m, *v, after)
    return [res[4 * j:4 * j + 4] for j in range(n)]


ROW_FFN1_NORM, ROW_MIX_NORM, ROW_SGU_G, ROW_SGU_B, ROW_FFN2_NORM, ROW_FINAL_NORM, ROW_B_IN = 0, 1, 2, 3, 4, 5, 8
ROW_WS, ROW_BS = 0, G * C


def _adamw_small(ga, gs, gd, gn1, gl, params):
    def body(ga_ref, gs_ref, gd_ref, gn1_ref, gl_ref, *refs):
        ins, outs = refs[:30], refs[30:]

        def total(ref, r0, n):
            g = ref[0, r0:r0 + n, :]
            for j in range(1, NDEV):
                g = g + ref[j, r0:r0 + n, :]
            return g

        def apply(i, g, rows=slice(None), cols=slice(None)):
            w, m, v = ins[3 * i][rows, cols], ins[3 * i + 1][rows, cols], ins[3 * i + 2][rows, cols]
            outs[4 * i][rows, cols] = g
            outs[4 * i + 1][rows, cols], outs[4 * i + 2][rows, cols], outs[4 * i + 3][rows, cols] = _adamw_math(g, w, m, v)

        outs[40][...] = total(gl_ref, 0, 8)[0:1, 0:1]
        apply(0, total(gn1_ref, 0, 1))
        for i, r in enumerate((ROW_FFN1_NORM, ROW_MIX_NORM, ROW_SGU_G, ROW_SGU_B, ROW_FFN2_NORM, ROW_FINAL_NORM)):
            if i:
                apply(i, total(ga_ref, r, 1))
        for k in range(8):
            apply(6, total(ga_ref, ROW_B_IN + k, 1), cols=slice(k * D, (k + 1) * D))
        apply(7, total(gs_ref, ROW_WS, G * C))
        for gi in range(G):
            apply(8, total(gs_ref, ROW_BS + 8 * gi, 1), slice(gi, gi + 1))
        dec = total(gd_ref, 0, 8)
        for hh in range(1, H):
            dec = dec + total(gd_ref, 8 * hh, 8)
        apply(9, dec[0:2, 0:H])

    flat = [a for p in params for a in p]
    out_shape = [jax.ShapeDtypeStruct(p[0].shape, f32) for p in params for _ in range(4)]
    out_shape.append(jax.ShapeDtypeStruct((1, 1), f32))
    vm = pl.BlockSpec(memory_space=pltpu.VMEM)
    return pl.pallas_call(
        body, name="adamw_small", out_shape=out_shape,
        in_specs=[vm] * (5 + len(flat)), out_specs=[vm] * len(out_shape),
        compiler_params=pltpu.CompilerParams(vmem_limit_bytes=VMEM_LIMIT),
    )(ga, gs, gd, gn1, gl, *flat)


def kernel(x, ffn1_norm, ffn1_w_gate, ffn1_w_up, ffn1_w_down, mix_norm, w_in, b_in, sgu_norm_g, sgu_norm_b, sgu_w_s, sgu_b_s, ret_decay_logit, w_branch_a, w_branch_b, w_out, ffn2_norm, ffn2_w_gate, ffn2_w_up, ffn2_w_down, final_norm, loss_target, m_ffn1_norm, m_ffn1_w_gate, m_ffn1_w_up, m_ffn1_w_down, m_mix_norm, m_w_in, m_b_in, m_sgu_norm_g, m_sgu_norm_b, m_sgu_w_s, m_sgu_b_s, m_ret_decay_logit, m_w_branch_a, m_w_branch_b, m_w_out, m_ffn2_norm, m_ffn2_w_gate, m_ffn2_w_up, m_ffn2_w_down, m_final_norm, v_ffn1_norm, v_ffn1_w_gate, v_ffn1_w_up, v_ffn1_w_down, v_mix_norm, v_w_in, v_b_in, v_sgu_norm_g, v_sgu_norm_b, v_sgu_w_s, v_sgu_b_s, v_ret_decay_logit, v_w_branch_a, v_w_branch_b, v_w_out, v_ffn2_norm, v_ffn2_w_gate, v_ffn2_w_up, v_ffn2_w_down, v_final_norm):
    args = dict(locals())
    S = x.shape[1]
    xs = x[0]
    target = loss_target[0]

    def buf_layout(name, a):
        a = a[0]
        return a.T if name in W_TRANSPOSED else a

    sh = {n: buf_layout(n, args[n]).astype(CDT) for n in W_NAMES}
    wf = {}

    b3 = b_in.reshape(8, 1, D)
    ws = sgu_w_s[0].astype(CDT)
    wst = jnp.swapaxes(sgu_w_s[0], 1, 2).astype(CDT)
    bsc = sgu_b_s[0].reshape(G, C, 1)
    dl = jnp.zeros((H, 8, DK), f32).at[:, 0:2, :].set(jnp.broadcast_to(ret_decay_logit[0].T[:, :, None], (H, 2, DK)))
    theta = ROPE_BASE ** (-jnp.arange(0, DK, 2, dtype=f32) / DK)
    ang = jnp.arange(S, dtype=f32)[:, None] * theta[None, :]
    cos, sin = jnp.cos(ang), jnp.sin(ang)
    fnorm = final_norm.reshape(1, D)

    f1 = ("ffn1_w_gate", "ffn1_w_up", "ffn1_w_down")
    f2 = ("ffn2_w_gate", "ffn2_w_up", "ffn2_w_down")
    br = ("w_branch_a", "w_branch_b", "w_out")
    for cid, names in enumerate((f1, ("w_in",), br, f2)):
        wf.update(zip(names, _sequence("ag_" + names[0], 1 + cid, NEIGHBOURS, [_ag_hook(sh[n]) for n in names])))
    x1, g1, u1, a1, hf1 = _ffn_fwd("ffn1_fwd", xs, ffn1_norm, *[wf[n] for n in f1])
    proj, h2 = _proj_fwd(x1, mix_norm, wf["w_in"], b3, cos, sin)
    a = _sgu_fwd(proj, sgu_norm_g, sgu_norm_b, ws, bsc)
    R, r, sfs, sbs = _ret_fwd(proj, dl)
    x2, ya, yb = _merge_fwd(a, r, proj, x1, *[wf[n] for n in br])
    x3, g2, u2, a2, hf2 = _ffn_fwd("ffn2_fwd", x2, ffn2_norm, *[wf[n] for n in f2])
    dx3, dyh2, d_final, loss_part = _loss_head(x3, fnorm, target)

    my_c = lax.axis_index("c").astype(jnp.int32).reshape(1)
    gw, landed, sequenced = {}, {}, []

    def d2d(*names):
        return [_rs_d2d_hook(gw[n]) for n in names]

    def behind(x, token):
        return lax.optimization_barrier((x, token))[0]

    def to_chips(names, sibs, more=()):
        parts = list(_rs_sum("rs_sum_" + names[0], [gw[n] for n in names], list(sibs), my_c))
        token = parts[0]
        if sequenced:
            parts[0] = behind(parts[0], sequenced[-1])
        hooks = [_rs_ici_hook(p) for p in parts] + ([_small_hook(list(more))] if more else [])
        reach = EVERYONE if more else CHIPS
        got = _sequence("rs_chips_" + names[0], REACH_ID[reach], reach, hooks)
        sequenced.append(got[0])
        landed.update(zip(names, got))
        return got[len(names):], token

    def ffn_bwd(tag, names, dy, dyh, x, g, u, a, h, nrm, each_alone, more=()):
        wg, wu, wd = names
        (gw[wd],) = _wgrad(tag + "_wd_grad", a, dyh)
        dg, du, sib_d = _ffn_bwd_hidden(tag + "_bwd_hidden", dyh, g, u, wf[wd], d2d(wd))
        if each_alone:
            dg = behind(dg, to_chips([wd], [sib_d])[1])
        (gw[wg],) = _wgrad(tag + "_wg_grad", dg, h)
        gw[wu], sib_g = _wgrad(tag + "_wu_grad", du, h, d2d(wg))
        if each_alone:
            (sib_u,) = _sequence("rs_sib_" + wu, SIBLING_ID, SIBLING, [_rs_d2d_hook(behind(gw[wu], sequenced[-1]))])
            sequenced.append(sib_u)
            dy = behind(dy, to_chips([wg], [sib_g])[1])
            dx, dn = _ffn_bwd_in(tag + "_bwd_in", dy, x, dg, du, nrm, wf[wg], wf[wu])
            return (dx, dn) + to_chips([wu], [sib_u], more(dn))
        dx, dn, sib_u = _ffn_bwd_in(tag + "_bwd_in", dy, x, dg, du, nrm, wf[wg], wf[wu], d2d(wu))
        return (dx, dn) + to_chips([wd, wg, wu], [sib_d, sib_g, sib_u])

    dx2, d_ffn2n, _, token = ffn_bwd("ffn2", f2, dx3, dyh2, x2, g2, u2, a2, hf2, ffn2_norm, False)
    dproj, da, dr, mix, dx2b, dya, dyb = _merge_bwd_act(behind(dx2, token), ya, yb, proj, *[wf[n] for n in br])
    dproj, d_ws, d_bs, d_gn, d_bn = _sgu_bwd(da, proj, dproj, sgu_norm_g, sgu_norm_b, ws, wst, bsc)
    dproj, d_dec = _ret_bwd(dr, R, sfs, sbs, proj, dproj, cos, sin, dl)
    (gw["w_in"],) = _tn("win_grad", h2, dproj, _seg_of_slot)
    gw["w_out"], sib_win = _wgrad("wo_grad", mix, dx2b, d2d("w_in"))
    small_sgu = jnp.concatenate([d_ws.reshape(G * C, C), d_bs.reshape(G * 8, C)], axis=0)
    (g_sgu, gl), token = to_chips(["w_in"], [sib_win], [small_sgu, loss_part])
    (gw["w_branch_a"],) = _wgrad("wa_grad", a, behind(dya, token))
    (gw["w_branch_b"],) = _wgrad("wb_grad", r, dyb)
    dx1, dyh1, d_mixn, d_bin, *sib_br = _proj_bwd_act(behind(dproj, token), dx2, x1, mix_norm, wf["w_in"], d2d(*br))
    small_a = jnp.concatenate([jnp.zeros((1, D), f32), d_mixn, d_gn, d_bn, d_ffn2n, d_final, jnp.zeros((2, D), f32),
                               d_bin.reshape(8, D)], axis=0)
    (ga, g_dec), token = to_chips(list(br), sib_br, [small_a, d_dec.reshape(H * 8, 128)])
    dxs, d_ffn1n, (gn1,), token = ffn_bwd("ffn1", f1, dx1, behind(dyh1, token), xs, g1, u1, a1, hf1, ffn1_norm, True,
                                          lambda dn: [dn])

    out = {"grad_x": dxs[None]}

    def native(name, a):
        a = a.T if name in W_TRANSPOSED else a
        return a[None]

    after = token
    for names in (f2, ("w_in",), br, (f1[2], f1[0], f1[1])):
        res = _adamw_big("adamw_" + names[0], [landed[n] for n in names], [buf_layout(n, args[n]) for n in names],
                         [buf_layout(n, args["m_" + n]) for n in names], [buf_layout(n, args["v_" + n]) for n in names], after)
        after = res[-1][0]
        for n, four in zip(names, res):
            for pre, val in zip(("grad_", "delta_", "new_m_", "new_v_"), four):
                out[pre + n] = native(n, val)

    small = [
        ("ffn1_norm", lambda a: a, lambda a: a), ("mix_norm", lambda a: a, lambda a: a),
        ("sgu_norm_g", lambda a: a, lambda a: a), ("sgu_norm_b", lambda a: a, lambda a: a),
        ("ffn2_norm", lambda a: a, lambda a: a),
        ("final_norm", lambda a: a.reshape(1, D), lambda a: a.reshape(D)),
        ("b_in", lambda a: a, lambda a: a),
        ("sgu_w_s", lambda a: a.reshape(G * C, C), lambda a: a.reshape(1, G, C, C)),
        ("sgu_b_s", lambda a: a[0], lambda a: a[None]),
        ("ret_decay_logit", lambda a: a[0], lambda a: a[None]),
    ]
    res = _adamw_small(ga, g_sgu, g_dec, gn1, gl,
                       [(to(args[n]), to(args["m_" + n]), to(args["v_" + n])) for n, to, _ in small])
    out["loss"] = res[40].reshape(())
    for i, (n, _, back) in enumerate(small):
        for j, pre in enumerate(("grad_", "delta_", "new_m_", "new_v_")):
            out[pre + n] = back(res[4 * i + j])

    weights = ("ffn1_norm", "ffn1_w_gate", "ffn1_w_up", "ffn1_w_down", "mix_norm", "w_in", "b_in", "sgu_norm_g",
               "sgu_norm_b", "sgu_w_s", "sgu_b_s", "ret_decay_logit", "w_branch_a", "w_branch_b", "w_out", "ffn2_norm",
               "ffn2_w_gate", "ffn2_w_up", "ffn2_w_down", "final_norm")
    return (out["loss"], out["grad_x"], *[out["grad_" + n] for n in weights], *[out["delta_" + n] for n in weights],
            *[out["new_m_" + n] for n in weights], *[out["new_v_" + n] for n in weights])
```
